```python
import math
import jax, jax.numpy as jnp
from jax import lax
import numpy as np

D_MODEL = 1024
BATCH = 8
SEQ = 4096
DEPTH = 4

HEAD_DIM = 64
GM_WIDTH = D_MODEL // 4
GM_HEADS = GM_WIDTH // HEAD_DIM
GM_CHUNK = 128
S5_WIDTH = D_MODEL // 4
S5_GROUP_DIM = 16
S5_GROUPS = S5_WIDTH // S5_GROUP_DIM
S5_STATE = 64
S5_DT_MIN = 1e-3
S5_DT_MAX = 1e-1
FOX_WIDTH = D_MODEL // 2
FOX_HEADS = FOX_WIDTH // HEAD_DIM
Q_BLOCK = 128
D_MIX = GM_WIDTH + S5_WIDTH + FOX_WIDTH
D_IN_PROJ = 2 * GM_WIDTH + S5_WIDTH + 3 * FOX_WIDTH + FOX_HEADS
IN_SPLITS = (GM_WIDTH, 2 * GM_WIDTH, 2 * GM_WIDTH + S5_WIDTH,
             2 * GM_WIDTH + S5_WIDTH + FOX_WIDTH,
             2 * GM_WIDTH + S5_WIDTH + 2 * FOX_WIDTH,
             2 * GM_WIDTH + S5_WIDTH + 3 * FOX_WIDTH)
D_FF = ((8 * D_MODEL // 3 + 127) // 128) * 128
CONV_WIDTH = 3
LN_EPS = 1e-5
DN_ALPHA = (2.0 * DEPTH) ** 0.25
DN_BETA = (8.0 * DEPTH) ** -0.25
NEG_INF = -1e30

kernel_name = "hymba_style_gmlp_s5_fox_deepnorm_trunk"


def layer_norm(x, g, b):
    xf = x.astype(jnp.float32)
    mu = jnp.mean(xf, axis=-1, keepdims=True)
    var = jnp.mean(jnp.square(xf - mu), axis=-1, keepdims=True)
    y = (xf - mu) * lax.rsqrt(var + LN_EPS)
    return (y * g.astype(jnp.float32) + b.astype(jnp.float32)).astype(x.dtype)


def gmlp_mixer(u, v, ln_g, ln_b, w_s, b_s):
    B, S, _ = u.shape
    n = S // GM_CHUNK
    v = layer_norm(v.reshape(B, S, GM_HEADS, HEAD_DIM), ln_g, ln_b)
    v = v.reshape(B, n, GM_CHUNK, GM_HEADS, HEAD_DIM)
    causal = jnp.tril(jnp.ones((GM_CHUNK, GM_CHUNK), dtype=bool))
    w = jnp.where(causal[None], w_s, jnp.zeros_like(w_s))
    z = jnp.einsum('hts,bnshd->bnthd', w, v) + b_s.T[None, None, :, :, None]
    out = u.reshape(B, n, GM_CHUNK, GM_HEADS, HEAD_DIM) * z
    return out.reshape(B, S, GM_WIDTH)


def _ssm_combine(left, right):
    a_l, b_l = left
    a_r, b_r = right
    return a_r * a_l, a_r * b_l + b_r


def s5_mixer(u, lam_re, lam_im, log_dt, b_re, b_im, c_re, c_im, d_skip, w_glu, b_glu):
    B, S, _ = u.shape
    f32 = jnp.float32
    uf = u.astype(f32).reshape(B, S, S5_GROUPS, S5_GROUP_DIM)
    lam = lax.complex(lam_re.astype(f32), lam_im.astype(f32))
    dt = jnp.exp(log_dt.astype(f32))[:, None]
    lam_bar = jnp.exp(lam * dt)
    b_mat = lax.complex(b_re.astype(f32), b_im.astype(f32))
    b_bar = ((lam_bar - 1.0) / lam)[:, :, None] * b_mat
    c_mat = lax.complex(c_re.astype(f32), c_im.astype(f32))
    bu = jnp.einsum('gph,bsgh->bsgp', b_bar, uf.astype(jnp.complex64))
    a = jnp.broadcast_to(lam_bar, bu.shape)
    _, states = lax.associative_scan(_ssm_combine, (a, bu), axis=1)
    y = jnp.real(jnp.einsum('ghp,bsgp->bsgh', c_mat, states)) + d_skip.astype(f32) * uf
    y = jax.nn.gelu(y)
    gate = jnp.einsum('gij,bsgj->bsgi', w_glu.astype(f32), y) + b_glu.astype(f32)
    out = y * jax.nn.sigmoid(gate)
    return out.reshape(B, S, S5_WIDTH).astype(u.dtype)


def fox_mixer(q, k, v, f_logit):
    B, S, _ = q.shape
    q = q.reshape(B, S, FOX_HEADS, HEAD_DIM)
    k = k.reshape(B, S, FOX_HEADS, HEAD_DIM)
    v = v.reshape(B, S, FOX_HEADS, HEAD_DIM)
    log_f = jax.nn.log_sigmoid(f_logit.astype(jnp.float32))
    cum = jnp.cumsum(log_f, axis=1).transpose(0, 2, 1)
    scale = HEAD_DIM ** -0.5
    outs = []
    for i in range(S // Q_BLOCK):
        q0, q1 = i * Q_BLOCK, (i + 1) * Q_BLOCK
        logits = jnp.einsum('bthd,bshd->bhts', q[:, q0:q1], k[:, :q1]).astype(jnp.float32) * scale
        decay = cum[:, :, q0:q1, None] - cum[:, :, None, :q1]
        causal = jnp.arange(q1)[None, :] <= (q0 + jnp.arange(Q_BLOCK))[:, None]
        logits = jnp.where(causal, logits + decay, NEG_INF)
        p = jax.nn.softmax(logits, axis=-1)
        outs.append(jnp.einsum('bhts,bshd->bthd', p.astype(v.dtype), v[:, :q1]))
    return jnp.concatenate(outs, axis=1).reshape(B, S, FOX_WIDTH)


def conv_ffn(h, w_up, conv_w, conv_b, w_down):
    S = h.shape[1]
    up = h @ w_up
    a, g = jnp.split(up, 2, axis=-1)
    a_pad = jnp.pad(a, ((0, 0), (CONV_WIDTH - 1, 0), (0, 0)))
    conv = conv_b
    for j in range(CONV_WIDTH):
        conv = conv + conv_w[j] * a_pad[:, j:j + S]
    return (jax.nn.gelu(conv) * g) @ w_down


def _fwd_setup_inputs(seed: int = 0) -> dict:
    key = jax.random.key(seed)
    ks = jax.random.split(key, 32)
    f32 = jnp.float32
    L, D = DEPTH, D_MODEL
    nrm = lambda k, shape, s: jax.random.normal(k, shape, f32) * s
    n_idx = jnp.arange(S5_STATE, dtype=f32)
    return {
        "x": jax.random.normal(ks[0], (BATCH, SEQ, D), f32),
        "c": jax.random.normal(ks[1], (BATCH, D), f32),
        "w_ada": nrm(ks[2], (L, D, 6 * D), 0.1 * D ** -0.5),
        "b_ada": nrm(ks[3], (L, 6 * D), 0.01),
        "w_in": nrm(ks[4], (L, D, D_IN_PROJ), D ** -0.5),
        "b_f": jax.random.uniform(ks[5], (L, FOX_HEADS), f32, 1.0, 3.0),
        "gm_ln_g": 1.0 + nrm(ks[6], (L, GM_HEADS, HEAD_DIM), 0.02),
        "gm_ln_b": nrm(ks[7], (L, GM_HEADS, HEAD_DIM), 0.02),
        "gm_w_s": nrm(ks[8], (L, GM_HEADS, GM_CHUNK, GM_CHUNK), GM_CHUNK ** -0.5),
        "gm_b_s": 1.0 + nrm(ks[9], (L, GM_HEADS, GM_CHUNK), 0.02),
        "s5_lam_re": -0.5 + nrm(ks[10], (L, S5_GROUPS, S5_STATE), 0.01),
        "s5_lam_im": math.pi * n_idx + nrm(ks[11], (L, S5_GROUPS, S5_STATE), 0.01),
        "s5_log_dt": jax.random.uniform(ks[12], (L, S5_GROUPS), f32,
                                        math.log(S5_DT_MIN), math.log(S5_DT_MAX)),
        "s5_b_re": nrm(ks[13], (L, S5_GROUPS, S5_STATE, S5_GROUP_DIM), (2 * S5_GROUP_DIM) ** -0.5),
        "s5_b_im": nrm(ks[14], (L, S5_GROUPS, S5_STATE, S5_GROUP_DIM), (2 * S5_GROUP_DIM) ** -0.5),
        "s5_c_re": nrm(ks[15], (L, S5_GROUPS, S5_GROUP_DIM, S5_STATE), (2 * S5_STATE) ** -0.5),
        "s5_c_im": nrm(ks[16], (L, S5_GROUPS, S5_GROUP_DIM, S5_STATE), (2 * S5_STATE) ** -0.5),
        "s5_d": nrm(ks[17], (L, S5_GROUPS, S5_GROUP_DIM), 1.0),
        "s5_w_glu": nrm(ks[18], (L, S5_GROUPS, S5_GROUP_DIM, S5_GROUP_DIM), S5_GROUP_DIM ** -0.5),
        "s5_b_glu": nrm(ks[19], (L, S5_GROUPS, S5_GROUP_DIM), 0.02),
        "w_out": nrm(ks[20], (L, D_MIX, D), DN_BETA * D_MIX ** -0.5),
        "ln1_g": 1.0 + nrm(ks[21], (L, D), 0.02),
        "ln1_b": nrm(ks[22], (L, D), 0.02),
        "w_up": nrm(ks[23], (L, D, 2 * D_FF), D ** -0.5),
        "conv_w": nrm(ks[24], (L, CONV_WIDTH, D_FF), CONV_WIDTH ** -0.5),
        "conv_b": nrm(ks[25], (L, D_FF), 0.02),
        "w_down": nrm(ks[26], (L, D_FF, D), DN_BETA * D_FF ** -0.5),
        "ln2_g": 1.0 + nrm(ks[27], (L, D), 0.02),
        "ln2_b": nrm(ks[28], (L, D), 0.02),
    }


def _fwd_reference(x, c, w_ada, b_ada, w_in, b_f, gm_ln_g, gm_ln_b, gm_w_s, gm_b_s,
              s5_lam_re, s5_lam_im, s5_log_dt, s5_b_re, s5_b_im, s5_c_re, s5_c_im,
              s5_d, s5_w_glu, s5_b_glu, w_out, ln1_g, ln1_b, w_up, conv_w, conv_b,
              w_down, ln2_g, ln2_b):
    cond = jax.nn.silu(c)
    for l in range(DEPTH):
        mod = (cond @ w_ada[l] + b_ada[l])[:, None, :]
        sh1, sc1, g1, sh2, sc2, g2 = jnp.split(mod, 6, axis=-1)

        h = x * (1.0 + sc1) + sh1
        p = h @ w_in[l]
        gm_u, gm_v, s5_in, fq, fk, fv, ff = jnp.split(p, IN_SPLITS, axis=-1)
        y_gm = gmlp_mixer(gm_u, gm_v, gm_ln_g[l], gm_ln_b[l], gm_w_s[l], gm_b_s[l])
        y_s5 = s5_mixer(s5_in, s5_lam_re[l], s5_lam_im[l], s5_log_dt[l], s5_b_re[l], s5_b_im[l],
                        s5_c_re[l], s5_c_im[l], s5_d[l], s5_w_glu[l], s5_b_glu[l])
        y_fox = fox_mixer(fq, fk, fv, ff + b_f[l])
        mix = jnp.concatenate([y_gm, y_s5, y_fox], axis=-1) @ w_out[l]
        x = layer_norm(DN_ALPHA * x + (1.0 + g1) * mix, ln1_g[l], ln1_b[l])

        h = x * (1.0 + sc2) + sh2
        ffn = conv_ffn(h, w_up[l], conv_w[l], conv_b[l], w_down[l])
        x = layer_norm(DN_ALPHA * x + (1.0 + g2) * ffn, ln2_g[l], ln2_b[l])
    return x


import jax as _jax
import jax.numpy as _jnp

TWIN_FORMAT = 'train_step'
FWD_PARAMS = ['x', 'c', 'w_ada', 'b_ada', 'w_in', 'b_f', 'gm_ln_g', 'gm_ln_b', 'gm_w_s', 'gm_b_s', 's5_lam_re', 's5_lam_im', 's5_log_dt', 's5_b_re', 's5_b_im', 's5_c_re', 's5_c_im', 's5_d', 's5_w_glu', 's5_b_glu', 'w_out', 'ln1_g', 'ln1_b', 'w_up', 'conv_w', 'conv_b', 'w_down', 'ln2_g', 'ln2_b']
TWIN_WEIGHTS = ['w_ada', 'b_ada', 'w_in', 'b_f', 'gm_ln_g', 'gm_ln_b', 'gm_w_s', 'gm_b_s', 's5_lam_re', 's5_lam_im', 's5_log_dt', 's5_b_re', 's5_b_im', 's5_c_re', 's5_c_im', 's5_d', 's5_w_glu', 's5_b_glu', 'w_out', 'ln1_g', 'ln1_b', 'w_up', 'conv_w', 'conv_b', 'w_down', 'ln2_g', 'ln2_b']
TWIN_DIFF_INPUT = 'x'
TWIN_INPUTS = ['x', 'c', 'w_ada', 'b_ada', 'w_in', 'b_f', 'gm_ln_g', 'gm_ln_b', 'gm_w_s', 'gm_b_s', 's5_lam_re', 's5_lam_im', 's5_log_dt', 's5_b_re', 's5_b_im', 's5_c_re', 's5_c_im', 's5_d', 's5_w_glu', 's5_b_glu', 'w_out', 'ln1_g', 'ln1_b', 'w_up', 'conv_w', 'conv_b', 'w_down', 'ln2_g', 'ln2_b', 'loss_target', 'm_w_ada', 'm_b_ada', 'm_w_in', 'm_b_f', 'm_gm_ln_g', 'm_gm_ln_b', 'm_gm_w_s', 'm_gm_b_s', 'm_s5_lam_re', 'm_s5_lam_im', 'm_s5_log_dt', 'm_s5_b_re', 'm_s5_b_im', 'm_s5_c_re', 'm_s5_c_im', 'm_s5_d', 'm_s5_w_glu', 'm_s5_b_glu', 'm_w_out', 'm_ln1_g', 'm_ln1_b', 'm_w_up', 'm_conv_w', 'm_conv_b', 'm_w_down', 'm_ln2_g', 'm_ln2_b', 'v_w_ada', 'v_b_ada', 'v_w_in', 'v_b_f', 'v_gm_ln_g', 'v_gm_ln_b', 'v_gm_w_s', 'v_gm_b_s', 'v_s5_lam_re', 'v_s5_lam_im', 'v_s5_log_dt', 'v_s5_b_re', 'v_s5_b_im', 'v_s5_c_re', 'v_s5_c_im', 'v_s5_d', 'v_s5_w_glu', 'v_s5_b_glu', 'v_w_out', 'v_ln1_g', 'v_ln1_b', 'v_w_up', 'v_conv_w', 'v_conv_b', 'v_w_down', 'v_ln2_g', 'v_ln2_b']
TWIN_OUTPUTS = ['loss', 'grad_x', 'grad_w_ada', 'grad_b_ada', 'grad_w_in', 'grad_b_f', 'grad_gm_ln_g', 'grad_gm_ln_b', 'grad_gm_w_s', 'grad_gm_b_s', 'grad_s5_lam_re', 'grad_s5_lam_im', 'grad_s5_log_dt', 'grad_s5_b_re', 'grad_s5_b_im', 'grad_s5_c_re', 'grad_s5_c_im', 'grad_s5_d', 'grad_s5_w_glu', 'grad_s5_b_glu', 'grad_w_out', 'grad_ln1_g', 'grad_ln1_b', 'grad_w_up', 'grad_conv_w', 'grad_conv_b', 'grad_w_down', 'grad_ln2_g', 'grad_ln2_b', 'delta_w_ada', 'delta_b_ada', 'delta_w_in', 'delta_b_f', 'delta_gm_ln_g', 'delta_gm_ln_b', 'delta_gm_w_s', 'delta_gm_b_s', 'delta_s5_lam_re', 'delta_s5_lam_im', 'delta_s5_log_dt', 'delta_s5_b_re', 'delta_s5_b_im', 'delta_s5_c_re', 'delta_s5_c_im', 'delta_s5_d', 'delta_s5_w_glu', 'delta_s5_b_glu', 'delta_w_out', 'delta_ln1_g', 'delta_ln1_b', 'delta_w_up', 'delta_conv_w', 'delta_conv_b', 'delta_w_down', 'delta_ln2_g', 'delta_ln2_b', 'new_m_w_ada', 'new_m_b_ada', 'new_m_w_in', 'new_m_b_f', 'new_m_gm_ln_g', 'new_m_gm_ln_b', 'new_m_gm_w_s', 'new_m_gm_b_s', 'new_m_s5_lam_re', 'new_m_s5_lam_im', 'new_m_s5_log_dt', 'new_m_s5_b_re', 'new_m_s5_b_im', 'new_m_s5_c_re', 'new_m_s5_c_im', 'new_m_s5_d', 'new_m_s5_w_glu', 'new_m_s5_b_glu', 'new_m_w_out', 'new_m_ln1_g', 'new_m_ln1_b', 'new_m_w_up', 'new_m_conv_w', 'new_m_conv_b', 'new_m_w_down', 'new_m_ln2_g', 'new_m_ln2_b', 'new_v_w_ada', 'new_v_b_ada', 'new_v_w_in', 'new_v_b_f', 'new_v_gm_ln_g', 'new_v_gm_ln_b', 'new_v_gm_w_s', 'new_v_gm_b_s', 'new_v_s5_lam_re', 'new_v_s5_lam_im', 'new_v_s5_log_dt', 'new_v_s5_b_re', 'new_v_s5_b_im', 'new_v_s5_c_re', 'new_v_s5_c_im', 'new_v_s5_d', 'new_v_s5_w_glu', 'new_v_s5_b_glu', 'new_v_w_out', 'new_v_ln1_g', 'new_v_ln1_b', 'new_v_w_up', 'new_v_conv_w', 'new_v_conv_b', 'new_v_w_down', 'new_v_ln2_g', 'new_v_ln2_b']
TWIN_LEAF_KINDS = {'loss': 'loss', 'grad_x': 'grad_x', 'grad_w_ada': 'grad_w', 'grad_b_ada': 'grad_w', 'grad_w_in': 'grad_w', 'grad_b_f': 'grad_w', 'grad_gm_ln_g': 'grad_w', 'grad_gm_ln_b': 'grad_w', 'grad_gm_w_s': 'grad_w', 'grad_gm_b_s': 'grad_w', 'grad_s5_lam_re': 'grad_w', 'grad_s5_lam_im': 'grad_w', 'grad_s5_log_dt': 'grad_w', 'grad_s5_b_re': 'grad_w', 'grad_s5_b_im': 'grad_w', 'grad_s5_c_re': 'grad_w', 'grad_s5_c_im': 'grad_w', 'grad_s5_d': 'grad_w', 'grad_s5_w_glu': 'grad_w', 'grad_s5_b_glu': 'grad_w', 'grad_w_out': 'grad_w', 'grad_ln1_g': 'grad_w', 'grad_ln1_b': 'grad_w', 'grad_w_up': 'grad_w', 'grad_conv_w': 'grad_w', 'grad_conv_b': 'grad_w', 'grad_w_down': 'grad_w', 'grad_ln2_g': 'grad_w', 'grad_ln2_b': 'grad_w', 'delta_w_ada': 'delta_w', 'delta_b_ada': 'delta_w', 'delta_w_in': 'delta_w', 'delta_b_f': 'delta_w', 'delta_gm_ln_g': 'delta_w', 'delta_gm_ln_b': 'delta_w', 'delta_gm_w_s': 'delta_w', 'delta_gm_b_s': 'delta_w', 'delta_s5_lam_re': 'delta_w', 'delta_s5_lam_im': 'delta_w', 'delta_s5_log_dt': 'delta_w', 'delta_s5_b_re': 'delta_w', 'delta_s5_b_im': 'delta_w', 'delta_s5_c_re': 'delta_w', 'delta_s5_c_im': 'delta_w', 'delta_s5_d': 'delta_w', 'delta_s5_w_glu': 'delta_w', 'delta_s5_b_glu': 'delta_w', 'delta_w_out': 'delta_w', 'delta_ln1_g': 'delta_w', 'delta_ln1_b': 'delta_w', 'delta_w_up': 'delta_w', 'delta_conv_w': 'delta_w', 'delta_conv_b': 'delta_w', 'delta_w_down': 'delta_w', 'delta_ln2_g': 'delta_w', 'delta_ln2_b': 'delta_w', 'new_m_w_ada': 'new_m', 'new_m_b_ada': 'new_m', 'new_m_w_in': 'new_m', 'new_m_b_f': 'new_m', 'new_m_gm_ln_g': 'new_m', 'new_m_gm_ln_b': 'new_m', 'new_m_gm_w_s': 'new_m', 'new_m_gm_b_s': 'new_m', 'new_m_s5_lam_re': 'new_m', 'new_m_s5_lam_im': 'new_m', 'new_m_s5_log_dt': 'new_m', 'new_m_s5_b_re': 'new_m', 'new_m_s5_b_im': 'new_m', 'new_m_s5_c_re': 'new_m', 'new_m_s5_c_im': 'new_m', 'new_m_s5_d': 'new_m', 'new_m_s5_w_glu': 'new_m', 'new_m_s5_b_glu': 'new_m', 'new_m_w_out': 'new_m', 'new_m_ln1_g': 'new_m', 'new_m_ln1_b': 'new_m', 'new_m_w_up': 'new_m', 'new_m_conv_w': 'new_m', 'new_m_conv_b': 'new_m', 'new_m_w_down': 'new_m', 'new_m_ln2_g': 'new_m', 'new_m_ln2_b': 'new_m', 'new_v_w_ada': 'new_v', 'new_v_b_ada': 'new_v', 'new_v_w_in': 'new_v', 'new_v_b_f': 'new_v', 'new_v_gm_ln_g': 'new_v', 'new_v_gm_ln_b': 'new_v', 'new_v_gm_w_s': 'new_v', 'new_v_gm_b_s': 'new_v', 'new_v_s5_lam_re': 'new_v', 'new_v_s5_lam_im': 'new_v', 'new_v_s5_log_dt': 'new_v', 'new_v_s5_b_re': 'new_v', 'new_v_s5_b_im': 'new_v', 'new_v_s5_c_re': 'new_v', 'new_v_s5_c_im': 'new_v', 'new_v_s5_d': 'new_v', 'new_v_s5_w_glu': 'new_v', 'new_v_s5_b_glu': 'new_v', 'new_v_w_out': 'new_v', 'new_v_ln1_g': 'new_v', 'new_v_ln1_b': 'new_v', 'new_v_w_up': 'new_v', 'new_v_conv_w': 'new_v', 'new_v_conv_b': 'new_v', 'new_v_w_down': 'new_v', 'new_v_ln2_g': 'new_v', 'new_v_ln2_b': 'new_v'}


def _forward(args):
    return _fwd_reference(*[args[k] for k in FWD_PARAMS])


def _output_shape():
    def fwd():
        inp = _fwd_setup_inputs(0)
        return _fwd_reference(*[inp[k] for k in FWD_PARAMS])
    out = _jax.eval_shape(fwd)
    return out.shape, out.dtype

N_MICROBATCH = 1
ADAM_LR = 0.001
ADAM_B1 = 0.9
ADAM_B2 = 0.999
ADAM_EPS = 1e-08
ADAM_WD = 0.01
ADAM_STEP = 10
PER_EXAMPLE_BATCH_AXIS = {'x': 0, 'c': 0, 'loss_target': 0}
SHARED_INPUTS = []
_WEIGHT_DTYPES = {'w_ada': _jnp.float32, 'b_ada': _jnp.float32, 'w_in': _jnp.float32, 'b_f': _jnp.float32, 'gm_ln_g': _jnp.float32, 'gm_ln_b': _jnp.float32, 'gm_w_s': _jnp.float32, 'gm_b_s': _jnp.float32, 's5_lam_re': _jnp.float32, 's5_lam_im': _jnp.float32, 's5_log_dt': _jnp.float32, 's5_b_re': _jnp.float32, 's5_b_im': _jnp.float32, 's5_c_re': _jnp.float32, 's5_c_im': _jnp.float32, 's5_d': _jnp.float32, 's5_w_glu': _jnp.float32, 's5_b_glu': _jnp.float32, 'w_out': _jnp.float32, 'ln1_g': _jnp.float32, 'ln1_b': _jnp.float32, 'w_up': _jnp.float32, 'conv_w': _jnp.float32, 'conv_b': _jnp.float32, 'w_down': _jnp.float32, 'ln2_g': _jnp.float32, 'ln2_b': _jnp.float32}
MOMENT_SCALE = {'w_ada': 2.675547e-02, 'b_ada': 5.999735e-02, 'w_in': 2.600702e-02, 'b_f': 1.133902e-01, 'gm_ln_g': 3.332238e-02, 'gm_ln_b': 3.274502e-02, 'gm_w_s': 2.298824e-02, 'gm_b_s': 3.275349e-02, 's5_lam_re': 9.760487e-04, 's5_lam_im': 9.745716e-04, 's5_log_dt': 9.194326e-01, 's5_b_re': 5.838162e-04, 's5_b_im': 5.925003e-04, 's5_c_re': 1.164089e-03, 's5_c_im': 1.172244e-03, 's5_d': 2.724352e-02, 's5_w_glu': 7.886614e-03, 's5_b_glu': 1.111987e-02, 'w_out': 8.004505e-02, 'ln1_g': 1.100214e+00, 'ln1_b': 5.646981e-01, 'w_up': 1.855499e-02, 'conv_w': 1.887188e-02, 'conv_b': 1.816412e-02, 'w_down': 7.221054e-02, 'ln2_g': 1.610161e+01, 'ln2_b': 1.211836e+00}


def _to_microbatches(a, axis):
    t = _jnp.moveaxis(a, axis, 0)
    t = t.reshape((N_MICROBATCH, t.shape[0] // N_MICROBATCH) + t.shape[1:])
    return _jnp.moveaxis(t, 1, axis + 1)


def setup_inputs(seed: int = 0) -> dict:
    inp = _fwd_setup_inputs(seed)
    key = _jax.random.fold_in(_jax.random.key(seed), 7919)
    shape, _ = _output_shape()
    out = dict(inp)
    out["loss_target"] = _jax.random.normal(_jax.random.fold_in(key, 0), shape, _jnp.float32)
    for i, name in enumerate(TWIN_WEIGHTS):
        w = inp[name].astype(_jnp.float32)
        if MOMENT_SCALE is None:
            s = _jnp.sqrt(_jnp.mean(_jnp.square(w)) + 1e-30)
        else:
            s = MOMENT_SCALE[name]
        km, kv = _jax.random.split(_jax.random.fold_in(key, i + 1))
        out[name] = w
        out["m_" + name] = s * _jax.random.normal(km, w.shape, _jnp.float32)
        out["v_" + name] = (s * s) * _jax.random.uniform(kv, w.shape, _jnp.float32, 0.5, 1.5)
    if N_MICROBATCH > 1:
        for name, axis in PER_EXAMPLE_BATCH_AXIS.items():
            out[name] = _to_microbatches(out[name], axis)
    return {'x': out['x'], 'c': out['c'], 'w_ada': out['w_ada'], 'b_ada': out['b_ada'], 'w_in': out['w_in'], 'b_f': out['b_f'], 'gm_ln_g': out['gm_ln_g'], 'gm_ln_b': out['gm_ln_b'], 'gm_w_s': out['gm_w_s'], 'gm_b_s': out['gm_b_s'], 's5_lam_re': out['s5_lam_re'], 's5_lam_im': out['s5_lam_im'], 's5_log_dt': out['s5_log_dt'], 's5_b_re': out['s5_b_re'], 's5_b_im': out['s5_b_im'], 's5_c_re': out['s5_c_re'], 's5_c_im': out['s5_c_im'], 's5_d': out['s5_d'], 's5_w_glu': out['s5_w_glu'], 's5_b_glu': out['s5_b_glu'], 'w_out': out['w_out'], 'ln1_g': out['ln1_g'], 'ln1_b': out['ln1_b'], 'w_up': out['w_up'], 'conv_w': out['conv_w'], 'conv_b': out['conv_b'], 'w_down': out['w_down'], 'ln2_g': out['ln2_g'], 'ln2_b': out['ln2_b'], 'loss_target': out['loss_target'], 'm_w_ada': out['m_w_ada'], 'm_b_ada': out['m_b_ada'], 'm_w_in': out['m_w_in'], 'm_b_f': out['m_b_f'], 'm_gm_ln_g': out['m_gm_ln_g'], 'm_gm_ln_b': out['m_gm_ln_b'], 'm_gm_w_s': out['m_gm_w_s'], 'm_gm_b_s': out['m_gm_b_s'], 'm_s5_lam_re': out['m_s5_lam_re'], 'm_s5_lam_im': out['m_s5_lam_im'], 'm_s5_log_dt': out['m_s5_log_dt'], 'm_s5_b_re': out['m_s5_b_re'], 'm_s5_b_im': out['m_s5_b_im'], 'm_s5_c_re': out['m_s5_c_re'], 'm_s5_c_im': out['m_s5_c_im'], 'm_s5_d': out['m_s5_d'], 'm_s5_w_glu': out['m_s5_w_glu'], 'm_s5_b_glu': out['m_s5_b_glu'], 'm_w_out': out['m_w_out'], 'm_ln1_g': out['m_ln1_g'], 'm_ln1_b': out['m_ln1_b'], 'm_w_up': out['m_w_up'], 'm_conv_w': out['m_conv_w'], 'm_conv_b': out['m_conv_b'], 'm_w_down': out['m_w_down'], 'm_ln2_g': out['m_ln2_g'], 'm_ln2_b': out['m_ln2_b'], 'v_w_ada': out['v_w_ada'], 'v_b_ada': out['v_b_ada'], 'v_w_in': out['v_w_in'], 'v_b_f': out['v_b_f'], 'v_gm_ln_g': out['v_gm_ln_g'], 'v_gm_ln_b': out['v_gm_ln_b'], 'v_gm_w_s': out['v_gm_w_s'], 'v_gm_b_s': out['v_gm_b_s'], 'v_s5_lam_re': out['v_s5_lam_re'], 'v_s5_lam_im': out['v_s5_lam_im'], 'v_s5_log_dt': out['v_s5_log_dt'], 'v_s5_b_re': out['v_s5_b_re'], 'v_s5_b_im': out['v_s5_b_im'], 'v_s5_c_re': out['v_s5_c_re'], 'v_s5_c_im': out['v_s5_c_im'], 'v_s5_d': out['v_s5_d'], 'v_s5_w_glu': out['v_s5_w_glu'], 'v_s5_b_glu': out['v_s5_b_glu'], 'v_w_out': out['v_w_out'], 'v_ln1_g': out['v_ln1_g'], 'v_ln1_b': out['v_ln1_b'], 'v_w_up': out['v_w_up'], 'v_conv_w': out['v_conv_w'], 'v_conv_b': out['v_conv_b'], 'v_w_down': out['v_w_down'], 'v_ln2_g': out['v_ln2_g'], 'v_ln2_b': out['v_ln2_b']}


def _loss(weights, diff, rest, loss_target):
    with _jax.named_scope("forward"):
        args = {**rest, TWIN_DIFF_INPUT: diff, **{k: w.astype(_WEIGHT_DTYPES[k]) for k, w in weights.items()}}
        y = _forward(args)
    with _jax.named_scope("loss_head"):
        err = _jnp.square(y.astype(_jnp.float32) - loss_target)
        return 0.5 * _jnp.sum(_jnp.mean(err, axis=-1)) if err.ndim else 0.5 * err


def _adamw(w, g, m, v):
    m = ADAM_B1 * m + (1.0 - ADAM_B1) * g
    v = ADAM_B2 * v + (1.0 - ADAM_B2) * _jnp.square(g)
    m_hat = m / (1.0 - ADAM_B1 ** ADAM_STEP)
    v_hat = v / (1.0 - ADAM_B2 ** ADAM_STEP)
    delta = -ADAM_LR * (m_hat / (_jnp.sqrt(v_hat) + ADAM_EPS) + ADAM_WD * w)
    return delta, m, v


def reference(x, c, w_ada, b_ada, w_in, b_f, gm_ln_g, gm_ln_b, gm_w_s, gm_b_s, s5_lam_re, s5_lam_im, s5_log_dt, s5_b_re, s5_b_im, s5_c_re, s5_c_im, s5_d, s5_w_glu, s5_b_glu, w_out, ln1_g, ln1_b, w_up, conv_w, conv_b, w_down, ln2_g, ln2_b, loss_target, m_w_ada, m_b_ada, m_w_in, m_b_f, m_gm_ln_g, m_gm_ln_b, m_gm_w_s, m_gm_b_s, m_s5_lam_re, m_s5_lam_im, m_s5_log_dt, m_s5_b_re, m_s5_b_im, m_s5_c_re, m_s5_c_im, m_s5_d, m_s5_w_glu, m_s5_b_glu, m_w_out, m_ln1_g, m_ln1_b, m_w_up, m_conv_w, m_conv_b, m_w_down, m_ln2_g, m_ln2_b, v_w_ada, v_b_ada, v_w_in, v_b_f, v_gm_ln_g, v_gm_ln_b, v_gm_w_s, v_gm_b_s, v_s5_lam_re, v_s5_lam_im, v_s5_log_dt, v_s5_b_re, v_s5_b_im, v_s5_c_re, v_s5_c_im, v_s5_d, v_s5_w_glu, v_s5_b_glu, v_w_out, v_ln1_g, v_ln1_b, v_w_up, v_conv_w, v_conv_b, v_w_down, v_ln2_g, v_ln2_b):
    given = dict(x=x, c=c, w_ada=w_ada, b_ada=b_ada, w_in=w_in, b_f=b_f, gm_ln_g=gm_ln_g, gm_ln_b=gm_ln_b, gm_w_s=gm_w_s, gm_b_s=gm_b_s, s5_lam_re=s5_lam_re, s5_lam_im=s5_lam_im, s5_log_dt=s5_log_dt, s5_b_re=s5_b_re, s5_b_im=s5_b_im, s5_c_re=s5_c_re, s5_c_im=s5_c_im, s5_d=s5_d, s5_w_glu=s5_w_glu, s5_b_glu=s5_b_glu, w_out=w_out, ln1_g=ln1_g, ln1_b=ln1_b, w_up=w_up, conv_w=conv_w, conv_b=conv_b, w_down=w_down, ln2_g=ln2_g, ln2_b=ln2_b, loss_target=loss_target, m_w_ada=m_w_ada, m_b_ada=m_b_ada, m_w_in=m_w_in, m_b_f=m_b_f, m_gm_ln_g=m_gm_ln_g, m_gm_ln_b=m_gm_ln_b, m_gm_w_s=m_gm_w_s, m_gm_b_s=m_gm_b_s, m_s5_lam_re=m_s5_lam_re, m_s5_lam_im=m_s5_lam_im, m_s5_log_dt=m_s5_log_dt, m_s5_b_re=m_s5_b_re, m_s5_b_im=m_s5_b_im, m_s5_c_re=m_s5_c_re, m_s5_c_im=m_s5_c_im, m_s5_d=m_s5_d, m_s5_w_glu=m_s5_w_glu, m_s5_b_glu=m_s5_b_glu, m_w_out=m_w_out, m_ln1_g=m_ln1_g, m_ln1_b=m_ln1_b, m_w_up=m_w_up, m_conv_w=m_conv_w, m_conv_b=m_conv_b, m_w_down=m_w_down, m_ln2_g=m_ln2_g, m_ln2_b=m_ln2_b, v_w_ada=v_w_ada, v_b_ada=v_b_ada, v_w_in=v_w_in, v_b_f=v_b_f, v_gm_ln_g=v_gm_ln_g, v_gm_ln_b=v_gm_ln_b, v_gm_w_s=v_gm_w_s, v_gm_b_s=v_gm_b_s, v_s5_lam_re=v_s5_lam_re, v_s5_lam_im=v_s5_lam_im, v_s5_log_dt=v_s5_log_dt, v_s5_b_re=v_s5_b_re, v_s5_b_im=v_s5_b_im, v_s5_c_re=v_s5_c_re, v_s5_c_im=v_s5_c_im, v_s5_d=v_s5_d, v_s5_w_glu=v_s5_w_glu, v_s5_b_glu=v_s5_b_glu, v_w_out=v_w_out, v_ln1_g=v_ln1_g, v_ln1_b=v_ln1_b, v_w_up=v_w_up, v_conv_w=v_conv_w, v_conv_b=v_conv_b, v_w_down=v_w_down, v_ln2_g=v_ln2_g, v_ln2_b=v_ln2_b)
    weights = {n: given[n] for n in TWIN_WEIGHTS}
    shared = {n: given[n] for n in SHARED_INPUTS}
    per_example = {n: given[n] for n in ['x', 'c']}
    grad_fn = _jax.value_and_grad(_loss, argnums=(0, 1))

    def one_microbatch(ex, loss_target):
        ex = dict(ex)
        diff = ex.pop(TWIN_DIFF_INPUT)
        return grad_fn(weights, diff, {**shared, **ex}, loss_target)

    if N_MICROBATCH == 1:
        loss, (grad_w, grad_x) = one_microbatch(per_example, given["loss_target"])
    else:
        def body(carry, xs):
            loss_sum, grad_sum = carry
            l_k, (gw_k, gx_k) = one_microbatch(xs[0], xs[1])
            with _jax.named_scope("update"):
                return (loss_sum + l_k, _jax.tree.map(_jnp.add, grad_sum, gw_k)), gx_k

        init = (_jnp.zeros((), _jnp.float32), _jax.tree.map(_jnp.zeros_like, weights))
        (loss, grad_w), grad_x = _jax.lax.scan(body, init, (per_example, given["loss_target"]))
    with _jax.named_scope("update"):
        delta_w, new_m, new_v = {}, {}, {}
        for n in TWIN_WEIGHTS:
            delta_w[n], new_m[n], new_v[n] = _adamw(weights[n], grad_w[n], given["m_" + n], given["v_" + n])
    return (loss, grad_x, *[grad_w[n] for n in TWIN_WEIGHTS], *[delta_w[n] for n in TWIN_WEIGHTS],
            *[new_m[n] for n in TWIN_WEIGHTS], *[new_v[n] for n in TWIN_WEIGHTS])
```

```python
import functools
import math

import jax
import jax.numpy as jnp
from jax import lax
from jax.experimental import pallas as pl
from jax.experimental.pallas import tpu as pltpu

F32 = jnp.float32
BF16 = jnp.bfloat16
MESH_IDS = pl.DeviceIdType.MESH

D = 1024
SEQ = 4096
DEPTH = 4
NDEV = 8
HD = 64
GM_W = 256
GM_H = 4
GM_C = 128
S5_W = 256
S5_G = 16
S5_H = 16
S5_P = 64
S5_N = S5_G * S5_P
FX_W = 512
FX_H = 8
D_IN = 2 * GM_W + S5_W + 3 * FX_W + FX_H
NP = 2432
FF_COL = 2304
DFF = 2816
LN_EPS = 1e-5
DN_ALPHA = (2.0 * DEPTH) ** 0.25
NEG_INF = -1e30
ADAM_LR = 0.001
ADAM_B1 = 0.9
ADAM_B2 = 0.999
ADAM_EPS = 1e-08
ADAM_WD = 0.01
ADAM_STEP = 10

V7X_VMEM_LIMIT = 56 * 1024 * 1024
TS = 512
TS_C = 256
T_S5 = 256
TQ = 256


def _call(body, **kw):
    return pl.pallas_call(body, **kw)


def _params(*sem):
    return pltpu.CompilerParams(dimension_semantics=sem if sem else None,
                                vmem_limit_bytes=V7X_VMEM_LIMIT)


def _nn(a, b):
    return jnp.dot(a.astype(BF16), b.astype(BF16), preferred_element_type=F32)


def _nt(a, b):
    return lax.dot_general(a.astype(BF16), b.astype(BF16), (((1,), (1,)), ((), ())),
                           preferred_element_type=F32)


def _tn(a, b):
    return lax.dot_general(a.astype(BF16), b.astype(BF16), (((0,), (0,)), ((), ())),
                           preferred_element_type=F32)


@jax.custom_vjp
def _bdot(a, b):
    return _nn(a, b)


def _bdot_fwd(a, b):
    return _nn(a, b), (a, b)


def _bdot_bwd(res, g):
    a, b = res
    return _nt(g, b), _tn(a, g)


_bdot.defvjp(_bdot_fwd, _bdot_bwd)


@jax.custom_vjp
def _bdot_nt(a, b):
    return _nt(a, b)


def _bdot_nt_fwd(a, b):
    return _nt(a, b), (a, b)


def _bdot_nt_bwd(res, g):
    a, b = res
    return _nn(g, b), _tn(g, a)


_bdot_nt.defvjp(_bdot_nt_fwd, _bdot_nt_bwd)


def _ln(r, g, b):
    mu = jnp.mean(r, axis=-1, keepdims=True)
    xc = r - mu
    var = jnp.mean(xc * xc, axis=-1, keepdims=True)
    return xc * lax.rsqrt(var + LN_EPS) * g + b


def _rows(shape):
    return lax.broadcasted_iota(jnp.int32, shape, 0)


def _lanes(shape):
    return lax.broadcasted_iota(jnp.int32, shape, 1)


def mm_nn(a, w, tn, name, mod=None, rows=None, out_dtype=F32):
    s, k = a.shape
    n = w.shape[1]
    ts = min(TS, s)

    def body(*refs):
        if mod is None:
            a_ref, w_ref, o_ref = refs
            h = a_ref[...]
        else:
            a_ref, m_ref, w_ref, o_ref = refs
            h = a_ref[...] * (1.0 + m_ref[rows[1]:rows[1] + 1, :]) + m_ref[rows[0]:rows[0] + 1, :]
        o_ref[...] = jnp.dot(h.astype(BF16), w_ref[...], preferred_element_type=F32).astype(out_dtype)

    in_specs = [pl.BlockSpec((ts, k), lambda j, i: (i, 0))]
    args = [a]
    if mod is not None:
        in_specs.append(pl.BlockSpec((8, k), lambda j, i: (0, 0)))
        args.append(mod)
    in_specs.append(pl.BlockSpec((k, tn), lambda j, i: (0, j)))
    args.append(w)
    return _call(body, name=name, grid=(n // tn, s // ts), in_specs=in_specs,
                 out_specs=pl.BlockSpec((ts, tn), lambda j, i: (i, j)),
                 out_shape=jax.ShapeDtypeStruct((s, n), out_dtype),
                 compiler_params=_params("arbitrary", "arbitrary"))(*args)


def mm_nt(dy, w, name):
    s, n = dy.shape
    k = w.shape[0]
    ts = min(TS, s)

    def body(dy_ref, w_ref, o_ref):
        o_ref[...] = _nt(dy_ref[...], w_ref[...])

    return _call(body, name=name, grid=(s // ts,),
                 in_specs=[pl.BlockSpec((ts, n), lambda i: (i, 0)),
                           pl.BlockSpec((k, n), lambda i: (0, 0))],
                 out_specs=pl.BlockSpec((ts, k), lambda i: (i, 0)),
                 out_shape=jax.ShapeDtypeStruct((s, k), F32),
                 compiler_params=_params("arbitrary"))(dy, w)


def mm_nt_mod(dy, w, x, dres, mod, rows, name):
    s, n = dy.shape
    k = w.shape[0]
    ts = min(TS, s)

    def body(dy_ref, w_ref, x_ref, r_ref, m_ref, dx_ref, dsh_ref, dsc_ref):
        @pl.when(pl.program_id(0) == 0)
        def _():
            dsh_ref[...] = jnp.zeros_like(dsh_ref)
            dsc_ref[...] = jnp.zeros_like(dsc_ref)

        dh = _nt(dy_ref[...], w_ref[...])
        dx_ref[...] = r_ref[...] + dh * (1.0 + m_ref[rows[1]:rows[1] + 1, :])
        dsh_ref[...] += jnp.sum(dh, axis=0, keepdims=True)
        dsc_ref[...] += jnp.sum(dh * x_ref[...], axis=0, keepdims=True)

    row = pl.BlockSpec((1, k), lambda i: (0, 0))
    tile = pl.BlockSpec((ts, k), lambda i: (i, 0))
    return _call(body, name=name, grid=(s // ts,),
                 in_specs=[pl.BlockSpec((ts, n), lambda i: (i, 0)),
                           pl.BlockSpec((k, n), lambda i: (0, 0)), tile, tile,
                           pl.BlockSpec((8, k), lambda i: (0, 0))],
                 out_specs=[tile, row, row],
                 out_shape=[jax.ShapeDtypeStruct((s, k), F32),
                            jax.ShapeDtypeStruct((1, k), F32),
                            jax.ShapeDtypeStruct((1, k), F32)],
                 compiler_params=_params("arbitrary"))(dy, w, x, dres, mod)


def mm_tn(a, dy, tn, name, mod=None, rows=None):
    s, k = a.shape
    n = dy.shape[1]
    ts = min(TS, s)

    def body(*refs):
        if mod is None:
            a_ref, dy_ref, o_ref = refs
            h = a_ref[...]
        else:
            a_ref, m_ref, dy_ref, o_ref = refs
            h = a_ref[...] * (1.0 + m_ref[rows[1]:rows[1] + 1, :]) + m_ref[rows[0]:rows[0] + 1, :]

        @pl.when(pl.program_id(1) == 0)
        def _():
            o_ref[...] = jnp.zeros_like(o_ref)

        o_ref[...] += _tn(h, dy_ref[...])

    in_specs = [pl.BlockSpec((ts, k), lambda j, i: (i, 0))]
    args = [a]
    if mod is not None:
        in_specs.append(pl.BlockSpec((8, k), lambda j, i: (0, 0)))
        args.append(mod)
    in_specs.append(pl.BlockSpec((ts, tn), lambda j, i: (i, j)))
    args.append(dy)
    return _call(body, name=name, grid=(n // tn, s // ts), in_specs=in_specs,
                 out_specs=pl.BlockSpec((k, tn), lambda j, i: (0, j)),
                 out_shape=jax.ShapeDtypeStruct((k, n), F32),
                 compiler_params=_params("arbitrary", "arbitrary"))(*args)


def _post_fn(x, br, gate, lg, lb):
    return _ln(DN_ALPHA * x + (1.0 + gate) * br, lg, lb)


def post_fwd(x, br, mod, grow, lg, lb, name):
    s = x.shape[0]
    ts = min(TS, s)

    def body(x_ref, b_ref, m_ref, lg_ref, lb_ref, o_ref):
        o_ref[...] = _post_fn(x_ref[...], b_ref[...], m_ref[grow:grow + 1, :], lg_ref[...], lb_ref[...])

    tile = pl.BlockSpec((ts, D), lambda i: (i, 0))
    row = pl.BlockSpec((1, D), lambda i: (0, 0))
    return _call(body, name=name, grid=(s // ts,),
                 in_specs=[tile, tile, pl.BlockSpec((8, D), lambda i: (0, 0)), row, row],
                 out_specs=tile, out_shape=jax.ShapeDtypeStruct((s, D), F32),
                 compiler_params=_params("arbitrary"))(x, br, mod, lg, lb)


def post_bwd(x, br, mod, grow, lg, lb, dy, name):
    s = x.shape[0]
    ts = min(TS, s)

    def body(x_ref, b_ref, m_ref, lg_ref, lb_ref, dy_ref, dx_ref, db_ref, dg_ref, dlg_ref, dlb_ref):
        @pl.when(pl.program_id(0) == 0)
        def _():
            dg_ref[...] = jnp.zeros_like(dg_ref)
            dlg_ref[...] = jnp.zeros_like(dlg_ref)
            dlb_ref[...] = jnp.zeros_like(dlb_ref)

        _, vjp = jax.vjp(_post_fn, x_ref[...], b_ref[...], m_ref[grow:grow + 1, :], lg_ref[...], lb_ref[...])
        dx, db, dg, dlg, dlb = vjp(dy_ref[...])
        dx_ref[...] = dx
        db_ref[...] = db.astype(BF16)
        dg_ref[...] += dg
        dlg_ref[...] += dlg
        dlb_ref[...] += dlb

    tile = pl.BlockSpec((ts, D), lambda i: (i, 0))
    row = pl.BlockSpec((1, D), lambda i: (0, 0))
    rs = jax.ShapeDtypeStruct((1, D), F32)
    return _call(body, name=name, grid=(s // ts,),
                 in_specs=[tile, tile, pl.BlockSpec((8, D), lambda i: (0, 0)), row, row, tile],
                 out_specs=[tile, tile, row, row, row],
                 out_shape=[jax.ShapeDtypeStruct((s, D), F32), jax.ShapeDtypeStruct((s, D), BF16), rs, rs, rs],
                 compiler_params=_params("arbitrary"))(x, br, mod, lg, lb, dy)


def loss_kernel(y, target):
    s = y.shape[0]
    ts = min(TS, s)

    def body(y_ref, t_ref, l_ref, dy_ref):
        @pl.when(pl.program_id(0) == 0)
        def _():
            l_ref[...] = jnp.zeros_like(l_ref)

        err = y_ref[...] - t_ref[...]
        dy_ref[...] = err * (1.0 / D)
        per_tok = jnp.mean(err * err, axis=-1, keepdims=True)
        l_ref[...] += 0.5 * jnp.sum(per_tok)

    tile = pl.BlockSpec((ts, D), lambda i: (i, 0))
    return _call(body, name="loss", grid=(s // ts,), in_specs=[tile, tile],
                 out_specs=[pl.BlockSpec((8, 128), lambda i: (0, 0)), tile],
                 out_shape=[jax.ShapeDtypeStruct((8, 128), F32), jax.ShapeDtypeStruct((s, D), F32)],
                 compiler_params=_params("arbitrary"))(y, target)


def _gm_head(u, v, lg, lb, w, bs):
    t = u.shape[0]
    causal = _rows((GM_C, GM_C)) >= _lanes((GM_C, GM_C))
    vn = _ln(v, lg, lb)
    wm = jnp.where(causal, w, 0.0)
    chunks = []
    for n in range(t // GM_C):
        rs = slice(n * GM_C, (n + 1) * GM_C)
        chunks.append(u[rs] * (_bdot(wm, vn[rs]) + bs))
    return jnp.concatenate(chunks, axis=0)


def gm_fwd(p, lg, lb, ws, bst):
    s = p.shape[0]
    ts = min(TS_C, s)

    def body(u_ref, v_ref, lg_ref, lb_ref, ws_ref, bs_ref, o_ref):
        for h in range(GM_H):
            sl = slice(h * HD, (h + 1) * HD)
            o_ref[:, sl] = _gm_head(u_ref[:, sl], v_ref[:, sl], lg_ref[:, sl], lb_ref[:, sl], ws_ref[h],
                                    bs_ref[:, h:h + 1]).astype(BF16)

    full = lambda shape: pl.BlockSpec(shape, lambda i: (0,) * len(shape))
    return _call(body, name="gm_fwd", grid=(s // ts,),
                 in_specs=[pl.BlockSpec((ts, GM_W), lambda i: (i, 0)), pl.BlockSpec((ts, GM_W), lambda i: (i, 1)),
                           full((1, GM_W)), full((1, GM_W)), full((GM_H, GM_C, GM_C)), full((GM_C, GM_H))],
                 out_specs=pl.BlockSpec((ts, GM_W), lambda i: (i, 0)),
                 out_shape=jax.ShapeDtypeStruct((s, GM_W), BF16),
                 compiler_params=_params("arbitrary"))(p, p, lg, lb, ws, bst)


def gm_bwd(p, lg, lb, ws, bst, dmix):
    s = p.shape[0]
    ts = min(TS_C, s)

    def body(u_ref, v_ref, lg_ref, lb_ref, ws_ref, bs_ref, dy_ref, duv_ref, dlg_ref, dlb_ref, dws_ref, dbs_ref):
        @pl.when(pl.program_id(0) == 0)
        def _():
            dlg_ref[...] = jnp.zeros_like(dlg_ref)
            dlb_ref[...] = jnp.zeros_like(dlb_ref)
            dws_ref[...] = jnp.zeros_like(dws_ref)
            dbs_ref[...] = jnp.zeros_like(dbs_ref)

        for h in range(GM_H):
            sl = slice(h * HD, (h + 1) * HD)
            _, vjp = jax.vjp(_gm_head, u_ref[:, sl], v_ref[:, sl], lg_ref[:, sl], lb_ref[:, sl], ws_ref[h],
                             bs_ref[:, h:h + 1])
            du, dv, dlg, dlb, dw, dbs = vjp(dy_ref[:, sl])
            duv_ref[:, sl] = du.astype(BF16)
            duv_ref[:, GM_W + h * HD:GM_W + (h + 1) * HD] = dv.astype(BF16)
            dlg_ref[:, sl] += dlg
            dlb_ref[:, sl] += dlb
            dws_ref[h] += dw
            dbs_ref[:, h:h + 1] += dbs

    full = lambda shape: pl.BlockSpec(shape, lambda i: (0,) * len(shape))
    return _call(body, name="gm_bwd", grid=(s // ts,),
                 in_specs=[pl.BlockSpec((ts, GM_W), lambda i: (i, 0)), pl.BlockSpec((ts, GM_W), lambda i: (i, 1)),
                           full((1, GM_W)), full((1, GM_W)), full((GM_H, GM_C, GM_C)), full((GM_C, GM_H)),
                           pl.BlockSpec((ts, GM_W), lambda i: (i, 0))],
                 out_specs=[pl.BlockSpec((ts, 2 * GM_W), lambda i: (i, 0)), full((1, GM_W)), full((1, GM_W)),
                            full((GM_H, GM_C, GM_C)), full((GM_C, GM_H))],
                 out_shape=[jax.ShapeDtypeStruct((s, 2 * GM_W), BF16), jax.ShapeDtypeStruct((1, GM_W), F32),
                            jax.ShapeDtypeStruct((1, GM_W), F32), jax.ShapeDtypeStruct((GM_H, GM_C, GM_C), F32),
                            jax.ShapeDtypeStruct((GM_C, GM_H), F32)],
                 compiler_params=_params("arbitrary"))(p, p, lg, lb, ws, bst, dmix)


def _s5_prep_fn(lr, li, ldt, bt):
    dt = jnp.exp(ldt)
    er = jnp.exp(lr * dt)
    ar = er * jnp.cos(li * dt)
    ai = er * jnp.sin(li * dt)
    den = lr * lr + li * li
    nr = ar - 1.0
    cr = (nr * lr + ai * li) / den
    ci = (ai * lr - nr * li) / den
    br, bi = bt[:, :S5_N], bt[:, S5_N:]
    return ar, ai, jnp.concatenate([cr * br - ci * bi, cr * bi + ci * br], axis=1)


def s5_prep_fwd(lr, li, ldt, bt):
    def body(lr_ref, li_ref, ldt_ref, bt_ref, a_ref, bb_ref):
        ar, ai, bb = _s5_prep_fn(lr_ref[...], li_ref[...], ldt_ref[...], bt_ref[...])
        a_ref[...] = jnp.concatenate([ar, ai, jnp.zeros((6, S5_N), F32)], axis=0)
        bb_ref[...] = bb

    return _call(body, name="s5_prep_fwd",
                 out_shape=[jax.ShapeDtypeStruct((8, S5_N), F32), jax.ShapeDtypeStruct((S5_H, 2 * S5_N), F32)])(lr, li, ldt, bt)


def s5_prep_bwd(lr, li, ldt, bt, da, dbb):
    def body(lr_ref, li_ref, ldt_ref, bt_ref, da_ref, dbb_ref, dlr_ref, dli_ref, dldt_ref, dbt_ref):
        _, vjp = jax.vjp(_s5_prep_fn, lr_ref[...], li_ref[...], ldt_ref[...], bt_ref[...])
        dlr, dli, dldt, dbt = vjp((da_ref[0:1, :], da_ref[1:2, :], dbb_ref[...]))
        dlr_ref[...] = dlr
        dli_ref[...] = dli
        dbt_ref[...] = dbt
        group = (_rows((S5_N, 128)) // S5_P == _lanes((S5_N, 128))).astype(F32)
        dldt_ref[...] = jnp.dot(jnp.broadcast_to(dldt, (8, S5_N)), group, precision=lax.Precision.HIGHEST,
                                preferred_element_type=F32)[0:1, :]

    r = jax.ShapeDtypeStruct((1, S5_N), F32)
    return _call(body, name="s5_prep_bwd",
                 out_shape=[r, r, jax.ShapeDtypeStruct((1, 128), F32),
                            jax.ShapeDtypeStruct((S5_H, 2 * S5_N), F32)])(lr, li, ldt, bt, da, dbb)


def _s5_out_fn(x, u, cbd, drow, wg, bg):
    y = _bdot_nt(x[:, :S5_N], cbd[:, :S5_N]) - _bdot_nt(x[:, S5_N:], cbd[:, S5_N:]) + drow * u
    y = jax.nn.gelu(y)
    gate = _bdot_nt(y, wg) + bg
    return y * jax.nn.sigmoid(gate)


def _scan_steps(t):
    return int(math.log2(t))


def s5_fwd(p, arow, bbd, cbd, drow, wg, bg):
    s = p.shape[0]
    t = min(T_S5, s)

    def body(u_ref, a_ref, bbd_ref, cbd_ref, d_ref, wg_ref, bg_ref, y_ref, st_ref, carry):
        @pl.when(pl.program_id(0) == 0)
        def _():
            carry[...] = jnp.zeros_like(carry)

        u = u_ref[...]
        bu = _nn(u, bbd_ref[...])
        ar, ai = a_ref[0:1, :], a_ref[1:2, :]
        cr, ci = carry[0:1, :S5_N], carry[0:1, S5_N:]
        rid = _rows((t, S5_N))
        first = rid == 0
        xr = bu[:, :S5_N] + jnp.where(first, ar * cr - ai * ci, 0.0)
        xi = bu[:, S5_N:] + jnp.where(first, ar * ci + ai * cr, 0.0)
        pr, pi = ar, ai
        for k in range(_scan_steps(t)):
            d = 1 << k
            keep = rid >= d
            sr = jnp.where(keep, pltpu.roll(xr, d, 0), 0.0)
            si = jnp.where(keep, pltpu.roll(xi, d, 0), 0.0)
            xr, xi = xr + pr * sr - pi * si, xi + pr * si + pi * sr
            pr, pi = pr * pr - pi * pi, 2.0 * pr * pi
        st_ref[:, :S5_N] = xr
        st_ref[:, S5_N:] = xi
        carry[0:1, :S5_N] = xr[t - 1:t, :]
        carry[0:1, S5_N:] = xi[t - 1:t, :]
        x = jnp.concatenate([xr, xi], axis=1)
        y_ref[...] = _s5_out_fn(x, u, cbd_ref[...], d_ref[...], wg_ref[...], bg_ref[...]).astype(BF16)

    full = lambda shape: pl.BlockSpec(shape, lambda i: (0,) * len(shape))
    return _call(body, name="s5_fwd", grid=(s // t,),
                 in_specs=[pl.BlockSpec((t, S5_W), lambda i: (i, 2)), full((8, S5_N)), full((S5_W, 2 * S5_N)),
                           full((S5_W, 2 * S5_N)), full((1, S5_W)), full((S5_W, S5_W)), full((1, S5_W))],
                 out_specs=[pl.BlockSpec((t, S5_W), lambda i: (i, 0)), pl.BlockSpec((t, 2 * S5_N), lambda i: (i, 0))],
                 out_shape=[jax.ShapeDtypeStruct((s, S5_W), BF16), jax.ShapeDtypeStruct((s, 2 * S5_N), F32)],
                 scratch_shapes=[pltpu.VMEM((8, 2 * S5_N), F32)],
                 compiler_params=_params("arbitrary"))(p, arow, bbd, cbd, drow, wg, bg)


def s5_bwd(p, st, arow, bbd, cbd, drow, wg, bg, dmix):
    s = p.shape[0]
    t = min(T_S5, s)
    nc = s // t

    def body(u_ref, st_ref, prev_ref, a_ref, bbd_ref, cbd_ref, d_ref, wg_ref, bg_ref, dy_ref,
             du_ref, da_ref, dbbd_ref, dcbd_ref, dd_ref, dwg_ref, dbg_ref, carry):
        i = pl.program_id(0)

        @pl.when(i == 0)
        def _():
            carry[...] = jnp.zeros_like(carry)
            for r in (da_ref, dbbd_ref, dcbd_ref, dd_ref, dwg_ref, dbg_ref):
                r[...] = jnp.zeros_like(r)

        u = u_ref[...]
        x = st_ref[...]
        _, vjp = jax.vjp(_s5_out_fn, x, u, cbd_ref[...], d_ref[...], wg_ref[...], bg_ref[...])
        dx, du1, dcbd, dd, dwg, dbg = vjp(dy_ref[...])
        ar, ai = a_ref[0:1, :], a_ref[1:2, :]
        cr, ci = carry[0:1, :S5_N], carry[0:1, S5_N:]
        rid = _rows((t, S5_N))
        last = rid == t - 1
        gr = dx[:, :S5_N] + jnp.where(last, ar * cr + ai * ci, 0.0)
        gi = dx[:, S5_N:] + jnp.where(last, ar * ci - ai * cr, 0.0)
        pr, pi = ar, -ai
        for k in range(_scan_steps(t)):
            d = 1 << k
            keep = rid < t - d
            sr = jnp.where(keep, pltpu.roll(gr, t - d, 0), 0.0)
            si = jnp.where(keep, pltpu.roll(gi, t - d, 0), 0.0)
            gr, gi = gr + pr * sr - pi * si, gi + pr * si + pi * sr
            pr, pi = pr * pr - pi * pi, 2.0 * pr * pi
        carry[0:1, :S5_N] = gr[0:1, :]
        carry[0:1, S5_N:] = gi[0:1, :]
        has_prev = (i < nc - 1).astype(F32)
        top_r = prev_ref[7:8, :S5_N] * has_prev
        top_i = prev_ref[7:8, S5_N:] * has_prev
        xpr = jnp.where(rid == 0, top_r, pltpu.roll(x[:, :S5_N], 1, 0))
        xpi = jnp.where(rid == 0, top_i, pltpu.roll(x[:, S5_N:], 1, 0))
        da_ref[0:1, :] += jnp.sum(xpr * gr + xpi * gi, axis=0, keepdims=True)
        da_ref[1:2, :] += jnp.sum(xpr * gi - xpi * gr, axis=0, keepdims=True)
        g = jnp.concatenate([gr, gi], axis=1)
        dbbd_ref[...] += _tn(u, g)
        du_ref[...] = (_nt(g, bbd_ref[...]) + du1).astype(BF16)
        dcbd_ref[...] += dcbd
        dd_ref[...] += dd
        dwg_ref[...] += dwg
        dbg_ref[...] += dbg

    full = lambda shape: pl.BlockSpec(shape, lambda i: (0,) * len(shape))
    rev = lambda col: (lambda i: (nc - 1 - i, col))
    prev_map = lambda i: (jnp.maximum((nc - 1 - i) * (t // 8) - 1, 0), 0)
    return _call(body, name="s5_bwd", grid=(nc,),
                 in_specs=[pl.BlockSpec((t, S5_W), rev(2)), pl.BlockSpec((t, 2 * S5_N), rev(0)),
                           pl.BlockSpec((8, 2 * S5_N), prev_map), full((8, S5_N)), full((S5_W, 2 * S5_N)),
                           full((S5_W, 2 * S5_N)), full((1, S5_W)), full((S5_W, S5_W)), full((1, S5_W)),
                           pl.BlockSpec((t, S5_W), rev(1))],
                 out_specs=[pl.BlockSpec((t, S5_W), rev(0)), full((8, S5_N)), full((S5_W, 2 * S5_N)),
                            full((S5_W, 2 * S5_N)), full((1, S5_W)), full((S5_W, S5_W)), full((1, S5_W))],
                 out_shape=[jax.ShapeDtypeStruct((s, S5_W), BF16), jax.ShapeDtypeStruct((8, S5_N), F32),
                            jax.ShapeDtypeStruct((S5_W, 2 * S5_N), F32), jax.ShapeDtypeStruct((S5_W, 2 * S5_N), F32),
                            jax.ShapeDtypeStruct((1, S5_W), F32), jax.ShapeDtypeStruct((S5_W, S5_W), F32),
                            jax.ShapeDtypeStruct((1, S5_W), F32)],
                 scratch_shapes=[pltpu.VMEM((8, 2 * S5_N), F32)],
                 compiler_params=_params("arbitrary"))(p, st, st, arow, bbd, cbd, drow, wg, bg, dmix)


def _cum_steps(s):
    return int(math.ceil(math.log2(s)))


def fox_prep_fwd(p, bf):
    s = p.shape[0]

    def body(f_ref, bf_ref, fc_ref, fr_ref):
        lf = jax.nn.log_sigmoid(f_ref[...] + bf_ref[...])
        acc = jnp.where(_lanes((s, 128)) < FX_H, lf, 0.0)
        rid = _rows((s, 128))
        for k in range(_cum_steps(s)):
            d = 1 << k
            acc = acc + jnp.where(rid >= d, pltpu.roll(acc, d, 0), 0.0)
        low = _lanes((s, 128)) < HD
        for j in range(FX_H // 2):
            fc_ref[:, j * 128:(j + 1) * 128] = jnp.where(low, acc[:, 2 * j:2 * j + 1], acc[:, 2 * j + 1:2 * j + 2])
        fr_ref[...] = jnp.transpose(acc)[0:FX_H, :]

    return _call(body, name="fox_prep_fwd", grid=(1,),
                 in_specs=[pl.BlockSpec((s, 128), lambda i: (0, FF_COL // 128)), pl.BlockSpec((1, 128), lambda i: (0, 0))],
                 out_specs=[pl.BlockSpec((s, FX_W), lambda i: (0, 0)), pl.BlockSpec((FX_H, s), lambda i: (0, 0))],
                 out_shape=[jax.ShapeDtypeStruct((s, FX_W), F32), jax.ShapeDtypeStruct((FX_H, s), F32)],
                 compiler_params=_params("arbitrary"))(p, bf)


def fox_prep_bwd(p, bf, dfq, dfk_t):
    s = p.shape[0]

    def body(f_ref, bf_ref, dfq_ref, dfk_ref, df_ref, dbf_ref):
        lane = _lanes((s, 128))
        acc = jnp.transpose(dfk_ref[...])
        for h in range(FX_H):
            acc = acc + jnp.where(lane == h, dfq_ref[:, h * HD:h * HD + 1], 0.0)
        acc = jnp.where(lane < FX_H, acc, 0.0)
        rid = _rows((s, 128))
        for k in range(_cum_steps(s)):
            d = 1 << k
            acc = acc + jnp.where(rid < s - d, pltpu.roll(acc, s - d, 0), 0.0)
        z = f_ref[...] + bf_ref[...]
        df = jnp.where(lane < FX_H, acc * jax.nn.sigmoid(-z), 0.0)
        df_ref[...] = df.astype(BF16)
        dbf_ref[...] = jnp.sum(df, axis=0, keepdims=True)

    return _call(body, name="fox_prep_bwd", grid=(1,),
                 in_specs=[pl.BlockSpec((s, 128), lambda i: (0, FF_COL // 128)), pl.BlockSpec((1, 128), lambda i: (0, 0)),
                           pl.BlockSpec((s, FX_W), lambda i: (0, 0)), pl.BlockSpec((128, s), lambda i: (0, 0))],
                 out_specs=[pl.BlockSpec((s, 128), lambda i: (0, 0)), pl.BlockSpec((1, 128), lambda i: (0, 0))],
                 out_shape=[jax.ShapeDtypeStruct((s, 128), BF16), jax.ShapeDtypeStruct((1, 128), F32)],
                 compiler_params=_params("arbitrary"))(p, bf, dfq, dfk_t)


Q_BLK = (2 * GM_W + S5_W) // 128
K_BLK = Q_BLK + FX_W // 128
V_BLK = K_BLK + FX_W // 128


def attn_fwd(p, fcol, frow):
    s = p.shape[0]
    tq = min(TQ, s)
    nq = s // tq

    def body(q_ref, k_ref, v_ref, fc_ref, fr_ref, o_ref, lse_ref):
        hp = pl.program_id(0)
        qi = pl.program_id(1)
        causal = _rows((tq, tq)) >= _lanes((tq, tq))
        for hh in range(2):
            sl = slice(hh * HD, (hh + 1) * HD)
            q = (q_ref[:, sl] * (HD ** -0.5)).astype(BF16)
            fq = fc_ref[:, hh * HD:hh * HD + 1]
            head = 2 * hp + hh

            def scores(kj):
                off = pl.multiple_of(kj * tq, tq)
                k = k_ref[pl.ds(off, tq), sl]
                v = v_ref[pl.ds(off, tq), sl]
                fk = fr_ref[pl.ds(head, 1), pl.ds(off, tq)]
                return _nt(q, k) + fq - fk, v

            def update(carry, sc, v):
                m, l, acc = carry
                m_new = jnp.maximum(m, jnp.max(sc, axis=-1, keepdims=True))
                alpha = jnp.exp(m - m_new)
                pr = jnp.exp(sc - m_new)
                return m_new, alpha * l + jnp.sum(pr, axis=-1, keepdims=True), alpha * acc + _nn(pr, v)

            def step(kj, carry):
                sc, v = scores(kj)
                return update(carry, sc, v)

            init = (jnp.full((tq, 1), NEG_INF, F32), jnp.zeros((tq, 1), F32), jnp.zeros((tq, HD), F32))
            carry = lax.fori_loop(0, qi, step, init)
            sc, v = scores(qi)
            m, l, acc = update(carry, jnp.where(causal, sc, NEG_INF), v)
            o_ref[:, sl] = (acc / l).astype(BF16)
            lse_ref[:, sl] = jnp.broadcast_to(m + jnp.log(l), (tq, HD))

    return _call(body, name="attn_fwd", grid=(FX_H // 2, nq),
                 in_specs=[pl.BlockSpec((tq, 128), lambda h, i: (i, Q_BLK + h)),
                           pl.BlockSpec((s, 128), lambda h, i: (0, K_BLK + h)),
                           pl.BlockSpec((s, 128), lambda h, i: (0, V_BLK + h)),
                           pl.BlockSpec((tq, 128), lambda h, i: (i, h)),
                           pl.BlockSpec((FX_H, s), lambda h, i: (0, 0))],
                 out_specs=[pl.BlockSpec((tq, 128), lambda h, i: (i, h)), pl.BlockSpec((tq, 128), lambda h, i: (i, h))],
                 out_shape=[jax.ShapeDtypeStruct((s, FX_W), BF16), jax.ShapeDtypeStruct((s, FX_W), F32)],
                 compiler_params=_params("arbitrary", "arbitrary"))(p, p, p, fcol, frow)


def attn_bwd(p, fcol, frow, o, lse, dmix):
    s = p.shape[0]
    tq = min(TQ, s)
    nq = s // tq
    scale = HD ** -0.5

    def body(q_ref, k_ref, v_ref, fc_ref, fr_ref, o_ref, lse_ref, do_ref,
             dq_ref, dk_ref, dv_ref, dfq_ref, dfk_ref, delta):
        hp = pl.program_id(0)
        kj = pl.program_id(1)
        causal = _rows((tq, tq)) >= _lanes((tq, tq))

        @pl.when(kj == 0)
        def _():
            dq_ref[...] = jnp.zeros_like(dq_ref)
            dfq_ref[...] = jnp.zeros_like(dfq_ref)
            delta[...] = o_ref[...].astype(F32) * do_ref[...]

        dfk_ref[...] = jnp.zeros_like(dfk_ref)
        koff = pl.multiple_of(kj * tq, tq)
        for hh in range(2):
            sl = slice(hh * HD, (hh + 1) * HD)
            head = 2 * hp + hh
            k = k_ref[pl.ds(koff, tq), sl].astype(BF16)
            v = v_ref[pl.ds(koff, tq), sl].astype(BF16)
            fk = fr_ref[pl.ds(head, 1), pl.ds(koff, tq)]

            def tile(qi, carry, masked):
                dk, dv, dfk = carry
                off = pl.multiple_of(qi * tq, tq)
                rows = pl.ds(off, tq)
                q = (q_ref[rows, sl] * scale).astype(BF16)
                do = do_ref[rows, sl].astype(BF16)
                sc = _nt(q, k) + fc_ref[rows, hh * HD:hh * HD + 1] - fk
                if masked:
                    sc = jnp.where(causal, sc, NEG_INF)
                pr = jnp.exp(sc - lse_ref[rows, hh * HD:hh * HD + 1])
                dl = jnp.sum(delta[rows, sl], axis=-1, keepdims=True)
                ds = pr * (_nt(do, v) - dl)
                dsb = ds.astype(BF16)
                dq_ref[rows, sl] += _nn(dsb, k) * scale
                dfq_ref[rows, sl] += jnp.broadcast_to(jnp.sum(ds, axis=-1, keepdims=True), (tq, HD))
                return (dk + _tn(dsb, q), dv + _tn(pr, do), dfk - jnp.sum(ds, axis=0, keepdims=True))

            init = (jnp.zeros((tq, HD), F32), jnp.zeros((tq, HD), F32), jnp.zeros((1, tq), F32))
            carry = tile(kj, init, True)
            dk, dv, dfk = lax.fori_loop(kj + 1, nq, lambda qi, c: tile(qi, c, False), carry)
            dk_ref[:, sl] = dk.astype(BF16)
            dv_ref[:, sl] = dv.astype(BF16)
            dfk_ref[0, hh:hh + 1, :] = dfk

    seq = lambda blk: pl.BlockSpec((s, 128), lambda h, j: (0, blk + h))
    return _call(body, name="attn_bwd", grid=(FX_H // 2, nq),
                 in_specs=[seq(Q_BLK), seq(K_BLK), seq(V_BLK), seq(0), pl.BlockSpec((FX_H, s), lambda h, j: (0, 0)),
                           seq(0), seq(0), seq(4)],
                 out_specs=[seq(0), pl.BlockSpec((tq, 128), lambda h, j: (j, h)), pl.BlockSpec((tq, 128), lambda h, j: (j, h)),
                            seq(0), pl.BlockSpec((1, 8, tq), lambda h, j: (h, 0, j))],
                 out_shape=[jax.ShapeDtypeStruct((s, FX_W), F32), jax.ShapeDtypeStruct((s, FX_W), BF16),
                            jax.ShapeDtypeStruct((s, FX_W), BF16), jax.ShapeDtypeStruct((s, FX_W), F32),
                            jax.ShapeDtypeStruct((FX_H // 2, 8, s), F32)],
                 scratch_shapes=[pltpu.VMEM((s, 128), F32)],
                 compiler_params=_params("arbitrary", "arbitrary"))(p, p, p, fcol, frow, o, lse, dmix)


def _shift_down(a, prev8, k):
    r = pltpu.roll(a, k, 0)
    top = jnp.where(_rows(prev8.shape) < k, pltpu.roll(prev8, k, 0), r[0:8])
    return jnp.concatenate([top, r[8:]], axis=0)


def _shift_up(a, next8, k):
    t = a.shape[0]
    r = pltpu.roll(a, t - k, 0)
    bot = jnp.where(_rows(next8.shape) >= 8 - k, pltpu.roll(next8, 8 - k, 0), r[t - 8:t])
    return jnp.concatenate([r[:t - 8], bot], axis=0)


def _conv(a, prev8, cw, cb):
    return cb + cw[0:1, :] * _shift_down(a, prev8, 2) + cw[1:2, :] * _shift_down(a, prev8, 1) + cw[2:3, :] * a


def conv_fwd(up, cw, cb):
    s = up.shape[0]
    ts = min(TS_C, s)

    def body(a_ref, g_ref, cw_ref, cb_ref, o_ref, halo):
        @pl.when(pl.program_id(0) == 0)
        def _():
            halo[...] = jnp.zeros_like(halo)

        a = a_ref[...]
        c = _conv(a, halo[...], cw_ref[...], cb_ref[...])
        o_ref[...] = (jax.nn.gelu(c) * g_ref[...]).astype(BF16)
        halo[...] = a[ts - 8:ts, :]

    return _call(body, name="conv_fwd", grid=(s // ts,),
                 in_specs=[pl.BlockSpec((ts, DFF), lambda i: (i, 0)), pl.BlockSpec((ts, DFF), lambda i: (i, 1)),
                           pl.BlockSpec((3, DFF), lambda i: (0, 0)), pl.BlockSpec((1, DFF), lambda i: (0, 0))],
                 out_specs=pl.BlockSpec((ts, DFF), lambda i: (i, 0)),
                 out_shape=jax.ShapeDtypeStruct((s, DFF), BF16),
                 scratch_shapes=[pltpu.VMEM((8, DFF), F32)],
                 compiler_params=_params("arbitrary"))(up, up, cw, cb)


def conv_bwd(up, cw, cb, dact):
    s = up.shape[0]
    ts = min(TS_C, s)
    ns = s // ts

    def body(a_ref, g_ref, prev_ref, cw_ref, cb_ref, dact_ref, dup_ref, dcw_ref, dcb_ref, halo):
        i = pl.program_id(0)

        @pl.when(i == 0)
        def _():
            halo[...] = jnp.zeros_like(halo)
            dcw_ref[...] = jnp.zeros_like(dcw_ref)
            dcb_ref[...] = jnp.zeros_like(dcb_ref)

        a = a_ref[...]
        g = g_ref[...]
        cw = cw_ref[...]
        prev8 = prev_ref[...] * (i < ns - 1).astype(F32)
        c = _conv(a, prev8, cw, cb_ref[...])
        gel, vjp = jax.vjp(jax.nn.gelu, c)
        dact = dact_ref[...]
        (dc,) = vjp(dact * g)
        up1 = _shift_up(dc, halo[...], 1)
        up2 = _shift_up(dc, halo[...], 2)
        da = cw[2:3, :] * dc + cw[1:2, :] * up1 + cw[0:1, :] * up2
        dup_ref[:, :DFF] = da.astype(BF16)
        dup_ref[:, DFF:] = (dact * gel).astype(BF16)
        dcw_ref[0:1, :] += jnp.sum(a * up2, axis=0, keepdims=True)
        dcw_ref[1:2, :] += jnp.sum(a * up1, axis=0, keepdims=True)
        dcw_ref[2:3, :] += jnp.sum(a * dc, axis=0, keepdims=True)
        dcb_ref[...] += jnp.sum(dc, axis=0, keepdims=True)
        halo[...] = dc[0:8, :]

    rev = lambda col: (lambda i: (ns - 1 - i, col))
    prev_map = lambda i: (jnp.maximum((ns - 1 - i) * (ts // 8) - 1, 0), 0)
    return _call(body, name="conv_bwd", grid=(ns,),
                 in_specs=[pl.BlockSpec((ts, DFF), rev(0)), pl.BlockSpec((ts, DFF), rev(1)),
                           pl.BlockSpec((8, DFF), prev_map), pl.BlockSpec((3, DFF), lambda i: (0, 0)),
                           pl.BlockSpec((1, DFF), lambda i: (0, 0)), pl.BlockSpec((ts, DFF), rev(0))],
                 out_specs=[pl.BlockSpec((ts, 2 * DFF), rev(0)), pl.BlockSpec((3, DFF), lambda i: (0, 0)),
                            pl.BlockSpec((1, DFF), lambda i: (0, 0))],
                 out_shape=[jax.ShapeDtypeStruct((s, 2 * DFF), BF16), jax.ShapeDtypeStruct((3, DFF), F32),
                            jax.ShapeDtypeStruct((1, DFF), F32)],
                 scratch_shapes=[pltpu.VMEM((8, DFF), F32)],
                 compiler_params=_params("arbitrary"))(up, up, up, cw, cb, dact)


def _blockdiag_expand(m):
    m4 = m.reshape(S5_H, 2, S5_G, S5_P)
    eye = jnp.eye(S5_G, dtype=bool)[:, None, None, :, None]
    return jnp.where(eye, m4[None], 0.0).reshape(S5_W, 2 * S5_N)


def _blockdiag_extract(mbd):
    m5 = mbd.reshape(S5_G, S5_H, 2, S5_G, S5_P)
    diag = jnp.stack([m5[g, :, :, g, :] for g in range(S5_G)], axis=2)
    return diag.reshape(S5_H, 2 * S5_N)


def _c_expand(c_re, c_im):
    c4 = jnp.stack([c_re, c_im], axis=2)
    eye = jnp.eye(S5_G, dtype=bool)[:, None, None, :, None]
    return jnp.where(eye, c4[:, :, :, None, :], 0.0).reshape(S5_W, 2 * S5_N)


def _c_extract(cbd):
    m5 = cbd.reshape(S5_G, S5_H, 2, S5_G, S5_P)
    d = jnp.stack([m5[g, :, :, g, :] for g in range(S5_G)], axis=0)
    return d[:, :, 0, :], d[:, :, 1, :]


def _glu_expand(w):
    eye = jnp.eye(S5_G, dtype=bool)[:, None, :, None]
    return jnp.where(eye, w[:, :, None, :], 0.0).reshape(S5_W, S5_W)


def _glu_extract(wbd):
    m4 = wbd.reshape(S5_G, S5_H, S5_G, S5_H)
    return jnp.stack([m4[g, :, g, :] for g in range(S5_G)], axis=0)


SMALL = ("b_f", "gm_ln_g", "gm_ln_b", "gm_w_s", "gm_b_s", "s5_lam_re", "s5_lam_im", "s5_log_dt", "s5_b_re", "s5_b_im",
         "s5_c_re", "s5_c_im", "s5_d", "s5_w_glu", "s5_b_glu", "ln1_g", "ln1_b", "conv_b", "ln2_g", "ln2_b")


def _layer_operands(sp, l):
    f = {}
    f["bf"] = jnp.pad(sp["b_f"][l][None, :], ((0, 0), (0, 128 - FX_H)))
    f["gm_lg"] = sp["gm_ln_g"][l].reshape(1, GM_W)
    f["gm_lb"] = sp["gm_ln_b"][l].reshape(1, GM_W)
    f["gm_ws"] = sp["gm_w_s"][l]
    f["gm_bst"] = sp["gm_b_s"][l].T
    f["lr"] = sp["s5_lam_re"][l].reshape(1, S5_N)
    f["li"] = sp["s5_lam_im"][l].reshape(1, S5_N)
    f["ldt"] = jnp.repeat(sp["s5_log_dt"][l], S5_P).reshape(1, S5_N)
    bt = lambda b: jnp.transpose(b, (2, 0, 1)).reshape(S5_H, S5_N)
    f["bt"] = jnp.concatenate([bt(sp["s5_b_re"][l]), bt(sp["s5_b_im"][l])], axis=1)
    f["cbd"] = _c_expand(sp["s5_c_re"][l], sp["s5_c_im"][l])
    f["drow"] = sp["s5_d"][l].reshape(1, S5_W)
    f["wg"] = _glu_expand(sp["s5_w_glu"][l])
    f["bg"] = sp["s5_b_glu"][l].reshape(1, S5_W)
    for n in ("ln1_g", "ln1_b", "ln2_g", "ln2_b"):
        f[n] = sp[n][l][None, :]
    f["cb"] = sp["conv_b"][l][None, :]
    return f


def local_step(x, target, mods, big, sp):
    saved = []
    for l in range(DEPTH):
        f = _layer_operands(sp, l)
        w = big[l]
        mod = mods[l]
        p = mm_nn(x, w["w_in"], NP, "in_proj", mod=mod, rows=(0, 1))
        ygm = gm_fwd(p, f["gm_lg"], f["gm_lb"], f["gm_ws"], f["gm_bst"])
        arow, bbt = s5_prep_fwd(f["lr"], f["li"], f["ldt"], f["bt"])
        bbd = _blockdiag_expand(bbt)
        ys5, st = s5_fwd(p, arow, bbd, f["cbd"], f["drow"], f["wg"], f["bg"])
        fcol, frow = fox_prep_fwd(p, f["bf"])
        yfx, lse = attn_fwd(p, fcol, frow)
        mixcat = jnp.concatenate([ygm, ys5, yfx], axis=1)
        mix = mm_nn(mixcat, w["w_out"], D, "out_proj")
        x1 = post_fwd(x, mix, mod, 2, f["ln1_g"], f["ln1_b"], "post1_fwd")
        up = mm_nn(x1, w["w_up"], DFF // 2, "up_proj", mod=mod, rows=(3, 4))
        act = conv_fwd(up, w["conv_w"], f["cb"])
        ffn = mm_nn(act, w["w_down"], D, "down_proj")
        x2 = post_fwd(x1, ffn, mod, 5, f["ln2_g"], f["ln2_b"], "post2_fwd")
        saved.append(dict(f=f, x=x, p=p, arow=arow, bbd=bbd, st=st, fcol=fcol, frow=frow, yfx=yfx, lse=lse,
                          mixcat=mixcat, mix=mix, x1=x1, up=up, act=act, ffn=ffn))
        x = x2

    loss_tile, dx = loss_kernel(x, target)

    gbig = [None] * DEPTH
    gsm = {n: [None] * DEPTH for n in SMALL}
    dmods = [None] * DEPTH
    for l in reversed(range(DEPTH)):
        sv = saved[l]
        f = sv["f"]
        w = big[l]
        mod = mods[l]
        s = x.shape[0]
        dx1, dffn, dg2, dlg2, dlb2 = post_bwd(sv["x1"], sv["ffn"], mod, 5, f["ln2_g"], f["ln2_b"], dx, "post2_bwd")
        g_down = mm_tn(sv["act"], dffn, D // 2, "down_dw")
        dact = mm_nt(dffn, w["w_down"], "down_dx")
        dup, dcw, dcb = conv_bwd(sv["up"], w["conv_w"], f["cb"], dact)
        g_up = mm_tn(sv["x1"], dup, DFF // 2, "up_dw", mod=mod, rows=(3, 4))
        dx1, dsh2, dsc2 = mm_nt_mod(dup, w["w_up"], sv["x1"], dx1, mod, (3, 4), "up_dx")
        dx0, dmix, dg1, dlg1, dlb1 = post_bwd(sv["x"], sv["mix"], mod, 2, f["ln1_g"], f["ln1_b"], dx1, "post1_bwd")
        g_out = mm_tn(sv["mixcat"], dmix, D, "out_dw")
        dmc = mm_nt(dmix, w["w_out"], "out_dx")
        duv, dgm_lg, dgm_lb, dgm_ws, dgm_bst = gm_bwd(sv["p"], f["gm_lg"], f["gm_lb"], f["gm_ws"], f["gm_bst"], dmc)
        du5, da, dbbd, dcbd, dd5, dwg, dbg = s5_bwd(sv["p"], sv["st"], sv["arow"], sv["bbd"], f["cbd"], f["drow"],
                                                    f["wg"], f["bg"], dmc)
        dlr, dli, dldt, dbt = s5_prep_bwd(f["lr"], f["li"], f["ldt"], f["bt"], da, _blockdiag_extract(dbbd))
        dq, dk, dv, dfq, dfk = attn_bwd(sv["p"], sv["fcol"], sv["frow"], sv["yfx"], sv["lse"], dmc)
        dfk_t = jnp.pad(dfk[:, 0:2, :].reshape(FX_H, s), ((0, 128 - FX_H), (0, 0)))
        dff, dbf = fox_prep_bwd(sv["p"], f["bf"], dfq, dfk_t)
        dp = jnp.concatenate([duv, du5, dq.astype(BF16), dk, dv, dff], axis=1)
        g_in = mm_tn(sv["x"], dp, NP, "in_dw", mod=mod, rows=(0, 1))
        dx, dsh1, dsc1 = mm_nt_mod(dp, w["w_in"], sv["x"], dx0, mod, (0, 1), "in_dx")

        gbig[l] = dict(w_in=g_in, w_out=g_out, w_up=g_up, w_down=g_down, conv_w=dcw)
        dmods[l] = jnp.concatenate([dsh1, dsc1, dg1, dsh2, dsc2, dg2], axis=0)
        dc_re, dc_im = _c_extract(dcbd)
        dbt4 = dbt.reshape(S5_H, 2, S5_G, S5_P)
        vals = dict(b_f=dbf[0, :FX_H], gm_ln_g=dgm_lg.reshape(GM_H, HD), gm_ln_b=dgm_lb.reshape(GM_H, HD),
                    gm_w_s=dgm_ws, gm_b_s=dgm_bst.T, s5_lam_re=dlr.reshape(S5_G, S5_P),
                    s5_lam_im=dli.reshape(S5_G, S5_P), s5_log_dt=dldt[0, :S5_G],
                    s5_b_re=jnp.transpose(dbt4[:, 0], (1, 2, 0)), s5_b_im=jnp.transpose(dbt4[:, 1], (1, 2, 0)),
                    s5_c_re=dc_re, s5_c_im=dc_im, s5_d=dd5.reshape(S5_G, S5_H), s5_w_glu=_glu_extract(dwg),
                    s5_b_glu=dbg.reshape(S5_G, S5_H), ln1_g=dlg1[0], ln1_b=dlb1[0], conv_b=dcb[0],
                    ln2_g=dlg2[0], ln2_b=dlb2[0])
        for n in SMALL:
            gsm[n][l] = vals[n]
    gsm = {n: jnp.stack(v) for n, v in gsm.items()}
    return loss_tile, dx, gbig, gsm, jnp.stack(dmods)


def _my_index():
    return 4 * lax.axis_index("x") + 2 * lax.axis_index("y") + lax.axis_index("c")


def exchange(tensors, scatter, name):
    n = len(tensors)

    def body(*refs):
        ins, outs = refs[:n], refs[n:2 * n]
        send_sems, recv_sems, local_sems = refs[2 * n:]
        x, y, c = lax.axis_index("x"), lax.axis_index("y"), lax.axis_index("c")
        me = 4 * x + 2 * y + c
        local = []
        for t in range(n):
            cp = pltpu.make_async_copy(ins[t].at[me] if scatter else ins[t], outs[t].at[me], local_sems.at[t])
            cp.start()
            local.append(cp)
        remote = []
        for m in range(1, NDEV):
            px = 1 - x if m & 4 else x
            py = 1 - y if m & 2 else y
            pc = 1 - c if m & 1 else c
            peer = 4 * px + 2 * py + pc
            for t in range(n):
                k = t * (NDEV - 1) + m - 1
                cp = pltpu.make_async_remote_copy(
                    src_ref=ins[t].at[peer] if scatter else ins[t], dst_ref=outs[t].at[me],
                    send_sem=send_sems.at[k], recv_sem=recv_sems.at[k],
                    device_id=(px, py, pc), device_id_type=MESH_IDS)
                cp.start()
                remote.append(cp)
        for cp in remote:
            cp.wait()
        for cp in local:
            cp.wait()

    hbm = pl.BlockSpec(memory_space=pltpu.HBM)
    out_shape = [jax.ShapeDtypeStruct(t.shape if scatter else (NDEV,) + t.shape, t.dtype) for t in tensors]
    return _call(body, name=name, in_specs=[hbm] * n, out_specs=[hbm] * n, out_shape=out_shape,
                 scratch_shapes=[pltpu.SemaphoreType.DMA((n * (NDEV - 1),)), pltpu.SemaphoreType.DMA((n * (NDEV - 1),)),
                                 pltpu.SemaphoreType.DMA((n,))])(*tensors)


def mod_slices(c_all, w_ada, b_loc):
    nl, _, nc = w_ada.shape

    def body(c_ref, w_ref, b_ref, o_ref):
        cv = c_ref[...]
        o_ref[0] = _nn(cv * jax.nn.sigmoid(cv), w_ref[0]) + b_ref[0]

    return _call(body, name="mod_slices", grid=(nl,),
                 in_specs=[pl.BlockSpec((NDEV, D), lambda l: (0, 0)), pl.BlockSpec((1, D, nc), lambda l: (l, 0, 0)),
                           pl.BlockSpec((1, 1, nc), lambda l: (l, 0, 0))],
                 out_specs=pl.BlockSpec((1, NDEV, nc), lambda l: (l, 0, 0)),
                 out_shape=jax.ShapeDtypeStruct((nl, NDEV, nc), F32),
                 compiler_params=_params("arbitrary"))(c_all, w_ada, b_loc.reshape(nl, 1, nc))


def ada_grad(c_all, dm_loc):
    nl, _, nc = dm_loc.shape

    def body(c_ref, d_ref, o_ref):
        cv = c_ref[...]
        o_ref[0] = _tn(cv * jax.nn.sigmoid(cv), d_ref[0])

    return _call(body, name="ada_grad", grid=(nl,),
                 in_specs=[pl.BlockSpec((NDEV, D), lambda l: (0, 0)), pl.BlockSpec((1, NDEV, nc), lambda l: (l, 0, 0))],
                 out_specs=pl.BlockSpec((1, D, nc), lambda l: (l, 0, 0)),
                 out_shape=jax.ShapeDtypeStruct((nl, D, nc), F32),
                 compiler_params=_params("arbitrary"))(c_all, dm_loc)


def sum_chunks(chunks):
    r = chunks.shape[1]

    def body(c_ref, o_ref):
        acc = c_ref[0]
        for i in range(1, NDEV):
            acc = acc + c_ref[i]
        o_ref[...] = acc

    return _call(body, name="sum_chunks", out_shape=jax.ShapeDtypeStruct((r, 128), F32))(chunks)


def _row_tile(r):
    if r <= 256:
        return r
    for t in range(256, 7, -8):
        if r % t == 0:
            return t
    return r


def adamw(w, m, v, g=None, chunks=None, name="adamw"):
    r, cdim = w.shape
    tr = _row_tile(r)
    bc1 = 1.0 - ADAM_B1 ** ADAM_STEP
    bc2 = 1.0 - ADAM_B2 ** ADAM_STEP

    def body(g_ref, w_ref, m_ref, v_ref, go_ref, d_ref, mo_ref, vo_ref):
        if chunks is None:
            grad = g_ref[...]
        else:
            grad = g_ref[0].astype(F32)
            for i in range(1, NDEV):
                grad = grad + g_ref[i].astype(F32)
        mn = ADAM_B1 * m_ref[...] + (1.0 - ADAM_B1) * grad
        vn = ADAM_B2 * v_ref[...] + (1.0 - ADAM_B2) * (grad * grad)
        m_hat = mn / bc1
        v_hat = vn / bc2
        go_ref[...] = grad
        d_ref[...] = -ADAM_LR * (m_hat / (jnp.sqrt(v_hat) + ADAM_EPS) + ADAM_WD * w_ref[...])
        mo_ref[...] = mn
        vo_ref[...] = vn

    tile = pl.BlockSpec((tr, cdim), lambda i: (i, 0))
    gspec = tile if chunks is None else pl.BlockSpec((NDEV, tr, cdim), lambda i: (0, i, 0))
    shp = jax.ShapeDtypeStruct((r, cdim), F32)
    return _call(body, name=name, grid=(r // tr,), in_specs=[gspec, tile, tile, tile],
                 out_specs=[tile] * 4, out_shape=[shp] * 4,
                 compiler_params=_params("arbitrary"))(g if chunks is None else chunks, w, m, v)


WEIGHTS = ("w_ada", "b_ada", "w_in", "b_f", "gm_ln_g", "gm_ln_b", "gm_w_s", "gm_b_s", "s5_lam_re", "s5_lam_im",
           "s5_log_dt", "s5_b_re", "s5_b_im", "s5_c_re", "s5_c_im", "s5_d", "s5_w_glu", "s5_b_glu", "w_out", "ln1_g",
           "ln1_b", "w_up", "conv_w", "conv_b", "w_down", "ln2_g", "ln2_b")
SHARDED = ("w_in", "w_out", "w_up", "w_down", "conv_w")
COL_SHARDED = ("w_in", "w_up", "conv_w")
PACKED = ("b_ada",) + SMALL
PACK_ALIGN = NDEV * 8 * 128


def _gather_cols(g):
    nd, nl, r, c = g.shape
    return jnp.transpose(g, (1, 2, 0, 3)).reshape(nl, r, nd * c)


def _gather_rows(g):
    nd, nl, r, c = g.shape
    return jnp.transpose(g, (1, 0, 2, 3)).reshape(nl, nd * r, c)


def _chunk_cols(g):
    nl, r, c8 = g.shape
    return jnp.transpose(g.reshape(nl, r, NDEV, c8 // NDEV), (2, 0, 1, 3))


def _chunk_rows(g):
    nl, r8, c = g.shape
    return jnp.transpose(g.reshape(nl, NDEV, r8 // NDEV, c), (1, 0, 2, 3))


def _pack(parts):
    flat = jnp.concatenate([parts[n].reshape(-1) for n in PACKED])
    return jnp.pad(flat, (0, -flat.shape[0] % PACK_ALIGN))


def _unpack(flat, shapes):
    out, off = {}, 0
    for n in PACKED:
        size = math.prod(shapes[n])
        out[n] = flat[off:off + size].reshape(shapes[n])
        off += size
    return out


def kernel(x, c, w_ada, b_ada, w_in, b_f, gm_ln_g, gm_ln_b, gm_w_s, gm_b_s, s5_lam_re, s5_lam_im, s5_log_dt, s5_b_re, s5_b_im, s5_c_re, s5_c_im, s5_d, s5_w_glu, s5_b_glu, w_out, ln1_g, ln1_b, w_up, conv_w, conv_b, w_down, ln2_g, ln2_b, loss_target, m_w_ada, m_b_ada, m_w_in, m_b_f, m_gm_ln_g, m_gm_ln_b, m_gm_w_s, m_gm_b_s, m_s5_lam_re, m_s5_lam_im, m_s5_log_dt, m_s5_b_re, m_s5_b_im, m_s5_c_re, m_s5_c_im, m_s5_d, m_s5_w_glu, m_s5_b_glu, m_w_out, m_ln1_g, m_ln1_b, m_w_up, m_conv_w, m_conv_b, m_w_down, m_ln2_g, m_ln2_b, v_w_ada, v_b_ada, v_w_in, v_b_f, v_gm_ln_g, v_gm_ln_b, v_gm_w_s, v_gm_b_s, v_s5_lam_re, v_s5_lam_im, v_s5_log_dt, v_s5_b_re, v_s5_b_im, v_s5_c_re, v_s5_c_im, v_s5_d, v_s5_w_glu, v_s5_b_glu, v_w_out, v_ln1_g, v_ln1_b, v_w_up, v_conv_w, v_conv_b, v_w_down, v_ln2_g, v_ln2_b):
    given = dict(locals())
    wts = {n: given[n] for n in WEIGHTS}
    mom = {n: given["m_" + n] for n in WEIGHTS}
    var = {n: given["v_" + n] for n in WEIGHTS}
    nl = w_ada.shape[0]
    me = _my_index()
    ada_cols = w_ada.shape[2]

    (c_all,) = exchange([c], False, "gather_c")
    c_all = c_all.reshape(NDEV, D)
    b_loc = lax.dynamic_slice_in_dim(b_ada, me * ada_cols, ada_cols, axis=1)
    mod_part = mod_slices(c_all, w_ada, b_loc)

    gathered = exchange([mod_part] + [wts[n] if n == "conv_w" else wts[n].astype(BF16) for n in SHARDED],
                        False, "gather_weights")
    mod_mine = lax.dynamic_index_in_dim(gathered[0], me, axis=2, keepdims=False)
    mods = jnp.transpose(mod_mine, (1, 0, 2)).reshape(nl, 6, D)
    mods = jnp.pad(mods, ((0, 0), (0, 2), (0, 0)))
    full = {n: (_gather_cols if n in COL_SHARDED else _gather_rows)(g) for n, g in zip(SHARDED, gathered[1:])}
    full["w_in"] = jnp.pad(full["w_in"], ((0, 0), (0, 0), (0, NP - D_IN)))
    big = [{n: full[n][l] for n in SHARDED} for l in range(nl)]

    loss_tile, gx, gbig, gsm, dmods = local_step(x[0], loss_target[0], mods, big, {n: wts[n] for n in SMALL})

    stacked = {n: jnp.stack([gbig[l][n] for l in range(nl)]) for n in SHARDED}
    stacked["w_in"] = stacked["w_in"][:, :, :D_IN]
    chunked = [(_chunk_cols if n in COL_SHARDED else _chunk_rows)(stacked[n]) for n in SHARDED]
    chunked = [g if n == "conv_w" else g.astype(BF16) for n, g in zip(SHARDED, chunked)]
    gsm["b_ada"] = dmods.reshape(nl, 6 * D)
    packed = _pack(gsm).reshape(NDEV, -1, 128)
    received = exchange(chunked + [packed], True, "scatter_grads")
    small_sum = sum_chunks(received[-1])
    small_all, dmod_all = exchange([small_sum, dmods.reshape(nl, 6 * D)], False, "gather_small")

    out = {}
    for n, chunks in zip(SHARDED, received[:-1]):
        shp = wts[n].shape
        two_d = lambda a: a.reshape(shp[0] * shp[1], shp[2])
        res = adamw(two_d(wts[n]), two_d(mom[n]), two_d(var[n]),
                    chunks=chunks.reshape(NDEV, shp[0] * shp[1], shp[2]), name="adamw_" + n)
        out[n] = [r.reshape(shp) for r in res]

    dm_loc = lax.dynamic_slice_in_dim(dmod_all, me * ada_cols, ada_cols, axis=2)
    g_ada = ada_grad(c_all, jnp.transpose(dm_loc, (1, 0, 2)))
    two_d = lambda a: a.reshape(nl * D, ada_cols)
    res = adamw(two_d(w_ada), two_d(m_w_ada), two_d(v_w_ada), g=two_d(g_ada), name="adamw_w_ada")
    out["w_ada"] = [r.reshape(w_ada.shape) for r in res]

    shapes = {n: wts[n].shape for n in PACKED}
    as_rows = lambda parts: _pack(parts).reshape(-1, 128)
    res = adamw(as_rows(wts), as_rows(mom), as_rows(var), g=small_all.reshape(-1, 128), name="adamw_small")
    unpacked = [_unpack(r.reshape(-1), shapes) for r in res]
    for n in PACKED:
        out[n] = [u[n] for u in unpacked]

    loss = lax.psum(loss_tile[0, 0], ("x", "y", "c"))
    return (loss, gx[None], *[out[n][0] for n in WEIGHTS], *[out[n][1] for n in WEIGHTS],
            *[out[n][2] for n in WEIGHTS], *[out[n][3] for n in WEIGHTS])
```

```python
import functools
import math

import jax
import jax.numpy as jnp
from jax import lax
from jax.experimental import pallas as pl
from jax.experimental.pallas import tpu as pltpu

F32 = jnp.float32
BF16 = jnp.bfloat16
MESH_IDS = pl.DeviceIdType.MESH

D = 1024
SEQ = 4096
DEPTH = 4
NDEV = 8
HD = 64
GM_W = 256
GM_H = 4
GM_C = 128
S5_W = 256
S5_G = 16
S5_H = 16
S5_P = 64
S5_N = S5_G * S5_P
FX_W = 512
FX_H = 8
D_IN = 2 * GM_W + S5_W + 3 * FX_W + FX_H
NP = 2432
FF_COL = 2304
DFF = 2816
LN_EPS = 1e-5
DN_ALPHA = (2.0 * DEPTH) ** 0.25
NEG_INF = -1e30
ADAM_LR = 0.001
ADAM_B1 = 0.9
ADAM_B2 = 0.999
ADAM_EPS = 1e-08
ADAM_WD = 0.01
ADAM_STEP = 10

V7X_VMEM_LIMIT = 56 * 1024 * 1024
TS = 512
TS_C = 256
T_S5 = 256
TQ = 512
ATTN_HEADS = 2


def _call(body, **kw):
    return pl.pallas_call(body, **kw)


def _params(*sem):
    return pltpu.CompilerParams(dimension_semantics=sem if sem else None,
                                vmem_limit_bytes=V7X_VMEM_LIMIT)


def _nn(a, b):
    return jnp.dot(a.astype(BF16), b.astype(BF16), preferred_element_type=F32)


def _nt(a, b):
    return lax.dot_general(a.astype(BF16), b.astype(BF16), (((1,), (1,)), ((), ())),
                           preferred_element_type=F32)


def _tn(a, b):
    return lax.dot_general(a.astype(BF16), b.astype(BF16), (((0,), (0,)), ((), ())),
                           preferred_element_type=F32)


@jax.custom_vjp
def _bdot(a, b):
    return _nn(a, b)


def _bdot_fwd(a, b):
    return _nn(a, b), (a, b)


def _bdot_bwd(res, g):
    a, b = res
    return _nt(g, b), _tn(a, g)


_bdot.defvjp(_bdot_fwd, _bdot_bwd)


@jax.custom_vjp
def _bdot_nt(a, b):
    return _nt(a, b)


def _bdot_nt_fwd(a, b):
    return _nt(a, b), (a, b)


def _bdot_nt_bwd(res, g):
    a, b = res
    return _nn(g, b), _tn(g, a)


_bdot_nt.defvjp(_bdot_nt_fwd, _bdot_nt_bwd)


def _ln(r, g, b):
    mu = jnp.mean(r, axis=-1, keepdims=True)
    xc = r - mu
    var = jnp.mean(xc * xc, axis=-1, keepdims=True)
    return xc * lax.rsqrt(var + LN_EPS) * g + b


def _rows(shape):
    return lax.broadcasted_iota(jnp.int32, shape, 0)


def _lanes(shape):
    return lax.broadcasted_iota(jnp.int32, shape, 1)


def mm_nn(a, w, tn, name, mod=None, rows=None, out_dtype=F32):
    s, k = a.shape
    n = w.shape[1]
    ts = min(TS, s)

    def body(*refs):
        if mod is None:
            a_ref, w_ref, o_ref = refs
            h = a_ref[...]
        else:
            a_ref, m_ref, w_ref, o_ref = refs
            h = a_ref[...] * (1.0 + m_ref[rows[1]:rows[1] + 1, :]) + m_ref[rows[0]:rows[0] + 1, :]
        o_ref[...] = jnp.dot(h.astype(BF16), w_ref[...], preferred_element_type=F32).astype(out_dtype)

    in_specs = [pl.BlockSpec((ts, k), lambda j, i: (i, 0))]
    args = [a]
    if mod is not None:
        in_specs.append(pl.BlockSpec((8, k), lambda j, i: (0, 0)))
        args.append(mod)
    in_specs.append(pl.BlockSpec((k, tn), lambda j, i: (0, j)))
    args.append(w)
    return _call(body, name=name, grid=(n // tn, s // ts), in_specs=in_specs,
                 out_specs=pl.BlockSpec((ts, tn), lambda j, i: (i, j)),
                 out_shape=jax.ShapeDtypeStruct((s, n), out_dtype),
                 compiler_params=_params("arbitrary", "arbitrary"))(*args)


def mm_nt(dy, w, name):
    s, n = dy.shape
    k = w.shape[0]
    ts = min(TS, s)

    def body(dy_ref, w_ref, o_ref):
        o_ref[...] = _nt(dy_ref[...], w_ref[...])

    return _call(body, name=name, grid=(s // ts,),
                 in_specs=[pl.BlockSpec((ts, n), lambda i: (i, 0)),
                           pl.BlockSpec((k, n), lambda i: (0, 0))],
                 out_specs=pl.BlockSpec((ts, k), lambda i: (i, 0)),
                 out_shape=jax.ShapeDtypeStruct((s, k), F32),
                 compiler_params=_params("arbitrary"))(dy, w)


def mm_nt_mod(dy, w, x, dres, mod, rows, name):
    s, n = dy.shape
    k = w.shape[0]
    ts = min(TS, s)

    def body(dy_ref, w_ref, x_ref, r_ref, m_ref, dx_ref, dsh_ref, dsc_ref):
        @pl.when(pl.program_id(0) == 0)
        def _():
            dsh_ref[...] = jnp.zeros_like(dsh_ref)
            dsc_ref[...] = jnp.zeros_like(dsc_ref)

        dh = _nt(dy_ref[...], w_ref[...])
        dx_ref[...] = r_ref[...] + dh * (1.0 + m_ref[rows[1]:rows[1] + 1, :])
        dsh_ref[...] += jnp.sum(dh, axis=0, keepdims=True)
        dsc_ref[...] += jnp.sum(dh * x_ref[...], axis=0, keepdims=True)

    row = pl.BlockSpec((1, k), lambda i: (0, 0))
    tile = pl.BlockSpec((ts, k), lambda i: (i, 0))
    return _call(body, name=name, grid=(s // ts,),
                 in_specs=[pl.BlockSpec((ts, n), lambda i: (i, 0)),
                           pl.BlockSpec((k, n), lambda i: (0, 0)), tile, tile,
                           pl.BlockSpec((8, k), lambda i: (0, 0))],
                 out_specs=[tile, row, row],
                 out_shape=[jax.ShapeDtypeStruct((s, k), F32),
                            jax.ShapeDtypeStruct((1, k), F32),
                            jax.ShapeDtypeStruct((1, k), F32)],
                 compiler_params=_params("arbitrary"))(dy, w, x, dres, mod)


def mm_tn(a, dy, tn, name, mod=None, rows=None):
    s, k = a.shape
    n = dy.shape[1]
    ts = min(TS, s)

    def body(*refs):
        if mod is None:
            a_ref, dy_ref, o_ref = refs
            h = a_ref[...]
        else:
            a_ref, m_ref, dy_ref, o_ref = refs
            h = a_ref[...] * (1.0 + m_ref[rows[1]:rows[1] + 1, :]) + m_ref[rows[0]:rows[0] + 1, :]

        @pl.when(pl.program_id(1) == 0)
        def _():
            o_ref[...] = jnp.zeros_like(o_ref)

        o_ref[...] += _tn(h, dy_ref[...])

    in_specs = [pl.BlockSpec((ts, k), lambda j, i: (i, 0))]
    args = [a]
    if mod is not None:
        in_specs.append(pl.BlockSpec((8, k), lambda j, i: (0, 0)))
        args.append(mod)
    in_specs.append(pl.BlockSpec((ts, tn), lambda j, i: (i, j)))
    args.append(dy)
    return _call(body, name=name, grid=(n // tn, s // ts), in_specs=in_specs,
                 out_specs=pl.BlockSpec((k, tn), lambda j, i: (0, j)),
                 out_shape=jax.ShapeDtypeStruct((k, n), F32),
                 compiler_params=_params("arbitrary", "arbitrary"))(*args)


def _post_fn(x, br, gate, lg, lb):
    return _ln(DN_ALPHA * x + (1.0 + gate) * br, lg, lb)


def post_fwd(x, br, mod, grow, lg, lb, name):
    s = x.shape[0]
    ts = min(TS, s)

    def body(x_ref, b_ref, m_ref, lg_ref, lb_ref, o_ref):
        o_ref[...] = _post_fn(x_ref[...], b_ref[...], m_ref[grow:grow + 1, :], lg_ref[...], lb_ref[...])

    tile = pl.BlockSpec((ts, D), lambda i: (i, 0))
    row = pl.BlockSpec((1, D), lambda i: (0, 0))
    return _call(body, name=name, grid=(s // ts,),
                 in_specs=[tile, tile, pl.BlockSpec((8, D), lambda i: (0, 0)), row, row],
                 out_specs=tile, out_shape=jax.ShapeDtypeStruct((s, D), F32),
                 compiler_params=_params("arbitrary"))(x, br, mod, lg, lb)


def post_bwd(x, br, mod, grow, lg, lb, dy, name):
    s = x.shape[0]
    ts = min(TS, s)

    def body(x_ref, b_ref, m_ref, lg_ref, lb_ref, dy_ref, dx_ref, db_ref, dg_ref, dlg_ref, dlb_ref):
        @pl.when(pl.program_id(0) == 0)
        def _():
            dg_ref[...] = jnp.zeros_like(dg_ref)
            dlg_ref[...] = jnp.zeros_like(dlg_ref)
            dlb_ref[...] = jnp.zeros_like(dlb_ref)

        _, vjp = jax.vjp(_post_fn, x_ref[...], b_ref[...], m_ref[grow:grow + 1, :], lg_ref[...], lb_ref[...])
        dx, db, dg, dlg, dlb = vjp(dy_ref[...])
        dx_ref[...] = dx
        db_ref[...] = db.astype(BF16)
        dg_ref[...] += dg
        dlg_ref[...] += dlg
        dlb_ref[...] += dlb

    tile = pl.BlockSpec((ts, D), lambda i: (i, 0))
    row = pl.BlockSpec((1, D), lambda i: (0, 0))
    rs = jax.ShapeDtypeStruct((1, D), F32)
    return _call(body, name=name, grid=(s // ts,),
                 in_specs=[tile, tile, pl.BlockSpec((8, D), lambda i: (0, 0)), row, row, tile],
                 out_specs=[tile, tile, row, row, row],
                 out_shape=[jax.ShapeDtypeStruct((s, D), F32), jax.ShapeDtypeStruct((s, D), BF16), rs, rs, rs],
                 compiler_params=_params("arbitrary"))(x, br, mod, lg, lb, dy)


def loss_kernel(y, target):
    s = y.shape[0]
    ts = min(TS, s)

    def body(y_ref, t_ref, l_ref, dy_ref):
        @pl.when(pl.program_id(0) == 0)
        def _():
            l_ref[...] = jnp.zeros_like(l_ref)

        err = y_ref[...] - t_ref[...]
        dy_ref[...] = err * (1.0 / D)
        per_tok = jnp.mean(err * err, axis=-1, keepdims=True)
        l_ref[...] += 0.5 * jnp.sum(per_tok)

    tile = pl.BlockSpec((ts, D), lambda i: (i, 0))
    return _call(body, name="loss", grid=(s // ts,), in_specs=[tile, tile],
                 out_specs=[pl.BlockSpec((8, 128), lambda i: (0, 0)), tile],
                 out_shape=[jax.ShapeDtypeStruct((8, 128), F32), jax.ShapeDtypeStruct((s, D), F32)],
                 compiler_params=_params("arbitrary"))(y, target)


def _gm_head(u, v, lg, lb, w, bs):
    t = u.shape[0]
    causal = _rows((GM_C, GM_C)) >= _lanes((GM_C, GM_C))
    vn = _ln(v, lg, lb)
    wm = jnp.where(causal, w, 0.0)
    chunks = []
    for n in range(t // GM_C):
        rs = slice(n * GM_C, (n + 1) * GM_C)
        chunks.append(u[rs] * (_bdot(wm, vn[rs]) + bs))
    return jnp.concatenate(chunks, axis=0)


def gm_fwd(p, lg, lb, ws, bst):
    s = p.shape[0]
    ts = min(TS_C, s)

    def body(u_ref, v_ref, lg_ref, lb_ref, ws_ref, bs_ref, o_ref):
        for h in range(GM_H):
            sl = slice(h * HD, (h + 1) * HD)
            o_ref[:, sl] = _gm_head(u_ref[:, sl], v_ref[:, sl], lg_ref[:, sl], lb_ref[:, sl], ws_ref[h],
                                    bs_ref[:, h:h + 1]).astype(BF16)

    full = lambda shape: pl.BlockSpec(shape, lambda i: (0,) * len(shape))
    return _call(body, name="gm_fwd", grid=(s // ts,),
                 in_specs=[pl.BlockSpec((ts, GM_W), lambda i: (i, 0)), pl.BlockSpec((ts, GM_W), lambda i: (i, 1)),
                           full((1, GM_W)), full((1, GM_W)), full((GM_H, GM_C, GM_C)), full((GM_C, GM_H))],
                 out_specs=pl.BlockSpec((ts, GM_W), lambda i: (i, 0)),
                 out_shape=jax.ShapeDtypeStruct((s, GM_W), BF16),
                 compiler_params=_params("arbitrary"))(p, p, lg, lb, ws, bst)


def gm_bwd(p, lg, lb, ws, bst, dmix):
    s = p.shape[0]
    ts = min(TS_C, s)

    def body(u_ref, v_ref, lg_ref, lb_ref, ws_ref, bs_ref, dy_ref, duv_ref, dlg_ref, dlb_ref, dws_ref, dbs_ref):
        @pl.when(pl.program_id(0) == 0)
        def _():
            dlg_ref[...] = jnp.zeros_like(dlg_ref)
            dlb_ref[...] = jnp.zeros_like(dlb_ref)
            dws_ref[...] = jnp.zeros_like(dws_ref)
            dbs_ref[...] = jnp.zeros_like(dbs_ref)

        for h in range(GM_H):
            sl = slice(h * HD, (h + 1) * HD)
            _, vjp = jax.vjp(_gm_head, u_ref[:, sl], v_ref[:, sl], lg_ref[:, sl], lb_ref[:, sl], ws_ref[h],
                             bs_ref[:, h:h + 1])
            du, dv, dlg, dlb, dw, dbs = vjp(dy_ref[:, sl])
            duv_ref[:, sl] = du.astype(BF16)
            duv_ref[:, GM_W + h * HD:GM_W + (h + 1) * HD] = dv.astype(BF16)
            dlg_ref[:, sl] += dlg
            dlb_ref[:, sl] += dlb
            dws_ref[h] += dw
            dbs_ref[:, h:h + 1] += dbs

    full = lambda shape: pl.BlockSpec(shape, lambda i: (0,) * len(shape))
    return _call(body, name="gm_bwd", grid=(s // ts,),
                 in_specs=[pl.BlockSpec((ts, GM_W), lambda i: (i, 0)), pl.BlockSpec((ts, GM_W), lambda i: (i, 1)),
                           full((1, GM_W)), full((1, GM_W)), full((GM_H, GM_C, GM_C)), full((GM_C, GM_H)),
                           pl.BlockSpec((ts, GM_W), lambda i: (i, 0))],
                 out_specs=[pl.BlockSpec((ts, 2 * GM_W), lambda i: (i, 0)), full((1, GM_W)), full((1, GM_W)),
                            full((GM_H, GM_C, GM_C)), full((GM_C, GM_H))],
                 out_shape=[jax.ShapeDtypeStruct((s, 2 * GM_W), BF16), jax.ShapeDtypeStruct((1, GM_W), F32),
                            jax.ShapeDtypeStruct((1, GM_W), F32), jax.ShapeDtypeStruct((GM_H, GM_C, GM_C), F32),
                            jax.ShapeDtypeStruct((GM_C, GM_H), F32)],
                 compiler_params=_params("arbitrary"))(p, p, lg, lb, ws, bst, dmix)


def _s5_prep_fn(lr, li, ldt, bt):
    dt = jnp.exp(ldt)
    er = jnp.exp(lr * dt)
    ar = er * jnp.cos(li * dt)
    ai = er * jnp.sin(li * dt)
    den = lr * lr + li * li
    nr = ar - 1.0
    cr = (nr * lr + ai * li) / den
    ci = (ai * lr - nr * li) / den
    br, bi = bt[:, :S5_N], bt[:, S5_N:]
    return ar, ai, jnp.concatenate([cr * br - ci * bi, cr * bi + ci * br], axis=1)


def s5_prep_fwd(lr, li, ldt, bt):
    def body(lr_ref, li_ref, ldt_ref, bt_ref, a_ref, bb_ref):
        ar, ai, bb = _s5_prep_fn(lr_ref[...], li_ref[...], ldt_ref[...], bt_ref[...])
        a_ref[...] = jnp.concatenate([ar, ai, jnp.zeros((6, S5_N), F32)], axis=0)
        bb_ref[...] = bb

    return _call(body, name="s5_prep_fwd",
                 out_shape=[jax.ShapeDtypeStruct((8, S5_N), F32), jax.ShapeDtypeStruct((S5_H, 2 * S5_N), F32)])(lr, li, ldt, bt)


def s5_prep_bwd(lr, li, ldt, bt, da, dbb):
    def body(lr_ref, li_ref, ldt_ref, bt_ref, da_ref, dbb_ref, dlr_ref, dli_ref, dldt_ref, dbt_ref):
        _, vjp = jax.vjp(_s5_prep_fn, lr_ref[...], li_ref[...], ldt_ref[...], bt_ref[...])
        dlr, dli, dldt, dbt = vjp((da_ref[0:1, :], da_ref[1:2, :], dbb_ref[...]))
        dlr_ref[...] = dlr
        dli_ref[...] = dli
        dbt_ref[...] = dbt
        group = (_rows((S5_N, 128)) // S5_P == _lanes((S5_N, 128))).astype(F32)
        dldt_ref[...] = jnp.dot(jnp.broadcast_to(dldt, (8, S5_N)), group, precision=lax.Precision.HIGHEST,
                                preferred_element_type=F32)[0:1, :]

    r = jax.ShapeDtypeStruct((1, S5_N), F32)
    return _call(body, name="s5_prep_bwd",
                 out_shape=[r, r, jax.ShapeDtypeStruct((1, 128), F32),
                            jax.ShapeDtypeStruct((S5_H, 2 * S5_N), F32)])(lr, li, ldt, bt, da, dbb)


def _s5_out_fn(x, u, cbd, drow, wg, bg):
    y = _bdot_nt(x[:, :S5_N], cbd[:, :S5_N]) - _bdot_nt(x[:, S5_N:], cbd[:, S5_N:]) + drow * u
    y = jax.nn.gelu(y)
    gate = _bdot_nt(y, wg) + bg
    return y * jax.nn.sigmoid(gate)


def _scan_steps(t):
    return int(math.log2(t))


def s5_fwd(p, arow, bbd, cbd, drow, wg, bg):
    s = p.shape[0]
    t = min(T_S5, s)

    def body(u_ref, a_ref, bbd_ref, cbd_ref, d_ref, wg_ref, bg_ref, y_ref, st_ref, carry):
        @pl.when(pl.program_id(0) == 0)
        def _():
            carry[...] = jnp.zeros_like(carry)

        u = u_ref[...]
        bu = _nn(u, bbd_ref[...])
        ar, ai = a_ref[0:1, :], a_ref[1:2, :]
        cr, ci = carry[0:1, :S5_N], carry[0:1, S5_N:]
        rid = _rows((t, S5_N))
        first = rid == 0
        xr = bu[:, :S5_N] + jnp.where(first, ar * cr - ai * ci, 0.0)
        xi = bu[:, S5_N:] + jnp.where(first, ar * ci + ai * cr, 0.0)
        pr, pi = ar, ai
        for k in range(_scan_steps(t)):
            d = 1 << k
            keep = rid >= d
            sr = jnp.where(keep, pltpu.roll(xr, d, 0), 0.0)
            si = jnp.where(keep, pltpu.roll(xi, d, 0), 0.0)
            xr, xi = xr + pr * sr - pi * si, xi + pr * si + pi * sr
            pr, pi = pr * pr - pi * pi, 2.0 * pr * pi
        st_ref[:, :S5_N] = xr
        st_ref[:, S5_N:] = xi
        carry[0:1, :S5_N] = xr[t - 1:t, :]
        carry[0:1, S5_N:] = xi[t - 1:t, :]
        x = jnp.concatenate([xr, xi], axis=1)
        y_ref[...] = _s5_out_fn(x, u, cbd_ref[...], d_ref[...], wg_ref[...], bg_ref[...]).astype(BF16)

    full = lambda shape: pl.BlockSpec(shape, lambda i: (0,) * len(shape))
    return _call(body, name="s5_fwd", grid=(s // t,),
                 in_specs=[pl.BlockSpec((t, S5_W), lambda i: (i, 2)), full((8, S5_N)), full((S5_W, 2 * S5_N)),
                           full((S5_W, 2 * S5_N)), full((1, S5_W)), full((S5_W, S5_W)), full((1, S5_W))],
                 out_specs=[pl.BlockSpec((t, S5_W), lambda i: (i, 0)), pl.BlockSpec((t, 2 * S5_N), lambda i: (i, 0))],
                 out_shape=[jax.ShapeDtypeStruct((s, S5_W), BF16), jax.ShapeDtypeStruct((s, 2 * S5_N), F32)],
                 scratch_shapes=[pltpu.VMEM((8, 2 * S5_N), F32)],
                 compiler_params=_params("arbitrary"))(p, arow, bbd, cbd, drow, wg, bg)


def s5_bwd(p, st, arow, bbd, cbd, drow, wg, bg, dmix):
    s = p.shape[0]
    t = min(T_S5, s)
    nc = s // t

    def body(u_ref, st_ref, prev_ref, a_ref, bbd_ref, cbd_ref, d_ref, wg_ref, bg_ref, dy_ref,
             du_ref, da_ref, dbbd_ref, dcbd_ref, dd_ref, dwg_ref, dbg_ref, carry):
        i = pl.program_id(0)

        @pl.when(i == 0)
        def _():
            carry[...] = jnp.zeros_like(carry)
            for r in (da_ref, dbbd_ref, dcbd_ref, dd_ref, dwg_ref, dbg_ref):
                r[...] = jnp.zeros_like(r)

        u = u_ref[...]
        x = st_ref[...]
        _, vjp = jax.vjp(_s5_out_fn, x, u, cbd_ref[...], d_ref[...], wg_ref[...], bg_ref[...])
        dx, du1, dcbd, dd, dwg, dbg = vjp(dy_ref[...])
        ar, ai = a_ref[0:1, :], a_ref[1:2, :]
        cr, ci = carry[0:1, :S5_N], carry[0:1, S5_N:]
        rid = _rows((t, S5_N))
        last = rid == t - 1
        gr = dx[:, :S5_N] + jnp.where(last, ar * cr + ai * ci, 0.0)
        gi = dx[:, S5_N:] + jnp.where(last, ar * ci - ai * cr, 0.0)
        pr, pi = ar, -ai
        for k in range(_scan_steps(t)):
            d = 1 << k
            keep = rid < t - d
            sr = jnp.where(keep, pltpu.roll(gr, t - d, 0), 0.0)
            si = jnp.where(keep, pltpu.roll(gi, t - d, 0), 0.0)
            gr, gi = gr + pr * sr - pi * si, gi + pr * si + pi * sr
            pr, pi = pr * pr - pi * pi, 2.0 * pr * pi
        carry[0:1, :S5_N] = gr[0:1, :]
        carry[0:1, S5_N:] = gi[0:1, :]
        has_prev = (i < nc - 1).astype(F32)
        top_r = prev_ref[7:8, :S5_N] * has_prev
        top_i = prev_ref[7:8, S5_N:] * has_prev
        xpr = jnp.where(rid == 0, top_r, pltpu.roll(x[:, :S5_N], 1, 0))
        xpi = jnp.where(rid == 0, top_i, pltpu.roll(x[:, S5_N:], 1, 0))
        da_ref[0:1, :] += jnp.sum(xpr * gr + xpi * gi, axis=0, keepdims=True)
        da_ref[1:2, :] += jnp.sum(xpr * gi - xpi * gr, axis=0, keepdims=True)
        g = jnp.concatenate([gr, gi], axis=1)
        dbbd_ref[...] += _tn(u, g)
        du_ref[...] = (_nt(g, bbd_ref[...]) + du1).astype(BF16)
        dcbd_ref[...] += dcbd
        dd_ref[...] += dd
        dwg_ref[...] += dwg
        dbg_ref[...] += dbg

    full = lambda shape: pl.BlockSpec(shape, lambda i: (0,) * len(shape))
    rev = lambda col: (lambda i: (nc - 1 - i, col))
    prev_map = lambda i: (jnp.maximum((nc - 1 - i) * (t // 8) - 1, 0), 0)
    return _call(body, name="s5_bwd", grid=(nc,),
                 in_specs=[pl.BlockSpec((t, S5_W), rev(2)), pl.BlockSpec((t, 2 * S5_N), rev(0)),
                           pl.BlockSpec((8, 2 * S5_N), prev_map), full((8, S5_N)), full((S5_W, 2 * S5_N)),
                           full((S5_W, 2 * S5_N)), full((1, S5_W)), full((S5_W, S5_W)), full((1, S5_W)),
                           pl.BlockSpec((t, S5_W), rev(1))],
                 out_specs=[pl.BlockSpec((t, S5_W), rev(0)), full((8, S5_N)), full((S5_W, 2 * S5_N)),
                            full((S5_W, 2 * S5_N)), full((1, S5_W)), full((S5_W, S5_W)), full((1, S5_W))],
                 out_shape=[jax.ShapeDtypeStruct((s, S5_W), BF16), jax.ShapeDtypeStruct((8, S5_N), F32),
                            jax.ShapeDtypeStruct((S5_W, 2 * S5_N), F32), jax.ShapeDtypeStruct((S5_W, 2 * S5_N), F32),
                            jax.ShapeDtypeStruct((1, S5_W), F32), jax.ShapeDtypeStruct((S5_W, S5_W), F32),
                            jax.ShapeDtypeStruct((1, S5_W), F32)],
                 scratch_shapes=[pltpu.VMEM((8, 2 * S5_N), F32)],
                 compiler_params=_params("arbitrary"))(p, st, st, arow, bbd, cbd, drow, wg, bg, dmix)


def _cum_steps(s):
    return int(math.ceil(math.log2(s)))


V_BLK = (2 * GM_W + S5_W + 2 * FX_W) // 128


AUG = 2 * HD
BIAS_COL = HD
FQ_COL = HD + 3
PAIR_W = 256


def _split3(f):
    hi = f.astype(BF16).astype(F32)
    r = f - hi
    mid = r.astype(BF16).astype(F32)
    lo = (r - mid).astype(BF16).astype(F32)
    return hi, mid, lo


def fox_prep(p, bf):
    s = p.shape[0]
    ts = min(TS, s)
    scale = HD ** -0.5

    def body(q0_ref, q1_ref, k0_ref, k1_ref, v0_ref, v1_ref, f_ref, bf_ref,
             qa_ref, ka_ref, qat_ref, kat_ref, vt_ref, carry):
        @pl.when(pl.program_id(0) == 0)
        def _():
            carry[...] = jnp.zeros_like(carry)

        lane = _lanes((ts, 128))
        lf = jax.nn.log_sigmoid(f_ref[...] + bf_ref[...])
        acc = jnp.where(lane < FX_H, lf, 0.0)
        rid = _rows((ts, 128))
        for k in range(_cum_steps(ts)):
            d = 1 << k
            acc = acc + jnp.where(rid >= d, pltpu.roll(acc, d, 0), 0.0)
        acc = acc + carry[0:1, :]
        carry[0:1, :] = acc[ts - 1:ts, :]

        low = lane < HD
        for h in range(FX_H):
            blk, pos = divmod(h, 4)
            pair = slice((pos // 2) * 128, (pos // 2) * 128 + 128)
            hi, mid, lo = _split3(acc[:, h:h + 1])
            one = jnp.ones((ts, 1), F32)

            def augment(ref, cols):
                x = ref[:, pair]
                if pos % 2:
                    x = pltpu.roll(x, HD, 1)
                out = jnp.where(low, x, 0.0)
                for j, cval in enumerate(cols):
                    out = jnp.where(lane == HD + j, cval, out)
                return out

            qa = augment((q0_ref, q1_ref)[blk], (one, one, one, hi, mid, lo))
            qa = jnp.where(low, qa * scale, qa)
            ka = augment((k0_ref, k1_ref)[blk], (-hi, -mid, -lo, one, one, one))
            cs = slice(h * AUG, (h + 1) * AUG)
            qa_ref[:, cs] = qa.astype(BF16)
            ka_ref[:, cs] = ka.astype(BF16)
            qat_ref[cs, :] = jnp.transpose(qa).astype(BF16)
            kat_ref[cs, :] = jnp.transpose(ka).astype(BF16)
        for j in range(FX_H // 2):
            vref = (v0_ref, v1_ref)[j // 2]
            vt_ref[j * 128:(j + 1) * 128, :] = jnp.transpose(vref[:, (j % 2) * 128:(j % 2) * 128 + 128]).astype(BF16)

    q_blk = (2 * GM_W + S5_W) // PAIR_W
    col = lambda b: pl.BlockSpec((ts, PAIR_W), lambda i: (i, b))
    wide = FX_H * AUG
    return _call(body, name="fox_prep", grid=(s // ts,),
                 in_specs=[col(q_blk), col(q_blk + 1), col(q_blk + 2), col(q_blk + 3), col(q_blk + 4), col(q_blk + 5),
                           pl.BlockSpec((ts, 128), lambda i: (i, FF_COL // 128)), pl.BlockSpec((1, 128), lambda i: (0, 0))],
                 out_specs=[pl.BlockSpec((ts, wide), lambda i: (i, 0)), pl.BlockSpec((ts, wide), lambda i: (i, 0)),
                            pl.BlockSpec((wide, ts), lambda i: (0, i)), pl.BlockSpec((wide, ts), lambda i: (0, i)),
                            pl.BlockSpec((FX_W, ts), lambda i: (0, i))],
                 out_shape=[jax.ShapeDtypeStruct((s, wide), BF16), jax.ShapeDtypeStruct((s, wide), BF16),
                            jax.ShapeDtypeStruct((wide, s), BF16), jax.ShapeDtypeStruct((wide, s), BF16),
                            jax.ShapeDtypeStruct((FX_W, s), BF16)],
                 scratch_shapes=[pltpu.VMEM((8, 128), F32)],
                 compiler_params=_params("arbitrary"))(p, p, p, p, p, p, p, bf)


def fox_prep_grad(p, bf, dfq, dfk):
    s = p.shape[0]
    ts = min(TS, s)
    ns = s // ts

    def body(f_ref, bf_ref, dfq_ref, dfk_ref, df_ref, dbf_ref, carry):
        @pl.when(pl.program_id(0) == 0)
        def _():
            carry[...] = jnp.zeros_like(carry)
            dbf_ref[...] = jnp.zeros_like(dbf_ref)

        lane = _lanes((ts, 128))
        acc = jnp.zeros((ts, 128), F32)
        for h in range(FX_H):
            c = (h // 2) * 128 + h % 2
            acc = jnp.where(lane == h, dfq_ref[:, c:c + 1] + dfk_ref[:, c:c + 1], acc)
        rid = _rows((ts, 128))
        for k in range(_cum_steps(ts)):
            d = 1 << k
            acc = acc + jnp.where(rid < ts - d, pltpu.roll(acc, ts - d, 0), 0.0)
        acc = acc + carry[0:1, :]
        carry[0:1, :] = acc[0:1, :]
        z = f_ref[...] + bf_ref[...]
        df = jnp.where(lane < FX_H, acc * jax.nn.sigmoid(-z), 0.0)
        df_ref[...] = df.astype(BF16)
        dbf_ref[...] += jnp.sum(df, axis=0, keepdims=True)

    rev = lambda i: (ns - 1 - i, 0)
    return _call(body, name="fox_prep_grad", grid=(ns,),
                 in_specs=[pl.BlockSpec((ts, 128), lambda i: (ns - 1 - i, FF_COL // 128)),
                           pl.BlockSpec((1, 128), lambda i: (0, 0)),
                           pl.BlockSpec((ts, FX_W), rev), pl.BlockSpec((ts, FX_W), rev)],
                 out_specs=[pl.BlockSpec((ts, 128), rev), pl.BlockSpec((1, 128), lambda i: (0, 0))],
                 out_shape=[jax.ShapeDtypeStruct((s, 128), BF16), jax.ShapeDtypeStruct((1, 128), F32)],
                 scratch_shapes=[pltpu.VMEM((8, 128), F32)],
                 compiler_params=_params("arbitrary"))(p, bf, dfq, dfk)


def attn(qat, ka, vt):
    s = ka.shape[0]
    tq = min(TQ, s)
    nq = s // tq

    nh = ATTN_HEADS

    def body(qat_ref, ka_ref, vt_ref, o_ref, lse_ref):
        qi = pl.program_id(1)
        causal = _rows((tq, tq)) <= _lanes((tq, tq))
        lse_ref[...] = jnp.zeros_like(lse_ref)

        def step(kj, carry, masked):
            off = pl.multiple_of(kj * tq, tq)
            out = []
            for hh in range(nh):
                m, l, acc = carry[hh]
                st = jnp.dot(ka_ref[pl.ds(off, tq), hh * AUG:(hh + 1) * AUG], qat_ref[hh * AUG:(hh + 1) * AUG, :],
                             preferred_element_type=F32)
                if masked:
                    st = jnp.where(causal, st, NEG_INF)
                m_new = jnp.maximum(m, jnp.max(st, axis=0, keepdims=True))
                alpha = jnp.exp(m - m_new)
                pt = jnp.exp(st - m_new)
                v = vt_ref[hh * HD:(hh + 1) * HD, pl.ds(off, tq)]
                out.append((m_new, alpha * l + jnp.sum(pt, axis=0, keepdims=True),
                            alpha * acc + jnp.dot(v, pt.astype(BF16), preferred_element_type=F32)))
            return tuple(out)

        init = tuple((jnp.full((1, tq), NEG_INF, F32), jnp.zeros((1, tq), F32), jnp.zeros((HD, tq), F32))
                     for _ in range(nh))
        carry = lax.fori_loop(0, qi, lambda kj, c: step(kj, c, False), init)
        carry = step(qi, carry, True)
        for hh in range(nh):
            m, l, _ = carry[hh]
            lse_ref[hh // 2, hh % 2:hh % 2 + 1, :] = m + jnp.log(l)
        for j in range(nh // 2):
            pair = jnp.concatenate([carry[2 * j][2] / carry[2 * j][1], carry[2 * j + 1][2] / carry[2 * j + 1][1]], axis=0)
            o_ref[:, j * 128:(j + 1) * 128] = jnp.transpose(pair).astype(BF16)

    return _call(body, name="attn", grid=(FX_H // nh, nq),
                 in_specs=[pl.BlockSpec((nh * AUG, tq), lambda h, i: (h, i)),
                           pl.BlockSpec((s, nh * AUG), lambda h, i: (0, h)),
                           pl.BlockSpec((nh * HD, s), lambda h, i: (h, 0))],
                 out_specs=[pl.BlockSpec((tq, nh * HD), lambda h, i: (i, h)),
                            pl.BlockSpec((nh // 2, 8, tq), lambda h, i: (h, 0, i))],
                 out_shape=[jax.ShapeDtypeStruct((s, FX_W), BF16), jax.ShapeDtypeStruct((FX_H // 2, 8, s), F32)],
                 compiler_params=_params("arbitrary", "arbitrary"))(qat, ka, vt)


def attn_grad(qa, qat, ka, kat, p, o, lse, dmix):
    s = qa.shape[0]
    tq = min(TQ, s)
    nq = s // tq
    scale = HD ** -0.5

    def body(qa_ref, qat_ref, ka_ref, kat_ref, v_ref, o_ref, lse_ref, do_ref,
             dq_ref, dk_ref, dv_ref, dfq_ref, dfk_ref, dot_scr, delta, dqt):
        kj = pl.program_id(1)
        lane = _lanes((tq, 128))
        low = lane < HD
        causal = _rows((tq, tq)) <= _lanes((tq, tq))

        @pl.when(kj == 0)
        def _():
            dqt[...] = jnp.zeros_like(dqt)
            delta[...] = jnp.zeros_like(delta)

            def prep(c, _):
                rows = pl.ds(pl.multiple_of(c * tq, tq), tq)
                do = do_ref[rows, :]
                pt = jnp.transpose(do * o_ref[rows, :].astype(F32))
                delta[0:1, rows] = jnp.sum(pt[:HD], axis=0, keepdims=True)
                delta[1:2, rows] = jnp.sum(pt[HD:], axis=0, keepdims=True)
                dot_scr[:, rows] = jnp.transpose(do).astype(BF16)
                return 0

            lax.fori_loop(0, nq, prep, 0)

        v = v_ref[...]
        vms = [jnp.where(low, v, 0.0).astype(BF16), jnp.where(low, 0.0, v).astype(BF16)]

        def tile(qi, carry, masked):
            cols = pl.ds(pl.multiple_of(qi * tq, tq), tq)
            do = do_ref[cols, :].astype(BF16)
            out = []
            for hh in range(2):
                cs = slice(hh * AUG, (hh + 1) * AUG)
                dka, dv = carry[hh]
                st = jnp.dot(ka_ref[:, cs], qat_ref[cs, cols], preferred_element_type=F32)
                if masked:
                    st = jnp.where(causal, st, NEG_INF)
                pt = jnp.exp(st - lse_ref[0, hh:hh + 1, cols])
                dv = dv + jnp.dot(pt.astype(BF16), do, preferred_element_type=F32)
                dpt = jnp.dot(vms[hh], dot_scr[:, cols], preferred_element_type=F32)
                dsb = (pt * (dpt - delta[hh:hh + 1, cols])).astype(BF16)
                dka = dka + jnp.dot(dsb, qa_ref[cols, cs], preferred_element_type=F32)
                dqt[hh, :, cols] += jnp.dot(kat_ref[cs, :], dsb, preferred_element_type=F32)
                out.append((dka, dv))
            return tuple(out)

        init = tuple((jnp.zeros((tq, AUG), F32), jnp.zeros((tq, 128), F32)) for _ in range(2))
        carry = tile(kj, init, True)
        carry = lax.fori_loop(kj + 1, nq, lambda qi, c: tile(qi, c, False), carry)
        dks = [carry[0][0], carry[1][0]]
        dvs = [carry[0][1], carry[1][1]]
        dv_ref[...] = jnp.where(low, dvs[0], dvs[1]).astype(BF16)
        dk_ref[...] = jnp.where(low, dks[0], pltpu.roll(dks[1], HD, 1)).astype(BF16)
        dfk_ref[...] = jnp.where(lane == 0, -dks[0][:, BIAS_COL:BIAS_COL + 1],
                                 jnp.where(lane == 1, -dks[1][:, BIAS_COL:BIAS_COL + 1], 0.0))

        @pl.when(kj == nq - 1)
        def _():
            def finish(c, _):
                rows = pl.ds(pl.multiple_of(c * tq, tq), tq)
                t0 = jnp.transpose(dqt[0, :, rows])
                t1 = jnp.transpose(dqt[1, :, rows])
                dq_ref[rows, :] = (jnp.where(low, t0, pltpu.roll(t1, HD, 1)) * scale).astype(BF16)
                dfq_ref[rows, :] = jnp.where(lane == 0, t0[:, FQ_COL:FQ_COL + 1],
                                             jnp.where(lane == 1, t1[:, FQ_COL:FQ_COL + 1], 0.0))
                return 0

            lax.fori_loop(0, nq, finish, 0)

    seq128 = lambda blk: pl.BlockSpec((s, 128), lambda h, j: (0, blk + h))
    tile128 = pl.BlockSpec((tq, 128), lambda h, j: (j, h))
    out_b = jax.ShapeDtypeStruct((s, FX_W), BF16)
    out_f = jax.ShapeDtypeStruct((s, FX_W), F32)
    return _call(body, name="attn_grad", grid=(FX_H // 2, nq),
                 in_specs=[pl.BlockSpec((s, 2 * AUG), lambda h, j: (0, h)), pl.BlockSpec((2 * AUG, s), lambda h, j: (h, 0)),
                           pl.BlockSpec((tq, 2 * AUG), lambda h, j: (j, h)), pl.BlockSpec((2 * AUG, tq), lambda h, j: (h, j)),
                           pl.BlockSpec((tq, 128), lambda h, j: (j, V_BLK + h)), seq128(0),
                           pl.BlockSpec((1, 8, s), lambda h, j: (h, 0, 0)), seq128(4)],
                 out_specs=[seq128(0), tile128, tile128, seq128(0), tile128],
                 out_shape=[out_b, out_b, out_b, out_f, out_f],
                 scratch_shapes=[pltpu.VMEM((128, s), BF16), pltpu.VMEM((8, s), F32), pltpu.VMEM((2, AUG, s), F32)],
                 compiler_params=_params("arbitrary", "arbitrary"))(qa, qat, ka, kat, p, o, lse, dmix)


def _shift_down(a, prev8, k):
    r = pltpu.roll(a, k, 0)
    top = jnp.where(_rows(prev8.shape) < k, pltpu.roll(prev8, k, 0), r[0:8])
    return jnp.concatenate([top, r[8:]], axis=0)


def _shift_up(a, next8, k):
    t = a.shape[0]
    r = pltpu.roll(a, t - k, 0)
    bot = jnp.where(_rows(next8.shape) >= 8 - k, pltpu.roll(next8, 8 - k, 0), r[t - 8:t])
    return jnp.concatenate([r[:t - 8], bot], axis=0)


def _conv(a, prev8, cw, cb):
    return cb + cw[0:1, :] * _shift_down(a, prev8, 2) + cw[1:2, :] * _shift_down(a, prev8, 1) + cw[2:3, :] * a


def conv_fwd(up, cw, cb):
    s = up.shape[0]
    ts = min(TS_C, s)

    def body(a_ref, g_ref, cw_ref, cb_ref, o_ref, halo):
        @pl.when(pl.program_id(0) == 0)
        def _():
            halo[...] = jnp.zeros_like(halo)

        a = a_ref[...]
        c = _conv(a, halo[...], cw_ref[...], cb_ref[...])
        o_ref[...] = (jax.nn.gelu(c) * g_ref[...]).astype(BF16)
        halo[...] = a[ts - 8:ts, :]

    return _call(body, name="conv_fwd", grid=(s // ts,),
                 in_specs=[pl.BlockSpec((ts, DFF), lambda i: (i, 0)), pl.BlockSpec((ts, DFF), lambda i: (i, 1)),
                           pl.BlockSpec((3, DFF), lambda i: (0, 0)), pl.BlockSpec((1, DFF), lambda i: (0, 0))],
                 out_specs=pl.BlockSpec((ts, DFF), lambda i: (i, 0)),
                 out_shape=jax.ShapeDtypeStruct((s, DFF), BF16),
                 scratch_shapes=[pltpu.VMEM((8, DFF), F32)],
                 compiler_params=_params("arbitrary"))(up, up, cw, cb)


def conv_bwd(up, cw, cb, dact):
    s = up.shape[0]
    ts = min(TS_C, s)
    ns = s // ts

    def body(a_ref, g_ref, prev_ref, cw_ref, cb_ref, dact_ref, dup_ref, dcw_ref, dcb_ref, halo):
        i = pl.program_id(0)

        @pl.when(i == 0)
        def _():
            halo[...] = jnp.zeros_like(halo)
            dcw_ref[...] = jnp.zeros_like(dcw_ref)
            dcb_ref[...] = jnp.zeros_like(dcb_ref)

        a = a_ref[...]
        g = g_ref[...]
        cw = cw_ref[...]
        prev8 = prev_ref[...] * (i < ns - 1).astype(F32)
        c = _conv(a, prev8, cw, cb_ref[...])
        gel, vjp = jax.vjp(jax.nn.gelu, c)
        dact = dact_ref[...]
        (dc,) = vjp(dact * g)
        up1 = _shift_up(dc, halo[...], 1)
        up2 = _shift_up(dc, halo[...], 2)
        da = cw[2:3, :] * dc + cw[1:2, :] * up1 + cw[0:1, :] * up2
        dup_ref[:, :DFF] = da.astype(BF16)
        dup_ref[:, DFF:] = (dact * gel).astype(BF16)
        dcw_ref[0:1, :] += jnp.sum(a * up2, axis=0, keepdims=True)
        dcw_ref[1:2, :] += jnp.sum(a * up1, axis=0, keepdims=True)
        dcw_ref[2:3, :] += jnp.sum(a * dc, axis=0, keepdims=True)
        dcb_ref[...] += jnp.sum(dc, axis=0, keepdims=True)
        halo[...] = dc[0:8, :]

    rev = lambda col: (lambda i: (ns - 1 - i, col))
    prev_map = lambda i: (jnp.maximum((ns - 1 - i) * (ts // 8) - 1, 0), 0)
    return _call(body, name="conv_bwd", grid=(ns,),
                 in_specs=[pl.BlockSpec((ts, DFF), rev(0)), pl.BlockSpec((ts, DFF), rev(1)),
                           pl.BlockSpec((8, DFF), prev_map), pl.BlockSpec((3, DFF), lambda i: (0, 0)),
                           pl.BlockSpec((1, DFF), lambda i: (0, 0)), pl.BlockSpec((ts, DFF), rev(0))],
                 out_specs=[pl.BlockSpec((ts, 2 * DFF), rev(0)), pl.BlockSpec((3, DFF), lambda i: (0, 0)),
                            pl.BlockSpec((1, DFF), lambda i: (0, 0))],
                 out_shape=[jax.ShapeDtypeStruct((s, 2 * DFF), BF16), jax.ShapeDtypeStruct((3, DFF), F32),
                            jax.ShapeDtypeStruct((1, DFF), F32)],
                 scratch_shapes=[pltpu.VMEM((8, DFF), F32)],
                 compiler_params=_params("arbitrary"))(up, up, up, cw, cb, dact)


def _blockdiag_expand(m):
    m4 = m.reshape(S5_H, 2, S5_G, S5_P)
    eye = jnp.eye(S5_G, dtype=bool)[:, None, None, :, None]
    return jnp.where(eye, m4[None], 0.0).reshape(S5_W, 2 * S5_N)


def _blockdiag_extract(mbd):
    m5 = mbd.reshape(S5_G, S5_H, 2, S5_G, S5_P)
    diag = jnp.stack([m5[g, :, :, g, :] for g in range(S5_G)], axis=2)
    return diag.reshape(S5_H, 2 * S5_N)


def _c_expand(c_re, c_im):
    c4 = jnp.stack([c_re, c_im], axis=2)
    eye = jnp.eye(S5_G, dtype=bool)[:, None, None, :, None]
    return jnp.where(eye, c4[:, :, :, None, :], 0.0).reshape(S5_W, 2 * S5_N)


def _c_extract(cbd):
    m5 = cbd.reshape(S5_G, S5_H, 2, S5_G, S5_P)
    d = jnp.stack([m5[g, :, :, g, :] for g in range(S5_G)], axis=0)
    return d[:, :, 0, :], d[:, :, 1, :]


def _glu_expand(w):
    eye = jnp.eye(S5_G, dtype=bool)[:, None, :, None]
    return jnp.where(eye, w[:, :, None, :], 0.0).reshape(S5_W, S5_W)


def _glu_extract(wbd):
    m4 = wbd.reshape(S5_G, S5_H, S5_G, S5_H)
    return jnp.stack([m4[g, :, g, :] for g in range(S5_G)], axis=0)


SMALL = ("b_f", "gm_ln_g", "gm_ln_b", "gm_w_s", "gm_b_s", "s5_lam_re", "s5_lam_im", "s5_log_dt", "s5_b_re", "s5_b_im",
         "s5_c_re", "s5_c_im", "s5_d", "s5_w_glu", "s5_b_glu", "ln1_g", "ln1_b", "conv_b", "ln2_g", "ln2_b")


def _layer_operands(sp, l):
    f = {}
    f["bf"] = jnp.pad(sp["b_f"][l][None, :], ((0, 0), (0, 128 - FX_H)))
    f["gm_lg"] = sp["gm_ln_g"][l].reshape(1, GM_W)
    f["gm_lb"] = sp["gm_ln_b"][l].reshape(1, GM_W)
    f["gm_ws"] = sp["gm_w_s"][l]
    f["gm_bst"] = sp["gm_b_s"][l].T
    f["lr"] = sp["s5_lam_re"][l].reshape(1, S5_N)
    f["li"] = sp["s5_lam_im"][l].reshape(1, S5_N)
    f["ldt"] = jnp.repeat(sp["s5_log_dt"][l], S5_P).reshape(1, S5_N)
    bt = lambda b: jnp.transpose(b, (2, 0, 1)).reshape(S5_H, S5_N)
    f["bt"] = jnp.concatenate([bt(sp["s5_b_re"][l]), bt(sp["s5_b_im"][l])], axis=1)
    f["cbd"] = _c_expand(sp["s5_c_re"][l], sp["s5_c_im"][l])
    f["drow"] = sp["s5_d"][l].reshape(1, S5_W)
    f["wg"] = _glu_expand(sp["s5_w_glu"][l])
    f["bg"] = sp["s5_b_glu"][l].reshape(1, S5_W)
    for n in ("ln1_g", "ln1_b", "ln2_g", "ln2_b"):
        f[n] = sp[n][l][None, :]
    f["cb"] = sp["conv_b"][l][None, :]
    return f


def local_step(x, target, mods, big, sp):
    saved = []
    for l in range(DEPTH):
        f = _layer_operands(sp, l)
        w = big[l]
        mod = mods[l]
        p = mm_nn(x, w["w_in"], NP, "in_proj", mod=mod, rows=(0, 1))
        ygm = gm_fwd(p, f["gm_lg"], f["gm_lb"], f["gm_ws"], f["gm_bst"])
        arow, bbt = s5_prep_fwd(f["lr"], f["li"], f["ldt"], f["bt"])
        bbd = _blockdiag_expand(bbt)
        ys5, st = s5_fwd(p, arow, bbd, f["cbd"], f["drow"], f["wg"], f["bg"])
        qa, ka, qat, kat, vt = fox_prep(p, f["bf"])
        yfx, lse = attn(qat, ka, vt)
        mixcat = jnp.concatenate([ygm, ys5, yfx], axis=1)
        mix = mm_nn(mixcat, w["w_out"], D, "out_proj")
        x1 = post_fwd(x, mix, mod, 2, f["ln1_g"], f["ln1_b"], "post1_fwd")
        up = mm_nn(x1, w["w_up"], DFF // 2, "up_proj", mod=mod, rows=(3, 4))
        act = conv_fwd(up, w["conv_w"], f["cb"])
        ffn = mm_nn(act, w["w_down"], D, "down_proj")
        x2 = post_fwd(x1, ffn, mod, 5, f["ln2_g"], f["ln2_b"], "post2_fwd")
        saved.append(dict(f=f, x=x, p=p, arow=arow, bbd=bbd, st=st, qa=qa, ka=ka, qat=qat, kat=kat, yfx=yfx, lse=lse,
                          mixcat=mixcat, mix=mix, x1=x1, up=up, act=act, ffn=ffn))
        x = x2

    loss_tile, dx = loss_kernel(x, target)

    gbig = [None] * DEPTH
    gsm = {n: [None] * DEPTH for n in SMALL}
    dmods = [None] * DEPTH
    for l in reversed(range(DEPTH)):
        sv = saved[l]
        f = sv["f"]
        w = big[l]
        mod = mods[l]
        s = x.shape[0]
        dx1, dffn, dg2, dlg2, dlb2 = post_bwd(sv["x1"], sv["ffn"], mod, 5, f["ln2_g"], f["ln2_b"], dx, "post2_bwd")
        g_down = mm_tn(sv["act"], dffn, D // 2, "down_dw")
        dact = mm_nt(dffn, w["w_down"], "down_dx")
        dup, dcw, dcb = conv_bwd(sv["up"], w["conv_w"], f["cb"], dact)
        g_up = mm_tn(sv["x1"], dup, DFF // 2, "up_dw", mod=mod, rows=(3, 4))
        dx1, dsh2, dsc2 = mm_nt_mod(dup, w["w_up"], sv["x1"], dx1, mod, (3, 4), "up_dx")
        dx0, dmix, dg1, dlg1, dlb1 = post_bwd(sv["x"], sv["mix"], mod, 2, f["ln1_g"], f["ln1_b"], dx1, "post1_bwd")
        g_out = mm_tn(sv["mixcat"], dmix, D, "out_dw")
        dmc = mm_nt(dmix, w["w_out"], "out_dx")
        duv, dgm_lg, dgm_lb, dgm_ws, dgm_bst = gm_bwd(sv["p"], f["gm_lg"], f["gm_lb"], f["gm_ws"], f["gm_bst"], dmc)
        du5, da, dbbd, dcbd, dd5, dwg, dbg = s5_bwd(sv["p"], sv["st"], sv["arow"], sv["bbd"], f["cbd"], f["drow"],
                                                    f["wg"], f["bg"], dmc)
        dlr, dli, dldt, dbt = s5_prep_bwd(f["lr"], f["li"], f["ldt"], f["bt"], da, _blockdiag_extract(dbbd))
        dq, dk, dv, dfq, dfk = attn_grad(sv["qa"], sv["qat"], sv["ka"], sv["kat"], sv["p"], sv["yfx"], sv["lse"], dmc)
        dff, dbf = fox_prep_grad(sv["p"], f["bf"], dfq, dfk)
        dp = jnp.concatenate([duv, du5, dq, dk, dv, dff], axis=1)
        g_in = mm_tn(sv["x"], dp, NP, "in_dw", mod=mod, rows=(0, 1))
        dx, dsh1, dsc1 = mm_nt_mod(dp, w["w_in"], sv["x"], dx0, mod, (0, 1), "in_dx")

        gbig[l] = dict(w_in=g_in, w_out=g_out, w_up=g_up, w_down=g_down, conv_w=dcw)
        dmods[l] = jnp.concatenate([dsh1, dsc1, dg1, dsh2, dsc2, dg2], axis=0)
        dc_re, dc_im = _c_extract(dcbd)
        dbt4 = dbt.reshape(S5_H, 2, S5_G, S5_P)
        vals = dict(b_f=dbf[0, :FX_H], gm_ln_g=dgm_lg.reshape(GM_H, HD), gm_ln_b=dgm_lb.reshape(GM_H, HD),
                    gm_w_s=dgm_ws, gm_b_s=dgm_bst.T, s5_lam_re=dlr.reshape(S5_G, S5_P),
                    s5_lam_im=dli.reshape(S5_G, S5_P), s5_log_dt=dldt[0, :S5_G],
                    s5_b_re=jnp.transpose(dbt4[:, 0], (1, 2, 0)), s5_b_im=jnp.transpose(dbt4[:, 1], (1, 2, 0)),
                    s5_c_re=dc_re, s5_c_im=dc_im, s5_d=dd5.reshape(S5_G, S5_H), s5_w_glu=_glu_extract(dwg),
                    s5_b_glu=dbg.reshape(S5_G, S5_H), ln1_g=dlg1[0], ln1_b=dlb1[0], conv_b=dcb[0],
                    ln2_g=dlg2[0], ln2_b=dlb2[0])
        for n in SMALL:
            gsm[n][l] = vals[n]
    gsm = {n: jnp.stack(v) for n, v in gsm.items()}
    return loss_tile, dx, gbig, gsm, jnp.stack(dmods)


def _my_index():
    return 4 * lax.axis_index("x") + 2 * lax.axis_index("y") + lax.axis_index("c")


def exchange(tensors, scatter, name):
    n = len(tensors)

    def body(*refs):
        ins, outs = refs[:n], refs[n:2 * n]
        send_sems, recv_sems, local_sems = refs[2 * n:]
        x, y, c = lax.axis_index("x"), lax.axis_index("y"), lax.axis_index("c")
        me = 4 * x + 2 * y + c
        local = []
        for t in range(n):
            cp = pltpu.make_async_copy(ins[t].at[me] if scatter else ins[t], outs[t].at[me], local_sems.at[t])
            cp.start()
            local.append(cp)
        remote = []
        for m in range(1, NDEV):
            px = 1 - x if m & 4 else x
            py = 1 - y if m & 2 else y
            pc = 1 - c if m & 1 else c
            peer = 4 * px + 2 * py + pc
            for t in range(n):
                k = t * (NDEV - 1) + m - 1
                cp = pltpu.make_async_remote_copy(
                    src_ref=ins[t].at[peer] if scatter else ins[t], dst_ref=outs[t].at[me],
                    send_sem=send_sems.at[k], recv_sem=recv_sems.at[k],
                    device_id=(px, py, pc), device_id_type=MESH_IDS)
                cp.start()
                remote.append(cp)
        for cp in remote:
            cp.wait()
        for cp in local:
            cp.wait()

    hbm = pl.BlockSpec(memory_space=pltpu.HBM)
    out_shape = [jax.ShapeDtypeStruct(t.shape if scatter else (NDEV,) + t.shape, t.dtype) for t in tensors]
    return _call(body, name=name, in_specs=[hbm] * n, out_specs=[hbm] * n, out_shape=out_shape,
                 scratch_shapes=[pltpu.SemaphoreType.DMA((n * (NDEV - 1),)), pltpu.SemaphoreType.DMA((n * (NDEV - 1),)),
                                 pltpu.SemaphoreType.DMA((n,))])(*tensors)


def mod_slices(c_all, w_ada, b_loc):
    nl, _, nc = w_ada.shape

    def body(c_ref, w_ref, b_ref, o_ref):
        cv = c_ref[...]
        o_ref[0] = _nn(cv * jax.nn.sigmoid(cv), w_ref[0]) + b_ref[0]

    return _call(body, name="mod_slices", grid=(nl,),
                 in_specs=[pl.BlockSpec((NDEV, D), lambda l: (0, 0)), pl.BlockSpec((1, D, nc), lambda l: (l, 0, 0)),
                           pl.BlockSpec((1, 1, nc), lambda l: (l, 0, 0))],
                 out_specs=pl.BlockSpec((1, NDEV, nc), lambda l: (l, 0, 0)),
                 out_shape=jax.ShapeDtypeStruct((nl, NDEV, nc), F32),
                 compiler_params=_params("arbitrary"))(c_all, w_ada, b_loc.reshape(nl, 1, nc))


def ada_grad(c_all, dm_loc):
    nl, _, nc = dm_loc.shape

    def body(c_ref, d_ref, o_ref):
        cv = c_ref[...]
        o_ref[0] = _tn(cv * jax.nn.sigmoid(cv), d_ref[0])

    return _call(body, name="ada_grad", grid=(nl,),
                 in_specs=[pl.BlockSpec((NDEV, D), lambda l: (0, 0)), pl.BlockSpec((1, NDEV, nc), lambda l: (l, 0, 0))],
                 out_specs=pl.BlockSpec((1, D, nc), lambda l: (l, 0, 0)),
                 out_shape=jax.ShapeDtypeStruct((nl, D, nc), F32),
                 compiler_params=_params("arbitrary"))(c_all, dm_loc)


def sum_chunks(chunks):
    r = chunks.shape[1]

    def body(c_ref, o_ref):
        acc = c_ref[0]
        for i in range(1, NDEV):
            acc = acc + c_ref[i]
        o_ref[...] = acc

    return _call(body, name="sum_chunks", out_shape=jax.ShapeDtypeStruct((r, 128), F32))(chunks)


def _row_tile(r):
    if r <= 256:
        return r
    for t in range(256, 7, -8):
        if r % t == 0:
            return t
    return r


def adamw(w, m, v, g=None, chunks=None, name="adamw"):
    r, cdim = w.shape
    tr = _row_tile(r)
    bc1 = 1.0 - ADAM_B1 ** ADAM_STEP
    bc2 = 1.0 - ADAM_B2 ** ADAM_STEP

    def body(g_ref, w_ref, m_ref, v_ref, go_ref, d_ref, mo_ref, vo_ref):
        if chunks is None:
            grad = g_ref[...]
        else:
            grad = g_ref[0].astype(F32)
            for i in range(1, NDEV):
                grad = grad + g_ref[i].astype(F32)
        mn = ADAM_B1 * m_ref[...] + (1.0 - ADAM_B1) * grad
        vn = ADAM_B2 * v_ref[...] + (1.0 - ADAM_B2) * (grad * grad)
        m_hat = mn / bc1
        v_hat = vn / bc2
        go_ref[...] = grad
        d_ref[...] = -ADAM_LR * (m_hat / (jnp.sqrt(v_hat) + ADAM_EPS) + ADAM_WD * w_ref[...])
        mo_ref[...] = mn
        vo_ref[...] = vn

    tile = pl.BlockSpec((tr, cdim), lambda i: (i, 0))
    gspec = tile if chunks is None else pl.BlockSpec((NDEV, tr, cdim), lambda i: (0, i, 0))
    shp = jax.ShapeDtypeStruct((r, cdim), F32)
    return _call(body, name=name, grid=(r // tr,), in_specs=[gspec, tile, tile, tile],
                 out_specs=[tile] * 4, out_shape=[shp] * 4,
                 compiler_params=_params("arbitrary"))(g if chunks is None else chunks, w, m, v)


WEIGHTS = ("w_ada", "b_ada", "w_in", "b_f", "gm_ln_g", "gm_ln_b", "gm_w_s", "gm_b_s", "s5_lam_re", "s5_lam_im",
           "s5_log_dt", "s5_b_re", "s5_b_im", "s5_c_re", "s5_c_im", "s5_d", "s5_w_glu", "s5_b_glu", "w_out", "ln1_g",
           "ln1_b", "w_up", "conv_w", "conv_b", "w_down", "ln2_g", "ln2_b")
SHARDED = ("w_in", "w_out", "w_up", "w_down", "conv_w")
COL_SHARDED = ("w_in", "w_up", "conv_w")
PACKED = ("b_ada",) + SMALL
PACK_ALIGN = NDEV * 8 * 128


def _gather_cols(g):
    nd, nl, r, c = g.shape
    return jnp.transpose(g, (1, 2, 0, 3)).reshape(nl, r, nd * c)


def _gather_rows(g):
    nd, nl, r, c = g.shape
    return jnp.transpose(g, (1, 0, 2, 3)).reshape(nl, nd * r, c)


def _chunk_cols(g):
    nl, r, c8 = g.shape
    return jnp.transpose(g.reshape(nl, r, NDEV, c8 // NDEV), (2, 0, 1, 3))


def _chunk_rows(g):
    nl, r8, c = g.shape
    return jnp.transpose(g.reshape(nl, NDEV, r8 // NDEV, c), (1, 0, 2, 3))


def _pack(parts):
    flat = jnp.concatenate([parts[n].reshape(-1) for n in PACKED])
    return jnp.pad(flat, (0, -flat.shape[0] % PACK_ALIGN))


def _unpack(flat, shapes):
    out, off = {}, 0
    for n in PACKED:
        size = math.prod(shapes[n])
        out[n] = flat[off:off + size].reshape(shapes[n])
        off += size
    return out


def kernel(x, c, w_ada, b_ada, w_in, b_f, gm_ln_g, gm_ln_b, gm_w_s, gm_b_s, s5_lam_re, s5_lam_im, s5_log_dt, s5_b_re, s5_b_im, s5_c_re, s5_c_im, s5_d, s5_w_glu, s5_b_glu, w_out, ln1_g, ln1_b, w_up, conv_w, conv_b, w_down, ln2_g, ln2_b, loss_target, m_w_ada, m_b_ada, m_w_in, m_b_f, m_gm_ln_g, m_gm_ln_b, m_gm_w_s, m_gm_b_s, m_s5_lam_re, m_s5_lam_im, m_s5_log_dt, m_s5_b_re, m_s5_b_im, m_s5_c_re, m_s5_c_im, m_s5_d, m_s5_w_glu, m_s5_b_glu, m_w_out, m_ln1_g, m_ln1_b, m_w_up, m_conv_w, m_conv_b, m_w_down, m_ln2_g, m_ln2_b, v_w_ada, v_b_ada, v_w_in, v_b_f, v_gm_ln_g, v_gm_ln_b, v_gm_w_s, v_gm_b_s, v_s5_lam_re, v_s5_lam_im, v_s5_log_dt, v_s5_b_re, v_s5_b_im, v_s5_c_re, v_s5_c_im, v_s5_d, v_s5_w_glu, v_s5_b_glu, v_w_out, v_ln1_g, v_ln1_b, v_w_up, v_conv_w, v_conv_b, v_w_down, v_ln2_g, v_ln2_b):
    given = dict(locals())
    wts = {n: given[n] for n in WEIGHTS}
    mom = {n: given["m_" + n] for n in WEIGHTS}
    var = {n: given["v_" + n] for n in WEIGHTS}
    nl = w_ada.shape[0]
    me = _my_index()
    ada_cols = w_ada.shape[2]

    (c_all,) = exchange([c], False, "gather_c")
    c_all = c_all.reshape(NDEV, D)
    b_loc = lax.dynamic_slice_in_dim(b_ada, me * ada_cols, ada_cols, axis=1)
    mod_part = mod_slices(c_all, w_ada, b_loc)

    gathered = exchange([mod_part] + [wts[n] if n == "conv_w" else wts[n].astype(BF16) for n in SHARDED],
                        False, "gather_weights")
    mod_mine = lax.dynamic_index_in_dim(gathered[0], me, axis=2, keepdims=False)
    mods = jnp.transpose(mod_mine, (1, 0, 2)).reshape(nl, 6, D)
    mods = jnp.pad(mods, ((0, 0), (0, 2), (0, 0)))
    full = {n: (_gather_cols if n in COL_SHARDED else _gather_rows)(g) for n, g in zip(SHARDED, gathered[1:])}
    full["w_in"] = jnp.pad(full["w_in"], ((0, 0), (0, 0), (0, NP - D_IN)))
    big = [{n: full[n][l] for n in SHARDED} for l in range(nl)]

    loss_tile, gx, gbig, gsm, dmods = local_step(x[0], loss_target[0], mods, big, {n: wts[n] for n in SMALL})

    stacked = {n: jnp.stack([gbig[l][n] for l in range(nl)]) for n in SHARDED}
    stacked["w_in"] = stacked["w_in"][:, :, :D_IN]
    chunked = [(_chunk_cols if n in COL_SHARDED else _chunk_rows)(stacked[n]) for n in SHARDED]
    chunked = [g if n == "conv_w" else g.astype(BF16) for n, g in zip(SHARDED, chunked)]
    gsm["b_ada"] = dmods.reshape(nl, 6 * D)
    packed = _pack(gsm).reshape(NDEV, -1, 128)
    received = exchange(chunked + [packed], True, "scatter_grads")
    small_sum = sum_chunks(received[-1])
    small_all, dmod_all = exchange([small_sum, dmods.reshape(nl, 6 * D)], False, "gather_small")

    out = {}
    for n, chunks in zip(SHARDED, received[:-1]):
        shp = wts[n].shape
        two_d = lambda a: a.reshape(shp[0] * shp[1], shp[2])
        res = adamw(two_d(wts[n]), two_d(mom[n]), two_d(var[n]),
                    chunks=chunks.reshape(NDEV, shp[0] * shp[1], shp[2]), name="adamw_" + n)
        out[n] = [r.reshape(shp) for r in res]

    dm_loc = lax.dynamic_slice_in_dim(dmod_all, me * ada_cols, ada_cols, axis=2)
    g_ada = ada_grad(c_all, jnp.transpose(dm_loc, (1, 0, 2)))
    two_d = lambda a: a.reshape(nl * D, ada_cols)
    res = adamw(two_d(w_ada), two_d(m_w_ada), two_d(v_w_ada), g=two_d(g_ada), name="adamw_w_ada")
    out["w_ada"] = [r.reshape(w_ada.shape) for r in res]

    shapes = {n: wts[n].shape for n in PACKED}
    as_rows = lambda parts: _pack(parts).reshape(-1, 128)
    res = adamw(as_rows(wts), as_rows(mom), as_rows(var), g=small_all.reshape(-1, 128), name="adamw_small")
    unpacked = [_unpack(r.reshape(-1), shapes) for r in res]
    for n in PACKED:
        out[n] = [u[n] for u in unpacked]

    loss = lax.psum(loss_tile[0, 0], ("x", "y", "c"))
    return (loss, gx[None], *[out[n][0] for n in WEIGHTS], *[out[n][1] for n in WEIGHTS],
            *[out[n][2] for n in WEIGHTS], *[out[n][3] for n in WEIGHTS])
```

```python
import functools
import math

import jax
import jax.numpy as jnp
from jax import lax
from jax.experimental import pallas as pl
from jax.experimental.pallas import tpu as pltpu

F32 = jnp.float32
BF16 = jnp.bfloat16
MESH_IDS = pl.DeviceIdType.MESH

D = 1024
SEQ = 4096
DEPTH = 4
NDEV = 8
HD = 64
GM_W = 256
GM_H = 4
GM_C = 128
S5_W = 256
S5_G = 16
S5_H = 16
S5_P = 64
S5_N = S5_G * S5_P
FX_W = 512
FX_H = 8
D_IN = 2 * GM_W + S5_W + 3 * FX_W + FX_H
NP = 2432
FF_COL = 2304
DFF = 2816
LN_EPS = 1e-5
DN_ALPHA = (2.0 * DEPTH) ** 0.25
NEG_INF = -1e30
ADAM_LR = 0.001
ADAM_B1 = 0.9
ADAM_B2 = 0.999
ADAM_EPS = 1e-08
ADAM_WD = 0.01
ADAM_STEP = 10

V7X_VMEM_LIMIT = 56 * 1024 * 1024
TS = 512
TS_C = 256
T_S5 = 256
TQ = 512
ATTN_HEADS = 2


def _call(body, **kw):
    return pl.pallas_call(body, **kw)


def _params(*sem):
    return pltpu.CompilerParams(dimension_semantics=sem if sem else None,
                                vmem_limit_bytes=V7X_VMEM_LIMIT)


def _nn(a, b):
    return jnp.dot(a.astype(BF16), b.astype(BF16), preferred_element_type=F32)


def _nt(a, b):
    return lax.dot_general(a.astype(BF16), b.astype(BF16), (((1,), (1,)), ((), ())),
                           preferred_element_type=F32)


def _tn(a, b):
    return lax.dot_general(a.astype(BF16), b.astype(BF16), (((0,), (0,)), ((), ())),
                           preferred_element_type=F32)


@jax.custom_vjp
def _bdot(a, b):
    return _nn(a, b)


def _bdot_fwd(a, b):
    return _nn(a, b), (a, b)


def _bdot_bwd(res, g):
    a, b = res
    return _nt(g, b), _tn(a, g)


_bdot.defvjp(_bdot_fwd, _bdot_bwd)


@jax.custom_vjp
def _bdot_nt(a, b):
    return _nt(a, b)


def _bdot_nt_fwd(a, b):
    return _nt(a, b), (a, b)


def _bdot_nt_bwd(res, g):
    a, b = res
    return _nn(g, b), _tn(g, a)


_bdot_nt.defvjp(_bdot_nt_fwd, _bdot_nt_bwd)


def _ln(r, g, b):
    mu = jnp.mean(r, axis=-1, keepdims=True)
    xc = r - mu
    var = jnp.mean(xc * xc, axis=-1, keepdims=True)
    return xc * lax.rsqrt(var + LN_EPS) * g + b


def _rows(shape):
    return lax.broadcasted_iota(jnp.int32, shape, 0)


def _lanes(shape):
    return lax.broadcasted_iota(jnp.int32, shape, 1)


def mm_nn(a, w, tn, name, mod=None, rows=None, out_dtype=F32):
    s, k = a.shape
    n = w.shape[1]
    ts = min(TS, s)

    def body(*refs):
        if mod is None:
            a_ref, w_ref, o_ref = refs
            h = a_ref[...]
        else:
            a_ref, m_ref, w_ref, o_ref = refs
            h = a_ref[...] * (1.0 + m_ref[rows[1]:rows[1] + 1, :]) + m_ref[rows[0]:rows[0] + 1, :]
        o_ref[...] = jnp.dot(h.astype(BF16), w_ref[...], preferred_element_type=F32).astype(out_dtype)

    in_specs = [pl.BlockSpec((ts, k), lambda j, i: (i, 0))]
    args = [a]
    if mod is not None:
        in_specs.append(pl.BlockSpec((8, k), lambda j, i: (0, 0)))
        args.append(mod)
    in_specs.append(pl.BlockSpec((k, tn), lambda j, i: (0, j)))
    args.append(w)
    return _call(body, name=name, grid=(n // tn, s // ts), in_specs=in_specs,
                 out_specs=pl.BlockSpec((ts, tn), lambda j, i: (i, j)),
                 out_shape=jax.ShapeDtypeStruct((s, n), out_dtype),
                 compiler_params=_params("arbitrary", "arbitrary"))(*args)


def mm_nt(dy, w, name):
    s, n = dy.shape
    k = w.shape[0]
    ts = min(TS, s)

    def body(dy_ref, w_ref, o_ref):
        o_ref[...] = _nt(dy_ref[...], w_ref[...])

    return _call(body, name=name, grid=(s // ts,),
                 in_specs=[pl.BlockSpec((ts, n), lambda i: (i, 0)),
                           pl.BlockSpec((k, n), lambda i: (0, 0))],
                 out_specs=pl.BlockSpec((ts, k), lambda i: (i, 0)),
                 out_shape=jax.ShapeDtypeStruct((s, k), F32),
                 compiler_params=_params("arbitrary"))(dy, w)


def mm_nt_mod(dy, w, x, dres, mod, rows, name):
    s, n = dy.shape
    k = w.shape[0]
    ts = min(TS, s)

    def body(dy_ref, w_ref, x_ref, r_ref, m_ref, dx_ref, dsh_ref, dsc_ref):
        @pl.when(pl.program_id(0) == 0)
        def _():
            dsh_ref[...] = jnp.zeros_like(dsh_ref)
            dsc_ref[...] = jnp.zeros_like(dsc_ref)

        dh = _nt(dy_ref[...], w_ref[...])
        dx_ref[...] = r_ref[...] + dh * (1.0 + m_ref[rows[1]:rows[1] + 1, :])
        dsh_ref[...] += jnp.sum(dh, axis=0, keepdims=True)
        dsc_ref[...] += jnp.sum(dh * x_ref[...], axis=0, keepdims=True)

    row = pl.BlockSpec((1, k), lambda i: (0, 0))
    tile = pl.BlockSpec((ts, k), lambda i: (i, 0))
    return _call(body, name=name, grid=(s // ts,),
                 in_specs=[pl.BlockSpec((ts, n), lambda i: (i, 0)),
                           pl.BlockSpec((k, n), lambda i: (0, 0)), tile, tile,
                           pl.BlockSpec((8, k), lambda i: (0, 0))],
                 out_specs=[tile, row, row],
                 out_shape=[jax.ShapeDtypeStruct((s, k), F32),
                            jax.ShapeDtypeStruct((1, k), F32),
                            jax.ShapeDtypeStruct((1, k), F32)],
                 compiler_params=_params("arbitrary"))(dy, w, x, dres, mod)


def mm_tn(a, dy, tn, name, mod=None, rows=None):
    s, k = a.shape
    n = dy.shape[1]
    ts = min(TS, s)

    def body(*refs):
        if mod is None:
            a_ref, dy_ref, o_ref = refs
            h = a_ref[...]
        else:
            a_ref, m_ref, dy_ref, o_ref = refs
            h = a_ref[...] * (1.0 + m_ref[rows[1]:rows[1] + 1, :]) + m_ref[rows[0]:rows[0] + 1, :]

        @pl.when(pl.program_id(1) == 0)
        def _():
            o_ref[...] = jnp.zeros_like(o_ref)

        o_ref[...] += _tn(h, dy_ref[...])

    in_specs = [pl.BlockSpec((ts, k), lambda j, i: (i, 0))]
    args = [a]
    if mod is not None:
        in_specs.append(pl.BlockSpec((8, k), lambda j, i: (0, 0)))
        args.append(mod)
    in_specs.append(pl.BlockSpec((ts, tn), lambda j, i: (i, j)))
    args.append(dy)
    return _call(body, name=name, grid=(n // tn, s // ts), in_specs=in_specs,
                 out_specs=pl.BlockSpec((k, tn), lambda j, i: (0, j)),
                 out_shape=jax.ShapeDtypeStruct((k, n), F32),
                 compiler_params=_params("arbitrary", "arbitrary"))(*args)


def _post_fn(x, br, gate, lg, lb):
    return _ln(DN_ALPHA * x + (1.0 + gate) * br, lg, lb)


def post_fwd(x, br, mod, grow, lg, lb, name):
    s = x.shape[0]
    ts = min(TS, s)

    def body(x_ref, b_ref, m_ref, lg_ref, lb_ref, o_ref):
        o_ref[...] = _post_fn(x_ref[...], b_ref[...], m_ref[grow:grow + 1, :], lg_ref[...], lb_ref[...])

    tile = pl.BlockSpec((ts, D), lambda i: (i, 0))
    row = pl.BlockSpec((1, D), lambda i: (0, 0))
    return _call(body, name=name, grid=(s // ts,),
                 in_specs=[tile, tile, pl.BlockSpec((8, D), lambda i: (0, 0)), row, row],
                 out_specs=tile, out_shape=jax.ShapeDtypeStruct((s, D), F32),
                 compiler_params=_params("arbitrary"))(x, br, mod, lg, lb)


def post_bwd(x, br, mod, grow, lg, lb, dy, name):
    s = x.shape[0]
    ts = min(TS, s)

    def body(x_ref, b_ref, m_ref, lg_ref, lb_ref, dy_ref, dx_ref, db_ref, dg_ref, dlg_ref, dlb_ref):
        @pl.when(pl.program_id(0) == 0)
        def _():
            dg_ref[...] = jnp.zeros_like(dg_ref)
            dlg_ref[...] = jnp.zeros_like(dlg_ref)
            dlb_ref[...] = jnp.zeros_like(dlb_ref)

        _, vjp = jax.vjp(_post_fn, x_ref[...], b_ref[...], m_ref[grow:grow + 1, :], lg_ref[...], lb_ref[...])
        dx, db, dg, dlg, dlb = vjp(dy_ref[...])
        dx_ref[...] = dx
        db_ref[...] = db.astype(BF16)
        dg_ref[...] += dg
        dlg_ref[...] += dlg
        dlb_ref[...] += dlb

    tile = pl.BlockSpec((ts, D), lambda i: (i, 0))
    row = pl.BlockSpec((1, D), lambda i: (0, 0))
    rs = jax.ShapeDtypeStruct((1, D), F32)
    return _call(body, name=name, grid=(s // ts,),
                 in_specs=[tile, tile, pl.BlockSpec((8, D), lambda i: (0, 0)), row, row, tile],
                 out_specs=[tile, tile, row, row, row],
                 out_shape=[jax.ShapeDtypeStruct((s, D), F32), jax.ShapeDtypeStruct((s, D), BF16), rs, rs, rs],
                 compiler_params=_params("arbitrary"))(x, br, mod, lg, lb, dy)


def loss_kernel(y, target):
    s = y.shape[0]
    ts = min(TS, s)

    def body(y_ref, t_ref, l_ref, dy_ref):
        @pl.when(pl.program_id(0) == 0)
        def _():
            l_ref[...] = jnp.zeros_like(l_ref)

        err = y_ref[...] - t_ref[...]
        dy_ref[...] = err * (1.0 / D)
        per_tok = jnp.mean(err * err, axis=-1, keepdims=True)
        l_ref[...] += 0.5 * jnp.sum(per_tok)

    tile = pl.BlockSpec((ts, D), lambda i: (i, 0))
    return _call(body, name="loss", grid=(s // ts,), in_specs=[tile, tile],
                 out_specs=[pl.BlockSpec((8, 128), lambda i: (0, 0)), tile],
                 out_shape=[jax.ShapeDtypeStruct((8, 128), F32), jax.ShapeDtypeStruct((s, D), F32)],
                 compiler_params=_params("arbitrary"))(y, target)


def _gm_head(u, v, lg, lb, w, bs):
    t = u.shape[0]
    causal = _rows((GM_C, GM_C)) >= _lanes((GM_C, GM_C))
    vn = _ln(v, lg, lb)
    wm = jnp.where(causal, w, 0.0)
    chunks = []
    for n in range(t // GM_C):
        rs = slice(n * GM_C, (n + 1) * GM_C)
        chunks.append(u[rs] * (_bdot(wm, vn[rs]) + bs))
    return jnp.concatenate(chunks, axis=0)


def gm_fwd(p, lg, lb, ws, bst):
    s = p.shape[0]
    ts = min(TS_C, s)

    def body(u_ref, v_ref, lg_ref, lb_ref, ws_ref, bs_ref, o_ref):
        for h in range(GM_H):
            sl = slice(h * HD, (h + 1) * HD)
            o_ref[:, sl] = _gm_head(u_ref[:, sl], v_ref[:, sl], lg_ref[:, sl], lb_ref[:, sl], ws_ref[h],
                                    bs_ref[:, h:h + 1]).astype(BF16)

    full = lambda shape: pl.BlockSpec(shape, lambda i: (0,) * len(shape))
    return _call(body, name="gm_fwd", grid=(s // ts,),
                 in_specs=[pl.BlockSpec((ts, GM_W), lambda i: (i, 0)), pl.BlockSpec((ts, GM_W), lambda i: (i, 1)),
                           full((1, GM_W)), full((1, GM_W)), full((GM_H, GM_C, GM_C)), full((GM_C, GM_H))],
                 out_specs=pl.BlockSpec((ts, GM_W), lambda i: (i, 0)),
                 out_shape=jax.ShapeDtypeStruct((s, GM_W), BF16),
                 compiler_params=_params("arbitrary"))(p, p, lg, lb, ws, bst)


def gm_bwd(p, lg, lb, ws, bst, dmix):
    s = p.shape[0]
    ts = min(TS_C, s)

    def body(u_ref, v_ref, lg_ref, lb_ref, ws_ref, bs_ref, dy_ref, duv_ref, dlg_ref, dlb_ref, dws_ref, dbs_ref):
        @pl.when(pl.program_id(0) == 0)
        def _():
            dlg_ref[...] = jnp.zeros_like(dlg_ref)
            dlb_ref[...] = jnp.zeros_like(dlb_ref)
            dws_ref[...] = jnp.zeros_like(dws_ref)
            dbs_ref[...] = jnp.zeros_like(dbs_ref)

        for h in range(GM_H):
            sl = slice(h * HD, (h + 1) * HD)
            _, vjp = jax.vjp(_gm_head, u_ref[:, sl], v_ref[:, sl], lg_ref[:, sl], lb_ref[:, sl], ws_ref[h],
                             bs_ref[:, h:h + 1])
            du, dv, dlg, dlb, dw, dbs = vjp(dy_ref[:, sl])
            duv_ref[:, sl] = du.astype(BF16)
            duv_ref[:, GM_W + h * HD:GM_W + (h + 1) * HD] = dv.astype(BF16)
            dlg_ref[:, sl] += dlg
            dlb_ref[:, sl] += dlb
            dws_ref[h] += dw
            dbs_ref[:, h:h + 1] += dbs

    full = lambda shape: pl.BlockSpec(shape, lambda i: (0,) * len(shape))
    return _call(body, name="gm_bwd", grid=(s // ts,),
                 in_specs=[pl.BlockSpec((ts, GM_W), lambda i: (i, 0)), pl.BlockSpec((ts, GM_W), lambda i: (i, 1)),
                           full((1, GM_W)), full((1, GM_W)), full((GM_H, GM_C, GM_C)), full((GM_C, GM_H)),
                           pl.BlockSpec((ts, GM_W), lambda i: (i, 0))],
                 out_specs=[pl.BlockSpec((ts, 2 * GM_W), lambda i: (i, 0)), full((1, GM_W)), full((1, GM_W)),
                            full((GM_H, GM_C, GM_C)), full((GM_C, GM_H))],
                 out_shape=[jax.ShapeDtypeStruct((s, 2 * GM_W), BF16), jax.ShapeDtypeStruct((1, GM_W), F32),
                            jax.ShapeDtypeStruct((1, GM_W), F32), jax.ShapeDtypeStruct((GM_H, GM_C, GM_C), F32),
                            jax.ShapeDtypeStruct((GM_C, GM_H), F32)],
                 compiler_params=_params("arbitrary"))(p, p, lg, lb, ws, bst, dmix)


def _s5_prep_fn(lr, li, ldt, bt):
    dt = jnp.exp(ldt)
    er = jnp.exp(lr * dt)
    ar = er * jnp.cos(li * dt)
    ai = er * jnp.sin(li * dt)
    den = lr * lr + li * li
    nr = ar - 1.0
    cr = (nr * lr + ai * li) / den
    ci = (ai * lr - nr * li) / den
    br, bi = bt[:, :S5_N], bt[:, S5_N:]
    return ar, ai, jnp.concatenate([cr * br - ci * bi, cr * bi + ci * br], axis=1)


def s5_prep_fwd(lr, li, ldt, bt):
    def body(lr_ref, li_ref, ldt_ref, bt_ref, a_ref, bb_ref):
        ar, ai, bb = _s5_prep_fn(lr_ref[...], li_ref[...], ldt_ref[...], bt_ref[...])
        a_ref[...] = jnp.concatenate([ar, ai, jnp.zeros((6, S5_N), F32)], axis=0)
        bb_ref[...] = bb

    return _call(body, name="s5_prep_fwd",
                 out_shape=[jax.ShapeDtypeStruct((8, S5_N), F32), jax.ShapeDtypeStruct((S5_H, 2 * S5_N), F32)])(lr, li, ldt, bt)


def s5_prep_bwd(lr, li, ldt, bt, da, dbb):
    def body(lr_ref, li_ref, ldt_ref, bt_ref, da_ref, dbb_ref, dlr_ref, dli_ref, dldt_ref, dbt_ref):
        _, vjp = jax.vjp(_s5_prep_fn, lr_ref[...], li_ref[...], ldt_ref[...], bt_ref[...])
        dlr, dli, dldt, dbt = vjp((da_ref[0:1, :], da_ref[1:2, :], dbb_ref[...]))
        dlr_ref[...] = dlr
        dli_ref[...] = dli
        dbt_ref[...] = dbt
        group = (_rows((S5_N, 128)) // S5_P == _lanes((S5_N, 128))).astype(F32)
        dldt_ref[...] = jnp.dot(jnp.broadcast_to(dldt, (8, S5_N)), group, precision=lax.Precision.HIGHEST,
                                preferred_element_type=F32)[0:1, :]

    r = jax.ShapeDtypeStruct((1, S5_N), F32)
    return _call(body, name="s5_prep_bwd",
                 out_shape=[r, r, jax.ShapeDtypeStruct((1, 128), F32),
                            jax.ShapeDtypeStruct((S5_H, 2 * S5_N), F32)])(lr, li, ldt, bt, da, dbb)


def _s5_out_fn(x, u, cbd, drow, wg, bg):
    y = _bdot_nt(x[:, :S5_N], cbd[:, :S5_N]) - _bdot_nt(x[:, S5_N:], cbd[:, S5_N:]) + drow * u
    y = jax.nn.gelu(y)
    gate = _bdot_nt(y, wg) + bg
    return y * jax.nn.sigmoid(gate)


def _scan_steps(t):
    return int(math.log2(t))


def s5_fwd(p, arow, bbd, cbd, drow, wg, bg):
    s = p.shape[0]
    t = min(T_S5, s)

    def body(u_ref, a_ref, bbd_ref, cbd_ref, d_ref, wg_ref, bg_ref, y_ref, st_ref, carry):
        @pl.when(pl.program_id(0) == 0)
        def _():
            carry[...] = jnp.zeros_like(carry)

        u = u_ref[...]
        bu = _nn(u, bbd_ref[...])
        ar, ai = a_ref[0:1, :], a_ref[1:2, :]
        cr, ci = carry[0:1, :S5_N], carry[0:1, S5_N:]
        rid = _rows((t, S5_N))
        first = rid == 0
        xr = bu[:, :S5_N] + jnp.where(first, ar * cr - ai * ci, 0.0)
        xi = bu[:, S5_N:] + jnp.where(first, ar * ci + ai * cr, 0.0)
        pr, pi = ar, ai
        for k in range(_scan_steps(t)):
            d = 1 << k
            keep = rid >= d
            sr = jnp.where(keep, pltpu.roll(xr, d, 0), 0.0)
            si = jnp.where(keep, pltpu.roll(xi, d, 0), 0.0)
            xr, xi = xr + pr * sr - pi * si, xi + pr * si + pi * sr
            pr, pi = pr * pr - pi * pi, 2.0 * pr * pi
        st_ref[:, :S5_N] = xr
        st_ref[:, S5_N:] = xi
        carry[0:1, :S5_N] = xr[t - 1:t, :]
        carry[0:1, S5_N:] = xi[t - 1:t, :]
        x = jnp.concatenate([xr, xi], axis=1)
        y_ref[...] = _s5_out_fn(x, u, cbd_ref[...], d_ref[...], wg_ref[...], bg_ref[...]).astype(BF16)

    full = lambda shape: pl.BlockSpec(shape, lambda i: (0,) * len(shape))
    return _call(body, name="s5_fwd", grid=(s // t,),
                 in_specs=[pl.BlockSpec((t, S5_W), lambda i: (i, 2)), full((8, S5_N)), full((S5_W, 2 * S5_N)),
                           full((S5_W, 2 * S5_N)), full((1, S5_W)), full((S5_W, S5_W)), full((1, S5_W))],
                 out_specs=[pl.BlockSpec((t, S5_W), lambda i: (i, 0)), pl.BlockSpec((t, 2 * S5_N), lambda i: (i, 0))],
                 out_shape=[jax.ShapeDtypeStruct((s, S5_W), BF16), jax.ShapeDtypeStruct((s, 2 * S5_N), F32)],
                 scratch_shapes=[pltpu.VMEM((8, 2 * S5_N), F32)],
                 compiler_params=_params("arbitrary"))(p, arow, bbd, cbd, drow, wg, bg)


def s5_bwd(p, st, arow, bbd, cbd, drow, wg, bg, dmix):
    s = p.shape[0]
    t = min(T_S5, s)
    nc = s // t

    def body(u_ref, st_ref, prev_ref, a_ref, bbd_ref, cbd_ref, d_ref, wg_ref, bg_ref, dy_ref,
             du_ref, da_ref, dbbd_ref, dcbd_ref, dd_ref, dwg_ref, dbg_ref, carry):
        i = pl.program_id(0)

        @pl.when(i == 0)
        def _():
            carry[...] = jnp.zeros_like(carry)
            for r in (da_ref, dbbd_ref, dcbd_ref, dd_ref, dwg_ref, dbg_ref):
                r[...] = jnp.zeros_like(r)

        u = u_ref[...]
        x = st_ref[...]
        _, vjp = jax.vjp(_s5_out_fn, x, u, cbd_ref[...], d_ref[...], wg_ref[...], bg_ref[...])
        dx, du1, dcbd, dd, dwg, dbg = vjp(dy_ref[...])
        ar, ai = a_ref[0:1, :], a_ref[1:2, :]
        cr, ci = carry[0:1, :S5_N], carry[0:1, S5_N:]
        rid = _rows((t, S5_N))
        last = rid == t - 1
        gr = dx[:, :S5_N] + jnp.where(last, ar * cr + ai * ci, 0.0)
        gi = dx[:, S5_N:] + jnp.where(last, ar * ci - ai * cr, 0.0)
        pr, pi = ar, -ai
        for k in range(_scan_steps(t)):
            d = 1 << k
            keep = rid < t - d
            sr = jnp.where(keep, pltpu.roll(gr, t - d, 0), 0.0)
            si = jnp.where(keep, pltpu.roll(gi, t - d, 0), 0.0)
            gr, gi = gr + pr * sr - pi * si, gi + pr * si + pi * sr
            pr, pi = pr * pr - pi * pi, 2.0 * pr * pi
        carry[0:1, :S5_N] = gr[0:1, :]
        carry[0:1, S5_N:] = gi[0:1, :]
        has_prev = (i < nc - 1).astype(F32)
        top_r = prev_ref[7:8, :S5_N] * has_prev
        top_i = prev_ref[7:8, S5_N:] * has_prev
        xpr = jnp.where(rid == 0, top_r, pltpu.roll(x[:, :S5_N], 1, 0))
        xpi = jnp.where(rid == 0, top_i, pltpu.roll(x[:, S5_N:], 1, 0))
        da_ref[0:1, :] += jnp.sum(xpr * gr + xpi * gi, axis=0, keepdims=True)
        da_ref[1:2, :] += jnp.sum(xpr * gi - xpi * gr, axis=0, keepdims=True)
        g = jnp.concatenate([gr, gi], axis=1)
        dbbd_ref[...] += _tn(u, g)
        du_ref[...] = (_nt(g, bbd_ref[...]) + du1).astype(BF16)
        dcbd_ref[...] += dcbd
        dd_ref[...] += dd
        dwg_ref[...] += dwg
        dbg_ref[...] += dbg

    full = lambda shape: pl.BlockSpec(shape, lambda i: (0,) * len(shape))
    rev = lambda col: (lambda i: (nc - 1 - i, col))
    prev_map = lambda i: (jnp.maximum((nc - 1 - i) * (t // 8) - 1, 0), 0)
    return _call(body, name="s5_bwd", grid=(nc,),
                 in_specs=[pl.BlockSpec((t, S5_W), rev(2)), pl.BlockSpec((t, 2 * S5_N), rev(0)),
                           pl.BlockSpec((8, 2 * S5_N), prev_map), full((8, S5_N)), full((S5_W, 2 * S5_N)),
                           full((S5_W, 2 * S5_N)), full((1, S5_W)), full((S5_W, S5_W)), full((1, S5_W)),
                           pl.BlockSpec((t, S5_W), rev(1))],
                 out_specs=[pl.BlockSpec((t, S5_W), rev(0)), full((8, S5_N)), full((S5_W, 2 * S5_N)),
                            full((S5_W, 2 * S5_N)), full((1, S5_W)), full((S5_W, S5_W)), full((1, S5_W))],
                 out_shape=[jax.ShapeDtypeStruct((s, S5_W), BF16), jax.ShapeDtypeStruct((8, S5_N), F32),
                            jax.ShapeDtypeStruct((S5_W, 2 * S5_N), F32), jax.ShapeDtypeStruct((S5_W, 2 * S5_N), F32),
                            jax.ShapeDtypeStruct((1, S5_W), F32), jax.ShapeDtypeStruct((S5_W, S5_W), F32),
                            jax.ShapeDtypeStruct((1, S5_W), F32)],
                 scratch_shapes=[pltpu.VMEM((8, 2 * S5_N), F32)],
                 compiler_params=_params("arbitrary"))(p, st, st, arow, bbd, cbd, drow, wg, bg, dmix)


def _cum_steps(s):
    return int(math.ceil(math.log2(s)))


V_BLK = (2 * GM_W + S5_W + 2 * FX_W) // 128


AUG = 2 * HD
BIAS_COL = HD
FQ_COL = HD + 3
PAIR_W = 256


def _split3(f):
    hi = f.astype(BF16).astype(F32)
    r = f - hi
    mid = r.astype(BF16).astype(F32)
    lo = (r - mid).astype(BF16).astype(F32)
    return hi, mid, lo


def fox_prep(p, bf):
    s = p.shape[0]
    ts = min(TS, s)
    scale = HD ** -0.5

    def body(q0_ref, q1_ref, k0_ref, k1_ref, v0_ref, v1_ref, f_ref, bf_ref,
             qa_ref, ka_ref, qat_ref, kat_ref, vt_ref, carry):
        @pl.when(pl.program_id(0) == 0)
        def _():
            carry[...] = jnp.zeros_like(carry)

        lane = _lanes((ts, 128))
        lf = jax.nn.log_sigmoid(f_ref[...] + bf_ref[...])
        acc = jnp.where(lane < FX_H, lf, 0.0)
        rid = _rows((ts, 128))
        for k in range(_cum_steps(ts)):
            d = 1 << k
            acc = acc + jnp.where(rid >= d, pltpu.roll(acc, d, 0), 0.0)
        acc = acc + carry[0:1, :]
        carry[0:1, :] = acc[ts - 1:ts, :]

        low = lane < HD
        for h in range(FX_H):
            blk, pos = divmod(h, 4)
            pair = slice((pos // 2) * 128, (pos // 2) * 128 + 128)
            hi, mid, lo = _split3(acc[:, h:h + 1])
            one = jnp.ones((ts, 1), F32)

            def augment(ref, cols):
                x = ref[:, pair]
                if pos % 2:
                    x = pltpu.roll(x, HD, 1)
                out = jnp.where(low, x, 0.0)
                for j, cval in enumerate(cols):
                    out = jnp.where(lane == HD + j, cval, out)
                return out

            qa = augment((q0_ref, q1_ref)[blk], (one, one, one, hi, mid, lo))
            qa = jnp.where(low, qa * scale, qa)
            ka = augment((k0_ref, k1_ref)[blk], (-hi, -mid, -lo, one, one, one))
            cs = slice(h * AUG, (h + 1) * AUG)
            qa_ref[:, cs] = qa.astype(BF16)
            ka_ref[:, cs] = ka.astype(BF16)
            qat_ref[cs, :] = jnp.transpose(qa).astype(BF16)
            kat_ref[cs, :] = jnp.transpose(ka).astype(BF16)
        for j in range(FX_H // 2):
            vref = (v0_ref, v1_ref)[j // 2]
            vt_ref[j * 128:(j + 1) * 128, :] = jnp.transpose(vref[:, (j % 2) * 128:(j % 2) * 128 + 128]).astype(BF16)

    q_blk = (2 * GM_W + S5_W) // PAIR_W
    col = lambda b: pl.BlockSpec((ts, PAIR_W), lambda i: (i, b))
    wide = FX_H * AUG
    return _call(body, name="fox_prep", grid=(s // ts,),
                 in_specs=[col(q_blk), col(q_blk + 1), col(q_blk + 2), col(q_blk + 3), col(q_blk + 4), col(q_blk + 5),
                           pl.BlockSpec((ts, 128), lambda i: (i, FF_COL // 128)), pl.BlockSpec((1, 128), lambda i: (0, 0))],
                 out_specs=[pl.BlockSpec((ts, wide), lambda i: (i, 0)), pl.BlockSpec((ts, wide), lambda i: (i, 0)),
                            pl.BlockSpec((wide, ts), lambda i: (0, i)), pl.BlockSpec((wide, ts), lambda i: (0, i)),
                            pl.BlockSpec((FX_W, ts), lambda i: (0, i))],
                 out_shape=[jax.ShapeDtypeStruct((s, wide), BF16), jax.ShapeDtypeStruct((s, wide), BF16),
                            jax.ShapeDtypeStruct((wide, s), BF16), jax.ShapeDtypeStruct((wide, s), BF16),
                            jax.ShapeDtypeStruct((FX_W, s), BF16)],
                 scratch_shapes=[pltpu.VMEM((8, 128), F32)],
                 compiler_params=_params("arbitrary"))(p, p, p, p, p, p, p, bf)


def fox_prep_grad(p, bf, dfq, dfk):
    s = p.shape[0]
    ts = min(TS, s)
    ns = s // ts

    def body(f_ref, bf_ref, dfq_ref, dfk_ref, df_ref, dbf_ref, carry):
        @pl.when(pl.program_id(0) == 0)
        def _():
            carry[...] = jnp.zeros_like(carry)
            dbf_ref[...] = jnp.zeros_like(dbf_ref)

        lane = _lanes((ts, 128))
        acc = jnp.zeros((ts, 128), F32)
        for h in range(FX_H):
            c = (h // 2) * 128 + h % 2
            acc = jnp.where(lane == h, dfq_ref[:, c:c + 1] + dfk_ref[:, c:c + 1], acc)
        rid = _rows((ts, 128))
        for k in range(_cum_steps(ts)):
            d = 1 << k
            acc = acc + jnp.where(rid < ts - d, pltpu.roll(acc, ts - d, 0), 0.0)
        acc = acc + carry[0:1, :]
        carry[0:1, :] = acc[0:1, :]
        z = f_ref[...] + bf_ref[...]
        df = jnp.where(lane < FX_H, acc * jax.nn.sigmoid(-z), 0.0)
        df_ref[...] = df.astype(BF16)
        dbf_ref[...] += jnp.sum(df, axis=0, keepdims=True)

    rev = lambda i: (ns - 1 - i, 0)
    return _call(body, name="fox_prep_grad", grid=(ns,),
                 in_specs=[pl.BlockSpec((ts, 128), lambda i: (ns - 1 - i, FF_COL // 128)),
                           pl.BlockSpec((1, 128), lambda i: (0, 0)),
                           pl.BlockSpec((ts, FX_W), rev), pl.BlockSpec((ts, FX_W), rev)],
                 out_specs=[pl.BlockSpec((ts, 128), rev), pl.BlockSpec((1, 128), lambda i: (0, 0))],
                 out_shape=[jax.ShapeDtypeStruct((s, 128), BF16), jax.ShapeDtypeStruct((1, 128), F32)],
                 scratch_shapes=[pltpu.VMEM((8, 128), F32)],
                 compiler_params=_params("arbitrary"))(p, bf, dfq, dfk)


def attn(qat, ka, vt):
    s = ka.shape[0]
    tq = min(TQ, s)
    nq = s // tq

    nh = ATTN_HEADS

    def body(qat_ref, ka_ref, vt_ref, o_ref, lse_ref):
        qi = pl.program_id(1)
        causal = _rows((tq, tq)) <= _lanes((tq, tq))
        lse_ref[...] = jnp.zeros_like(lse_ref)

        def step(kj, carry, masked):
            off = pl.multiple_of(kj * tq, tq)
            out = []
            for hh in range(nh):
                m, l, acc = carry[hh]
                st = jnp.dot(ka_ref[pl.ds(off, tq), hh * AUG:(hh + 1) * AUG], qat_ref[hh * AUG:(hh + 1) * AUG, :],
                             preferred_element_type=F32)
                if masked:
                    st = jnp.where(causal, st, NEG_INF)
                m_new = jnp.maximum(m, jnp.max(st, axis=0, keepdims=True))
                alpha = jnp.exp(m - m_new)
                pt = jnp.exp(st - m_new)
                v = vt_ref[hh * HD:(hh + 1) * HD, pl.ds(off, tq)]
                out.append((m_new, alpha * l + jnp.sum(pt, axis=0, keepdims=True),
                            alpha * acc + jnp.dot(v, pt.astype(BF16), preferred_element_type=F32)))
            return tuple(out)

        init = tuple((jnp.full((1, tq), NEG_INF, F32), jnp.zeros((1, tq), F32), jnp.zeros((HD, tq), F32))
                     for _ in range(nh))
        carry = lax.fori_loop(0, qi, lambda kj, c: step(kj, c, False), init)
        carry = step(qi, carry, True)
        for hh in range(nh):
            m, l, _ = carry[hh]
            lse_ref[hh // 2, hh % 2:hh % 2 + 1, :] = m + jnp.log(l)
        for j in range(nh // 2):
            pair = jnp.concatenate([carry[2 * j][2] / carry[2 * j][1], carry[2 * j + 1][2] / carry[2 * j + 1][1]], axis=0)
            o_ref[:, j * 128:(j + 1) * 128] = jnp.transpose(pair).astype(BF16)

    return _call(body, name="attn", grid=(FX_H // nh, nq),
                 in_specs=[pl.BlockSpec((nh * AUG, tq), lambda h, i: (h, i)),
                           pl.BlockSpec((s, nh * AUG), lambda h, i: (0, h)),
                           pl.BlockSpec((nh * HD, s), lambda h, i: (h, 0))],
                 out_specs=[pl.BlockSpec((tq, nh * HD), lambda h, i: (i, h)),
                            pl.BlockSpec((nh // 2, 8, tq), lambda h, i: (h, 0, i))],
                 out_shape=[jax.ShapeDtypeStruct((s, FX_W), BF16), jax.ShapeDtypeStruct((FX_H // 2, 8, s), F32)],
                 compiler_params=_params("arbitrary", "arbitrary"))(qat, ka, vt)


def attn_grad(qa, qat, ka, kat, p, o, lse, dmix):
    s = qa.shape[0]
    tq = min(TQ, s)
    nq = s // tq
    scale = HD ** -0.5

    def body(qa_ref, qat_ref, ka_ref, kat_ref, v_ref, o_ref, lse_ref, do_ref,
             dq_ref, dk_ref, dv_ref, dfq_ref, dfk_ref, dot_scr, delta, dqt):
        kj = pl.program_id(1)
        lane = _lanes((tq, 128))
        low = lane < HD
        causal = _rows((tq, tq)) <= _lanes((tq, tq))

        @pl.when(kj == 0)
        def _():
            dqt[...] = jnp.zeros_like(dqt)
            delta[...] = jnp.zeros_like(delta)

            def prep(c, _):
                rows = pl.ds(pl.multiple_of(c * tq, tq), tq)
                do = do_ref[rows, :]
                pt = jnp.transpose(do * o_ref[rows, :].astype(F32))
                delta[0:1, rows] = jnp.sum(pt[:HD], axis=0, keepdims=True)
                delta[1:2, rows] = jnp.sum(pt[HD:], axis=0, keepdims=True)
                dot_scr[:, rows] = jnp.transpose(do).astype(BF16)
                return 0

            lax.fori_loop(0, nq, prep, 0)

        v = v_ref[...]
        vms = [jnp.where(low, v, 0.0).astype(BF16), jnp.where(low, 0.0, v).astype(BF16)]

        def tile(qi, carry, masked):
            cols = pl.ds(pl.multiple_of(qi * tq, tq), tq)
            do = do_ref[cols, :].astype(BF16)
            out = []
            for hh in range(2):
                cs = slice(hh * AUG, (hh + 1) * AUG)
                dka, dv = carry[hh]
                st = jnp.dot(ka_ref[:, cs], qat_ref[cs, cols], preferred_element_type=F32)
                if masked:
                    st = jnp.where(causal, st, NEG_INF)
                pt = jnp.exp(st - lse_ref[0, hh:hh + 1, cols])
                dv = dv + jnp.dot(pt.astype(BF16), do, preferred_element_type=F32)
                dpt = jnp.dot(vms[hh], dot_scr[:, cols], preferred_element_type=F32)
                dsb = (pt * (dpt - delta[hh:hh + 1, cols])).astype(BF16)
                dka = dka + jnp.dot(dsb, qa_ref[cols, cs], preferred_element_type=F32)
                dqt[hh, :, cols] += jnp.dot(kat_ref[cs, :], dsb, preferred_element_type=F32)
                out.append((dka, dv))
            return tuple(out)

        init = tuple((jnp.zeros((tq, AUG), F32), jnp.zeros((tq, 128), F32)) for _ in range(2))
        carry = tile(kj, init, True)
        carry = lax.fori_loop(kj + 1, nq, lambda qi, c: tile(qi, c, False), carry)
        dks = [carry[0][0], carry[1][0]]
        dvs = [carry[0][1], carry[1][1]]
        dv_ref[...] = jnp.where(low, dvs[0], dvs[1]).astype(BF16)
        dk_ref[...] = jnp.where(low, dks[0], pltpu.roll(dks[1], HD, 1)).astype(BF16)
        dfk_ref[...] = jnp.where(lane == 0, -dks[0][:, BIAS_COL:BIAS_COL + 1],
                                 jnp.where(lane == 1, -dks[1][:, BIAS_COL:BIAS_COL + 1], 0.0))

        @pl.when(kj == nq - 1)
        def _():
            def finish(c, _):
                rows = pl.ds(pl.multiple_of(c * tq, tq), tq)
                t0 = jnp.transpose(dqt[0, :, rows])
                t1 = jnp.transpose(dqt[1, :, rows])
                dq_ref[rows, :] = (jnp.where(low, t0, pltpu.roll(t1, HD, 1)) * scale).astype(BF16)
                dfq_ref[rows, :] = jnp.where(lane == 0, t0[:, FQ_COL:FQ_COL + 1],
                                             jnp.where(lane == 1, t1[:, FQ_COL:FQ_COL + 1], 0.0))
                return 0

            lax.fori_loop(0, nq, finish, 0)

    seq128 = lambda blk: pl.BlockSpec((s, 128), lambda h, j: (0, blk + h))
    tile128 = pl.BlockSpec((tq, 128), lambda h, j: (j, h))
    out_b = jax.ShapeDtypeStruct((s, FX_W), BF16)
    out_f = jax.ShapeDtypeStruct((s, FX_W), F32)
    return _call(body, name="attn_grad", grid=(FX_H // 2, nq),
                 in_specs=[pl.BlockSpec((s, 2 * AUG), lambda h, j: (0, h)), pl.BlockSpec((2 * AUG, s), lambda h, j: (h, 0)),
                           pl.BlockSpec((tq, 2 * AUG), lambda h, j: (j, h)), pl.BlockSpec((2 * AUG, tq), lambda h, j: (h, j)),
                           pl.BlockSpec((tq, 128), lambda h, j: (j, V_BLK + h)), seq128(0),
                           pl.BlockSpec((1, 8, s), lambda h, j: (h, 0, 0)), seq128(4)],
                 out_specs=[seq128(0), tile128, tile128, seq128(0), tile128],
                 out_shape=[out_b, out_b, out_b, out_f, out_f],
                 scratch_shapes=[pltpu.VMEM((128, s), BF16), pltpu.VMEM((8, s), F32), pltpu.VMEM((2, AUG, s), F32)],
                 compiler_params=_params("arbitrary", "arbitrary"))(qa, qat, ka, kat, p, o, lse, dmix)


def _shift_down(a, prev8, k):
    r = pltpu.roll(a, k, 0)
    top = jnp.where(_rows(prev8.shape) < k, pltpu.roll(prev8, k, 0), r[0:8])
    return jnp.concatenate([top, r[8:]], axis=0)


def _shift_up(a, next8, k):
    t = a.shape[0]
    r = pltpu.roll(a, t - k, 0)
    bot = jnp.where(_rows(next8.shape) >= 8 - k, pltpu.roll(next8, 8 - k, 0), r[t - 8:t])
    return jnp.concatenate([r[:t - 8], bot], axis=0)


def _conv(a, prev8, cw, cb):
    return cb + cw[0:1, :] * _shift_down(a, prev8, 2) + cw[1:2, :] * _shift_down(a, prev8, 1) + cw[2:3, :] * a


def conv_fwd(up, cw, cb):
    s = up.shape[0]
    ts = min(TS_C, s)

    def body(a_ref, g_ref, cw_ref, cb_ref, o_ref, halo):
        @pl.when(pl.program_id(0) == 0)
        def _():
            halo[...] = jnp.zeros_like(halo)

        a = a_ref[...]
        c = _conv(a, halo[...], cw_ref[...], cb_ref[...])
        o_ref[...] = (jax.nn.gelu(c) * g_ref[...]).astype(BF16)
        halo[...] = a[ts - 8:ts, :]

    return _call(body, name="conv_fwd", grid=(s // ts,),
                 in_specs=[pl.BlockSpec((ts, DFF), lambda i: (i, 0)), pl.BlockSpec((ts, DFF), lambda i: (i, 1)),
                           pl.BlockSpec((3, DFF), lambda i: (0, 0)), pl.BlockSpec((1, DFF), lambda i: (0, 0))],
                 out_specs=pl.BlockSpec((ts, DFF), lambda i: (i, 0)),
                 out_shape=jax.ShapeDtypeStruct((s, DFF), BF16),
                 scratch_shapes=[pltpu.VMEM((8, DFF), F32)],
                 compiler_params=_params("arbitrary"))(up, up, cw, cb)


def conv_bwd(up, cw, cb, dact):
    s = up.shape[0]
    ts = min(TS_C, s)
    ns = s // ts

    def body(a_ref, g_ref, prev_ref, cw_ref, cb_ref, dact_ref, dup_ref, dcw_ref, dcb_ref, halo):
        i = pl.program_id(0)

        @pl.when(i == 0)
        def _():
            halo[...] = jnp.zeros_like(halo)
            dcw_ref[...] = jnp.zeros_like(dcw_ref)
            dcb_ref[...] = jnp.zeros_like(dcb_ref)

        a = a_ref[...]
        g = g_ref[...]
        cw = cw_ref[...]
        prev8 = prev_ref[...] * (i < ns - 1).astype(F32)
        c = _conv(a, prev8, cw, cb_ref[...])
        gel, vjp = jax.vjp(jax.nn.gelu, c)
        dact = dact_ref[...]
        (dc,) = vjp(dact * g)
        up1 = _shift_up(dc, halo[...], 1)
        up2 = _shift_up(dc, halo[...], 2)
        da = cw[2:3, :] * dc + cw[1:2, :] * up1 + cw[0:1, :] * up2
        dup_ref[:, :DFF] = da.astype(BF16)
        dup_ref[:, DFF:] = (dact * gel).astype(BF16)
        dcw_ref[0:1, :] += jnp.sum(a * up2, axis=0, keepdims=True)
        dcw_ref[1:2, :] += jnp.sum(a * up1, axis=0, keepdims=True)
        dcw_ref[2:3, :] += jnp.sum(a * dc, axis=0, keepdims=True)
        dcb_ref[...] += jnp.sum(dc, axis=0, keepdims=True)
        halo[...] = dc[0:8, :]

    rev = lambda col: (lambda i: (ns - 1 - i, col))
    prev_map = lambda i: (jnp.maximum((ns - 1 - i) * (ts // 8) - 1, 0), 0)
    return _call(body, name="conv_bwd", grid=(ns,),
                 in_specs=[pl.BlockSpec((ts, DFF), rev(0)), pl.BlockSpec((ts, DFF), rev(1)),
                           pl.BlockSpec((8, DFF), prev_map), pl.BlockSpec((3, DFF), lambda i: (0, 0)),
                           pl.BlockSpec((1, DFF), lambda i: (0, 0)), pl.BlockSpec((ts, DFF), rev(0))],
                 out_specs=[pl.BlockSpec((ts, 2 * DFF), rev(0)), pl.BlockSpec((3, DFF), lambda i: (0, 0)),
                            pl.BlockSpec((1, DFF), lambda i: (0, 0))],
                 out_shape=[jax.ShapeDtypeStruct((s, 2 * DFF), BF16), jax.ShapeDtypeStruct((3, DFF), F32),
                            jax.ShapeDtypeStruct((1, DFF), F32)],
                 scratch_shapes=[pltpu.VMEM((8, DFF), F32)],
                 compiler_params=_params("arbitrary"))(up, up, up, cw, cb, dact)


def _blockdiag_expand(m):
    m4 = m.reshape(S5_H, 2, S5_G, S5_P)
    eye = jnp.eye(S5_G, dtype=bool)[:, None, None, :, None]
    return jnp.where(eye, m4[None], 0.0).reshape(S5_W, 2 * S5_N)


def _blockdiag_extract(mbd):
    m5 = mbd.reshape(S5_G, S5_H, 2, S5_G, S5_P)
    diag = jnp.stack([m5[g, :, :, g, :] for g in range(S5_G)], axis=2)
    return diag.reshape(S5_H, 2 * S5_N)


def _c_expand(c_re, c_im):
    c4 = jnp.stack([c_re, c_im], axis=2)
    eye = jnp.eye(S5_G, dtype=bool)[:, None, None, :, None]
    return jnp.where(eye, c4[:, :, :, None, :], 0.0).reshape(S5_W, 2 * S5_N)


def _c_extract(cbd):
    m5 = cbd.reshape(S5_G, S5_H, 2, S5_G, S5_P)
    d = jnp.stack([m5[g, :, :, g, :] for g in range(S5_G)], axis=0)
    return d[:, :, 0, :], d[:, :, 1, :]


def _glu_expand(w):
    eye = jnp.eye(S5_G, dtype=bool)[:, None, :, None]
    return jnp.where(eye, w[:, :, None, :], 0.0).reshape(S5_W, S5_W)


def _glu_extract(wbd):
    m4 = wbd.reshape(S5_G, S5_H, S5_G, S5_H)
    return jnp.stack([m4[g, :, g, :] for g in range(S5_G)], axis=0)


SMALL = ("b_f", "gm_ln_g", "gm_ln_b", "gm_w_s", "gm_b_s", "s5_lam_re", "s5_lam_im", "s5_log_dt", "s5_b_re", "s5_b_im",
         "s5_c_re", "s5_c_im", "s5_d", "s5_w_glu", "s5_b_glu", "ln1_g", "ln1_b", "conv_b", "ln2_g", "ln2_b")


def _layer_operands(sp, l):
    f = {}
    f["bf"] = jnp.pad(sp["b_f"][l][None, :], ((0, 0), (0, 128 - FX_H)))
    f["gm_lg"] = sp["gm_ln_g"][l].reshape(1, GM_W)
    f["gm_lb"] = sp["gm_ln_b"][l].reshape(1, GM_W)
    f["gm_ws"] = sp["gm_w_s"][l]
    f["gm_bst"] = sp["gm_b_s"][l].T
    f["lr"] = sp["s5_lam_re"][l].reshape(1, S5_N)
    f["li"] = sp["s5_lam_im"][l].reshape(1, S5_N)
    f["ldt"] = jnp.repeat(sp["s5_log_dt"][l], S5_P).reshape(1, S5_N)
    bt = lambda b: jnp.transpose(b, (2, 0, 1)).reshape(S5_H, S5_N)
    f["bt"] = jnp.concatenate([bt(sp["s5_b_re"][l]), bt(sp["s5_b_im"][l])], axis=1)
    f["cbd"] = _c_expand(sp["s5_c_re"][l], sp["s5_c_im"][l])
    f["drow"] = sp["s5_d"][l].reshape(1, S5_W)
    f["wg"] = _glu_expand(sp["s5_w_glu"][l])
    f["bg"] = sp["s5_b_glu"][l].reshape(1, S5_W)
    for n in ("ln1_g", "ln1_b", "ln2_g", "ln2_b"):
        f[n] = sp[n][l][None, :]
    f["cb"] = sp["conv_b"][l][None, :]
    return f


def layer_fwd(x, mod, w, f):
    p = mm_nn(x, w["w_in"], NP, "in_proj", mod=mod, rows=(0, 1))
    ygm = gm_fwd(p, f["gm_lg"], f["gm_lb"], f["gm_ws"], f["gm_bst"])
    arow, bbt = s5_prep_fwd(f["lr"], f["li"], f["ldt"], f["bt"])
    bbd = _blockdiag_expand(bbt)
    ys5, st = s5_fwd(p, arow, bbd, f["cbd"], f["drow"], f["wg"], f["bg"])
    qa, ka, qat, kat, vt = fox_prep(p, f["bf"])
    yfx, lse = attn(qat, ka, vt)
    mixcat = jnp.concatenate([ygm, ys5, yfx], axis=1)
    mix = mm_nn(mixcat, w["w_out"], D, "out_proj")
    x1 = post_fwd(x, mix, mod, 2, f["ln1_g"], f["ln1_b"], "post1_fwd")
    up = mm_nn(x1, w["w_up"], DFF // 2, "up_proj", mod=mod, rows=(3, 4))
    act = conv_fwd(up, w["conv_w"], f["cb"])
    ffn = mm_nn(act, w["w_down"], D, "down_proj")
    x2 = post_fwd(x1, ffn, mod, 5, f["ln2_g"], f["ln2_b"], "post2_fwd")
    saved = dict(f=f, x=x, p=p, arow=arow, bbd=bbd, st=st, qa=qa, ka=ka, qat=qat, kat=kat, yfx=yfx, lse=lse,
                 mixcat=mixcat, mix=mix, x1=x1, up=up, act=act, ffn=ffn)
    return x2, saved


def layer_bwd(dx, sv, mod, w):
    f = sv["f"]
    dx1, dffn, dg2, dlg2, dlb2 = post_bwd(sv["x1"], sv["ffn"], mod, 5, f["ln2_g"], f["ln2_b"], dx, "post2_bwd")
    g_down = mm_tn(sv["act"], dffn, D // 2, "down_dw")
    dact = mm_nt(dffn, w["w_down"], "down_dx")
    dup, dcw, dcb = conv_bwd(sv["up"], w["conv_w"], f["cb"], dact)
    g_up = mm_tn(sv["x1"], dup, DFF // 2, "up_dw", mod=mod, rows=(3, 4))
    dx1, dsh2, dsc2 = mm_nt_mod(dup, w["w_up"], sv["x1"], dx1, mod, (3, 4), "up_dx")
    dx0, dmix, dg1, dlg1, dlb1 = post_bwd(sv["x"], sv["mix"], mod, 2, f["ln1_g"], f["ln1_b"], dx1, "post1_bwd")
    g_out = mm_tn(sv["mixcat"], dmix, D, "out_dw")
    dmc = mm_nt(dmix, w["w_out"], "out_dx")
    duv, dgm_lg, dgm_lb, dgm_ws, dgm_bst = gm_bwd(sv["p"], f["gm_lg"], f["gm_lb"], f["gm_ws"], f["gm_bst"], dmc)
    du5, da, dbbd, dcbd, dd5, dwg, dbg = s5_bwd(sv["p"], sv["st"], sv["arow"], sv["bbd"], f["cbd"], f["drow"],
                                                f["wg"], f["bg"], dmc)
    dlr, dli, dldt, dbt = s5_prep_bwd(f["lr"], f["li"], f["ldt"], f["bt"], da, _blockdiag_extract(dbbd))
    dq, dk, dv, dfq, dfk = attn_grad(sv["qa"], sv["qat"], sv["ka"], sv["kat"], sv["p"], sv["yfx"], sv["lse"], dmc)
    dff, dbf = fox_prep_grad(sv["p"], f["bf"], dfq, dfk)
    dp = jnp.concatenate([duv, du5, dq, dk, dv, dff], axis=1)
    g_in = mm_tn(sv["x"], dp, NP, "in_dw", mod=mod, rows=(0, 1))
    dx, dsh1, dsc1 = mm_nt_mod(dp, w["w_in"], sv["x"], dx0, mod, (0, 1), "in_dx")

    gbig = dict(w_in=g_in, w_out=g_out, w_up=g_up, w_down=g_down, conv_w=dcw)
    dmod = jnp.concatenate([dsh1, dsc1, dg1, dsh2, dsc2, dg2], axis=0)
    dc_re, dc_im = _c_extract(dcbd)
    dbt4 = dbt.reshape(S5_H, 2, S5_G, S5_P)
    vals = dict(b_f=dbf[0, :FX_H], gm_ln_g=dgm_lg.reshape(GM_H, HD), gm_ln_b=dgm_lb.reshape(GM_H, HD),
                gm_w_s=dgm_ws, gm_b_s=dgm_bst.T, s5_lam_re=dlr.reshape(S5_G, S5_P),
                s5_lam_im=dli.reshape(S5_G, S5_P), s5_log_dt=dldt[0, :S5_G],
                s5_b_re=jnp.transpose(dbt4[:, 0], (1, 2, 0)), s5_b_im=jnp.transpose(dbt4[:, 1], (1, 2, 0)),
                s5_c_re=dc_re, s5_c_im=dc_im, s5_d=dd5.reshape(S5_G, S5_H), s5_w_glu=_glu_extract(dwg),
                s5_b_glu=dbg.reshape(S5_G, S5_H), ln1_g=dlg1[0], ln1_b=dlb1[0], conv_b=dcb[0],
                ln2_g=dlg2[0], ln2_b=dlb2[0])
    return dx, gbig, vals, dmod


def local_step(x, target, mods, big, sp):
    saved = []
    for l in range(DEPTH):
        x, sv = layer_fwd(x, mods[l], big[l], _layer_operands(sp, l))
        saved.append(sv)
    loss_tile, dx = loss_kernel(x, target)
    gbig, vals, dmods = [None] * DEPTH, [None] * DEPTH, [None] * DEPTH
    for l in reversed(range(DEPTH)):
        dx, gbig[l], vals[l], dmods[l] = layer_bwd(dx, saved[l], mods[l], big[l])
    gsm = {n: jnp.stack([v[n] for v in vals]) for n in SMALL}
    return loss_tile, dx, gbig, gsm, jnp.stack(dmods)


def _my_index():
    return 4 * lax.axis_index("x") + 2 * lax.axis_index("y") + lax.axis_index("c")


def exchange(tensors, scatter, name):
    n = len(tensors)

    def body(*refs):
        ins, outs = refs[:n], refs[n:2 * n]
        send_sems, recv_sems, local_sems = refs[2 * n:]
        x, y, c = lax.axis_index("x"), lax.axis_index("y"), lax.axis_index("c")
        me = 4 * x + 2 * y + c
        local = []
        for t in range(n):
            cp = pltpu.make_async_copy(ins[t].at[me] if scatter else ins[t], outs[t].at[me], local_sems.at[t])
            cp.start()
            local.append(cp)
        remote = []
        for m in range(1, NDEV):
            px = 1 - x if m & 4 else x
            py = 1 - y if m & 2 else y
            pc = 1 - c if m & 1 else c
            peer = 4 * px + 2 * py + pc
            for t in range(n):
                k = t * (NDEV - 1) + m - 1
                cp = pltpu.make_async_remote_copy(
                    src_ref=ins[t].at[peer] if scatter else ins[t], dst_ref=outs[t].at[me],
                    send_sem=send_sems.at[k], recv_sem=recv_sems.at[k],
                    device_id=(px, py, pc), device_id_type=MESH_IDS)
                cp.start()
                remote.append(cp)
        for cp in remote:
            cp.wait()
        for cp in local:
            cp.wait()

    hbm = pl.BlockSpec(memory_space=pltpu.HBM)
    out_shape = [jax.ShapeDtypeStruct(t.shape if scatter else (NDEV,) + t.shape, t.dtype) for t in tensors]
    return _call(body, name=name, in_specs=[hbm] * n, out_specs=[hbm] * n, out_shape=out_shape,
                 scratch_shapes=[pltpu.SemaphoreType.DMA((n * (NDEV - 1),)), pltpu.SemaphoreType.DMA((n * (NDEV - 1),)),
                                 pltpu.SemaphoreType.DMA((n,))])(*tensors)


def _peers():
    x, y, c = lax.axis_index("x"), lax.axis_index("y"), lax.axis_index("c")
    out = []
    for m in range(1, NDEV):
        px = 1 - x if m & 4 else x
        py = 1 - y if m & 2 else y
        pc = 1 - c if m & 1 else c
        out.append(((px, py, pc), 4 * px + 2 * py + pc))
    return 4 * x + 2 * y + c, out


def _split_copies(v_refs, land_refs, send_sems, recv_sems, scatter):
    me, peers = _peers()
    return [pltpu.make_async_remote_copy(
        src_ref=v_ref.at[idx] if scatter else v_ref, dst_ref=land_ref.at[me],
        send_sem=send_sems.at[t * (NDEV - 1) + k], recv_sem=recv_sems.at[t * (NDEV - 1) + k],
        device_id=pos, device_id_type=MESH_IDS)
        for t, (v_ref, land_ref) in enumerate(zip(v_refs, land_refs)) for k, (pos, idx) in enumerate(peers)]


_HBM_SPEC = pl.BlockSpec(memory_space=pltpu.HBM)
_SEM_SPEC = pl.BlockSpec(memory_space=pltpu.SEMAPHORE)
_SPLIT_EFFECT = pltpu.SideEffectType.DATAFLOW_SIDE_EFFECTING


def exchange_start(tensors, scatter, name):
    n = len(tensors)
    land_shapes = [t.shape if scatter else (NDEV,) + t.shape for t in tensors]

    def body(*refs):
        v_refs, land_refs = refs[:n], refs[n:2 * n]
        send_sems, recv_sems = refs[2 * n], refs[2 * n + 1]
        token = refs[-1]
        for cp in _split_copies(v_refs, land_refs, send_sems, recv_sems, scatter):
            cp.start()
        token[...] = jnp.zeros_like(token)

    sems = pltpu.SemaphoreType.DMA((n * (NDEV - 1),))
    out = _call(
        body, name=name,
        out_shape=(sems, sems, *[pltpu.HBM(t.shape, t.dtype) for t in tensors],
                   *[pltpu.HBM(s, t.dtype) for s, t in zip(land_shapes, tensors)], jax.ShapeDtypeStruct((8, 128), F32)),
        in_specs=(_HBM_SPEC,) * (2 * n),
        out_specs=(_SEM_SPEC, _SEM_SPEC) + (_HBM_SPEC,) * (2 * n) + (pl.BlockSpec(memory_space=pltpu.VMEM),),
        input_output_aliases={i: i + 2 for i in range(2 * n)},
        compiler_params=pltpu.CompilerParams(has_side_effects=_SPLIT_EFFECT),
    )(*[pltpu.with_memory_space_constraint(t, pltpu.HBM) for t in tensors],
      *[pltpu.with_memory_space_constraint(lax.empty(s, t.dtype), pltpu.HBM) for s, t in zip(land_shapes, tensors)])
    return out[0], out[1], list(out[2:2 + n]), list(out[2 + n:2 + 2 * n]), out[-1]


def exchange_wait(started, after, scatter, name):
    send_sems, recv_sems, v_thru, land_thru, _ = started
    n = len(v_thru)

    def body(*refs):
        v_refs, land_refs = refs[:n], refs[n:2 * n]
        for cp in _split_copies(v_refs, land_refs, refs[2 * n], refs[2 * n + 1], scatter):
            cp.wait_send()
            cp.wait_recv()

    out = _call(
        body, name=name,
        out_shape=tuple(pltpu.HBM(t.shape, t.dtype) for t in v_thru + land_thru),
        in_specs=(_HBM_SPEC,) * (2 * n) + (_SEM_SPEC, _SEM_SPEC, pl.BlockSpec(memory_space=pl.ANY)),
        out_specs=(_HBM_SPEC,) * (2 * n), input_output_aliases={i: i for i in range(2 * n)},
        compiler_params=pltpu.CompilerParams(has_side_effects=_SPLIT_EFFECT),
    )(*v_thru, *land_thru, send_sems, recv_sems, after)
    return list(out[:n]), list(out[n:])


def own_slots(vs, lands, scatter, name):
    n = len(vs)

    def body(*refs):
        v_refs, out_refs, sems = refs[:n], refs[2 * n:3 * n], refs[3 * n]
        me = _my_index()
        copies = [pltpu.make_async_copy(v.at[me] if scatter else v, o.at[me], sems.at[t])
                  for t, (v, o) in enumerate(zip(v_refs, out_refs))]
        for cp in copies:
            cp.start()
        for cp in copies:
            cp.wait()

    return list(_call(
        body, name=name, out_shape=tuple(pltpu.HBM(t.shape, t.dtype) for t in lands),
        in_specs=(_HBM_SPEC,) * (2 * n), out_specs=(_HBM_SPEC,) * n,
        input_output_aliases={n + i: i for i in range(n)},
        scratch_shapes=[pltpu.SemaphoreType.DMA((n,))],
    )(*vs, *lands))


def mod_slices(c_all, w_ada, b_loc):
    nl, _, nc = w_ada.shape

    def body(c_ref, w_ref, b_ref, o_ref):
        cv = c_ref[...]
        o_ref[0] = _nn(cv * jax.nn.sigmoid(cv), w_ref[0]) + b_ref[0]

    return _call(body, name="mod_slices", grid=(nl,),
                 in_specs=[pl.BlockSpec((NDEV, D), lambda l: (0, 0)), pl.BlockSpec((1, D, nc), lambda l: (l, 0, 0)),
                           pl.BlockSpec((1, 1, nc), lambda l: (l, 0, 0))],
                 out_specs=pl.BlockSpec((1, NDEV, nc), lambda l: (l, 0, 0)),
                 out_shape=jax.ShapeDtypeStruct((nl, NDEV, nc), F32),
                 compiler_params=_params("arbitrary"))(c_all, w_ada, b_loc.reshape(nl, 1, nc))


def ada_grad(c_all, dm_loc):
    nl, _, nc = dm_loc.shape

    def body(c_ref, d_ref, o_ref):
        cv = c_ref[...]
        o_ref[0] = _tn(cv * jax.nn.sigmoid(cv), d_ref[0])

    return _call(body, name="ada_grad", grid=(nl,),
                 in_specs=[pl.BlockSpec((NDEV, D), lambda l: (0, 0)), pl.BlockSpec((1, NDEV, nc), lambda l: (l, 0, 0))],
                 out_specs=pl.BlockSpec((1, D, nc), lambda l: (l, 0, 0)),
                 out_shape=jax.ShapeDtypeStruct((nl, D, nc), F32),
                 compiler_params=_params("arbitrary"))(c_all, dm_loc)


def sum_chunks(chunks):
    r = chunks.shape[1]

    def body(c_ref, o_ref):
        acc = c_ref[0]
        for i in range(1, NDEV):
            acc = acc + c_ref[i]
        o_ref[...] = acc

    return _call(body, name="sum_chunks", out_shape=jax.ShapeDtypeStruct((r, 128), F32))(chunks)


def _row_tile(r):
    if r <= 256:
        return r
    for t in range(256, 7, -8):
        if r % t == 0:
            return t
    return r


def adamw(w, m, v, g=None, chunks=None, name="adamw"):
    r, cdim = w.shape
    tr = _row_tile(r)
    bc1 = 1.0 - ADAM_B1 ** ADAM_STEP
    bc2 = 1.0 - ADAM_B2 ** ADAM_STEP

    def body(g_ref, w_ref, m_ref, v_ref, go_ref, d_ref, mo_ref, vo_ref):
        if chunks is None:
            grad = g_ref[...]
        else:
            grad = g_ref[0].astype(F32)
            for i in range(1, NDEV):
                grad = grad + g_ref[i].astype(F32)
        mn = ADAM_B1 * m_ref[...] + (1.0 - ADAM_B1) * grad
        vn = ADAM_B2 * v_ref[...] + (1.0 - ADAM_B2) * (grad * grad)
        m_hat = mn / bc1
        v_hat = vn / bc2
        go_ref[...] = grad
        d_ref[...] = -ADAM_LR * (m_hat / (jnp.sqrt(v_hat) + ADAM_EPS) + ADAM_WD * w_ref[...])
        mo_ref[...] = mn
        vo_ref[...] = vn

    tile = pl.BlockSpec((tr, cdim), lambda i: (i, 0))
    gspec = tile if chunks is None else pl.BlockSpec((NDEV, tr, cdim), lambda i: (0, i, 0))
    shp = jax.ShapeDtypeStruct((r, cdim), F32)
    return _call(body, name=name, grid=(r // tr,), in_specs=[gspec, tile, tile, tile],
                 out_specs=[tile] * 4, out_shape=[shp] * 4,
                 compiler_params=_params("arbitrary"))(g if chunks is None else chunks, w, m, v)


def adamw_layers(w, m, v, chunks, name):
    nl, r, cdim = w.shape
    tr = _row_tile(r)
    bc1 = 1.0 - ADAM_B1 ** ADAM_STEP
    bc2 = 1.0 - ADAM_B2 ** ADAM_STEP
    outs = [lax.empty(w.shape, F32) for _ in range(4)]
    for l in range(nl):
        def body(g_ref, w_ref, m_ref, v_ref, p0, p1, p2, p3, go_ref, d_ref, mo_ref, vo_ref):
            grad = g_ref[0].astype(F32)
            for i in range(1, NDEV):
                grad = grad + g_ref[i].astype(F32)
            mn = ADAM_B1 * m_ref[...] + (1.0 - ADAM_B1) * grad
            vn = ADAM_B2 * v_ref[...] + (1.0 - ADAM_B2) * (grad * grad)
            go_ref[...] = grad
            d_ref[...] = -ADAM_LR * ((mn / bc1) / (jnp.sqrt(vn / bc2) + ADAM_EPS) + ADAM_WD * w_ref[...])
            mo_ref[...] = mn
            vo_ref[...] = vn

        tile = pl.BlockSpec((None, tr, cdim), lambda i, l=l: (l, i, 0))
        whole = pl.BlockSpec(memory_space=pl.ANY)
        outs = _call(body, name=f"{name}_{l}", grid=(r // tr,),
                     in_specs=[pl.BlockSpec((NDEV, tr, cdim), lambda i: (0, i, 0)), tile, tile, tile] + [whole] * 4,
                     out_specs=[tile] * 4, out_shape=[jax.ShapeDtypeStruct(w.shape, F32)] * 4,
                     input_output_aliases={4: 0, 5: 1, 6: 2, 7: 3},
                     compiler_params=_params("arbitrary"))(chunks[l], w, m, v, *outs)
    return outs


WEIGHTS = ("w_ada", "b_ada", "w_in", "b_f", "gm_ln_g", "gm_ln_b", "gm_w_s", "gm_b_s", "s5_lam_re", "s5_lam_im",
           "s5_log_dt", "s5_b_re", "s5_b_im", "s5_c_re", "s5_c_im", "s5_d", "s5_w_glu", "s5_b_glu", "w_out", "ln1_g",
           "ln1_b", "w_up", "conv_w", "conv_b", "w_down", "ln2_g", "ln2_b")
SHARDED = ("w_in", "w_out", "w_up", "w_down", "conv_w")
LARGE = ("w_in", "w_out", "w_up", "w_down")
COL_SHARDED = ("w_in", "w_up", "conv_w")
PACKED = ("b_ada",) + SMALL
PACK_ALIGN = NDEV * 8 * 128


def _gather_cols(g):
    nd, nl, r, c = g.shape
    return jnp.transpose(g, (1, 2, 0, 3)).reshape(nl, r, nd * c)


def _chunk_cols(g):
    nl, r, c8 = g.shape
    return jnp.transpose(g.reshape(nl, r, NDEV, c8 // NDEV), (2, 0, 1, 3))


def _join_cols(g):
    nd, r, c = g.shape
    return jnp.transpose(g, (1, 0, 2)).reshape(r, nd * c)


def _join_rows(g):
    nd, r, c = g.shape
    return g.reshape(nd * r, c)


def _split_cols(g):
    r, c8 = g.shape
    return jnp.transpose(g.reshape(r, NDEV, c8 // NDEV), (1, 0, 2))


def _split_rows(g):
    r8, c = g.shape
    return g.reshape(NDEV, r8 // NDEV, c)


def _pack(parts):
    flat = jnp.concatenate([parts[n].reshape(-1) for n in PACKED])
    return jnp.pad(flat, (0, -flat.shape[0] % PACK_ALIGN))


def _unpack(flat, shapes):
    out, off = {}, 0
    for n in PACKED:
        size = math.prod(shapes[n])
        out[n] = flat[off:off + size].reshape(shapes[n])
        off += size
    return out


def kernel(x, c, w_ada, b_ada, w_in, b_f, gm_ln_g, gm_ln_b, gm_w_s, gm_b_s, s5_lam_re, s5_lam_im, s5_log_dt, s5_b_re, s5_b_im, s5_c_re, s5_c_im, s5_d, s5_w_glu, s5_b_glu, w_out, ln1_g, ln1_b, w_up, conv_w, conv_b, w_down, ln2_g, ln2_b, loss_target, m_w_ada, m_b_ada, m_w_in, m_b_f, m_gm_ln_g, m_gm_ln_b, m_gm_w_s, m_gm_b_s, m_s5_lam_re, m_s5_lam_im, m_s5_log_dt, m_s5_b_re, m_s5_b_im, m_s5_c_re, m_s5_c_im, m_s5_d, m_s5_w_glu, m_s5_b_glu, m_w_out, m_ln1_g, m_ln1_b, m_w_up, m_conv_w, m_conv_b, m_w_down, m_ln2_g, m_ln2_b, v_w_ada, v_b_ada, v_w_in, v_b_f, v_gm_ln_g, v_gm_ln_b, v_gm_w_s, v_gm_b_s, v_s5_lam_re, v_s5_lam_im, v_s5_log_dt, v_s5_b_re, v_s5_b_im, v_s5_c_re, v_s5_c_im, v_s5_d, v_s5_w_glu, v_s5_b_glu, v_w_out, v_ln1_g, v_ln1_b, v_w_up, v_conv_w, v_conv_b, v_w_down, v_ln2_g, v_ln2_b):
    given = dict(locals())
    wts = {n: given[n] for n in WEIGHTS}
    mom = {n: given["m_" + n] for n in WEIGHTS}
    var = {n: given["v_" + n] for n in WEIGHTS}
    nl = w_ada.shape[0]
    me = _my_index()
    ada_cols = w_ada.shape[2]

    (c_all,) = exchange([c], False, "gather_c")
    c_all = c_all.reshape(NDEV, D)
    b_loc = lax.dynamic_slice_in_dim(b_ada, me * ada_cols, ada_cols, axis=1)
    mod_part = mod_slices(c_all, w_ada, b_loc)

    mod_all, conv_all = exchange([mod_part, conv_w], False, "gather_mod")
    mod_mine = lax.dynamic_index_in_dim(mod_all, me, axis=2, keepdims=False)
    mods = jnp.transpose(mod_mine, (1, 0, 2)).reshape(nl, 6, D)
    mods = jnp.pad(mods, ((0, 0), (0, 2), (0, 0)))
    conv_full = _gather_cols(conv_all)
    sp = {n: wts[n] for n in SMALL}

    blocks = [[wts[n][l].astype(BF16) for n in LARGE] for l in range(nl)]
    started = exchange_start(blocks[0], False, "gather_start_0")
    v_thru, land = exchange_wait(started, started[4], False, "gather_wait_0")
    xl, saved, weights = x[0], [], []
    for l in range(nl):
        land = own_slots(v_thru, land, False, f"gather_own_{l}")
        mod_l = mods[l]
        if l + 1 < nl:
            nxt, land = lax.optimization_barrier((blocks[l + 1], land))
            started = exchange_start(nxt, False, f"gather_start_{l + 1}")
            mod_l, _ = lax.optimization_barrier((mod_l, started[4]))
        w = {n: (_join_cols if n in COL_SHARDED else _join_rows)(g) for n, g in zip(LARGE, land)}
        w["w_in"] = jnp.pad(w["w_in"], ((0, 0), (0, NP - D_IN)))
        w["conv_w"] = conv_full[l]
        weights.append(w)
        xl, sv = layer_fwd(xl, mod_l, w, _layer_operands(sp, l))
        saved.append(sv)
        if l + 1 < nl:
            v_thru, land = exchange_wait(started, xl, False, f"gather_wait_{l + 1}")

    loss_tile, dx = loss_kernel(xl, loss_target[0])

    scattering, vals, dmods, gconv = [None] * nl, [None] * nl, [None] * nl, [None] * nl
    for l in reversed(range(nl)):
        dx, gbig, vals[l], dmods[l] = layer_bwd(dx, saved[l], mods[l], weights[l])
        gconv[l] = gbig["conv_w"]
        gbig["w_in"] = gbig["w_in"][:, :D_IN]
        chunks = [(_split_cols if n in COL_SHARDED else _split_rows)(gbig[n]).astype(BF16) for n in LARGE]
        scattering[l] = exchange_start(chunks, True, f"scatter_start_{l}")
        dx, _ = lax.optimization_barrier((dx, scattering[l][4]))
    gx = dx
    dmods = jnp.stack(dmods)
    gsm = {n: jnp.stack([v[n] for v in vals]) for n in SMALL}

    gsm["b_ada"] = dmods.reshape(nl, 6 * D)
    packed = _pack(gsm).reshape(NDEV, -1, 128)
    conv_recv, small_recv = exchange([_chunk_cols(jnp.stack(gconv)), packed], True, "scatter_small")
    small_sum = sum_chunks(small_recv)
    small_all, dmod_all = exchange([small_sum, dmods.reshape(nl, 6 * D)], False, "gather_small")

    received, after = [None] * nl, small_all
    for l in reversed(range(nl)):
        v_thru, land = exchange_wait(scattering[l], after, True, f"scatter_wait_{l}")
        received[l] = dict(zip(LARGE, own_slots(v_thru, land, True, f"scatter_own_{l}")))
        after = land[0]

    out = {}
    for n in LARGE:
        out[n] = adamw_layers(wts[n], mom[n], var[n], [received[l][n] for l in range(nl)], "adamw_" + n)
    shp = conv_w.shape
    two_d = lambda a: a.reshape(shp[0] * shp[1], shp[2])
    res = adamw(two_d(conv_w), two_d(m_conv_w), two_d(v_conv_w),
                chunks=conv_recv.reshape(NDEV, shp[0] * shp[1], shp[2]), name="adamw_conv_w")
    out["conv_w"] = [r.reshape(shp) for r in res]

    dm_loc = lax.dynamic_slice_in_dim(dmod_all, me * ada_cols, ada_cols, axis=2)
    g_ada = ada_grad(c_all, jnp.transpose(dm_loc, (1, 0, 2)))
    two_d = lambda a: a.reshape(nl * D, ada_cols)
    res = adamw(two_d(w_ada), two_d(m_w_ada), two_d(v_w_ada), g=two_d(g_ada), name="adamw_w_ada")
    out["w_ada"] = [r.reshape(w_ada.shape) for r in res]

    shapes = {n: wts[n].shape for n in PACKED}
    as_rows = lambda parts: _pack(parts).reshape(-1, 128)
    res = adamw(as_rows(wts), as_rows(mom), as_rows(var), g=small_all.reshape(-1, 128), name="adamw_small")
    unpacked = [_unpack(r.reshape(-1), shapes) for r in res]
    for n in PACKED:
        out[n] = [u[n] for u in unpacked]

    loss = lax.psum(loss_tile[0, 0], ("x", "y", "c"))
    return (loss, gx[None], *[out[n][0] for n in WEIGHTS], *[out[n][1] for n in WEIGHTS],
            *[out[n][2] for n in WEIGHTS], *[out[n][3] for n in WEIGHTS])
```

```python
import functools
import math

import jax
import jax.numpy as jnp
from jax import lax
from jax.experimental import pallas as pl
from jax.experimental.pallas import tpu as pltpu

F32 = jnp.float32
BF16 = jnp.bfloat16
MESH_IDS = pl.DeviceIdType.MESH

D = 1024
SEQ = 4096
DEPTH = 4
NDEV = 8
HD = 64
GM_W = 256
GM_H = 4
GM_C = 128
S5_W = 256
S5_G = 16
S5_H = 16
S5_P = 64
S5_N = S5_G * S5_P
FX_W = 512
FX_H = 8
D_IN = 2 * GM_W + S5_W + 3 * FX_W + FX_H
NP = 2432
FF_COL = 2304
DFF = 2816
LN_EPS = 1e-5
DN_ALPHA = (2.0 * DEPTH) ** 0.25
NEG_INF = -1e30
ADAM_LR = 0.001
ADAM_B1 = 0.9
ADAM_B2 = 0.999
ADAM_EPS = 1e-08
ADAM_WD = 0.01
ADAM_STEP = 10

V7X_VMEM_LIMIT = 56 * 1024 * 1024
TS = 512
TS_C = 256
T_S5 = 256
TQ = 512
ATTN_HEADS = 2


def _call(body, **kw):
    return pl.pallas_call(body, **kw)


def _params(*sem):
    return pltpu.CompilerParams(dimension_semantics=sem if sem else None,
                                vmem_limit_bytes=V7X_VMEM_LIMIT)


def _nn(a, b):
    return jnp.dot(a.astype(BF16), b.astype(BF16), preferred_element_type=F32)


def _nt(a, b):
    return lax.dot_general(a.astype(BF16), b.astype(BF16), (((1,), (1,)), ((), ())),
                           preferred_element_type=F32)


def _tn(a, b):
    return lax.dot_general(a.astype(BF16), b.astype(BF16), (((0,), (0,)), ((), ())),
                           preferred_element_type=F32)


@jax.custom_vjp
def _bdot(a, b):
    return _nn(a, b)


def _bdot_fwd(a, b):
    return _nn(a, b), (a, b)


def _bdot_bwd(res, g):
    a, b = res
    return _nt(g, b), _tn(a, g)


_bdot.defvjp(_bdot_fwd, _bdot_bwd)


@jax.custom_vjp
def _bdot_nt(a, b):
    return _nt(a, b)


def _bdot_nt_fwd(a, b):
    return _nt(a, b), (a, b)


def _bdot_nt_bwd(res, g):
    a, b = res
    return _nn(g, b), _tn(g, a)


_bdot_nt.defvjp(_bdot_nt_fwd, _bdot_nt_bwd)


def _ln(r, g, b):
    mu = jnp.mean(r, axis=-1, keepdims=True)
    xc = r - mu
    var = jnp.mean(xc * xc, axis=-1, keepdims=True)
    return xc * lax.rsqrt(var + LN_EPS) * g + b


def _rows(shape):
    return lax.broadcasted_iota(jnp.int32, shape, 0)


def _lanes(shape):
    return lax.broadcasted_iota(jnp.int32, shape, 1)


def mm_nn(a, w, tn, name, mod=None, rows=None, out_dtype=F32):
    s, k = a.shape
    n = w.shape[1]
    ts = min(TS, s)

    def body(*refs):
        if mod is None:
            a_ref, w_ref, o_ref = refs
            h = a_ref[...]
        else:
            a_ref, m_ref, w_ref, o_ref = refs
            h = a_ref[...] * (1.0 + m_ref[rows[1]:rows[1] + 1, :]) + m_ref[rows[0]:rows[0] + 1, :]
        o_ref[...] = jnp.dot(h.astype(BF16), w_ref[...], preferred_element_type=F32).astype(out_dtype)

    in_specs = [pl.BlockSpec((ts, k), lambda j, i: (i, 0))]
    args = [a]
    if mod is not None:
        in_specs.append(pl.BlockSpec((8, k), lambda j, i: (0, 0)))
        args.append(mod)
    in_specs.append(pl.BlockSpec((k, tn), lambda j, i: (0, j)))
    args.append(w)
    return _call(body, name=name, grid=(n // tn, s // ts), in_specs=in_specs,
                 out_specs=pl.BlockSpec((ts, tn), lambda j, i: (i, j)),
                 out_shape=jax.ShapeDtypeStruct((s, n), out_dtype),
                 compiler_params=_params("arbitrary", "arbitrary"))(*args)


def mm_nt(dy, w, name):
    s, n = dy.shape
    k = w.shape[0]
    ts = min(TS, s)

    def body(dy_ref, w_ref, o_ref):
        o_ref[...] = _nt(dy_ref[...], w_ref[...])

    return _call(body, name=name, grid=(s // ts,),
                 in_specs=[pl.BlockSpec((ts, n), lambda i: (i, 0)),
                           pl.BlockSpec((k, n), lambda i: (0, 0))],
                 out_specs=pl.BlockSpec((ts, k), lambda i: (i, 0)),
                 out_shape=jax.ShapeDtypeStruct((s, k), F32),
                 compiler_params=_params("arbitrary"))(dy, w)


def mm_nt_mod(dy, w, x, dres, mod, rows, name):
    s, n = dy.shape
    k = w.shape[0]
    ts = min(TS, s)

    def body(dy_ref, w_ref, x_ref, r_ref, m_ref, dx_ref, dsh_ref, dsc_ref):
        @pl.when(pl.program_id(0) == 0)
        def _():
            dsh_ref[...] = jnp.zeros_like(dsh_ref)
            dsc_ref[...] = jnp.zeros_like(dsc_ref)

        dh = _nt(dy_ref[...], w_ref[...])
        dx_ref[...] = r_ref[...] + dh * (1.0 + m_ref[rows[1]:rows[1] + 1, :])
        dsh_ref[...] += jnp.sum(dh, axis=0, keepdims=True)
        dsc_ref[...] += jnp.sum(dh * x_ref[...], axis=0, keepdims=True)

    row = pl.BlockSpec((1, k), lambda i: (0, 0))
    tile = pl.BlockSpec((ts, k), lambda i: (i, 0))
    return _call(body, name=name, grid=(s // ts,),
                 in_specs=[pl.BlockSpec((ts, n), lambda i: (i, 0)),
                           pl.BlockSpec((k, n), lambda i: (0, 0)), tile, tile,
                           pl.BlockSpec((8, k), lambda i: (0, 0))],
                 out_specs=[tile, row, row],
                 out_shape=[jax.ShapeDtypeStruct((s, k), F32),
                            jax.ShapeDtypeStruct((1, k), F32),
                            jax.ShapeDtypeStruct((1, k), F32)],
                 compiler_params=_params("arbitrary"))(dy, w, x, dres, mod)


def mm_tn(a, dy, tn, name, mod=None, rows=None):
    s, k = a.shape
    n = dy.shape[1]
    ts = min(TS, s)

    def body(*refs):
        if mod is None:
            a_ref, dy_ref, o_ref = refs
            h = a_ref[...]
        else:
            a_ref, m_ref, dy_ref, o_ref = refs
            h = a_ref[...] * (1.0 + m_ref[rows[1]:rows[1] + 1, :]) + m_ref[rows[0]:rows[0] + 1, :]

        @pl.when(pl.program_id(1) == 0)
        def _():
            o_ref[...] = jnp.zeros_like(o_ref)

        o_ref[...] += _tn(h, dy_ref[...])

    in_specs = [pl.BlockSpec((ts, k), lambda j, i: (i, 0))]
    args = [a]
    if mod is not None:
        in_specs.append(pl.BlockSpec((8, k), lambda j, i: (0, 0)))
        args.append(mod)
    in_specs.append(pl.BlockSpec((ts, tn), lambda j, i: (i, j)))
    args.append(dy)
    return _call(body, name=name, grid=(n // tn, s // ts), in_specs=in_specs,
                 out_specs=pl.BlockSpec((k, tn), lambda j, i: (0, j)),
                 out_shape=jax.ShapeDtypeStruct((k, n), F32),
                 compiler_params=_params("arbitrary", "arbitrary"))(*args)


def _post_fn(x, br, gate, lg, lb):
    return _ln(DN_ALPHA * x + (1.0 + gate) * br, lg, lb)


def post_fwd(x, br, mod, grow, lg, lb, name):
    s = x.shape[0]
    ts = min(TS, s)

    def body(x_ref, b_ref, m_ref, lg_ref, lb_ref, o_ref):
        o_ref[...] = _post_fn(x_ref[...], b_ref[...], m_ref[grow:grow + 1, :], lg_ref[...], lb_ref[...])

    tile = pl.BlockSpec((ts, D), lambda i: (i, 0))
    row = pl.BlockSpec((1, D), lambda i: (0, 0))
    return _call(body, name=name, grid=(s // ts,),
                 in_specs=[tile, tile, pl.BlockSpec((8, D), lambda i: (0, 0)), row, row],
                 out_specs=tile, out_shape=jax.ShapeDtypeStruct((s, D), F32),
                 compiler_params=_params("arbitrary"))(x, br, mod, lg, lb)


def post_bwd(x, br, mod, grow, lg, lb, dy, name):
    s = x.shape[0]
    ts = min(TS, s)

    def body(x_ref, b_ref, m_ref, lg_ref, lb_ref, dy_ref, dx_ref, db_ref, dg_ref, dlg_ref, dlb_ref):
        @pl.when(pl.program_id(0) == 0)
        def _():
            dg_ref[...] = jnp.zeros_like(dg_ref)
            dlg_ref[...] = jnp.zeros_like(dlg_ref)
            dlb_ref[...] = jnp.zeros_like(dlb_ref)

        _, vjp = jax.vjp(_post_fn, x_ref[...], b_ref[...], m_ref[grow:grow + 1, :], lg_ref[...], lb_ref[...])
        dx, db, dg, dlg, dlb = vjp(dy_ref[...])
        dx_ref[...] = dx
        db_ref[...] = db.astype(BF16)
        dg_ref[...] += dg
        dlg_ref[...] += dlg
        dlb_ref[...] += dlb

    tile = pl.BlockSpec((ts, D), lambda i: (i, 0))
    row = pl.BlockSpec((1, D), lambda i: (0, 0))
    rs = jax.ShapeDtypeStruct((1, D), F32)
    return _call(body, name=name, grid=(s // ts,),
                 in_specs=[tile, tile, pl.BlockSpec((8, D), lambda i: (0, 0)), row, row, tile],
                 out_specs=[tile, tile, row, row, row],
                 out_shape=[jax.ShapeDtypeStruct((s, D), F32), jax.ShapeDtypeStruct((s, D), BF16), rs, rs, rs],
                 compiler_params=_params("arbitrary"))(x, br, mod, lg, lb, dy)


def loss_kernel(y, target):
    s = y.shape[0]
    ts = min(TS, s)

    def body(y_ref, t_ref, l_ref, dy_ref):
        @pl.when(pl.program_id(0) == 0)
        def _():
            l_ref[...] = jnp.zeros_like(l_ref)

        err = y_ref[...] - t_ref[...]
        dy_ref[...] = err * (1.0 / D)
        per_tok = jnp.mean(err * err, axis=-1, keepdims=True)
        l_ref[...] += 0.5 * jnp.sum(per_tok)

    tile = pl.BlockSpec((ts, D), lambda i: (i, 0))
    return _call(body, name="loss", grid=(s // ts,), in_specs=[tile, tile],
                 out_specs=[pl.BlockSpec((8, 128), lambda i: (0, 0)), tile],
                 out_shape=[jax.ShapeDtypeStruct((8, 128), F32), jax.ShapeDtypeStruct((s, D), F32)],
                 compiler_params=_params("arbitrary"))(y, target)


def _gm_head(u, v, lg, lb, w, bs):
    t = u.shape[0]
    causal = _rows((GM_C, GM_C)) >= _lanes((GM_C, GM_C))
    vn = _ln(v, lg, lb)
    wm = jnp.where(causal, w, 0.0)
    chunks = []
    for n in range(t // GM_C):
        rs = slice(n * GM_C, (n + 1) * GM_C)
        chunks.append(u[rs] * (_bdot(wm, vn[rs]) + bs))
    return jnp.concatenate(chunks, axis=0)


def gm_fwd(p, lg, lb, ws, bst):
    s = p.shape[0]
    ts = min(TS_C, s)

    def body(u_ref, v_ref, lg_ref, lb_ref, ws_ref, bs_ref, o_ref):
        for h in range(GM_H):
            sl = slice(h * HD, (h + 1) * HD)
            o_ref[:, sl] = _gm_head(u_ref[:, sl], v_ref[:, sl], lg_ref[:, sl], lb_ref[:, sl], ws_ref[h],
                                    bs_ref[:, h:h + 1]).astype(BF16)

    full = lambda shape: pl.BlockSpec(shape, lambda i: (0,) * len(shape))
    return _call(body, name="gm_fwd", grid=(s // ts,),
                 in_specs=[pl.BlockSpec((ts, GM_W), lambda i: (i, 0)), pl.BlockSpec((ts, GM_W), lambda i: (i, 1)),
                           full((1, GM_W)), full((1, GM_W)), full((GM_H, GM_C, GM_C)), full((GM_C, GM_H))],
                 out_specs=pl.BlockSpec((ts, GM_W), lambda i: (i, 0)),
                 out_shape=jax.ShapeDtypeStruct((s, GM_W), BF16),
                 compiler_params=_params("arbitrary"))(p, p, lg, lb, ws, bst)


def gm_bwd(p, lg, lb, ws, bst, dmix):
    s = p.shape[0]
    ts = min(TS_C, s)

    def body(u_ref, v_ref, lg_ref, lb_ref, ws_ref, bs_ref, dy_ref, duv_ref, dlg_ref, dlb_ref, dws_ref, dbs_ref):
        @pl.when(pl.program_id(0) == 0)
        def _():
            dlg_ref[...] = jnp.zeros_like(dlg_ref)
            dlb_ref[...] = jnp.zeros_like(dlb_ref)
            dws_ref[...] = jnp.zeros_like(dws_ref)
            dbs_ref[...] = jnp.zeros_like(dbs_ref)

        for h in range(GM_H):
            sl = slice(h * HD, (h + 1) * HD)
            _, vjp = jax.vjp(_gm_head, u_ref[:, sl], v_ref[:, sl], lg_ref[:, sl], lb_ref[:, sl], ws_ref[h],
                             bs_ref[:, h:h + 1])
            du, dv, dlg, dlb, dw, dbs = vjp(dy_ref[:, sl])
            duv_ref[:, sl] = du.astype(BF16)
            duv_ref[:, GM_W + h * HD:GM_W + (h + 1) * HD] = dv.astype(BF16)
            dlg_ref[:, sl] += dlg
            dlb_ref[:, sl] += dlb
            dws_ref[h] += dw
            dbs_ref[:, h:h + 1] += dbs

    full = lambda shape: pl.BlockSpec(shape, lambda i: (0,) * len(shape))
    return _call(body, name="gm_bwd", grid=(s // ts,),
                 in_specs=[pl.BlockSpec((ts, GM_W), lambda i: (i, 0)), pl.BlockSpec((ts, GM_W), lambda i: (i, 1)),
                           full((1, GM_W)), full((1, GM_W)), full((GM_H, GM_C, GM_C)), full((GM_C, GM_H)),
                           pl.BlockSpec((ts, GM_W), lambda i: (i, 0))],
                 out_specs=[pl.BlockSpec((ts, 2 * GM_W), lambda i: (i, 0)), full((1, GM_W)), full((1, GM_W)),
                            full((GM_H, GM_C, GM_C)), full((GM_C, GM_H))],
                 out_shape=[jax.ShapeDtypeStruct((s, 2 * GM_W), BF16), jax.ShapeDtypeStruct((1, GM_W), F32),
                            jax.ShapeDtypeStruct((1, GM_W), F32), jax.ShapeDtypeStruct((GM_H, GM_C, GM_C), F32),
                            jax.ShapeDtypeStruct((GM_C, GM_H), F32)],
                 compiler_params=_params("arbitrary"))(p, p, lg, lb, ws, bst, dmix)


def _s5_prep_fn(lr, li, ldt, bt):
    dt = jnp.exp(ldt)
    er = jnp.exp(lr * dt)
    ar = er * jnp.cos(li * dt)
    ai = er * jnp.sin(li * dt)
    den = lr * lr + li * li
    nr = ar - 1.0
    cr = (nr * lr + ai * li) / den
    ci = (ai * lr - nr * li) / den
    br, bi = bt[:, :S5_N], bt[:, S5_N:]
    return ar, ai, jnp.concatenate([cr * br - ci * bi, cr * bi + ci * br], axis=1)


def s5_prep_fwd(lr, li, ldt, bt):
    def body(lr_ref, li_ref, ldt_ref, bt_ref, a_ref, bb_ref):
        ar, ai, bb = _s5_prep_fn(lr_ref[...], li_ref[...], ldt_ref[...], bt_ref[...])
        a_ref[...] = jnp.concatenate([ar, ai, jnp.zeros((6, S5_N), F32)], axis=0)
        bb_ref[...] = bb

    return _call(body, name="s5_prep_fwd",
                 out_shape=[jax.ShapeDtypeStruct((8, S5_N), F32), jax.ShapeDtypeStruct((S5_H, 2 * S5_N), F32)])(lr, li, ldt, bt)


def s5_prep_bwd(lr, li, ldt, bt, da, dbb):
    def body(lr_ref, li_ref, ldt_ref, bt_ref, da_ref, dbb_ref, dlr_ref, dli_ref, dldt_ref, dbt_ref):
        _, vjp = jax.vjp(_s5_prep_fn, lr_ref[...], li_ref[...], ldt_ref[...], bt_ref[...])
        dlr, dli, dldt, dbt = vjp((da_ref[0:1, :], da_ref[1:2, :], dbb_ref[...]))
        dlr_ref[...] = dlr
        dli_ref[...] = dli
        dbt_ref[...] = dbt
        group = (_rows((S5_N, 128)) // S5_P == _lanes((S5_N, 128))).astype(F32)
        dldt_ref[...] = jnp.dot(jnp.broadcast_to(dldt, (8, S5_N)), group, precision=lax.Precision.HIGHEST,
                                preferred_element_type=F32)[0:1, :]

    r = jax.ShapeDtypeStruct((1, S5_N), F32)
    return _call(body, name="s5_prep_bwd",
                 out_shape=[r, r, jax.ShapeDtypeStruct((1, 128), F32),
                            jax.ShapeDtypeStruct((S5_H, 2 * S5_N), F32)])(lr, li, ldt, bt, da, dbb)


def _s5_out_fn(x, u, cbd, drow, wg, bg):
    y = _bdot_nt(x[:, :S5_N], cbd[:, :S5_N]) - _bdot_nt(x[:, S5_N:], cbd[:, S5_N:]) + drow * u
    y = jax.nn.gelu(y)
    gate = _bdot_nt(y, wg) + bg
    return y * jax.nn.sigmoid(gate)


def _scan_steps(t):
    return int(math.log2(t))


def s5_fwd(p, arow, bbd, cbd, drow, wg, bg):
    s = p.shape[0]
    t = min(T_S5, s)

    def body(u_ref, a_ref, bbd_ref, cbd_ref, d_ref, wg_ref, bg_ref, y_ref, st_ref, carry):
        @pl.when(pl.program_id(0) == 0)
        def _():
            carry[...] = jnp.zeros_like(carry)

        u = u_ref[...]
        bu = _nn(u, bbd_ref[...])
        ar, ai = a_ref[0:1, :], a_ref[1:2, :]
        cr, ci = carry[0:1, :S5_N], carry[0:1, S5_N:]
        rid = _rows((t, S5_N))
        first = rid == 0
        xr = bu[:, :S5_N] + jnp.where(first, ar * cr - ai * ci, 0.0)
        xi = bu[:, S5_N:] + jnp.where(first, ar * ci + ai * cr, 0.0)
        pr, pi = ar, ai
        for k in range(_scan_steps(t)):
            d = 1 << k
            keep = rid >= d
            sr = jnp.where(keep, pltpu.roll(xr, d, 0), 0.0)
            si = jnp.where(keep, pltpu.roll(xi, d, 0), 0.0)
            xr, xi = xr + pr * sr - pi * si, xi + pr * si + pi * sr
            pr, pi = pr * pr - pi * pi, 2.0 * pr * pi
        st_ref[:, :S5_N] = xr
        st_ref[:, S5_N:] = xi
        carry[0:1, :S5_N] = xr[t - 1:t, :]
        carry[0:1, S5_N:] = xi[t - 1:t, :]
        x = jnp.concatenate([xr, xi], axis=1)
        y_ref[...] = _s5_out_fn(x, u, cbd_ref[...], d_ref[...], wg_ref[...], bg_ref[...]).astype(BF16)

    full = lambda shape: pl.BlockSpec(shape, lambda i: (0,) * len(shape))
    return _call(body, name="s5_fwd", grid=(s // t,),
                 in_specs=[pl.BlockSpec((t, S5_W), lambda i: (i, 2)), full((8, S5_N)), full((S5_W, 2 * S5_N)),
                           full((S5_W, 2 * S5_N)), full((1, S5_W)), full((S5_W, S5_W)), full((1, S5_W))],
                 out_specs=[pl.BlockSpec((t, S5_W), lambda i: (i, 0)), pl.BlockSpec((t, 2 * S5_N), lambda i: (i, 0))],
                 out_shape=[jax.ShapeDtypeStruct((s, S5_W), BF16), jax.ShapeDtypeStruct((s, 2 * S5_N), F32)],
                 scratch_shapes=[pltpu.VMEM((8, 2 * S5_N), F32)],
                 compiler_params=_params("arbitrary"))(p, arow, bbd, cbd, drow, wg, bg)


def s5_bwd(p, st, arow, bbd, cbd, drow, wg, bg, dmix):
    s = p.shape[0]
    t = min(T_S5, s)
    nc = s // t

    def body(u_ref, st_ref, prev_ref, a_ref, bbd_ref, cbd_ref, d_ref, wg_ref, bg_ref, dy_ref,
             du_ref, da_ref, dbbd_ref, dcbd_ref, dd_ref, dwg_ref, dbg_ref, carry):
        i = pl.program_id(0)

        @pl.when(i == 0)
        def _():
            carry[...] = jnp.zeros_like(carry)
            for r in (da_ref, dbbd_ref, dcbd_ref, dd_ref, dwg_ref, dbg_ref):
                r[...] = jnp.zeros_like(r)

        u = u_ref[...]
        x = st_ref[...]
        _, vjp = jax.vjp(_s5_out_fn, x, u, cbd_ref[...], d_ref[...], wg_ref[...], bg_ref[...])
        dx, du1, dcbd, dd, dwg, dbg = vjp(dy_ref[...])
        ar, ai = a_ref[0:1, :], a_ref[1:2, :]
        cr, ci = carry[0:1, :S5_N], carry[0:1, S5_N:]
        rid = _rows((t, S5_N))
        last = rid == t - 1
        gr = dx[:, :S5_N] + jnp.where(last, ar * cr + ai * ci, 0.0)
        gi = dx[:, S5_N:] + jnp.where(last, ar * ci - ai * cr, 0.0)
        pr, pi = ar, -ai
        for k in range(_scan_steps(t)):
            d = 1 << k
            keep = rid < t - d
            sr = jnp.where(keep, pltpu.roll(gr, t - d, 0), 0.0)
            si = jnp.where(keep, pltpu.roll(gi, t - d, 0), 0.0)
            gr, gi = gr + pr * sr - pi * si, gi + pr * si + pi * sr
            pr, pi = pr * pr - pi * pi, 2.0 * pr * pi
        carry[0:1, :S5_N] = gr[0:1, :]
        carry[0:1, S5_N:] = gi[0:1, :]
        has_prev = (i < nc - 1).astype(F32)
        top_r = prev_ref[7:8, :S5_N] * has_prev
        top_i = prev_ref[7:8, S5_N:] * has_prev
        xpr = jnp.where(rid == 0, top_r, pltpu.roll(x[:, :S5_N], 1, 0))
        xpi = jnp.where(rid == 0, top_i, pltpu.roll(x[:, S5_N:], 1, 0))
        da_ref[0:1, :] += jnp.sum(xpr * gr + xpi * gi, axis=0, keepdims=True)
        da_ref[1:2, :] += jnp.sum(xpr * gi - xpi * gr, axis=0, keepdims=True)
        g = jnp.concatenate([gr, gi], axis=1)
        dbbd_ref[...] += _tn(u, g)
        du_ref[...] = (_nt(g, bbd_ref[...]) + du1).astype(BF16)
        dcbd_ref[...] += dcbd
        dd_ref[...] += dd
        dwg_ref[...] += dwg
        dbg_ref[...] += dbg

    full = lambda shape: pl.BlockSpec(shape, lambda i: (0,) * len(shape))
    rev = lambda col: (lambda i: (nc - 1 - i, col))
    prev_map = lambda i: (jnp.maximum((nc - 1 - i) * (t // 8) - 1, 0), 0)
    return _call(body, name="s5_bwd", grid=(nc,),
                 in_specs=[pl.BlockSpec((t, S5_W), rev(2)), pl.BlockSpec((t, 2 * S5_N), rev(0)),
                           pl.BlockSpec((8, 2 * S5_N), prev_map), full((8, S5_N)), full((S5_W, 2 * S5_N)),
                           full((S5_W, 2 * S5_N)), full((1, S5_W)), full((S5_W, S5_W)), full((1, S5_W)),
                           pl.BlockSpec((t, S5_W), rev(1))],
                 out_specs=[pl.BlockSpec((t, S5_W), rev(0)), full((8, S5_N)), full((S5_W, 2 * S5_N)),
                            full((S5_W, 2 * S5_N)), full((1, S5_W)), full((S5_W, S5_W)), full((1, S5_W))],
                 out_shape=[jax.ShapeDtypeStruct((s, S5_W), BF16), jax.ShapeDtypeStruct((8, S5_N), F32),
                            jax.ShapeDtypeStruct((S5_W, 2 * S5_N), F32), jax.ShapeDtypeStruct((S5_W, 2 * S5_N), F32),
                            jax.ShapeDtypeStruct((1, S5_W), F32), jax.ShapeDtypeStruct((S5_W, S5_W), F32),
                            jax.ShapeDtypeStruct((1, S5_W), F32)],
                 scratch_shapes=[pltpu.VMEM((8, 2 * S5_N), F32)],
                 compiler_params=_params("arbitrary"))(p, st, st, arow, bbd, cbd, drow, wg, bg, dmix)


def _cum_steps(s):
    return int(math.ceil(math.log2(s)))


V_BLK = (2 * GM_W + S5_W + 2 * FX_W) // 128


AUG = 2 * HD
BIAS_COL = HD
FQ_COL = HD + 3
PAIR_W = 256


def _split3(f):
    hi = f.astype(BF16).astype(F32)
    r = f - hi
    mid = r.astype(BF16).astype(F32)
    lo = (r - mid).astype(BF16).astype(F32)
    return hi, mid, lo


def fox_prep(p, bf):
    s = p.shape[0]
    ts = min(TS, s)
    scale = HD ** -0.5

    def body(q0_ref, q1_ref, k0_ref, k1_ref, v0_ref, v1_ref, f_ref, bf_ref,
             qa_ref, ka_ref, qat_ref, kat_ref, vt_ref, carry):
        @pl.when(pl.program_id(0) == 0)
        def _():
            carry[...] = jnp.zeros_like(carry)

        lane = _lanes((ts, 128))
        lf = jax.nn.log_sigmoid(f_ref[...] + bf_ref[...])
        acc = jnp.where(lane < FX_H, lf, 0.0)
        rid = _rows((ts, 128))
        for k in range(_cum_steps(ts)):
            d = 1 << k
            acc = acc + jnp.where(rid >= d, pltpu.roll(acc, d, 0), 0.0)
        acc = acc + carry[0:1, :]
        carry[0:1, :] = acc[ts - 1:ts, :]

        low = lane < HD
        for h in range(FX_H):
            blk, pos = divmod(h, 4)
            pair = slice((pos // 2) * 128, (pos // 2) * 128 + 128)
            hi, mid, lo = _split3(acc[:, h:h + 1])
            one = jnp.ones((ts, 1), F32)

            def augment(ref, cols):
                x = ref[:, pair]
                if pos % 2:
                    x = pltpu.roll(x, HD, 1)
                out = jnp.where(low, x, 0.0)
                for j, cval in enumerate(cols):
                    out = jnp.where(lane == HD + j, cval, out)
                return out

            qa = augment((q0_ref, q1_ref)[blk], (one, one, one, hi, mid, lo))
            qa = jnp.where(low, qa * scale, qa)
            ka = augment((k0_ref, k1_ref)[blk], (-hi, -mid, -lo, one, one, one))
            cs = slice(h * AUG, (h + 1) * AUG)
            qa_ref[:, cs] = qa.astype(BF16)
            ka_ref[:, cs] = ka.astype(BF16)
            qat_ref[cs, :] = jnp.transpose(qa).astype(BF16)
            kat_ref[cs, :] = jnp.transpose(ka).astype(BF16)
        for j in range(FX_H // 2):
            vref = (v0_ref, v1_ref)[j // 2]
            vt_ref[j * 128:(j + 1) * 128, :] = jnp.transpose(vref[:, (j % 2) * 128:(j % 2) * 128 + 128]).astype(BF16)

    q_blk = (2 * GM_W + S5_W) // PAIR_W
    col = lambda b: pl.BlockSpec((ts, PAIR_W), lambda i: (i, b))
    wide = FX_H * AUG
    return _call(body, name="fox_prep", grid=(s // ts,),
                 in_specs=[col(q_blk), col(q_blk + 1), col(q_blk + 2), col(q_blk + 3), col(q_blk + 4), col(q_blk + 5),
                           pl.BlockSpec((ts, 128), lambda i: (i, FF_COL // 128)), pl.BlockSpec((1, 128), lambda i: (0, 0))],
                 out_specs=[pl.BlockSpec((ts, wide), lambda i: (i, 0)), pl.BlockSpec((ts, wide), lambda i: (i, 0)),
                            pl.BlockSpec((wide, ts), lambda i: (0, i)), pl.BlockSpec((wide, ts), lambda i: (0, i)),
                            pl.BlockSpec((FX_W, ts), lambda i: (0, i))],
                 out_shape=[jax.ShapeDtypeStruct((s, wide), BF16), jax.ShapeDtypeStruct((s, wide), BF16),
                            jax.ShapeDtypeStruct((wide, s), BF16), jax.ShapeDtypeStruct((wide, s), BF16),
                            jax.ShapeDtypeStruct((FX_W, s), BF16)],
                 scratch_shapes=[pltpu.VMEM((8, 128), F32)],
                 compiler_params=_params("arbitrary"))(p, p, p, p, p, p, p, bf)


def fox_prep_grad(p, bf, dfq, dfk):
    s = p.shape[0]
    ts = min(TS, s)
    ns = s // ts

    def body(f_ref, bf_ref, dfq_ref, dfk_ref, df_ref, dbf_ref, carry):
        @pl.when(pl.program_id(0) == 0)
        def _():
            carry[...] = jnp.zeros_like(carry)
            dbf_ref[...] = jnp.zeros_like(dbf_ref)

        lane = _lanes((ts, 128))
        acc = jnp.zeros((ts, 128), F32)
        for h in range(FX_H):
            c = (h // 2) * 128 + h % 2
            acc = jnp.where(lane == h, dfq_ref[:, c:c + 1] + dfk_ref[:, c:c + 1], acc)
        rid = _rows((ts, 128))
        for k in range(_cum_steps(ts)):
            d = 1 << k
            acc = acc + jnp.where(rid < ts - d, pltpu.roll(acc, ts - d, 0), 0.0)
        acc = acc + carry[0:1, :]
        carry[0:1, :] = acc[0:1, :]
        z = f_ref[...] + bf_ref[...]
        df = jnp.where(lane < FX_H, acc * jax.nn.sigmoid(-z), 0.0)
        df_ref[...] = df.astype(BF16)
        dbf_ref[...] += jnp.sum(df, axis=0, keepdims=True)

    rev = lambda i: (ns - 1 - i, 0)
    return _call(body, name="fox_prep_grad", grid=(ns,),
                 in_specs=[pl.BlockSpec((ts, 128), lambda i: (ns - 1 - i, FF_COL // 128)),
                           pl.BlockSpec((1, 128), lambda i: (0, 0)),
                           pl.BlockSpec((ts, FX_W), rev), pl.BlockSpec((ts, FX_W), rev)],
                 out_specs=[pl.BlockSpec((ts, 128), rev), pl.BlockSpec((1, 128), lambda i: (0, 0))],
                 out_shape=[jax.ShapeDtypeStruct((s, 128), BF16), jax.ShapeDtypeStruct((1, 128), F32)],
                 scratch_shapes=[pltpu.VMEM((8, 128), F32)],
                 compiler_params=_params("arbitrary"))(p, bf, dfq, dfk)


def attn(qat, ka, vt):
    s = ka.shape[0]
    tq = min(TQ, s)
    nq = s // tq

    nh = ATTN_HEADS

    def body(qat_ref, ka_ref, vt_ref, o_ref, lse_ref):
        qi = pl.program_id(1)
        causal = _rows((tq, tq)) <= _lanes((tq, tq))
        lse_ref[...] = jnp.zeros_like(lse_ref)

        def step(kj, carry, masked):
            off = pl.multiple_of(kj * tq, tq)
            out = []
            for hh in range(nh):
                m, l, acc = carry[hh]
                st = jnp.dot(ka_ref[pl.ds(off, tq), hh * AUG:(hh + 1) * AUG], qat_ref[hh * AUG:(hh + 1) * AUG, :],
                             preferred_element_type=F32)
                if masked:
                    st = jnp.where(causal, st, NEG_INF)
                m_new = jnp.maximum(m, jnp.max(st, axis=0, keepdims=True))
                alpha = jnp.exp(m - m_new)
                pt = jnp.exp(st - m_new)
                v = vt_ref[hh * HD:(hh + 1) * HD, pl.ds(off, tq)]
                out.append((m_new, alpha * l + jnp.sum(pt, axis=0, keepdims=True),
                            alpha * acc + jnp.dot(v, pt.astype(BF16), preferred_element_type=F32)))
            return tuple(out)

        init = tuple((jnp.full((1, tq), NEG_INF, F32), jnp.zeros((1, tq), F32), jnp.zeros((HD, tq), F32))
                     for _ in range(nh))
        carry = lax.fori_loop(0, qi, lambda kj, c: step(kj, c, False), init)
        carry = step(qi, carry, True)
        for hh in range(nh):
            m, l, _ = carry[hh]
            lse_ref[hh // 2, hh % 2:hh % 2 + 1, :] = m + jnp.log(l)
        for j in range(nh // 2):
            pair = jnp.concatenate([carry[2 * j][2] / carry[2 * j][1], carry[2 * j + 1][2] / carry[2 * j + 1][1]], axis=0)
            o_ref[:, j * 128:(j + 1) * 128] = jnp.transpose(pair).astype(BF16)

    return _call(body, name="attn", grid=(FX_H // nh, nq),
                 in_specs=[pl.BlockSpec((nh * AUG, tq), lambda h, i: (h, i)),
                           pl.BlockSpec((s, nh * AUG), lambda h, i: (0, h)),
                           pl.BlockSpec((nh * HD, s), lambda h, i: (h, 0))],
                 out_specs=[pl.BlockSpec((tq, nh * HD), lambda h, i: (i, h)),
                            pl.BlockSpec((nh // 2, 8, tq), lambda h, i: (h, 0, i))],
                 out_shape=[jax.ShapeDtypeStruct((s, FX_W), BF16), jax.ShapeDtypeStruct((FX_H // 2, 8, s), F32)],
                 compiler_params=_params("arbitrary", "arbitrary"))(qat, ka, vt)


def attn_grad(qa, qat, ka, kat, p, o, lse, dmix):
    s = qa.shape[0]
    tq = min(TQ, s)
    nq = s // tq
    scale = HD ** -0.5

    def body(qa_ref, qat_ref, ka_ref, kat_ref, v_ref, o_ref, lse_ref, do_ref,
             dq_ref, dk_ref, dv_ref, dfq_ref, dfk_ref, dot_scr, delta, dqt):
        kj = pl.program_id(1)
        lane = _lanes((tq, 128))
        low = lane < HD
        causal = _rows((tq, tq)) <= _lanes((tq, tq))

        @pl.when(kj == 0)
        def _():
            dqt[...] = jnp.zeros_like(dqt)
            delta[...] = jnp.zeros_like(delta)

            def prep(c, _):
                rows = pl.ds(pl.multiple_of(c * tq, tq), tq)
                do = do_ref[rows, :]
                pt = jnp.transpose(do * o_ref[rows, :].astype(F32))
                delta[0:1, rows] = jnp.sum(pt[:HD], axis=0, keepdims=True)
                delta[1:2, rows] = jnp.sum(pt[HD:], axis=0, keepdims=True)
                dot_scr[:, rows] = jnp.transpose(do).astype(BF16)
                return 0

            lax.fori_loop(0, nq, prep, 0)

        v = v_ref[...]
        vms = [jnp.where(low, v, 0.0).astype(BF16), jnp.where(low, 0.0, v).astype(BF16)]

        def tile(qi, carry, masked):
            cols = pl.ds(pl.multiple_of(qi * tq, tq), tq)
            do = do_ref[cols, :].astype(BF16)
            out = []
            for hh in range(2):
                cs = slice(hh * AUG, (hh + 1) * AUG)
                dka, dv = carry[hh]
                st = jnp.dot(ka_ref[:, cs], qat_ref[cs, cols], preferred_element_type=F32)
                if masked:
                    st = jnp.where(causal, st, NEG_INF)
                pt = jnp.exp(st - lse_ref[0, hh:hh + 1, cols])
                dv = dv + jnp.dot(pt.astype(BF16), do, preferred_element_type=F32)
                dpt = jnp.dot(vms[hh], dot_scr[:, cols], preferred_element_type=F32)
                dsb = (pt * (dpt - delta[hh:hh + 1, cols])).astype(BF16)
                dka = dka + jnp.dot(dsb, qa_ref[cols, cs], preferred_element_type=F32)
                dqt[hh, :, cols] += jnp.dot(kat_ref[cs, :], dsb, preferred_element_type=F32)
                out.append((dka, dv))
            return tuple(out)

        init = tuple((jnp.zeros((tq, AUG), F32), jnp.zeros((tq, 128), F32)) for _ in range(2))
        carry = tile(kj, init, True)
        carry = lax.fori_loop(kj + 1, nq, lambda qi, c: tile(qi, c, False), carry)
        dks = [carry[0][0], carry[1][0]]
        dvs = [carry[0][1], carry[1][1]]
        dv_ref[...] = jnp.where(low, dvs[0], dvs[1]).astype(BF16)
        dk_ref[...] = jnp.where(low, dks[0], pltpu.roll(dks[1], HD, 1)).astype(BF16)
        dfk_ref[...] = jnp.where(lane == 0, -dks[0][:, BIAS_COL:BIAS_COL + 1],
                                 jnp.where(lane == 1, -dks[1][:, BIAS_COL:BIAS_COL + 1], 0.0))

        @pl.when(kj == nq - 1)
        def _():
            def finish(c, _):
                rows = pl.ds(pl.multiple_of(c * tq, tq), tq)
                t0 = jnp.transpose(dqt[0, :, rows])
                t1 = jnp.transpose(dqt[1, :, rows])
                dq_ref[rows, :] = (jnp.where(low, t0, pltpu.roll(t1, HD, 1)) * scale).astype(BF16)
                dfq_ref[rows, :] = jnp.where(lane == 0, t0[:, FQ_COL:FQ_COL + 1],
                                             jnp.where(lane == 1, t1[:, FQ_COL:FQ_COL + 1], 0.0))
                return 0

            lax.fori_loop(0, nq, finish, 0)

    seq128 = lambda blk: pl.BlockSpec((s, 128), lambda h, j: (0, blk + h))
    tile128 = pl.BlockSpec((tq, 128), lambda h, j: (j, h))
    out_b = jax.ShapeDtypeStruct((s, FX_W), BF16)
    out_f = jax.ShapeDtypeStruct((s, FX_W), F32)
    return _call(body, name="attn_grad", grid=(FX_H // 2, nq),
                 in_specs=[pl.BlockSpec((s, 2 * AUG), lambda h, j: (0, h)), pl.BlockSpec((2 * AUG, s), lambda h, j: (h, 0)),
                           pl.BlockSpec((tq, 2 * AUG), lambda h, j: (j, h)), pl.BlockSpec((2 * AUG, tq), lambda h, j: (h, j)),
                           pl.BlockSpec((tq, 128), lambda h, j: (j, V_BLK + h)), seq128(0),
                           pl.BlockSpec((1, 8, s), lambda h, j: (h, 0, 0)), seq128(4)],
                 out_specs=[seq128(0), tile128, tile128, seq128(0), tile128],
                 out_shape=[out_b, out_b, out_b, out_f, out_f],
                 scratch_shapes=[pltpu.VMEM((128, s), BF16), pltpu.VMEM((8, s), F32), pltpu.VMEM((2, AUG, s), F32)],
                 compiler_params=_params("arbitrary", "arbitrary"))(qa, qat, ka, kat, p, o, lse, dmix)


def _shift_down(a, prev8, k):
    r = pltpu.roll(a, k, 0)
    top = jnp.where(_rows(prev8.shape) < k, pltpu.roll(prev8, k, 0), r[0:8])
    return jnp.concatenate([top, r[8:]], axis=0)


def _shift_up(a, next8, k):
    t = a.shape[0]
    r = pltpu.roll(a, t - k, 0)
    bot = jnp.where(_rows(next8.shape) >= 8 - k, pltpu.roll(next8, 8 - k, 0), r[t - 8:t])
    return jnp.concatenate([r[:t - 8], bot], axis=0)


def _conv(a, prev8, cw, cb):
    return cb + cw[0:1, :] * _shift_down(a, prev8, 2) + cw[1:2, :] * _shift_down(a, prev8, 1) + cw[2:3, :] * a


def conv_fwd(up, cw, cb):
    s = up.shape[0]
    ts = min(TS_C, s)

    def body(a_ref, g_ref, cw_ref, cb_ref, o_ref, halo):
        @pl.when(pl.program_id(0) == 0)
        def _():
            halo[...] = jnp.zeros_like(halo)

        a = a_ref[...]
        c = _conv(a, halo[...], cw_ref[...], cb_ref[...])
        o_ref[...] = (jax.nn.gelu(c) * g_ref[...]).astype(BF16)
        halo[...] = a[ts - 8:ts, :]

    return _call(body, name="conv_fwd", grid=(s // ts,),
                 in_specs=[pl.BlockSpec((ts, DFF), lambda i: (i, 0)), pl.BlockSpec((ts, DFF), lambda i: (i, 1)),
                           pl.BlockSpec((3, DFF), lambda i: (0, 0)), pl.BlockSpec((1, DFF), lambda i: (0, 0))],
                 out_specs=pl.BlockSpec((ts, DFF), lambda i: (i, 0)),
                 out_shape=jax.ShapeDtypeStruct((s, DFF), BF16),
                 scratch_shapes=[pltpu.VMEM((8, DFF), F32)],
                 compiler_params=_params("arbitrary"))(up, up, cw, cb)


def conv_bwd(up, cw, cb, dact):
    s = up.shape[0]
    ts = min(TS_C, s)
    ns = s // ts

    def body(a_ref, g_ref, prev_ref, cw_ref, cb_ref, dact_ref, dup_ref, dcw_ref, dcb_ref, halo):
        i = pl.program_id(0)

        @pl.when(i == 0)
        def _():
            halo[...] = jnp.zeros_like(halo)
            dcw_ref[...] = jnp.zeros_like(dcw_ref)
            dcb_ref[...] = jnp.zeros_like(dcb_ref)

        a = a_ref[...]
        g = g_ref[...]
        cw = cw_ref[...]
        prev8 = prev_ref[...] * (i < ns - 1).astype(F32)
        c = _conv(a, prev8, cw, cb_ref[...])
        gel, vjp = jax.vjp(jax.nn.gelu, c)
        dact = dact_ref[...]
        (dc,) = vjp(dact * g)
        up1 = _shift_up(dc, halo[...], 1)
        up2 = _shift_up(dc, halo[...], 2)
        da = cw[2:3, :] * dc + cw[1:2, :] * up1 + cw[0:1, :] * up2
        dup_ref[:, :DFF] = da.astype(BF16)
        dup_ref[:, DFF:] = (dact * gel).astype(BF16)
        dcw_ref[0:1, :] += jnp.sum(a * up2, axis=0, keepdims=True)
        dcw_ref[1:2, :] += jnp.sum(a * up1, axis=0, keepdims=True)
        dcw_ref[2:3, :] += jnp.sum(a * dc, axis=0, keepdims=True)
        dcb_ref[...] += jnp.sum(dc, axis=0, keepdims=True)
        halo[...] = dc[0:8, :]

    rev = lambda col: (lambda i: (ns - 1 - i, col))
    prev_map = lambda i: (jnp.maximum((ns - 1 - i) * (ts // 8) - 1, 0), 0)
    return _call(body, name="conv_bwd", grid=(ns,),
                 in_specs=[pl.BlockSpec((ts, DFF), rev(0)), pl.BlockSpec((ts, DFF), rev(1)),
                           pl.BlockSpec((8, DFF), prev_map), pl.BlockSpec((3, DFF), lambda i: (0, 0)),
                           pl.BlockSpec((1, DFF), lambda i: (0, 0)), pl.BlockSpec((ts, DFF), rev(0))],
                 out_specs=[pl.BlockSpec((ts, 2 * DFF), rev(0)), pl.BlockSpec((3, DFF), lambda i: (0, 0)),
                            pl.BlockSpec((1, DFF), lambda i: (0, 0))],
                 out_shape=[jax.ShapeDtypeStruct((s, 2 * DFF), BF16), jax.ShapeDtypeStruct((3, DFF), F32),
                            jax.ShapeDtypeStruct((1, DFF), F32)],
                 scratch_shapes=[pltpu.VMEM((8, DFF), F32)],
                 compiler_params=_params("arbitrary"))(up, up, up, cw, cb, dact)


def _blockdiag_expand(m):
    m4 = m.reshape(S5_H, 2, S5_G, S5_P)
    eye = jnp.eye(S5_G, dtype=bool)[:, None, None, :, None]
    return jnp.where(eye, m4[None], 0.0).reshape(S5_W, 2 * S5_N)


def _blockdiag_extract(mbd):
    m5 = mbd.reshape(S5_G, S5_H, 2, S5_G, S5_P)
    diag = jnp.stack([m5[g, :, :, g, :] for g in range(S5_G)], axis=2)
    return diag.reshape(S5_H, 2 * S5_N)


def _c_expand(c_re, c_im):
    c4 = jnp.stack([c_re, c_im], axis=2)
    eye = jnp.eye(S5_G, dtype=bool)[:, None, None, :, None]
    return jnp.where(eye, c4[:, :, :, None, :], 0.0).reshape(S5_W, 2 * S5_N)


def _c_extract(cbd):
    m5 = cbd.reshape(S5_G, S5_H, 2, S5_G, S5_P)
    d = jnp.stack([m5[g, :, :, g, :] for g in range(S5_G)], axis=0)
    return d[:, :, 0, :], d[:, :, 1, :]


def _glu_expand(w):
    eye = jnp.eye(S5_G, dtype=bool)[:, None, :, None]
    return jnp.where(eye, w[:, :, None, :], 0.0).reshape(S5_W, S5_W)


def _glu_extract(wbd):
    m4 = wbd.reshape(S5_G, S5_H, S5_G, S5_H)
    return jnp.stack([m4[g, :, g, :] for g in range(S5_G)], axis=0)


SMALL = ("b_f", "gm_ln_g", "gm_ln_b", "gm_w_s", "gm_b_s", "s5_lam_re", "s5_lam_im", "s5_log_dt", "s5_b_re", "s5_b_im",
         "s5_c_re", "s5_c_im", "s5_d", "s5_w_glu", "s5_b_glu", "ln1_g", "ln1_b", "conv_b", "ln2_g", "ln2_b")


def _layer_operands(sp, l):
    f = {}
    f["bf"] = jnp.pad(sp["b_f"][l][None, :], ((0, 0), (0, 128 - FX_H)))
    f["gm_lg"] = sp["gm_ln_g"][l].reshape(1, GM_W)
    f["gm_lb"] = sp["gm_ln_b"][l].reshape(1, GM_W)
    f["gm_ws"] = sp["gm_w_s"][l]
    f["gm_bst"] = sp["gm_b_s"][l].T
    f["lr"] = sp["s5_lam_re"][l].reshape(1, S5_N)
    f["li"] = sp["s5_lam_im"][l].reshape(1, S5_N)
    f["ldt"] = jnp.repeat(sp["s5_log_dt"][l], S5_P).reshape(1, S5_N)
    bt = lambda b: jnp.transpose(b, (2, 0, 1)).reshape(S5_H, S5_N)
    f["bt"] = jnp.concatenate([bt(sp["s5_b_re"][l]), bt(sp["s5_b_im"][l])], axis=1)
    f["cbd"] = _c_expand(sp["s5_c_re"][l], sp["s5_c_im"][l])
    f["drow"] = sp["s5_d"][l].reshape(1, S5_W)
    f["wg"] = _glu_expand(sp["s5_w_glu"][l])
    f["bg"] = sp["s5_b_glu"][l].reshape(1, S5_W)
    for n in ("ln1_g", "ln1_b", "ln2_g", "ln2_b"):
        f[n] = sp[n][l][None, :]
    f["cb"] = sp["conv_b"][l][None, :]
    return f


def layer_fwd(x, mod, w, f):
    p = mm_nn(x, w["w_in"], NP, "in_proj", mod=mod, rows=(0, 1))
    ygm = gm_fwd(p, f["gm_lg"], f["gm_lb"], f["gm_ws"], f["gm_bst"])
    arow, bbt = s5_prep_fwd(f["lr"], f["li"], f["ldt"], f["bt"])
    bbd = _blockdiag_expand(bbt)
    ys5, st = s5_fwd(p, arow, bbd, f["cbd"], f["drow"], f["wg"], f["bg"])
    qa, ka, qat, kat, vt = fox_prep(p, f["bf"])
    yfx, lse = attn(qat, ka, vt)
    mixcat = jnp.concatenate([ygm, ys5, yfx], axis=1)
    mix = mm_nn(mixcat, w["w_out"], D, "out_proj")
    x1 = post_fwd(x, mix, mod, 2, f["ln1_g"], f["ln1_b"], "post1_fwd")
    up = mm_nn(x1, w["w_up"], DFF // 2, "up_proj", mod=mod, rows=(3, 4))
    act = conv_fwd(up, w["conv_w"], f["cb"])
    ffn = mm_nn(act, w["w_down"], D, "down_proj")
    x2 = post_fwd(x1, ffn, mod, 5, f["ln2_g"], f["ln2_b"], "post2_fwd")
    saved = dict(f=f, x=x, p=p, arow=arow, bbd=bbd, st=st, qa=qa, ka=ka, qat=qat, kat=kat, yfx=yfx, lse=lse,
                 mixcat=mixcat, mix=mix, x1=x1, up=up, act=act, ffn=ffn)
    return x2, saved


def layer_bwd(dx, sv, mod, w):
    f = sv["f"]
    dx1, dffn, dg2, dlg2, dlb2 = post_bwd(sv["x1"], sv["ffn"], mod, 5, f["ln2_g"], f["ln2_b"], dx, "post2_bwd")
    g_down = mm_tn(sv["act"], dffn, D // 2, "down_dw")
    dact = mm_nt(dffn, w["w_down"], "down_dx")
    dup, dcw, dcb = conv_bwd(sv["up"], w["conv_w"], f["cb"], dact)
    g_up = mm_tn(sv["x1"], dup, DFF // 2, "up_dw", mod=mod, rows=(3, 4))
    dx1, dsh2, dsc2 = mm_nt_mod(dup, w["w_up"], sv["x1"], dx1, mod, (3, 4), "up_dx")
    dx0, dmix, dg1, dlg1, dlb1 = post_bwd(sv["x"], sv["mix"], mod, 2, f["ln1_g"], f["ln1_b"], dx1, "post1_bwd")
    g_out = mm_tn(sv["mixcat"], dmix, D, "out_dw")
    dmc = mm_nt(dmix, w["w_out"], "out_dx")
    duv, dgm_lg, dgm_lb, dgm_ws, dgm_bst = gm_bwd(sv["p"], f["gm_lg"], f["gm_lb"], f["gm_ws"], f["gm_bst"], dmc)
    du5, da, dbbd, dcbd, dd5, dwg, dbg = s5_bwd(sv["p"], sv["st"], sv["arow"], sv["bbd"], f["cbd"], f["drow"],
                                                f["wg"], f["bg"], dmc)
    dlr, dli, dldt, dbt = s5_prep_bwd(f["lr"], f["li"], f["ldt"], f["bt"], da, _blockdiag_extract(dbbd))
    dq, dk, dv, dfq, dfk = attn_grad(sv["qa"], sv["qat"], sv["ka"], sv["kat"], sv["p"], sv["yfx"], sv["lse"], dmc)
    dff, dbf = fox_prep_grad(sv["p"], f["bf"], dfq, dfk)
    dp = jnp.concatenate([duv, du5, dq, dk, dv, dff], axis=1)
    g_in = mm_tn(sv["x"], dp, NP, "in_dw", mod=mod, rows=(0, 1))
    dx, dsh1, dsc1 = mm_nt_mod(dp, w["w_in"], sv["x"], dx0, mod, (0, 1), "in_dx")

    gbig = dict(w_in=g_in, w_out=g_out, w_up=g_up, w_down=g_down, conv_w=dcw)
    dmod = jnp.concatenate([dsh1, dsc1, dg1, dsh2, dsc2, dg2], axis=0)
    dc_re, dc_im = _c_extract(dcbd)
    dbt4 = dbt.reshape(S5_H, 2, S5_G, S5_P)
    vals = dict(b_f=dbf[0, :FX_H], gm_ln_g=dgm_lg.reshape(GM_H, HD), gm_ln_b=dgm_lb.reshape(GM_H, HD),
                gm_w_s=dgm_ws, gm_b_s=dgm_bst.T, s5_lam_re=dlr.reshape(S5_G, S5_P),
                s5_lam_im=dli.reshape(S5_G, S5_P), s5_log_dt=dldt[0, :S5_G],
                s5_b_re=jnp.transpose(dbt4[:, 0], (1, 2, 0)), s5_b_im=jnp.transpose(dbt4[:, 1], (1, 2, 0)),
                s5_c_re=dc_re, s5_c_im=dc_im, s5_d=dd5.reshape(S5_G, S5_H), s5_w_glu=_glu_extract(dwg),
                s5_b_glu=dbg.reshape(S5_G, S5_H), ln1_g=dlg1[0], ln1_b=dlb1[0], conv_b=dcb[0],
                ln2_g=dlg2[0], ln2_b=dlb2[0])
    return dx, gbig, vals, dmod


def local_step(x, target, mods, big, sp):
    saved = []
    for l in range(DEPTH):
        x, sv = layer_fwd(x, mods[l], big[l], _layer_operands(sp, l))
        saved.append(sv)
    loss_tile, dx = loss_kernel(x, target)
    gbig, vals, dmods = [None] * DEPTH, [None] * DEPTH, [None] * DEPTH
    for l in reversed(range(DEPTH)):
        dx, gbig[l], vals[l], dmods[l] = layer_bwd(dx, saved[l], mods[l], big[l])
    gsm = {n: jnp.stack([v[n] for v in vals]) for n in SMALL}
    return loss_tile, dx, gbig, gsm, jnp.stack(dmods)


def _my_index():
    return 4 * lax.axis_index("x") + 2 * lax.axis_index("y") + lax.axis_index("c")


def exchange(tensors, scatter, name):
    n = len(tensors)

    def body(*refs):
        ins, outs = refs[:n], refs[n:2 * n]
        send_sems, recv_sems, local_sems = refs[2 * n:]
        x, y, c = lax.axis_index("x"), lax.axis_index("y"), lax.axis_index("c")
        me = 4 * x + 2 * y + c
        local = []
        for t in range(n):
            cp = pltpu.make_async_copy(ins[t].at[me] if scatter else ins[t], outs[t].at[me], local_sems.at[t])
            cp.start()
            local.append(cp)
        remote = []
        for m in range(1, NDEV):
            px = 1 - x if m & 4 else x
            py = 1 - y if m & 2 else y
            pc = 1 - c if m & 1 else c
            peer = 4 * px + 2 * py + pc
            for t in range(n):
                k = t * (NDEV - 1) + m - 1
                cp = pltpu.make_async_remote_copy(
                    src_ref=ins[t].at[peer] if scatter else ins[t], dst_ref=outs[t].at[me],
                    send_sem=send_sems.at[k], recv_sem=recv_sems.at[k],
                    device_id=(px, py, pc), device_id_type=MESH_IDS)
                cp.start()
                remote.append(cp)
        for cp in remote:
            cp.wait()
        for cp in local:
            cp.wait()

    hbm = pl.BlockSpec(memory_space=pltpu.HBM)
    out_shape = [jax.ShapeDtypeStruct(t.shape if scatter else (NDEV,) + t.shape, t.dtype) for t in tensors]
    return _call(body, name=name, in_specs=[hbm] * n, out_specs=[hbm] * n, out_shape=out_shape,
                 scratch_shapes=[pltpu.SemaphoreType.DMA((n * (NDEV - 1),)), pltpu.SemaphoreType.DMA((n * (NDEV - 1),)),
                                 pltpu.SemaphoreType.DMA((n,))])(*tensors)


def _peers():
    x, y, c = lax.axis_index("x"), lax.axis_index("y"), lax.axis_index("c")
    out = []
    for m in range(1, NDEV):
        px = 1 - x if m & 4 else x
        py = 1 - y if m & 2 else y
        pc = 1 - c if m & 1 else c
        out.append(((px, py, pc), 4 * px + 2 * py + pc))
    return 4 * x + 2 * y + c, out


def _split_copies(v_refs, land_refs, send_sems, recv_sems, scatter):
    me, peers = _peers()
    return [pltpu.make_async_remote_copy(
        src_ref=v_ref.at[idx] if scatter else v_ref, dst_ref=land_ref.at[me],
        send_sem=send_sems.at[t * (NDEV - 1) + k], recv_sem=recv_sems.at[t * (NDEV - 1) + k],
        device_id=pos, device_id_type=MESH_IDS)
        for t, (v_ref, land_ref) in enumerate(zip(v_refs, land_refs)) for k, (pos, idx) in enumerate(peers)]


_HBM_SPEC = pl.BlockSpec(memory_space=pltpu.HBM)
_SEM_SPEC = pl.BlockSpec(memory_space=pltpu.SEMAPHORE)
_SPLIT_EFFECT = pltpu.SideEffectType.DATAFLOW_SIDE_EFFECTING


def exchange_start(tensors, scatter, name):
    n = len(tensors)
    land_shapes = [t.shape if scatter else (NDEV,) + t.shape for t in tensors]

    def body(*refs):
        v_refs, land_refs = refs[:n], refs[n:2 * n]
        send_sems, recv_sems = refs[2 * n], refs[2 * n + 1]
        token = refs[-1]
        for cp in _split_copies(v_refs, land_refs, send_sems, recv_sems, scatter):
            cp.start()
        token[...] = jnp.zeros_like(token)

    sems = pltpu.SemaphoreType.DMA((n * (NDEV - 1),))
    out = _call(
        body, name=name,
        out_shape=(sems, sems, *[pltpu.HBM(t.shape, t.dtype) for t in tensors],
                   *[pltpu.HBM(s, t.dtype) for s, t in zip(land_shapes, tensors)], jax.ShapeDtypeStruct((8, 128), F32)),
        in_specs=(_HBM_SPEC,) * (2 * n),
        out_specs=(_SEM_SPEC, _SEM_SPEC) + (_HBM_SPEC,) * (2 * n) + (pl.BlockSpec(memory_space=pltpu.VMEM),),
        input_output_aliases={i: i + 2 for i in range(2 * n)},
        compiler_params=pltpu.CompilerParams(has_side_effects=_SPLIT_EFFECT),
    )(*[pltpu.with_memory_space_constraint(t, pltpu.HBM) for t in tensors],
      *[pltpu.with_memory_space_constraint(lax.empty(s, t.dtype), pltpu.HBM) for s, t in zip(land_shapes, tensors)])
    return out[0], out[1], list(out[2:2 + n]), list(out[2 + n:2 + 2 * n]), out[-1]


def exchange_wait(started, after, scatter, name):
    send_sems, recv_sems, v_thru, land_thru, _ = started
    n = len(v_thru)

    def body(*refs):
        v_refs, land_refs = refs[:n], refs[n:2 * n]
        for cp in _split_copies(v_refs, land_refs, refs[2 * n], refs[2 * n + 1], scatter):
            cp.wait_send()
            cp.wait_recv()

    out = _call(
        body, name=name,
        out_shape=tuple(pltpu.HBM(t.shape, t.dtype) for t in v_thru + land_thru),
        in_specs=(_HBM_SPEC,) * (2 * n) + (_SEM_SPEC, _SEM_SPEC, pl.BlockSpec(memory_space=pl.ANY)),
        out_specs=(_HBM_SPEC,) * (2 * n), input_output_aliases={i: i for i in range(2 * n)},
        compiler_params=pltpu.CompilerParams(has_side_effects=_SPLIT_EFFECT),
    )(*v_thru, *land_thru, send_sems, recv_sems, after)
    return list(out[:n]), list(out[n:])


def mod_slices(c_all, w_ada, b_loc):
    nl, _, nc = w_ada.shape

    def body(c_ref, w_ref, b_ref, o_ref):
        cv = c_ref[...]
        o_ref[0] = _nn(cv * jax.nn.sigmoid(cv), w_ref[0]) + b_ref[0]

    return _call(body, name="mod_slices", grid=(nl,),
                 in_specs=[pl.BlockSpec((NDEV, D), lambda l: (0, 0)), pl.BlockSpec((1, D, nc), lambda l: (l, 0, 0)),
                           pl.BlockSpec((1, 1, nc), lambda l: (l, 0, 0))],
                 out_specs=pl.BlockSpec((1, NDEV, nc), lambda l: (l, 0, 0)),
                 out_shape=jax.ShapeDtypeStruct((nl, NDEV, nc), F32),
                 compiler_params=_params("arbitrary"))(c_all, w_ada, b_loc.reshape(nl, 1, nc))


def ada_grad(c_all, dm_loc):
    nl, _, nc = dm_loc.shape

    def body(c_ref, d_ref, o_ref):
        cv = c_ref[...]
        o_ref[0] = _tn(cv * jax.nn.sigmoid(cv), d_ref[0])

    return _call(body, name="ada_grad", grid=(nl,),
                 in_specs=[pl.BlockSpec((NDEV, D), lambda l: (0, 0)), pl.BlockSpec((1, NDEV, nc), lambda l: (l, 0, 0))],
                 out_specs=pl.BlockSpec((1, D, nc), lambda l: (l, 0, 0)),
                 out_shape=jax.ShapeDtypeStruct((nl, D, nc), F32),
                 compiler_params=_params("arbitrary"))(c_all, dm_loc)


def sum_chunks(chunks):
    r = chunks.shape[1]

    def body(c_ref, o_ref):
        acc = c_ref[0]
        for i in range(1, NDEV):
            acc = acc + c_ref[i]
        o_ref[...] = acc

    return _call(body, name="sum_chunks", out_shape=jax.ShapeDtypeStruct((r, 128), F32))(chunks)


def _row_tile(r):
    if r <= 256:
        return r
    for t in range(256, 7, -8):
        if r % t == 0:
            return t
    return r


def adamw(w, m, v, g=None, chunks=None, name="adamw"):
    r, cdim = w.shape
    tr = _row_tile(r)
    bc1 = 1.0 - ADAM_B1 ** ADAM_STEP
    bc2 = 1.0 - ADAM_B2 ** ADAM_STEP

    def body(g_ref, w_ref, m_ref, v_ref, go_ref, d_ref, mo_ref, vo_ref):
        if chunks is None:
            grad = g_ref[...]
        else:
            grad = g_ref[0].astype(F32)
            for i in range(1, NDEV):
                grad = grad + g_ref[i].astype(F32)
        mn = ADAM_B1 * m_ref[...] + (1.0 - ADAM_B1) * grad
        vn = ADAM_B2 * v_ref[...] + (1.0 - ADAM_B2) * (grad * grad)
        m_hat = mn / bc1
        v_hat = vn / bc2
        go_ref[...] = grad
        d_ref[...] = -ADAM_LR * (m_hat / (jnp.sqrt(v_hat) + ADAM_EPS) + ADAM_WD * w_ref[...])
        mo_ref[...] = mn
        vo_ref[...] = vn

    tile = pl.BlockSpec((tr, cdim), lambda i: (i, 0))
    gspec = tile if chunks is None else pl.BlockSpec((NDEV, tr, cdim), lambda i: (0, i, 0))
    shp = jax.ShapeDtypeStruct((r, cdim), F32)
    return _call(body, name=name, grid=(r // tr,), in_specs=[gspec, tile, tile, tile],
                 out_specs=[tile] * 4, out_shape=[shp] * 4,
                 compiler_params=_params("arbitrary"))(g if chunks is None else chunks, w, m, v)


def adamw_layers(w, m, v, chunks, name):
    nl, r, cdim = w.shape
    tr = _row_tile(r)
    bc1 = 1.0 - ADAM_B1 ** ADAM_STEP
    bc2 = 1.0 - ADAM_B2 ** ADAM_STEP
    outs = [lax.empty(w.shape, F32) for _ in range(4)]
    for l in range(nl):
        def body(g_ref, w_ref, m_ref, v_ref, p0, p1, p2, p3, go_ref, d_ref, mo_ref, vo_ref):
            grad = g_ref[0].astype(F32)
            for i in range(1, NDEV):
                grad = grad + g_ref[i].astype(F32)
            mn = ADAM_B1 * m_ref[...] + (1.0 - ADAM_B1) * grad
            vn = ADAM_B2 * v_ref[...] + (1.0 - ADAM_B2) * (grad * grad)
            go_ref[...] = grad
            d_ref[...] = -ADAM_LR * ((mn / bc1) / (jnp.sqrt(vn / bc2) + ADAM_EPS) + ADAM_WD * w_ref[...])
            mo_ref[...] = mn
            vo_ref[...] = vn

        tile = pl.BlockSpec((None, tr, cdim), lambda i, l=l: (l, i, 0))
        whole = pl.BlockSpec(memory_space=pl.ANY)
        outs = _call(body, name=f"{name}_{l}", grid=(r // tr,),
                     in_specs=[pl.BlockSpec((NDEV, tr, cdim), lambda i: (0, i, 0)), tile, tile, tile] + [whole] * 4,
                     out_specs=[tile] * 4, out_shape=[jax.ShapeDtypeStruct(w.shape, F32)] * 4,
                     input_output_aliases={4: 0, 5: 1, 6: 2, 7: 3},
                     compiler_params=_params("arbitrary"))(chunks[l], w, m, v, *outs)
    return outs


WEIGHTS = ("w_ada", "b_ada", "w_in", "b_f", "gm_ln_g", "gm_ln_b", "gm_w_s", "gm_b_s", "s5_lam_re", "s5_lam_im",
           "s5_log_dt", "s5_b_re", "s5_b_im", "s5_c_re", "s5_c_im", "s5_d", "s5_w_glu", "s5_b_glu", "w_out", "ln1_g",
           "ln1_b", "w_up", "conv_w", "conv_b", "w_down", "ln2_g", "ln2_b")
SHARDED = ("w_in", "w_out", "w_up", "w_down", "conv_w")
LARGE = ("w_in", "w_out", "w_up", "w_down")
COL_SHARDED = ("w_in", "w_up", "conv_w")
PACKED = ("b_ada",) + SMALL
PACK_ALIGN = NDEV * 8 * 128


def _gather_cols(g):
    nd, nl, r, c = g.shape
    return jnp.transpose(g, (1, 2, 0, 3)).reshape(nl, r, nd * c)


def _chunk_cols(g):
    nl, r, c8 = g.shape
    return jnp.transpose(g.reshape(nl, r, NDEV, c8 // NDEV), (2, 0, 1, 3))


def _join_cols(g):
    nd, r, c = g.shape
    return jnp.transpose(g, (1, 0, 2)).reshape(r, nd * c)


def _join_rows(g):
    nd, r, c = g.shape
    return g.reshape(nd * r, c)


def _split_cols(g):
    r, c8 = g.shape
    return jnp.transpose(g.reshape(r, NDEV, c8 // NDEV), (1, 0, 2))


def _split_rows(g):
    r8, c = g.shape
    return g.reshape(NDEV, r8 // NDEV, c)


def _pack(parts):
    flat = jnp.concatenate([parts[n].reshape(-1) for n in PACKED])
    return jnp.pad(flat, (0, -flat.shape[0] % PACK_ALIGN))


def _unpack(flat, shapes):
    out, off = {}, 0
    for n in PACKED:
        size = math.prod(shapes[n])
        out[n] = flat[off:off + size].reshape(shapes[n])
        off += size
    return out


def kernel(x, c, w_ada, b_ada, w_in, b_f, gm_ln_g, gm_ln_b, gm_w_s, gm_b_s, s5_lam_re, s5_lam_im, s5_log_dt, s5_b_re, s5_b_im, s5_c_re, s5_c_im, s5_d, s5_w_glu, s5_b_glu, w_out, ln1_g, ln1_b, w_up, conv_w, conv_b, w_down, ln2_g, ln2_b, loss_target, m_w_ada, m_b_ada, m_w_in, m_b_f, m_gm_ln_g, m_gm_ln_b, m_gm_w_s, m_gm_b_s, m_s5_lam_re, m_s5_lam_im, m_s5_log_dt, m_s5_b_re, m_s5_b_im, m_s5_c_re, m_s5_c_im, m_s5_d, m_s5_w_glu, m_s5_b_glu, m_w_out, m_ln1_g, m_ln1_b, m_w_up, m_conv_w, m_conv_b, m_w_down, m_ln2_g, m_ln2_b, v_w_ada, v_b_ada, v_w_in, v_b_f, v_gm_ln_g, v_gm_ln_b, v_gm_w_s, v_gm_b_s, v_s5_lam_re, v_s5_lam_im, v_s5_log_dt, v_s5_b_re, v_s5_b_im, v_s5_c_re, v_s5_c_im, v_s5_d, v_s5_w_glu, v_s5_b_glu, v_w_out, v_ln1_g, v_ln1_b, v_w_up, v_conv_w, v_conv_b, v_w_down, v_ln2_g, v_ln2_b):
    given = dict(locals())
    wts = {n: given[n] for n in WEIGHTS}
    mom = {n: given["m_" + n] for n in WEIGHTS}
    var = {n: given["v_" + n] for n in WEIGHTS}
    nl = w_ada.shape[0]
    me = _my_index()
    ada_cols = w_ada.shape[2]

    (c_all,) = exchange([c], False, "gather_c")
    c_all = c_all.reshape(NDEV, D)
    b_loc = lax.dynamic_slice_in_dim(b_ada, me * ada_cols, ada_cols, axis=1)
    mod_part = mod_slices(c_all, w_ada, b_loc)

    mod_all, conv_all = exchange([mod_part, conv_w], False, "gather_mod")
    mod_mine = lax.dynamic_index_in_dim(mod_all, me, axis=2, keepdims=False)
    mods = jnp.transpose(mod_mine, (1, 0, 2)).reshape(nl, 6, D)
    mods = jnp.pad(mods, ((0, 0), (0, 2), (0, 0)))
    conv_full = _gather_cols(conv_all)
    sp = {n: wts[n] for n in SMALL}
    mine = (jnp.arange(NDEV) == me)[:, None, None]

    blocks = [[wts[n][l].astype(BF16) for n in LARGE] for l in range(nl)]
    started = exchange_start(blocks[0], False, "gather_start_0")
    v_thru, land = exchange_wait(started, started[4], False, "gather_wait_0")
    xl, saved, weights = x[0], [], []
    for l in range(nl):
        land = [jnp.where(mine, own[None], g) for own, g in zip(v_thru, land)]
        mod_l = mods[l]
        if l + 1 < nl:
            nxt, land = lax.optimization_barrier((blocks[l + 1], land))
            started = exchange_start(nxt, False, f"gather_start_{l + 1}")
            mod_l = mod_l + started[4][0, 0]
        w = {n: (_join_cols if n in COL_SHARDED else _join_rows)(g) for n, g in zip(LARGE, land)}
        w["w_in"] = jnp.pad(w["w_in"], ((0, 0), (0, NP - D_IN)))
        w["conv_w"] = conv_full[l]
        weights.append(w)
        xl, sv = layer_fwd(xl, mod_l, w, _layer_operands(sp, l))
        saved.append(sv)
        if l + 1 < nl:
            v_thru, land = exchange_wait(started, xl, False, f"gather_wait_{l + 1}")

    loss_tile, dx = loss_kernel(xl, loss_target[0])

    scattering, vals, dmods, gconv = [None] * nl, [None] * nl, [None] * nl, [None] * nl
    token = jnp.zeros((), F32)
    for l in reversed(range(nl)):
        dx, gbig, vals[l], dmods[l] = layer_bwd(dx, saved[l], mods[l] + token, weights[l])
        gconv[l] = gbig["conv_w"]
        gbig["w_in"] = gbig["w_in"][:, :D_IN]
        chunks = [(_split_cols if n in COL_SHARDED else _split_rows)(gbig[n]).astype(BF16) for n in LARGE]
        scattering[l] = exchange_start(chunks, True, f"scatter_start_{l}")
        token = scattering[l][4][0, 0]
    gx = dx
    dmods = jnp.stack(dmods)
    gsm = {n: jnp.stack([v[n] for v in vals]) for n in SMALL}

    gsm["b_ada"] = dmods.reshape(nl, 6 * D)
    packed = _pack(gsm).reshape(NDEV, -1, 128)
    conv_recv, small_recv = exchange([_chunk_cols(jnp.stack(gconv)), packed], True, "scatter_small")
    small_sum = sum_chunks(small_recv)
    small_all, dmod_all = exchange([small_sum, dmods.reshape(nl, 6 * D)], False, "gather_small")

    received, after = [None] * nl, small_all
    for l in reversed(range(nl)):
        v_thru, land = exchange_wait(scattering[l], after, True, f"scatter_wait_{l}")
        received[l] = {n: jnp.where(mine, own, g) for n, own, g in zip(LARGE, v_thru, land)}
        after = land[0]

    out = {}
    for n in LARGE:
        out[n] = adamw_layers(wts[n], mom[n], var[n], [received[l][n] for l in range(nl)], "adamw_" + n)
    shp = conv_w.shape
    two_d = lambda a: a.reshape(shp[0] * shp[1], shp[2])
    res = adamw(two_d(conv_w), two_d(m_conv_w), two_d(v_conv_w),
                chunks=conv_recv.reshape(NDEV, shp[0] * shp[1], shp[2]), name="adamw_conv_w")
    out["conv_w"] = [r.reshape(shp) for r in res]

    dm_loc = lax.dynamic_slice_in_dim(dmod_all, me * ada_cols, ada_cols, axis=2)
    g_ada = ada_grad(c_all, jnp.transpose(dm_loc, (1, 0, 2)))
    two_d = lambda a: a.reshape(nl * D, ada_cols)
    res = adamw(two_d(w_ada), two_d(m_w_ada), two_d(v_w_ada), g=two_d(g_ada), name="adamw_w_ada")
    out["w_ada"] = [r.reshape(w_ada.shape) for r in res]

    shapes = {n: wts[n].shape for n in PACKED}
    as_rows = lambda parts: _pack(parts).reshape(-1, 128)
    res = adamw(as_rows(wts), as_rows(mom), as_rows(var), g=small_all.reshape(-1, 128), name="adamw_small")
    unpacked = [_unpack(r.reshape(-1), shapes) for r in res]
    for n in PACKED:
        out[n] = [u[n] for u in unpacked]

    loss = lax.psum(loss_tile[0, 0], ("x", "y", "c"))
    return (loss, gx[None], *[out[n][0] for n in WEIGHTS], *[out[n][1] for n in WEIGHTS],
            *[out[n][2] for n in WEIGHTS], *[out[n][3] for n in WEIGHTS])
```

```python
import functools
import math

import jax
import jax.numpy as jnp
from jax import lax
from jax.experimental import pallas as pl
from jax.experimental.pallas import tpu as pltpu

F32 = jnp.float32
BF16 = jnp.bfloat16
MESH_IDS = pl.DeviceIdType.MESH

D = 1024
SEQ = 4096
DEPTH = 4
NDEV = 8
HD = 64
GM_W = 256
GM_H = 4
GM_C = 128
S5_W = 256
S5_G = 16
S5_H = 16
S5_P = 64
S5_N = S5_G * S5_P
FX_W = 512
FX_H = 8
D_IN = 2 * GM_W + S5_W + 3 * FX_W + FX_H
NP = 2432
FF_COL = 2304
DFF = 2816
LN_EPS = 1e-5
DN_ALPHA = (2.0 * DEPTH) ** 0.25
NEG_INF = -1e30
ADAM_LR = 0.001
ADAM_B1 = 0.9
ADAM_B2 = 0.999
ADAM_EPS = 1e-08
ADAM_WD = 0.01
ADAM_STEP = 10

V7X_VMEM_LIMIT = 56 * 1024 * 1024
TS = 512
TS_C = 256
T_S5 = 256
TQ = 512
ATTN_HEADS = 2


def _call(body, **kw):
    return pl.pallas_call(body, **kw)


def _params(*sem):
    return pltpu.CompilerParams(dimension_semantics=sem if sem else None,
                                vmem_limit_bytes=V7X_VMEM_LIMIT)


def _nn(a, b):
    return jnp.dot(a.astype(BF16), b.astype(BF16), preferred_element_type=F32)


def _nt(a, b):
    return lax.dot_general(a.astype(BF16), b.astype(BF16), (((1,), (1,)), ((), ())),
                           preferred_element_type=F32)


def _tn(a, b):
    return lax.dot_general(a.astype(BF16), b.astype(BF16), (((0,), (0,)), ((), ())),
                           preferred_element_type=F32)


@jax.custom_vjp
def _bdot(a, b):
    return _nn(a, b)


def _bdot_fwd(a, b):
    return _nn(a, b), (a, b)


def _bdot_bwd(res, g):
    a, b = res
    return _nt(g, b), _tn(a, g)


_bdot.defvjp(_bdot_fwd, _bdot_bwd)


@jax.custom_vjp
def _bdot_nt(a, b):
    return _nt(a, b)


def _bdot_nt_fwd(a, b):
    return _nt(a, b), (a, b)


def _bdot_nt_bwd(res, g):
    a, b = res
    return _nn(g, b), _tn(g, a)


_bdot_nt.defvjp(_bdot_nt_fwd, _bdot_nt_bwd)


def _ln(r, g, b):
    mu = jnp.mean(r, axis=-1, keepdims=True)
    xc = r - mu
    var = jnp.mean(xc * xc, axis=-1, keepdims=True)
    return xc * lax.rsqrt(var + LN_EPS) * g + b


def _rows(shape):
    return lax.broadcasted_iota(jnp.int32, shape, 0)


def _lanes(shape):
    return lax.broadcasted_iota(jnp.int32, shape, 1)


def mm_nn(a, w, tn, name, mod=None, rows=None, out_dtype=F32):
    s, k = a.shape
    n = w.shape[1]
    ts = min(TS, s)

    def body(*refs):
        if mod is None:
            a_ref, w_ref, o_ref = refs
            h = a_ref[...]
        else:
            a_ref, m_ref, w_ref, o_ref = refs
            h = a_ref[...] * (1.0 + m_ref[rows[1]:rows[1] + 1, :]) + m_ref[rows[0]:rows[0] + 1, :]
        o_ref[...] = jnp.dot(h.astype(BF16), w_ref[...], preferred_element_type=F32).astype(out_dtype)

    in_specs = [pl.BlockSpec((ts, k), lambda j, i: (i, 0))]
    args = [a]
    if mod is not None:
        in_specs.append(pl.BlockSpec((8, k), lambda j, i: (0, 0)))
        args.append(mod)
    in_specs.append(pl.BlockSpec((k, tn), lambda j, i: (0, j)))
    args.append(w)
    return _call(body, name=name, grid=(n // tn, s // ts), in_specs=in_specs,
                 out_specs=pl.BlockSpec((ts, tn), lambda j, i: (i, j)),
                 out_shape=jax.ShapeDtypeStruct((s, n), out_dtype),
                 compiler_params=_params("arbitrary", "arbitrary"))(*args)


def mm_nt(dy, w, name):
    s, n = dy.shape
    k = w.shape[0]
    ts = min(TS, s)

    def body(dy_ref, w_ref, o_ref):
        o_ref[...] = _nt(dy_ref[...], w_ref[...])

    return _call(body, name=name, grid=(s // ts,),
                 in_specs=[pl.BlockSpec((ts, n), lambda i: (i, 0)),
                           pl.BlockSpec((k, n), lambda i: (0, 0))],
                 out_specs=pl.BlockSpec((ts, k), lambda i: (i, 0)),
                 out_shape=jax.ShapeDtypeStruct((s, k), F32),
                 compiler_params=_params("arbitrary"))(dy, w)


def mm_nt_mod(dy, w, x, dres, mod, rows, name):
    s, n = dy.shape
    k = w.shape[0]
    ts = min(TS, s)

    def body(dy_ref, w_ref, x_ref, r_ref, m_ref, dx_ref, dsh_ref, dsc_ref):
        @pl.when(pl.program_id(0) == 0)
        def _():
            dsh_ref[...] = jnp.zeros_like(dsh_ref)
            dsc_ref[...] = jnp.zeros_like(dsc_ref)

        dh = _nt(dy_ref[...], w_ref[...])
        dx_ref[...] = r_ref[...] + dh * (1.0 + m_ref[rows[1]:rows[1] + 1, :])
        dsh_ref[...] += jnp.sum(dh, axis=0, keepdims=True)
        dsc_ref[...] += jnp.sum(dh * x_ref[...], axis=0, keepdims=True)

    row = pl.BlockSpec((1, k), lambda i: (0, 0))
    tile = pl.BlockSpec((ts, k), lambda i: (i, 0))
    return _call(body, name=name, grid=(s // ts,),
                 in_specs=[pl.BlockSpec((ts, n), lambda i: (i, 0)),
                           pl.BlockSpec((k, n), lambda i: (0, 0)), tile, tile,
                           pl.BlockSpec((8, k), lambda i: (0, 0))],
                 out_specs=[tile, row, row],
                 out_shape=[jax.ShapeDtypeStruct((s, k), F32),
                            jax.ShapeDtypeStruct((1, k), F32),
                            jax.ShapeDtypeStruct((1, k), F32)],
                 compiler_params=_params("arbitrary"))(dy, w, x, dres, mod)


def mm_tn(a, dy, tn, name, mod=None, rows=None):
    s, k = a.shape
    n = dy.shape[1]
    ts = min(TS, s)
    ns = s // ts

    def body(*refs):
        if mod is None:
            a_ref, dy_ref, o_ref, acc = refs
            h = a_ref[...]
        else:
            a_ref, m_ref, dy_ref, o_ref, acc = refs
            h = a_ref[...] * (1.0 + m_ref[rows[1]:rows[1] + 1, :]) + m_ref[rows[0]:rows[0] + 1, :]
        i = pl.program_id(1)

        @pl.when(i == 0)
        def _():
            acc[...] = jnp.zeros_like(acc)

        acc[...] += _tn(h, dy_ref[...])

        @pl.when(i == ns - 1)
        def _():
            o_ref[...] = acc[...].astype(BF16)

    in_specs = [pl.BlockSpec((ts, k), lambda j, i: (i, 0))]
    args = [a]
    if mod is not None:
        in_specs.append(pl.BlockSpec((8, k), lambda j, i: (0, 0)))
        args.append(mod)
    in_specs.append(pl.BlockSpec((ts, tn), lambda j, i: (i, j)))
    args.append(dy)
    return _call(body, name=name, grid=(n // tn, s // ts), in_specs=in_specs,
                 out_specs=pl.BlockSpec((k, tn), lambda j, i: (0, j)),
                 out_shape=jax.ShapeDtypeStruct((k, n), BF16),
                 scratch_shapes=[pltpu.VMEM((k, tn), F32)],
                 compiler_params=_params("arbitrary", "arbitrary"))(*args)


def _post_fn(x, br, gate, lg, lb):
    return _ln(DN_ALPHA * x + (1.0 + gate) * br, lg, lb)


def post_fwd(x, br, mod, grow, lg, lb, name):
    s = x.shape[0]
    ts = min(TS, s)

    def body(x_ref, b_ref, m_ref, lg_ref, lb_ref, o_ref):
        o_ref[...] = _post_fn(x_ref[...], b_ref[...], m_ref[grow:grow + 1, :], lg_ref[...], lb_ref[...])

    tile = pl.BlockSpec((ts, D), lambda i: (i, 0))
    row = pl.BlockSpec((1, D), lambda i: (0, 0))
    return _call(body, name=name, grid=(s // ts,),
                 in_specs=[tile, tile, pl.BlockSpec((8, D), lambda i: (0, 0)), row, row],
                 out_specs=tile, out_shape=jax.ShapeDtypeStruct((s, D), F32),
                 compiler_params=_params("arbitrary"))(x, br, mod, lg, lb)


def post_bwd(x, br, mod, grow, lg, lb, dy, name):
    s = x.shape[0]
    ts = min(TS, s)

    def body(x_ref, b_ref, m_ref, lg_ref, lb_ref, dy_ref, dx_ref, db_ref, dg_ref, dlg_ref, dlb_ref):
        @pl.when(pl.program_id(0) == 0)
        def _():
            dg_ref[...] = jnp.zeros_like(dg_ref)
            dlg_ref[...] = jnp.zeros_like(dlg_ref)
            dlb_ref[...] = jnp.zeros_like(dlb_ref)

        _, vjp = jax.vjp(_post_fn, x_ref[...], b_ref[...], m_ref[grow:grow + 1, :], lg_ref[...], lb_ref[...])
        dx, db, dg, dlg, dlb = vjp(dy_ref[...])
        dx_ref[...] = dx
        db_ref[...] = db.astype(BF16)
        dg_ref[...] += dg
        dlg_ref[...] += dlg
        dlb_ref[...] += dlb

    tile = pl.BlockSpec((ts, D), lambda i: (i, 0))
    row = pl.BlockSpec((1, D), lambda i: (0, 0))
    rs = jax.ShapeDtypeStruct((1, D), F32)
    return _call(body, name=name, grid=(s // ts,),
                 in_specs=[tile, tile, pl.BlockSpec((8, D), lambda i: (0, 0)), row, row, tile],
                 out_specs=[tile, tile, row, row, row],
                 out_shape=[jax.ShapeDtypeStruct((s, D), F32), jax.ShapeDtypeStruct((s, D), BF16), rs, rs, rs],
                 compiler_params=_params("arbitrary"))(x, br, mod, lg, lb, dy)


def loss_kernel(y, target):
    s = y.shape[0]
    ts = min(TS, s)

    def body(y_ref, t_ref, l_ref, dy_ref):
        @pl.when(pl.program_id(0) == 0)
        def _():
            l_ref[...] = jnp.zeros_like(l_ref)

        err = y_ref[...] - t_ref[...]
        dy_ref[...] = err * (1.0 / D)
        per_tok = jnp.mean(err * err, axis=-1, keepdims=True)
        l_ref[...] += 0.5 * jnp.sum(per_tok)

    tile = pl.BlockSpec((ts, D), lambda i: (i, 0))
    return _call(body, name="loss", grid=(s // ts,), in_specs=[tile, tile],
                 out_specs=[pl.BlockSpec((8, 128), lambda i: (0, 0)), tile],
                 out_shape=[jax.ShapeDtypeStruct((8, 128), F32), jax.ShapeDtypeStruct((s, D), F32)],
                 compiler_params=_params("arbitrary"))(y, target)


def _gm_head(u, v, lg, lb, w, bs):
    t = u.shape[0]
    causal = _rows((GM_C, GM_C)) >= _lanes((GM_C, GM_C))
    vn = _ln(v, lg, lb)
    wm = jnp.where(causal, w, 0.0)
    chunks = []
    for n in range(t // GM_C):
        rs = slice(n * GM_C, (n + 1) * GM_C)
        chunks.append(u[rs] * (_bdot(wm, vn[rs]) + bs))
    return jnp.concatenate(chunks, axis=0)


def gm_fwd(p, lg, lb, ws, bst):
    s = p.shape[0]
    ts = min(TS_C, s)

    def body(u_ref, v_ref, lg_ref, lb_ref, ws_ref, bs_ref, o_ref):
        for h in range(GM_H):
            sl = slice(h * HD, (h + 1) * HD)
            o_ref[:, sl] = _gm_head(u_ref[:, sl], v_ref[:, sl], lg_ref[:, sl], lb_ref[:, sl], ws_ref[h],
                                    bs_ref[:, h:h + 1]).astype(BF16)

    full = lambda shape: pl.BlockSpec(shape, lambda i: (0,) * len(shape))
    return _call(body, name="gm_fwd", grid=(s // ts,),
                 in_specs=[pl.BlockSpec((ts, GM_W), lambda i: (i, 0)), pl.BlockSpec((ts, GM_W), lambda i: (i, 1)),
                           full((1, GM_W)), full((1, GM_W)), full((GM_H, GM_C, GM_C)), full((GM_C, GM_H))],
                 out_specs=pl.BlockSpec((ts, GM_W), lambda i: (i, 0)),
                 out_shape=jax.ShapeDtypeStruct((s, GM_W), BF16),
                 compiler_params=_params("arbitrary"))(p, p, lg, lb, ws, bst)


def gm_bwd(p, lg, lb, ws, bst, dmix):
    s = p.shape[0]
    ts = min(TS_C, s)

    def body(u_ref, v_ref, lg_ref, lb_ref, ws_ref, bs_ref, dy_ref, duv_ref, dlg_ref, dlb_ref, dws_ref, dbs_ref):
        @pl.when(pl.program_id(0) == 0)
        def _():
            dlg_ref[...] = jnp.zeros_like(dlg_ref)
            dlb_ref[...] = jnp.zeros_like(dlb_ref)
            dws_ref[...] = jnp.zeros_like(dws_ref)
            dbs_ref[...] = jnp.zeros_like(dbs_ref)

        for h in range(GM_H):
            sl = slice(h * HD, (h + 1) * HD)
            _, vjp = jax.vjp(_gm_head, u_ref[:, sl], v_ref[:, sl], lg_ref[:, sl], lb_ref[:, sl], ws_ref[h],
                             bs_ref[:, h:h + 1])
            du, dv, dlg, dlb, dw, dbs = vjp(dy_ref[:, sl])
            duv_ref[:, sl] = du.astype(BF16)
            duv_ref[:, GM_W + h * HD:GM_W + (h + 1) * HD] = dv.astype(BF16)
            dlg_ref[:, sl] += dlg
            dlb_ref[:, sl] += dlb
            dws_ref[h] += dw
            dbs_ref[:, h:h + 1] += dbs

    full = lambda shape: pl.BlockSpec(shape, lambda i: (0,) * len(shape))
    return _call(body, name="gm_bwd", grid=(s // ts,),
                 in_specs=[pl.BlockSpec((ts, GM_W), lambda i: (i, 0)), pl.BlockSpec((ts, GM_W), lambda i: (i, 1)),
                           full((1, GM_W)), full((1, GM_W)), full((GM_H, GM_C, GM_C)), full((GM_C, GM_H)),
                           pl.BlockSpec((ts, GM_W), lambda i: (i, 0))],
                 out_specs=[pl.BlockSpec((ts, 2 * GM_W), lambda i: (i, 0)), full((1, GM_W)), full((1, GM_W)),
                            full((GM_H, GM_C, GM_C)), full((GM_C, GM_H))],
                 out_shape=[jax.ShapeDtypeStruct((s, 2 * GM_W), BF16), jax.ShapeDtypeStruct((1, GM_W), F32),
                            jax.ShapeDtypeStruct((1, GM_W), F32), jax.ShapeDtypeStruct((GM_H, GM_C, GM_C), F32),
                            jax.ShapeDtypeStruct((GM_C, GM_H), F32)],
                 compiler_params=_params("arbitrary"))(p, p, lg, lb, ws, bst, dmix)


def _s5_prep_fn(lr, li, ldt, bt):
    dt = jnp.exp(ldt)
    er = jnp.exp(lr * dt)
    ar = er * jnp.cos(li * dt)
    ai = er * jnp.sin(li * dt)
    den = lr * lr + li * li
    nr = ar - 1.0
    cr = (nr * lr + ai * li) / den
    ci = (ai * lr - nr * li) / den
    br, bi = bt[:, :S5_N], bt[:, S5_N:]
    return ar, ai, jnp.concatenate([cr * br - ci * bi, cr * bi + ci * br], axis=1)


def s5_prep_fwd(lr, li, ldt, bt):
    def body(lr_ref, li_ref, ldt_ref, bt_ref, a_ref, bb_ref):
        ar, ai, bb = _s5_prep_fn(lr_ref[...], li_ref[...], ldt_ref[...], bt_ref[...])
        a_ref[...] = jnp.concatenate([ar, ai, jnp.zeros((6, S5_N), F32)], axis=0)
        bb_ref[...] = bb

    return _call(body, name="s5_prep_fwd",
                 out_shape=[jax.ShapeDtypeStruct((8, S5_N), F32), jax.ShapeDtypeStruct((S5_H, 2 * S5_N), F32)])(lr, li, ldt, bt)


def s5_prep_bwd(lr, li, ldt, bt, da, dbb):
    def body(lr_ref, li_ref, ldt_ref, bt_ref, da_ref, dbb_ref, dlr_ref, dli_ref, dldt_ref, dbt_ref):
        _, vjp = jax.vjp(_s5_prep_fn, lr_ref[...], li_ref[...], ldt_ref[...], bt_ref[...])
        dlr, dli, dldt, dbt = vjp((da_ref[0:1, :], da_ref[1:2, :], dbb_ref[...]))
        dlr_ref[...] = dlr
        dli_ref[...] = dli
        dbt_ref[...] = dbt
        group = (_rows((S5_N, 128)) // S5_P == _lanes((S5_N, 128))).astype(F32)
        dldt_ref[...] = jnp.dot(jnp.broadcast_to(dldt, (8, S5_N)), group, precision=lax.Precision.HIGHEST,
                                preferred_element_type=F32)[0:1, :]

    r = jax.ShapeDtypeStruct((1, S5_N), F32)
    return _call(body, name="s5_prep_bwd",
                 out_shape=[r, r, jax.ShapeDtypeStruct((1, 128), F32),
                            jax.ShapeDtypeStruct((S5_H, 2 * S5_N), F32)])(lr, li, ldt, bt, da, dbb)


def _s5_out_fn(x, u, cbd, drow, wg, bg):
    y = _bdot_nt(x[:, :S5_N], cbd[:, :S5_N]) - _bdot_nt(x[:, S5_N:], cbd[:, S5_N:]) + drow * u
    y = jax.nn.gelu(y)
    gate = _bdot_nt(y, wg) + bg
    return y * jax.nn.sigmoid(gate)


def _scan_steps(t):
    return int(math.log2(t))


def s5_fwd(p, arow, bbd, cbd, drow, wg, bg):
    s = p.shape[0]
    t = min(T_S5, s)

    def body(u_ref, a_ref, bbd_ref, cbd_ref, d_ref, wg_ref, bg_ref, y_ref, st_ref, carry):
        @pl.when(pl.program_id(0) == 0)
        def _():
            carry[...] = jnp.zeros_like(carry)

        u = u_ref[...]
        bu = _nn(u, bbd_ref[...])
        ar, ai = a_ref[0:1, :], a_ref[1:2, :]
        cr, ci = carry[0:1, :S5_N], carry[0:1, S5_N:]
        rid = _rows((t, S5_N))
        first = rid == 0
        xr = bu[:, :S5_N] + jnp.where(first, ar * cr - ai * ci, 0.0)
        xi = bu[:, S5_N:] + jnp.where(first, ar * ci + ai * cr, 0.0)
        pr, pi = ar, ai
        for k in range(_scan_steps(t)):
            d = 1 << k
            keep = rid >= d
            sr = jnp.where(keep, pltpu.roll(xr, d, 0), 0.0)
            si = jnp.where(keep, pltpu.roll(xi, d, 0), 0.0)
            xr, xi = xr + pr * sr - pi * si, xi + pr * si + pi * sr
            pr, pi = pr * pr - pi * pi, 2.0 * pr * pi
        st_ref[:, :S5_N] = xr
        st_ref[:, S5_N:] = xi
        carry[0:1, :S5_N] = xr[t - 1:t, :]
        carry[0:1, S5_N:] = xi[t - 1:t, :]
        x = jnp.concatenate([xr, xi], axis=1)
        y_ref[...] = _s5_out_fn(x, u, cbd_ref[...], d_ref[...], wg_ref[...], bg_ref[...]).astype(BF16)

    full = lambda shape: pl.BlockSpec(shape, lambda i: (0,) * len(shape))
    return _call(body, name="s5_fwd", grid=(s // t,),
                 in_specs=[pl.BlockSpec((t, S5_W), lambda i: (i, 2)), full((8, S5_N)), full((S5_W, 2 * S5_N)),
                           full((S5_W, 2 * S5_N)), full((1, S5_W)), full((S5_W, S5_W)), full((1, S5_W))],
                 out_specs=[pl.BlockSpec((t, S5_W), lambda i: (i, 0)), pl.BlockSpec((t, 2 * S5_N), lambda i: (i, 0))],
                 out_shape=[jax.ShapeDtypeStruct((s, S5_W), BF16), jax.ShapeDtypeStruct((s, 2 * S5_N), F32)],
                 scratch_shapes=[pltpu.VMEM((8, 2 * S5_N), F32)],
                 compiler_params=_params("arbitrary"))(p, arow, bbd, cbd, drow, wg, bg)


def s5_bwd(p, st, arow, bbd, cbd, drow, wg, bg, dmix):
    s = p.shape[0]
    t = min(T_S5, s)
    nc = s // t

    def body(u_ref, st_ref, prev_ref, a_ref, bbd_ref, cbd_ref, d_ref, wg_ref, bg_ref, dy_ref,
             du_ref, da_ref, dbbd_ref, dcbd_ref, dd_ref, dwg_ref, dbg_ref, carry):
        i = pl.program_id(0)

        @pl.when(i == 0)
        def _():
            carry[...] = jnp.zeros_like(carry)
            for r in (da_ref, dbbd_ref, dcbd_ref, dd_ref, dwg_ref, dbg_ref):
                r[...] = jnp.zeros_like(r)

        u = u_ref[...]
        x = st_ref[...]
        _, vjp = jax.vjp(_s5_out_fn, x, u, cbd_ref[...], d_ref[...], wg_ref[...], bg_ref[...])
        dx, du1, dcbd, dd, dwg, dbg = vjp(dy_ref[...])
        ar, ai = a_ref[0:1, :], a_ref[1:2, :]
        cr, ci = carry[0:1, :S5_N], carry[0:1, S5_N:]
        rid = _rows((t, S5_N))
        last = rid == t - 1
        gr = dx[:, :S5_N] + jnp.where(last, ar * cr + ai * ci, 0.0)
        gi = dx[:, S5_N:] + jnp.where(last, ar * ci - ai * cr, 0.0)
        pr, pi = ar, -ai
        for k in range(_scan_steps(t)):
            d = 1 << k
            keep = rid < t - d
            sr = jnp.where(keep, pltpu.roll(gr, t - d, 0), 0.0)
            si = jnp.where(keep, pltpu.roll(gi, t - d, 0), 0.0)
            gr, gi = gr + pr * sr - pi * si, gi + pr * si + pi * sr
            pr, pi = pr * pr - pi * pi, 2.0 * pr * pi
        carry[0:1, :S5_N] = gr[0:1, :]
        carry[0:1, S5_N:] = gi[0:1, :]
        has_prev = (i < nc - 1).astype(F32)
        top_r = prev_ref[7:8, :S5_N] * has_prev
        top_i = prev_ref[7:8, S5_N:] * has_prev
        xpr = jnp.where(rid == 0, top_r, pltpu.roll(x[:, :S5_N], 1, 0))
        xpi = jnp.where(rid == 0, top_i, pltpu.roll(x[:, S5_N:], 1, 0))
        da_ref[0:1, :] += jnp.sum(xpr * gr + xpi * gi, axis=0, keepdims=True)
        da_ref[1:2, :] += jnp.sum(xpr * gi - xpi * gr, axis=0, keepdims=True)
        g = jnp.concatenate([gr, gi], axis=1)
        dbbd_ref[...] += _tn(u, g)
        du_ref[...] = (_nt(g, bbd_ref[...]) + du1).astype(BF16)
        dcbd_ref[...] += dcbd
        dd_ref[...] += dd
        dwg_ref[...] += dwg
        dbg_ref[...] += dbg

    full = lambda shape: pl.BlockSpec(shape, lambda i: (0,) * len(shape))
    rev = lambda col: (lambda i: (nc - 1 - i, col))
    prev_map = lambda i: (jnp.maximum((nc - 1 - i) * (t // 8) - 1, 0), 0)
    return _call(body, name="s5_bwd", grid=(nc,),
                 in_specs=[pl.BlockSpec((t, S5_W), rev(2)), pl.BlockSpec((t, 2 * S5_N), rev(0)),
                           pl.BlockSpec((8, 2 * S5_N), prev_map), full((8, S5_N)), full((S5_W, 2 * S5_N)),
                           full((S5_W, 2 * S5_N)), full((1, S5_W)), full((S5_W, S5_W)), full((1, S5_W)),
                           pl.BlockSpec((t, S5_W), rev(1))],
                 out_specs=[pl.BlockSpec((t, S5_W), rev(0)), full((8, S5_N)), full((S5_W, 2 * S5_N)),
                            full((S5_W, 2 * S5_N)), full((1, S5_W)), full((S5_W, S5_W)), full((1, S5_W))],
                 out_shape=[jax.ShapeDtypeStruct((s, S5_W), BF16), jax.ShapeDtypeStruct((8, S5_N), F32),
                            jax.ShapeDtypeStruct((S5_W, 2 * S5_N), F32), jax.ShapeDtypeStruct((S5_W, 2 * S5_N), F32),
                            jax.ShapeDtypeStruct((1, S5_W), F32), jax.ShapeDtypeStruct((S5_W, S5_W), F32),
                            jax.ShapeDtypeStruct((1, S5_W), F32)],
                 scratch_shapes=[pltpu.VMEM((8, 2 * S5_N), F32)],
                 compiler_params=_params("arbitrary"))(p, st, st, arow, bbd, cbd, drow, wg, bg, dmix)


def _cum_steps(s):
    return int(math.ceil(math.log2(s)))


V_BLK = (2 * GM_W + S5_W + 2 * FX_W) // 128


AUG = 2 * HD
BIAS_COL = HD
FQ_COL = HD + 3
PAIR_W = 256


def _split3(f):
    hi = f.astype(BF16).astype(F32)
    r = f - hi
    mid = r.astype(BF16).astype(F32)
    lo = (r - mid).astype(BF16).astype(F32)
    return hi, mid, lo


def fox_prep(p, bf):
    s = p.shape[0]
    ts = min(TS, s)
    scale = HD ** -0.5

    def body(q0_ref, q1_ref, k0_ref, k1_ref, v0_ref, v1_ref, f_ref, bf_ref,
             qa_ref, ka_ref, qat_ref, kat_ref, vt_ref, carry):
        @pl.when(pl.program_id(0) == 0)
        def _():
            carry[...] = jnp.zeros_like(carry)

        lane = _lanes((ts, 128))
        lf = jax.nn.log_sigmoid(f_ref[...] + bf_ref[...])
        acc = jnp.where(lane < FX_H, lf, 0.0)
        rid = _rows((ts, 128))
        for k in range(_cum_steps(ts)):
            d = 1 << k
            acc = acc + jnp.where(rid >= d, pltpu.roll(acc, d, 0), 0.0)
        acc = acc + carry[0:1, :]
        carry[0:1, :] = acc[ts - 1:ts, :]

        low = lane < HD
        for h in range(FX_H):
            blk, pos = divmod(h, 4)
            pair = slice((pos // 2) * 128, (pos // 2) * 128 + 128)
            hi, mid, lo = _split3(acc[:, h:h + 1])
            one = jnp.ones((ts, 1), F32)

            def augment(ref, cols):
                x = ref[:, pair]
                if pos % 2:
                    x = pltpu.roll(x, HD, 1)
                out = jnp.where(low, x, 0.0)
                for j, cval in enumerate(cols):
                    out = jnp.where(lane == HD + j, cval, out)
                return out

            qa = augment((q0_ref, q1_ref)[blk], (one, one, one, hi, mid, lo))
            qa = jnp.where(low, qa * scale, qa)
            ka = augment((k0_ref, k1_ref)[blk], (-hi, -mid, -lo, one, one, one))
            cs = slice(h * AUG, (h + 1) * AUG)
            qa_ref[:, cs] = qa.astype(BF16)
            ka_ref[:, cs] = ka.astype(BF16)
            qat_ref[cs, :] = jnp.transpose(qa).astype(BF16)
            kat_ref[cs, :] = jnp.transpose(ka).astype(BF16)
        for j in range(FX_H // 2):
            vref = (v0_ref, v1_ref)[j // 2]
            vt_ref[j * 128:(j + 1) * 128, :] = jnp.transpose(vref[:, (j % 2) * 128:(j % 2) * 128 + 128]).astype(BF16)

    q_blk = (2 * GM_W + S5_W) // PAIR_W
    col = lambda b: pl.BlockSpec((ts, PAIR_W), lambda i: (i, b))
    wide = FX_H * AUG
    return _call(body, name="fox_prep", grid=(s // ts,),
                 in_specs=[col(q_blk), col(q_blk + 1), col(q_blk + 2), col(q_blk + 3), col(q_blk + 4), col(q_blk + 5),
                           pl.BlockSpec((ts, 128), lambda i: (i, FF_COL // 128)), pl.BlockSpec((1, 128), lambda i: (0, 0))],
                 out_specs=[pl.BlockSpec((ts, wide), lambda i: (i, 0)), pl.BlockSpec((ts, wide), lambda i: (i, 0)),
                            pl.BlockSpec((wide, ts), lambda i: (0, i)), pl.BlockSpec((wide, ts), lambda i: (0, i)),
                            pl.BlockSpec((FX_W, ts), lambda i: (0, i))],
                 out_shape=[jax.ShapeDtypeStruct((s, wide), BF16), jax.ShapeDtypeStruct((s, wide), BF16),
                            jax.ShapeDtypeStruct((wide, s), BF16), jax.ShapeDtypeStruct((wide, s), BF16),
                            jax.ShapeDtypeStruct((FX_W, s), BF16)],
                 scratch_shapes=[pltpu.VMEM((8, 128), F32)],
                 compiler_params=_params("arbitrary"))(p, p, p, p, p, p, p, bf)


def fox_prep_grad(p, bf, dfq, dfk):
    s = p.shape[0]
    ts = min(TS, s)
    ns = s // ts

    def body(f_ref, bf_ref, dfq_ref, dfk_ref, df_ref, dbf_ref, carry):
        @pl.when(pl.program_id(0) == 0)
        def _():
            carry[...] = jnp.zeros_like(carry)
            dbf_ref[...] = jnp.zeros_like(dbf_ref)

        lane = _lanes((ts, 128))
        acc = jnp.zeros((ts, 128), F32)
        for h in range(FX_H):
            c = (h // 2) * 128 + h % 2
            acc = jnp.where(lane == h, dfq_ref[:, c:c + 1] + dfk_ref[:, c:c + 1], acc)
        rid = _rows((ts, 128))
        for k in range(_cum_steps(ts)):
            d = 1 << k
            acc = acc + jnp.where(rid < ts - d, pltpu.roll(acc, ts - d, 0), 0.0)
        acc = acc + carry[0:1, :]
        carry[0:1, :] = acc[0:1, :]
        z = f_ref[...] + bf_ref[...]
        df = jnp.where(lane < FX_H, acc * jax.nn.sigmoid(-z), 0.0)
        df_ref[...] = df.astype(BF16)
        dbf_ref[...] += jnp.sum(df, axis=0, keepdims=True)

    rev = lambda i: (ns - 1 - i, 0)
    return _call(body, name="fox_prep_grad", grid=(ns,),
                 in_specs=[pl.BlockSpec((ts, 128), lambda i: (ns - 1 - i, FF_COL // 128)),
                           pl.BlockSpec((1, 128), lambda i: (0, 0)),
                           pl.BlockSpec((ts, FX_W), rev), pl.BlockSpec((ts, FX_W), rev)],
                 out_specs=[pl.BlockSpec((ts, 128), rev), pl.BlockSpec((1, 128), lambda i: (0, 0))],
                 out_shape=[jax.ShapeDtypeStruct((s, 128), BF16), jax.ShapeDtypeStruct((1, 128), F32)],
                 scratch_shapes=[pltpu.VMEM((8, 128), F32)],
                 compiler_params=_params("arbitrary"))(p, bf, dfq, dfk)


def attn(qat, ka, vt):
    s = ka.shape[0]
    tq = min(TQ, s)
    nq = s // tq

    nh = ATTN_HEADS

    def body(qat_ref, ka_ref, vt_ref, o_ref, lse_ref):
        qi = pl.program_id(1)
        causal = _rows((tq, tq)) <= _lanes((tq, tq))
        lse_ref[...] = jnp.zeros_like(lse_ref)

        def step(kj, carry, masked):
            off = pl.multiple_of(kj * tq, tq)
            out = []
            for hh in range(nh):
                m, l, acc = carry[hh]
                st = jnp.dot(ka_ref[pl.ds(off, tq), hh * AUG:(hh + 1) * AUG], qat_ref[hh * AUG:(hh + 1) * AUG, :],
                             preferred_element_type=F32)
                if masked:
                    st = jnp.where(causal, st, NEG_INF)
                m_new = jnp.maximum(m, jnp.max(st, axis=0, keepdims=True))
                alpha = jnp.exp(m - m_new)
                pt = jnp.exp(st - m_new)
                v = vt_ref[hh * HD:(hh + 1) * HD, pl.ds(off, tq)]
                out.append((m_new, alpha * l + jnp.sum(pt, axis=0, keepdims=True),
                            alpha * acc + jnp.dot(v, pt.astype(BF16), preferred_element_type=F32)))
            return tuple(out)

        init = tuple((jnp.full((1, tq), NEG_INF, F32), jnp.zeros((1, tq), F32), jnp.zeros((HD, tq), F32))
                     for _ in range(nh))
        carry = lax.fori_loop(0, qi, lambda kj, c: step(kj, c, False), init)
        carry = step(qi, carry, True)
        for hh in range(nh):
            m, l, _ = carry[hh]
            lse_ref[hh // 2, hh % 2:hh % 2 + 1, :] = m + jnp.log(l)
        for j in range(nh // 2):
            pair = jnp.concatenate([carry[2 * j][2] / carry[2 * j][1], carry[2 * j + 1][2] / carry[2 * j + 1][1]], axis=0)
            o_ref[:, j * 128:(j + 1) * 128] = jnp.transpose(pair).astype(BF16)

    return _call(body, name="attn", grid=(FX_H // nh, nq),
                 in_specs=[pl.BlockSpec((nh * AUG, tq), lambda h, i: (h, i)),
                           pl.BlockSpec((s, nh * AUG), lambda h, i: (0, h)),
                           pl.BlockSpec((nh * HD, s), lambda h, i: (h, 0))],
                 out_specs=[pl.BlockSpec((tq, nh * HD), lambda h, i: (i, h)),
                            pl.BlockSpec((nh // 2, 8, tq), lambda h, i: (h, 0, i))],
                 out_shape=[jax.ShapeDtypeStruct((s, FX_W), BF16), jax.ShapeDtypeStruct((FX_H // 2, 8, s), F32)],
                 compiler_params=_params("arbitrary", "arbitrary"))(qat, ka, vt)


def attn_grad(qa, qat, ka, kat, p, o, lse, dmix):
    s = qa.shape[0]
    tq = min(TQ, s)
    nq = s // tq
    scale = HD ** -0.5

    def body(qa_ref, qat_ref, ka_ref, kat_ref, v_ref, o_ref, lse_ref, do_ref,
             dq_ref, dk_ref, dv_ref, dfq_ref, dfk_ref, dot_scr, delta, dqt):
        kj = pl.program_id(1)
        lane = _lanes((tq, 128))
        low = lane < HD
        causal = _rows((tq, tq)) <= _lanes((tq, tq))

        @pl.when(kj == 0)
        def _():
            dqt[...] = jnp.zeros_like(dqt)
            delta[...] = jnp.zeros_like(delta)

            def prep(c, _):
                rows = pl.ds(pl.multiple_of(c * tq, tq), tq)
                do = do_ref[rows, :]
                pt = jnp.transpose(do * o_ref[rows, :].astype(F32))
                delta[0:1, rows] = jnp.sum(pt[:HD], axis=0, keepdims=True)
                delta[1:2, rows] = jnp.sum(pt[HD:], axis=0, keepdims=True)
                dot_scr[:, rows] = jnp.transpose(do).astype(BF16)
                return 0

            lax.fori_loop(0, nq, prep, 0)

        v = v_ref[...]
        vms = [jnp.where(low, v, 0.0).astype(BF16), jnp.where(low, 0.0, v).astype(BF16)]

        def tile(qi, carry, masked):
            cols = pl.ds(pl.multiple_of(qi * tq, tq), tq)
            do = do_ref[cols, :].astype(BF16)
            out = []
            for hh in range(2):
                cs = slice(hh * AUG, (hh + 1) * AUG)
                dka, dv = carry[hh]
                st = jnp.dot(ka_ref[:, cs], qat_ref[cs, cols], preferred_element_type=F32)
                if masked:
                    st = jnp.where(causal, st, NEG_INF)
                pt = jnp.exp(st - lse_ref[0, hh:hh + 1, cols])
                dv = dv + jnp.dot(pt.astype(BF16), do, preferred_element_type=F32)
                dpt = jnp.dot(vms[hh], dot_scr[:, cols], preferred_element_type=F32)
                dsb = (pt * (dpt - delta[hh:hh + 1, cols])).astype(BF16)
                dka = dka + jnp.dot(dsb, qa_ref[cols, cs], preferred_element_type=F32)
                dqt[hh, :, cols] += jnp.dot(kat_ref[cs, :], dsb, preferred_element_type=F32)
                out.append((dka, dv))
            return tuple(out)

        init = tuple((jnp.zeros((tq, AUG), F32), jnp.zeros((tq, 128), F32)) for _ in range(2))
        carry = tile(kj, init, True)
        carry = lax.fori_loop(kj + 1, nq, lambda qi, c: tile(qi, c, False), carry)
        dks = [carry[0][0], carry[1][0]]
        dvs = [carry[0][1], carry[1][1]]
        dv_ref[...] = jnp.where(low, dvs[0], dvs[1]).astype(BF16)
        dk_ref[...] = jnp.where(low, dks[0], pltpu.roll(dks[1], HD, 1)).astype(BF16)
        dfk_ref[...] = jnp.where(lane == 0, -dks[0][:, BIAS_COL:BIAS_COL + 1],
                                 jnp.where(lane == 1, -dks[1][:, BIAS_COL:BIAS_COL + 1], 0.0))

        @pl.when(kj == nq - 1)
        def _():
            def finish(c, _):
                rows = pl.ds(pl.multiple_of(c * tq, tq), tq)
                t0 = jnp.transpose(dqt[0, :, rows])
                t1 = jnp.transpose(dqt[1, :, rows])
                dq_ref[rows, :] = (jnp.where(low, t0, pltpu.roll(t1, HD, 1)) * scale).astype(BF16)
                dfq_ref[rows, :] = jnp.where(lane == 0, t0[:, FQ_COL:FQ_COL + 1],
                                             jnp.where(lane == 1, t1[:, FQ_COL:FQ_COL + 1], 0.0))
                return 0

            lax.fori_loop(0, nq, finish, 0)

    seq128 = lambda blk: pl.BlockSpec((s, 128), lambda h, j: (0, blk + h))
    tile128 = pl.BlockSpec((tq, 128), lambda h, j: (j, h))
    out_b = jax.ShapeDtypeStruct((s, FX_W), BF16)
    out_f = jax.ShapeDtypeStruct((s, FX_W), F32)
    return _call(body, name="attn_grad", grid=(FX_H // 2, nq),
                 in_specs=[pl.BlockSpec((s, 2 * AUG), lambda h, j: (0, h)), pl.BlockSpec((2 * AUG, s), lambda h, j: (h, 0)),
                           pl.BlockSpec((tq, 2 * AUG), lambda h, j: (j, h)), pl.BlockSpec((2 * AUG, tq), lambda h, j: (h, j)),
                           pl.BlockSpec((tq, 128), lambda h, j: (j, V_BLK + h)), seq128(0),
                           pl.BlockSpec((1, 8, s), lambda h, j: (h, 0, 0)), seq128(4)],
                 out_specs=[seq128(0), tile128, tile128, seq128(0), tile128],
                 out_shape=[out_b, out_b, out_b, out_f, out_f],
                 scratch_shapes=[pltpu.VMEM((128, s), BF16), pltpu.VMEM((8, s), F32), pltpu.VMEM((2, AUG, s), F32)],
                 compiler_params=_params("arbitrary", "arbitrary"))(qa, qat, ka, kat, p, o, lse, dmix)


def _shift_down(a, prev8, k):
    r = pltpu.roll(a, k, 0)
    top = jnp.where(_rows(prev8.shape) < k, pltpu.roll(prev8, k, 0), r[0:8])
    return jnp.concatenate([top, r[8:]], axis=0)


def _shift_up(a, next8, k):
    t = a.shape[0]
    r = pltpu.roll(a, t - k, 0)
    bot = jnp.where(_rows(next8.shape) >= 8 - k, pltpu.roll(next8, 8 - k, 0), r[t - 8:t])
    return jnp.concatenate([r[:t - 8], bot], axis=0)


def _conv(a, prev8, cw, cb):
    return cb + cw[0:1, :] * _shift_down(a, prev8, 2) + cw[1:2, :] * _shift_down(a, prev8, 1) + cw[2:3, :] * a


def conv_fwd(up, cw, cb):
    s = up.shape[0]
    ts = min(TS_C, s)

    def body(a_ref, g_ref, cw_ref, cb_ref, o_ref, halo):
        @pl.when(pl.program_id(0) == 0)
        def _():
            halo[...] = jnp.zeros_like(halo)

        a = a_ref[...]
        c = _conv(a, halo[...], cw_ref[...], cb_ref[...])
        o_ref[...] = (jax.nn.gelu(c) * g_ref[...]).astype(BF16)
        halo[...] = a[ts - 8:ts, :]

    return _call(body, name="conv_fwd", grid=(s // ts,),
                 in_specs=[pl.BlockSpec((ts, DFF), lambda i: (i, 0)), pl.BlockSpec((ts, DFF), lambda i: (i, 1)),
                           pl.BlockSpec((3, DFF), lambda i: (0, 0)), pl.BlockSpec((1, DFF), lambda i: (0, 0))],
                 out_specs=pl.BlockSpec((ts, DFF), lambda i: (i, 0)),
                 out_shape=jax.ShapeDtypeStruct((s, DFF), BF16),
                 scratch_shapes=[pltpu.VMEM((8, DFF), F32)],
                 compiler_params=_params("arbitrary"))(up, up, cw, cb)


def conv_bwd(up, cw, cb, dact):
    s = up.shape[0]
    ts = min(TS_C, s)
    ns = s // ts

    def body(a_ref, g_ref, prev_ref, cw_ref, cb_ref, dact_ref, dup_ref, dcw_ref, dcb_ref, halo):
        i = pl.program_id(0)

        @pl.when(i == 0)
        def _():
            halo[...] = jnp.zeros_like(halo)
            dcw_ref[...] = jnp.zeros_like(dcw_ref)
            dcb_ref[...] = jnp.zeros_like(dcb_ref)

        a = a_ref[...]
        g = g_ref[...]
        cw = cw_ref[...]
        prev8 = prev_ref[...] * (i < ns - 1).astype(F32)
        c = _conv(a, prev8, cw, cb_ref[...])
        gel, vjp = jax.vjp(jax.nn.gelu, c)
        dact = dact_ref[...]
        (dc,) = vjp(dact * g)
        up1 = _shift_up(dc, halo[...], 1)
        up2 = _shift_up(dc, halo[...], 2)
        da = cw[2:3, :] * dc + cw[1:2, :] * up1 + cw[0:1, :] * up2
        dup_ref[:, :DFF] = da.astype(BF16)
        dup_ref[:, DFF:] = (dact * gel).astype(BF16)
        dcw_ref[0:1, :] += jnp.sum(a * up2, axis=0, keepdims=True)
        dcw_ref[1:2, :] += jnp.sum(a * up1, axis=0, keepdims=True)
        dcw_ref[2:3, :] += jnp.sum(a * dc, axis=0, keepdims=True)
        dcb_ref[...] += jnp.sum(dc, axis=0, keepdims=True)
        halo[...] = dc[0:8, :]

    rev = lambda col: (lambda i: (ns - 1 - i, col))
    prev_map = lambda i: (jnp.maximum((ns - 1 - i) * (ts // 8) - 1, 0), 0)
    return _call(body, name="conv_bwd", grid=(ns,),
                 in_specs=[pl.BlockSpec((ts, DFF), rev(0)), pl.BlockSpec((ts, DFF), rev(1)),
                           pl.BlockSpec((8, DFF), prev_map), pl.BlockSpec((3, DFF), lambda i: (0, 0)),
                           pl.BlockSpec((1, DFF), lambda i: (0, 0)), pl.BlockSpec((ts, DFF), rev(0))],
                 out_specs=[pl.BlockSpec((ts, 2 * DFF), rev(0)), pl.BlockSpec((3, DFF), lambda i: (0, 0)),
                            pl.BlockSpec((1, DFF), lambda i: (0, 0))],
                 out_shape=[jax.ShapeDtypeStruct((s, 2 * DFF), BF16), jax.ShapeDtypeStruct((3, DFF), F32),
                            jax.ShapeDtypeStruct((1, DFF), F32)],
                 scratch_shapes=[pltpu.VMEM((8, DFF), F32)],
                 compiler_params=_params("arbitrary"))(up, up, up, cw, cb, dact)


def _blockdiag_expand(m):
    m4 = m.reshape(S5_H, 2, S5_G, S5_P)
    eye = jnp.eye(S5_G, dtype=bool)[:, None, None, :, None]
    return jnp.where(eye, m4[None], 0.0).reshape(S5_W, 2 * S5_N)


def _blockdiag_extract(mbd):
    m5 = mbd.reshape(S5_G, S5_H, 2, S5_G, S5_P)
    diag = jnp.stack([m5[g, :, :, g, :] for g in range(S5_G)], axis=2)
    return diag.reshape(S5_H, 2 * S5_N)


def _c_expand(c_re, c_im):
    c4 = jnp.stack([c_re, c_im], axis=2)
    eye = jnp.eye(S5_G, dtype=bool)[:, None, None, :, None]
    return jnp.where(eye, c4[:, :, :, None, :], 0.0).reshape(S5_W, 2 * S5_N)


def _c_extract(cbd):
    m5 = cbd.reshape(S5_G, S5_H, 2, S5_G, S5_P)
    d = jnp.stack([m5[g, :, :, g, :] for g in range(S5_G)], axis=0)
    return d[:, :, 0, :], d[:, :, 1, :]


def _glu_expand(w):
    eye = jnp.eye(S5_G, dtype=bool)[:, None, :, None]
    return jnp.where(eye, w[:, :, None, :], 0.0).reshape(S5_W, S5_W)


def _glu_extract(wbd):
    m4 = wbd.reshape(S5_G, S5_H, S5_G, S5_H)
    return jnp.stack([m4[g, :, g, :] for g in range(S5_G)], axis=0)


SMALL = ("b_f", "gm_ln_g", "gm_ln_b", "gm_w_s", "gm_b_s", "s5_lam_re", "s5_lam_im", "s5_log_dt", "s5_b_re", "s5_b_im",
         "s5_c_re", "s5_c_im", "s5_d", "s5_w_glu", "s5_b_glu", "ln1_g", "ln1_b", "conv_b", "ln2_g", "ln2_b")


def _layer_operands(sp, l):
    f = {}
    f["bf"] = jnp.pad(sp["b_f"][l][None, :], ((0, 0), (0, 128 - FX_H)))
    f["gm_lg"] = sp["gm_ln_g"][l].reshape(1, GM_W)
    f["gm_lb"] = sp["gm_ln_b"][l].reshape(1, GM_W)
    f["gm_ws"] = sp["gm_w_s"][l]
    f["gm_bst"] = sp["gm_b_s"][l].T
    f["lr"] = sp["s5_lam_re"][l].reshape(1, S5_N)
    f["li"] = sp["s5_lam_im"][l].reshape(1, S5_N)
    f["ldt"] = jnp.repeat(sp["s5_log_dt"][l], S5_P).reshape(1, S5_N)
    bt = lambda b: jnp.transpose(b, (2, 0, 1)).reshape(S5_H, S5_N)
    f["bt"] = jnp.concatenate([bt(sp["s5_b_re"][l]), bt(sp["s5_b_im"][l])], axis=1)
    f["cbd"] = _c_expand(sp["s5_c_re"][l], sp["s5_c_im"][l])
    f["drow"] = sp["s5_d"][l].reshape(1, S5_W)
    f["wg"] = _glu_expand(sp["s5_w_glu"][l])
    f["bg"] = sp["s5_b_glu"][l].reshape(1, S5_W)
    for n in ("ln1_g", "ln1_b", "ln2_g", "ln2_b"):
        f[n] = sp[n][l][None, :]
    f["cb"] = sp["conv_b"][l][None, :]
    return f


def layer_fwd_mix(x, mod, w_in, f):
    p = mm_nn(x, w_in, NP, "in_proj", mod=mod, rows=(0, 1))
    ygm = gm_fwd(p, f["gm_lg"], f["gm_lb"], f["gm_ws"], f["gm_bst"])
    arow, bbt = s5_prep_fwd(f["lr"], f["li"], f["ldt"], f["bt"])
    bbd = _blockdiag_expand(bbt)
    ys5, st = s5_fwd(p, arow, bbd, f["cbd"], f["drow"], f["wg"], f["bg"])
    qa, ka, qat, kat, vt = fox_prep(p, f["bf"])
    yfx, lse = attn(qat, ka, vt)
    mixcat = jnp.concatenate([ygm, ys5, yfx], axis=1)
    return mixcat, dict(f=f, x=x, p=p, arow=arow, bbd=bbd, st=st, qa=qa, ka=ka, qat=qat, kat=kat, yfx=yfx, lse=lse,
                        mixcat=mixcat)


def layer_fwd_rest(x, mixcat, mod, w, f, saved):
    mix = mm_nn(mixcat, w["w_out"], D, "out_proj")
    x1 = post_fwd(x, mix, mod, 2, f["ln1_g"], f["ln1_b"], "post1_fwd")
    up = mm_nn(x1, w["w_up"], DFF // 2, "up_proj", mod=mod, rows=(3, 4))
    act = conv_fwd(up, w["conv_w"], f["cb"])
    ffn = mm_nn(act, w["w_down"], D, "down_proj")
    x2 = post_fwd(x1, ffn, mod, 5, f["ln2_g"], f["ln2_b"], "post2_fwd")
    return x2, dict(saved, mix=mix, x1=x1, up=up, act=act, ffn=ffn)


def layer_fwd(x, mod, w, f):
    mixcat, saved = layer_fwd_mix(x, mod, w["w_in"], f)
    return layer_fwd_rest(x, mixcat, mod, w, f, saved)


def layer_bwd_ffn(dx, sv, mod, w):
    f = sv["f"]
    dx1, dffn, dg2, dlg2, dlb2 = post_bwd(sv["x1"], sv["ffn"], mod, 5, f["ln2_g"], f["ln2_b"], dx, "post2_bwd")
    g_down = mm_tn(sv["act"], dffn, D // 2, "down_dw")
    dact = mm_nt(dffn, w["w_down"], "down_dx")
    dup, dcw, dcb = conv_bwd(sv["up"], w["conv_w"], f["cb"], dact)
    g_up = mm_tn(sv["x1"], dup, DFF // 2, "up_dw", mod=mod, rows=(3, 4))
    dx1, dsh2, dsc2 = mm_nt_mod(dup, w["w_up"], sv["x1"], dx1, mod, (3, 4), "up_dx")
    return dx1, dict(w_up=g_up, w_down=g_down, conv_w=dcw), dict(dsh2=dsh2, dsc2=dsc2, dg2=dg2, conv_b=dcb[0],
                                                                 ln2_g=dlg2[0], ln2_b=dlb2[0])


def layer_bwd_mix(dx1, sv, mod, w, part):
    f = sv["f"]
    dx0, dmix, dg1, dlg1, dlb1 = post_bwd(sv["x"], sv["mix"], mod, 2, f["ln1_g"], f["ln1_b"], dx1, "post1_bwd")
    g_out = mm_tn(sv["mixcat"], dmix, D, "out_dw")
    dmc = mm_nt(dmix, w["w_out"], "out_dx")
    duv, dgm_lg, dgm_lb, dgm_ws, dgm_bst = gm_bwd(sv["p"], f["gm_lg"], f["gm_lb"], f["gm_ws"], f["gm_bst"], dmc)
    du5, da, dbbd, dcbd, dd5, dwg, dbg = s5_bwd(sv["p"], sv["st"], sv["arow"], sv["bbd"], f["cbd"], f["drow"],
                                                f["wg"], f["bg"], dmc)
    dlr, dli, dldt, dbt = s5_prep_bwd(f["lr"], f["li"], f["ldt"], f["bt"], da, _blockdiag_extract(dbbd))
    dq, dk, dv, dfq, dfk = attn_grad(sv["qa"], sv["qat"], sv["ka"], sv["kat"], sv["p"], sv["yfx"], sv["lse"], dmc)
    dff, dbf = fox_prep_grad(sv["p"], f["bf"], dfq, dfk)
    dp = jnp.concatenate([duv, du5, dq, dk, dv, dff], axis=1)
    g_in = mm_tn(sv["x"], dp, NP, "in_dw", mod=mod, rows=(0, 1))
    dx, dsh1, dsc1 = mm_nt_mod(dp, w["w_in"], sv["x"], dx0, mod, (0, 1), "in_dx")

    dmod = jnp.concatenate([dsh1, dsc1, dg1, part["dsh2"], part["dsc2"], part["dg2"]], axis=0)
    dc_re, dc_im = _c_extract(dcbd)
    dbt4 = dbt.reshape(S5_H, 2, S5_G, S5_P)
    vals = dict(b_f=dbf[0, :FX_H], gm_ln_g=dgm_lg.reshape(GM_H, HD), gm_ln_b=dgm_lb.reshape(GM_H, HD),
                gm_w_s=dgm_ws, gm_b_s=dgm_bst.T, s5_lam_re=dlr.reshape(S5_G, S5_P),
                s5_lam_im=dli.reshape(S5_G, S5_P), s5_log_dt=dldt[0, :S5_G],
                s5_b_re=jnp.transpose(dbt4[:, 0], (1, 2, 0)), s5_b_im=jnp.transpose(dbt4[:, 1], (1, 2, 0)),
                s5_c_re=dc_re, s5_c_im=dc_im, s5_d=dd5.reshape(S5_G, S5_H), s5_w_glu=_glu_extract(dwg),
                s5_b_glu=dbg.reshape(S5_G, S5_H), ln1_g=dlg1[0], ln1_b=dlb1[0], conv_b=part["conv_b"],
                ln2_g=part["ln2_g"], ln2_b=part["ln2_b"])
    return dx, dict(w_in=g_in, w_out=g_out), vals, dmod


def layer_bwd(dx, sv, mod, w):
    dx1, g_ffn, part = layer_bwd_ffn(dx, sv, mod, w)
    dx, g_mix, vals, dmod = layer_bwd_mix(dx1, sv, mod, w, part)
    return dx, dict(g_ffn, **g_mix), vals, dmod


def local_step(x, target, mods, big, sp):
    saved = []
    for l in range(DEPTH):
        x, sv = layer_fwd(x, mods[l], big[l], _layer_operands(sp, l))
        saved.append(sv)
    loss_tile, dx = loss_kernel(x, target)
    gbig, vals, dmods = [None] * DEPTH, [None] * DEPTH, [None] * DEPTH
    for l in reversed(range(DEPTH)):
        dx, gbig[l], vals[l], dmods[l] = layer_bwd(dx, saved[l], mods[l], big[l])
    gsm = {n: jnp.stack([v[n] for v in vals]) for n in SMALL}
    return loss_tile, dx, gbig, gsm, jnp.stack(dmods)


def _my_index():
    return 4 * lax.axis_index("x") + 2 * lax.axis_index("y") + lax.axis_index("c")


def exchange(tensors, scatter, name):
    n = len(tensors)

    def body(*refs):
        ins, outs = refs[:n], refs[n:2 * n]
        send_sems, recv_sems, local_sems = refs[2 * n:]
        x, y, c = lax.axis_index("x"), lax.axis_index("y"), lax.axis_index("c")
        me = 4 * x + 2 * y + c
        local = []
        for t in range(n):
            cp = pltpu.make_async_copy(ins[t].at[me] if scatter else ins[t], outs[t].at[me], local_sems.at[t])
            cp.start()
            local.append(cp)
        remote = []
        for m in range(1, NDEV):
            px = 1 - x if m & 4 else x
            py = 1 - y if m & 2 else y
            pc = 1 - c if m & 1 else c
            peer = 4 * px + 2 * py + pc
            for t in range(n):
                k = t * (NDEV - 1) + m - 1
                cp = pltpu.make_async_remote_copy(
                    src_ref=ins[t].at[peer] if scatter else ins[t], dst_ref=outs[t].at[me],
                    send_sem=send_sems.at[k], recv_sem=recv_sems.at[k],
                    device_id=(px, py, pc), device_id_type=MESH_IDS)
                cp.start()
                remote.append(cp)
        for cp in remote:
            cp.wait()
        for cp in local:
            cp.wait()

    hbm = pl.BlockSpec(memory_space=pltpu.HBM)
    out_shape = [jax.ShapeDtypeStruct(t.shape if scatter else (NDEV,) + t.shape, t.dtype) for t in tensors]
    return _call(body, name=name, in_specs=[hbm] * n, out_specs=[hbm] * n, out_shape=out_shape,
                 scratch_shapes=[pltpu.SemaphoreType.DMA((n * (NDEV - 1),)), pltpu.SemaphoreType.DMA((n * (NDEV - 1),)),
                                 pltpu.SemaphoreType.DMA((n,))])(*tensors)


def _peers():
    x, y, c = lax.axis_index("x"), lax.axis_index("y"), lax.axis_index("c")
    out = []
    for m in range(1, NDEV):
        px = 1 - x if m & 4 else x
        py = 1 - y if m & 2 else y
        pc = 1 - c if m & 1 else c
        out.append(((px, py, pc), 4 * px + 2 * py + pc))
    return 4 * x + 2 * y + c, out


def _split_copies(v_refs, land_refs, send_sems, recv_sems, scatter):
    me, peers = _peers()
    return [pltpu.make_async_remote_copy(
        src_ref=v_ref.at[idx] if scatter else v_ref, dst_ref=land_ref.at[me],
        send_sem=send_sems.at[t * (NDEV - 1) + k], recv_sem=recv_sems.at[t * (NDEV - 1) + k],
        device_id=pos, device_id_type=MESH_IDS)
        for t, (v_ref, land_ref) in enumerate(zip(v_refs, land_refs)) for k, (pos, idx) in enumerate(peers)]


_HBM_SPEC = pl.BlockSpec(memory_space=pltpu.HBM)
_SEM_SPEC = pl.BlockSpec(memory_space=pltpu.SEMAPHORE)
_SPLIT_EFFECT = pltpu.SideEffectType.DATAFLOW_SIDE_EFFECTING


def exchange_start(tensors, scatter, name):
    n = len(tensors)
    land_shapes = [t.shape if scatter else (NDEV,) + t.shape for t in tensors]

    def body(*refs):
        v_refs, land_refs = refs[:n], refs[n:2 * n]
        send_sems, recv_sems = refs[2 * n], refs[2 * n + 1]
        token = refs[-1]
        for cp in _split_copies(v_refs, land_refs, send_sems, recv_sems, scatter):
            cp.start()
        token[...] = jnp.zeros_like(token)

    sems = pltpu.SemaphoreType.DMA((n * (NDEV - 1),))
    out = _call(
        body, name=name,
        out_shape=(sems, sems, *[pltpu.HBM(t.shape, t.dtype) for t in tensors],
                   *[pltpu.HBM(s, t.dtype) for s, t in zip(land_shapes, tensors)], jax.ShapeDtypeStruct((8, 128), F32)),
        in_specs=(_HBM_SPEC,) * (2 * n),
        out_specs=(_SEM_SPEC, _SEM_SPEC) + (_HBM_SPEC,) * (2 * n) + (pl.BlockSpec(memory_space=pltpu.VMEM),),
        input_output_aliases={i: i + 2 for i in range(2 * n)},
        compiler_params=pltpu.CompilerParams(has_side_effects=_SPLIT_EFFECT),
    )(*[pltpu.with_memory_space_constraint(t, pltpu.HBM) for t in tensors],
      *[pltpu.with_memory_space_constraint(lax.empty(s, t.dtype), pltpu.HBM) for s, t in zip(land_shapes, tensors)])
    return out[0], out[1], list(out[2:2 + n]), list(out[2 + n:2 + 2 * n]), out[-1]


def exchange_wait(started, after, scatter, name):
    send_sems, recv_sems, v_thru, land_thru, _ = started
    n = len(v_thru)

    def body(*refs):
        v_refs, land_refs = refs[:n], refs[n:2 * n]
        for cp in _split_copies(v_refs, land_refs, refs[2 * n], refs[2 * n + 1], scatter):
            cp.wait_send()
            cp.wait_recv()

    out = _call(
        body, name=name,
        out_shape=tuple(pltpu.HBM(t.shape, t.dtype) for t in v_thru + land_thru),
        in_specs=(_HBM_SPEC,) * (2 * n) + (_SEM_SPEC, _SEM_SPEC, pl.BlockSpec(memory_space=pl.ANY)),
        out_specs=(_HBM_SPEC,) * (2 * n), input_output_aliases={i: i for i in range(2 * n)},
        compiler_params=pltpu.CompilerParams(has_side_effects=_SPLIT_EFFECT),
    )(*v_thru, *land_thru, send_sems, recv_sems, after)
    return list(out[:n]), list(out[n:])


def mod_slices(c_all, w_ada, b_loc):
    nl, _, nc = w_ada.shape

    def body(c_ref, w_ref, b_ref, o_ref):
        cv = c_ref[...]
        o_ref[0] = _nn(cv * jax.nn.sigmoid(cv), w_ref[0]) + b_ref[0]

    return _call(body, name="mod_slices", grid=(nl,),
                 in_specs=[pl.BlockSpec((NDEV, D), lambda l: (0, 0)), pl.BlockSpec((1, D, nc), lambda l: (l, 0, 0)),
                           pl.BlockSpec((1, 1, nc), lambda l: (l, 0, 0))],
                 out_specs=pl.BlockSpec((1, NDEV, nc), lambda l: (l, 0, 0)),
                 out_shape=jax.ShapeDtypeStruct((nl, NDEV, nc), F32),
                 compiler_params=_params("arbitrary"))(c_all, w_ada, b_loc.reshape(nl, 1, nc))


def ada_grad(c_all, dm_loc):
    nl, _, nc = dm_loc.shape

    def body(c_ref, d_ref, o_ref):
        cv = c_ref[...]
        o_ref[0] = _tn(cv * jax.nn.sigmoid(cv), d_ref[0])

    return _call(body, name="ada_grad", grid=(nl,),
                 in_specs=[pl.BlockSpec((NDEV, D), lambda l: (0, 0)), pl.BlockSpec((1, NDEV, nc), lambda l: (l, 0, 0))],
                 out_specs=pl.BlockSpec((1, D, nc), lambda l: (l, 0, 0)),
                 out_shape=jax.ShapeDtypeStruct((nl, D, nc), F32),
                 compiler_params=_params("arbitrary"))(c_all, dm_loc)


def sum_chunks(chunks):
    r = chunks.shape[1]

    def body(c_ref, o_ref):
        acc = c_ref[0]
        for i in range(1, NDEV):
            acc = acc + c_ref[i]
        o_ref[...] = acc

    return _call(body, name="sum_chunks", out_shape=jax.ShapeDtypeStruct((r, 128), F32))(chunks)


def _row_tile(r):
    if r <= 256:
        return r
    for t in range(256, 7, -8):
        if r % t == 0:
            return t
    return r


def adamw(w, m, v, g=None, chunks=None, name="adamw"):
    r, cdim = w.shape
    tr = _row_tile(r)
    bc1 = 1.0 - ADAM_B1 ** ADAM_STEP
    bc2 = 1.0 - ADAM_B2 ** ADAM_STEP

    def body(g_ref, w_ref, m_ref, v_ref, go_ref, d_ref, mo_ref, vo_ref):
        if chunks is None:
            grad = g_ref[...]
        else:
            grad = g_ref[0].astype(F32)
            for i in range(1, NDEV):
                grad = grad + g_ref[i].astype(F32)
        mn = ADAM_B1 * m_ref[...] + (1.0 - ADAM_B1) * grad
        vn = ADAM_B2 * v_ref[...] + (1.0 - ADAM_B2) * (grad * grad)
        m_hat = mn / bc1
        v_hat = vn / bc2
        go_ref[...] = grad
        d_ref[...] = -ADAM_LR * (m_hat / (jnp.sqrt(v_hat) + ADAM_EPS) + ADAM_WD * w_ref[...])
        mo_ref[...] = mn
        vo_ref[...] = vn

    tile = pl.BlockSpec((tr, cdim), lambda i: (i, 0))
    gspec = tile if chunks is None else pl.BlockSpec((NDEV, tr, cdim), lambda i: (0, i, 0))
    shp = jax.ShapeDtypeStruct((r, cdim), F32)
    return _call(body, name=name, grid=(r // tr,), in_specs=[gspec, tile, tile, tile],
                 out_specs=[tile] * 4, out_shape=[shp] * 4,
                 compiler_params=_params("arbitrary"))(g if chunks is None else chunks, w, m, v)


def adamw_layers(w, m, v, chunks, name):
    nl, r, cdim = w.shape
    tr = _row_tile(r)
    bc1 = 1.0 - ADAM_B1 ** ADAM_STEP
    bc2 = 1.0 - ADAM_B2 ** ADAM_STEP
    outs = [lax.empty(w.shape, F32) for _ in range(4)]
    for l in range(nl):
        def body(g_ref, w_ref, m_ref, v_ref, p0, p1, p2, p3, go_ref, d_ref, mo_ref, vo_ref):
            grad = g_ref[0].astype(F32)
            for i in range(1, NDEV):
                grad = grad + g_ref[i].astype(F32)
            mn = ADAM_B1 * m_ref[...] + (1.0 - ADAM_B1) * grad
            vn = ADAM_B2 * v_ref[...] + (1.0 - ADAM_B2) * (grad * grad)
            go_ref[...] = grad
            d_ref[...] = -ADAM_LR * ((mn / bc1) / (jnp.sqrt(vn / bc2) + ADAM_EPS) + ADAM_WD * w_ref[...])
            mo_ref[...] = mn
            vo_ref[...] = vn

        tile = pl.BlockSpec((None, tr, cdim), lambda i, l=l: (l, i, 0))
        whole = pl.BlockSpec(memory_space=pl.ANY)
        outs = _call(body, name=f"{name}_{l}", grid=(r // tr,),
                     in_specs=[pl.BlockSpec((NDEV, tr, cdim), lambda i: (0, i, 0)), tile, tile, tile] + [whole] * 4,
                     out_specs=[tile] * 4, out_shape=[jax.ShapeDtypeStruct(w.shape, F32)] * 4,
                     input_output_aliases={4: 0, 5: 1, 6: 2, 7: 3},
                     compiler_params=_params("arbitrary"))(chunks[l], w, m, v, *outs)
    return outs


WEIGHTS = ("w_ada", "b_ada", "w_in", "b_f", "gm_ln_g", "gm_ln_b", "gm_w_s", "gm_b_s", "s5_lam_re", "s5_lam_im",
           "s5_log_dt", "s5_b_re", "s5_b_im", "s5_c_re", "s5_c_im", "s5_d", "s5_w_glu", "s5_b_glu", "w_out", "ln1_g",
           "ln1_b", "w_up", "conv_w", "conv_b", "w_down", "ln2_g", "ln2_b")
SHARDED = ("w_in", "w_out", "w_up", "w_down", "conv_w")
LARGE = ("w_in", "w_out", "w_up", "w_down")
COL_SHARDED = ("w_in", "w_up", "conv_w")
PACKED = ("b_ada",) + SMALL
PACK_SEG = 8 * 128


def _gather_cols(g):
    nd, nl, r, c = g.shape
    return jnp.transpose(g, (1, 2, 0, 3)).reshape(nl, r, nd * c)


def _chunk_cols(g):
    nl, r, c8 = g.shape
    return jnp.transpose(g.reshape(nl, r, NDEV, c8 // NDEV), (2, 0, 1, 3))


def _join_cols(g):
    nd, r, c = g.shape
    return jnp.transpose(g, (1, 0, 2)).reshape(r, nd * c)


def _join_rows(g):
    nd, r, c = g.shape
    return g.reshape(nd * r, c)


def _split_cols(g):
    r, c8 = g.shape
    return jnp.transpose(g.reshape(r, NDEV, c8 // NDEV), (1, 0, 2))


def _split_rows(g):
    r8, c = g.shape
    return g.reshape(NDEV, r8 // NDEV, c)


def _pack(parts):
    segs = []
    for n in PACKED:
        flat = parts[n].reshape(-1)
        segs.append(jnp.pad(flat, (0, -flat.shape[0] % PACK_SEG)).reshape(-1, 128))
    rows = jnp.concatenate(segs, axis=0)
    return jnp.pad(rows, ((0, -rows.shape[0] % (NDEV * 8)), (0, 0)))


def _unpack(rows, shapes):
    out, off = {}, 0
    for n in PACKED:
        size = math.prod(shapes[n])
        nrows = -(-size // PACK_SEG) * 8
        out[n] = rows[off:off + nrows].reshape(-1)[:size].reshape(shapes[n])
        off += nrows
    return out


def kernel(x, c, w_ada, b_ada, w_in, b_f, gm_ln_g, gm_ln_b, gm_w_s, gm_b_s, s5_lam_re, s5_lam_im, s5_log_dt, s5_b_re, s5_b_im, s5_c_re, s5_c_im, s5_d, s5_w_glu, s5_b_glu, w_out, ln1_g, ln1_b, w_up, conv_w, conv_b, w_down, ln2_g, ln2_b, loss_target, m_w_ada, m_b_ada, m_w_in, m_b_f, m_gm_ln_g, m_gm_ln_b, m_gm_w_s, m_gm_b_s, m_s5_lam_re, m_s5_lam_im, m_s5_log_dt, m_s5_b_re, m_s5_b_im, m_s5_c_re, m_s5_c_im, m_s5_d, m_s5_w_glu, m_s5_b_glu, m_w_out, m_ln1_g, m_ln1_b, m_w_up, m_conv_w, m_conv_b, m_w_down, m_ln2_g, m_ln2_b, v_w_ada, v_b_ada, v_w_in, v_b_f, v_gm_ln_g, v_gm_ln_b, v_gm_w_s, v_gm_b_s, v_s5_lam_re, v_s5_lam_im, v_s5_log_dt, v_s5_b_re, v_s5_b_im, v_s5_c_re, v_s5_c_im, v_s5_d, v_s5_w_glu, v_s5_b_glu, v_w_out, v_ln1_g, v_ln1_b, v_w_up, v_conv_w, v_conv_b, v_w_down, v_ln2_g, v_ln2_b):
    given = dict(locals())
    wts = {n: given[n] for n in WEIGHTS}
    mom = {n: given["m_" + n] for n in WEIGHTS}
    var = {n: given["v_" + n] for n in WEIGHTS}
    nl = w_ada.shape[0]
    me = _my_index()
    ada_cols = w_ada.shape[2]

    (c_all,) = exchange([c], False, "gather_c")
    c_all = c_all.reshape(NDEV, D)
    b_loc = lax.dynamic_slice_in_dim(b_ada, me * ada_cols, ada_cols, axis=1)
    mod_part = mod_slices(c_all, w_ada, b_loc)

    mod_all, conv_all = exchange([mod_part, conv_w], False, "gather_mod")
    mod_mine = lax.dynamic_index_in_dim(mod_all, me, axis=2, keepdims=False)
    mods = jnp.transpose(mod_mine, (1, 0, 2)).reshape(nl, 6, D)
    mods = jnp.pad(mods, ((0, 0), (0, 2), (0, 0)))
    conv_full = _gather_cols(conv_all)
    sp = {n: wts[n] for n in SMALL}

    def joined(name, own, land):
        full = (_join_cols if name in COL_SHARDED else _join_rows)(land)
        at = (0, me * own.shape[1]) if name in COL_SHARDED else (me * own.shape[0], 0)
        return lax.dynamic_update_slice(full, own, at)

    def block(l, names):
        return [wts[n][l].astype(BF16) for n in names]

    def chunked(grads, names):
        return [(_split_cols if n in COL_SHARDED else _split_rows)(grads[n]) for n in names]

    head, tail = LARGE[:1], LARGE[1:]
    got_head = exchange_start(block(0, head), False, "gather_start_0_in")
    rest = block(0, tail)
    rest[0] = rest[0] + got_head[4][0, 0].astype(BF16)
    got_tail = exchange_start(rest, False, "gather_start_0_rest")
    xl, saved, weights = x[0], [], []
    for l in range(nl):
        if l == 0:
            own, land = exchange_wait(got_head, got_tail[4], False, "gather_wait_0_in")
            w = {n: joined(n, o, g) for n, o, g in zip(head, own, land)}
        else:
            own, land = exchange_wait(started, xl, False, f"gather_wait_{l}")
            w = {n: joined(n, o, g) for n, o, g in zip(LARGE, own, land)}
        mod_l = mods[l]
        if l + 1 < nl:
            nxt, w["w_in"] = lax.optimization_barrier((block(l + 1, LARGE), w["w_in"]))
            started = exchange_start(nxt, False, f"gather_start_{l + 1}")
            mod_l = mod_l + started[4][0, 0]
        w["w_in"] = jnp.pad(w["w_in"], ((0, 0), (0, NP - D_IN)))
        f = _layer_operands(sp, l)
        mixcat, sv = layer_fwd_mix(xl, mod_l, w["w_in"], f)
        if l == 0:
            own, land = exchange_wait(got_tail, mixcat, False, "gather_wait_0_rest")
            w.update({n: joined(n, o, g) for n, o, g in zip(tail, own, land)})
        w["conv_w"] = conv_full[l]
        xl, sv = layer_fwd_rest(xl, mixcat, mod_l, w, f, sv)
        weights.append(w)
        saved.append(sv)

    loss_tile, dx = loss_kernel(xl, loss_target[0])

    ffn_names, mix_names = ("w_up", "w_down"), ("w_in", "w_out")
    scattering, vals, dmods, gconv = [None] * nl, [None] * nl, [None] * nl, [None] * nl
    token = jnp.zeros((), F32)
    for l in reversed(range(nl)):
        mod_l = mods[l] + token
        dx1, g_ffn, part = layer_bwd_ffn(dx, saved[l], mod_l, weights[l])
        gconv[l] = g_ffn["conv_w"]
        if l == 0:
            sent_ffn = exchange_start(chunked(g_ffn, ffn_names), True, "scatter_start_0_ffn")
            mod_l = mod_l + sent_ffn[4][0, 0]
        dx, g_mix, vals[l], dmods[l] = layer_bwd_mix(dx1, saved[l], mod_l, weights[l], part)
        g_mix["w_in"] = g_mix["w_in"][:, :D_IN]
        if l == 0:
            scattering[l] = [(ffn_names, sent_ffn),
                             (mix_names, exchange_start(chunked(g_mix, mix_names), True, "scatter_start_0_mix"))]
        else:
            sent = exchange_start(chunked(dict(g_ffn, **g_mix), LARGE), True, f"scatter_start_{l}")
            scattering[l] = [(LARGE, sent)]
            token = sent[4][0, 0]
    gx = dx
    dmods = jnp.stack(dmods)
    gsm = {n: jnp.stack([v[n] for v in vals]) for n in SMALL}

    gsm["b_ada"] = dmods.reshape(nl, 6 * D)
    packed = _pack(gsm).reshape(NDEV, -1, 128)
    conv_recv, small_recv = exchange([_chunk_cols(jnp.stack(gconv)), packed], True, "scatter_small")
    small_sum = sum_chunks(small_recv)
    small_all, dmod_all = exchange([small_sum, dmods.reshape(nl, 6 * D)], False, "gather_small")

    received, after = [dict() for _ in range(nl)], small_all
    for l in reversed(range(nl)):
        for k, (names, sent) in enumerate(scattering[l]):
            own, land = exchange_wait(sent, after, True, f"scatter_wait_{l}_{k}")
            for n, o, g in zip(names, own, land):
                mine = lax.dynamic_index_in_dim(o, me, 0, keepdims=False)
                received[l][n] = lax.dynamic_update_index_in_dim(g, mine, me, 0)
            after = land[0]

    out = {}
    for n in LARGE:
        out[n] = adamw_layers(wts[n], mom[n], var[n], [received[l][n] for l in range(nl)], "adamw_" + n)
    shp = conv_w.shape
    two_d = lambda a: a.reshape(shp[0] * shp[1], shp[2])
    res = adamw(two_d(conv_w), two_d(m_conv_w), two_d(v_conv_w),
                chunks=conv_recv.reshape(NDEV, shp[0] * shp[1], shp[2]), name="adamw_conv_w")
    out["conv_w"] = [r.reshape(shp) for r in res]

    dm_loc = lax.dynamic_slice_in_dim(dmod_all, me * ada_cols, ada_cols, axis=2)
    g_ada = ada_grad(c_all, jnp.transpose(dm_loc, (1, 0, 2)))
    two_d = lambda a: a.reshape(nl * D, ada_cols)
    res = adamw(two_d(w_ada), two_d(m_w_ada), two_d(v_w_ada), g=two_d(g_ada), name="adamw_w_ada")
    out["w_ada"] = [r.reshape(w_ada.shape) for r in res]

    shapes = {n: wts[n].shape for n in PACKED}
    res = adamw(_pack(wts), _pack(mom), _pack(var), g=small_all.reshape(-1, 128), name="adamw_small")
    unpacked = [_unpack(r, shapes) for r in res]
    for n in PACKED:
        out[n] = [u[n] for u in unpacked]

    loss = lax.psum(loss_tile[0, 0], ("x", "y", "c"))
    return (loss, gx[None], *[out[n][0] for n in WEIGHTS], *[out[n][1] for n in WEIGHTS],
            *[out[n][2] for n in WEIGHTS], *[out[n][3] for n in WEIGHTS])
```

```python
import functools
import math

import jax
import jax.numpy as jnp
from jax import lax
from jax.experimental import pallas as pl
from jax.experimental.pallas import tpu as pltpu

F32 = jnp.float32
BF16 = jnp.bfloat16
MESH_IDS = pl.DeviceIdType.MESH

D = 1024
SEQ = 4096
DEPTH = 4
NDEV = 8
HD = 64
GM_W = 256
GM_H = 4
GM_C = 128
S5_W = 256
S5_G = 16
S5_H = 16
S5_P = 64
S5_N = S5_G * S5_P
FX_W = 512
FX_H = 8
D_IN = 2 * GM_W + S5_W + 3 * FX_W + FX_H
NP = 2432
FF_COL = 2304
DFF = 2816
LN_EPS = 1e-5
DN_ALPHA = (2.0 * DEPTH) ** 0.25
NEG_INF = -1e30
ADAM_LR = 0.001
ADAM_B1 = 0.9
ADAM_B2 = 0.999
ADAM_EPS = 1e-08
ADAM_WD = 0.01
ADAM_STEP = 10

V7X_VMEM_LIMIT = 56 * 1024 * 1024
TS = 512
TS_C = 256
T_S5 = 256
TQ = 512
ATTN_HEADS = 2


def _call(body, **kw):
    return pl.pallas_call(body, **kw)


def _params(*sem):
    return pltpu.CompilerParams(dimension_semantics=sem if sem else None,
                                vmem_limit_bytes=V7X_VMEM_LIMIT)


def _nn(a, b):
    return jnp.dot(a.astype(BF16), b.astype(BF16), preferred_element_type=F32)


def _nt(a, b):
    return lax.dot_general(a.astype(BF16), b.astype(BF16), (((1,), (1,)), ((), ())),
                           preferred_element_type=F32)


def _tn(a, b):
    return lax.dot_general(a.astype(BF16), b.astype(BF16), (((0,), (0,)), ((), ())),
                           preferred_element_type=F32)


@jax.custom_vjp
def _bdot(a, b):
    return _nn(a, b)


def _bdot_fwd(a, b):
    return _nn(a, b), (a, b)


def _bdot_bwd(res, g):
    a, b = res
    return _nt(g, b), _tn(a, g)


_bdot.defvjp(_bdot_fwd, _bdot_bwd)


@jax.custom_vjp
def _bdot_nt(a, b):
    return _nt(a, b)


def _bdot_nt_fwd(a, b):
    return _nt(a, b), (a, b)


def _bdot_nt_bwd(res, g):
    a, b = res
    return _nn(g, b), _tn(g, a)


_bdot_nt.defvjp(_bdot_nt_fwd, _bdot_nt_bwd)


def _ln(r, g, b):
    mu = jnp.mean(r, axis=-1, keepdims=True)
    xc = r - mu
    var = jnp.mean(xc * xc, axis=-1, keepdims=True)
    return xc * lax.rsqrt(var + LN_EPS) * g + b


def _rows(shape):
    return lax.broadcasted_iota(jnp.int32, shape, 0)


def _lanes(shape):
    return lax.broadcasted_iota(jnp.int32, shape, 1)


def mm_nn(a, w, tn, name, mod=None, rows=None, out_dtype=F32):
    s, k = a.shape
    n = w.shape[1]
    ts = min(TS, s)

    def body(*refs):
        if mod is None:
            a_ref, w_ref, o_ref = refs
            h = a_ref[...]
        else:
            a_ref, m_ref, w_ref, o_ref = refs
            h = a_ref[...] * (1.0 + m_ref[rows[1]:rows[1] + 1, :]) + m_ref[rows[0]:rows[0] + 1, :]
        o_ref[...] = jnp.dot(h.astype(BF16), w_ref[...], preferred_element_type=F32).astype(out_dtype)

    in_specs = [pl.BlockSpec((ts, k), lambda j, i: (i, 0))]
    args = [a]
    if mod is not None:
        in_specs.append(pl.BlockSpec((8, k), lambda j, i: (0, 0)))
        args.append(mod)
    in_specs.append(pl.BlockSpec((k, tn), lambda j, i: (0, j)))
    args.append(w)
    return _call(body, name=name, grid=(n // tn, s // ts), in_specs=in_specs,
                 out_specs=pl.BlockSpec((ts, tn), lambda j, i: (i, j)),
                 out_shape=jax.ShapeDtypeStruct((s, n), out_dtype),
                 compiler_params=_params("arbitrary", "arbitrary"))(*args)


def mm_nt(dy, w, name):
    s, n = dy.shape
    k = w.shape[0]
    ts = min(TS, s)

    def body(dy_ref, w_ref, o_ref):
        o_ref[...] = _nt(dy_ref[...], w_ref[...])

    return _call(body, name=name, grid=(s // ts,),
                 in_specs=[pl.BlockSpec((ts, n), lambda i: (i, 0)),
                           pl.BlockSpec((k, n), lambda i: (0, 0))],
                 out_specs=pl.BlockSpec((ts, k), lambda i: (i, 0)),
                 out_shape=jax.ShapeDtypeStruct((s, k), F32),
                 compiler_params=_params("arbitrary"))(dy, w)


def mm_nt_mod(dy, w, x, dres, mod, rows, name):
    s, n = dy.shape
    k = w.shape[0]
    ts = min(TS, s)

    def body(dy_ref, w_ref, x_ref, r_ref, m_ref, dx_ref, dsh_ref, dsc_ref):
        @pl.when(pl.program_id(0) == 0)
        def _():
            dsh_ref[...] = jnp.zeros_like(dsh_ref)
            dsc_ref[...] = jnp.zeros_like(dsc_ref)

        dh = _nt(dy_ref[...], w_ref[...])
        dx_ref[...] = r_ref[...] + dh * (1.0 + m_ref[rows[1]:rows[1] + 1, :])
        dsh_ref[...] += jnp.sum(dh, axis=0, keepdims=True)
        dsc_ref[...] += jnp.sum(dh * x_ref[...], axis=0, keepdims=True)

    row = pl.BlockSpec((1, k), lambda i: (0, 0))
    tile = pl.BlockSpec((ts, k), lambda i: (i, 0))
    return _call(body, name=name, grid=(s // ts,),
                 in_specs=[pl.BlockSpec((ts, n), lambda i: (i, 0)),
                           pl.BlockSpec((k, n), lambda i: (0, 0)), tile, tile,
                           pl.BlockSpec((8, k), lambda i: (0, 0))],
                 out_specs=[tile, row, row],
                 out_shape=[jax.ShapeDtypeStruct((s, k), F32),
                            jax.ShapeDtypeStruct((1, k), F32),
                            jax.ShapeDtypeStruct((1, k), F32)],
                 compiler_params=_params("arbitrary"))(dy, w, x, dres, mod)


def mm_tn(a, dy, tn, name, mod=None, rows=None):
    s, k = a.shape
    n = dy.shape[1]
    ts = min(TS, s)
    ns = s // ts

    def body(*refs):
        if mod is None:
            a_ref, dy_ref, o_ref, acc = refs
            h = a_ref[...]
        else:
            a_ref, m_ref, dy_ref, o_ref, acc = refs
            h = a_ref[...] * (1.0 + m_ref[rows[1]:rows[1] + 1, :]) + m_ref[rows[0]:rows[0] + 1, :]
        i = pl.program_id(1)

        @pl.when(i == 0)
        def _():
            acc[...] = jnp.zeros_like(acc)

        acc[...] += _tn(h, dy_ref[...])

        @pl.when(i == ns - 1)
        def _():
            o_ref[...] = acc[...].astype(BF16)

    in_specs = [pl.BlockSpec((ts, k), lambda j, i: (i, 0))]
    args = [a]
    if mod is not None:
        in_specs.append(pl.BlockSpec((8, k), lambda j, i: (0, 0)))
        args.append(mod)
    in_specs.append(pl.BlockSpec((ts, tn), lambda j, i: (i, j)))
    args.append(dy)
    return _call(body, name=name, grid=(n // tn, s // ts), in_specs=in_specs,
                 out_specs=pl.BlockSpec((k, tn), lambda j, i: (0, j)),
                 out_shape=jax.ShapeDtypeStruct((k, n), BF16),
                 scratch_shapes=[pltpu.VMEM((k, tn), F32)],
                 compiler_params=_params("arbitrary", "arbitrary"))(*args)


def _post_fn(x, br, gate, lg, lb):
    return _ln(DN_ALPHA * x + (1.0 + gate) * br, lg, lb)


def post_fwd(x, br, mod, grow, lg, lb, name):
    s = x.shape[0]
    ts = min(TS, s)

    def body(x_ref, b_ref, m_ref, lg_ref, lb_ref, o_ref):
        o_ref[...] = _post_fn(x_ref[...], b_ref[...], m_ref[grow:grow + 1, :], lg_ref[...], lb_ref[...])

    tile = pl.BlockSpec((ts, D), lambda i: (i, 0))
    row = pl.BlockSpec((1, D), lambda i: (0, 0))
    return _call(body, name=name, grid=(s // ts,),
                 in_specs=[tile, tile, pl.BlockSpec((8, D), lambda i: (0, 0)), row, row],
                 out_specs=tile, out_shape=jax.ShapeDtypeStruct((s, D), F32),
                 compiler_params=_params("arbitrary"))(x, br, mod, lg, lb)


def post_bwd(x, br, mod, grow, lg, lb, dy, name):
    s = x.shape[0]
    ts = min(TS, s)

    def body(x_ref, b_ref, m_ref, lg_ref, lb_ref, dy_ref, dx_ref, db_ref, dg_ref, dlg_ref, dlb_ref):
        @pl.when(pl.program_id(0) == 0)
        def _():
            dg_ref[...] = jnp.zeros_like(dg_ref)
            dlg_ref[...] = jnp.zeros_like(dlg_ref)
            dlb_ref[...] = jnp.zeros_like(dlb_ref)

        _, vjp = jax.vjp(_post_fn, x_ref[...], b_ref[...], m_ref[grow:grow + 1, :], lg_ref[...], lb_ref[...])
        dx, db, dg, dlg, dlb = vjp(dy_ref[...])
        dx_ref[...] = dx
        db_ref[...] = db.astype(BF16)
        dg_ref[...] += dg
        dlg_ref[...] += dlg
        dlb_ref[...] += dlb

    tile = pl.BlockSpec((ts, D), lambda i: (i, 0))
    row = pl.BlockSpec((1, D), lambda i: (0, 0))
    rs = jax.ShapeDtypeStruct((1, D), F32)
    return _call(body, name=name, grid=(s // ts,),
                 in_specs=[tile, tile, pl.BlockSpec((8, D), lambda i: (0, 0)), row, row, tile],
                 out_specs=[tile, tile, row, row, row],
                 out_shape=[jax.ShapeDtypeStruct((s, D), F32), jax.ShapeDtypeStruct((s, D), BF16), rs, rs, rs],
                 compiler_params=_params("arbitrary"))(x, br, mod, lg, lb, dy)


def loss_kernel(y, target):
    s = y.shape[0]
    ts = min(TS, s)

    def body(y_ref, t_ref, l_ref, dy_ref):
        @pl.when(pl.program_id(0) == 0)
        def _():
            l_ref[...] = jnp.zeros_like(l_ref)

        err = y_ref[...] - t_ref[...]
        dy_ref[...] = err * (1.0 / D)
        per_tok = jnp.mean(err * err, axis=-1, keepdims=True)
        l_ref[...] += 0.5 * jnp.sum(per_tok)

    tile = pl.BlockSpec((ts, D), lambda i: (i, 0))
    return _call(body, name="loss", grid=(s // ts,), in_specs=[tile, tile],
                 out_specs=[pl.BlockSpec((8, 128), lambda i: (0, 0)), tile],
                 out_shape=[jax.ShapeDtypeStruct((8, 128), F32), jax.ShapeDtypeStruct((s, D), F32)],
                 compiler_params=_params("arbitrary"))(y, target)


def _gm_head(u, v, lg, lb, w, bs):
    t = u.shape[0]
    causal = _rows((GM_C, GM_C)) >= _lanes((GM_C, GM_C))
    vn = _ln(v, lg, lb)
    wm = jnp.where(causal, w, 0.0)
    chunks = []
    for n in range(t // GM_C):
        rs = slice(n * GM_C, (n + 1) * GM_C)
        chunks.append(u[rs] * (_bdot(wm, vn[rs]) + bs))
    return jnp.concatenate(chunks, axis=0)


def gm_fwd(p, lg, lb, ws, bst):
    s = p.shape[0]
    ts = min(TS_C, s)

    def body(u_ref, v_ref, lg_ref, lb_ref, ws_ref, bs_ref, o_ref):
        for h in range(GM_H):
            sl = slice(h * HD, (h + 1) * HD)
            o_ref[:, sl] = _gm_head(u_ref[:, sl], v_ref[:, sl], lg_ref[:, sl], lb_ref[:, sl], ws_ref[h],
                                    bs_ref[:, h:h + 1]).astype(BF16)

    full = lambda shape: pl.BlockSpec(shape, lambda i: (0,) * len(shape))
    return _call(body, name="gm_fwd", grid=(s // ts,),
                 in_specs=[pl.BlockSpec((ts, GM_W), lambda i: (i, 0)), pl.BlockSpec((ts, GM_W), lambda i: (i, 1)),
                           full((1, GM_W)), full((1, GM_W)), full((GM_H, GM_C, GM_C)), full((GM_C, GM_H))],
                 out_specs=pl.BlockSpec((ts, GM_W), lambda i: (i, 0)),
                 out_shape=jax.ShapeDtypeStruct((s, GM_W), BF16),
                 compiler_params=_params("arbitrary"))(p, p, lg, lb, ws, bst)


def gm_bwd(p, lg, lb, ws, bst, dmix):
    s = p.shape[0]
    ts = min(TS_C, s)

    def body(u_ref, v_ref, lg_ref, lb_ref, ws_ref, bs_ref, dy_ref, duv_ref, dlg_ref, dlb_ref, dws_ref, dbs_ref):
        @pl.when(pl.program_id(0) == 0)
        def _():
            dlg_ref[...] = jnp.zeros_like(dlg_ref)
            dlb_ref[...] = jnp.zeros_like(dlb_ref)
            dws_ref[...] = jnp.zeros_like(dws_ref)
            dbs_ref[...] = jnp.zeros_like(dbs_ref)

        for h in range(GM_H):
            sl = slice(h * HD, (h + 1) * HD)
            _, vjp = jax.vjp(_gm_head, u_ref[:, sl], v_ref[:, sl], lg_ref[:, sl], lb_ref[:, sl], ws_ref[h],
                             bs_ref[:, h:h + 1])
            du, dv, dlg, dlb, dw, dbs = vjp(dy_ref[:, sl])
            duv_ref[:, sl] = du.astype(BF16)
            duv_ref[:, GM_W + h * HD:GM_W + (h + 1) * HD] = dv.astype(BF16)
            dlg_ref[:, sl] += dlg
            dlb_ref[:, sl] += dlb
            dws_ref[h] += dw
            dbs_ref[:, h:h + 1] += dbs

    full = lambda shape: pl.BlockSpec(shape, lambda i: (0,) * len(shape))
    return _call(body, name="gm_bwd", grid=(s // ts,),
                 in_specs=[pl.BlockSpec((ts, GM_W), lambda i: (i, 0)), pl.BlockSpec((ts, GM_W), lambda i: (i, 1)),
                           full((1, GM_W)), full((1, GM_W)), full((GM_H, GM_C, GM_C)), full((GM_C, GM_H)),
                           pl.BlockSpec((ts, GM_W), lambda i: (i, 0))],
                 out_specs=[pl.BlockSpec((ts, 2 * GM_W), lambda i: (i, 0)), full((1, GM_W)), full((1, GM_W)),
                            full((GM_H, GM_C, GM_C)), full((GM_C, GM_H))],
                 out_shape=[jax.ShapeDtypeStruct((s, 2 * GM_W), BF16), jax.ShapeDtypeStruct((1, GM_W), F32),
                            jax.ShapeDtypeStruct((1, GM_W), F32), jax.ShapeDtypeStruct((GM_H, GM_C, GM_C), F32),
                            jax.ShapeDtypeStruct((GM_C, GM_H), F32)],
                 compiler_params=_params("arbitrary"))(p, p, lg, lb, ws, bst, dmix)


def _s5_prep_fn(lr, li, ldt, bt):
    dt = jnp.exp(ldt)
    er = jnp.exp(lr * dt)
    ar = er * jnp.cos(li * dt)
    ai = er * jnp.sin(li * dt)
    den = lr * lr + li * li
    nr = ar - 1.0
    cr = (nr * lr + ai * li) / den
    ci = (ai * lr - nr * li) / den
    br, bi = bt[:, :S5_N], bt[:, S5_N:]
    return ar, ai, jnp.concatenate([cr * br - ci * bi, cr * bi + ci * br], axis=1)


def s5_prep_fwd(lr, li, ldt, bt):
    def body(lr_ref, li_ref, ldt_ref, bt_ref, a_ref, bb_ref):
        ar, ai, bb = _s5_prep_fn(lr_ref[...], li_ref[...], ldt_ref[...], bt_ref[...])
        a_ref[...] = jnp.concatenate([ar, ai, jnp.zeros((6, S5_N), F32)], axis=0)
        bb_ref[...] = bb

    return _call(body, name="s5_prep_fwd",
                 out_shape=[jax.ShapeDtypeStruct((8, S5_N), F32), jax.ShapeDtypeStruct((S5_H, 2 * S5_N), F32)])(lr, li, ldt, bt)


def s5_prep_bwd(lr, li, ldt, bt, da, dbb):
    def body(lr_ref, li_ref, ldt_ref, bt_ref, da_ref, dbb_ref, dlr_ref, dli_ref, dldt_ref, dbt_ref):
        _, vjp = jax.vjp(_s5_prep_fn, lr_ref[...], li_ref[...], ldt_ref[...], bt_ref[...])
        dlr, dli, dldt, dbt = vjp((da_ref[0:1, :], da_ref[1:2, :], dbb_ref[...]))
        dlr_ref[...] = dlr
        dli_ref[...] = dli
        dbt_ref[...] = dbt
        group = (_rows((S5_N, 128)) // S5_P == _lanes((S5_N, 128))).astype(F32)
        dldt_ref[...] = jnp.dot(jnp.broadcast_to(dldt, (8, S5_N)), group, precision=lax.Precision.HIGHEST,
                                preferred_element_type=F32)[0:1, :]

    r = jax.ShapeDtypeStruct((1, S5_N), F32)
    return _call(body, name="s5_prep_bwd",
                 out_shape=[r, r, jax.ShapeDtypeStruct((1, 128), F32),
                            jax.ShapeDtypeStruct((S5_H, 2 * S5_N), F32)])(lr, li, ldt, bt, da, dbb)


def _s5_out_fn(x, u, cbd, drow, wg, bg):
    y = _bdot_nt(x[:, :S5_N], cbd[:, :S5_N]) - _bdot_nt(x[:, S5_N:], cbd[:, S5_N:]) + drow * u
    y = jax.nn.gelu(y)
    gate = _bdot_nt(y, wg) + bg
    return y * jax.nn.sigmoid(gate)


def _scan_chunk(buf, ar, ai, cr, ci, reverse):
    t = buf.shape[0]

    def local(xr, xi, rows):
        within = _rows(xr.shape) % 8
        pr, pi = ar, ai
        for d in (1, 2, 4):
            keep = within < 8 - d if reverse else within >= d
            shift = rows - d if reverse else d
            sr = jnp.where(keep, pltpu.roll(xr, shift, 0), 0.0)
            si = jnp.where(keep, pltpu.roll(xi, shift, 0), 0.0)
            xr, xi = xr + pr * sr - pi * si, xi + pr * si + pi * sr
            pr, pi = pr * pr - pi * pi, 2.0 * pr * pi
        return xr, xi

    xr, xi = local(buf[:, :S5_N], buf[:, S5_N:], t)
    buf[:, :S5_N] = xr
    buf[:, S5_N:] = xi
    edge = _rows((8, S5_N)) == (7 if reverse else 0)
    pr8, pi8 = local(jnp.where(edge, ar, 0.0), jnp.where(edge, ai, 0.0), 8)

    def group(j, c):
        g = t // 8 - 1 - j if reverse else j
        rows = pl.ds(pl.multiple_of(g * 8, 8), 8)
        gr = buf[rows, :S5_N] + pr8 * c[0] - pi8 * c[1]
        gi = buf[rows, S5_N:] + pr8 * c[1] + pi8 * c[0]
        buf[rows, :S5_N] = gr
        buf[rows, S5_N:] = gi
        return (gr[0:1, :], gi[0:1, :]) if reverse else (gr[7:8, :], gi[7:8, :])

    return lax.fori_loop(0, t // 8, group, (cr, ci), unroll=4)


def s5_fwd(p, arow, bbd, cbd, drow, wg, bg):
    s = p.shape[0]
    t = min(T_S5, s)

    def body(u_ref, a_ref, bbd_ref, cbd_ref, d_ref, wg_ref, bg_ref, y_ref, st_ref, carry):
        @pl.when(pl.program_id(0) == 0)
        def _():
            carry[...] = jnp.zeros_like(carry)

        u = u_ref[...]
        st_ref[...] = _nn(u, bbd_ref[...])
        cr, ci = _scan_chunk(st_ref, a_ref[0:1, :], a_ref[1:2, :], carry[0:1, :S5_N], carry[0:1, S5_N:], False)
        carry[0:1, :S5_N] = cr
        carry[0:1, S5_N:] = ci
        y_ref[...] = _s5_out_fn(st_ref[...], u, cbd_ref[...], d_ref[...], wg_ref[...], bg_ref[...]).astype(BF16)

    full = lambda shape: pl.BlockSpec(shape, lambda i: (0,) * len(shape))
    return _call(body, name="s5_fwd", grid=(s // t,),
                 in_specs=[pl.BlockSpec((t, S5_W), lambda i: (i, 2)), full((8, S5_N)), full((S5_W, 2 * S5_N)),
                           full((S5_W, 2 * S5_N)), full((1, S5_W)), full((S5_W, S5_W)), full((1, S5_W))],
                 out_specs=[pl.BlockSpec((t, S5_W), lambda i: (i, 0)), pl.BlockSpec((t, 2 * S5_N), lambda i: (i, 0))],
                 out_shape=[jax.ShapeDtypeStruct((s, S5_W), BF16), jax.ShapeDtypeStruct((s, 2 * S5_N), F32)],
                 scratch_shapes=[pltpu.VMEM((8, 2 * S5_N), F32)],
                 compiler_params=_params("arbitrary"))(p, arow, bbd, cbd, drow, wg, bg)


def s5_bwd(p, st, arow, bbd, cbd, drow, wg, bg, dmix):
    s = p.shape[0]
    t = min(T_S5, s)
    nc = s // t

    def body(u_ref, st_ref, prev_ref, a_ref, bbd_ref, cbd_ref, d_ref, wg_ref, bg_ref, dy_ref,
             du_ref, da_ref, dbbd_ref, dcbd_ref, dd_ref, dwg_ref, dbg_ref, carry, gbuf):
        i = pl.program_id(0)

        @pl.when(i == 0)
        def _():
            carry[...] = jnp.zeros_like(carry)
            for r in (da_ref, dbbd_ref, dcbd_ref, dd_ref, dwg_ref, dbg_ref):
                r[...] = jnp.zeros_like(r)

        u = u_ref[...]
        x = st_ref[...]
        _, vjp = jax.vjp(_s5_out_fn, x, u, cbd_ref[...], d_ref[...], wg_ref[...], bg_ref[...])
        dx, du1, dcbd, dd, dwg, dbg = vjp(dy_ref[...])
        gbuf[...] = dx
        cr, ci = _scan_chunk(gbuf, a_ref[0:1, :], -a_ref[1:2, :], carry[0:1, :S5_N], carry[0:1, S5_N:], True)
        carry[0:1, :S5_N] = cr
        carry[0:1, S5_N:] = ci
        gr, gi = gbuf[:, :S5_N], gbuf[:, S5_N:]
        rid = _rows((t, S5_N))
        has_prev = (i < nc - 1).astype(F32)
        top_r = prev_ref[7:8, :S5_N] * has_prev
        top_i = prev_ref[7:8, S5_N:] * has_prev
        xpr = jnp.where(rid == 0, top_r, pltpu.roll(x[:, :S5_N], 1, 0))
        xpi = jnp.where(rid == 0, top_i, pltpu.roll(x[:, S5_N:], 1, 0))
        da_ref[0:1, :] += jnp.sum(xpr * gr + xpi * gi, axis=0, keepdims=True)
        da_ref[1:2, :] += jnp.sum(xpr * gi - xpi * gr, axis=0, keepdims=True)
        g = jnp.concatenate([gr, gi], axis=1)
        dbbd_ref[...] += _tn(u, g)
        du_ref[...] = (_nt(g, bbd_ref[...]) + du1).astype(BF16)
        dcbd_ref[...] += dcbd
        dd_ref[...] += dd
        dwg_ref[...] += dwg
        dbg_ref[...] += dbg

    full = lambda shape: pl.BlockSpec(shape, lambda i: (0,) * len(shape))
    rev = lambda col: (lambda i: (nc - 1 - i, col))
    prev_map = lambda i: (jnp.maximum((nc - 1 - i) * (t // 8) - 1, 0), 0)
    return _call(body, name="s5_bwd", grid=(nc,),
                 in_specs=[pl.BlockSpec((t, S5_W), rev(2)), pl.BlockSpec((t, 2 * S5_N), rev(0)),
                           pl.BlockSpec((8, 2 * S5_N), prev_map), full((8, S5_N)), full((S5_W, 2 * S5_N)),
                           full((S5_W, 2 * S5_N)), full((1, S5_W)), full((S5_W, S5_W)), full((1, S5_W)),
                           pl.BlockSpec((t, S5_W), rev(1))],
                 out_specs=[pl.BlockSpec((t, S5_W), rev(0)), full((8, S5_N)), full((S5_W, 2 * S5_N)),
                            full((S5_W, 2 * S5_N)), full((1, S5_W)), full((S5_W, S5_W)), full((1, S5_W))],
                 out_shape=[jax.ShapeDtypeStruct((s, S5_W), BF16), jax.ShapeDtypeStruct((8, S5_N), F32),
                            jax.ShapeDtypeStruct((S5_W, 2 * S5_N), F32), jax.ShapeDtypeStruct((S5_W, 2 * S5_N), F32),
                            jax.ShapeDtypeStruct((1, S5_W), F32), jax.ShapeDtypeStruct((S5_W, S5_W), F32),
                            jax.ShapeDtypeStruct((1, S5_W), F32)],
                 scratch_shapes=[pltpu.VMEM((8, 2 * S5_N), F32), pltpu.VMEM((t, 2 * S5_N), F32)],
                 compiler_params=_params("arbitrary"))(p, st, st, arow, bbd, cbd, drow, wg, bg, dmix)


def _cum_steps(s):
    return int(math.ceil(math.log2(s)))


V_BLK = (2 * GM_W + S5_W + 2 * FX_W) // 128


AUG = 2 * HD
BIAS_COL = HD
FQ_COL = HD + 3
PAIR_W = 256


def _split3(f):
    hi = f.astype(BF16).astype(F32)
    r = f - hi
    mid = r.astype(BF16).astype(F32)
    lo = (r - mid).astype(BF16).astype(F32)
    return hi, mid, lo


def fox_prep(p, bf):
    s = p.shape[0]
    ts = min(TS, s)
    scale = HD ** -0.5

    def body(q0_ref, q1_ref, k0_ref, k1_ref, v0_ref, v1_ref, f_ref, bf_ref,
             qa_ref, ka_ref, qat_ref, kat_ref, vt_ref, carry):
        @pl.when(pl.program_id(0) == 0)
        def _():
            carry[...] = jnp.zeros_like(carry)

        lane = _lanes((ts, 128))
        lf = jax.nn.log_sigmoid(f_ref[...] + bf_ref[...])
        acc = jnp.where(lane < FX_H, lf, 0.0)
        rid = _rows((ts, 128))
        for k in range(_cum_steps(ts)):
            d = 1 << k
            acc = acc + jnp.where(rid >= d, pltpu.roll(acc, d, 0), 0.0)
        acc = acc + carry[0:1, :]
        carry[0:1, :] = acc[ts - 1:ts, :]

        low = lane < HD
        for h in range(FX_H):
            blk, pos = divmod(h, 4)
            pair = slice((pos // 2) * 128, (pos // 2) * 128 + 128)
            hi, mid, lo = _split3(acc[:, h:h + 1])
            one = jnp.ones((ts, 1), F32)

            def augment(ref, cols):
                x = ref[:, pair]
                if pos % 2:
                    x = pltpu.roll(x, HD, 1)
                out = jnp.where(low, x, 0.0)
                for j, cval in enumerate(cols):
                    out = jnp.where(lane == HD + j, cval, out)
                return out

            qa = augment((q0_ref, q1_ref)[blk], (one, one, one, hi, mid, lo))
            qa = jnp.where(low, qa * scale, qa)
            ka = augment((k0_ref, k1_ref)[blk], (-hi, -mid, -lo, one, one, one))
            cs = slice(h * AUG, (h + 1) * AUG)
            qa_ref[:, cs] = qa.astype(BF16)
            ka_ref[:, cs] = ka.astype(BF16)
            qat_ref[cs, :] = jnp.transpose(qa).astype(BF16)
            kat_ref[cs, :] = jnp.transpose(ka).astype(BF16)
        for j in range(FX_H // 2):
            vref = (v0_ref, v1_ref)[j // 2]
            vt_ref[j * 128:(j + 1) * 128, :] = jnp.transpose(vref[:, (j % 2) * 128:(j % 2) * 128 + 128]).astype(BF16)

    q_blk = (2 * GM_W + S5_W) // PAIR_W
    col = lambda b: pl.BlockSpec((ts, PAIR_W), lambda i: (i, b))
    wide = FX_H * AUG
    return _call(body, name="fox_prep", grid=(s // ts,),
                 in_specs=[col(q_blk), col(q_blk + 1), col(q_blk + 2), col(q_blk + 3), col(q_blk + 4), col(q_blk + 5),
                           pl.BlockSpec((ts, 128), lambda i: (i, FF_COL // 128)), pl.BlockSpec((1, 128), lambda i: (0, 0))],
                 out_specs=[pl.BlockSpec((ts, wide), lambda i: (i, 0)), pl.BlockSpec((ts, wide), lambda i: (i, 0)),
                            pl.BlockSpec((wide, ts), lambda i: (0, i)), pl.BlockSpec((wide, ts), lambda i: (0, i)),
                            pl.BlockSpec((FX_W, ts), lambda i: (0, i))],
                 out_shape=[jax.ShapeDtypeStruct((s, wide), BF16), jax.ShapeDtypeStruct((s, wide), BF16),
                            jax.ShapeDtypeStruct((wide, s), BF16), jax.ShapeDtypeStruct((wide, s), BF16),
                            jax.ShapeDtypeStruct((FX_W, s), BF16)],
                 scratch_shapes=[pltpu.VMEM((8, 128), F32)],
                 compiler_params=_params("arbitrary"))(p, p, p, p, p, p, p, bf)


def fox_prep_grad(p, bf, dfq, dfk):
    s = p.shape[0]
    ts = min(TS, s)
    ns = s // ts

    def body(f_ref, bf_ref, dfq_ref, dfk_ref, df_ref, dbf_ref, carry):
        @pl.when(pl.program_id(0) == 0)
        def _():
            carry[...] = jnp.zeros_like(carry)
            dbf_ref[...] = jnp.zeros_like(dbf_ref)

        lane = _lanes((ts, 128))
        acc = jnp.zeros((ts, 128), F32)
        for h in range(FX_H):
            c = (h // 2) * 128 + h % 2
            acc = jnp.where(lane == h, dfq_ref[:, c:c + 1] + dfk_ref[:, c:c + 1], acc)
        rid = _rows((ts, 128))
        for k in range(_cum_steps(ts)):
            d = 1 << k
            acc = acc + jnp.where(rid < ts - d, pltpu.roll(acc, ts - d, 0), 0.0)
        acc = acc + carry[0:1, :]
        carry[0:1, :] = acc[0:1, :]
        z = f_ref[...] + bf_ref[...]
        df = jnp.where(lane < FX_H, acc * jax.nn.sigmoid(-z), 0.0)
        df_ref[...] = df.astype(BF16)
        dbf_ref[...] += jnp.sum(df, axis=0, keepdims=True)

    rev = lambda i: (ns - 1 - i, 0)
    return _call(body, name="fox_prep_grad", grid=(ns,),
                 in_specs=[pl.BlockSpec((ts, 128), lambda i: (ns - 1 - i, FF_COL // 128)),
                           pl.BlockSpec((1, 128), lambda i: (0, 0)),
                           pl.BlockSpec((ts, FX_W), rev), pl.BlockSpec((ts, FX_W), rev)],
                 out_specs=[pl.BlockSpec((ts, 128), rev), pl.BlockSpec((1, 128), lambda i: (0, 0))],
                 out_shape=[jax.ShapeDtypeStruct((s, 128), BF16), jax.ShapeDtypeStruct((1, 128), F32)],
                 scratch_shapes=[pltpu.VMEM((8, 128), F32)],
                 compiler_params=_params("arbitrary"))(p, bf, dfq, dfk)


def attn(qat, ka, vt):
    s = ka.shape[0]
    tq = min(TQ, s)
    nq = s // tq

    nh = ATTN_HEADS

    def body(qat_ref, ka_ref, vt_ref, o_ref, lse_ref):
        qi = pl.program_id(1)
        causal = _rows((tq, tq)) <= _lanes((tq, tq))
        lse_ref[...] = jnp.zeros_like(lse_ref)

        def step(kj, carry, masked):
            off = pl.multiple_of(kj * tq, tq)
            out = []
            for hh in range(nh):
                m, l, acc = carry[hh]
                st = jnp.dot(ka_ref[pl.ds(off, tq), hh * AUG:(hh + 1) * AUG], qat_ref[hh * AUG:(hh + 1) * AUG, :],
                             preferred_element_type=F32)
                if masked:
                    st = jnp.where(causal, st, NEG_INF)
                m_new = jnp.maximum(m, jnp.max(st, axis=0, keepdims=True))
                alpha = jnp.exp(m - m_new)
                pt = jnp.exp(st - m_new)
                v = vt_ref[hh * HD:(hh + 1) * HD, pl.ds(off, tq)]
                out.append((m_new, alpha * l + jnp.sum(pt, axis=0, keepdims=True),
                            alpha * acc + jnp.dot(v, pt.astype(BF16), preferred_element_type=F32)))
            return tuple(out)

        init = tuple((jnp.full((1, tq), NEG_INF, F32), jnp.zeros((1, tq), F32), jnp.zeros((HD, tq), F32))
                     for _ in range(nh))
        carry = lax.fori_loop(0, qi, lambda kj, c: step(kj, c, False), init)
        carry = step(qi, carry, True)
        for hh in range(nh):
            m, l, _ = carry[hh]
            lse_ref[hh // 2, hh % 2:hh % 2 + 1, :] = m + jnp.log(l)
        for j in range(nh // 2):
            pair = jnp.concatenate([carry[2 * j][2] / carry[2 * j][1], carry[2 * j + 1][2] / carry[2 * j + 1][1]], axis=0)
            o_ref[:, j * 128:(j + 1) * 128] = jnp.transpose(pair).astype(BF16)

    return _call(body, name="attn", grid=(FX_H // nh, nq),
                 in_specs=[pl.BlockSpec((nh * AUG, tq), lambda h, i: (h, i)),
                           pl.BlockSpec((s, nh * AUG), lambda h, i: (0, h)),
                           pl.BlockSpec((nh * HD, s), lambda h, i: (h, 0))],
                 out_specs=[pl.BlockSpec((tq, nh * HD), lambda h, i: (i, h)),
                            pl.BlockSpec((nh // 2, 8, tq), lambda h, i: (h, 0, i))],
                 out_shape=[jax.ShapeDtypeStruct((s, FX_W), BF16), jax.ShapeDtypeStruct((FX_H // 2, 8, s), F32)],
                 compiler_params=_params("arbitrary", "arbitrary"))(qat, ka, vt)


def attn_grad(qa, qat, ka, kat, p, o, lse, dmix):
    s = qa.shape[0]
    tq = min(TQ, s)
    nq = s // tq
    scale = HD ** -0.5

    def body(qa_ref, qat_ref, ka_ref, kat_ref, v_ref, o_ref, lse_ref, do_ref,
             dq_ref, dk_ref, dv_ref, dfq_ref, dfk_ref, dot_scr, delta, dqt):
        kj = pl.program_id(1)
        lane = _lanes((tq, 128))
        low = lane < HD
        causal = _rows((tq, tq)) <= _lanes((tq, tq))

        @pl.when(kj == 0)
        def _():
            dqt[...] = jnp.zeros_like(dqt)
            delta[...] = jnp.zeros_like(delta)

            def prep(c, _):
                rows = pl.ds(pl.multiple_of(c * tq, tq), tq)
                do = do_ref[rows, :]
                pt = jnp.transpose(do * o_ref[rows, :].astype(F32))
                delta[0:1, rows] = jnp.sum(pt[:HD], axis=0, keepdims=True)
                delta[1:2, rows] = jnp.sum(pt[HD:], axis=0, keepdims=True)
                dot_scr[:, rows] = jnp.transpose(do).astype(BF16)
                return 0

            lax.fori_loop(0, nq, prep, 0)

        v = v_ref[...]
        vms = [jnp.where(low, v, 0.0).astype(BF16), jnp.where(low, 0.0, v).astype(BF16)]

        def tile(qi, carry, masked):
            cols = pl.ds(pl.multiple_of(qi * tq, tq), tq)
            do = do_ref[cols, :].astype(BF16)
            out = []
            for hh in range(2):
                cs = slice(hh * AUG, (hh + 1) * AUG)
                dka, dv = carry[hh]
                st = jnp.dot(ka_ref[:, cs], qat_ref[cs, cols], preferred_element_type=F32)
                if masked:
                    st = jnp.where(causal, st, NEG_INF)
                pt = jnp.exp(st - lse_ref[0, hh:hh + 1, cols])
                dv = dv + jnp.dot(pt.astype(BF16), do, preferred_element_type=F32)
                dpt = jnp.dot(vms[hh], dot_scr[:, cols], preferred_element_type=F32)
                dsb = (pt * (dpt - delta[hh:hh + 1, cols])).astype(BF16)
                dka = dka + jnp.dot(dsb, qa_ref[cols, cs], preferred_element_type=F32)
                dqt[hh, :, cols] += jnp.dot(kat_ref[cs, :], dsb, preferred_element_type=F32)
                out.append((dka, dv))
            return tuple(out)

        init = tuple((jnp.zeros((tq, AUG), F32), jnp.zeros((tq, 128), F32)) for _ in range(2))
        carry = tile(kj, init, True)
        carry = lax.fori_loop(kj + 1, nq, lambda qi, c: tile(qi, c, False), carry)
        dks = [carry[0][0], carry[1][0]]
        dvs = [carry[0][1], carry[1][1]]
        dv_ref[...] = jnp.where(low, dvs[0], dvs[1]).astype(BF16)
        dk_ref[...] = jnp.where(low, dks[0], pltpu.roll(dks[1], HD, 1)).astype(BF16)
        dfk_ref[...] = jnp.where(lane == 0, -dks[0][:, BIAS_COL:BIAS_COL + 1],
                                 jnp.where(lane == 1, -dks[1][:, BIAS_COL:BIAS_COL + 1], 0.0))

        @pl.when(kj == nq - 1)
        def _():
            def finish(c, _):
                rows = pl.ds(pl.multiple_of(c * tq, tq), tq)
                t0 = jnp.transpose(dqt[0, :, rows])
                t1 = jnp.transpose(dqt[1, :, rows])
                dq_ref[rows, :] = (jnp.where(low, t0, pltpu.roll(t1, HD, 1)) * scale).astype(BF16)
                dfq_ref[rows, :] = jnp.where(lane == 0, t0[:, FQ_COL:FQ_COL + 1],
                                             jnp.where(lane == 1, t1[:, FQ_COL:FQ_COL + 1], 0.0))
                return 0

            lax.fori_loop(0, nq, finish, 0)

    seq128 = lambda blk: pl.BlockSpec((s, 128), lambda h, j: (0, blk + h))
    tile128 = pl.BlockSpec((tq, 128), lambda h, j: (j, h))
    out_b = jax.ShapeDtypeStruct((s, FX_W), BF16)
    out_f = jax.ShapeDtypeStruct((s, FX_W), F32)
    return _call(body, name="attn_grad", grid=(FX_H // 2, nq),
                 in_specs=[pl.BlockSpec((s, 2 * AUG), lambda h, j: (0, h)), pl.BlockSpec((2 * AUG, s), lambda h, j: (h, 0)),
                           pl.BlockSpec((tq, 2 * AUG), lambda h, j: (j, h)), pl.BlockSpec((2 * AUG, tq), lambda h, j: (h, j)),
                           pl.BlockSpec((tq, 128), lambda h, j: (j, V_BLK + h)), seq128(0),
                           pl.BlockSpec((1, 8, s), lambda h, j: (h, 0, 0)), seq128(4)],
                 out_specs=[seq128(0), tile128, tile128, seq128(0), tile128],
                 out_shape=[out_b, out_b, out_b, out_f, out_f],
                 scratch_shapes=[pltpu.VMEM((128, s), BF16), pltpu.VMEM((8, s), F32), pltpu.VMEM((2, AUG, s), F32)],
                 compiler_params=_params("arbitrary", "arbitrary"))(qa, qat, ka, kat, p, o, lse, dmix)


def _shift_down(a, prev8, k):
    r = pltpu.roll(a, k, 0)
    top = jnp.where(_rows(prev8.shape) < k, pltpu.roll(prev8, k, 0), r[0:8])
    return jnp.concatenate([top, r[8:]], axis=0)


def _shift_up(a, next8, k):
    t = a.shape[0]
    r = pltpu.roll(a, t - k, 0)
    bot = jnp.where(_rows(next8.shape) >= 8 - k, pltpu.roll(next8, 8 - k, 0), r[t - 8:t])
    return jnp.concatenate([r[:t - 8], bot], axis=0)


def _conv(a, prev8, cw, cb):
    return cb + cw[0:1, :] * _shift_down(a, prev8, 2) + cw[1:2, :] * _shift_down(a, prev8, 1) + cw[2:3, :] * a


GELU_K0 = math.sqrt(2.0 / math.pi)
GELU_K1 = GELU_K0 * 0.044715


def _gelu_parts(c):
    c2 = c * c
    return c2, 0.5 + 0.5 * jnp.tanh(c * (GELU_K0 + GELU_K1 * c2))


def conv_fwd(up, cw, cb):
    s = up.shape[0]
    ts = min(TS_C, s)

    def body(a_ref, g_ref, cw_ref, cb_ref, o_ref, c_ref, halo):
        @pl.when(pl.program_id(0) == 0)
        def _():
            halo[...] = jnp.zeros_like(halo)

        a = a_ref[...]
        c = _conv(a, halo[...], cw_ref[...], cb_ref[...])
        _, h = _gelu_parts(c)
        c_ref[...] = c
        o_ref[...] = (c * h * g_ref[...]).astype(BF16)
        halo[...] = a[ts - 8:ts, :]

    tile = pl.BlockSpec((ts, DFF), lambda i: (i, 0))
    return _call(body, name="conv_fwd", grid=(s // ts,),
                 in_specs=[tile, pl.BlockSpec((ts, DFF), lambda i: (i, 1)),
                           pl.BlockSpec((3, DFF), lambda i: (0, 0)), pl.BlockSpec((1, DFF), lambda i: (0, 0))],
                 out_specs=[tile, tile],
                 out_shape=[jax.ShapeDtypeStruct((s, DFF), BF16), jax.ShapeDtypeStruct((s, DFF), F32)],
                 scratch_shapes=[pltpu.VMEM((8, DFF), F32)],
                 compiler_params=_params("arbitrary"))(up, up, cw, cb)


def conv_bwd(up, c, cw, dact):
    s = up.shape[0]
    ts = min(TS_C, s)
    ns = s // ts

    def body(a_ref, g_ref, c_ref, cw_ref, dact_ref, dup_ref, dcw_ref, dcb_ref, halo):
        @pl.when(pl.program_id(0) == 0)
        def _():
            halo[...] = jnp.zeros_like(halo)
            dcw_ref[...] = jnp.zeros_like(dcw_ref)
            dcb_ref[...] = jnp.zeros_like(dcb_ref)

        a = a_ref[...]
        cw = cw_ref[...]
        cv = c_ref[...]
        dact = dact_ref[...]
        c2, h = _gelu_parts(cv)
        dup_ref[:, DFF:] = (dact * (cv * h)).astype(BF16)
        dgel = h + cv * (2.0 * h * (1.0 - h)) * (GELU_K0 + 3.0 * GELU_K1 * c2)
        dc = dact * g_ref[...] * dgel
        up1 = _shift_up(dc, halo[...], 1)
        up2 = _shift_up(dc, halo[...], 2)
        dup_ref[:, :DFF] = (cw[2:3, :] * dc + cw[1:2, :] * up1 + cw[0:1, :] * up2).astype(BF16)
        dcw_ref[0:1, :] += jnp.sum(a * up2, axis=0, keepdims=True)
        dcw_ref[1:2, :] += jnp.sum(a * up1, axis=0, keepdims=True)
        dcw_ref[2:3, :] += jnp.sum(a * dc, axis=0, keepdims=True)
        dcb_ref[...] += jnp.sum(dc, axis=0, keepdims=True)
        halo[...] = dc[0:8, :]

    rev = lambda col: (lambda i: (ns - 1 - i, col))
    return _call(body, name="conv_bwd", grid=(ns,),
                 in_specs=[pl.BlockSpec((ts, DFF), rev(0)), pl.BlockSpec((ts, DFF), rev(1)),
                           pl.BlockSpec((ts, DFF), rev(0)), pl.BlockSpec((3, DFF), lambda i: (0, 0)),
                           pl.BlockSpec((ts, DFF), rev(0))],
                 out_specs=[pl.BlockSpec((ts, 2 * DFF), rev(0)), pl.BlockSpec((3, DFF), lambda i: (0, 0)),
                            pl.BlockSpec((1, DFF), lambda i: (0, 0))],
                 out_shape=[jax.ShapeDtypeStruct((s, 2 * DFF), BF16), jax.ShapeDtypeStruct((3, DFF), F32),
                            jax.ShapeDtypeStruct((1, DFF), F32)],
                 scratch_shapes=[pltpu.VMEM((8, DFF), F32)],
                 compiler_params=_params("arbitrary"))(up, up, c, cw, dact)


def _blockdiag_expand(m):
    m4 = m.reshape(S5_H, 2, S5_G, S5_P)
    eye = jnp.eye(S5_G, dtype=bool)[:, None, None, :, None]
    return jnp.where(eye, m4[None], 0.0).reshape(S5_W, 2 * S5_N)


def _blockdiag_extract(mbd):
    m5 = mbd.reshape(S5_G, S5_H, 2, S5_G, S5_P)
    diag = jnp.stack([m5[g, :, :, g, :] for g in range(S5_G)], axis=2)
    return diag.reshape(S5_H, 2 * S5_N)


def _c_expand(c_re, c_im):
    c4 = jnp.stack([c_re, c_im], axis=2)
    eye = jnp.eye(S5_G, dtype=bool)[:, None, None, :, None]
    return jnp.where(eye, c4[:, :, :, None, :], 0.0).reshape(S5_W, 2 * S5_N)


def _c_extract(cbd):
    m5 = cbd.reshape(S5_G, S5_H, 2, S5_G, S5_P)
    d = jnp.stack([m5[g, :, :, g, :] for g in range(S5_G)], axis=0)
    return d[:, :, 0, :], d[:, :, 1, :]


def _glu_expand(w):
    eye = jnp.eye(S5_G, dtype=bool)[:, None, :, None]
    return jnp.where(eye, w[:, :, None, :], 0.0).reshape(S5_W, S5_W)


def _glu_extract(wbd):
    m4 = wbd.reshape(S5_G, S5_H, S5_G, S5_H)
    return jnp.stack([m4[g, :, g, :] for g in range(S5_G)], axis=0)


SMALL = ("b_f", "gm_ln_g", "gm_ln_b", "gm_w_s", "gm_b_s", "s5_lam_re", "s5_lam_im", "s5_log_dt", "s5_b_re", "s5_b_im",
         "s5_c_re", "s5_c_im", "s5_d", "s5_w_glu", "s5_b_glu", "ln1_g", "ln1_b", "conv_b", "ln2_g", "ln2_b")


def _layer_operands(sp, l):
    f = {}
    f["bf"] = jnp.pad(sp["b_f"][l][None, :], ((0, 0), (0, 128 - FX_H)))
    f["gm_lg"] = sp["gm_ln_g"][l].reshape(1, GM_W)
    f["gm_lb"] = sp["gm_ln_b"][l].reshape(1, GM_W)
    f["gm_ws"] = sp["gm_w_s"][l]
    f["gm_bst"] = sp["gm_b_s"][l].T
    f["lr"] = sp["s5_lam_re"][l].reshape(1, S5_N)
    f["li"] = sp["s5_lam_im"][l].reshape(1, S5_N)
    f["ldt"] = jnp.repeat(sp["s5_log_dt"][l], S5_P).reshape(1, S5_N)
    bt = lambda b: jnp.transpose(b, (2, 0, 1)).reshape(S5_H, S5_N)
    f["bt"] = jnp.concatenate([bt(sp["s5_b_re"][l]), bt(sp["s5_b_im"][l])], axis=1)
    f["cbd"] = _c_expand(sp["s5_c_re"][l], sp["s5_c_im"][l])
    f["drow"] = sp["s5_d"][l].reshape(1, S5_W)
    f["wg"] = _glu_expand(sp["s5_w_glu"][l])
    f["bg"] = sp["s5_b_glu"][l].reshape(1, S5_W)
    for n in ("ln1_g", "ln1_b", "ln2_g", "ln2_b"):
        f[n] = sp[n][l][None, :]
    f["cb"] = sp["conv_b"][l][None, :]
    return f


def layer_fwd_mix(x, mod, w_in, f):
    p = mm_nn(x, w_in, NP, "in_proj", mod=mod, rows=(0, 1))
    ygm = gm_fwd(p, f["gm_lg"], f["gm_lb"], f["gm_ws"], f["gm_bst"])
    arow, bbt = s5_prep_fwd(f["lr"], f["li"], f["ldt"], f["bt"])
    bbd = _blockdiag_expand(bbt)
    ys5, st = s5_fwd(p, arow, bbd, f["cbd"], f["drow"], f["wg"], f["bg"])
    qa, ka, qat, kat, vt = fox_prep(p, f["bf"])
    yfx, lse = attn(qat, ka, vt)
    mixcat = jnp.concatenate([ygm, ys5, yfx], axis=1)
    return mixcat, dict(f=f, x=x, p=p, arow=arow, bbd=bbd, st=st, qa=qa, ka=ka, qat=qat, kat=kat, yfx=yfx, lse=lse,
                        mixcat=mixcat)


def layer_fwd_rest(x, mixcat, mod, w, f, saved):
    mix = mm_nn(mixcat, w["w_out"], D, "out_proj")
    x1 = post_fwd(x, mix, mod, 2, f["ln1_g"], f["ln1_b"], "post1_fwd")
    up = mm_nn(x1, w["w_up"], DFF // 2, "up_proj", mod=mod, rows=(3, 4))
    act, conv = conv_fwd(up, w["conv_w"], f["cb"])
    ffn = mm_nn(act, w["w_down"], D, "down_proj")
    x2 = post_fwd(x1, ffn, mod, 5, f["ln2_g"], f["ln2_b"], "post2_fwd")
    return x2, dict(saved, mix=mix, x1=x1, up=up, conv=conv, act=act, ffn=ffn)


def layer_fwd(x, mod, w, f):
    mixcat, saved = layer_fwd_mix(x, mod, w["w_in"], f)
    return layer_fwd_rest(x, mixcat, mod, w, f, saved)


def layer_bwd_ffn(dx, sv, mod, w):
    f = sv["f"]
    dx1, dffn, dg2, dlg2, dlb2 = post_bwd(sv["x1"], sv["ffn"], mod, 5, f["ln2_g"], f["ln2_b"], dx, "post2_bwd")
    g_down = mm_tn(sv["act"], dffn, D // 2, "down_dw")
    dact = mm_nt(dffn, w["w_down"], "down_dx")
    dup, dcw, dcb = conv_bwd(sv["up"], sv["conv"], w["conv_w"], dact)
    g_up = mm_tn(sv["x1"], dup, DFF // 2, "up_dw", mod=mod, rows=(3, 4))
    dx1, dsh2, dsc2 = mm_nt_mod(dup, w["w_up"], sv["x1"], dx1, mod, (3, 4), "up_dx")
    return dx1, dict(w_up=g_up, w_down=g_down, conv_w=dcw), dict(dsh2=dsh2, dsc2=dsc2, dg2=dg2, conv_b=dcb[0],
                                                                 ln2_g=dlg2[0], ln2_b=dlb2[0])


def layer_bwd_mix(dx1, sv, mod, w, part):
    f = sv["f"]
    dx0, dmix, dg1, dlg1, dlb1 = post_bwd(sv["x"], sv["mix"], mod, 2, f["ln1_g"], f["ln1_b"], dx1, "post1_bwd")
    g_out = mm_tn(sv["mixcat"], dmix, D, "out_dw")
    dmc = mm_nt(dmix, w["w_out"], "out_dx")
    duv, dgm_lg, dgm_lb, dgm_ws, dgm_bst = gm_bwd(sv["p"], f["gm_lg"], f["gm_lb"], f["gm_ws"], f["gm_bst"], dmc)
    du5, da, dbbd, dcbd, dd5, dwg, dbg = s5_bwd(sv["p"], sv["st"], sv["arow"], sv["bbd"], f["cbd"], f["drow"],
                                                f["wg"], f["bg"], dmc)
    dlr, dli, dldt, dbt = s5_prep_bwd(f["lr"], f["li"], f["ldt"], f["bt"], da, _blockdiag_extract(dbbd))
    dq, dk, dv, dfq, dfk = attn_grad(sv["qa"], sv["qat"], sv["ka"], sv["kat"], sv["p"], sv["yfx"], sv["lse"], dmc)
    dff, dbf = fox_prep_grad(sv["p"], f["bf"], dfq, dfk)
    dp = jnp.concatenate([duv, du5, dq, dk, dv, dff], axis=1)
    g_in = mm_tn(sv["x"], dp, NP, "in_dw", mod=mod, rows=(0, 1))
    dx, dsh1, dsc1 = mm_nt_mod(dp, w["w_in"], sv["x"], dx0, mod, (0, 1), "in_dx")

    dmod = jnp.concatenate([dsh1, dsc1, dg1, part["dsh2"], part["dsc2"], part["dg2"]], axis=0)
    dc_re, dc_im = _c_extract(dcbd)
    dbt4 = dbt.reshape(S5_H, 2, S5_G, S5_P)
    vals = dict(b_f=dbf[0, :FX_H], gm_ln_g=dgm_lg.reshape(GM_H, HD), gm_ln_b=dgm_lb.reshape(GM_H, HD),
                gm_w_s=dgm_ws, gm_b_s=dgm_bst.T, s5_lam_re=dlr.reshape(S5_G, S5_P),
                s5_lam_im=dli.reshape(S5_G, S5_P), s5_log_dt=dldt[0, :S5_G],
                s5_b_re=jnp.transpose(dbt4[:, 0], (1, 2, 0)), s5_b_im=jnp.transpose(dbt4[:, 1], (1, 2, 0)),
                s5_c_re=dc_re, s5_c_im=dc_im, s5_d=dd5.reshape(S5_G, S5_H), s5_w_glu=_glu_extract(dwg),
                s5_b_glu=dbg.reshape(S5_G, S5_H), ln1_g=dlg1[0], ln1_b=dlb1[0], conv_b=part["conv_b"],
                ln2_g=part["ln2_g"], ln2_b=part["ln2_b"])
    return dx, dict(w_in=g_in, w_out=g_out), vals, dmod


def layer_bwd(dx, sv, mod, w):
    dx1, g_ffn, part = layer_bwd_ffn(dx, sv, mod, w)
    dx, g_mix, vals, dmod = layer_bwd_mix(dx1, sv, mod, w, part)
    return dx, dict(g_ffn, **g_mix), vals, dmod


def local_step(x, target, mods, big, sp):
    saved = []
    for l in range(DEPTH):
        x, sv = layer_fwd(x, mods[l], big[l], _layer_operands(sp, l))
        saved.append(sv)
    loss_tile, dx = loss_kernel(x, target)
    gbig, vals, dmods = [None] * DEPTH, [None] * DEPTH, [None] * DEPTH
    for l in reversed(range(DEPTH)):
        dx, gbig[l], vals[l], dmods[l] = layer_bwd(dx, saved[l], mods[l], big[l])
    gsm = {n: jnp.stack([v[n] for v in vals]) for n in SMALL}
    return loss_tile, dx, gbig, gsm, jnp.stack(dmods)


def _my_index():
    return 4 * lax.axis_index("x") + 2 * lax.axis_index("y") + lax.axis_index("c")


def exchange(tensors, scatter, name):
    n = len(tensors)

    def body(*refs):
        ins, outs = refs[:n], refs[n:2 * n]
        send_sems, recv_sems, local_sems = refs[2 * n:]
        x, y, c = lax.axis_index("x"), lax.axis_index("y"), lax.axis_index("c")
        me = 4 * x + 2 * y + c
        local = []
        for t in range(n):
            cp = pltpu.make_async_copy(ins[t].at[me] if scatter else ins[t], outs[t].at[me], local_sems.at[t])
            cp.start()
            local.append(cp)
        remote = []
        for m in range(1, NDEV):
            px = 1 - x if m & 4 else x
            py = 1 - y if m & 2 else y
            pc = 1 - c if m & 1 else c
            peer = 4 * px + 2 * py + pc
            for t in range(n):
                k = t * (NDEV - 1) + m - 1
                cp = pltpu.make_async_remote_copy(
                    src_ref=ins[t].at[peer] if scatter else ins[t], dst_ref=outs[t].at[me],
                    send_sem=send_sems.at[k], recv_sem=recv_sems.at[k],
                    device_id=(px, py, pc), device_id_type=MESH_IDS)
                cp.start()
                remote.append(cp)
        for cp in remote:
            cp.wait()
        for cp in local:
            cp.wait()

    hbm = pl.BlockSpec(memory_space=pltpu.HBM)
    out_shape = [jax.ShapeDtypeStruct(t.shape if scatter else (NDEV,) + t.shape, t.dtype) for t in tensors]
    return _call(body, name=name, in_specs=[hbm] * n, out_specs=[hbm] * n, out_shape=out_shape,
                 scratch_shapes=[pltpu.SemaphoreType.DMA((n * (NDEV - 1),)), pltpu.SemaphoreType.DMA((n * (NDEV - 1),)),
                                 pltpu.SemaphoreType.DMA((n,))])(*tensors)


def _peers():
    x, y, c = lax.axis_index("x"), lax.axis_index("y"), lax.axis_index("c")
    out = []
    for m in range(1, NDEV):
        px = 1 - x if m & 4 else x
        py = 1 - y if m & 2 else y
        pc = 1 - c if m & 1 else c
        out.append(((px, py, pc), 4 * px + 2 * py + pc))
    return 4 * x + 2 * y + c, out


def _split_copies(v_refs, land_refs, send_sems, recv_sems, scatter):
    me, peers = _peers()
    return [pltpu.make_async_remote_copy(
        src_ref=v_ref.at[idx] if scatter else v_ref, dst_ref=land_ref.at[me],
        send_sem=send_sems.at[t * (NDEV - 1) + k], recv_sem=recv_sems.at[t * (NDEV - 1) + k],
        device_id=pos, device_id_type=MESH_IDS)
        for t, (v_ref, land_ref) in enumerate(zip(v_refs, land_refs)) for k, (pos, idx) in enumerate(peers)]


_HBM_SPEC = pl.BlockSpec(memory_space=pltpu.HBM)
_SEM_SPEC = pl.BlockSpec(memory_space=pltpu.SEMAPHORE)
_SPLIT_EFFECT = pltpu.SideEffectType.DATAFLOW_SIDE_EFFECTING


def exchange_start(tensors, scatter, name):
    n = len(tensors)
    land_shapes = [t.shape if scatter else (NDEV,) + t.shape for t in tensors]

    def body(*refs):
        v_refs, land_refs = refs[:n], refs[n:2 * n]
        send_sems, recv_sems = refs[2 * n], refs[2 * n + 1]
        token = refs[-1]
        for cp in _split_copies(v_refs, land_refs, send_sems, recv_sems, scatter):
            cp.start()
        token[...] = jnp.zeros_like(token)

    sems = pltpu.SemaphoreType.DMA((n * (NDEV - 1),))
    out = _call(
        body, name=name,
        out_shape=(sems, sems, *[pltpu.HBM(t.shape, t.dtype) for t in tensors],
                   *[pltpu.HBM(s, t.dtype) for s, t in zip(land_shapes, tensors)], jax.ShapeDtypeStruct((8, 128), F32)),
        in_specs=(_HBM_SPEC,) * (2 * n),
        out_specs=(_SEM_SPEC, _SEM_SPEC) + (_HBM_SPEC,) * (2 * n) + (pl.BlockSpec(memory_space=pltpu.VMEM),),
        input_output_aliases={i: i + 2 for i in range(2 * n)},
        compiler_params=pltpu.CompilerParams(has_side_effects=_SPLIT_EFFECT),
    )(*[pltpu.with_memory_space_constraint(t, pltpu.HBM) for t in tensors],
      *[pltpu.with_memory_space_constraint(lax.empty(s, t.dtype), pltpu.HBM) for s, t in zip(land_shapes, tensors)])
    return out[0], out[1], list(out[2:2 + n]), list(out[2 + n:2 + 2 * n]), out[-1]


def exchange_wait(started, after, scatter, name):
    send_sems, recv_sems, v_thru, land_thru, _ = started
    n = len(v_thru)

    def body(*refs):
        v_refs, land_refs = refs[:n], refs[n:2 * n]
        for cp in _split_copies(v_refs, land_refs, refs[2 * n], refs[2 * n + 1], scatter):
            cp.wait_send()
            cp.wait_recv()

    out = _call(
        body, name=name,
        out_shape=tuple(pltpu.HBM(t.shape, t.dtype) for t in v_thru + land_thru),
        in_specs=(_HBM_SPEC,) * (2 * n) + (_SEM_SPEC, _SEM_SPEC, pl.BlockSpec(memory_space=pl.ANY)),
        out_specs=(_HBM_SPEC,) * (2 * n), input_output_aliases={i: i for i in range(2 * n)},
        compiler_params=pltpu.CompilerParams(has_side_effects=_SPLIT_EFFECT),
    )(*v_thru, *land_thru, send_sems, recv_sems, after)
    return list(out[:n]), list(out[n:])


def mod_slices(c_all, w_ada, b_loc):
    nl, _, nc = w_ada.shape

    def body(c_ref, w_ref, b_ref, o_ref):
        cv = c_ref[...]
        o_ref[0] = _nn(cv * jax.nn.sigmoid(cv), w_ref[0]) + b_ref[0]

    return _call(body, name="mod_slices", grid=(nl,),
                 in_specs=[pl.BlockSpec((NDEV, D), lambda l: (0, 0)), pl.BlockSpec((1, D, nc), lambda l: (l, 0, 0)),
                           pl.BlockSpec((1, 1, nc), lambda l: (l, 0, 0))],
                 out_specs=pl.BlockSpec((1, NDEV, nc), lambda l: (l, 0, 0)),
                 out_shape=jax.ShapeDtypeStruct((nl, NDEV, nc), F32),
                 compiler_params=_params("arbitrary"))(c_all, w_ada, b_loc.reshape(nl, 1, nc))


def ada_grad(c_all, dm_loc):
    nl, _, nc = dm_loc.shape

    def body(c_ref, d_ref, o_ref):
        cv = c_ref[...]
        o_ref[0] = _tn(cv * jax.nn.sigmoid(cv), d_ref[0])

    return _call(body, name="ada_grad", grid=(nl,),
                 in_specs=[pl.BlockSpec((NDEV, D), lambda l: (0, 0)), pl.BlockSpec((1, NDEV, nc), lambda l: (l, 0, 0))],
                 out_specs=pl.BlockSpec((1, D, nc), lambda l: (l, 0, 0)),
                 out_shape=jax.ShapeDtypeStruct((nl, D, nc), F32),
                 compiler_params=_params("arbitrary"))(c_all, dm_loc)


def sum_chunks(chunks):
    r = chunks.shape[1]

    def body(c_ref, o_ref):
        acc = c_ref[0]
        for i in range(1, NDEV):
            acc = acc + c_ref[i]
        o_ref[...] = acc

    return _call(body, name="sum_chunks", out_shape=jax.ShapeDtypeStruct((r, 128), F32))(chunks)


def _row_tile(r):
    if r <= 256:
        return r
    for t in range(256, 7, -8):
        if r % t == 0:
            return t
    return r


def adamw(w, m, v, g=None, chunks=None, name="adamw"):
    r, cdim = w.shape
    tr = _row_tile(r)
    bc1 = 1.0 - ADAM_B1 ** ADAM_STEP
    bc2 = 1.0 - ADAM_B2 ** ADAM_STEP

    def body(g_ref, w_ref, m_ref, v_ref, go_ref, d_ref, mo_ref, vo_ref):
        if chunks is None:
            grad = g_ref[...]
        else:
            grad = g_ref[0].astype(F32)
            for i in range(1, NDEV):
                grad = grad + g_ref[i].astype(F32)
        mn = ADAM_B1 * m_ref[...] + (1.0 - ADAM_B1) * grad
        vn = ADAM_B2 * v_ref[...] + (1.0 - ADAM_B2) * (grad * grad)
        m_hat = mn / bc1
        v_hat = vn / bc2
        go_ref[...] = grad
        d_ref[...] = -ADAM_LR * (m_hat / (jnp.sqrt(v_hat) + ADAM_EPS) + ADAM_WD * w_ref[...])
        mo_ref[...] = mn
        vo_ref[...] = vn

    tile = pl.BlockSpec((tr, cdim), lambda i: (i, 0))
    gspec = tile if chunks is None else pl.BlockSpec((NDEV, tr, cdim), lambda i: (0, i, 0))
    shp = jax.ShapeDtypeStruct((r, cdim), F32)
    return _call(body, name=name, grid=(r // tr,), in_specs=[gspec, tile, tile, tile],
                 out_specs=[tile] * 4, out_shape=[shp] * 4,
                 compiler_params=_params("arbitrary"))(g if chunks is None else chunks, w, m, v)


def adamw_layers(w, m, v, chunks, name):
    nl, r, cdim = w.shape
    tr = _row_tile(r)
    bc1 = 1.0 - ADAM_B1 ** ADAM_STEP
    bc2 = 1.0 - ADAM_B2 ** ADAM_STEP
    outs = [lax.empty(w.shape, F32) for _ in range(4)]
    for l in range(nl):
        def body(g_ref, w_ref, m_ref, v_ref, p0, p1, p2, p3, go_ref, d_ref, mo_ref, vo_ref):
            grad = g_ref[0].astype(F32)
            for i in range(1, NDEV):
                grad = grad + g_ref[i].astype(F32)
            mn = ADAM_B1 * m_ref[...] + (1.0 - ADAM_B1) * grad
            vn = ADAM_B2 * v_ref[...] + (1.0 - ADAM_B2) * (grad * grad)
            go_ref[...] = grad
            d_ref[...] = -ADAM_LR * ((mn / bc1) / (jnp.sqrt(vn / bc2) + ADAM_EPS) + ADAM_WD * w_ref[...])
            mo_ref[...] = mn
            vo_ref[...] = vn

        tile = pl.BlockSpec((None, tr, cdim), lambda i, l=l: (l, i, 0))
        whole = pl.BlockSpec(memory_space=pl.ANY)
        outs = _call(body, name=f"{name}_{l}", grid=(r // tr,),
                     in_specs=[pl.BlockSpec((NDEV, tr, cdim), lambda i: (0, i, 0)), tile, tile, tile] + [whole] * 4,
                     out_specs=[tile] * 4, out_shape=[jax.ShapeDtypeStruct(w.shape, F32)] * 4,
                     input_output_aliases={4: 0, 5: 1, 6: 2, 7: 3},
                     compiler_params=_params("arbitrary"))(chunks[l], w, m, v, *outs)
    return outs


WEIGHTS = ("w_ada", "b_ada", "w_in", "b_f", "gm_ln_g", "gm_ln_b", "gm_w_s", "gm_b_s", "s5_lam_re", "s5_lam_im",
           "s5_log_dt", "s5_b_re", "s5_b_im", "s5_c_re", "s5_c_im", "s5_d", "s5_w_glu", "s5_b_glu", "w_out", "ln1_g",
           "ln1_b", "w_up", "conv_w", "conv_b", "w_down", "ln2_g", "ln2_b")
SHARDED = ("w_in", "w_out", "w_up", "w_down", "conv_w")
LARGE = ("w_in", "w_out", "w_up", "w_down")
COL_SHARDED = ("w_in", "w_up", "conv_w")
PACKED = ("b_ada",) + SMALL
PACK_SEG = 8 * 128


def _gather_cols(g):
    nd, nl, r, c = g.shape
    return jnp.transpose(g, (1, 2, 0, 3)).reshape(nl, r, nd * c)


def _chunk_cols(g):
    nl, r, c8 = g.shape
    return jnp.transpose(g.reshape(nl, r, NDEV, c8 // NDEV), (2, 0, 1, 3))


def _join_cols(g):
    nd, r, c = g.shape
    return jnp.transpose(g, (1, 0, 2)).reshape(r, nd * c)


def _join_rows(g):
    nd, r, c = g.shape
    return g.reshape(nd * r, c)


def _split_cols(g):
    r, c8 = g.shape
    return jnp.transpose(g.reshape(r, NDEV, c8 // NDEV), (1, 0, 2))


def _split_rows(g):
    r8, c = g.shape
    return g.reshape(NDEV, r8 // NDEV, c)


def _pack(parts):
    segs = []
    for n in PACKED:
        flat = parts[n].reshape(-1)
        segs.append(jnp.pad(flat, (0, -flat.shape[0] % PACK_SEG)).reshape(-1, 128))
    rows = jnp.concatenate(segs, axis=0)
    return jnp.pad(rows, ((0, -rows.shape[0] % (NDEV * 8)), (0, 0)))


def _unpack(rows, shapes):
    out, off = {}, 0
    for n in PACKED:
        size = math.prod(shapes[n])
        nrows = -(-size // PACK_SEG) * 8
        out[n] = rows[off:off + nrows].reshape(-1)[:size].reshape(shapes[n])
        off += nrows
    return out


def kernel(x, c, w_ada, b_ada, w_in, b_f, gm_ln_g, gm_ln_b, gm_w_s, gm_b_s, s5_lam_re, s5_lam_im, s5_log_dt, s5_b_re, s5_b_im, s5_c_re, s5_c_im, s5_d, s5_w_glu, s5_b_glu, w_out, ln1_g, ln1_b, w_up, conv_w, conv_b, w_down, ln2_g, ln2_b, loss_target, m_w_ada, m_b_ada, m_w_in, m_b_f, m_gm_ln_g, m_gm_ln_b, m_gm_w_s, m_gm_b_s, m_s5_lam_re, m_s5_lam_im, m_s5_log_dt, m_s5_b_re, m_s5_b_im, m_s5_c_re, m_s5_c_im, m_s5_d, m_s5_w_glu, m_s5_b_glu, m_w_out, m_ln1_g, m_ln1_b, m_w_up, m_conv_w, m_conv_b, m_w_down, m_ln2_g, m_ln2_b, v_w_ada, v_b_ada, v_w_in, v_b_f, v_gm_ln_g, v_gm_ln_b, v_gm_w_s, v_gm_b_s, v_s5_lam_re, v_s5_lam_im, v_s5_log_dt, v_s5_b_re, v_s5_b_im, v_s5_c_re, v_s5_c_im, v_s5_d, v_s5_w_glu, v_s5_b_glu, v_w_out, v_ln1_g, v_ln1_b, v_w_up, v_conv_w, v_conv_b, v_w_down, v_ln2_g, v_ln2_b):
    given = dict(locals())
    wts = {n: given[n] for n in WEIGHTS}
    mom = {n: given["m_" + n] for n in WEIGHTS}
    var = {n: given["v_" + n] for n in WEIGHTS}
    nl = w_ada.shape[0]
    me = _my_index()
    ada_cols = w_ada.shape[2]

    (c_all,) = exchange([c], False, "gather_c")
    c_all = c_all.reshape(NDEV, D)
    b_loc = lax.dynamic_slice_in_dim(b_ada, me * ada_cols, ada_cols, axis=1)
    mod_part = mod_slices(c_all, w_ada, b_loc)

    mod_all, conv_all = exchange([mod_part, conv_w], False, "gather_mod")
    mod_mine = lax.dynamic_index_in_dim(mod_all, me, axis=2, keepdims=False)
    mods = jnp.transpose(mod_mine, (1, 0, 2)).reshape(nl, 6, D)
    mods = jnp.pad(mods, ((0, 0), (0, 2), (0, 0)))
    conv_full = _gather_cols(conv_all)
    sp = {n: wts[n] for n in SMALL}

    def joined(name, own, land):
        full = (_join_cols if name in COL_SHARDED else _join_rows)(land)
        at = (0, me * own.shape[1]) if name in COL_SHARDED else (me * own.shape[0], 0)
        return lax.dynamic_update_slice(full, own, at)

    def block(l, names):
        return [wts[n][l].astype(BF16) for n in names]

    def chunked(grads, names):
        return [(_split_cols if n in COL_SHARDED else _split_rows)(grads[n]) for n in names]

    head, tail = LARGE[:1], LARGE[1:]
    got_head = exchange_start(block(0, head), False, "gather_start_0_in")
    rest = block(0, tail)
    rest[0] = rest[0] + got_head[4][0, 0].astype(BF16)
    got_tail = exchange_start(rest, False, "gather_start_0_rest")
    xl, saved, weights = x[0], [], []
    for l in range(nl):
        if l == 0:
            own, land = exchange_wait(got_head, got_tail[4], False, "gather_wait_0_in")
            w = {n: joined(n, o, g) for n, o, g in zip(head, own, land)}
        else:
            own, land = exchange_wait(started, xl, False, f"gather_wait_{l}")
            w = {n: joined(n, o, g) for n, o, g in zip(LARGE, own, land)}
        mod_l = mods[l]
        if l + 1 < nl:
            nxt, w["w_in"] = lax.optimization_barrier((block(l + 1, LARGE), w["w_in"]))
            started = exchange_start(nxt, False, f"gather_start_{l + 1}")
            mod_l = mod_l + started[4][0, 0]
        w["w_in"] = jnp.pad(w["w_in"], ((0, 0), (0, NP - D_IN)))
        f = _layer_operands(sp, l)
        mixcat, sv = layer_fwd_mix(xl, mod_l, w["w_in"], f)
        if l == 0:
            own, land = exchange_wait(got_tail, mixcat, False, "gather_wait_0_rest")
            w.update({n: joined(n, o, g) for n, o, g in zip(tail, own, land)})
        w["conv_w"] = conv_full[l]
        xl, sv = layer_fwd_rest(xl, mixcat, mod_l, w, f, sv)
        weights.append(w)
        saved.append(sv)

    loss_tile, dx = loss_kernel(xl, loss_target[0])

    ffn_names, mix_names = ("w_up", "w_down"), ("w_in", "w_out")
    scattering, vals, dmods, gconv = [None] * nl, [None] * nl, [None] * nl, [None] * nl
    token = jnp.zeros((), F32)
    for l in reversed(range(nl)):
        mod_l = mods[l] + token
        dx1, g_ffn, part = layer_bwd_ffn(dx, saved[l], mod_l, weights[l])
        gconv[l] = g_ffn["conv_w"]
        if l == 0:
            sent_ffn = exchange_start(chunked(g_ffn, ffn_names), True, "scatter_start_0_ffn")
            mod_l = mod_l + sent_ffn[4][0, 0]
        dx, g_mix, vals[l], dmods[l] = layer_bwd_mix(dx1, saved[l], mod_l, weights[l], part)
        g_mix["w_in"] = g_mix["w_in"][:, :D_IN]
        if l == 0:
            scattering[l] = [(ffn_names, sent_ffn),
                             (mix_names, exchange_start(chunked(g_mix, mix_names), True, "scatter_start_0_mix"))]
        else:
            sent = exchange_start(chunked(dict(g_ffn, **g_mix), LARGE), True, f"scatter_start_{l}")
            scattering[l] = [(LARGE, sent)]
            token = sent[4][0, 0]
    gx = dx
    dmods = jnp.stack(dmods)
    gsm = {n: jnp.stack([v[n] for v in vals]) for n in SMALL}

    gsm["b_ada"] = dmods.reshape(nl, 6 * D)
    packed = _pack(gsm).reshape(NDEV, -1, 128)
    conv_recv, small_recv = exchange([_chunk_cols(jnp.stack(gconv)), packed], True, "scatter_small")
    small_sum = sum_chunks(small_recv)
    small_all, dmod_all = exchange([small_sum, dmods.reshape(nl, 6 * D)], False, "gather_small")

    received, after = [dict() for _ in range(nl)], small_all
    for l in reversed(range(nl)):
        for k, (names, sent) in enumerate(scattering[l]):
            own, land = exchange_wait(sent, after, True, f"scatter_wait_{l}_{k}")
            for n, o, g in zip(names, own, land):
                mine = lax.dynamic_index_in_dim(o, me, 0, keepdims=False)
                received[l][n] = lax.dynamic_update_index_in_dim(g, mine, me, 0)
            after = land[0]

    out = {}
    for n in LARGE:
        out[n] = adamw_layers(wts[n], mom[n], var[n], [received[l][n] for l in range(nl)], "adamw_" + n)
    shp = conv_w.shape
    two_d = lambda a: a.reshape(shp[0] * shp[1], shp[2])
    res = adamw(two_d(conv_w), two_d(m_conv_w), two_d(v_conv_w),
                chunks=conv_recv.reshape(NDEV, shp[0] * shp[1], shp[2]), name="adamw_conv_w")
    out["conv_w"] = [r.reshape(shp) for r in res]

    dm_loc = lax.dynamic_slice_in_dim(dmod_all, me * ada_cols, ada_cols, axis=2)
    g_ada = ada_grad(c_all, jnp.transpose(dm_loc, (1, 0, 2)))
    two_d = lambda a: a.reshape(nl * D, ada_cols)
    res = adamw(two_d(w_ada), two_d(m_w_ada), two_d(v_w_ada), g=two_d(g_ada), name="adamw_w_ada")
    out["w_ada"] = [r.reshape(w_ada.shape) for r in res]

    shapes = {n: wts[n].shape for n in PACKED}
    res = adamw(_pack(wts), _pack(mom), _pack(var), g=small_all.reshape(-1, 128), name="adamw_small")
    unpacked = [_unpack(r, shapes) for r in res]
    for n in PACKED:
        out[n] = [u[n] for u in unpacked]

    loss = lax.psum(loss_tile[0, 0], ("x", "y", "c"))
    return (loss, gx[None], *[out[n][0] for n in WEIGHTS], *[out[n][1] for n in WEIGHTS],
            *[out[n][2] for n in WEIGHTS], *[out[n][3] for n in WEIGHTS])
```

```python
import functools
import math

import jax
import jax.numpy as jnp
from jax import lax
from jax.experimental import pallas as pl
from jax.experimental.pallas import tpu as pltpu

F32 = jnp.float32
BF16 = jnp.bfloat16
MESH_IDS = pl.DeviceIdType.MESH

D = 1024
SEQ = 4096
DEPTH = 4
NDEV = 8
HD = 64
GM_W = 256
GM_H = 4
GM_C = 128
S5_W = 256
S5_G = 16
S5_H = 16
S5_P = 64
S5_N = S5_G * S5_P
FX_W = 512
FX_H = 8
D_IN = 2 * GM_W + S5_W + 3 * FX_W + FX_H
NP = 2432
FF_COL = 2304
DFF = 2816
LN_EPS = 1e-5
DN_ALPHA = (2.0 * DEPTH) ** 0.25
NEG_INF = -1e30
ADAM_LR = 0.001
ADAM_B1 = 0.9
ADAM_B2 = 0.999
ADAM_EPS = 1e-08
ADAM_WD = 0.01
ADAM_STEP = 10

V7X_VMEM_LIMIT = 56 * 1024 * 1024
TS = 512
TS_C = 256
T_S5 = 256
TQ = 512
ATTN_HEADS = 2


def _call(body, **kw):
    return pl.pallas_call(body, **kw)


def _params(*sem):
    return pltpu.CompilerParams(dimension_semantics=sem if sem else None,
                                vmem_limit_bytes=V7X_VMEM_LIMIT)


def _nn(a, b):
    return jnp.dot(a.astype(BF16), b.astype(BF16), preferred_element_type=F32)


def _nt(a, b):
    return lax.dot_general(a.astype(BF16), b.astype(BF16), (((1,), (1,)), ((), ())),
                           preferred_element_type=F32)


def _tn(a, b):
    return lax.dot_general(a.astype(BF16), b.astype(BF16), (((0,), (0,)), ((), ())),
                           preferred_element_type=F32)


@jax.custom_vjp
def _bdot(a, b):
    return _nn(a, b)


def _bdot_fwd(a, b):
    return _nn(a, b), (a, b)


def _bdot_bwd(res, g):
    a, b = res
    return _nt(g, b), _tn(a, g)


_bdot.defvjp(_bdot_fwd, _bdot_bwd)


@jax.custom_vjp
def _bdot_nt(a, b):
    return _nt(a, b)


def _bdot_nt_fwd(a, b):
    return _nt(a, b), (a, b)


def _bdot_nt_bwd(res, g):
    a, b = res
    return _nn(g, b), _tn(g, a)


_bdot_nt.defvjp(_bdot_nt_fwd, _bdot_nt_bwd)


def _ln(r, g, b):
    mu = jnp.mean(r, axis=-1, keepdims=True)
    xc = r - mu
    var = jnp.mean(xc * xc, axis=-1, keepdims=True)
    return xc * lax.rsqrt(var + LN_EPS) * g + b


def _rows(shape):
    return lax.broadcasted_iota(jnp.int32, shape, 0)


def _lanes(shape):
    return lax.broadcasted_iota(jnp.int32, shape, 1)


def mm_nn(a, w, tn, name, mod=None, rows=None, out_dtype=F32):
    s, k = a.shape
    n = w.shape[1]
    ts = min(TS, s)

    def body(*refs):
        if mod is None:
            a_ref, w_ref, o_ref = refs
            h = a_ref[...]
        else:
            a_ref, m_ref, w_ref, o_ref = refs
            h = a_ref[...] * (1.0 + m_ref[rows[1]:rows[1] + 1, :]) + m_ref[rows[0]:rows[0] + 1, :]
        o_ref[...] = jnp.dot(h.astype(BF16), w_ref[...], preferred_element_type=F32).astype(out_dtype)

    in_specs = [pl.BlockSpec((ts, k), lambda j, i: (i, 0))]
    args = [a]
    if mod is not None:
        in_specs.append(pl.BlockSpec((8, k), lambda j, i: (0, 0)))
        args.append(mod)
    in_specs.append(pl.BlockSpec((k, tn), lambda j, i: (0, j)))
    args.append(w)
    return _call(body, name=name, grid=(n // tn, s // ts), in_specs=in_specs,
                 out_specs=pl.BlockSpec((ts, tn), lambda j, i: (i, j)),
                 out_shape=jax.ShapeDtypeStruct((s, n), out_dtype),
                 compiler_params=_params("arbitrary", "arbitrary"))(*args)


def mm_nt(dy, w, name):
    s, n = dy.shape
    k = w.shape[0]
    ts = min(TS, s)

    def body(dy_ref, w_ref, o_ref):
        o_ref[...] = _nt(dy_ref[...], w_ref[...])

    return _call(body, name=name, grid=(s // ts,),
                 in_specs=[pl.BlockSpec((ts, n), lambda i: (i, 0)),
                           pl.BlockSpec((k, n), lambda i: (0, 0))],
                 out_specs=pl.BlockSpec((ts, k), lambda i: (i, 0)),
                 out_shape=jax.ShapeDtypeStruct((s, k), F32),
                 compiler_params=_params("arbitrary"))(dy, w)


def mm_nt_mod(dy, w, x, dres, mod, rows, name):
    s, n = dy.shape
    k = w.shape[0]
    ts = min(TS, s)

    def body(dy_ref, w_ref, x_ref, r_ref, m_ref, dx_ref, dsh_ref, dsc_ref):
        @pl.when(pl.program_id(0) == 0)
        def _():
            dsh_ref[...] = jnp.zeros_like(dsh_ref)
            dsc_ref[...] = jnp.zeros_like(dsc_ref)

        dh = _nt(dy_ref[...], w_ref[...])
        dx_ref[...] = r_ref[...] + dh * (1.0 + m_ref[rows[1]:rows[1] + 1, :])
        dsh_ref[...] += jnp.sum(dh, axis=0, keepdims=True)
        dsc_ref[...] += jnp.sum(dh * x_ref[...], axis=0, keepdims=True)

    row = pl.BlockSpec((1, k), lambda i: (0, 0))
    tile = pl.BlockSpec((ts, k), lambda i: (i, 0))
    return _call(body, name=name, grid=(s // ts,),
                 in_specs=[pl.BlockSpec((ts, n), lambda i: (i, 0)),
                           pl.BlockSpec((k, n), lambda i: (0, 0)), tile, tile,
                           pl.BlockSpec((8, k), lambda i: (0, 0))],
                 out_specs=[tile, row, row],
                 out_shape=[jax.ShapeDtypeStruct((s, k), F32),
                            jax.ShapeDtypeStruct((1, k), F32),
                            jax.ShapeDtypeStruct((1, k), F32)],
                 compiler_params=_params("arbitrary"))(dy, w, x, dres, mod)


def mm_tn(a, dy, tn, name, mod=None, rows=None):
    s, k = a.shape
    n = dy.shape[1]
    ts = min(TS, s)
    ns = s // ts

    def body(*refs):
        if mod is None:
            a_ref, dy_ref, o_ref, acc = refs
            h = a_ref[...]
        else:
            a_ref, m_ref, dy_ref, o_ref, acc = refs
            h = a_ref[...] * (1.0 + m_ref[rows[1]:rows[1] + 1, :]) + m_ref[rows[0]:rows[0] + 1, :]
        i = pl.program_id(1)

        @pl.when(i == 0)
        def _():
            acc[...] = jnp.zeros_like(acc)

        acc[...] += _tn(h, dy_ref[...])

        @pl.when(i == ns - 1)
        def _():
            o_ref[...] = acc[...].astype(BF16)

    in_specs = [pl.BlockSpec((ts, k), lambda j, i: (i, 0))]
    args = [a]
    if mod is not None:
        in_specs.append(pl.BlockSpec((8, k), lambda j, i: (0, 0)))
        args.append(mod)
    in_specs.append(pl.BlockSpec((ts, tn), lambda j, i: (i, j)))
    args.append(dy)
    return _call(body, name=name, grid=(n // tn, s // ts), in_specs=in_specs,
                 out_specs=pl.BlockSpec((k, tn), lambda j, i: (0, j)),
                 out_shape=jax.ShapeDtypeStruct((k, n), BF16),
                 scratch_shapes=[pltpu.VMEM((k, tn), F32)],
                 compiler_params=_params("arbitrary", "arbitrary"))(*args)


def _post_fn(x, br, gate, lg, lb):
    return _ln(DN_ALPHA * x + (1.0 + gate) * br, lg, lb)


def post_fwd(x, br, mod, grow, lg, lb, name):
    s = x.shape[0]
    ts = min(TS, s)

    def body(x_ref, b_ref, m_ref, lg_ref, lb_ref, o_ref):
        o_ref[...] = _post_fn(x_ref[...], b_ref[...], m_ref[grow:grow + 1, :], lg_ref[...], lb_ref[...])

    tile = pl.BlockSpec((ts, D), lambda i: (i, 0))
    row = pl.BlockSpec((1, D), lambda i: (0, 0))
    return _call(body, name=name, grid=(s // ts,),
                 in_specs=[tile, tile, pl.BlockSpec((8, D), lambda i: (0, 0)), row, row],
                 out_specs=tile, out_shape=jax.ShapeDtypeStruct((s, D), F32),
                 compiler_params=_params("arbitrary"))(x, br, mod, lg, lb)


def post_bwd(x, br, mod, grow, lg, lb, dy, name):
    s = x.shape[0]
    ts = min(TS, s)

    def body(x_ref, b_ref, m_ref, lg_ref, lb_ref, dy_ref, dx_ref, db_ref, dg_ref, dlg_ref, dlb_ref):
        @pl.when(pl.program_id(0) == 0)
        def _():
            dg_ref[...] = jnp.zeros_like(dg_ref)
            dlg_ref[...] = jnp.zeros_like(dlg_ref)
            dlb_ref[...] = jnp.zeros_like(dlb_ref)

        _, vjp = jax.vjp(_post_fn, x_ref[...], b_ref[...], m_ref[grow:grow + 1, :], lg_ref[...], lb_ref[...])
        dx, db, dg, dlg, dlb = vjp(dy_ref[...])
        dx_ref[...] = dx
        db_ref[...] = db.astype(BF16)
        dg_ref[...] += dg
        dlg_ref[...] += dlg
        dlb_ref[...] += dlb

    tile = pl.BlockSpec((ts, D), lambda i: (i, 0))
    row = pl.BlockSpec((1, D), lambda i: (0, 0))
    rs = jax.ShapeDtypeStruct((1, D), F32)
    return _call(body, name=name, grid=(s // ts,),
                 in_specs=[tile, tile, pl.BlockSpec((8, D), lambda i: (0, 0)), row, row, tile],
                 out_specs=[tile, tile, row, row, row],
                 out_shape=[jax.ShapeDtypeStruct((s, D), F32), jax.ShapeDtypeStruct((s, D), BF16), rs, rs, rs],
                 compiler_params=_params("arbitrary"))(x, br, mod, lg, lb, dy)


def loss_kernel(y, target):
    s = y.shape[0]
    ts = min(TS, s)

    def body(y_ref, t_ref, l_ref, dy_ref):
        @pl.when(pl.program_id(0) == 0)
        def _():
            l_ref[...] = jnp.zeros_like(l_ref)

        err = y_ref[...] - t_ref[...]
        dy_ref[...] = err * (1.0 / D)
        per_tok = jnp.mean(err * err, axis=-1, keepdims=True)
        l_ref[...] += 0.5 * jnp.sum(per_tok)

    tile = pl.BlockSpec((ts, D), lambda i: (i, 0))
    return _call(body, name="loss", grid=(s // ts,), in_specs=[tile, tile],
                 out_specs=[pl.BlockSpec((8, 128), lambda i: (0, 0)), tile],
                 out_shape=[jax.ShapeDtypeStruct((8, 128), F32), jax.ShapeDtypeStruct((s, D), F32)],
                 compiler_params=_params("arbitrary"))(y, target)


def _gm_pair(u, v, lg, lb, w0, w1, bs0, bs1):
    t = u.shape[0]
    low = _lanes((t, 2 * HD)) < HD

    def head_mean(x):
        lo = jnp.sum(jnp.where(low, x, 0.0), axis=-1, keepdims=True)
        hi = jnp.sum(jnp.where(low, 0.0, x), axis=-1, keepdims=True)
        return jnp.where(low, lo, hi) * (1.0 / HD)

    xc = v - head_mean(v)
    vn = xc * lax.rsqrt(head_mean(xc * xc) + LN_EPS) * lg + lb
    v0 = jnp.where(low, vn, 0.0)
    v1 = jnp.where(low, 0.0, vn)
    causal = _rows((GM_C, GM_C)) >= _lanes((GM_C, GM_C))
    wm0 = jnp.where(causal, w0, 0.0)
    wm1 = jnp.where(causal, w1, 0.0)
    bias = jnp.where(_lanes((GM_C, 2 * HD)) < HD, bs0, bs1)
    chunks = []
    for n in range(t // GM_C):
        rs = slice(n * GM_C, (n + 1) * GM_C)
        chunks.append(u[rs] * (_bdot(wm0, v0[rs]) + _bdot(wm1, v1[rs]) + bias))
    return jnp.concatenate(chunks, axis=0)


def gm_fwd(p, lg, lb, ws, bst):
    s = p.shape[0]
    ts = min(TS_C, s)

    def body(u_ref, v_ref, lg_ref, lb_ref, ws_ref, bs_ref, o_ref):
        for j in range(GM_H // 2):
            sl = slice(j * 2 * HD, (j + 1) * 2 * HD)
            o_ref[:, sl] = _gm_pair(u_ref[:, sl], v_ref[:, sl], lg_ref[:, sl], lb_ref[:, sl], ws_ref[2 * j],
                                    ws_ref[2 * j + 1], bs_ref[:, 2 * j:2 * j + 1],
                                    bs_ref[:, 2 * j + 1:2 * j + 2]).astype(BF16)

    full = lambda shape: pl.BlockSpec(shape, lambda i: (0,) * len(shape))
    return _call(body, name="gm_fwd", grid=(s // ts,),
                 in_specs=[pl.BlockSpec((ts, GM_W), lambda i: (i, 0)), pl.BlockSpec((ts, GM_W), lambda i: (i, 1)),
                           full((1, GM_W)), full((1, GM_W)), full((GM_H, GM_C, GM_C)), full((GM_C, GM_H))],
                 out_specs=pl.BlockSpec((ts, GM_W), lambda i: (i, 0)),
                 out_shape=jax.ShapeDtypeStruct((s, GM_W), BF16),
                 compiler_params=_params("arbitrary"))(p, p, lg, lb, ws, bst)


def gm_bwd(p, lg, lb, ws, bst, dmix):
    s = p.shape[0]
    ts = min(TS_C, s)

    def body(u_ref, v_ref, lg_ref, lb_ref, ws_ref, bs_ref, dy_ref, duv_ref, dlg_ref, dlb_ref, dws_ref, dbs_ref):
        @pl.when(pl.program_id(0) == 0)
        def _():
            dlg_ref[...] = jnp.zeros_like(dlg_ref)
            dlb_ref[...] = jnp.zeros_like(dlb_ref)
            dws_ref[...] = jnp.zeros_like(dws_ref)
            dbs_ref[...] = jnp.zeros_like(dbs_ref)

        for j in range(GM_H // 2):
            sl = slice(j * 2 * HD, (j + 1) * 2 * HD)
            _, vjp = jax.vjp(_gm_pair, u_ref[:, sl], v_ref[:, sl], lg_ref[:, sl], lb_ref[:, sl], ws_ref[2 * j],
                             ws_ref[2 * j + 1], bs_ref[:, 2 * j:2 * j + 1], bs_ref[:, 2 * j + 1:2 * j + 2])
            du, dv, dlg, dlb, dw0, dw1, dbs0, dbs1 = vjp(dy_ref[:, sl])
            duv_ref[:, sl] = du.astype(BF16)
            duv_ref[:, GM_W + j * 2 * HD:GM_W + (j + 1) * 2 * HD] = dv.astype(BF16)
            dlg_ref[:, sl] += dlg
            dlb_ref[:, sl] += dlb
            dws_ref[2 * j] += dw0
            dws_ref[2 * j + 1] += dw1
            dbs_ref[:, 2 * j:2 * j + 1] += dbs0
            dbs_ref[:, 2 * j + 1:2 * j + 2] += dbs1

    full = lambda shape: pl.BlockSpec(shape, lambda i: (0,) * len(shape))
    return _call(body, name="gm_bwd", grid=(s // ts,),
                 in_specs=[pl.BlockSpec((ts, GM_W), lambda i: (i, 0)), pl.BlockSpec((ts, GM_W), lambda i: (i, 1)),
                           full((1, GM_W)), full((1, GM_W)), full((GM_H, GM_C, GM_C)), full((GM_C, GM_H)),
                           pl.BlockSpec((ts, GM_W), lambda i: (i, 0))],
                 out_specs=[pl.BlockSpec((ts, 2 * GM_W), lambda i: (i, 0)), full((1, GM_W)), full((1, GM_W)),
                            full((GM_H, GM_C, GM_C)), full((GM_C, GM_H))],
                 out_shape=[jax.ShapeDtypeStruct((s, 2 * GM_W), BF16), jax.ShapeDtypeStruct((1, GM_W), F32),
                            jax.ShapeDtypeStruct((1, GM_W), F32), jax.ShapeDtypeStruct((GM_H, GM_C, GM_C), F32),
                            jax.ShapeDtypeStruct((GM_C, GM_H), F32)],
                 compiler_params=_params("arbitrary"))(p, p, lg, lb, ws, bst, dmix)


def _s5_prep_fn(lr, li, ldt, bt):
    dt = jnp.exp(ldt)
    er = jnp.exp(lr * dt)
    ar = er * jnp.cos(li * dt)
    ai = er * jnp.sin(li * dt)
    den = lr * lr + li * li
    nr = ar - 1.0
    cr = (nr * lr + ai * li) / den
    ci = (ai * lr - nr * li) / den
    br, bi = bt[:, :S5_N], bt[:, S5_N:]
    return ar, ai, jnp.concatenate([cr * br - ci * bi, cr * bi + ci * br], axis=1)


def s5_prep_fwd(lr, li, ldt, bt):
    def body(lr_ref, li_ref, ldt_ref, bt_ref, a_ref, bb_ref):
        ar, ai, bb = _s5_prep_fn(lr_ref[...], li_ref[...], ldt_ref[...], bt_ref[...])
        a_ref[...] = jnp.concatenate([ar, ai, jnp.zeros((6, S5_N), F32)], axis=0)
        bb_ref[...] = bb

    return _call(body, name="s5_prep_fwd",
                 out_shape=[jax.ShapeDtypeStruct((8, S5_N), F32), jax.ShapeDtypeStruct((S5_H, 2 * S5_N), F32)])(lr, li, ldt, bt)


def s5_prep_bwd(lr, li, ldt, bt, da, dbb):
    def body(lr_ref, li_ref, ldt_ref, bt_ref, da_ref, dbb_ref, dlr_ref, dli_ref, dldt_ref, dbt_ref):
        _, vjp = jax.vjp(_s5_prep_fn, lr_ref[...], li_ref[...], ldt_ref[...], bt_ref[...])
        dlr, dli, dldt, dbt = vjp((da_ref[0:1, :], da_ref[1:2, :], dbb_ref[...]))
        dlr_ref[...] = dlr
        dli_ref[...] = dli
        dbt_ref[...] = dbt
        group = (_rows((S5_N, 128)) // S5_P == _lanes((S5_N, 128))).astype(F32)
        dldt_ref[...] = jnp.dot(jnp.broadcast_to(dldt, (8, S5_N)), group, precision=lax.Precision.HIGHEST,
                                preferred_element_type=F32)[0:1, :]

    r = jax.ShapeDtypeStruct((1, S5_N), F32)
    return _call(body, name="s5_prep_bwd",
                 out_shape=[r, r, jax.ShapeDtypeStruct((1, 128), F32),
                            jax.ShapeDtypeStruct((S5_H, 2 * S5_N), F32)])(lr, li, ldt, bt, da, dbb)


def _s5_out_fn(x, u, cbd, drow, wg, bg):
    y = _bdot_nt(x[:, :S5_N], cbd[:, :S5_N]) - _bdot_nt(x[:, S5_N:], cbd[:, S5_N:]) + drow * u
    y = jax.nn.gelu(y)
    gate = _bdot_nt(y, wg) + bg
    return y * jax.nn.sigmoid(gate)


def _scan_chunk(buf, ar, ai, cr, ci, reverse):
    t = buf.shape[0]

    def local(xr, xi, rows):
        within = _rows(xr.shape) % 8
        pr, pi = ar, ai
        for d in (1, 2, 4):
            keep = within < 8 - d if reverse else within >= d
            shift = rows - d if reverse else d
            sr = jnp.where(keep, pltpu.roll(xr, shift, 0), 0.0)
            si = jnp.where(keep, pltpu.roll(xi, shift, 0), 0.0)
            xr, xi = xr + pr * sr - pi * si, xi + pr * si + pi * sr
            pr, pi = pr * pr - pi * pi, 2.0 * pr * pi
        return xr, xi

    xr, xi = local(buf[:, :S5_N], buf[:, S5_N:], t)
    buf[:, :S5_N] = xr
    buf[:, S5_N:] = xi
    edge = _rows((8, S5_N)) == (7 if reverse else 0)
    pr8, pi8 = local(jnp.where(edge, ar, 0.0), jnp.where(edge, ai, 0.0), 8)

    def group(j, c):
        g = t // 8 - 1 - j if reverse else j
        rows = pl.ds(pl.multiple_of(g * 8, 8), 8)
        gr = buf[rows, :S5_N] + pr8 * c[0] - pi8 * c[1]
        gi = buf[rows, S5_N:] + pr8 * c[1] + pi8 * c[0]
        buf[rows, :S5_N] = gr
        buf[rows, S5_N:] = gi
        return (gr[0:1, :], gi[0:1, :]) if reverse else (gr[7:8, :], gi[7:8, :])

    return lax.fori_loop(0, t // 8, group, (cr, ci), unroll=4)


def s5_fwd(p, arow, bbd, cbd, drow, wg, bg):
    s = p.shape[0]
    t = min(T_S5, s)

    def body(u_ref, a_ref, bbd_ref, cbd_ref, d_ref, wg_ref, bg_ref, y_ref, st_ref, carry):
        @pl.when(pl.program_id(0) == 0)
        def _():
            carry[...] = jnp.zeros_like(carry)

        u = u_ref[...]
        st_ref[...] = _nn(u, bbd_ref[...])
        cr, ci = _scan_chunk(st_ref, a_ref[0:1, :], a_ref[1:2, :], carry[0:1, :S5_N], carry[0:1, S5_N:], False)
        carry[0:1, :S5_N] = cr
        carry[0:1, S5_N:] = ci
        y_ref[...] = _s5_out_fn(st_ref[...], u, cbd_ref[...], d_ref[...], wg_ref[...], bg_ref[...]).astype(BF16)

    full = lambda shape: pl.BlockSpec(shape, lambda i: (0,) * len(shape))
    return _call(body, name="s5_fwd", grid=(s // t,),
                 in_specs=[pl.BlockSpec((t, S5_W), lambda i: (i, 2)), full((8, S5_N)), full((S5_W, 2 * S5_N)),
                           full((S5_W, 2 * S5_N)), full((1, S5_W)), full((S5_W, S5_W)), full((1, S5_W))],
                 out_specs=[pl.BlockSpec((t, S5_W), lambda i: (i, 0)), pl.BlockSpec((t, 2 * S5_N), lambda i: (i, 0))],
                 out_shape=[jax.ShapeDtypeStruct((s, S5_W), BF16), jax.ShapeDtypeStruct((s, 2 * S5_N), F32)],
                 scratch_shapes=[pltpu.VMEM((8, 2 * S5_N), F32)],
                 compiler_params=_params("arbitrary"))(p, arow, bbd, cbd, drow, wg, bg)


def s5_bwd(p, st, arow, bbd, cbd, drow, wg, bg, dmix):
    s = p.shape[0]
    t = min(T_S5, s)
    nc = s // t

    def body(u_ref, st_ref, prev_ref, a_ref, bbd_ref, cbd_ref, d_ref, wg_ref, bg_ref, dy_ref,
             du_ref, da_ref, dbbd_ref, dcbd_ref, dd_ref, dwg_ref, dbg_ref, carry, gbuf):
        i = pl.program_id(0)

        @pl.when(i == 0)
        def _():
            carry[...] = jnp.zeros_like(carry)
            for r in (da_ref, dbbd_ref, dcbd_ref, dd_ref, dwg_ref, dbg_ref):
                r[...] = jnp.zeros_like(r)

        u = u_ref[...]
        x = st_ref[...]
        _, vjp = jax.vjp(_s5_out_fn, x, u, cbd_ref[...], d_ref[...], wg_ref[...], bg_ref[...])
        dx, du1, dcbd, dd, dwg, dbg = vjp(dy_ref[...])
        gbuf[...] = dx
        cr, ci = _scan_chunk(gbuf, a_ref[0:1, :], -a_ref[1:2, :], carry[0:1, :S5_N], carry[0:1, S5_N:], True)
        carry[0:1, :S5_N] = cr
        carry[0:1, S5_N:] = ci
        gr, gi = gbuf[:, :S5_N], gbuf[:, S5_N:]
        rid = _rows((t, S5_N))
        has_prev = (i < nc - 1).astype(F32)
        top_r = prev_ref[7:8, :S5_N] * has_prev
        top_i = prev_ref[7:8, S5_N:] * has_prev
        xpr = jnp.where(rid == 0, top_r, pltpu.roll(x[:, :S5_N], 1, 0))
        xpi = jnp.where(rid == 0, top_i, pltpu.roll(x[:, S5_N:], 1, 0))
        da_ref[0:1, :] += jnp.sum(xpr * gr + xpi * gi, axis=0, keepdims=True)
        da_ref[1:2, :] += jnp.sum(xpr * gi - xpi * gr, axis=0, keepdims=True)
        g = jnp.concatenate([gr, gi], axis=1)
        dbbd_ref[...] += _tn(u, g)
        du_ref[...] = (_nt(g, bbd_ref[...]) + du1).astype(BF16)
        dcbd_ref[...] += dcbd
        dd_ref[...] += dd
        dwg_ref[...] += dwg
        dbg_ref[...] += dbg

    full = lambda shape: pl.BlockSpec(shape, lambda i: (0,) * len(shape))
    rev = lambda col: (lambda i: (nc - 1 - i, col))
    prev_map = lambda i: (jnp.maximum((nc - 1 - i) * (t // 8) - 1, 0), 0)
    return _call(body, name="s5_bwd", grid=(nc,),
                 in_specs=[pl.BlockSpec((t, S5_W), rev(2)), pl.BlockSpec((t, 2 * S5_N), rev(0)),
                           pl.BlockSpec((8, 2 * S5_N), prev_map), full((8, S5_N)), full((S5_W, 2 * S5_N)),
                           full((S5_W, 2 * S5_N)), full((1, S5_W)), full((S5_W, S5_W)), full((1, S5_W)),
                           pl.BlockSpec((t, S5_W), rev(1))],
                 out_specs=[pl.BlockSpec((t, S5_W), rev(0)), full((8, S5_N)), full((S5_W, 2 * S5_N)),
                            full((S5_W, 2 * S5_N)), full((1, S5_W)), full((S5_W, S5_W)), full((1, S5_W))],
                 out_shape=[jax.ShapeDtypeStruct((s, S5_W), BF16), jax.ShapeDtypeStruct((8, S5_N), F32),
                            jax.ShapeDtypeStruct((S5_W, 2 * S5_N), F32), jax.ShapeDtypeStruct((S5_W, 2 * S5_N), F32),
                            jax.ShapeDtypeStruct((1, S5_W), F32), jax.ShapeDtypeStruct((S5_W, S5_W), F32),
                            jax.ShapeDtypeStruct((1, S5_W), F32)],
                 scratch_shapes=[pltpu.VMEM((8, 2 * S5_N), F32), pltpu.VMEM((t, 2 * S5_N), F32)],
                 compiler_params=_params("arbitrary"))(p, st, st, arow, bbd, cbd, drow, wg, bg, dmix)


def _cum_steps(s):
    return int(math.ceil(math.log2(s)))


V_BLK = (2 * GM_W + S5_W + 2 * FX_W) // 128


AUG = 2 * HD
BIAS_COL = HD
FQ_COL = HD + 3
PAIR_W = 256


def _split3(f):
    hi = f.astype(BF16).astype(F32)
    r = f - hi
    mid = r.astype(BF16).astype(F32)
    lo = (r - mid).astype(BF16).astype(F32)
    return hi, mid, lo


def fox_prep(p, bf):
    s = p.shape[0]
    ts = min(TS, s)
    scale = HD ** -0.5

    def body(q0_ref, q1_ref, k0_ref, k1_ref, v0_ref, v1_ref, f_ref, bf_ref,
             qa_ref, ka_ref, qat_ref, kat_ref, vt_ref, carry):
        @pl.when(pl.program_id(0) == 0)
        def _():
            carry[...] = jnp.zeros_like(carry)

        lane = _lanes((ts, 128))
        lf = jax.nn.log_sigmoid(f_ref[...] + bf_ref[...])
        acc = jnp.where(lane < FX_H, lf, 0.0)
        rid = _rows((ts, 128))
        for k in range(_cum_steps(ts)):
            d = 1 << k
            acc = acc + jnp.where(rid >= d, pltpu.roll(acc, d, 0), 0.0)
        acc = acc + carry[0:1, :]
        carry[0:1, :] = acc[ts - 1:ts, :]

        low = lane < HD
        for h in range(FX_H):
            blk, pos = divmod(h, 4)
            pair = slice((pos // 2) * 128, (pos // 2) * 128 + 128)
            hi, mid, lo = _split3(acc[:, h:h + 1])
            one = jnp.ones((ts, 1), F32)

            def augment(ref, cols):
                x = ref[:, pair]
                if pos % 2:
                    x = pltpu.roll(x, HD, 1)
                out = jnp.where(low, x, 0.0)
                for j, cval in enumerate(cols):
                    out = jnp.where(lane == HD + j, cval, out)
                return out

            qa = augment((q0_ref, q1_ref)[blk], (one, one, one, hi, mid, lo))
            qa = jnp.where(low, qa * scale, qa)
            ka = augment((k0_ref, k1_ref)[blk], (-hi, -mid, -lo, one, one, one))
            cs = slice(h * AUG, (h + 1) * AUG)
            qa_ref[:, cs] = qa.astype(BF16)
            ka_ref[:, cs] = ka.astype(BF16)
            qat_ref[cs, :] = jnp.transpose(qa).astype(BF16)
            kat_ref[cs, :] = jnp.transpose(ka).astype(BF16)
        for j in range(FX_H // 2):
            vref = (v0_ref, v1_ref)[j // 2]
            vt_ref[j * 128:(j + 1) * 128, :] = jnp.transpose(vref[:, (j % 2) * 128:(j % 2) * 128 + 128]).astype(BF16)

    q_blk = (2 * GM_W + S5_W) // PAIR_W
    col = lambda b: pl.BlockSpec((ts, PAIR_W), lambda i: (i, b))
    wide = FX_H * AUG
    return _call(body, name="fox_prep", grid=(s // ts,),
                 in_specs=[col(q_blk), col(q_blk + 1), col(q_blk + 2), col(q_blk + 3), col(q_blk + 4), col(q_blk + 5),
                           pl.BlockSpec((ts, 128), lambda i: (i, FF_COL // 128)), pl.BlockSpec((1, 128), lambda i: (0, 0))],
                 out_specs=[pl.BlockSpec((ts, wide), lambda i: (i, 0)), pl.BlockSpec((ts, wide), lambda i: (i, 0)),
                            pl.BlockSpec((wide, ts), lambda i: (0, i)), pl.BlockSpec((wide, ts), lambda i: (0, i)),
                            pl.BlockSpec((FX_W, ts), lambda i: (0, i))],
                 out_shape=[jax.ShapeDtypeStruct((s, wide), BF16), jax.ShapeDtypeStruct((s, wide), BF16),
                            jax.ShapeDtypeStruct((wide, s), BF16), jax.ShapeDtypeStruct((wide, s), BF16),
                            jax.ShapeDtypeStruct((FX_W, s), BF16)],
                 scratch_shapes=[pltpu.VMEM((8, 128), F32)],
                 compiler_params=_params("arbitrary"))(p, p, p, p, p, p, p, bf)


def fox_prep_grad(p, bf, dfq, dfk):
    s = p.shape[0]
    ts = min(TS, s)
    ns = s // ts

    def body(f_ref, bf_ref, dfq_ref, dfk_ref, df_ref, dbf_ref, carry):
        @pl.when(pl.program_id(0) == 0)
        def _():
            carry[...] = jnp.zeros_like(carry)
            dbf_ref[...] = jnp.zeros_like(dbf_ref)

        lane = _lanes((ts, 128))
        acc = jnp.zeros((ts, 128), F32)
        for h in range(FX_H):
            c = (h // 2) * 128 + h % 2
            acc = jnp.where(lane == h, dfq_ref[:, c:c + 1] + dfk_ref[:, c:c + 1], acc)
        rid = _rows((ts, 128))
        for k in range(_cum_steps(ts)):
            d = 1 << k
            acc = acc + jnp.where(rid < ts - d, pltpu.roll(acc, ts - d, 0), 0.0)
        acc = acc + carry[0:1, :]
        carry[0:1, :] = acc[0:1, :]
        z = f_ref[...] + bf_ref[...]
        df = jnp.where(lane < FX_H, acc * jax.nn.sigmoid(-z), 0.0)
        df_ref[...] = df.astype(BF16)
        dbf_ref[...] += jnp.sum(df, axis=0, keepdims=True)

    rev = lambda i: (ns - 1 - i, 0)
    return _call(body, name="fox_prep_grad", grid=(ns,),
                 in_specs=[pl.BlockSpec((ts, 128), lambda i: (ns - 1 - i, FF_COL // 128)),
                           pl.BlockSpec((1, 128), lambda i: (0, 0)),
                           pl.BlockSpec((ts, FX_W), rev), pl.BlockSpec((ts, FX_W), rev)],
                 out_specs=[pl.BlockSpec((ts, 128), rev), pl.BlockSpec((1, 128), lambda i: (0, 0))],
                 out_shape=[jax.ShapeDtypeStruct((s, 128), BF16), jax.ShapeDtypeStruct((1, 128), F32)],
                 scratch_shapes=[pltpu.VMEM((8, 128), F32)],
                 compiler_params=_params("arbitrary"))(p, bf, dfq, dfk)


def attn(qat, ka, vt):
    s = ka.shape[0]
    tq = min(TQ, s)
    nq = s // tq

    nh = ATTN_HEADS

    def body(qat_ref, ka_ref, vt_ref, o_ref, lse_ref):
        qi = pl.program_id(1)
        causal = _rows((tq, tq)) <= _lanes((tq, tq))
        lse_ref[...] = jnp.zeros_like(lse_ref)

        def step(kj, carry, masked):
            off = pl.multiple_of(kj * tq, tq)
            out = []
            for hh in range(nh):
                m, l, acc = carry[hh]
                st = jnp.dot(ka_ref[pl.ds(off, tq), hh * AUG:(hh + 1) * AUG], qat_ref[hh * AUG:(hh + 1) * AUG, :],
                             preferred_element_type=F32)
                if masked:
                    st = jnp.where(causal, st, NEG_INF)
                m_new = jnp.maximum(m, jnp.max(st, axis=0, keepdims=True))
                alpha = jnp.exp(m - m_new)
                pt = jnp.exp(st - m_new)
                v = vt_ref[hh * HD:(hh + 1) * HD, pl.ds(off, tq)]
                out.append((m_new, alpha * l + jnp.sum(pt, axis=0, keepdims=True),
                            alpha * acc + jnp.dot(v, pt.astype(BF16), preferred_element_type=F32)))
            return tuple(out)

        init = tuple((jnp.full((1, tq), NEG_INF, F32), jnp.zeros((1, tq), F32), jnp.zeros((HD, tq), F32))
                     for _ in range(nh))
        carry = lax.fori_loop(0, qi, lambda kj, c: step(kj, c, False), init)
        carry = step(qi, carry, True)
        for hh in range(nh):
            m, l, _ = carry[hh]
            lse_ref[hh // 2, hh % 2:hh % 2 + 1, :] = m + jnp.log(l)
        for j in range(nh // 2):
            pair = jnp.concatenate([carry[2 * j][2] / carry[2 * j][1], carry[2 * j + 1][2] / carry[2 * j + 1][1]], axis=0)
            o_ref[:, j * 128:(j + 1) * 128] = jnp.transpose(pair).astype(BF16)

    return _call(body, name="attn", grid=(FX_H // nh, nq),
                 in_specs=[pl.BlockSpec((nh * AUG, tq), lambda h, i: (h, i)),
                           pl.BlockSpec((s, nh * AUG), lambda h, i: (0, h)),
                           pl.BlockSpec((nh * HD, s), lambda h, i: (h, 0))],
                 out_specs=[pl.BlockSpec((tq, nh * HD), lambda h, i: (i, h)),
                            pl.BlockSpec((nh // 2, 8, tq), lambda h, i: (h, 0, i))],
                 out_shape=[jax.ShapeDtypeStruct((s, FX_W), BF16), jax.ShapeDtypeStruct((FX_H // 2, 8, s), F32)],
                 compiler_params=_params("arbitrary", "arbitrary"))(qat, ka, vt)


def attn_grad(qa, qat, ka, kat, p, o, lse, dmix):
    s = qa.shape[0]
    tq = min(TQ, s)
    nq = s // tq
    scale = HD ** -0.5

    def body(qa_ref, qat_ref, ka_ref, kat_ref, v_ref, o_ref, lse_ref, do_ref,
             dq_ref, dk_ref, dv_ref, dfq_ref, dfk_ref, dot_scr, delta, dqt):
        kj = pl.program_id(1)
        lane = _lanes((tq, 128))
        low = lane < HD
        causal = _rows((tq, tq)) <= _lanes((tq, tq))

        @pl.when(kj == 0)
        def _():
            dqt[...] = jnp.zeros_like(dqt)
            delta[...] = jnp.zeros_like(delta)

            def prep(c, _):
                rows = pl.ds(pl.multiple_of(c * tq, tq), tq)
                do = do_ref[rows, :]
                pt = jnp.transpose(do * o_ref[rows, :].astype(F32))
                delta[0:1, rows] = jnp.sum(pt[:HD], axis=0, keepdims=True)
                delta[1:2, rows] = jnp.sum(pt[HD:], axis=0, keepdims=True)
                dot_scr[:, rows] = jnp.transpose(do).astype(BF16)
                return 0

            lax.fori_loop(0, nq, prep, 0)

        v = v_ref[...]
        vms = [jnp.where(low, v, 0.0).astype(BF16), jnp.where(low, 0.0, v).astype(BF16)]

        def tile(qi, carry, masked):
            cols = pl.ds(pl.multiple_of(qi * tq, tq), tq)
            do = do_ref[cols, :].astype(BF16)
            out = []
            for hh in range(2):
                cs = slice(hh * AUG, (hh + 1) * AUG)
                dka, dv = carry[hh]
                st = jnp.dot(ka_ref[:, cs], qat_ref[cs, cols], preferred_element_type=F32)
                if masked:
                    st = jnp.where(causal, st, NEG_INF)
                pt = jnp.exp(st - lse_ref[0, hh:hh + 1, cols])
                dv = dv + jnp.dot(pt.astype(BF16), do, preferred_element_type=F32)
                dpt = jnp.dot(vms[hh], dot_scr[:, cols], preferred_element_type=F32)
                dsb = (pt * (dpt - delta[hh:hh + 1, cols])).astype(BF16)
                dka = dka + jnp.dot(dsb, qa_ref[cols, cs], preferred_element_type=F32)
                dqt[hh, :, cols] += jnp.dot(kat_ref[cs, :], dsb, preferred_element_type=F32)
                out.append((dka, dv))
            return tuple(out)

        init = tuple((jnp.zeros((tq, AUG), F32), jnp.zeros((tq, 128), F32)) for _ in range(2))
        carry = tile(kj, init, True)
        carry = lax.fori_loop(kj + 1, nq, lambda qi, c: tile(qi, c, False), carry)
        dks = [carry[0][0], carry[1][0]]
        dvs = [carry[0][1], carry[1][1]]
        dv_ref[...] = jnp.where(low, dvs[0], dvs[1]).astype(BF16)
        dk_ref[...] = jnp.where(low, dks[0], pltpu.roll(dks[1], HD, 1)).astype(BF16)
        dfk_ref[...] = jnp.where(lane == 0, -dks[0][:, BIAS_COL:BIAS_COL + 1],
                                 jnp.where(lane == 1, -dks[1][:, BIAS_COL:BIAS_COL + 1], 0.0))

        @pl.when(kj == nq - 1)
        def _():
            def finish(c, _):
                rows = pl.ds(pl.multiple_of(c * tq, tq), tq)
                t0 = jnp.transpose(dqt[0, :, rows])
                t1 = jnp.transpose(dqt[1, :, rows])
                dq_ref[rows, :] = (jnp.where(low, t0, pltpu.roll(t1, HD, 1)) * scale).astype(BF16)
                dfq_ref[rows, :] = jnp.where(lane == 0, t0[:, FQ_COL:FQ_COL + 1],
                                             jnp.where(lane == 1, t1[:, FQ_COL:FQ_COL + 1], 0.0))
                return 0

            lax.fori_loop(0, nq, finish, 0)

    seq128 = lambda blk: pl.BlockSpec((s, 128), lambda h, j: (0, blk + h))
    tile128 = pl.BlockSpec((tq, 128), lambda h, j: (j, h))
    out_b = jax.ShapeDtypeStruct((s, FX_W), BF16)
    out_f = jax.ShapeDtypeStruct((s, FX_W), F32)
    return _call(body, name="attn_grad", grid=(FX_H // 2, nq),
                 in_specs=[pl.BlockSpec((s, 2 * AUG), lambda h, j: (0, h)), pl.BlockSpec((2 * AUG, s), lambda h, j: (h, 0)),
                           pl.BlockSpec((tq, 2 * AUG), lambda h, j: (j, h)), pl.BlockSpec((2 * AUG, tq), lambda h, j: (h, j)),
                           pl.BlockSpec((tq, 128), lambda h, j: (j, V_BLK + h)), seq128(0),
                           pl.BlockSpec((1, 8, s), lambda h, j: (h, 0, 0)), seq128(4)],
                 out_specs=[seq128(0), tile128, tile128, seq128(0), tile128],
                 out_shape=[out_b, out_b, out_b, out_f, out_f],
                 scratch_shapes=[pltpu.VMEM((128, s), BF16), pltpu.VMEM((8, s), F32), pltpu.VMEM((2, AUG, s), F32)],
                 compiler_params=_params("arbitrary", "arbitrary"))(qa, qat, ka, kat, p, o, lse, dmix)


def _shift_down(a, prev8, k):
    r = pltpu.roll(a, k, 0)
    top = jnp.where(_rows(prev8.shape) < k, pltpu.roll(prev8, k, 0), r[0:8])
    return jnp.concatenate([top, r[8:]], axis=0)


def _shift_up(a, next8, k):
    t = a.shape[0]
    r = pltpu.roll(a, t - k, 0)
    bot = jnp.where(_rows(next8.shape) >= 8 - k, pltpu.roll(next8, 8 - k, 0), r[t - 8:t])
    return jnp.concatenate([r[:t - 8], bot], axis=0)


def _conv(a, prev8, cw, cb):
    return cb + cw[0:1, :] * _shift_down(a, prev8, 2) + cw[1:2, :] * _shift_down(a, prev8, 1) + cw[2:3, :] * a


GELU_K0 = math.sqrt(2.0 / math.pi)
GELU_K1 = GELU_K0 * 0.044715


def _gelu_parts(c):
    c2 = c * c
    return c2, 0.5 + 0.5 * jnp.tanh(c * (GELU_K0 + GELU_K1 * c2))


def conv_fwd(up, cw, cb):
    s = up.shape[0]
    ts = min(TS_C, s)

    def body(a_ref, g_ref, cw_ref, cb_ref, o_ref, c_ref, halo):
        @pl.when(pl.program_id(0) == 0)
        def _():
            halo[...] = jnp.zeros_like(halo)

        a = a_ref[...]
        c = _conv(a, halo[...], cw_ref[...], cb_ref[...])
        _, h = _gelu_parts(c)
        c_ref[...] = c
        o_ref[...] = (c * h * g_ref[...]).astype(BF16)
        halo[...] = a[ts - 8:ts, :]

    tile = pl.BlockSpec((ts, DFF), lambda i: (i, 0))
    return _call(body, name="conv_fwd", grid=(s // ts,),
                 in_specs=[tile, pl.BlockSpec((ts, DFF), lambda i: (i, 1)),
                           pl.BlockSpec((3, DFF), lambda i: (0, 0)), pl.BlockSpec((1, DFF), lambda i: (0, 0))],
                 out_specs=[tile, tile],
                 out_shape=[jax.ShapeDtypeStruct((s, DFF), BF16), jax.ShapeDtypeStruct((s, DFF), F32)],
                 scratch_shapes=[pltpu.VMEM((8, DFF), F32)],
                 compiler_params=_params("arbitrary"))(up, up, cw, cb)


def conv_bwd(up, c, cw, dact):
    s = up.shape[0]
    ts = min(TS_C, s)
    ns = s // ts

    def body(a_ref, g_ref, c_ref, cw_ref, dact_ref, dup_ref, dcw_ref, dcb_ref, halo):
        @pl.when(pl.program_id(0) == 0)
        def _():
            halo[...] = jnp.zeros_like(halo)
            dcw_ref[...] = jnp.zeros_like(dcw_ref)
            dcb_ref[...] = jnp.zeros_like(dcb_ref)

        a = a_ref[...]
        cw = cw_ref[...]
        cv = c_ref[...]
        dact = dact_ref[...]
        c2, h = _gelu_parts(cv)
        dup_ref[:, DFF:] = (dact * (cv * h)).astype(BF16)
        dgel = h + cv * (2.0 * h * (1.0 - h)) * (GELU_K0 + 3.0 * GELU_K1 * c2)
        dc = dact * g_ref[...] * dgel
        up1 = _shift_up(dc, halo[...], 1)
        up2 = _shift_up(dc, halo[...], 2)
        dup_ref[:, :DFF] = (cw[2:3, :] * dc + cw[1:2, :] * up1 + cw[0:1, :] * up2).astype(BF16)
        dcw_ref[0:1, :] += jnp.sum(a * up2, axis=0, keepdims=True)
        dcw_ref[1:2, :] += jnp.sum(a * up1, axis=0, keepdims=True)
        dcw_ref[2:3, :] += jnp.sum(a * dc, axis=0, keepdims=True)
        dcb_ref[...] += jnp.sum(dc, axis=0, keepdims=True)
        halo[...] = dc[0:8, :]

    rev = lambda col: (lambda i: (ns - 1 - i, col))
    return _call(body, name="conv_bwd", grid=(ns,),
                 in_specs=[pl.BlockSpec((ts, DFF), rev(0)), pl.BlockSpec((ts, DFF), rev(1)),
                           pl.BlockSpec((ts, DFF), rev(0)), pl.BlockSpec((3, DFF), lambda i: (0, 0)),
                           pl.BlockSpec((ts, DFF), rev(0))],
                 out_specs=[pl.BlockSpec((ts, 2 * DFF), rev(0)), pl.BlockSpec((3, DFF), lambda i: (0, 0)),
                            pl.BlockSpec((1, DFF), lambda i: (0, 0))],
                 out_shape=[jax.ShapeDtypeStruct((s, 2 * DFF), BF16), jax.ShapeDtypeStruct((3, DFF), F32),
                            jax.ShapeDtypeStruct((1, DFF), F32)],
                 scratch_shapes=[pltpu.VMEM((8, DFF), F32)],
                 compiler_params=_params("arbitrary"))(up, up, c, cw, dact)


def _blockdiag_expand(m):
    m4 = m.reshape(S5_H, 2, S5_G, S5_P)
    eye = jnp.eye(S5_G, dtype=bool)[:, None, None, :, None]
    return jnp.where(eye, m4[None], 0.0).reshape(S5_W, 2 * S5_N)


def _blockdiag_extract(mbd):
    m5 = mbd.reshape(S5_G, S5_H, 2, S5_G, S5_P)
    diag = jnp.stack([m5[g, :, :, g, :] for g in range(S5_G)], axis=2)
    return diag.reshape(S5_H, 2 * S5_N)


def _c_expand(c_re, c_im):
    c4 = jnp.stack([c_re, c_im], axis=2)
    eye = jnp.eye(S5_G, dtype=bool)[:, None, None, :, None]
    return jnp.where(eye, c4[:, :, :, None, :], 0.0).reshape(S5_W, 2 * S5_N)


def _c_extract(cbd):
    m5 = cbd.reshape(S5_G, S5_H, 2, S5_G, S5_P)
    d = jnp.stack([m5[g, :, :, g, :] for g in range(S5_G)], axis=0)
    return d[:, :, 0, :], d[:, :, 1, :]


def _glu_expand(w):
    eye = jnp.eye(S5_G, dtype=bool)[:, None, :, None]
    return jnp.where(eye, w[:, :, None, :], 0.0).reshape(S5_W, S5_W)


def _glu_extract(wbd):
    m4 = wbd.reshape(S5_G, S5_H, S5_G, S5_H)
    return jnp.stack([m4[g, :, g, :] for g in range(S5_G)], axis=0)


SMALL = ("b_f", "gm_ln_g", "gm_ln_b", "gm_w_s", "gm_b_s", "s5_lam_re", "s5_lam_im", "s5_log_dt", "s5_b_re", "s5_b_im",
         "s5_c_re", "s5_c_im", "s5_d", "s5_w_glu", "s5_b_glu", "ln1_g", "ln1_b", "conv_b", "ln2_g", "ln2_b")


def _layer_operands(sp, l):
    f = {}
    f["bf"] = jnp.pad(sp["b_f"][l][None, :], ((0, 0), (0, 128 - FX_H)))
    f["gm_lg"] = sp["gm_ln_g"][l].reshape(1, GM_W)
    f["gm_lb"] = sp["gm_ln_b"][l].reshape(1, GM_W)
    f["gm_ws"] = sp["gm_w_s"][l]
    f["gm_bst"] = sp["gm_b_s"][l].T
    f["lr"] = sp["s5_lam_re"][l].reshape(1, S5_N)
    f["li"] = sp["s5_lam_im"][l].reshape(1, S5_N)
    f["ldt"] = jnp.repeat(sp["s5_log_dt"][l], S5_P).reshape(1, S5_N)
    bt = lambda b: jnp.transpose(b, (2, 0, 1)).reshape(S5_H, S5_N)
    f["bt"] = jnp.concatenate([bt(sp["s5_b_re"][l]), bt(sp["s5_b_im"][l])], axis=1)
    f["cbd"] = _c_expand(sp["s5_c_re"][l], sp["s5_c_im"][l])
    f["drow"] = sp["s5_d"][l].reshape(1, S5_W)
    f["wg"] = _glu_expand(sp["s5_w_glu"][l])
    f["bg"] = sp["s5_b_glu"][l].reshape(1, S5_W)
    for n in ("ln1_g", "ln1_b", "ln2_g", "ln2_b"):
        f[n] = sp[n][l][None, :]
    f["cb"] = sp["conv_b"][l][None, :]
    return f


def layer_fwd_mix(x, mod, w_in, f):
    p = mm_nn(x, w_in, NP, "in_proj", mod=mod, rows=(0, 1))
    ygm = gm_fwd(p, f["gm_lg"], f["gm_lb"], f["gm_ws"], f["gm_bst"])
    arow, bbt = s5_prep_fwd(f["lr"], f["li"], f["ldt"], f["bt"])
    bbd = _blockdiag_expand(bbt)
    ys5, st = s5_fwd(p, arow, bbd, f["cbd"], f["drow"], f["wg"], f["bg"])
    qa, ka, qat, kat, vt = fox_prep(p, f["bf"])
    yfx, lse = attn(qat, ka, vt)
    mixcat = jnp.concatenate([ygm, ys5, yfx], axis=1)
    return mixcat, dict(f=f, x=x, p=p, arow=arow, bbd=bbd, st=st, qa=qa, ka=ka, qat=qat, kat=kat, yfx=yfx, lse=lse,
                        mixcat=mixcat)


def layer_fwd_rest(x, mixcat, mod, w, f, saved):
    mix = mm_nn(mixcat, w["w_out"], D, "out_proj")
    x1 = post_fwd(x, mix, mod, 2, f["ln1_g"], f["ln1_b"], "post1_fwd")
    up = mm_nn(x1, w["w_up"], DFF // 2, "up_proj", mod=mod, rows=(3, 4))
    act, conv = conv_fwd(up, w["conv_w"], f["cb"])
    ffn = mm_nn(act, w["w_down"], D, "down_proj")
    x2 = post_fwd(x1, ffn, mod, 5, f["ln2_g"], f["ln2_b"], "post2_fwd")
    return x2, dict(saved, mix=mix, x1=x1, up=up, conv=conv, act=act, ffn=ffn)


def layer_fwd(x, mod, w, f):
    mixcat, saved = layer_fwd_mix(x, mod, w["w_in"], f)
    return layer_fwd_rest(x, mixcat, mod, w, f, saved)


def layer_bwd_ffn(dx, sv, mod, w):
    f = sv["f"]
    dx1, dffn, dg2, dlg2, dlb2 = post_bwd(sv["x1"], sv["ffn"], mod, 5, f["ln2_g"], f["ln2_b"], dx, "post2_bwd")
    g_down = mm_tn(sv["act"], dffn, D // 2, "down_dw")
    dact = mm_nt(dffn, w["w_down"], "down_dx")
    dup, dcw, dcb = conv_bwd(sv["up"], sv["conv"], w["conv_w"], dact)
    g_up = mm_tn(sv["x1"], dup, DFF // 2, "up_dw", mod=mod, rows=(3, 4))
    dx1, dsh2, dsc2 = mm_nt_mod(dup, w["w_up"], sv["x1"], dx1, mod, (3, 4), "up_dx")
    return dx1, dict(w_up=g_up, w_down=g_down, conv_w=dcw), dict(dsh2=dsh2, dsc2=dsc2, dg2=dg2, conv_b=dcb[0],
                                                                 ln2_g=dlg2[0], ln2_b=dlb2[0])


def layer_bwd_mix(dx1, sv, mod, w, part):
    f = sv["f"]
    dx0, dmix, dg1, dlg1, dlb1 = post_bwd(sv["x"], sv["mix"], mod, 2, f["ln1_g"], f["ln1_b"], dx1, "post1_bwd")
    g_out = mm_tn(sv["mixcat"], dmix, D, "out_dw")
    dmc = mm_nt(dmix, w["w_out"], "out_dx")
    duv, dgm_lg, dgm_lb, dgm_ws, dgm_bst = gm_bwd(sv["p"], f["gm_lg"], f["gm_lb"], f["gm_ws"], f["gm_bst"], dmc)
    du5, da, dbbd, dcbd, dd5, dwg, dbg = s5_bwd(sv["p"], sv["st"], sv["arow"], sv["bbd"], f["cbd"], f["drow"],
                                                f["wg"], f["bg"], dmc)
    dlr, dli, dldt, dbt = s5_prep_bwd(f["lr"], f["li"], f["ldt"], f["bt"], da, _blockdiag_extract(dbbd))
    dq, dk, dv, dfq, dfk = attn_grad(sv["qa"], sv["qat"], sv["ka"], sv["kat"], sv["p"], sv["yfx"], sv["lse"], dmc)
    dff, dbf = fox_prep_grad(sv["p"], f["bf"], dfq, dfk)
    dp = jnp.concatenate([duv, du5, dq, dk, dv, dff], axis=1)
    g_in = mm_tn(sv["x"], dp, NP, "in_dw", mod=mod, rows=(0, 1))
    dx, dsh1, dsc1 = mm_nt_mod(dp, w["w_in"], sv["x"], dx0, mod, (0, 1), "in_dx")

    dmod = jnp.concatenate([dsh1, dsc1, dg1, part["dsh2"], part["dsc2"], part["dg2"]], axis=0)
    dc_re, dc_im = _c_extract(dcbd)
    dbt4 = dbt.reshape(S5_H, 2, S5_G, S5_P)
    vals = dict(b_f=dbf[0, :FX_H], gm_ln_g=dgm_lg.reshape(GM_H, HD), gm_ln_b=dgm_lb.reshape(GM_H, HD),
                gm_w_s=dgm_ws, gm_b_s=dgm_bst.T, s5_lam_re=dlr.reshape(S5_G, S5_P),
                s5_lam_im=dli.reshape(S5_G, S5_P), s5_log_dt=dldt[0, :S5_G],
                s5_b_re=jnp.transpose(dbt4[:, 0], (1, 2, 0)), s5_b_im=jnp.transpose(dbt4[:, 1], (1, 2, 0)),
                s5_c_re=dc_re, s5_c_im=dc_im, s5_d=dd5.reshape(S5_G, S5_H), s5_w_glu=_glu_extract(dwg),
                s5_b_glu=dbg.reshape(S5_G, S5_H), ln1_g=dlg1[0], ln1_b=dlb1[0], conv_b=part["conv_b"],
                ln2_g=part["ln2_g"], ln2_b=part["ln2_b"])
    return dx, dict(w_in=g_in, w_out=g_out), vals, dmod


def layer_bwd(dx, sv, mod, w):
    dx1, g_ffn, part = layer_bwd_ffn(dx, sv, mod, w)
    dx, g_mix, vals, dmod = layer_bwd_mix(dx1, sv, mod, w, part)
    return dx, dict(g_ffn, **g_mix), vals, dmod


def local_step(x, target, mods, big, sp):
    saved = []
    for l in range(DEPTH):
        x, sv = layer_fwd(x, mods[l], big[l], _layer_operands(sp, l))
        saved.append(sv)
    loss_tile, dx = loss_kernel(x, target)
    gbig, vals, dmods = [None] * DEPTH, [None] * DEPTH, [None] * DEPTH
    for l in reversed(range(DEPTH)):
        dx, gbig[l], vals[l], dmods[l] = layer_bwd(dx, saved[l], mods[l], big[l])
    gsm = {n: jnp.stack([v[n] for v in vals]) for n in SMALL}
    return loss_tile, dx, gbig, gsm, jnp.stack(dmods)


def _my_index():
    return 4 * lax.axis_index("x") + 2 * lax.axis_index("y") + lax.axis_index("c")


def exchange(tensors, scatter, name):
    n = len(tensors)

    def body(*refs):
        ins, outs = refs[:n], refs[n:2 * n]
        send_sems, recv_sems, local_sems = refs[2 * n:]
        x, y, c = lax.axis_index("x"), lax.axis_index("y"), lax.axis_index("c")
        me = 4 * x + 2 * y + c
        local = []
        for t in range(n):
            cp = pltpu.make_async_copy(ins[t].at[me] if scatter else ins[t], outs[t].at[me], local_sems.at[t])
            cp.start()
            local.append(cp)
        remote = []
        for m in range(1, NDEV):
            px = 1 - x if m & 4 else x
            py = 1 - y if m & 2 else y
            pc = 1 - c if m & 1 else c
            peer = 4 * px + 2 * py + pc
            for t in range(n):
                k = t * (NDEV - 1) + m - 1
                cp = pltpu.make_async_remote_copy(
                    src_ref=ins[t].at[peer] if scatter else ins[t], dst_ref=outs[t].at[me],
                    send_sem=send_sems.at[k], recv_sem=recv_sems.at[k],
                    device_id=(px, py, pc), device_id_type=MESH_IDS)
                cp.start()
                remote.append(cp)
        for cp in remote:
            cp.wait()
        for cp in local:
            cp.wait()

    hbm = pl.BlockSpec(memory_space=pltpu.HBM)
    out_shape = [jax.ShapeDtypeStruct(t.shape if scatter else (NDEV,) + t.shape, t.dtype) for t in tensors]
    return _call(body, name=name, in_specs=[hbm] * n, out_specs=[hbm] * n, out_shape=out_shape,
                 scratch_shapes=[pltpu.SemaphoreType.DMA((n * (NDEV - 1),)), pltpu.SemaphoreType.DMA((n * (NDEV - 1),)),
                                 pltpu.SemaphoreType.DMA((n,))])(*tensors)


def _peers():
    x, y, c = lax.axis_index("x"), lax.axis_index("y"), lax.axis_index("c")
    out = []
    for m in range(1, NDEV):
        px = 1 - x if m & 4 else x
        py = 1 - y if m & 2 else y
        pc = 1 - c if m & 1 else c
        out.append(((px, py, pc), 4 * px + 2 * py + pc))
    return 4 * x + 2 * y + c, out


def _split_copies(v_refs, land_refs, send_sems, recv_sems, scatter):
    me, peers = _peers()
    return [pltpu.make_async_remote_copy(
        src_ref=v_ref.at[idx] if scatter else v_ref, dst_ref=land_ref.at[me],
        send_sem=send_sems.at[t * (NDEV - 1) + k], recv_sem=recv_sems.at[t * (NDEV - 1) + k],
        device_id=pos, device_id_type=MESH_IDS)
        for t, (v_ref, land_ref) in enumerate(zip(v_refs, land_refs)) for k, (pos, idx) in enumerate(peers)]


_HBM_SPEC = pl.BlockSpec(memory_space=pltpu.HBM)
_SEM_SPEC = pl.BlockSpec(memory_space=pltpu.SEMAPHORE)
_SPLIT_EFFECT = pltpu.SideEffectType.DATAFLOW_SIDE_EFFECTING


def exchange_start(tensors, scatter, name):
    n = len(tensors)
    land_shapes = [t.shape if scatter else (NDEV,) + t.shape for t in tensors]

    def body(*refs):
        v_refs, land_refs = refs[:n], refs[n:2 * n]
        send_sems, recv_sems = refs[2 * n], refs[2 * n + 1]
        token = refs[-1]
        for cp in _split_copies(v_refs, land_refs, send_sems, recv_sems, scatter):
            cp.start()
        token[...] = jnp.zeros_like(token)

    sems = pltpu.SemaphoreType.DMA((n * (NDEV - 1),))
    out = _call(
        body, name=name,
        out_shape=(sems, sems, *[pltpu.HBM(t.shape, t.dtype) for t in tensors],
                   *[pltpu.HBM(s, t.dtype) for s, t in zip(land_shapes, tensors)], jax.ShapeDtypeStruct((8, 128), F32)),
        in_specs=(_HBM_SPEC,) * (2 * n),
        out_specs=(_SEM_SPEC, _SEM_SPEC) + (_HBM_SPEC,) * (2 * n) + (pl.BlockSpec(memory_space=pltpu.VMEM),),
        input_output_aliases={i: i + 2 for i in range(2 * n)},
        compiler_params=pltpu.CompilerParams(has_side_effects=_SPLIT_EFFECT),
    )(*[pltpu.with_memory_space_constraint(t, pltpu.HBM) for t in tensors],
      *[pltpu.with_memory_space_constraint(lax.empty(s, t.dtype), pltpu.HBM) for s, t in zip(land_shapes, tensors)])
    return out[0], out[1], list(out[2:2 + n]), list(out[2 + n:2 + 2 * n]), out[-1]


def exchange_wait(started, after, scatter, name):
    send_sems, recv_sems, v_thru, land_thru, _ = started
    n = len(v_thru)

    def body(*refs):
        v_refs, land_refs = refs[:n], refs[n:2 * n]
        for cp in _split_copies(v_refs, land_refs, refs[2 * n], refs[2 * n + 1], scatter):
            cp.wait_send()
            cp.wait_recv()

    out = _call(
        body, name=name,
        out_shape=tuple(pltpu.HBM(t.shape, t.dtype) for t in v_thru + land_thru),
        in_specs=(_HBM_SPEC,) * (2 * n) + (_SEM_SPEC, _SEM_SPEC, pl.BlockSpec(memory_space=pl.ANY)),
        out_specs=(_HBM_SPEC,) * (2 * n), input_output_aliases={i: i for i in range(2 * n)},
        compiler_params=pltpu.CompilerParams(has_side_effects=_SPLIT_EFFECT),
    )(*v_thru, *land_thru, send_sems, recv_sems, after)
    return list(out[:n]), list(out[n:])


def mod_slices(c_all, w_ada, b_loc):
    nl, _, nc = w_ada.shape

    def body(c_ref, w_ref, b_ref, o_ref):
        cv = c_ref[...]
        o_ref[0] = _nn(cv * jax.nn.sigmoid(cv), w_ref[0]) + b_ref[0]

    return _call(body, name="mod_slices", grid=(nl,),
                 in_specs=[pl.BlockSpec((NDEV, D), lambda l: (0, 0)), pl.BlockSpec((1, D, nc), lambda l: (l, 0, 0)),
                           pl.BlockSpec((1, 1, nc), lambda l: (l, 0, 0))],
                 out_specs=pl.BlockSpec((1, NDEV, nc), lambda l: (l, 0, 0)),
                 out_shape=jax.ShapeDtypeStruct((nl, NDEV, nc), F32),
                 compiler_params=_params("arbitrary"))(c_all, w_ada, b_loc.reshape(nl, 1, nc))


def ada_grad(c_all, dm_loc):
    nl, _, nc = dm_loc.shape

    def body(c_ref, d_ref, o_ref):
        cv = c_ref[...]
        o_ref[0] = _tn(cv * jax.nn.sigmoid(cv), d_ref[0])

    return _call(body, name="ada_grad", grid=(nl,),
                 in_specs=[pl.BlockSpec((NDEV, D), lambda l: (0, 0)), pl.BlockSpec((1, NDEV, nc), lambda l: (l, 0, 0))],
                 out_specs=pl.BlockSpec((1, D, nc), lambda l: (l, 0, 0)),
                 out_shape=jax.ShapeDtypeStruct((nl, D, nc), F32),
                 compiler_params=_params("arbitrary"))(c_all, dm_loc)


def sum_chunks(chunks):
    r = chunks.shape[1]

    def body(c_ref, o_ref):
        acc = c_ref[0]
        for i in range(1, NDEV):
            acc = acc + c_ref[i]
        o_ref[...] = acc

    return _call(body, name="sum_chunks", out_shape=jax.ShapeDtypeStruct((r, 128), F32))(chunks)


def _row_tile(r):
    if r <= 256:
        return r
    for t in range(256, 7, -8):
        if r % t == 0:
            return t
    return r


def adamw(w, m, v, g=None, chunks=None, name="adamw"):
    r, cdim = w.shape
    tr = _row_tile(r)
    bc1 = 1.0 - ADAM_B1 ** ADAM_STEP
    bc2 = 1.0 - ADAM_B2 ** ADAM_STEP

    def body(g_ref, w_ref, m_ref, v_ref, go_ref, d_ref, mo_ref, vo_ref):
        if chunks is None:
            grad = g_ref[...]
        else:
            grad = g_ref[0].astype(F32)
            for i in range(1, NDEV):
                grad = grad + g_ref[i].astype(F32)
        mn = ADAM_B1 * m_ref[...] + (1.0 - ADAM_B1) * grad
        vn = ADAM_B2 * v_ref[...] + (1.0 - ADAM_B2) * (grad * grad)
        m_hat = mn / bc1
        v_hat = vn / bc2
        go_ref[...] = grad
        d_ref[...] = -ADAM_LR * (m_hat / (jnp.sqrt(v_hat) + ADAM_EPS) + ADAM_WD * w_ref[...])
        mo_ref[...] = mn
        vo_ref[...] = vn

    tile = pl.BlockSpec((tr, cdim), lambda i: (i, 0))
    gspec = tile if chunks is None else pl.BlockSpec((NDEV, tr, cdim), lambda i: (0, i, 0))
    shp = jax.ShapeDtypeStruct((r, cdim), F32)
    return _call(body, name=name, grid=(r // tr,), in_specs=[gspec, tile, tile, tile],
                 out_specs=[tile] * 4, out_shape=[shp] * 4,
                 compiler_params=_params("arbitrary"))(g if chunks is None else chunks, w, m, v)


def adamw_layers(w, m, v, chunks, name):
    nl, r, cdim = w.shape
    tr = _row_tile(r)
    bc1 = 1.0 - ADAM_B1 ** ADAM_STEP
    bc2 = 1.0 - ADAM_B2 ** ADAM_STEP
    outs = [lax.empty(w.shape, F32) for _ in range(4)]
    for l in range(nl):
        def body(g_ref, w_ref, m_ref, v_ref, p0, p1, p2, p3, go_ref, d_ref, mo_ref, vo_ref):
            grad = g_ref[0].astype(F32)
            for i in range(1, NDEV):
                grad = grad + g_ref[i].astype(F32)
            mn = ADAM_B1 * m_ref[...] + (1.0 - ADAM_B1) * grad
            vn = ADAM_B2 * v_ref[...] + (1.0 - ADAM_B2) * (grad * grad)
            go_ref[...] = grad
            d_ref[...] = -ADAM_LR * ((mn / bc1) / (jnp.sqrt(vn / bc2) + ADAM_EPS) + ADAM_WD * w_ref[...])
            mo_ref[...] = mn
            vo_ref[...] = vn

        tile = pl.BlockSpec((None, tr, cdim), lambda i, l=l: (l, i, 0))
        whole = pl.BlockSpec(memory_space=pl.ANY)
        outs = _call(body, name=f"{name}_{l}", grid=(r // tr,),
                     in_specs=[pl.BlockSpec((NDEV, tr, cdim), lambda i: (0, i, 0)), tile, tile, tile] + [whole] * 4,
                     out_specs=[tile] * 4, out_shape=[jax.ShapeDtypeStruct(w.shape, F32)] * 4,
                     input_output_aliases={4: 0, 5: 1, 6: 2, 7: 3},
                     compiler_params=_params("arbitrary"))(chunks[l], w, m, v, *outs)
    return outs


WEIGHTS = ("w_ada", "b_ada", "w_in", "b_f", "gm_ln_g", "gm_ln_b", "gm_w_s", "gm_b_s", "s5_lam_re", "s5_lam_im",
           "s5_log_dt", "s5_b_re", "s5_b_im", "s5_c_re", "s5_c_im", "s5_d", "s5_w_glu", "s5_b_glu", "w_out", "ln1_g",
           "ln1_b", "w_up", "conv_w", "conv_b", "w_down", "ln2_g", "ln2_b")
SHARDED = ("w_in", "w_out", "w_up", "w_down", "conv_w")
LARGE = ("w_in", "w_out", "w_up", "w_down")
COL_SHARDED = ("w_in", "w_up", "conv_w")
PACKED = ("b_ada",) + SMALL
PACK_SEG = 8 * 128


def _gather_cols(g):
    nd, nl, r, c = g.shape
    return jnp.transpose(g, (1, 2, 0, 3)).reshape(nl, r, nd * c)


def _chunk_cols(g):
    nl, r, c8 = g.shape
    return jnp.transpose(g.reshape(nl, r, NDEV, c8 // NDEV), (2, 0, 1, 3))


def _join_cols(g):
    nd, r, c = g.shape
    return jnp.transpose(g, (1, 0, 2)).reshape(r, nd * c)


def _join_rows(g):
    nd, r, c = g.shape
    return g.reshape(nd * r, c)


def _split_cols(g):
    r, c8 = g.shape
    return jnp.transpose(g.reshape(r, NDEV, c8 // NDEV), (1, 0, 2))


def _split_rows(g):
    r8, c = g.shape
    return g.reshape(NDEV, r8 // NDEV, c)


def _pack(parts):
    segs = []
    for n in PACKED:
        flat = parts[n].reshape(-1)
        segs.append(jnp.pad(flat, (0, -flat.shape[0] % PACK_SEG)).reshape(-1, 128))
    rows = jnp.concatenate(segs, axis=0)
    return jnp.pad(rows, ((0, -rows.shape[0] % (NDEV * 8)), (0, 0)))


def _unpack(rows, shapes):
    out, off = {}, 0
    for n in PACKED:
        size = math.prod(shapes[n])
        nrows = -(-size // PACK_SEG) * 8
        out[n] = rows[off:off + nrows].reshape(-1)[:size].reshape(shapes[n])
        off += nrows
    return out


def kernel(x, c, w_ada, b_ada, w_in, b_f, gm_ln_g, gm_ln_b, gm_w_s, gm_b_s, s5_lam_re, s5_lam_im, s5_log_dt, s5_b_re, s5_b_im, s5_c_re, s5_c_im, s5_d, s5_w_glu, s5_b_glu, w_out, ln1_g, ln1_b, w_up, conv_w, conv_b, w_down, ln2_g, ln2_b, loss_target, m_w_ada, m_b_ada, m_w_in, m_b_f, m_gm_ln_g, m_gm_ln_b, m_gm_w_s, m_gm_b_s, m_s5_lam_re, m_s5_lam_im, m_s5_log_dt, m_s5_b_re, m_s5_b_im, m_s5_c_re, m_s5_c_im, m_s5_d, m_s5_w_glu, m_s5_b_glu, m_w_out, m_ln1_g, m_ln1_b, m_w_up, m_conv_w, m_conv_b, m_w_down, m_ln2_g, m_ln2_b, v_w_ada, v_b_ada, v_w_in, v_b_f, v_gm_ln_g, v_gm_ln_b, v_gm_w_s, v_gm_b_s, v_s5_lam_re, v_s5_lam_im, v_s5_log_dt, v_s5_b_re, v_s5_b_im, v_s5_c_re, v_s5_c_im, v_s5_d, v_s5_w_glu, v_s5_b_glu, v_w_out, v_ln1_g, v_ln1_b, v_w_up, v_conv_w, v_conv_b, v_w_down, v_ln2_g, v_ln2_b):
    given = dict(locals())
    wts = {n: given[n] for n in WEIGHTS}
    mom = {n: given["m_" + n] for n in WEIGHTS}
    var = {n: given["v_" + n] for n in WEIGHTS}
    nl = w_ada.shape[0]
    me = _my_index()
    ada_cols = w_ada.shape[2]

    (c_all,) = exchange([c], False, "gather_c")
    c_all = c_all.reshape(NDEV, D)
    b_loc = lax.dynamic_slice_in_dim(b_ada, me * ada_cols, ada_cols, axis=1)
    mod_part = mod_slices(c_all, w_ada, b_loc)

    mod_all, conv_all = exchange([mod_part, conv_w], False, "gather_mod")
    mod_mine = lax.dynamic_index_in_dim(mod_all, me, axis=2, keepdims=False)
    mods = jnp.transpose(mod_mine, (1, 0, 2)).reshape(nl, 6, D)
    mods = jnp.pad(mods, ((0, 0), (0, 2), (0, 0)))
    conv_full = _gather_cols(conv_all)
    sp = {n: wts[n] for n in SMALL}

    def joined(name, own, land):
        full = (_join_cols if name in COL_SHARDED else _join_rows)(land)
        at = (0, me * own.shape[1]) if name in COL_SHARDED else (me * own.shape[0], 0)
        return lax.dynamic_update_slice(full, own, at)

    def block(l, names):
        return [wts[n][l].astype(BF16) for n in names]

    def chunked(grads, names):
        return [(_split_cols if n in COL_SHARDED else _split_rows)(grads[n]) for n in names]

    head, tail = LARGE[:1], LARGE[1:]
    got_head = exchange_start(block(0, head), False, "gather_start_0_in")
    rest = block(0, tail)
    rest[0] = rest[0] + got_head[4][0, 0].astype(BF16)
    got_tail = exchange_start(rest, False, "gather_start_0_rest")
    xl, saved, weights = x[0], [], []
    for l in range(nl):
        if l == 0:
            own, land = exchange_wait(got_head, got_tail[4], False, "gather_wait_0_in")
            w = {n: joined(n, o, g) for n, o, g in zip(head, own, land)}
        else:
            own, land = exchange_wait(started, xl, False, f"gather_wait_{l}")
            w = {n: joined(n, o, g) for n, o, g in zip(LARGE, own, land)}
        mod_l = mods[l]
        if l + 1 < nl:
            nxt, w["w_in"] = lax.optimization_barrier((block(l + 1, LARGE), w["w_in"]))
            started = exchange_start(nxt, False, f"gather_start_{l + 1}")
            mod_l = mod_l + started[4][0, 0]
        w["w_in"] = jnp.pad(w["w_in"], ((0, 0), (0, NP - D_IN)))
        f = _layer_operands(sp, l)
        mixcat, sv = layer_fwd_mix(xl, mod_l, w["w_in"], f)
        if l == 0:
            own, land = exchange_wait(got_tail, mixcat, False, "gather_wait_0_rest")
            w.update({n: joined(n, o, g) for n, o, g in zip(tail, own, land)})
        w["conv_w"] = conv_full[l]
        xl, sv = layer_fwd_rest(xl, mixcat, mod_l, w, f, sv)
        weights.append(w)
        saved.append(sv)

    loss_tile, dx = loss_kernel(xl, loss_target[0])

    ffn_names, mix_names = ("w_up", "w_down"), ("w_in", "w_out")
    scattering, vals, dmods, gconv = [None] * nl, [None] * nl, [None] * nl, [None] * nl
    token = jnp.zeros((), F32)
    for l in reversed(range(nl)):
        mod_l = mods[l] + token
        dx1, g_ffn, part = layer_bwd_ffn(dx, saved[l], mod_l, weights[l])
        gconv[l] = g_ffn["conv_w"]
        if l == 0:
            sent_ffn = exchange_start(chunked(g_ffn, ffn_names), True, "scatter_start_0_ffn")
            mod_l = mod_l + sent_ffn[4][0, 0]
        dx, g_mix, vals[l], dmods[l] = layer_bwd_mix(dx1, saved[l], mod_l, weights[l], part)
        g_mix["w_in"] = g_mix["w_in"][:, :D_IN]
        if l == 0:
            scattering[l] = [(ffn_names, sent_ffn),
                             (mix_names, exchange_start(chunked(g_mix, mix_names), True, "scatter_start_0_mix"))]
        else:
            sent = exchange_start(chunked(dict(g_ffn, **g_mix), LARGE), True, f"scatter_start_{l}")
            scattering[l] = [(LARGE, sent)]
            token = sent[4][0, 0]
    gx = dx
    dmods = jnp.stack(dmods)
    gsm = {n: jnp.stack([v[n] for v in vals]) for n in SMALL}

    gsm["b_ada"] = dmods.reshape(nl, 6 * D)
    packed = _pack(gsm).reshape(NDEV, -1, 128)
    conv_recv, small_recv = exchange([_chunk_cols(jnp.stack(gconv)), packed], True, "scatter_small")
    small_sum = sum_chunks(small_recv)
    small_all, dmod_all = exchange([small_sum, dmods.reshape(nl, 6 * D)], False, "gather_small")

    received = [dict() for _ in range(nl)]

    def arrive(l, k, after):
        names, sent = scattering[l][k]
        own, land = exchange_wait(sent, after, True, f"scatter_wait_{l}_{k}")
        for n, o, g in zip(names, own, land):
            mine = lax.dynamic_index_in_dim(o, me, 0, keepdims=False)
            received[l][n] = lax.dynamic_update_index_in_dim(g, mine, me, 0)
        return land[0]

    after = small_all
    for l in reversed(range(nl)):
        after = arrive(l, 0, after)
    out = {}
    for n in ffn_names:
        out[n] = adamw_layers(wts[n], mom[n], var[n], [received[l][n] for l in range(nl)], "adamw_" + n)
    arrive(0, 1, out[ffn_names[-1]][0])
    for n in mix_names:
        out[n] = adamw_layers(wts[n], mom[n], var[n], [received[l][n] for l in range(nl)], "adamw_" + n)
    shp = conv_w.shape
    two_d = lambda a: a.reshape(shp[0] * shp[1], shp[2])
    res = adamw(two_d(conv_w), two_d(m_conv_w), two_d(v_conv_w),
                chunks=conv_recv.reshape(NDEV, shp[0] * shp[1], shp[2]), name="adamw_conv_w")
    out["conv_w"] = [r.reshape(shp) for r in res]

    dm_loc = lax.dynamic_slice_in_dim(dmod_all, me * ada_cols, ada_cols, axis=2)
    g_ada = ada_grad(c_all, jnp.transpose(dm_loc, (1, 0, 2)))
    two_d = lambda a: a.reshape(nl * D, ada_cols)
    res = adamw(two_d(w_ada), two_d(m_w_ada), two_d(v_w_ada), g=two_d(g_ada), name="adamw_w_ada")
    out["w_ada"] = [r.reshape(w_ada.shape) for r in res]

    shapes = {n: wts[n].shape for n in PACKED}
    res = adamw(_pack(wts), _pack(mom), _pack(var), g=small_all.reshape(-1, 128), name="adamw_small")
    unpacked = [_unpack(r, shapes) for r in res]
    for n in PACKED:
        out[n] = [u[n] for u in unpacked]

    loss = lax.psum(loss_tile[0, 0], ("x", "y", "c"))
    return (loss, gx[None], *[out[n][0] for n in WEIGHTS], *[out[n][1] for n in WEIGHTS],
            *[out[n][2] for n in WEIGHTS], *[out[n][3] for n in WEIGHTS])
```

```python
import functools
import math

import jax
import jax.numpy as jnp
from jax import lax
from jax.experimental import pallas as pl
from jax.experimental.pallas import tpu as pltpu

F32 = jnp.float32
BF16 = jnp.bfloat16
MESH_IDS = pl.DeviceIdType.MESH

D = 1024
SEQ = 4096
DEPTH = 4
NDEV = 8
HD = 64
GM_W = 256
GM_H = 4
GM_C = 128
S5_W = 256
S5_G = 16
S5_H = 16
S5_P = 64
S5_N = S5_G * S5_P
FX_W = 512
FX_H = 8
D_IN = 2 * GM_W + S5_W + 3 * FX_W + FX_H
NP = 2432
FF_COL = 2304
DFF = 2816
LN_EPS = 1e-5
DN_ALPHA = (2.0 * DEPTH) ** 0.25
NEG_INF = -1e30
ADAM_LR = 0.001
ADAM_B1 = 0.9
ADAM_B2 = 0.999
ADAM_EPS = 1e-08
ADAM_WD = 0.01
ADAM_STEP = 10

V7X_VMEM_LIMIT = 56 * 1024 * 1024
TS = 512
TS_C = 256
T_S5 = 256
TQ = 512
TQ_FWD = 1024
ATTN_HEADS = 2


def _call(body, **kw):
    return pl.pallas_call(body, **kw)


def _params(*sem):
    return pltpu.CompilerParams(dimension_semantics=sem if sem else None,
                                vmem_limit_bytes=V7X_VMEM_LIMIT)


def _nn(a, b):
    return jnp.dot(a.astype(BF16), b.astype(BF16), preferred_element_type=F32)


def _nt(a, b):
    return lax.dot_general(a.astype(BF16), b.astype(BF16), (((1,), (1,)), ((), ())),
                           preferred_element_type=F32)


def _tn(a, b):
    return lax.dot_general(a.astype(BF16), b.astype(BF16), (((0,), (0,)), ((), ())),
                           preferred_element_type=F32)


@jax.custom_vjp
def _bdot(a, b):
    return _nn(a, b)


def _bdot_fwd(a, b):
    return _nn(a, b), (a, b)


def _bdot_bwd(res, g):
    a, b = res
    return _nt(g, b), _tn(a, g)


_bdot.defvjp(_bdot_fwd, _bdot_bwd)


@jax.custom_vjp
def _bdot_nt(a, b):
    return _nt(a, b)


def _bdot_nt_fwd(a, b):
    return _nt(a, b), (a, b)


def _bdot_nt_bwd(res, g):
    a, b = res
    return _nn(g, b), _tn(g, a)


_bdot_nt.defvjp(_bdot_nt_fwd, _bdot_nt_bwd)


def _ln(r, g, b):
    mu = jnp.mean(r, axis=-1, keepdims=True)
    xc = r - mu
    var = jnp.mean(xc * xc, axis=-1, keepdims=True)
    return xc * lax.rsqrt(var + LN_EPS) * g + b


def _rows(shape):
    return lax.broadcasted_iota(jnp.int32, shape, 0)


def _lanes(shape):
    return lax.broadcasted_iota(jnp.int32, shape, 1)


def mm_nn(a, w, tn, name, mod=None, rows=None, out_dtype=F32):
    s, k = a.shape
    n = w.shape[1]
    ts = min(TS, s)

    def body(*refs):
        if mod is None:
            a_ref, w_ref, o_ref = refs
            h = a_ref[...]
        else:
            a_ref, m_ref, w_ref, o_ref = refs
            h = a_ref[...] * (1.0 + m_ref[rows[1]:rows[1] + 1, :]) + m_ref[rows[0]:rows[0] + 1, :]
        o_ref[...] = jnp.dot(h.astype(BF16), w_ref[...], preferred_element_type=F32).astype(out_dtype)

    in_specs = [pl.BlockSpec((ts, k), lambda j, i: (i, 0))]
    args = [a]
    if mod is not None:
        in_specs.append(pl.BlockSpec((8, k), lambda j, i: (0, 0)))
        args.append(mod)
    in_specs.append(pl.BlockSpec((k, tn), lambda j, i: (0, j)))
    args.append(w)
    return _call(body, name=name, grid=(n // tn, s // ts), in_specs=in_specs,
                 out_specs=pl.BlockSpec((ts, tn), lambda j, i: (i, j)),
                 out_shape=jax.ShapeDtypeStruct((s, n), out_dtype),
                 compiler_params=_params("arbitrary", "arbitrary"))(*args)


def mm_nt(dy, w, name):
    s, n = dy.shape
    k = w.shape[0]
    ts = min(TS, s)

    def body(dy_ref, w_ref, o_ref):
        o_ref[...] = _nt(dy_ref[...], w_ref[...])

    return _call(body, name=name, grid=(s // ts,),
                 in_specs=[pl.BlockSpec((ts, n), lambda i: (i, 0)),
                           pl.BlockSpec((k, n), lambda i: (0, 0))],
                 out_specs=pl.BlockSpec((ts, k), lambda i: (i, 0)),
                 out_shape=jax.ShapeDtypeStruct((s, k), F32),
                 compiler_params=_params("arbitrary"))(dy, w)


def mm_nt_mod(dy, w, x, dres, mod, rows, name):
    s, n = dy.shape
    k = w.shape[0]
    ts = min(TS, s)

    def body(dy_ref, w_ref, x_ref, r_ref, m_ref, dx_ref, dsh_ref, dsc_ref):
        @pl.when(pl.program_id(0) == 0)
        def _():
            dsh_ref[...] = jnp.zeros_like(dsh_ref)
            dsc_ref[...] = jnp.zeros_like(dsc_ref)

        dh = _nt(dy_ref[...], w_ref[...])
        dx_ref[...] = r_ref[...] + dh * (1.0 + m_ref[rows[1]:rows[1] + 1, :])
        dsh_ref[...] += jnp.sum(dh, axis=0, keepdims=True)
        dsc_ref[...] += jnp.sum(dh * x_ref[...], axis=0, keepdims=True)

    row = pl.BlockSpec((1, k), lambda i: (0, 0))
    tile = pl.BlockSpec((ts, k), lambda i: (i, 0))
    return _call(body, name=name, grid=(s // ts,),
                 in_specs=[pl.BlockSpec((ts, n), lambda i: (i, 0)),
                           pl.BlockSpec((k, n), lambda i: (0, 0)), tile, tile,
                           pl.BlockSpec((8, k), lambda i: (0, 0))],
                 out_specs=[tile, row, row],
                 out_shape=[jax.ShapeDtypeStruct((s, k), F32),
                            jax.ShapeDtypeStruct((1, k), F32),
                            jax.ShapeDtypeStruct((1, k), F32)],
                 compiler_params=_params("arbitrary"))(dy, w, x, dres, mod)


def mm_tn(a, dy, tn, name, mod=None, rows=None):
    s, k = a.shape
    n = dy.shape[1]
    ts = min(TS, s)
    ns = s // ts

    def body(*refs):
        if mod is None:
            a_ref, dy_ref, o_ref, acc = refs
            h = a_ref[...]
        else:
            a_ref, m_ref, dy_ref, o_ref, acc = refs
            h = a_ref[...] * (1.0 + m_ref[rows[1]:rows[1] + 1, :]) + m_ref[rows[0]:rows[0] + 1, :]
        i = pl.program_id(1)

        @pl.when(i == 0)
        def _():
            acc[...] = jnp.zeros_like(acc)

        acc[...] += _tn(h, dy_ref[...])

        @pl.when(i == ns - 1)
        def _():
            o_ref[...] = acc[...].astype(BF16)

    in_specs = [pl.BlockSpec((ts, k), lambda j, i: (i, 0))]
    args = [a]
    if mod is not None:
        in_specs.append(pl.BlockSpec((8, k), lambda j, i: (0, 0)))
        args.append(mod)
    in_specs.append(pl.BlockSpec((ts, tn), lambda j, i: (i, j)))
    args.append(dy)
    return _call(body, name=name, grid=(n // tn, s // ts), in_specs=in_specs,
                 out_specs=pl.BlockSpec((k, tn), lambda j, i: (0, j)),
                 out_shape=jax.ShapeDtypeStruct((k, n), BF16),
                 scratch_shapes=[pltpu.VMEM((k, tn), F32)],
                 compiler_params=_params("arbitrary", "arbitrary"))(*args)


def _post_fn(x, br, gate, lg, lb):
    return _ln(DN_ALPHA * x + (1.0 + gate) * br, lg, lb)


def post_fwd(x, br, mod, grow, lg, lb, name):
    s = x.shape[0]
    ts = min(TS, s)

    def body(x_ref, b_ref, m_ref, lg_ref, lb_ref, o_ref):
        o_ref[...] = _post_fn(x_ref[...], b_ref[...], m_ref[grow:grow + 1, :], lg_ref[...], lb_ref[...])

    tile = pl.BlockSpec((ts, D), lambda i: (i, 0))
    row = pl.BlockSpec((1, D), lambda i: (0, 0))
    return _call(body, name=name, grid=(s // ts,),
                 in_specs=[tile, tile, pl.BlockSpec((8, D), lambda i: (0, 0)), row, row],
                 out_specs=tile, out_shape=jax.ShapeDtypeStruct((s, D), F32),
                 compiler_params=_params("arbitrary"))(x, br, mod, lg, lb)


def post_bwd(x, br, mod, grow, lg, lb, dy, name):
    s = x.shape[0]
    ts = min(TS, s)

    def body(x_ref, b_ref, m_ref, lg_ref, lb_ref, dy_ref, dx_ref, db_ref, dg_ref, dlg_ref, dlb_ref):
        @pl.when(pl.program_id(0) == 0)
        def _():
            dg_ref[...] = jnp.zeros_like(dg_ref)
            dlg_ref[...] = jnp.zeros_like(dlg_ref)
            dlb_ref[...] = jnp.zeros_like(dlb_ref)

        _, vjp = jax.vjp(_post_fn, x_ref[...], b_ref[...], m_ref[grow:grow + 1, :], lg_ref[...], lb_ref[...])
        dx, db, dg, dlg, dlb = vjp(dy_ref[...])
        dx_ref[...] = dx
        db_ref[...] = db.astype(BF16)
        dg_ref[...] += dg
        dlg_ref[...] += dlg
        dlb_ref[...] += dlb

    tile = pl.BlockSpec((ts, D), lambda i: (i, 0))
    row = pl.BlockSpec((1, D), lambda i: (0, 0))
    rs = jax.ShapeDtypeStruct((1, D), F32)
    return _call(body, name=name, grid=(s // ts,),
                 in_specs=[tile, tile, pl.BlockSpec((8, D), lambda i: (0, 0)), row, row, tile],
                 out_specs=[tile, tile, row, row, row],
                 out_shape=[jax.ShapeDtypeStruct((s, D), F32), jax.ShapeDtypeStruct((s, D), BF16), rs, rs, rs],
                 compiler_params=_params("arbitrary"))(x, br, mod, lg, lb, dy)


def loss_kernel(y, target):
    s = y.shape[0]
    ts = min(TS, s)

    def body(y_ref, t_ref, l_ref, dy_ref):
        @pl.when(pl.program_id(0) == 0)
        def _():
            l_ref[...] = jnp.zeros_like(l_ref)

        err = y_ref[...] - t_ref[...]
        dy_ref[...] = err * (1.0 / D)
        per_tok = jnp.mean(err * err, axis=-1, keepdims=True)
        l_ref[...] += 0.5 * jnp.sum(per_tok)

    tile = pl.BlockSpec((ts, D), lambda i: (i, 0))
    return _call(body, name="loss", grid=(s // ts,), in_specs=[tile, tile],
                 out_specs=[pl.BlockSpec((8, 128), lambda i: (0, 0)), tile],
                 out_shape=[jax.ShapeDtypeStruct((8, 128), F32), jax.ShapeDtypeStruct((s, D), F32)],
                 compiler_params=_params("arbitrary"))(y, target)


def _gm_pair(u, v, lg, lb, w0, w1, bs0, bs1):
    t = u.shape[0]
    low = _lanes((t, 2 * HD)) < HD

    def head_mean(x):
        lo = jnp.sum(jnp.where(low, x, 0.0), axis=-1, keepdims=True)
        hi = jnp.sum(jnp.where(low, 0.0, x), axis=-1, keepdims=True)
        return jnp.where(low, lo, hi) * (1.0 / HD)

    xc = v - head_mean(v)
    vn = xc * lax.rsqrt(head_mean(xc * xc) + LN_EPS) * lg + lb
    v0 = jnp.where(low, vn, 0.0)
    v1 = jnp.where(low, 0.0, vn)
    causal = _rows((GM_C, GM_C)) >= _lanes((GM_C, GM_C))
    wm0 = jnp.where(causal, w0, 0.0)
    wm1 = jnp.where(causal, w1, 0.0)
    bias = jnp.where(_lanes((GM_C, 2 * HD)) < HD, bs0, bs1)
    chunks = []
    for n in range(t // GM_C):
        rs = slice(n * GM_C, (n + 1) * GM_C)
        chunks.append(u[rs] * (_bdot(wm0, v0[rs]) + _bdot(wm1, v1[rs]) + bias))
    return jnp.concatenate(chunks, axis=0)


def gm_fwd(p, lg, lb, ws, bst):
    s = p.shape[0]
    ts = min(TS_C, s)

    def body(u_ref, v_ref, lg_ref, lb_ref, ws_ref, bs_ref, o_ref):
        for j in range(GM_H // 2):
            sl = slice(j * 2 * HD, (j + 1) * 2 * HD)
            o_ref[:, sl] = _gm_pair(u_ref[:, sl], v_ref[:, sl], lg_ref[:, sl], lb_ref[:, sl], ws_ref[2 * j],
                                    ws_ref[2 * j + 1], bs_ref[:, 2 * j:2 * j + 1],
                                    bs_ref[:, 2 * j + 1:2 * j + 2]).astype(BF16)

    full = lambda shape: pl.BlockSpec(shape, lambda i: (0,) * len(shape))
    return _call(body, name="gm_fwd", grid=(s // ts,),
                 in_specs=[pl.BlockSpec((ts, GM_W), lambda i: (i, 0)), pl.BlockSpec((ts, GM_W), lambda i: (i, 1)),
                           full((1, GM_W)), full((1, GM_W)), full((GM_H, GM_C, GM_C)), full((GM_C, GM_H))],
                 out_specs=pl.BlockSpec((ts, GM_W), lambda i: (i, 0)),
                 out_shape=jax.ShapeDtypeStruct((s, GM_W), BF16),
                 compiler_params=_params("arbitrary"))(p, p, lg, lb, ws, bst)


def gm_bwd(p, lg, lb, ws, bst, dmix):
    s = p.shape[0]
    ts = min(TS_C, s)

    def body(u_ref, v_ref, lg_ref, lb_ref, ws_ref, bs_ref, dy_ref, duv_ref, dlg_ref, dlb_ref, dws_ref, dbs_ref):
        @pl.when(pl.program_id(0) == 0)
        def _():
            dlg_ref[...] = jnp.zeros_like(dlg_ref)
            dlb_ref[...] = jnp.zeros_like(dlb_ref)
            dws_ref[...] = jnp.zeros_like(dws_ref)
            dbs_ref[...] = jnp.zeros_like(dbs_ref)

        for j in range(GM_H // 2):
            sl = slice(j * 2 * HD, (j + 1) * 2 * HD)
            _, vjp = jax.vjp(_gm_pair, u_ref[:, sl], v_ref[:, sl], lg_ref[:, sl], lb_ref[:, sl], ws_ref[2 * j],
                             ws_ref[2 * j + 1], bs_ref[:, 2 * j:2 * j + 1], bs_ref[:, 2 * j + 1:2 * j + 2])
            du, dv, dlg, dlb, dw0, dw1, dbs0, dbs1 = vjp(dy_ref[:, sl])
            duv_ref[:, sl] = du.astype(BF16)
            duv_ref[:, GM_W + j * 2 * HD:GM_W + (j + 1) * 2 * HD] = dv.astype(BF16)
            dlg_ref[:, sl] += dlg
            dlb_ref[:, sl] += dlb
            dws_ref[2 * j] += dw0
            dws_ref[2 * j + 1] += dw1
            dbs_ref[:, 2 * j:2 * j + 1] += dbs0
            dbs_ref[:, 2 * j + 1:2 * j + 2] += dbs1

    full = lambda shape: pl.BlockSpec(shape, lambda i: (0,) * len(shape))
    return _call(body, name="gm_bwd", grid=(s // ts,),
                 in_specs=[pl.BlockSpec((ts, GM_W), lambda i: (i, 0)), pl.BlockSpec((ts, GM_W), lambda i: (i, 1)),
                           full((1, GM_W)), full((1, GM_W)), full((GM_H, GM_C, GM_C)), full((GM_C, GM_H)),
                           pl.BlockSpec((ts, GM_W), lambda i: (i, 0))],
                 out_specs=[pl.BlockSpec((ts, 2 * GM_W), lambda i: (i, 0)), full((1, GM_W)), full((1, GM_W)),
                            full((GM_H, GM_C, GM_C)), full((GM_C, GM_H))],
                 out_shape=[jax.ShapeDtypeStruct((s, 2 * GM_W), BF16), jax.ShapeDtypeStruct((1, GM_W), F32),
                            jax.ShapeDtypeStruct((1, GM_W), F32), jax.ShapeDtypeStruct((GM_H, GM_C, GM_C), F32),
                            jax.ShapeDtypeStruct((GM_C, GM_H), F32)],
                 compiler_params=_params("arbitrary"))(p, p, lg, lb, ws, bst, dmix)


def _s5_prep_fn(lr, li, ldt, bt):
    dt = jnp.exp(ldt)
    er = jnp.exp(lr * dt)
    ar = er * jnp.cos(li * dt)
    ai = er * jnp.sin(li * dt)
    den = lr * lr + li * li
    nr = ar - 1.0
    cr = (nr * lr + ai * li) / den
    ci = (ai * lr - nr * li) / den
    br, bi = bt[:, :S5_N], bt[:, S5_N:]
    return ar, ai, jnp.concatenate([cr * br - ci * bi, cr * bi + ci * br], axis=1)


def s5_prep_fwd(lr, li, ldt, bt):
    def body(lr_ref, li_ref, ldt_ref, bt_ref, a_ref, bb_ref):
        ar, ai, bb = _s5_prep_fn(lr_ref[...], li_ref[...], ldt_ref[...], bt_ref[...])
        a_ref[...] = jnp.concatenate([ar, ai, jnp.zeros((6, S5_N), F32)], axis=0)
        bb_ref[...] = bb

    return _call(body, name="s5_prep_fwd",
                 out_shape=[jax.ShapeDtypeStruct((8, S5_N), F32), jax.ShapeDtypeStruct((S5_H, 2 * S5_N), F32)])(lr, li, ldt, bt)


def s5_prep_bwd(lr, li, ldt, bt, da, dbb):
    def body(lr_ref, li_ref, ldt_ref, bt_ref, da_ref, dbb_ref, dlr_ref, dli_ref, dldt_ref, dbt_ref):
        _, vjp = jax.vjp(_s5_prep_fn, lr_ref[...], li_ref[...], ldt_ref[...], bt_ref[...])
        dlr, dli, dldt, dbt = vjp((da_ref[0:1, :], da_ref[1:2, :], dbb_ref[...]))
        dlr_ref[...] = dlr
        dli_ref[...] = dli
        dbt_ref[...] = dbt
        group = (_rows((S5_N, 128)) // S5_P == _lanes((S5_N, 128))).astype(F32)
        dldt_ref[...] = jnp.dot(jnp.broadcast_to(dldt, (8, S5_N)), group, precision=lax.Precision.HIGHEST,
                                preferred_element_type=F32)[0:1, :]

    r = jax.ShapeDtypeStruct((1, S5_N), F32)
    return _call(body, name="s5_prep_bwd",
                 out_shape=[r, r, jax.ShapeDtypeStruct((1, 128), F32),
                            jax.ShapeDtypeStruct((S5_H, 2 * S5_N), F32)])(lr, li, ldt, bt, da, dbb)


def _s5_out_fn(x, u, cbd, drow, wg, bg):
    y = _bdot_nt(x[:, :S5_N], cbd[:, :S5_N]) - _bdot_nt(x[:, S5_N:], cbd[:, S5_N:]) + drow * u
    y = jax.nn.gelu(y)
    gate = _bdot_nt(y, wg) + bg
    return y * jax.nn.sigmoid(gate)


def _scan_chunk(buf, ar, ai, cr, ci, reverse):
    t = buf.shape[0]

    def local(xr, xi, rows):
        within = _rows(xr.shape) % 8
        pr, pi = ar, ai
        for d in (1, 2, 4):
            keep = within < 8 - d if reverse else within >= d
            shift = rows - d if reverse else d
            sr = jnp.where(keep, pltpu.roll(xr, shift, 0), 0.0)
            si = jnp.where(keep, pltpu.roll(xi, shift, 0), 0.0)
            xr, xi = xr + pr * sr - pi * si, xi + pr * si + pi * sr
            pr, pi = pr * pr - pi * pi, 2.0 * pr * pi
        return xr, xi

    xr, xi = local(buf[:, :S5_N], buf[:, S5_N:], t)
    buf[:, :S5_N] = xr
    buf[:, S5_N:] = xi
    edge = _rows((8, S5_N)) == (7 if reverse else 0)
    pr8, pi8 = local(jnp.where(edge, ar, 0.0), jnp.where(edge, ai, 0.0), 8)

    def group(j, c):
        g = t // 8 - 1 - j if reverse else j
        rows = pl.ds(pl.multiple_of(g * 8, 8), 8)
        gr = buf[rows, :S5_N] + pr8 * c[0] - pi8 * c[1]
        gi = buf[rows, S5_N:] + pr8 * c[1] + pi8 * c[0]
        buf[rows, :S5_N] = gr
        buf[rows, S5_N:] = gi
        return (gr[0:1, :], gi[0:1, :]) if reverse else (gr[7:8, :], gi[7:8, :])

    return lax.fori_loop(0, t // 8, group, (cr, ci), unroll=4)


def s5_fwd(p, arow, bbd, cbd, drow, wg, bg):
    s = p.shape[0]
    t = min(T_S5, s)

    def body(u_ref, a_ref, bbd_ref, cbd_ref, d_ref, wg_ref, bg_ref, y_ref, st_ref, carry):
        @pl.when(pl.program_id(0) == 0)
        def _():
            carry[...] = jnp.zeros_like(carry)

        u = u_ref[...]
        st_ref[...] = _nn(u, bbd_ref[...])
        cr, ci = _scan_chunk(st_ref, a_ref[0:1, :], a_ref[1:2, :], carry[0:1, :S5_N], carry[0:1, S5_N:], False)
        carry[0:1, :S5_N] = cr
        carry[0:1, S5_N:] = ci
        y_ref[...] = _s5_out_fn(st_ref[...], u, cbd_ref[...], d_ref[...], wg_ref[...], bg_ref[...]).astype(BF16)

    full = lambda shape: pl.BlockSpec(shape, lambda i: (0,) * len(shape))
    return _call(body, name="s5_fwd", grid=(s // t,),
                 in_specs=[pl.BlockSpec((t, S5_W), lambda i: (i, 2)), full((8, S5_N)), full((S5_W, 2 * S5_N)),
                           full((S5_W, 2 * S5_N)), full((1, S5_W)), full((S5_W, S5_W)), full((1, S5_W))],
                 out_specs=[pl.BlockSpec((t, S5_W), lambda i: (i, 0)), pl.BlockSpec((t, 2 * S5_N), lambda i: (i, 0))],
                 out_shape=[jax.ShapeDtypeStruct((s, S5_W), BF16), jax.ShapeDtypeStruct((s, 2 * S5_N), F32)],
                 scratch_shapes=[pltpu.VMEM((8, 2 * S5_N), F32)],
                 compiler_params=_params("arbitrary"))(p, arow, bbd, cbd, drow, wg, bg)


def s5_bwd(p, st, arow, bbd, cbd, drow, wg, bg, dmix):
    s = p.shape[0]
    t = min(T_S5, s)
    nc = s // t

    def body(u_ref, st_ref, prev_ref, a_ref, bbd_ref, cbd_ref, d_ref, wg_ref, bg_ref, dy_ref,
             du_ref, da_ref, dbbd_ref, dcbd_ref, dd_ref, dwg_ref, dbg_ref, carry, gbuf):
        i = pl.program_id(0)

        @pl.when(i == 0)
        def _():
            carry[...] = jnp.zeros_like(carry)
            for r in (da_ref, dbbd_ref, dcbd_ref, dd_ref, dwg_ref, dbg_ref):
                r[...] = jnp.zeros_like(r)

        u = u_ref[...]
        x = st_ref[...]
        _, vjp = jax.vjp(_s5_out_fn, x, u, cbd_ref[...], d_ref[...], wg_ref[...], bg_ref[...])
        dx, du1, dcbd, dd, dwg, dbg = vjp(dy_ref[...])
        gbuf[...] = dx
        cr, ci = _scan_chunk(gbuf, a_ref[0:1, :], -a_ref[1:2, :], carry[0:1, :S5_N], carry[0:1, S5_N:], True)
        carry[0:1, :S5_N] = cr
        carry[0:1, S5_N:] = ci
        gr, gi = gbuf[:, :S5_N], gbuf[:, S5_N:]
        rid = _rows((t, S5_N))
        has_prev = (i < nc - 1).astype(F32)
        top_r = prev_ref[7:8, :S5_N] * has_prev
        top_i = prev_ref[7:8, S5_N:] * has_prev
        xpr = jnp.where(rid == 0, top_r, pltpu.roll(x[:, :S5_N], 1, 0))
        xpi = jnp.where(rid == 0, top_i, pltpu.roll(x[:, S5_N:], 1, 0))
        da_ref[0:1, :] += jnp.sum(xpr * gr + xpi * gi, axis=0, keepdims=True)
        da_ref[1:2, :] += jnp.sum(xpr * gi - xpi * gr, axis=0, keepdims=True)
        g = jnp.concatenate([gr, gi], axis=1)
        dbbd_ref[...] += _tn(u, g)
        du_ref[...] = (_nt(g, bbd_ref[...]) + du1).astype(BF16)
        dcbd_ref[...] += dcbd
        dd_ref[...] += dd
        dwg_ref[...] += dwg
        dbg_ref[...] += dbg

    full = lambda shape: pl.BlockSpec(shape, lambda i: (0,) * len(shape))
    rev = lambda col: (lambda i: (nc - 1 - i, col))
    prev_map = lambda i: (jnp.maximum((nc - 1 - i) * (t // 8) - 1, 0), 0)
    return _call(body, name="s5_bwd", grid=(nc,),
                 in_specs=[pl.BlockSpec((t, S5_W), rev(2)), pl.BlockSpec((t, 2 * S5_N), rev(0)),
                           pl.BlockSpec((8, 2 * S5_N), prev_map), full((8, S5_N)), full((S5_W, 2 * S5_N)),
                           full((S5_W, 2 * S5_N)), full((1, S5_W)), full((S5_W, S5_W)), full((1, S5_W)),
                           pl.BlockSpec((t, S5_W), rev(1))],
                 out_specs=[pl.BlockSpec((t, S5_W), rev(0)), full((8, S5_N)), full((S5_W, 2 * S5_N)),
                            full((S5_W, 2 * S5_N)), full((1, S5_W)), full((S5_W, S5_W)), full((1, S5_W))],
                 out_shape=[jax.ShapeDtypeStruct((s, S5_W), BF16), jax.ShapeDtypeStruct((8, S5_N), F32),
                            jax.ShapeDtypeStruct((S5_W, 2 * S5_N), F32), jax.ShapeDtypeStruct((S5_W, 2 * S5_N), F32),
                            jax.ShapeDtypeStruct((1, S5_W), F32), jax.ShapeDtypeStruct((S5_W, S5_W), F32),
                            jax.ShapeDtypeStruct((1, S5_W), F32)],
                 scratch_shapes=[pltpu.VMEM((8, 2 * S5_N), F32), pltpu.VMEM((t, 2 * S5_N), F32)],
                 compiler_params=_params("arbitrary"))(p, st, st, arow, bbd, cbd, drow, wg, bg, dmix)


def _cum_steps(s):
    return int(math.ceil(math.log2(s)))


V_BLK = (2 * GM_W + S5_W + 2 * FX_W) // 128


AUG = 2 * HD
BIAS_COL = HD
FQ_COL = HD + 3
PAIR_W = 256


def _split3(f):
    hi = f.astype(BF16).astype(F32)
    r = f - hi
    mid = r.astype(BF16).astype(F32)
    lo = (r - mid).astype(BF16).astype(F32)
    return hi, mid, lo


def fox_prep(p, bf):
    s = p.shape[0]
    ts = min(TS, s)
    scale = HD ** -0.5

    def body(q0_ref, q1_ref, k0_ref, k1_ref, v0_ref, v1_ref, f_ref, bf_ref,
             qa_ref, ka_ref, qat_ref, kat_ref, vt_ref, carry):
        @pl.when(pl.program_id(0) == 0)
        def _():
            carry[...] = jnp.zeros_like(carry)

        lane = _lanes((ts, 128))
        lf = jax.nn.log_sigmoid(f_ref[...] + bf_ref[...])
        acc = jnp.where(lane < FX_H, lf, 0.0)
        rid = _rows((ts, 128))
        for k in range(_cum_steps(ts)):
            d = 1 << k
            acc = acc + jnp.where(rid >= d, pltpu.roll(acc, d, 0), 0.0)
        acc = acc + carry[0:1, :]
        carry[0:1, :] = acc[ts - 1:ts, :]

        low = lane < HD
        for h in range(FX_H):
            blk, pos = divmod(h, 4)
            pair = slice((pos // 2) * 128, (pos // 2) * 128 + 128)
            hi, mid, lo = _split3(acc[:, h:h + 1])
            one = jnp.ones((ts, 1), F32)

            def augment(ref, cols):
                x = ref[:, pair]
                if pos % 2:
                    x = pltpu.roll(x, HD, 1)
                out = jnp.where(low, x, 0.0)
                for j, cval in enumerate(cols):
                    out = jnp.where(lane == HD + j, cval, out)
                return out

            qa = augment((q0_ref, q1_ref)[blk], (one, one, one, hi, mid, lo))
            qa = jnp.where(low, qa * scale, qa)
            ka = augment((k0_ref, k1_ref)[blk], (-hi, -mid, -lo, one, one, one))
            cs = slice(h * AUG, (h + 1) * AUG)
            qa_ref[:, cs] = qa.astype(BF16)
            ka_ref[:, cs] = ka.astype(BF16)
            qat_ref[cs, :] = jnp.transpose(qa).astype(BF16)
            kat_ref[cs, :] = jnp.transpose(ka).astype(BF16)
        for j in range(FX_H // 2):
            vref = (v0_ref, v1_ref)[j // 2]
            vt_ref[j * 128:(j + 1) * 128, :] = jnp.transpose(vref[:, (j % 2) * 128:(j % 2) * 128 + 128]).astype(BF16)

    q_blk = (2 * GM_W + S5_W) // PAIR_W
    col = lambda b: pl.BlockSpec((ts, PAIR_W), lambda i: (i, b))
    wide = FX_H * AUG
    return _call(body, name="fox_prep", grid=(s // ts,),
                 in_specs=[col(q_blk), col(q_blk + 1), col(q_blk + 2), col(q_blk + 3), col(q_blk + 4), col(q_blk + 5),
                           pl.BlockSpec((ts, 128), lambda i: (i, FF_COL // 128)), pl.BlockSpec((1, 128), lambda i: (0, 0))],
                 out_specs=[pl.BlockSpec((ts, wide), lambda i: (i, 0)), pl.BlockSpec((ts, wide), lambda i: (i, 0)),
                            pl.BlockSpec((wide, ts), lambda i: (0, i)), pl.BlockSpec((wide, ts), lambda i: (0, i)),
                            pl.BlockSpec((FX_W, ts), lambda i: (0, i))],
                 out_shape=[jax.ShapeDtypeStruct((s, wide), BF16), jax.ShapeDtypeStruct((s, wide), BF16),
                            jax.ShapeDtypeStruct((wide, s), BF16), jax.ShapeDtypeStruct((wide, s), BF16),
                            jax.ShapeDtypeStruct((FX_W, s), BF16)],
                 scratch_shapes=[pltpu.VMEM((8, 128), F32)],
                 compiler_params=_params("arbitrary"))(p, p, p, p, p, p, p, bf)


def fox_prep_grad(p, bf, dfq, dfk):
    s = p.shape[0]
    ts = min(TS, s)
    ns = s // ts

    def body(f_ref, bf_ref, dfq_ref, dfk_ref, df_ref, dbf_ref, carry):
        @pl.when(pl.program_id(0) == 0)
        def _():
            carry[...] = jnp.zeros_like(carry)
            dbf_ref[...] = jnp.zeros_like(dbf_ref)

        lane = _lanes((ts, 128))
        acc = jnp.zeros((ts, 128), F32)
        for h in range(FX_H):
            c = (h // 2) * 128 + h % 2
            acc = jnp.where(lane == h, dfq_ref[:, c:c + 1] + dfk_ref[:, c:c + 1], acc)
        rid = _rows((ts, 128))
        for k in range(_cum_steps(ts)):
            d = 1 << k
            acc = acc + jnp.where(rid < ts - d, pltpu.roll(acc, ts - d, 0), 0.0)
        acc = acc + carry[0:1, :]
        carry[0:1, :] = acc[0:1, :]
        z = f_ref[...] + bf_ref[...]
        df = jnp.where(lane < FX_H, acc * jax.nn.sigmoid(-z), 0.0)
        df_ref[...] = df.astype(BF16)
        dbf_ref[...] += jnp.sum(df, axis=0, keepdims=True)

    rev = lambda i: (ns - 1 - i, 0)
    return _call(body, name="fox_prep_grad", grid=(ns,),
                 in_specs=[pl.BlockSpec((ts, 128), lambda i: (ns - 1 - i, FF_COL // 128)),
                           pl.BlockSpec((1, 128), lambda i: (0, 0)),
                           pl.BlockSpec((ts, FX_W), rev), pl.BlockSpec((ts, FX_W), rev)],
                 out_specs=[pl.BlockSpec((ts, 128), rev), pl.BlockSpec((1, 128), lambda i: (0, 0))],
                 out_shape=[jax.ShapeDtypeStruct((s, 128), BF16), jax.ShapeDtypeStruct((1, 128), F32)],
                 scratch_shapes=[pltpu.VMEM((8, 128), F32)],
                 compiler_params=_params("arbitrary"))(p, bf, dfq, dfk)


def attn(qat, ka, vt):
    s = ka.shape[0]
    tq = min(TQ_FWD, s)
    nq = s // tq

    nh = ATTN_HEADS

    def body(qat_ref, ka_ref, vt_ref, o_ref, lse_ref):
        qi = pl.program_id(1)
        causal = _rows((tq, tq)) <= _lanes((tq, tq))
        lse_ref[...] = jnp.zeros_like(lse_ref)

        def step(kj, carry, masked):
            off = pl.multiple_of(kj * tq, tq)
            out = []
            for hh in range(nh):
                m, l, acc = carry[hh]
                st = jnp.dot(ka_ref[pl.ds(off, tq), hh * AUG:(hh + 1) * AUG], qat_ref[hh * AUG:(hh + 1) * AUG, :],
                             preferred_element_type=F32)
                if masked:
                    st = jnp.where(causal, st, NEG_INF)
                m_new = jnp.maximum(m, jnp.max(st, axis=0, keepdims=True))
                alpha = jnp.exp(m - m_new)
                pt = jnp.exp(st - m_new)
                v = vt_ref[hh * HD:(hh + 1) * HD, pl.ds(off, tq)]
                out.append((m_new, alpha * l + jnp.sum(pt, axis=0, keepdims=True),
                            alpha * acc + jnp.dot(v, pt.astype(BF16), preferred_element_type=F32)))
            return tuple(out)

        init = tuple((jnp.full((1, tq), NEG_INF, F32), jnp.zeros((1, tq), F32), jnp.zeros((HD, tq), F32))
                     for _ in range(nh))
        carry = lax.fori_loop(0, qi, lambda kj, c: step(kj, c, False), init)
        carry = step(qi, carry, True)
        for hh in range(nh):
            m, l, _ = carry[hh]
            lse_ref[hh // 2, hh % 2:hh % 2 + 1, :] = m + jnp.log(l)
        for j in range(nh // 2):
            pair = jnp.concatenate([carry[2 * j][2] / carry[2 * j][1], carry[2 * j + 1][2] / carry[2 * j + 1][1]], axis=0)
            o_ref[:, j * 128:(j + 1) * 128] = jnp.transpose(pair).astype(BF16)

    return _call(body, name="attn", grid=(FX_H // nh, nq),
                 in_specs=[pl.BlockSpec((nh * AUG, tq), lambda h, i: (h, i)),
                           pl.BlockSpec((s, nh * AUG), lambda h, i: (0, h)),
                           pl.BlockSpec((nh * HD, s), lambda h, i: (h, 0))],
                 out_specs=[pl.BlockSpec((tq, nh * HD), lambda h, i: (i, h)),
                            pl.BlockSpec((nh // 2, 8, tq), lambda h, i: (h, 0, i))],
                 out_shape=[jax.ShapeDtypeStruct((s, FX_W), BF16), jax.ShapeDtypeStruct((FX_H // 2, 8, s), F32)],
                 compiler_params=_params("arbitrary", "arbitrary"))(qat, ka, vt)


def attn_grad(qa, qat, ka, kat, p, o, lse, dmix):
    s = qa.shape[0]
    tq = min(TQ, s)
    nq = s // tq
    scale = HD ** -0.5

    def body(qa_ref, qat_ref, ka_ref, kat_ref, v_ref, o_ref, lse_ref, do_ref,
             dq_ref, dk_ref, dv_ref, dfq_ref, dfk_ref, dot_scr, delta, dqt):
        kj = pl.program_id(1)
        lane = _lanes((tq, 128))
        low = lane < HD
        causal = _rows((tq, tq)) <= _lanes((tq, tq))

        @pl.when(kj == 0)
        def _():
            dqt[...] = jnp.zeros_like(dqt)
            delta[...] = jnp.zeros_like(delta)

            def prep(c, _):
                rows = pl.ds(pl.multiple_of(c * tq, tq), tq)
                do = do_ref[rows, :]
                pt = jnp.transpose(do * o_ref[rows, :].astype(F32))
                delta[0:1, rows] = jnp.sum(pt[:HD], axis=0, keepdims=True)
                delta[1:2, rows] = jnp.sum(pt[HD:], axis=0, keepdims=True)
                dot_scr[:, rows] = jnp.transpose(do).astype(BF16)
                return 0

            lax.fori_loop(0, nq, prep, 0)

        v = v_ref[...]
        vms = [jnp.where(low, v, 0.0).astype(BF16), jnp.where(low, 0.0, v).astype(BF16)]

        def tile(qi, carry, masked):
            cols = pl.ds(pl.multiple_of(qi * tq, tq), tq)
            do = do_ref[cols, :].astype(BF16)
            out = []
            for hh in range(2):
                cs = slice(hh * AUG, (hh + 1) * AUG)
                dka, dv = carry[hh]
                st = jnp.dot(ka_ref[:, cs], qat_ref[cs, cols], preferred_element_type=F32)
                if masked:
                    st = jnp.where(causal, st, NEG_INF)
                pt = jnp.exp(st - lse_ref[0, hh:hh + 1, cols])
                dv = dv + jnp.dot(pt.astype(BF16), do, preferred_element_type=F32)
                dpt = jnp.dot(vms[hh], dot_scr[:, cols], preferred_element_type=F32)
                dsb = (pt * (dpt - delta[hh:hh + 1, cols])).astype(BF16)
                dka = dka + jnp.dot(dsb, qa_ref[cols, cs], preferred_element_type=F32)
                dqt[hh, :, cols] += jnp.dot(kat_ref[cs, :], dsb, preferred_element_type=F32)
                out.append((dka, dv))
            return tuple(out)

        init = tuple((jnp.zeros((tq, AUG), F32), jnp.zeros((tq, 128), F32)) for _ in range(2))
        carry = tile(kj, init, True)
        carry = lax.fori_loop(kj + 1, nq, lambda qi, c: tile(qi, c, False), carry)
        dks = [carry[0][0], carry[1][0]]
        dvs = [carry[0][1], carry[1][1]]
        dv_ref[...] = jnp.where(low, dvs[0], dvs[1]).astype(BF16)
        dk_ref[...] = jnp.where(low, dks[0], pltpu.roll(dks[1], HD, 1)).astype(BF16)
        dfk_ref[...] = jnp.where(lane == 0, -dks[0][:, BIAS_COL:BIAS_COL + 1],
                                 jnp.where(lane == 1, -dks[1][:, BIAS_COL:BIAS_COL + 1], 0.0))

        @pl.when(kj == nq - 1)
        def _():
            def finish(c, _):
                rows = pl.ds(pl.multiple_of(c * tq, tq), tq)
                t0 = jnp.transpose(dqt[0, :, rows])
                t1 = jnp.transpose(dqt[1, :, rows])
                dq_ref[rows, :] = (jnp.where(low, t0, pltpu.roll(t1, HD, 1)) * scale).astype(BF16)
                dfq_ref[rows, :] = jnp.where(lane == 0, t0[:, FQ_COL:FQ_COL + 1],
                                             jnp.where(lane == 1, t1[:, FQ_COL:FQ_COL + 1], 0.0))
                return 0

            lax.fori_loop(0, nq, finish, 0)

    seq128 = lambda blk: pl.BlockSpec((s, 128), lambda h, j: (0, blk + h))
    tile128 = pl.BlockSpec((tq, 128), lambda h, j: (j, h))
    out_b = jax.ShapeDtypeStruct((s, FX_W), BF16)
    out_f = jax.ShapeDtypeStruct((s, FX_W), F32)
    return _call(body, name="attn_grad", grid=(FX_H // 2, nq),
                 in_specs=[pl.BlockSpec((s, 2 * AUG), lambda h, j: (0, h)), pl.BlockSpec((2 * AUG, s), lambda h, j: (h, 0)),
                           pl.BlockSpec((tq, 2 * AUG), lambda h, j: (j, h)), pl.BlockSpec((2 * AUG, tq), lambda h, j: (h, j)),
                           pl.BlockSpec((tq, 128), lambda h, j: (j, V_BLK + h)), seq128(0),
                           pl.BlockSpec((1, 8, s), lambda h, j: (h, 0, 0)), seq128(4)],
                 out_specs=[seq128(0), tile128, tile128, seq128(0), tile128],
                 out_shape=[out_b, out_b, out_b, out_f, out_f],
                 scratch_shapes=[pltpu.VMEM((128, s), BF16), pltpu.VMEM((8, s), F32), pltpu.VMEM((2, AUG, s), F32)],
                 compiler_params=_params("arbitrary", "arbitrary"))(qa, qat, ka, kat, p, o, lse, dmix)


def _shift_down(a, prev8, k):
    r = pltpu.roll(a, k, 0)
    top = jnp.where(_rows(prev8.shape) < k, pltpu.roll(prev8, k, 0), r[0:8])
    return jnp.concatenate([top, r[8:]], axis=0)


def _shift_up(a, next8, k):
    t = a.shape[0]
    r = pltpu.roll(a, t - k, 0)
    bot = jnp.where(_rows(next8.shape) >= 8 - k, pltpu.roll(next8, 8 - k, 0), r[t - 8:t])
    return jnp.concatenate([r[:t - 8], bot], axis=0)


def _conv(a, prev8, cw, cb):
    return cb + cw[0:1, :] * _shift_down(a, prev8, 2) + cw[1:2, :] * _shift_down(a, prev8, 1) + cw[2:3, :] * a


GELU_K0 = math.sqrt(2.0 / math.pi)
GELU_K1 = GELU_K0 * 0.044715


def _gelu_parts(c):
    c2 = c * c
    return c2, 0.5 + 0.5 * jnp.tanh(c * (GELU_K0 + GELU_K1 * c2))


def conv_fwd(up, cw, cb):
    s = up.shape[0]
    ts = min(TS_C, s)

    def body(a_ref, g_ref, cw_ref, cb_ref, o_ref, c_ref, halo):
        @pl.when(pl.program_id(0) == 0)
        def _():
            halo[...] = jnp.zeros_like(halo)

        a = a_ref[...]
        c = _conv(a, halo[...], cw_ref[...], cb_ref[...])
        _, h = _gelu_parts(c)
        c_ref[...] = c
        o_ref[...] = (c * h * g_ref[...]).astype(BF16)
        halo[...] = a[ts - 8:ts, :]

    tile = pl.BlockSpec((ts, DFF), lambda i: (i, 0))
    return _call(body, name="conv_fwd", grid=(s // ts,),
                 in_specs=[tile, pl.BlockSpec((ts, DFF), lambda i: (i, 1)),
                           pl.BlockSpec((3, DFF), lambda i: (0, 0)), pl.BlockSpec((1, DFF), lambda i: (0, 0))],
                 out_specs=[tile, tile],
                 out_shape=[jax.ShapeDtypeStruct((s, DFF), BF16), jax.ShapeDtypeStruct((s, DFF), F32)],
                 scratch_shapes=[pltpu.VMEM((8, DFF), F32)],
                 compiler_params=_params("arbitrary"))(up, up, cw, cb)


def conv_bwd(up, c, cw, dact):
    s = up.shape[0]
    ts = min(TS_C, s)
    ns = s // ts

    def body(a_ref, g_ref, c_ref, cw_ref, dact_ref, dup_ref, dcw_ref, dcb_ref, halo):
        @pl.when(pl.program_id(0) == 0)
        def _():
            halo[...] = jnp.zeros_like(halo)
            dcw_ref[...] = jnp.zeros_like(dcw_ref)
            dcb_ref[...] = jnp.zeros_like(dcb_ref)

        a = a_ref[...]
        cw = cw_ref[...]
        cv = c_ref[...]
        dact = dact_ref[...]
        c2, h = _gelu_parts(cv)
        dup_ref[:, DFF:] = (dact * (cv * h)).astype(BF16)
        dgel = h + cv * (2.0 * h * (1.0 - h)) * (GELU_K0 + 3.0 * GELU_K1 * c2)
        dc = dact * g_ref[...] * dgel
        up1 = _shift_up(dc, halo[...], 1)
        up2 = _shift_up(dc, halo[...], 2)
        dup_ref[:, :DFF] = (cw[2:3, :] * dc + cw[1:2, :] * up1 + cw[0:1, :] * up2).astype(BF16)
        dcw_ref[0:1, :] += jnp.sum(a * up2, axis=0, keepdims=True)
        dcw_ref[1:2, :] += jnp.sum(a * up1, axis=0, keepdims=True)
        dcw_ref[2:3, :] += jnp.sum(a * dc, axis=0, keepdims=True)
        dcb_ref[...] += jnp.sum(dc, axis=0, keepdims=True)
        halo[...] = dc[0:8, :]

    rev = lambda col: (lambda i: (ns - 1 - i, col))
    return _call(body, name="conv_bwd", grid=(ns,),
                 in_specs=[pl.BlockSpec((ts, DFF), rev(0)), pl.BlockSpec((ts, DFF), rev(1)),
                           pl.BlockSpec((ts, DFF), rev(0)), pl.BlockSpec((3, DFF), lambda i: (0, 0)),
                           pl.BlockSpec((ts, DFF), rev(0))],
                 out_specs=[pl.BlockSpec((ts, 2 * DFF), rev(0)), pl.BlockSpec((3, DFF), lambda i: (0, 0)),
                            pl.BlockSpec((1, DFF), lambda i: (0, 0))],
                 out_shape=[jax.ShapeDtypeStruct((s, 2 * DFF), BF16), jax.ShapeDtypeStruct((3, DFF), F32),
                            jax.ShapeDtypeStruct((1, DFF), F32)],
                 scratch_shapes=[pltpu.VMEM((8, DFF), F32)],
                 compiler_params=_params("arbitrary"))(up, up, c, cw, dact)


def _blockdiag_expand(m):
    m4 = m.reshape(S5_H, 2, S5_G, S5_P)
    eye = jnp.eye(S5_G, dtype=bool)[:, None, None, :, None]
    return jnp.where(eye, m4[None], 0.0).reshape(S5_W, 2 * S5_N)


def _blockdiag_extract(mbd):
    m5 = mbd.reshape(S5_G, S5_H, 2, S5_G, S5_P)
    diag = jnp.stack([m5[g, :, :, g, :] for g in range(S5_G)], axis=2)
    return diag.reshape(S5_H, 2 * S5_N)


def _c_expand(c_re, c_im):
    c4 = jnp.stack([c_re, c_im], axis=2)
    eye = jnp.eye(S5_G, dtype=bool)[:, None, None, :, None]
    return jnp.where(eye, c4[:, :, :, None, :], 0.0).reshape(S5_W, 2 * S5_N)


def _c_extract(cbd):
    m5 = cbd.reshape(S5_G, S5_H, 2, S5_G, S5_P)
    d = jnp.stack([m5[g, :, :, g, :] for g in range(S5_G)], axis=0)
    return d[:, :, 0, :], d[:, :, 1, :]


def _glu_expand(w):
    eye = jnp.eye(S5_G, dtype=bool)[:, None, :, None]
    return jnp.where(eye, w[:, :, None, :], 0.0).reshape(S5_W, S5_W)


def _glu_extract(wbd):
    m4 = wbd.reshape(S5_G, S5_H, S5_G, S5_H)
    return jnp.stack([m4[g, :, g, :] for g in range(S5_G)], axis=0)


SMALL = ("b_f", "gm_ln_g", "gm_ln_b", "gm_w_s", "gm_b_s", "s5_lam_re", "s5_lam_im", "s5_log_dt", "s5_b_re", "s5_b_im",
         "s5_c_re", "s5_c_im", "s5_d", "s5_w_glu", "s5_b_glu", "ln1_g", "ln1_b", "conv_b", "ln2_g", "ln2_b")


def _layer_operands(sp, l):
    f = {}
    f["bf"] = jnp.pad(sp["b_f"][l][None, :], ((0, 0), (0, 128 - FX_H)))
    f["gm_lg"] = sp["gm_ln_g"][l].reshape(1, GM_W)
    f["gm_lb"] = sp["gm_ln_b"][l].reshape(1, GM_W)
    f["gm_ws"] = sp["gm_w_s"][l]
    f["gm_bst"] = sp["gm_b_s"][l].T
    f["lr"] = sp["s5_lam_re"][l].reshape(1, S5_N)
    f["li"] = sp["s5_lam_im"][l].reshape(1, S5_N)
    f["ldt"] = jnp.repeat(sp["s5_log_dt"][l], S5_P).reshape(1, S5_N)
    bt = lambda b: jnp.transpose(b, (2, 0, 1)).reshape(S5_H, S5_N)
    f["bt"] = jnp.concatenate([bt(sp["s5_b_re"][l]), bt(sp["s5_b_im"][l])], axis=1)
    f["cbd"] = _c_expand(sp["s5_c_re"][l], sp["s5_c_im"][l])
    f["drow"] = sp["s5_d"][l].reshape(1, S5_W)
    f["wg"] = _glu_expand(sp["s5_w_glu"][l])
    f["bg"] = sp["s5_b_glu"][l].reshape(1, S5_W)
    for n in ("ln1_g", "ln1_b", "ln2_g", "ln2_b"):
        f[n] = sp[n][l][None, :]
    f["cb"] = sp["conv_b"][l][None, :]
    return f


def layer_fwd_mix(x, mod, w_in, f):
    p = mm_nn(x, w_in, NP, "in_proj", mod=mod, rows=(0, 1))
    ygm = gm_fwd(p, f["gm_lg"], f["gm_lb"], f["gm_ws"], f["gm_bst"])
    arow, bbt = s5_prep_fwd(f["lr"], f["li"], f["ldt"], f["bt"])
    bbd = _blockdiag_expand(bbt)
    ys5, st = s5_fwd(p, arow, bbd, f["cbd"], f["drow"], f["wg"], f["bg"])
    qa, ka, qat, kat, vt = fox_prep(p, f["bf"])
    yfx, lse = attn(qat, ka, vt)
    mixcat = jnp.concatenate([ygm, ys5, yfx], axis=1)
    return mixcat, dict(f=f, x=x, p=p, arow=arow, bbd=bbd, st=st, qa=qa, ka=ka, qat=qat, kat=kat, yfx=yfx, lse=lse,
                        mixcat=mixcat)


def layer_fwd_rest(x, mixcat, mod, w, f, saved):
    mix = mm_nn(mixcat, w["w_out"], D, "out_proj")
    x1 = post_fwd(x, mix, mod, 2, f["ln1_g"], f["ln1_b"], "post1_fwd")
    up = mm_nn(x1, w["w_up"], DFF // 2, "up_proj", mod=mod, rows=(3, 4))
    act, conv = conv_fwd(up, w["conv_w"], f["cb"])
    ffn = mm_nn(act, w["w_down"], D, "down_proj")
    x2 = post_fwd(x1, ffn, mod, 5, f["ln2_g"], f["ln2_b"], "post2_fwd")
    return x2, dict(saved, mix=mix, x1=x1, up=up, conv=conv, act=act, ffn=ffn)


def layer_fwd(x, mod, w, f):
    mixcat, saved = layer_fwd_mix(x, mod, w["w_in"], f)
    return layer_fwd_rest(x, mixcat, mod, w, f, saved)


def layer_bwd_ffn(dx, sv, mod, w):
    f = sv["f"]
    dx1, dffn, dg2, dlg2, dlb2 = post_bwd(sv["x1"], sv["ffn"], mod, 5, f["ln2_g"], f["ln2_b"], dx, "post2_bwd")
    g_down = mm_tn(sv["act"], dffn, D // 2, "down_dw")
    dact = mm_nt(dffn, w["w_down"], "down_dx")
    dup, dcw, dcb = conv_bwd(sv["up"], sv["conv"], w["conv_w"], dact)
    g_up = mm_tn(sv["x1"], dup, DFF // 2, "up_dw", mod=mod, rows=(3, 4))
    dx1, dsh2, dsc2 = mm_nt_mod(dup, w["w_up"], sv["x1"], dx1, mod, (3, 4), "up_dx")
    return dx1, dict(w_up=g_up, w_down=g_down, conv_w=dcw), dict(dsh2=dsh2, dsc2=dsc2, dg2=dg2, conv_b=dcb[0],
                                                                 ln2_g=dlg2[0], ln2_b=dlb2[0])


def layer_bwd_mix(dx1, sv, mod, w, part):
    f = sv["f"]
    dx0, dmix, dg1, dlg1, dlb1 = post_bwd(sv["x"], sv["mix"], mod, 2, f["ln1_g"], f["ln1_b"], dx1, "post1_bwd")
    g_out = mm_tn(sv["mixcat"], dmix, D, "out_dw")
    dmc = mm_nt(dmix, w["w_out"], "out_dx")
    duv, dgm_lg, dgm_lb, dgm_ws, dgm_bst = gm_bwd(sv["p"], f["gm_lg"], f["gm_lb"], f["gm_ws"], f["gm_bst"], dmc)
    du5, da, dbbd, dcbd, dd5, dwg, dbg = s5_bwd(sv["p"], sv["st"], sv["arow"], sv["bbd"], f["cbd"], f["drow"],
                                                f["wg"], f["bg"], dmc)
    dlr, dli, dldt, dbt = s5_prep_bwd(f["lr"], f["li"], f["ldt"], f["bt"], da, _blockdiag_extract(dbbd))
    dq, dk, dv, dfq, dfk = attn_grad(sv["qa"], sv["qat"], sv["ka"], sv["kat"], sv["p"], sv["yfx"], sv["lse"], dmc)
    dff, dbf = fox_prep_grad(sv["p"], f["bf"], dfq, dfk)
    dp = jnp.concatenate([duv, du5, dq, dk, dv, dff], axis=1)
    g_in = mm_tn(sv["x"], dp, NP, "in_dw", mod=mod, rows=(0, 1))
    dx, dsh1, dsc1 = mm_nt_mod(dp, w["w_in"], sv["x"], dx0, mod, (0, 1), "in_dx")

    dmod = jnp.concatenate([dsh1, dsc1, dg1, part["dsh2"], part["dsc2"], part["dg2"]], axis=0)
    dc_re, dc_im = _c_extract(dcbd)
    dbt4 = dbt.reshape(S5_H, 2, S5_G, S5_P)
    vals = dict(b_f=dbf[0, :FX_H], gm_ln_g=dgm_lg.reshape(GM_H, HD), gm_ln_b=dgm_lb.reshape(GM_H, HD),
                gm_w_s=dgm_ws, gm_b_s=dgm_bst.T, s5_lam_re=dlr.reshape(S5_G, S5_P),
                s5_lam_im=dli.reshape(S5_G, S5_P), s5_log_dt=dldt[0, :S5_G],
                s5_b_re=jnp.transpose(dbt4[:, 0], (1, 2, 0)), s5_b_im=jnp.transpose(dbt4[:, 1], (1, 2, 0)),
                s5_c_re=dc_re, s5_c_im=dc_im, s5_d=dd5.reshape(S5_G, S5_H), s5_w_glu=_glu_extract(dwg),
                s5_b_glu=dbg.reshape(S5_G, S5_H), ln1_g=dlg1[0], ln1_b=dlb1[0], conv_b=part["conv_b"],
                ln2_g=part["ln2_g"], ln2_b=part["ln2_b"])
    return dx, dict(w_in=g_in, w_out=g_out), vals, dmod


def layer_bwd(dx, sv, mod, w):
    dx1, g_ffn, part = layer_bwd_ffn(dx, sv, mod, w)
    dx, g_mix, vals, dmod = layer_bwd_mix(dx1, sv, mod, w, part)
    return dx, dict(g_ffn, **g_mix), vals, dmod


def local_step(x, target, mods, big, sp):
    saved = []
    for l in range(DEPTH):
        x, sv = layer_fwd(x, mods[l], big[l], _layer_operands(sp, l))
        saved.append(sv)
    loss_tile, dx = loss_kernel(x, target)
    gbig, vals, dmods = [None] * DEPTH, [None] * DEPTH, [None] * DEPTH
    for l in reversed(range(DEPTH)):
        dx, gbig[l], vals[l], dmods[l] = layer_bwd(dx, saved[l], mods[l], big[l])
    gsm = {n: jnp.stack([v[n] for v in vals]) for n in SMALL}
    return loss_tile, dx, gbig, gsm, jnp.stack(dmods)


def _my_index():
    return 4 * lax.axis_index("x") + 2 * lax.axis_index("y") + lax.axis_index("c")


def exchange(tensors, scatter, name):
    n = len(tensors)

    def body(*refs):
        ins, outs = refs[:n], refs[n:2 * n]
        send_sems, recv_sems, local_sems = refs[2 * n:]
        x, y, c = lax.axis_index("x"), lax.axis_index("y"), lax.axis_index("c")
        me = 4 * x + 2 * y + c
        local = []
        for t in range(n):
            cp = pltpu.make_async_copy(ins[t].at[me] if scatter else ins[t], outs[t].at[me], local_sems.at[t])
            cp.start()
            local.append(cp)
        remote = []
        for m in range(1, NDEV):
            px = 1 - x if m & 4 else x
            py = 1 - y if m & 2 else y
            pc = 1 - c if m & 1 else c
            peer = 4 * px + 2 * py + pc
            for t in range(n):
                k = t * (NDEV - 1) + m - 1
                cp = pltpu.make_async_remote_copy(
                    src_ref=ins[t].at[peer] if scatter else ins[t], dst_ref=outs[t].at[me],
                    send_sem=send_sems.at[k], recv_sem=recv_sems.at[k],
                    device_id=(px, py, pc), device_id_type=MESH_IDS)
                cp.start()
                remote.append(cp)
        for cp in remote:
            cp.wait()
        for cp in local:
            cp.wait()

    hbm = pl.BlockSpec(memory_space=pltpu.HBM)
    out_shape = [jax.ShapeDtypeStruct(t.shape if scatter else (NDEV,) + t.shape, t.dtype) for t in tensors]
    return _call(body, name=name, in_specs=[hbm] * n, out_specs=[hbm] * n, out_shape=out_shape,
                 scratch_shapes=[pltpu.SemaphoreType.DMA((n * (NDEV - 1),)), pltpu.SemaphoreType.DMA((n * (NDEV - 1),)),
                                 pltpu.SemaphoreType.DMA((n,))])(*tensors)


def _peers():
    x, y, c = lax.axis_index("x"), lax.axis_index("y"), lax.axis_index("c")
    out = []
    for m in range(1, NDEV):
        px = 1 - x if m & 4 else x
        py = 1 - y if m & 2 else y
        pc = 1 - c if m & 1 else c
        out.append(((px, py, pc), 4 * px + 2 * py + pc))
    return 4 * x + 2 * y + c, out


def _split_copies(v_refs, land_refs, send_sems, recv_sems, scatter):
    me, peers = _peers()
    return [pltpu.make_async_remote_copy(
        src_ref=v_ref.at[idx] if scatter else v_ref, dst_ref=land_ref.at[me],
        send_sem=send_sems.at[t * (NDEV - 1) + k], recv_sem=recv_sems.at[t * (NDEV - 1) + k],
        device_id=pos, device_id_type=MESH_IDS)
        for t, (v_ref, land_ref) in enumerate(zip(v_refs, land_refs)) for k, (pos, idx) in enumerate(peers)]


_HBM_SPEC = pl.BlockSpec(memory_space=pltpu.HBM)
_SEM_SPEC = pl.BlockSpec(memory_space=pltpu.SEMAPHORE)
_SPLIT_EFFECT = pltpu.SideEffectType.DATAFLOW_SIDE_EFFECTING


def exchange_start(tensors, scatter, name):
    n = len(tensors)
    land_shapes = [t.shape if scatter else (NDEV,) + t.shape for t in tensors]

    def body(*refs):
        v_refs, land_refs = refs[:n], refs[n:2 * n]
        send_sems, recv_sems = refs[2 * n], refs[2 * n + 1]
        token = refs[-1]
        for cp in _split_copies(v_refs, land_refs, send_sems, recv_sems, scatter):
            cp.start()
        token[...] = jnp.zeros_like(token)

    sems = pltpu.SemaphoreType.DMA((n * (NDEV - 1),))
    out = _call(
        body, name=name,
        out_shape=(sems, sems, *[pltpu.HBM(t.shape, t.dtype) for t in tensors],
                   *[pltpu.HBM(s, t.dtype) for s, t in zip(land_shapes, tensors)], jax.ShapeDtypeStruct((8, 128), F32)),
        in_specs=(_HBM_SPEC,) * (2 * n),
        out_specs=(_SEM_SPEC, _SEM_SPEC) + (_HBM_SPEC,) * (2 * n) + (pl.BlockSpec(memory_space=pltpu.VMEM),),
        input_output_aliases={i: i + 2 for i in range(2 * n)},
        compiler_params=pltpu.CompilerParams(has_side_effects=_SPLIT_EFFECT),
    )(*[pltpu.with_memory_space_constraint(t, pltpu.HBM) for t in tensors],
      *[pltpu.with_memory_space_constraint(lax.empty(s, t.dtype), pltpu.HBM) for s, t in zip(land_shapes, tensors)])
    return out[0], out[1], list(out[2:2 + n]), list(out[2 + n:2 + 2 * n]), out[-1]


def exchange_wait(started, after, scatter, name):
    send_sems, recv_sems, v_thru, land_thru, _ = started
    n = len(v_thru)

    def body(*refs):
        v_refs, land_refs = refs[:n], refs[n:2 * n]
        for cp in _split_copies(v_refs, land_refs, refs[2 * n], refs[2 * n + 1], scatter):
            cp.wait_send()
            cp.wait_recv()

    out = _call(
        body, name=name,
        out_shape=tuple(pltpu.HBM(t.shape, t.dtype) for t in v_thru + land_thru),
        in_specs=(_HBM_SPEC,) * (2 * n) + (_SEM_SPEC, _SEM_SPEC, pl.BlockSpec(memory_space=pl.ANY)),
        out_specs=(_HBM_SPEC,) * (2 * n), input_output_aliases={i: i for i in range(2 * n)},
        compiler_params=pltpu.CompilerParams(has_side_effects=_SPLIT_EFFECT),
    )(*v_thru, *land_thru, send_sems, recv_sems, after)
    return list(out[:n]), list(out[n:])


def mod_slices(c_all, w_ada, b_loc):
    nl, _, nc = w_ada.shape

    def body(c_ref, w_ref, b_ref, o_ref):
        cv = c_ref[...]
        o_ref[0] = _nn(cv * jax.nn.sigmoid(cv), w_ref[0]) + b_ref[0]

    return _call(body, name="mod_slices", grid=(nl,),
                 in_specs=[pl.BlockSpec((NDEV, D), lambda l: (0, 0)), pl.BlockSpec((1, D, nc), lambda l: (l, 0, 0)),
                           pl.BlockSpec((1, 1, nc), lambda l: (l, 0, 0))],
                 out_specs=pl.BlockSpec((1, NDEV, nc), lambda l: (l, 0, 0)),
                 out_shape=jax.ShapeDtypeStruct((nl, NDEV, nc), F32),
                 compiler_params=_params("arbitrary"))(c_all, w_ada, b_loc.reshape(nl, 1, nc))


def ada_grad(c_all, dm_loc):
    nl, _, nc = dm_loc.shape

    def body(c_ref, d_ref, o_ref):
        cv = c_ref[...]
        o_ref[0] = _tn(cv * jax.nn.sigmoid(cv), d_ref[0])

    return _call(body, name="ada_grad", grid=(nl,),
                 in_specs=[pl.BlockSpec((NDEV, D), lambda l: (0, 0)), pl.BlockSpec((1, NDEV, nc), lambda l: (l, 0, 0))],
                 out_specs=pl.BlockSpec((1, D, nc), lambda l: (l, 0, 0)),
                 out_shape=jax.ShapeDtypeStruct((nl, D, nc), F32),
                 compiler_params=_params("arbitrary"))(c_all, dm_loc)


def sum_chunks(chunks):
    r = chunks.shape[1]

    def body(c_ref, o_ref):
        acc = c_ref[0]
        for i in range(1, NDEV):
            acc = acc + c_ref[i]
        o_ref[...] = acc

    return _call(body, name="sum_chunks", out_shape=jax.ShapeDtypeStruct((r, 128), F32))(chunks)


def _row_tile(r):
    if r <= 256:
        return r
    for t in range(256, 7, -8):
        if r % t == 0:
            return t
    return r


def adamw(w, m, v, g=None, chunks=None, name="adamw"):
    r, cdim = w.shape
    tr = _row_tile(r)
    bc1 = 1.0 - ADAM_B1 ** ADAM_STEP
    bc2 = 1.0 - ADAM_B2 ** ADAM_STEP

    def body(g_ref, w_ref, m_ref, v_ref, go_ref, d_ref, mo_ref, vo_ref):
        if chunks is None:
            grad = g_ref[...]
        else:
            grad = g_ref[0].astype(F32)
            for i in range(1, NDEV):
                grad = grad + g_ref[i].astype(F32)
        mn = ADAM_B1 * m_ref[...] + (1.0 - ADAM_B1) * grad
        vn = ADAM_B2 * v_ref[...] + (1.0 - ADAM_B2) * (grad * grad)
        m_hat = mn / bc1
        v_hat = vn / bc2
        go_ref[...] = grad
        d_ref[...] = -ADAM_LR * (m_hat / (jnp.sqrt(v_hat) + ADAM_EPS) + ADAM_WD * w_ref[...])
        mo_ref[...] = mn
        vo_ref[...] = vn

    tile = pl.BlockSpec((tr, cdim), lambda i: (i, 0))
    gspec = tile if chunks is None else pl.BlockSpec((NDEV, tr, cdim), lambda i: (0, i, 0))
    shp = jax.ShapeDtypeStruct((r, cdim), F32)
    return _call(body, name=name, grid=(r // tr,), in_specs=[gspec, tile, tile, tile],
                 out_specs=[tile] * 4, out_shape=[shp] * 4,
                 compiler_params=_params("arbitrary"))(g if chunks is None else chunks, w, m, v)


def adamw_layers(w, m, v, chunks, name):
    nl, r, cdim = w.shape
    tr = _row_tile(r)
    bc1 = 1.0 - ADAM_B1 ** ADAM_STEP
    bc2 = 1.0 - ADAM_B2 ** ADAM_STEP
    outs = [lax.empty(w.shape, F32) for _ in range(4)]
    for l in range(nl):
        def body(g_ref, w_ref, m_ref, v_ref, p0, p1, p2, p3, go_ref, d_ref, mo_ref, vo_ref):
            grad = g_ref[0].astype(F32)
            for i in range(1, NDEV):
                grad = grad + g_ref[i].astype(F32)
            mn = ADAM_B1 * m_ref[...] + (1.0 - ADAM_B1) * grad
            vn = ADAM_B2 * v_ref[...] + (1.0 - ADAM_B2) * (grad * grad)
            go_ref[...] = grad
            d_ref[...] = -ADAM_LR * ((mn / bc1) / (jnp.sqrt(vn / bc2) + ADAM_EPS) + ADAM_WD * w_ref[...])
            mo_ref[...] = mn
            vo_ref[...] = vn

        tile = pl.BlockSpec((None, tr, cdim), lambda i, l=l: (l, i, 0))
        whole = pl.BlockSpec(memory_space=pl.ANY)
        outs = _call(body, name=f"{name}_{l}", grid=(r // tr,),
                     in_specs=[pl.BlockSpec((NDEV, tr, cdim), lambda i: (0, i, 0)), tile, tile, tile] + [whole] * 4,
                     out_specs=[tile] * 4, out_shape=[jax.ShapeDtypeStruct(w.shape, F32)] * 4,
                     input_output_aliases={4: 0, 5: 1, 6: 2, 7: 3},
                     compiler_params=_params("arbitrary"))(chunks[l], w, m, v, *outs)
    return outs


WEIGHTS = ("w_ada", "b_ada", "w_in", "b_f", "gm_ln_g", "gm_ln_b", "gm_w_s", "gm_b_s", "s5_lam_re", "s5_lam_im",
           "s5_log_dt", "s5_b_re", "s5_b_im", "s5_c_re", "s5_c_im", "s5_d", "s5_w_glu", "s5_b_glu", "w_out", "ln1_g",
           "ln1_b", "w_up", "conv_w", "conv_b", "w_down", "ln2_g", "ln2_b")
SHARDED = ("w_in", "w_out", "w_up", "w_down", "conv_w")
LARGE = ("w_in", "w_out", "w_up", "w_down")
COL_SHARDED = ("w_in", "w_up", "conv_w")
PACKED = ("b_ada",) + SMALL
PACK_SEG = 8 * 128


def _gather_cols(g):
    nd, nl, r, c = g.shape
    return jnp.transpose(g, (1, 2, 0, 3)).reshape(nl, r, nd * c)


def _chunk_cols(g):
    nl, r, c8 = g.shape
    return jnp.transpose(g.reshape(nl, r, NDEV, c8 // NDEV), (2, 0, 1, 3))


def _join_cols(g):
    nd, r, c = g.shape
    return jnp.transpose(g, (1, 0, 2)).reshape(r, nd * c)


def _join_rows(g):
    nd, r, c = g.shape
    return g.reshape(nd * r, c)


def _split_cols(g):
    r, c8 = g.shape
    return jnp.transpose(g.reshape(r, NDEV, c8 // NDEV), (1, 0, 2))


def _split_rows(g):
    r8, c = g.shape
    return g.reshape(NDEV, r8 // NDEV, c)


def _pack(parts):
    segs = []
    for n in PACKED:
        flat = parts[n].reshape(-1)
        segs.append(jnp.pad(flat, (0, -flat.shape[0] % PACK_SEG)).reshape(-1, 128))
    rows = jnp.concatenate(segs, axis=0)
    return jnp.pad(rows, ((0, -rows.shape[0] % (NDEV * 8)), (0, 0)))


def _unpack(rows, shapes):
    out, off = {}, 0
    for n in PACKED:
        size = math.prod(shapes[n])
        nrows = -(-size // PACK_SEG) * 8
        out[n] = rows[off:off + nrows].reshape(-1)[:size].reshape(shapes[n])
        off += nrows
    return out


def kernel(x, c, w_ada, b_ada, w_in, b_f, gm_ln_g, gm_ln_b, gm_w_s, gm_b_s, s5_lam_re, s5_lam_im, s5_log_dt, s5_b_re, s5_b_im, s5_c_re, s5_c_im, s5_d, s5_w_glu, s5_b_glu, w_out, ln1_g, ln1_b, w_up, conv_w, conv_b, w_down, ln2_g, ln2_b, loss_target, m_w_ada, m_b_ada, m_w_in, m_b_f, m_gm_ln_g, m_gm_ln_b, m_gm_w_s, m_gm_b_s, m_s5_lam_re, m_s5_lam_im, m_s5_log_dt, m_s5_b_re, m_s5_b_im, m_s5_c_re, m_s5_c_im, m_s5_d, m_s5_w_glu, m_s5_b_glu, m_w_out, m_ln1_g, m_ln1_b, m_w_up, m_conv_w, m_conv_b, m_w_down, m_ln2_g, m_ln2_b, v_w_ada, v_b_ada, v_w_in, v_b_f, v_gm_ln_g, v_gm_ln_b, v_gm_w_s, v_gm_b_s, v_s5_lam_re, v_s5_lam_im, v_s5_log_dt, v_s5_b_re, v_s5_b_im, v_s5_c_re, v_s5_c_im, v_s5_d, v_s5_w_glu, v_s5_b_glu, v_w_out, v_ln1_g, v_ln1_b, v_w_up, v_conv_w, v_conv_b, v_w_down, v_ln2_g, v_ln2_b):
    given = dict(locals())
    wts = {n: given[n] for n in WEIGHTS}
    mom = {n: given["m_" + n] for n in WEIGHTS}
    var = {n: given["v_" + n] for n in WEIGHTS}
    nl = w_ada.shape[0]
    me = _my_index()
    ada_cols = w_ada.shape[2]

    (c_all,) = exchange([c], False, "gather_c")
    c_all = c_all.reshape(NDEV, D)
    b_loc = lax.dynamic_slice_in_dim(b_ada, me * ada_cols, ada_cols, axis=1)
    mod_part = mod_slices(c_all, w_ada, b_loc)

    mod_all, conv_all = exchange([mod_part, conv_w], False, "gather_mod")
    mod_mine = lax.dynamic_index_in_dim(mod_all, me, axis=2, keepdims=False)
    mods = jnp.transpose(mod_mine, (1, 0, 2)).reshape(nl, 6, D)
    mods = jnp.pad(mods, ((0, 0), (0, 2), (0, 0)))
    conv_full = _gather_cols(conv_all)
    sp = {n: wts[n] for n in SMALL}

    def joined(name, own, land):
        land = lax.dynamic_update_index_in_dim(land, own, me, 0)
        return (_join_cols if name in COL_SHARDED else _join_rows)(land)

    def block(l, names):
        return [wts[n][l].astype(BF16) for n in names]

    def chunked(grads, names):
        return [(_split_cols if n in COL_SHARDED else _split_rows)(grads[n]) for n in names]

    head, tail = LARGE[:1], LARGE[1:]
    got_head = exchange_start(block(0, head), False, "gather_start_0_in")
    rest = block(0, tail)
    rest[0] = rest[0] + got_head[4][0, 0].astype(BF16)
    got_tail = exchange_start(rest, False, "gather_start_0_rest")
    xl, saved, weights = x[0], [], []
    for l in range(nl):
        if l == 0:
            own, land = exchange_wait(got_head, got_tail[4], False, "gather_wait_0_in")
            w = {n: joined(n, o, g) for n, o, g in zip(head, own, land)}
        else:
            own, land = exchange_wait(started, xl, False, f"gather_wait_{l}")
            w = {n: joined(n, o, g) for n, o, g in zip(LARGE, own, land)}
        mod_l = mods[l]
        if l + 1 < nl:
            nxt, w["w_in"] = lax.optimization_barrier((block(l + 1, LARGE), w["w_in"]))
            started = exchange_start(nxt, False, f"gather_start_{l + 1}")
            mod_l = mod_l + started[4][0, 0]
        w["w_in"] = jnp.pad(w["w_in"], ((0, 0), (0, NP - D_IN)))
        f = _layer_operands(sp, l)
        mixcat, sv = layer_fwd_mix(xl, mod_l, w["w_in"], f)
        if l == 0:
            own, land = exchange_wait(got_tail, mixcat, False, "gather_wait_0_rest")
            w.update({n: joined(n, o, g) for n, o, g in zip(tail, own, land)})
        w["conv_w"] = conv_full[l]
        xl, sv = layer_fwd_rest(xl, mixcat, mod_l, w, f, sv)
        weights.append(w)
        saved.append(sv)

    loss_tile, dx = loss_kernel(xl, loss_target[0])

    ffn_names, mix_names = ("w_up", "w_down"), ("w_in", "w_out")
    scattering, vals, dmods, gconv = [None] * nl, [None] * nl, [None] * nl, [None] * nl
    token = jnp.zeros((), F32)
    for l in reversed(range(nl)):
        mod_l = mods[l] + token
        dx1, g_ffn, part = layer_bwd_ffn(dx, saved[l], mod_l, weights[l])
        gconv[l] = g_ffn["conv_w"]
        if l == 0:
            sent_ffn = exchange_start(chunked(g_ffn, ffn_names), True, "scatter_start_0_ffn")
            mod_l = mod_l + sent_ffn[4][0, 0]
        dx, g_mix, vals[l], dmods[l] = layer_bwd_mix(dx1, saved[l], mod_l, weights[l], part)
        g_mix["w_in"] = g_mix["w_in"][:, :D_IN]
        if l == 0:
            scattering[l] = [(ffn_names, sent_ffn),
                             (mix_names, exchange_start(chunked(g_mix, mix_names), True, "scatter_start_0_mix"))]
        else:
            sent = exchange_start(chunked(dict(g_ffn, **g_mix), LARGE), True, f"scatter_start_{l}")
            scattering[l] = [(LARGE, sent)]
            token = sent[4][0, 0]
    gx = dx
    dmods = jnp.stack(dmods)
    gsm = {n: jnp.stack([v[n] for v in vals]) for n in SMALL}

    gsm["b_ada"] = dmods.reshape(nl, 6 * D)
    packed = _pack(gsm).reshape(NDEV, -1, 128)
    conv_recv, small_recv = exchange([_chunk_cols(jnp.stack(gconv)), packed], True, "scatter_small")
    small_sum = sum_chunks(small_recv)
    small_all, dmod_all = exchange([small_sum, dmods.reshape(nl, 6 * D)], False, "gather_small")

    received = [dict() for _ in range(nl)]

    def arrive(l, k, after):
        names, sent = scattering[l][k]
        own, land = exchange_wait(sent, after, True, f"scatter_wait_{l}_{k}")
        for n, o, g in zip(names, own, land):
            mine = lax.dynamic_index_in_dim(o, me, 0, keepdims=False)
            received[l][n] = lax.dynamic_update_index_in_dim(g, mine, me, 0)
        return land[0]

    after = small_all
    for l in reversed(range(nl)):
        after = arrive(l, 0, after)
    out = {}
    for n in ffn_names:
        out[n] = adamw_layers(wts[n], mom[n], var[n], [received[l][n] for l in range(nl)], "adamw_" + n)
    arrive(0, 1, out[ffn_names[-1]][0])
    for n in mix_names:
        out[n] = adamw_layers(wts[n], mom[n], var[n], [received[l][n] for l in range(nl)], "adamw_" + n)
    shp = conv_w.shape
    two_d = lambda a: a.reshape(shp[0] * shp[1], shp[2])
    res = adamw(two_d(conv_w), two_d(m_conv_w), two_d(v_conv_w),
                chunks=conv_recv.reshape(NDEV, shp[0] * shp[1], shp[2]), name="adamw_conv_w")
    out["conv_w"] = [r.reshape(shp) for r in res]

    dm_loc = lax.dynamic_slice_in_dim(dmod_all, me * ada_cols, ada_cols, axis=2)
    g_ada = ada_grad(c_all, jnp.transpose(dm_loc, (1, 0, 2)))
    two_d = lambda a: a.reshape(nl * D, ada_cols)
    res = adamw(two_d(w_ada), two_d(m_w_ada), two_d(v_w_ada), g=two_d(g_ada), name="adamw_w_ada")
    out["w_ada"] = [r.reshape(w_ada.shape) for r in res]

    shapes = {n: wts[n].shape for n in PACKED}
    res = adamw(_pack(wts), _pack(mom), _pack(var), g=small_all.reshape(-1, 128), name="adamw_small")
    unpacked = [_unpack(r, shapes) for r in res]
    for n in PACKED:
        out[n] = [u[n] for u in unpacked]

    loss = lax.psum(loss_tile[0, 0], ("x", "y", "c"))
    return (loss, gx[None], *[out[n][0] for n in WEIGHTS], *[out[n][1] for n in WEIGHTS],
            *[out[n][2] for n in WEIGHTS], *[out[n][3] for n in WEIGHTS])
```

```python
import functools
import math

import jax
import jax.numpy as jnp
from jax import lax
from jax.experimental import pallas as pl
from jax.experimental.pallas import tpu as pltpu

F32 = jnp.float32
BF16 = jnp.bfloat16
MESH_IDS = pl.DeviceIdType.MESH

D = 1024
SEQ = 4096
DEPTH = 4
NDEV = 8
HD = 64
GM_W = 256
GM_H = 4
GM_C = 128
S5_W = 256
S5_G = 16
S5_H = 16
S5_P = 64
S5_N = S5_G * S5_P
FX_W = 512
FX_H = 8
D_IN = 2 * GM_W + S5_W + 3 * FX_W + FX_H
NP = 2432
FF_COL = 2304
DFF = 2816
LN_EPS = 1e-5
DN_ALPHA = (2.0 * DEPTH) ** 0.25
NEG_INF = -1e30
ADAM_LR = 0.001
ADAM_B1 = 0.9
ADAM_B2 = 0.999
ADAM_EPS = 1e-08
ADAM_WD = 0.01
ADAM_STEP = 10

V7X_VMEM_LIMIT = 56 * 1024 * 1024
TS = 512
TS_C = 256
T_S5 = 256
TQ = 512
TQ_FWD = 1024
ATTN_HEADS = 2


def _call(body, **kw):
    return pl.pallas_call(body, **kw)


def _params(*sem):
    return pltpu.CompilerParams(dimension_semantics=sem if sem else None,
                                vmem_limit_bytes=V7X_VMEM_LIMIT)


def _nn(a, b):
    return jnp.dot(a.astype(BF16), b.astype(BF16), preferred_element_type=F32)


def _nt(a, b):
    return lax.dot_general(a.astype(BF16), b.astype(BF16), (((1,), (1,)), ((), ())),
                           preferred_element_type=F32)


def _tn(a, b):
    return lax.dot_general(a.astype(BF16), b.astype(BF16), (((0,), (0,)), ((), ())),
                           preferred_element_type=F32)


@jax.custom_vjp
def _bdot(a, b):
    return _nn(a, b)


def _bdot_fwd(a, b):
    return _nn(a, b), (a, b)


def _bdot_bwd(res, g):
    a, b = res
    return _nt(g, b), _tn(a, g)


_bdot.defvjp(_bdot_fwd, _bdot_bwd)


@jax.custom_vjp
def _bdot_nt(a, b):
    return _nt(a, b)


def _bdot_nt_fwd(a, b):
    return _nt(a, b), (a, b)


def _bdot_nt_bwd(res, g):
    a, b = res
    return _nn(g, b), _tn(g, a)


_bdot_nt.defvjp(_bdot_nt_fwd, _bdot_nt_bwd)


def _ln(r, g, b):
    mu = jnp.mean(r, axis=-1, keepdims=True)
    xc = r - mu
    var = jnp.mean(xc * xc, axis=-1, keepdims=True)
    return xc * lax.rsqrt(var + LN_EPS) * g + b


def _rows(shape):
    return lax.broadcasted_iota(jnp.int32, shape, 0)


def _lanes(shape):
    return lax.broadcasted_iota(jnp.int32, shape, 1)


def mm_nn(a, w, tn, name, mod=None, rows=None, out_dtype=F32, wt=False):
    s, k = a.shape
    n = w.shape[0] if wt else w.shape[1]
    ts = min(TS, s)

    def body(*refs):
        if mod is None:
            a_ref, w_ref, o_ref = refs
            h = a_ref[...]
        else:
            a_ref, m_ref, w_ref, o_ref = refs
            h = a_ref[...] * (1.0 + m_ref[rows[1]:rows[1] + 1, :]) + m_ref[rows[0]:rows[0] + 1, :]
        hb = h.astype(BF16)
        prod = _nt(hb, w_ref[...]) if wt else jnp.dot(hb, w_ref[...], preferred_element_type=F32)
        o_ref[...] = prod.astype(out_dtype)

    in_specs = [pl.BlockSpec((ts, k), lambda j, i: (i, 0))]
    args = [a]
    if mod is not None:
        in_specs.append(pl.BlockSpec((8, k), lambda j, i: (0, 0)))
        args.append(mod)
    in_specs.append(pl.BlockSpec((tn, k), lambda j, i: (j, 0)) if wt else pl.BlockSpec((k, tn), lambda j, i: (0, j)))
    args.append(w)
    return _call(body, name=name, grid=(n // tn, s // ts), in_specs=in_specs,
                 out_specs=pl.BlockSpec((ts, tn), lambda j, i: (i, j)),
                 out_shape=jax.ShapeDtypeStruct((s, n), out_dtype),
                 compiler_params=_params("arbitrary", "arbitrary"))(*args)


def mm_nt(dy, w, name):
    s, n = dy.shape
    k = w.shape[0]
    ts = min(TS, s)

    def body(dy_ref, w_ref, o_ref):
        o_ref[...] = _nt(dy_ref[...], w_ref[...])

    return _call(body, name=name, grid=(s // ts,),
                 in_specs=[pl.BlockSpec((ts, n), lambda i: (i, 0)),
                           pl.BlockSpec((k, n), lambda i: (0, 0))],
                 out_specs=pl.BlockSpec((ts, k), lambda i: (i, 0)),
                 out_shape=jax.ShapeDtypeStruct((s, k), F32),
                 compiler_params=_params("arbitrary"))(dy, w)


def mm_nt_mod(dy, w, x, dres, mod, rows, name):
    s, n = dy.shape
    k = w.shape[1]
    ts = min(TS, s)

    def body(dy_ref, w_ref, x_ref, r_ref, m_ref, dx_ref, dsh_ref, dsc_ref):
        @pl.when(pl.program_id(0) == 0)
        def _():
            dsh_ref[...] = jnp.zeros_like(dsh_ref)
            dsc_ref[...] = jnp.zeros_like(dsc_ref)

        dh = _nn(dy_ref[...], w_ref[...])
        dx_ref[...] = r_ref[...] + dh * (1.0 + m_ref[rows[1]:rows[1] + 1, :])
        dsh_ref[...] += jnp.sum(dh, axis=0, keepdims=True)
        dsc_ref[...] += jnp.sum(dh * x_ref[...], axis=0, keepdims=True)

    row = pl.BlockSpec((1, k), lambda i: (0, 0))
    tile = pl.BlockSpec((ts, k), lambda i: (i, 0))
    return _call(body, name=name, grid=(s // ts,),
                 in_specs=[pl.BlockSpec((ts, n), lambda i: (i, 0)),
                           pl.BlockSpec((n, k), lambda i: (0, 0)), tile, tile,
                           pl.BlockSpec((8, k), lambda i: (0, 0))],
                 out_specs=[tile, row, row],
                 out_shape=[jax.ShapeDtypeStruct((s, k), F32),
                            jax.ShapeDtypeStruct((1, k), F32),
                            jax.ShapeDtypeStruct((1, k), F32)],
                 compiler_params=_params("arbitrary"))(dy, w, x, dres, mod)


def mm_tn(a, dy, tn, name, mod=None, rows=None):
    s, k = a.shape
    n = dy.shape[1]
    ts = min(TS, s)
    ns = s // ts

    def body(*refs):
        if mod is None:
            a_ref, dy_ref, o_ref, acc = refs
            h = dy_ref[...]
        else:
            a_ref, dy_ref, m_ref, o_ref, acc = refs
            h = dy_ref[...] * (1.0 + m_ref[rows[1]:rows[1] + 1, :]) + m_ref[rows[0]:rows[0] + 1, :]
        i = pl.program_id(1)

        @pl.when(i == 0)
        def _():
            acc[...] = jnp.zeros_like(acc)

        acc[...] += _tn(a_ref[...], h)

        @pl.when(i == ns - 1)
        def _():
            o_ref[...] = acc[...].astype(BF16)

    in_specs = [pl.BlockSpec((ts, k), lambda j, i: (i, 0)), pl.BlockSpec((ts, tn), lambda j, i: (i, j))]
    args = [a, dy]
    if mod is not None:
        in_specs.append(pl.BlockSpec((8, tn), lambda j, i: (0, j)))
        args.append(mod)
    return _call(body, name=name, grid=(n // tn, s // ts), in_specs=in_specs,
                 out_specs=pl.BlockSpec((k, tn), lambda j, i: (0, j)),
                 out_shape=jax.ShapeDtypeStruct((k, n), BF16),
                 scratch_shapes=[pltpu.VMEM((k, tn), F32)],
                 compiler_params=_params("arbitrary", "arbitrary"))(*args)


def _post_fn(x, br, gate, lg, lb):
    return _ln(DN_ALPHA * x + (1.0 + gate) * br, lg, lb)


def post_fwd(x, br, mod, grow, lg, lb, name):
    s = x.shape[0]
    ts = min(TS, s)

    def body(x_ref, b_ref, m_ref, lg_ref, lb_ref, o_ref):
        o_ref[...] = _post_fn(x_ref[...], b_ref[...], m_ref[grow:grow + 1, :], lg_ref[...], lb_ref[...])

    tile = pl.BlockSpec((ts, D), lambda i: (i, 0))
    row = pl.BlockSpec((1, D), lambda i: (0, 0))
    return _call(body, name=name, grid=(s // ts,),
                 in_specs=[tile, tile, pl.BlockSpec((8, D), lambda i: (0, 0)), row, row],
                 out_specs=tile, out_shape=jax.ShapeDtypeStruct((s, D), F32),
                 compiler_params=_params("arbitrary"))(x, br, mod, lg, lb)


def post_bwd(x, br, mod, grow, lg, lb, dy, name):
    s = x.shape[0]
    ts = min(TS, s)

    def body(x_ref, b_ref, m_ref, lg_ref, lb_ref, dy_ref, dx_ref, db_ref, dg_ref, dlg_ref, dlb_ref):
        @pl.when(pl.program_id(0) == 0)
        def _():
            dg_ref[...] = jnp.zeros_like(dg_ref)
            dlg_ref[...] = jnp.zeros_like(dlg_ref)
            dlb_ref[...] = jnp.zeros_like(dlb_ref)

        _, vjp = jax.vjp(_post_fn, x_ref[...], b_ref[...], m_ref[grow:grow + 1, :], lg_ref[...], lb_ref[...])
        dx, db, dg, dlg, dlb = vjp(dy_ref[...])
        dx_ref[...] = dx
        db_ref[...] = db.astype(BF16)
        dg_ref[...] += dg
        dlg_ref[...] += dlg
        dlb_ref[...] += dlb

    tile = pl.BlockSpec((ts, D), lambda i: (i, 0))
    row = pl.BlockSpec((1, D), lambda i: (0, 0))
    rs = jax.ShapeDtypeStruct((1, D), F32)
    return _call(body, name=name, grid=(s // ts,),
                 in_specs=[tile, tile, pl.BlockSpec((8, D), lambda i: (0, 0)), row, row, tile],
                 out_specs=[tile, tile, row, row, row],
                 out_shape=[jax.ShapeDtypeStruct((s, D), F32), jax.ShapeDtypeStruct((s, D), BF16), rs, rs, rs],
                 compiler_params=_params("arbitrary"))(x, br, mod, lg, lb, dy)


def loss_kernel(y, target):
    s = y.shape[0]
    ts = min(TS, s)

    def body(y_ref, t_ref, l_ref, dy_ref):
        @pl.when(pl.program_id(0) == 0)
        def _():
            l_ref[...] = jnp.zeros_like(l_ref)

        err = y_ref[...] - t_ref[...]
        dy_ref[...] = err * (1.0 / D)
        per_tok = jnp.mean(err * err, axis=-1, keepdims=True)
        l_ref[...] += 0.5 * jnp.sum(per_tok)

    tile = pl.BlockSpec((ts, D), lambda i: (i, 0))
    return _call(body, name="loss", grid=(s // ts,), in_specs=[tile, tile],
                 out_specs=[pl.BlockSpec((8, 128), lambda i: (0, 0)), tile],
                 out_shape=[jax.ShapeDtypeStruct((8, 128), F32), jax.ShapeDtypeStruct((s, D), F32)],
                 compiler_params=_params("arbitrary"))(y, target)


def _gm_pair(u, v, lg, lb, w0, w1, bs0, bs1):
    t = u.shape[0]
    low = _lanes((t, 2 * HD)) < HD

    def head_mean(x):
        lo = jnp.sum(jnp.where(low, x, 0.0), axis=-1, keepdims=True)
        hi = jnp.sum(jnp.where(low, 0.0, x), axis=-1, keepdims=True)
        return jnp.where(low, lo, hi) * (1.0 / HD)

    xc = v - head_mean(v)
    vn = xc * lax.rsqrt(head_mean(xc * xc) + LN_EPS) * lg + lb
    v0 = jnp.where(low, vn, 0.0)
    v1 = jnp.where(low, 0.0, vn)
    causal = _rows((GM_C, GM_C)) >= _lanes((GM_C, GM_C))
    wm0 = jnp.where(causal, w0, 0.0)
    wm1 = jnp.where(causal, w1, 0.0)
    bias = jnp.where(_lanes((GM_C, 2 * HD)) < HD, bs0, bs1)
    chunks = []
    for n in range(t // GM_C):
        rs = slice(n * GM_C, (n + 1) * GM_C)
        chunks.append(u[rs] * (_bdot(wm0, v0[rs]) + _bdot(wm1, v1[rs]) + bias))
    return jnp.concatenate(chunks, axis=0)


def gm_fwd(p, lg, lb, ws, bst):
    s = p.shape[0]
    ts = min(TS_C, s)

    def body(u_ref, v_ref, lg_ref, lb_ref, ws_ref, bs_ref, o_ref):
        for j in range(GM_H // 2):
            sl = slice(j * 2 * HD, (j + 1) * 2 * HD)
            o_ref[:, sl] = _gm_pair(u_ref[:, sl], v_ref[:, sl], lg_ref[:, sl], lb_ref[:, sl], ws_ref[2 * j],
                                    ws_ref[2 * j + 1], bs_ref[:, 2 * j:2 * j + 1],
                                    bs_ref[:, 2 * j + 1:2 * j + 2]).astype(BF16)

    full = lambda shape: pl.BlockSpec(shape, lambda i: (0,) * len(shape))
    return _call(body, name="gm_fwd", grid=(s // ts,),
                 in_specs=[pl.BlockSpec((ts, GM_W), lambda i: (i, 0)), pl.BlockSpec((ts, GM_W), lambda i: (i, 1)),
                           full((1, GM_W)), full((1, GM_W)), full((GM_H, GM_C, GM_C)), full((GM_C, GM_H))],
                 out_specs=pl.BlockSpec((ts, GM_W), lambda i: (i, 0)),
                 out_shape=jax.ShapeDtypeStruct((s, GM_W), BF16),
                 compiler_params=_params("arbitrary"))(p, p, lg, lb, ws, bst)


def gm_bwd(p, lg, lb, ws, bst, dmix):
    s = p.shape[0]
    ts = min(TS_C, s)

    def body(u_ref, v_ref, lg_ref, lb_ref, ws_ref, bs_ref, dy_ref, duv_ref, dlg_ref, dlb_ref, dws_ref, dbs_ref):
        @pl.when(pl.program_id(0) == 0)
        def _():
            dlg_ref[...] = jnp.zeros_like(dlg_ref)
            dlb_ref[...] = jnp.zeros_like(dlb_ref)
            dws_ref[...] = jnp.zeros_like(dws_ref)
            dbs_ref[...] = jnp.zeros_like(dbs_ref)

        for j in range(GM_H // 2):
            sl = slice(j * 2 * HD, (j + 1) * 2 * HD)
            _, vjp = jax.vjp(_gm_pair, u_ref[:, sl], v_ref[:, sl], lg_ref[:, sl], lb_ref[:, sl], ws_ref[2 * j],
                             ws_ref[2 * j + 1], bs_ref[:, 2 * j:2 * j + 1], bs_ref[:, 2 * j + 1:2 * j + 2])
            du, dv, dlg, dlb, dw0, dw1, dbs0, dbs1 = vjp(dy_ref[:, sl])
            duv_ref[:, sl] = du.astype(BF16)
            duv_ref[:, GM_W + j * 2 * HD:GM_W + (j + 1) * 2 * HD] = dv.astype(BF16)
            dlg_ref[:, sl] += dlg
            dlb_ref[:, sl] += dlb
            dws_ref[2 * j] += dw0
            dws_ref[2 * j + 1] += dw1
            dbs_ref[:, 2 * j:2 * j + 1] += dbs0
            dbs_ref[:, 2 * j + 1:2 * j + 2] += dbs1

    full = lambda shape: pl.BlockSpec(shape, lambda i: (0,) * len(shape))
    return _call(body, name="gm_bwd", grid=(s // ts,),
                 in_specs=[pl.BlockSpec((ts, GM_W), lambda i: (i, 0)), pl.BlockSpec((ts, GM_W), lambda i: (i, 1)),
                           full((1, GM_W)), full((1, GM_W)), full((GM_H, GM_C, GM_C)), full((GM_C, GM_H)),
                           pl.BlockSpec((ts, GM_W), lambda i: (i, 0))],
                 out_specs=[pl.BlockSpec((ts, 2 * GM_W), lambda i: (i, 0)), full((1, GM_W)), full((1, GM_W)),
                            full((GM_H, GM_C, GM_C)), full((GM_C, GM_H))],
                 out_shape=[jax.ShapeDtypeStruct((s, 2 * GM_W), BF16), jax.ShapeDtypeStruct((1, GM_W), F32),
                            jax.ShapeDtypeStruct((1, GM_W), F32), jax.ShapeDtypeStruct((GM_H, GM_C, GM_C), F32),
                            jax.ShapeDtypeStruct((GM_C, GM_H), F32)],
                 compiler_params=_params("arbitrary"))(p, p, lg, lb, ws, bst, dmix)


def _s5_prep_fn(lr, li, ldt, bt):
    dt = jnp.exp(ldt)
    er = jnp.exp(lr * dt)
    ar = er * jnp.cos(li * dt)
    ai = er * jnp.sin(li * dt)
    den = lr * lr + li * li
    nr = ar - 1.0
    cr = (nr * lr + ai * li) / den
    ci = (ai * lr - nr * li) / den
    br, bi = bt[:, :S5_N], bt[:, S5_N:]
    return ar, ai, jnp.concatenate([cr * br - ci * bi, cr * bi + ci * br], axis=1)


def s5_prep_fwd(lr, li, ldt, bt):
    def body(lr_ref, li_ref, ldt_ref, bt_ref, a_ref, bb_ref):
        ar, ai, bb = _s5_prep_fn(lr_ref[...], li_ref[...], ldt_ref[...], bt_ref[...])
        a_ref[...] = jnp.concatenate([ar, ai, jnp.zeros((6, S5_N), F32)], axis=0)
        bb_ref[...] = bb

    return _call(body, name="s5_prep_fwd",
                 out_shape=[jax.ShapeDtypeStruct((8, S5_N), F32), jax.ShapeDtypeStruct((S5_H, 2 * S5_N), F32)])(lr, li, ldt, bt)


def s5_prep_bwd(lr, li, ldt, bt, da, dbb):
    def body(lr_ref, li_ref, ldt_ref, bt_ref, da_ref, dbb_ref, dlr_ref, dli_ref, dldt_ref, dbt_ref):
        _, vjp = jax.vjp(_s5_prep_fn, lr_ref[...], li_ref[...], ldt_ref[...], bt_ref[...])
        dlr, dli, dldt, dbt = vjp((da_ref[0:1, :], da_ref[1:2, :], dbb_ref[...]))
        dlr_ref[...] = dlr
        dli_ref[...] = dli
        dbt_ref[...] = dbt
        group = (_rows((S5_N, 128)) // S5_P == _lanes((S5_N, 128))).astype(F32)
        dldt_ref[...] = jnp.dot(jnp.broadcast_to(dldt, (8, S5_N)), group, precision=lax.Precision.HIGHEST,
                                preferred_element_type=F32)[0:1, :]

    r = jax.ShapeDtypeStruct((1, S5_N), F32)
    return _call(body, name="s5_prep_bwd",
                 out_shape=[r, r, jax.ShapeDtypeStruct((1, 128), F32),
                            jax.ShapeDtypeStruct((S5_H, 2 * S5_N), F32)])(lr, li, ldt, bt, da, dbb)


def _s5_out_fn(x, u, cbd, drow, wg, bg):
    y = _bdot_nt(x[:, :S5_N], cbd[:, :S5_N]) - _bdot_nt(x[:, S5_N:], cbd[:, S5_N:]) + drow * u
    y = jax.nn.gelu(y)
    gate = _bdot_nt(y, wg) + bg
    return y * jax.nn.sigmoid(gate)


def _scan_chunk(buf, ar, ai, cr, ci, reverse):
    t = buf.shape[0]

    def local(xr, xi, rows):
        within = _rows(xr.shape) % 8
        pr, pi = ar, ai
        for d in (1, 2, 4):
            keep = within < 8 - d if reverse else within >= d
            shift = rows - d if reverse else d
            sr = jnp.where(keep, pltpu.roll(xr, shift, 0), 0.0)
            si = jnp.where(keep, pltpu.roll(xi, shift, 0), 0.0)
            xr, xi = xr + pr * sr - pi * si, xi + pr * si + pi * sr
            pr, pi = pr * pr - pi * pi, 2.0 * pr * pi
        return xr, xi

    xr, xi = local(buf[:, :S5_N], buf[:, S5_N:], t)
    buf[:, :S5_N] = xr
    buf[:, S5_N:] = xi
    edge = _rows((8, S5_N)) == (7 if reverse else 0)
    pr8, pi8 = local(jnp.where(edge, ar, 0.0), jnp.where(edge, ai, 0.0), 8)

    def group(j, c):
        g = t // 8 - 1 - j if reverse else j
        rows = pl.ds(pl.multiple_of(g * 8, 8), 8)
        gr = buf[rows, :S5_N] + pr8 * c[0] - pi8 * c[1]
        gi = buf[rows, S5_N:] + pr8 * c[1] + pi8 * c[0]
        buf[rows, :S5_N] = gr
        buf[rows, S5_N:] = gi
        return (gr[0:1, :], gi[0:1, :]) if reverse else (gr[7:8, :], gi[7:8, :])

    return lax.fori_loop(0, t // 8, group, (cr, ci), unroll=4)


def s5_fwd(p, arow, bbd, cbd, drow, wg, bg):
    s = p.shape[0]
    t = min(T_S5, s)

    def body(u_ref, a_ref, bbd_ref, cbd_ref, d_ref, wg_ref, bg_ref, y_ref, st_ref, carry):
        @pl.when(pl.program_id(0) == 0)
        def _():
            carry[...] = jnp.zeros_like(carry)

        u = u_ref[...]
        st_ref[...] = _nn(u, bbd_ref[...])
        cr, ci = _scan_chunk(st_ref, a_ref[0:1, :], a_ref[1:2, :], carry[0:1, :S5_N], carry[0:1, S5_N:], False)
        carry[0:1, :S5_N] = cr
        carry[0:1, S5_N:] = ci
        y_ref[...] = _s5_out_fn(st_ref[...], u, cbd_ref[...], d_ref[...], wg_ref[...], bg_ref[...]).astype(BF16)

    full = lambda shape: pl.BlockSpec(shape, lambda i: (0,) * len(shape))
    return _call(body, name="s5_fwd", grid=(s // t,),
                 in_specs=[pl.BlockSpec((t, S5_W), lambda i: (i, 2)), full((8, S5_N)), full((S5_W, 2 * S5_N)),
                           full((S5_W, 2 * S5_N)), full((1, S5_W)), full((S5_W, S5_W)), full((1, S5_W))],
                 out_specs=[pl.BlockSpec((t, S5_W), lambda i: (i, 0)), pl.BlockSpec((t, 2 * S5_N), lambda i: (i, 0))],
                 out_shape=[jax.ShapeDtypeStruct((s, S5_W), BF16), jax.ShapeDtypeStruct((s, 2 * S5_N), F32)],
                 scratch_shapes=[pltpu.VMEM((8, 2 * S5_N), F32)],
                 compiler_params=_params("arbitrary"))(p, arow, bbd, cbd, drow, wg, bg)


def s5_bwd(p, st, arow, bbd, cbd, drow, wg, bg, dmix):
    s = p.shape[0]
    t = min(T_S5, s)
    nc = s // t

    def body(u_ref, st_ref, prev_ref, a_ref, bbd_ref, cbd_ref, d_ref, wg_ref, bg_ref, dy_ref,
             du_ref, da_ref, dbbd_ref, dcbd_ref, dd_ref, dwg_ref, dbg_ref, carry, gbuf):
        i = pl.program_id(0)

        @pl.when(i == 0)
        def _():
            carry[...] = jnp.zeros_like(carry)
            for r in (da_ref, dbbd_ref, dcbd_ref, dd_ref, dwg_ref, dbg_ref):
                r[...] = jnp.zeros_like(r)

        u = u_ref[...]
        x = st_ref[...]
        _, vjp = jax.vjp(_s5_out_fn, x, u, cbd_ref[...], d_ref[...], wg_ref[...], bg_ref[...])
        dx, du1, dcbd, dd, dwg, dbg = vjp(dy_ref[...])
        gbuf[...] = dx
        cr, ci = _scan_chunk(gbuf, a_ref[0:1, :], -a_ref[1:2, :], carry[0:1, :S5_N], carry[0:1, S5_N:], True)
        carry[0:1, :S5_N] = cr
        carry[0:1, S5_N:] = ci
        gr, gi = gbuf[:, :S5_N], gbuf[:, S5_N:]
        rid = _rows((t, S5_N))
        has_prev = (i < nc - 1).astype(F32)
        top_r = prev_ref[7:8, :S5_N] * has_prev
        top_i = prev_ref[7:8, S5_N:] * has_prev
        xpr = jnp.where(rid == 0, top_r, pltpu.roll(x[:, :S5_N], 1, 0))
        xpi = jnp.where(rid == 0, top_i, pltpu.roll(x[:, S5_N:], 1, 0))
        da_ref[0:1, :] += jnp.sum(xpr * gr + xpi * gi, axis=0, keepdims=True)
        da_ref[1:2, :] += jnp.sum(xpr * gi - xpi * gr, axis=0, keepdims=True)
        g = jnp.concatenate([gr, gi], axis=1)
        dbbd_ref[...] += _tn(u, g)
        du_ref[...] = (_nt(g, bbd_ref[...]) + du1).astype(BF16)
        dcbd_ref[...] += dcbd
        dd_ref[...] += dd
        dwg_ref[...] += dwg
        dbg_ref[...] += dbg

    full = lambda shape: pl.BlockSpec(shape, lambda i: (0,) * len(shape))
    rev = lambda col: (lambda i: (nc - 1 - i, col))
    prev_map = lambda i: (jnp.maximum((nc - 1 - i) * (t // 8) - 1, 0), 0)
    return _call(body, name="s5_bwd", grid=(nc,),
                 in_specs=[pl.BlockSpec((t, S5_W), rev(2)), pl.BlockSpec((t, 2 * S5_N), rev(0)),
                           pl.BlockSpec((8, 2 * S5_N), prev_map), full((8, S5_N)), full((S5_W, 2 * S5_N)),
                           full((S5_W, 2 * S5_N)), full((1, S5_W)), full((S5_W, S5_W)), full((1, S5_W)),
                           pl.BlockSpec((t, S5_W), rev(1))],
                 out_specs=[pl.BlockSpec((t, S5_W), rev(0)), full((8, S5_N)), full((S5_W, 2 * S5_N)),
                            full((S5_W, 2 * S5_N)), full((1, S5_W)), full((S5_W, S5_W)), full((1, S5_W))],
                 out_shape=[jax.ShapeDtypeStruct((s, S5_W), BF16), jax.ShapeDtypeStruct((8, S5_N), F32),
                            jax.ShapeDtypeStruct((S5_W, 2 * S5_N), F32), jax.ShapeDtypeStruct((S5_W, 2 * S5_N), F32),
                            jax.ShapeDtypeStruct((1, S5_W), F32), jax.ShapeDtypeStruct((S5_W, S5_W), F32),
                            jax.ShapeDtypeStruct((1, S5_W), F32)],
                 scratch_shapes=[pltpu.VMEM((8, 2 * S5_N), F32), pltpu.VMEM((t, 2 * S5_N), F32)],
                 compiler_params=_params("arbitrary"))(p, st, st, arow, bbd, cbd, drow, wg, bg, dmix)


def _cum_steps(s):
    return int(math.ceil(math.log2(s)))


V_BLK = (2 * GM_W + S5_W + 2 * FX_W) // 128


AUG = 2 * HD
BIAS_COL = HD
FQ_COL = HD + 3
PAIR_W = 256


def _split3(f):
    hi = f.astype(BF16).astype(F32)
    r = f - hi
    mid = r.astype(BF16).astype(F32)
    lo = (r - mid).astype(BF16).astype(F32)
    return hi, mid, lo


def fox_prep(p, bf):
    s = p.shape[0]
    ts = min(TS, s)
    scale = HD ** -0.5

    def body(q0_ref, q1_ref, k0_ref, k1_ref, v0_ref, v1_ref, f_ref, bf_ref,
             qa_ref, ka_ref, qat_ref, kat_ref, vt_ref, carry):
        @pl.when(pl.program_id(0) == 0)
        def _():
            carry[...] = jnp.zeros_like(carry)

        lane = _lanes((ts, 128))
        lf = jax.nn.log_sigmoid(f_ref[...] + bf_ref[...])
        acc = jnp.where(lane < FX_H, lf, 0.0)
        rid = _rows((ts, 128))
        for k in range(_cum_steps(ts)):
            d = 1 << k
            acc = acc + jnp.where(rid >= d, pltpu.roll(acc, d, 0), 0.0)
        acc = acc + carry[0:1, :]
        carry[0:1, :] = acc[ts - 1:ts, :]

        low = lane < HD
        for h in range(FX_H):
            blk, pos = divmod(h, 4)
            pair = slice((pos // 2) * 128, (pos // 2) * 128 + 128)
            hi, mid, lo = _split3(acc[:, h:h + 1])
            one = jnp.ones((ts, 1), F32)

            def augment(ref, cols):
                x = ref[:, pair]
                if pos % 2:
                    x = pltpu.roll(x, HD, 1)
                out = jnp.where(low, x, 0.0)
                for j, cval in enumerate(cols):
                    out = jnp.where(lane == HD + j, cval, out)
                return out

            qa = augment((q0_ref, q1_ref)[blk], (one, one, one, hi, mid, lo))
            qa = jnp.where(low, qa * scale, qa)
            ka = augment((k0_ref, k1_ref)[blk], (-hi, -mid, -lo, one, one, one))
            cs = slice(h * AUG, (h + 1) * AUG)
            qa_ref[:, cs] = qa.astype(BF16)
            ka_ref[:, cs] = ka.astype(BF16)
            qat_ref[cs, :] = jnp.transpose(qa).astype(BF16)
            kat_ref[cs, :] = jnp.transpose(ka).astype(BF16)
        for j in range(FX_H // 2):
            vref = (v0_ref, v1_ref)[j // 2]
            vt_ref[j * 128:(j + 1) * 128, :] = jnp.transpose(vref[:, (j % 2) * 128:(j % 2) * 128 + 128]).astype(BF16)

    q_blk = (2 * GM_W + S5_W) // PAIR_W
    col = lambda b: pl.BlockSpec((ts, PAIR_W), lambda i: (i, b))
    wide = FX_H * AUG
    return _call(body, name="fox_prep", grid=(s // ts,),
                 in_specs=[col(q_blk), col(q_blk + 1), col(q_blk + 2), col(q_blk + 3), col(q_blk + 4), col(q_blk + 5),
                           pl.BlockSpec((ts, 128), lambda i: (i, FF_COL // 128)), pl.BlockSpec((1, 128), lambda i: (0, 0))],
                 out_specs=[pl.BlockSpec((ts, wide), lambda i: (i, 0)), pl.BlockSpec((ts, wide), lambda i: (i, 0)),
                            pl.BlockSpec((wide, ts), lambda i: (0, i)), pl.BlockSpec((wide, ts), lambda i: (0, i)),
                            pl.BlockSpec((FX_W, ts), lambda i: (0, i))],
                 out_shape=[jax.ShapeDtypeStruct((s, wide), BF16), jax.ShapeDtypeStruct((s, wide), BF16),
                            jax.ShapeDtypeStruct((wide, s), BF16), jax.ShapeDtypeStruct((wide, s), BF16),
                            jax.ShapeDtypeStruct((FX_W, s), BF16)],
                 scratch_shapes=[pltpu.VMEM((8, 128), F32)],
                 compiler_params=_params("arbitrary"))(p, p, p, p, p, p, p, bf)


def fox_prep_grad(p, bf, dfq, dfk):
    s = p.shape[0]
    ts = min(TS, s)
    ns = s // ts

    def body(f_ref, bf_ref, dfq_ref, dfk_ref, df_ref, dbf_ref, carry):
        @pl.when(pl.program_id(0) == 0)
        def _():
            carry[...] = jnp.zeros_like(carry)
            dbf_ref[...] = jnp.zeros_like(dbf_ref)

        lane = _lanes((ts, 128))
        acc = jnp.zeros((ts, 128), F32)
        for h in range(FX_H):
            c = (h // 2) * 128 + h % 2
            acc = jnp.where(lane == h, dfq_ref[:, c:c + 1] + dfk_ref[:, c:c + 1], acc)
        rid = _rows((ts, 128))
        for k in range(_cum_steps(ts)):
            d = 1 << k
            acc = acc + jnp.where(rid < ts - d, pltpu.roll(acc, ts - d, 0), 0.0)
        acc = acc + carry[0:1, :]
        carry[0:1, :] = acc[0:1, :]
        z = f_ref[...] + bf_ref[...]
        df = jnp.where(lane < FX_H, acc * jax.nn.sigmoid(-z), 0.0)
        df_ref[...] = df.astype(BF16)
        dbf_ref[...] += jnp.sum(df, axis=0, keepdims=True)

    rev = lambda i: (ns - 1 - i, 0)
    return _call(body, name="fox_prep_grad", grid=(ns,),
                 in_specs=[pl.BlockSpec((ts, 128), lambda i: (ns - 1 - i, FF_COL // 128)),
                           pl.BlockSpec((1, 128), lambda i: (0, 0)),
                           pl.BlockSpec((ts, FX_W), rev), pl.BlockSpec((ts, FX_W), rev)],
                 out_specs=[pl.BlockSpec((ts, 128), rev), pl.BlockSpec((1, 128), lambda i: (0, 0))],
                 out_shape=[jax.ShapeDtypeStruct((s, 128), BF16), jax.ShapeDtypeStruct((1, 128), F32)],
                 scratch_shapes=[pltpu.VMEM((8, 128), F32)],
                 compiler_params=_params("arbitrary"))(p, bf, dfq, dfk)


def attn(qat, ka, vt):
    s = ka.shape[0]
    tq = min(TQ_FWD, s)
    nq = s // tq

    nh = ATTN_HEADS

    def body(qat_ref, ka_ref, vt_ref, o_ref, lse_ref):
        qi = pl.program_id(1)
        causal = _rows((tq, tq)) <= _lanes((tq, tq))
        lse_ref[...] = jnp.zeros_like(lse_ref)

        def step(kj, carry, masked):
            off = pl.multiple_of(kj * tq, tq)
            out = []
            for hh in range(nh):
                m, l, acc = carry[hh]
                st = jnp.dot(ka_ref[pl.ds(off, tq), hh * AUG:(hh + 1) * AUG], qat_ref[hh * AUG:(hh + 1) * AUG, :],
                             preferred_element_type=F32)
                if masked:
                    st = jnp.where(causal, st, NEG_INF)
                m_new = jnp.maximum(m, jnp.max(st, axis=0, keepdims=True))
                alpha = jnp.exp(m - m_new)
                pt = jnp.exp(st - m_new)
                v = vt_ref[hh * HD:(hh + 1) * HD, pl.ds(off, tq)]
                out.append((m_new, alpha * l + jnp.sum(pt, axis=0, keepdims=True),
                            alpha * acc + jnp.dot(v, pt.astype(BF16), preferred_element_type=F32)))
            return tuple(out)

        init = tuple((jnp.full((1, tq), NEG_INF, F32), jnp.zeros((1, tq), F32), jnp.zeros((HD, tq), F32))
                     for _ in range(nh))
        carry = lax.fori_loop(0, qi, lambda kj, c: step(kj, c, False), init)
        carry = step(qi, carry, True)
        for hh in range(nh):
            m, l, _ = carry[hh]
            lse_ref[hh // 2, hh % 2:hh % 2 + 1, :] = m + jnp.log(l)
        for j in range(nh // 2):
            pair = jnp.concatenate([carry[2 * j][2] / carry[2 * j][1], carry[2 * j + 1][2] / carry[2 * j + 1][1]], axis=0)
            o_ref[:, j * 128:(j + 1) * 128] = jnp.transpose(pair).astype(BF16)

    return _call(body, name="attn", grid=(FX_H // nh, nq),
                 in_specs=[pl.BlockSpec((nh * AUG, tq), lambda h, i: (h, i)),
                           pl.BlockSpec((s, nh * AUG), lambda h, i: (0, h)),
                           pl.BlockSpec((nh * HD, s), lambda h, i: (h, 0))],
                 out_specs=[pl.BlockSpec((tq, nh * HD), lambda h, i: (i, h)),
                            pl.BlockSpec((nh // 2, 8, tq), lambda h, i: (h, 0, i))],
                 out_shape=[jax.ShapeDtypeStruct((s, FX_W), BF16), jax.ShapeDtypeStruct((FX_H // 2, 8, s), F32)],
                 compiler_params=_params("arbitrary", "arbitrary"))(qat, ka, vt)


def attn_grad(qa, qat, ka, kat, p, o, lse, dmix):
    s = qa.shape[0]
    tq = min(TQ, s)
    nq = s // tq
    scale = HD ** -0.5

    def body(qa_ref, qat_ref, ka_ref, kat_ref, v_ref, o_ref, lse_ref, do_ref,
             dq_ref, dk_ref, dv_ref, dfq_ref, dfk_ref, dot_scr, delta, dqt):
        kj = pl.program_id(1)
        lane = _lanes((tq, 128))
        low = lane < HD
        causal = _rows((tq, tq)) <= _lanes((tq, tq))

        @pl.when(kj == 0)
        def _():
            dqt[...] = jnp.zeros_like(dqt)
            delta[...] = jnp.zeros_like(delta)

            def prep(c, _):
                rows = pl.ds(pl.multiple_of(c * tq, tq), tq)
                do = do_ref[rows, :]
                pt = jnp.transpose(do * o_ref[rows, :].astype(F32))
                delta[0:1, rows] = jnp.sum(pt[:HD], axis=0, keepdims=True)
                delta[1:2, rows] = jnp.sum(pt[HD:], axis=0, keepdims=True)
                dot_scr[:, rows] = jnp.transpose(do).astype(BF16)
                return 0

            lax.fori_loop(0, nq, prep, 0)

        v = v_ref[...]
        vms = [jnp.where(low, v, 0.0).astype(BF16), jnp.where(low, 0.0, v).astype(BF16)]

        def tile(qi, carry, masked):
            cols = pl.ds(pl.multiple_of(qi * tq, tq), tq)
            do = do_ref[cols, :].astype(BF16)
            out = []
            for hh in range(2):
                cs = slice(hh * AUG, (hh + 1) * AUG)
                dka, dv = carry[hh]
                st = jnp.dot(ka_ref[:, cs], qat_ref[cs, cols], preferred_element_type=F32)
                if masked:
                    st = jnp.where(causal, st, NEG_INF)
                pt = jnp.exp(st - lse_ref[0, hh:hh + 1, cols])
                dv = dv + jnp.dot(pt.astype(BF16), do, preferred_element_type=F32)
                dpt = jnp.dot(vms[hh], dot_scr[:, cols], preferred_element_type=F32)
                dsb = (pt * (dpt - delta[hh:hh + 1, cols])).astype(BF16)
                dka = dka + jnp.dot(dsb, qa_ref[cols, cs], preferred_element_type=F32)
                dqt[hh, :, cols] += jnp.dot(kat_ref[cs, :], dsb, preferred_element_type=F32)
                out.append((dka, dv))
            return tuple(out)

        init = tuple((jnp.zeros((tq, AUG), F32), jnp.zeros((tq, 128), F32)) for _ in range(2))
        carry = tile(kj, init, True)
        carry = lax.fori_loop(kj + 1, nq, lambda qi, c: tile(qi, c, False), carry)
        dks = [carry[0][0], carry[1][0]]
        dvs = [carry[0][1], carry[1][1]]
        dv_ref[...] = jnp.where(low, dvs[0], dvs[1]).astype(BF16)
        dk_ref[...] = jnp.where(low, dks[0], pltpu.roll(dks[1], HD, 1)).astype(BF16)
        dfk_ref[...] = jnp.where(lane == 0, -dks[0][:, BIAS_COL:BIAS_COL + 1],
                                 jnp.where(lane == 1, -dks[1][:, BIAS_COL:BIAS_COL + 1], 0.0))

        @pl.when(kj == nq - 1)
        def _():
            def finish(c, _):
                rows = pl.ds(pl.multiple_of(c * tq, tq), tq)
                t0 = jnp.transpose(dqt[0, :, rows])
                t1 = jnp.transpose(dqt[1, :, rows])
                dq_ref[rows, :] = (jnp.where(low, t0, pltpu.roll(t1, HD, 1)) * scale).astype(BF16)
                dfq_ref[rows, :] = jnp.where(lane == 0, t0[:, FQ_COL:FQ_COL + 1],
                                             jnp.where(lane == 1, t1[:, FQ_COL:FQ_COL + 1], 0.0))
                return 0

            lax.fori_loop(0, nq, finish, 0)

    seq128 = lambda blk: pl.BlockSpec((s, 128), lambda h, j: (0, blk + h))
    tile128 = pl.BlockSpec((tq, 128), lambda h, j: (j, h))
    out_b = jax.ShapeDtypeStruct((s, FX_W), BF16)
    out_f = jax.ShapeDtypeStruct((s, FX_W), F32)
    return _call(body, name="attn_grad", grid=(FX_H // 2, nq),
                 in_specs=[pl.BlockSpec((s, 2 * AUG), lambda h, j: (0, h)), pl.BlockSpec((2 * AUG, s), lambda h, j: (h, 0)),
                           pl.BlockSpec((tq, 2 * AUG), lambda h, j: (j, h)), pl.BlockSpec((2 * AUG, tq), lambda h, j: (h, j)),
                           pl.BlockSpec((tq, 128), lambda h, j: (j, V_BLK + h)), seq128(0),
                           pl.BlockSpec((1, 8, s), lambda h, j: (h, 0, 0)), seq128(4)],
                 out_specs=[seq128(0), tile128, tile128, seq128(0), tile128],
                 out_shape=[out_b, out_b, out_b, out_f, out_f],
                 scratch_shapes=[pltpu.VMEM((128, s), BF16), pltpu.VMEM((8, s), F32), pltpu.VMEM((2, AUG, s), F32)],
                 compiler_params=_params("arbitrary", "arbitrary"))(qa, qat, ka, kat, p, o, lse, dmix)


def _shift_down(a, prev8, k):
    r = pltpu.roll(a, k, 0)
    top = jnp.where(_rows(prev8.shape) < k, pltpu.roll(prev8, k, 0), r[0:8])
    return jnp.concatenate([top, r[8:]], axis=0)


def _shift_up(a, next8, k):
    t = a.shape[0]
    r = pltpu.roll(a, t - k, 0)
    bot = jnp.where(_rows(next8.shape) >= 8 - k, pltpu.roll(next8, 8 - k, 0), r[t - 8:t])
    return jnp.concatenate([r[:t - 8], bot], axis=0)


def _conv(a, prev8, cw, cb):
    return cb + cw[0:1, :] * _shift_down(a, prev8, 2) + cw[1:2, :] * _shift_down(a, prev8, 1) + cw[2:3, :] * a


GELU_K0 = math.sqrt(2.0 / math.pi)
GELU_K1 = GELU_K0 * 0.044715


def _gelu_parts(c):
    c2 = c * c
    return c2, 0.5 + 0.5 * jnp.tanh(c * (GELU_K0 + GELU_K1 * c2))


def conv_fwd(up, cw, cb):
    s = up.shape[0]
    ts = min(TS_C, s)

    def body(a_ref, g_ref, cw_ref, cb_ref, o_ref, c_ref, halo):
        @pl.when(pl.program_id(0) == 0)
        def _():
            halo[...] = jnp.zeros_like(halo)

        a = a_ref[...]
        c = _conv(a, halo[...], cw_ref[...], cb_ref[...])
        _, h = _gelu_parts(c)
        c_ref[...] = c
        o_ref[...] = (c * h * g_ref[...]).astype(BF16)
        halo[...] = a[ts - 8:ts, :]

    tile = pl.BlockSpec((ts, DFF), lambda i: (i, 0))
    return _call(body, name="conv_fwd", grid=(s // ts,),
                 in_specs=[tile, pl.BlockSpec((ts, DFF), lambda i: (i, 1)),
                           pl.BlockSpec((3, DFF), lambda i: (0, 0)), pl.BlockSpec((1, DFF), lambda i: (0, 0))],
                 out_specs=[tile, tile],
                 out_shape=[jax.ShapeDtypeStruct((s, DFF), BF16), jax.ShapeDtypeStruct((s, DFF), F32)],
                 scratch_shapes=[pltpu.VMEM((8, DFF), F32)],
                 compiler_params=_params("arbitrary"))(up, up, cw, cb)


def conv_bwd(up, c, cw, dact):
    s = up.shape[0]
    ts = min(TS_C, s)
    ns = s // ts

    def body(a_ref, g_ref, c_ref, cw_ref, dact_ref, dup_ref, dcw_ref, dcb_ref, halo):
        @pl.when(pl.program_id(0) == 0)
        def _():
            halo[...] = jnp.zeros_like(halo)
            dcw_ref[...] = jnp.zeros_like(dcw_ref)
            dcb_ref[...] = jnp.zeros_like(dcb_ref)

        a = a_ref[...]
        cw = cw_ref[...]
        cv = c_ref[...]
        dact = dact_ref[...]
        c2, h = _gelu_parts(cv)
        dup_ref[:, DFF:] = (dact * (cv * h)).astype(BF16)
        dgel = h + cv * (2.0 * h * (1.0 - h)) * (GELU_K0 + 3.0 * GELU_K1 * c2)
        dc = dact * g_ref[...] * dgel
        up1 = _shift_up(dc, halo[...], 1)
        up2 = _shift_up(dc, halo[...], 2)
        dup_ref[:, :DFF] = (cw[2:3, :] * dc + cw[1:2, :] * up1 + cw[0:1, :] * up2).astype(BF16)
        dcw_ref[0:1, :] += jnp.sum(a * up2, axis=0, keepdims=True)
        dcw_ref[1:2, :] += jnp.sum(a * up1, axis=0, keepdims=True)
        dcw_ref[2:3, :] += jnp.sum(a * dc, axis=0, keepdims=True)
        dcb_ref[...] += jnp.sum(dc, axis=0, keepdims=True)
        halo[...] = dc[0:8, :]

    rev = lambda col: (lambda i: (ns - 1 - i, col))
    return _call(body, name="conv_bwd", grid=(ns,),
                 in_specs=[pl.BlockSpec((ts, DFF), rev(0)), pl.BlockSpec((ts, DFF), rev(1)),
                           pl.BlockSpec((ts, DFF), rev(0)), pl.BlockSpec((3, DFF), lambda i: (0, 0)),
                           pl.BlockSpec((ts, DFF), rev(0))],
                 out_specs=[pl.BlockSpec((ts, 2 * DFF), rev(0)), pl.BlockSpec((3, DFF), lambda i: (0, 0)),
                            pl.BlockSpec((1, DFF), lambda i: (0, 0))],
                 out_shape=[jax.ShapeDtypeStruct((s, 2 * DFF), BF16), jax.ShapeDtypeStruct((3, DFF), F32),
                            jax.ShapeDtypeStruct((1, DFF), F32)],
                 scratch_shapes=[pltpu.VMEM((8, DFF), F32)],
                 compiler_params=_params("arbitrary"))(up, up, c, cw, dact)


def _blockdiag_expand(m):
    m4 = m.reshape(S5_H, 2, S5_G, S5_P)
    eye = jnp.eye(S5_G, dtype=bool)[:, None, None, :, None]
    return jnp.where(eye, m4[None], 0.0).reshape(S5_W, 2 * S5_N)


def _blockdiag_extract(mbd):
    m5 = mbd.reshape(S5_G, S5_H, 2, S5_G, S5_P)
    diag = jnp.stack([m5[g, :, :, g, :] for g in range(S5_G)], axis=2)
    return diag.reshape(S5_H, 2 * S5_N)


def _c_expand(c_re, c_im):
    c4 = jnp.stack([c_re, c_im], axis=2)
    eye = jnp.eye(S5_G, dtype=bool)[:, None, None, :, None]
    return jnp.where(eye, c4[:, :, :, None, :], 0.0).reshape(S5_W, 2 * S5_N)


def _c_extract(cbd):
    m5 = cbd.reshape(S5_G, S5_H, 2, S5_G, S5_P)
    d = jnp.stack([m5[g, :, :, g, :] for g in range(S5_G)], axis=0)
    return d[:, :, 0, :], d[:, :, 1, :]


def _glu_expand(w):
    eye = jnp.eye(S5_G, dtype=bool)[:, None, :, None]
    return jnp.where(eye, w[:, :, None, :], 0.0).reshape(S5_W, S5_W)


def _glu_extract(wbd):
    m4 = wbd.reshape(S5_G, S5_H, S5_G, S5_H)
    return jnp.stack([m4[g, :, g, :] for g in range(S5_G)], axis=0)


SMALL = ("b_f", "gm_ln_g", "gm_ln_b", "gm_w_s", "gm_b_s", "s5_lam_re", "s5_lam_im", "s5_log_dt", "s5_b_re", "s5_b_im",
         "s5_c_re", "s5_c_im", "s5_d", "s5_w_glu", "s5_b_glu", "ln1_g", "ln1_b", "conv_b", "ln2_g", "ln2_b")


def _layer_operands(sp, l):
    f = {}
    f["bf"] = jnp.pad(sp["b_f"][l][None, :], ((0, 0), (0, 128 - FX_H)))
    f["gm_lg"] = sp["gm_ln_g"][l].reshape(1, GM_W)
    f["gm_lb"] = sp["gm_ln_b"][l].reshape(1, GM_W)
    f["gm_ws"] = sp["gm_w_s"][l]
    f["gm_bst"] = sp["gm_b_s"][l].T
    f["lr"] = sp["s5_lam_re"][l].reshape(1, S5_N)
    f["li"] = sp["s5_lam_im"][l].reshape(1, S5_N)
    f["ldt"] = jnp.repeat(sp["s5_log_dt"][l], S5_P).reshape(1, S5_N)
    bt = lambda b: jnp.transpose(b, (2, 0, 1)).reshape(S5_H, S5_N)
    f["bt"] = jnp.concatenate([bt(sp["s5_b_re"][l]), bt(sp["s5_b_im"][l])], axis=1)
    f["cbd"] = _c_expand(sp["s5_c_re"][l], sp["s5_c_im"][l])
    f["drow"] = sp["s5_d"][l].reshape(1, S5_W)
    f["wg"] = _glu_expand(sp["s5_w_glu"][l])
    f["bg"] = sp["s5_b_glu"][l].reshape(1, S5_W)
    for n in ("ln1_g", "ln1_b", "ln2_g", "ln2_b"):
        f[n] = sp[n][l][None, :]
    f["cb"] = sp["conv_b"][l][None, :]
    return f


def layer_fwd_mix(x, mod, w_in, f):
    p = mm_nn(x, w_in, NP, "in_proj", mod=mod, rows=(0, 1), wt=True)
    ygm = gm_fwd(p, f["gm_lg"], f["gm_lb"], f["gm_ws"], f["gm_bst"])
    arow, bbt = s5_prep_fwd(f["lr"], f["li"], f["ldt"], f["bt"])
    bbd = _blockdiag_expand(bbt)
    ys5, st = s5_fwd(p, arow, bbd, f["cbd"], f["drow"], f["wg"], f["bg"])
    qa, ka, qat, kat, vt = fox_prep(p, f["bf"])
    yfx, lse = attn(qat, ka, vt)
    mixcat = jnp.concatenate([ygm, ys5, yfx], axis=1)
    return mixcat, dict(f=f, x=x, p=p, arow=arow, bbd=bbd, st=st, qa=qa, ka=ka, qat=qat, kat=kat, yfx=yfx, lse=lse,
                        mixcat=mixcat)


def layer_fwd_rest(x, mixcat, mod, w, f, saved):
    mix = mm_nn(mixcat, w["w_out"], D, "out_proj")
    x1 = post_fwd(x, mix, mod, 2, f["ln1_g"], f["ln1_b"], "post1_fwd")
    up = mm_nn(x1, w["w_up"], DFF // 2, "up_proj", mod=mod, rows=(3, 4), wt=True)
    act, conv = conv_fwd(up, w["conv_w"], f["cb"])
    ffn = mm_nn(act, w["w_down"], D, "down_proj")
    x2 = post_fwd(x1, ffn, mod, 5, f["ln2_g"], f["ln2_b"], "post2_fwd")
    return x2, dict(saved, mix=mix, x1=x1, up=up, conv=conv, act=act, ffn=ffn)


def layer_fwd(x, mod, w, f):
    mixcat, saved = layer_fwd_mix(x, mod, w["w_in"], f)
    return layer_fwd_rest(x, mixcat, mod, w, f, saved)


def layer_bwd_ffn(dx, sv, mod, w):
    f = sv["f"]
    dx1, dffn, dg2, dlg2, dlb2 = post_bwd(sv["x1"], sv["ffn"], mod, 5, f["ln2_g"], f["ln2_b"], dx, "post2_bwd")
    g_down = mm_tn(sv["act"], dffn, D // 2, "down_dw")
    dact = mm_nt(dffn, w["w_down"], "down_dx")
    dup, dcw, dcb = conv_bwd(sv["up"], sv["conv"], w["conv_w"], dact)
    g_up = mm_tn(dup, sv["x1"], D // 2, "up_dw", mod=mod, rows=(3, 4))
    dx1, dsh2, dsc2 = mm_nt_mod(dup, w["w_up"], sv["x1"], dx1, mod, (3, 4), "up_dx")
    return dx1, dict(w_up=g_up, w_down=g_down, conv_w=dcw), dict(dsh2=dsh2, dsc2=dsc2, dg2=dg2, conv_b=dcb[0],
                                                                 ln2_g=dlg2[0], ln2_b=dlb2[0])


def layer_bwd_mix(dx1, sv, mod, w, part):
    f = sv["f"]
    dx0, dmix, dg1, dlg1, dlb1 = post_bwd(sv["x"], sv["mix"], mod, 2, f["ln1_g"], f["ln1_b"], dx1, "post1_bwd")
    g_out = mm_tn(sv["mixcat"], dmix, D, "out_dw")
    dmc = mm_nt(dmix, w["w_out"], "out_dx")
    duv, dgm_lg, dgm_lb, dgm_ws, dgm_bst = gm_bwd(sv["p"], f["gm_lg"], f["gm_lb"], f["gm_ws"], f["gm_bst"], dmc)
    du5, da, dbbd, dcbd, dd5, dwg, dbg = s5_bwd(sv["p"], sv["st"], sv["arow"], sv["bbd"], f["cbd"], f["drow"],
                                                f["wg"], f["bg"], dmc)
    dlr, dli, dldt, dbt = s5_prep_bwd(f["lr"], f["li"], f["ldt"], f["bt"], da, _blockdiag_extract(dbbd))
    dq, dk, dv, dfq, dfk = attn_grad(sv["qa"], sv["qat"], sv["ka"], sv["kat"], sv["p"], sv["yfx"], sv["lse"], dmc)
    dff, dbf = fox_prep_grad(sv["p"], f["bf"], dfq, dfk)
    dp = jnp.concatenate([duv, du5, dq, dk, dv, dff], axis=1)
    g_in = mm_tn(dp, sv["x"], D, "in_dw", mod=mod, rows=(0, 1))
    dx, dsh1, dsc1 = mm_nt_mod(dp, w["w_in"], sv["x"], dx0, mod, (0, 1), "in_dx")

    dmod = jnp.concatenate([dsh1, dsc1, dg1, part["dsh2"], part["dsc2"], part["dg2"]], axis=0)
    dc_re, dc_im = _c_extract(dcbd)
    dbt4 = dbt.reshape(S5_H, 2, S5_G, S5_P)
    vals = dict(b_f=dbf[0, :FX_H], gm_ln_g=dgm_lg.reshape(GM_H, HD), gm_ln_b=dgm_lb.reshape(GM_H, HD),
                gm_w_s=dgm_ws, gm_b_s=dgm_bst.T, s5_lam_re=dlr.reshape(S5_G, S5_P),
                s5_lam_im=dli.reshape(S5_G, S5_P), s5_log_dt=dldt[0, :S5_G],
                s5_b_re=jnp.transpose(dbt4[:, 0], (1, 2, 0)), s5_b_im=jnp.transpose(dbt4[:, 1], (1, 2, 0)),
                s5_c_re=dc_re, s5_c_im=dc_im, s5_d=dd5.reshape(S5_G, S5_H), s5_w_glu=_glu_extract(dwg),
                s5_b_glu=dbg.reshape(S5_G, S5_H), ln1_g=dlg1[0], ln1_b=dlb1[0], conv_b=part["conv_b"],
                ln2_g=part["ln2_g"], ln2_b=part["ln2_b"])
    return dx, dict(w_in=g_in, w_out=g_out), vals, dmod


def layer_bwd(dx, sv, mod, w):
    dx1, g_ffn, part = layer_bwd_ffn(dx, sv, mod, w)
    dx, g_mix, vals, dmod = layer_bwd_mix(dx1, sv, mod, w, part)
    return dx, dict(g_ffn, **g_mix), vals, dmod


def local_step(x, target, mods, big, sp):
    saved = []
    for l in range(DEPTH):
        x, sv = layer_fwd(x, mods[l], big[l], _layer_operands(sp, l))
        saved.append(sv)
    loss_tile, dx = loss_kernel(x, target)
    gbig, vals, dmods = [None] * DEPTH, [None] * DEPTH, [None] * DEPTH
    for l in reversed(range(DEPTH)):
        dx, gbig[l], vals[l], dmods[l] = layer_bwd(dx, saved[l], mods[l], big[l])
    gsm = {n: jnp.stack([v[n] for v in vals]) for n in SMALL}
    return loss_tile, dx, gbig, gsm, jnp.stack(dmods)


def _my_index():
    return 4 * lax.axis_index("x") + 2 * lax.axis_index("y") + lax.axis_index("c")


def exchange(tensors, scatter, name):
    n = len(tensors)

    def body(*refs):
        ins, outs = refs[:n], refs[n:2 * n]
        send_sems, recv_sems, local_sems = refs[2 * n:]
        x, y, c = lax.axis_index("x"), lax.axis_index("y"), lax.axis_index("c")
        me = 4 * x + 2 * y + c
        local = []
        for t in range(n):
            cp = pltpu.make_async_copy(ins[t].at[me] if scatter else ins[t], outs[t].at[me], local_sems.at[t])
            cp.start()
            local.append(cp)
        remote = []
        for m in range(1, NDEV):
            px = 1 - x if m & 4 else x
            py = 1 - y if m & 2 else y
            pc = 1 - c if m & 1 else c
            peer = 4 * px + 2 * py + pc
            for t in range(n):
                k = t * (NDEV - 1) + m - 1
                cp = pltpu.make_async_remote_copy(
                    src_ref=ins[t].at[peer] if scatter else ins[t], dst_ref=outs[t].at[me],
                    send_sem=send_sems.at[k], recv_sem=recv_sems.at[k],
                    device_id=(px, py, pc), device_id_type=MESH_IDS)
                cp.start()
                remote.append(cp)
        for cp in remote:
            cp.wait()
        for cp in local:
            cp.wait()

    hbm = pl.BlockSpec(memory_space=pltpu.HBM)
    out_shape = [jax.ShapeDtypeStruct(t.shape if scatter else (NDEV,) + t.shape, t.dtype) for t in tensors]
    return _call(body, name=name, in_specs=[hbm] * n, out_specs=[hbm] * n, out_shape=out_shape,
                 scratch_shapes=[pltpu.SemaphoreType.DMA((n * (NDEV - 1),)), pltpu.SemaphoreType.DMA((n * (NDEV - 1),)),
                                 pltpu.SemaphoreType.DMA((n,))])(*tensors)


def _peers():
    x, y, c = lax.axis_index("x"), lax.axis_index("y"), lax.axis_index("c")
    out = []
    for m in range(1, NDEV):
        px = 1 - x if m & 4 else x
        py = 1 - y if m & 2 else y
        pc = 1 - c if m & 1 else c
        out.append(((px, py, pc), 4 * px + 2 * py + pc))
    return 4 * x + 2 * y + c, out


def _split_copies(v_refs, land_refs, send_sems, recv_sems, scatter):
    me, peers = _peers()
    return [pltpu.make_async_remote_copy(
        src_ref=v_ref.at[idx] if scatter else v_ref, dst_ref=land_ref.at[me],
        send_sem=send_sems.at[t * (NDEV - 1) + k], recv_sem=recv_sems.at[t * (NDEV - 1) + k],
        device_id=pos, device_id_type=MESH_IDS)
        for t, (v_ref, land_ref) in enumerate(zip(v_refs, land_refs)) for k, (pos, idx) in enumerate(peers)]


_HBM_SPEC = pl.BlockSpec(memory_space=pltpu.HBM)
_SEM_SPEC = pl.BlockSpec(memory_space=pltpu.SEMAPHORE)
_SPLIT_EFFECT = pltpu.SideEffectType.DATAFLOW_SIDE_EFFECTING


def exchange_start(tensors, scatter, name):
    n = len(tensors)
    land_shapes = [t.shape if scatter else (NDEV,) + t.shape for t in tensors]

    def body(*refs):
        v_refs, land_refs = refs[:n], refs[n:2 * n]
        send_sems, recv_sems = refs[2 * n], refs[2 * n + 1]
        token = refs[-1]
        for cp in _split_copies(v_refs, land_refs, send_sems, recv_sems, scatter):
            cp.start()
        token[...] = jnp.zeros_like(token)

    sems = pltpu.SemaphoreType.DMA((n * (NDEV - 1),))
    out = _call(
        body, name=name,
        out_shape=(sems, sems, *[pltpu.HBM(t.shape, t.dtype) for t in tensors],
                   *[pltpu.HBM(s, t.dtype) for s, t in zip(land_shapes, tensors)], jax.ShapeDtypeStruct((8, 128), F32)),
        in_specs=(_HBM_SPEC,) * (2 * n),
        out_specs=(_SEM_SPEC, _SEM_SPEC) + (_HBM_SPEC,) * (2 * n) + (pl.BlockSpec(memory_space=pltpu.VMEM),),
        input_output_aliases={i: i + 2 for i in range(2 * n)},
        compiler_params=pltpu.CompilerParams(has_side_effects=_SPLIT_EFFECT),
    )(*[pltpu.with_memory_space_constraint(t, pltpu.HBM) for t in tensors],
      *[pltpu.with_memory_space_constraint(lax.empty(s, t.dtype), pltpu.HBM) for s, t in zip(land_shapes, tensors)])
    return out[0], out[1], list(out[2:2 + n]), list(out[2 + n:2 + 2 * n]), out[-1]


def exchange_wait(started, after, scatter, name):
    send_sems, recv_sems, v_thru, land_thru, _ = started
    n = len(v_thru)

    def body(*refs):
        v_refs, land_refs = refs[:n], refs[n:2 * n]
        for cp in _split_copies(v_refs, land_refs, refs[2 * n], refs[2 * n + 1], scatter):
            cp.wait_send()
            cp.wait_recv()

    out = _call(
        body, name=name,
        out_shape=tuple(pltpu.HBM(t.shape, t.dtype) for t in v_thru + land_thru),
        in_specs=(_HBM_SPEC,) * (2 * n) + (_SEM_SPEC, _SEM_SPEC, pl.BlockSpec(memory_space=pl.ANY)),
        out_specs=(_HBM_SPEC,) * (2 * n), input_output_aliases={i: i for i in range(2 * n)},
        compiler_params=pltpu.CompilerParams(has_side_effects=_SPLIT_EFFECT),
    )(*v_thru, *land_thru, send_sems, recv_sems, after)
    return list(out[:n]), list(out[n:])


def mod_slices(c_all, w_ada, b_loc):
    nl, _, nc = w_ada.shape

    def body(c_ref, w_ref, b_ref, o_ref):
        cv = c_ref[...]
        o_ref[0] = _nn(cv * jax.nn.sigmoid(cv), w_ref[0]) + b_ref[0]

    return _call(body, name="mod_slices", grid=(nl,),
                 in_specs=[pl.BlockSpec((NDEV, D), lambda l: (0, 0)), pl.BlockSpec((1, D, nc), lambda l: (l, 0, 0)),
                           pl.BlockSpec((1, 1, nc), lambda l: (l, 0, 0))],
                 out_specs=pl.BlockSpec((1, NDEV, nc), lambda l: (l, 0, 0)),
                 out_shape=jax.ShapeDtypeStruct((nl, NDEV, nc), F32),
                 compiler_params=_params("arbitrary"))(c_all, w_ada, b_loc.reshape(nl, 1, nc))


def ada_grad(c_all, dm_loc):
    nl, _, nc = dm_loc.shape

    def body(c_ref, d_ref, o_ref):
        cv = c_ref[...]
        o_ref[0] = _tn(cv * jax.nn.sigmoid(cv), d_ref[0])

    return _call(body, name="ada_grad", grid=(nl,),
                 in_specs=[pl.BlockSpec((NDEV, D), lambda l: (0, 0)), pl.BlockSpec((1, NDEV, nc), lambda l: (l, 0, 0))],
                 out_specs=pl.BlockSpec((1, D, nc), lambda l: (l, 0, 0)),
                 out_shape=jax.ShapeDtypeStruct((nl, D, nc), F32),
                 compiler_params=_params("arbitrary"))(c_all, dm_loc)


def sum_chunks(chunks):
    r = chunks.shape[1]

    def body(c_ref, o_ref):
        acc = c_ref[0]
        for i in range(1, NDEV):
            acc = acc + c_ref[i]
        o_ref[...] = acc

    return _call(body, name="sum_chunks", out_shape=jax.ShapeDtypeStruct((r, 128), F32))(chunks)


def _row_tile(r):
    if r <= 256:
        return r
    for t in range(256, 7, -8):
        if r % t == 0:
            return t
    return r


def adamw(w, m, v, g=None, chunks=None, name="adamw"):
    r, cdim = w.shape
    tr = _row_tile(r)
    bc1 = 1.0 - ADAM_B1 ** ADAM_STEP
    bc2 = 1.0 - ADAM_B2 ** ADAM_STEP

    def body(g_ref, w_ref, m_ref, v_ref, go_ref, d_ref, mo_ref, vo_ref):
        if chunks is None:
            grad = g_ref[...]
        else:
            grad = g_ref[0].astype(F32)
            for i in range(1, NDEV):
                grad = grad + g_ref[i].astype(F32)
        mn = ADAM_B1 * m_ref[...] + (1.0 - ADAM_B1) * grad
        vn = ADAM_B2 * v_ref[...] + (1.0 - ADAM_B2) * (grad * grad)
        m_hat = mn / bc1
        v_hat = vn / bc2
        go_ref[...] = grad
        d_ref[...] = -ADAM_LR * (m_hat / (jnp.sqrt(v_hat) + ADAM_EPS) + ADAM_WD * w_ref[...])
        mo_ref[...] = mn
        vo_ref[...] = vn

    tile = pl.BlockSpec((tr, cdim), lambda i: (i, 0))
    gspec = tile if chunks is None else pl.BlockSpec((NDEV, tr, cdim), lambda i: (0, i, 0))
    shp = jax.ShapeDtypeStruct((r, cdim), F32)
    return _call(body, name=name, grid=(r // tr,), in_specs=[gspec, tile, tile, tile],
                 out_specs=[tile] * 4, out_shape=[shp] * 4,
                 compiler_params=_params("arbitrary"))(g if chunks is None else chunks, w, m, v)


def adamw_layers(w, m, v, chunks, name):
    nl, r, cdim = w.shape
    tr = _row_tile(r)
    bc1 = 1.0 - ADAM_B1 ** ADAM_STEP
    bc2 = 1.0 - ADAM_B2 ** ADAM_STEP
    outs = [lax.empty(w.shape, F32) for _ in range(4)]
    for l in range(nl):
        def body(g_ref, w_ref, m_ref, v_ref, p0, p1, p2, p3, go_ref, d_ref, mo_ref, vo_ref):
            grad = g_ref[0].astype(F32)
            for i in range(1, NDEV):
                grad = grad + g_ref[i].astype(F32)
            mn = ADAM_B1 * m_ref[...] + (1.0 - ADAM_B1) * grad
            vn = ADAM_B2 * v_ref[...] + (1.0 - ADAM_B2) * (grad * grad)
            go_ref[...] = grad
            d_ref[...] = -ADAM_LR * ((mn / bc1) / (jnp.sqrt(vn / bc2) + ADAM_EPS) + ADAM_WD * w_ref[...])
            mo_ref[...] = mn
            vo_ref[...] = vn

        tile = pl.BlockSpec((None, tr, cdim), lambda i, l=l: (l, i, 0))
        whole = pl.BlockSpec(memory_space=pl.ANY)
        outs = _call(body, name=f"{name}_{l}", grid=(r // tr,),
                     in_specs=[pl.BlockSpec((NDEV, tr, cdim), lambda i: (0, i, 0)), tile, tile, tile] + [whole] * 4,
                     out_specs=[tile] * 4, out_shape=[jax.ShapeDtypeStruct(w.shape, F32)] * 4,
                     input_output_aliases={4: 0, 5: 1, 6: 2, 7: 3},
                     compiler_params=_params("arbitrary"))(chunks[l], w, m, v, *outs)
    return outs


WEIGHTS = ("w_ada", "b_ada", "w_in", "b_f", "gm_ln_g", "gm_ln_b", "gm_w_s", "gm_b_s", "s5_lam_re", "s5_lam_im",
           "s5_log_dt", "s5_b_re", "s5_b_im", "s5_c_re", "s5_c_im", "s5_d", "s5_w_glu", "s5_b_glu", "w_out", "ln1_g",
           "ln1_b", "w_up", "conv_w", "conv_b", "w_down", "ln2_g", "ln2_b")
SHARDED = ("w_in", "w_out", "w_up", "w_down", "conv_w")
LARGE = ("w_in", "w_out", "w_up", "w_down")
TRANSPOSED = ("w_in", "w_up")
PACKED = ("b_ada",) + SMALL
PACK_SEG = 8 * 128


def _gather_cols(g):
    nd, nl, r, c = g.shape
    return jnp.transpose(g, (1, 2, 0, 3)).reshape(nl, r, nd * c)


def _chunk_cols(g):
    nl, r, c8 = g.shape
    return jnp.transpose(g.reshape(nl, r, NDEV, c8 // NDEV), (2, 0, 1, 3))


def _join_rows(g):
    nd, r, c = g.shape
    return g.reshape(nd * r, c)


def _split_rows(g):
    r8, c = g.shape
    return g.reshape(NDEV, r8 // NDEV, c)


def _pack(parts):
    segs = []
    for n in PACKED:
        flat = parts[n].reshape(-1)
        segs.append(jnp.pad(flat, (0, -flat.shape[0] % PACK_SEG)).reshape(-1, 128))
    rows = jnp.concatenate(segs, axis=0)
    return jnp.pad(rows, ((0, -rows.shape[0] % (NDEV * 8)), (0, 0)))


def _unpack(rows, shapes):
    out, off = {}, 0
    for n in PACKED:
        size = math.prod(shapes[n])
        nrows = -(-size // PACK_SEG) * 8
        out[n] = rows[off:off + nrows].reshape(-1)[:size].reshape(shapes[n])
        off += nrows
    return out


def kernel(x, c, w_ada, b_ada, w_in, b_f, gm_ln_g, gm_ln_b, gm_w_s, gm_b_s, s5_lam_re, s5_lam_im, s5_log_dt, s5_b_re, s5_b_im, s5_c_re, s5_c_im, s5_d, s5_w_glu, s5_b_glu, w_out, ln1_g, ln1_b, w_up, conv_w, conv_b, w_down, ln2_g, ln2_b, loss_target, m_w_ada, m_b_ada, m_w_in, m_b_f, m_gm_ln_g, m_gm_ln_b, m_gm_w_s, m_gm_b_s, m_s5_lam_re, m_s5_lam_im, m_s5_log_dt, m_s5_b_re, m_s5_b_im, m_s5_c_re, m_s5_c_im, m_s5_d, m_s5_w_glu, m_s5_b_glu, m_w_out, m_ln1_g, m_ln1_b, m_w_up, m_conv_w, m_conv_b, m_w_down, m_ln2_g, m_ln2_b, v_w_ada, v_b_ada, v_w_in, v_b_f, v_gm_ln_g, v_gm_ln_b, v_gm_w_s, v_gm_b_s, v_s5_lam_re, v_s5_lam_im, v_s5_log_dt, v_s5_b_re, v_s5_b_im, v_s5_c_re, v_s5_c_im, v_s5_d, v_s5_w_glu, v_s5_b_glu, v_w_out, v_ln1_g, v_ln1_b, v_w_up, v_conv_w, v_conv_b, v_w_down, v_ln2_g, v_ln2_b):
    given = dict(locals())
    wts = {n: given[n] for n in WEIGHTS}
    mom = {n: given["m_" + n] for n in WEIGHTS}
    var = {n: given["v_" + n] for n in WEIGHTS}
    nl = w_ada.shape[0]
    me = _my_index()
    ada_cols = w_ada.shape[2]

    (c_all,) = exchange([c], False, "gather_c")
    c_all = c_all.reshape(NDEV, D)
    b_loc = lax.dynamic_slice_in_dim(b_ada, me * ada_cols, ada_cols, axis=1)
    mod_part = mod_slices(c_all, w_ada, b_loc)

    mod_all, conv_all = exchange([mod_part, conv_w], False, "gather_mod")
    mod_mine = lax.dynamic_index_in_dim(mod_all, me, axis=2, keepdims=False)
    mods = jnp.transpose(mod_mine, (1, 0, 2)).reshape(nl, 6, D)
    mods = jnp.pad(mods, ((0, 0), (0, 2), (0, 0)))
    conv_full = _gather_cols(conv_all)
    sp = {n: wts[n] for n in SMALL}
    rowwise = {n: [jnp.swapaxes(a[n], 1, 2) if n in TRANSPOSED else a[n] for a in (wts, mom, var)] for n in LARGE}

    def joined(own, land):
        return _join_rows(lax.dynamic_update_index_in_dim(land, own, me, 0))

    def block(l, names):
        return [rowwise[n][0][l].astype(BF16) for n in names]

    def chunked(grads, names):
        return [_split_rows(grads[n]) for n in names]

    head, tail = LARGE[:1], LARGE[1:]
    got_head = exchange_start(block(0, head), False, "gather_start_0_in")
    rest = block(0, tail)
    rest[0] = rest[0] + got_head[4][0, 0].astype(BF16)
    got_tail = exchange_start(rest, False, "gather_start_0_rest")
    xl, saved, weights = x[0], [], []
    for l in range(nl):
        if l == 0:
            own, land = exchange_wait(got_head, got_tail[4], False, "gather_wait_0_in")
            w = {n: joined(o, g) for n, o, g in zip(head, own, land)}
        else:
            own, land = exchange_wait(started, xl, False, f"gather_wait_{l}")
            w = {n: joined(o, g) for n, o, g in zip(LARGE, own, land)}
        mod_l = mods[l]
        if l + 1 < nl:
            nxt, w["w_in"] = lax.optimization_barrier((block(l + 1, LARGE), w["w_in"]))
            started = exchange_start(nxt, False, f"gather_start_{l + 1}")
            mod_l = mod_l + started[4][0, 0]
        w["w_in"] = jnp.pad(w["w_in"], ((0, NP - D_IN), (0, 0)))
        f = _layer_operands(sp, l)
        mixcat, sv = layer_fwd_mix(xl, mod_l, w["w_in"], f)
        if l == 0:
            own, land = exchange_wait(got_tail, mixcat, False, "gather_wait_0_rest")
            w.update({n: joined(o, g) for n, o, g in zip(tail, own, land)})
        w["conv_w"] = conv_full[l]
        xl, sv = layer_fwd_rest(xl, mixcat, mod_l, w, f, sv)
        weights.append(w)
        saved.append(sv)

    loss_tile, dx = loss_kernel(xl, loss_target[0])

    ffn_names, mix_names = ("w_up", "w_down"), ("w_in", "w_out")
    scattering, vals, dmods, gconv = [None] * nl, [None] * nl, [None] * nl, [None] * nl
    token = jnp.zeros((), F32)
    for l in reversed(range(nl)):
        mod_l = mods[l] + token
        dx1, g_ffn, part = layer_bwd_ffn(dx, saved[l], mod_l, weights[l])
        gconv[l] = g_ffn["conv_w"]
        if l == 0:
            sent_ffn = exchange_start(chunked(g_ffn, ffn_names), True, "scatter_start_0_ffn")
            mod_l = mod_l + sent_ffn[4][0, 0]
        dx, g_mix, vals[l], dmods[l] = layer_bwd_mix(dx1, saved[l], mod_l, weights[l], part)
        g_mix["w_in"] = g_mix["w_in"][:D_IN]
        if l == 0:
            scattering[l] = [(ffn_names, sent_ffn),
                             (mix_names, exchange_start(chunked(g_mix, mix_names), True, "scatter_start_0_mix"))]
        else:
            sent = exchange_start(chunked(dict(g_ffn, **g_mix), LARGE), True, f"scatter_start_{l}")
            scattering[l] = [(LARGE, sent)]
            token = sent[4][0, 0]
    gx = dx
    dmods = jnp.stack(dmods)
    gsm = {n: jnp.stack([v[n] for v in vals]) for n in SMALL}

    gsm["b_ada"] = dmods.reshape(nl, 6 * D)
    packed = _pack(gsm).reshape(NDEV, -1, 128)
    conv_recv, small_recv = exchange([_chunk_cols(jnp.stack(gconv)), packed], True, "scatter_small")
    small_sum = sum_chunks(small_recv)
    small_all, dmod_all = exchange([small_sum, dmods.reshape(nl, 6 * D)], False, "gather_small")

    received = [dict() for _ in range(nl)]

    def arrive(l, k, after):
        names, sent = scattering[l][k]
        own, land = exchange_wait(sent, after, True, f"scatter_wait_{l}_{k}")
        for n, o, g in zip(names, own, land):
            mine = lax.dynamic_index_in_dim(o, me, 0, keepdims=False)
            received[l][n] = lax.dynamic_update_index_in_dim(g, mine, me, 0)
        return land[0]

    after = small_all
    for l in reversed(range(nl)):
        after = arrive(l, 0, after)
    out = {}
    def update(n):
        res = adamw_layers(*rowwise[n], [received[l][n] for l in range(nl)], "adamw_" + n)
        return [jnp.swapaxes(r, 1, 2) for r in res] if n in TRANSPOSED else res

    for n in ffn_names:
        out[n] = update(n)
    arrive(0, 1, out[ffn_names[-1]][0])
    for n in mix_names:
        out[n] = update(n)
    shp = conv_w.shape
    two_d = lambda a: a.reshape(shp[0] * shp[1], shp[2])
    res = adamw(two_d(conv_w), two_d(m_conv_w), two_d(v_conv_w),
                chunks=conv_recv.reshape(NDEV, shp[0] * shp[1], shp[2]), name="adamw_conv_w")
    out["conv_w"] = [r.reshape(shp) for r in res]

    dm_loc = lax.dynamic_slice_in_dim(dmod_all, me * ada_cols, ada_cols, axis=2)
    g_ada = ada_grad(c_all, jnp.transpose(dm_loc, (1, 0, 2)))
    two_d = lambda a: a.reshape(nl * D, ada_cols)
    res = adamw(two_d(w_ada), two_d(m_w_ada), two_d(v_w_ada), g=two_d(g_ada), name="adamw_w_ada")
    out["w_ada"] = [r.reshape(w_ada.shape) for r in res]

    shapes = {n: wts[n].shape for n in PACKED}
    res = adamw(_pack(wts), _pack(mom), _pack(var), g=small_all.reshape(-1, 128), name="adamw_small")
    unpacked = [_unpack(r, shapes) for r in res]
    for n in PACKED:
        out[n] = [u[n] for u in unpacked]

    loss = lax.psum(loss_tile[0, 0], ("x", "y", "c"))
    return (loss, gx[None], *[out[n][0] for n in WEIGHTS], *[out[n][1] for n in WEIGHTS],
            *[out[n][2] for n in WEIGHTS], *[out[n][3] for n in WEIGHTS])
```

```python
import math

import jax
import jax.numpy as jnp
from jax import lax
from jax.experimental import pallas as pl
from jax.experimental.pallas import tpu as pltpu

F32 = jnp.float32
BF16 = jnp.bfloat16
MESH_IDS = pl.DeviceIdType.MESH

D = 1024
SEQ = 4096
DEPTH = 4
NDEV = 8
HD = 64
GM_W = 256
GM_H = 4
GM_C = 128
S5_W = 256
S5_G = 16
S5_H = 16
S5_P = 64
S5_N = S5_G * S5_P
FX_W = 512
FX_H = 8
D_IN = 2 * GM_W + S5_W + 3 * FX_W + FX_H
NP = 2432
FF_COL = 2304
DFF = 2816
LN_EPS = 1e-5
DN_ALPHA = (2.0 * DEPTH) ** 0.25
NEG_INF = -1e30
ADAM_LR = 0.001
ADAM_B1 = 0.9
ADAM_B2 = 0.999
ADAM_EPS = 1e-08
ADAM_WD = 0.01
ADAM_STEP = 10

V7X_VMEM_LIMIT = 56 * 1024 * 1024
TS = 512
TS_C = 256
T_S5 = 512
TQ = 512
TQ_FWD = 1024
ATTN_HEADS = 2


def _call(body, **kw):
    return pl.pallas_call(body, **kw)


def _params(*sem):
    return pltpu.CompilerParams(dimension_semantics=sem if sem else None,
                                vmem_limit_bytes=V7X_VMEM_LIMIT)


def _nn(a, b):
    return jnp.dot(a.astype(BF16), b.astype(BF16), preferred_element_type=F32)


def _nt(a, b):
    return lax.dot_general(a.astype(BF16), b.astype(BF16), (((1,), (1,)), ((), ())),
                           preferred_element_type=F32)


def _tn(a, b):
    return lax.dot_general(a.astype(BF16), b.astype(BF16), (((0,), (0,)), ((), ())),
                           preferred_element_type=F32)


@jax.custom_vjp
def _bdot(a, b):
    return _nn(a, b)


def _bdot_fwd(a, b):
    return _nn(a, b), (a, b)


def _bdot_bwd(res, g):
    a, b = res
    return _nt(g, b), _tn(a, g)


_bdot.defvjp(_bdot_fwd, _bdot_bwd)


@jax.custom_vjp
def _bdot_nt(a, b):
    return _nt(a, b)


def _bdot_nt_fwd(a, b):
    return _nt(a, b), (a, b)


def _bdot_nt_bwd(res, g):
    a, b = res
    return _nn(g, b), _tn(g, a)


_bdot_nt.defvjp(_bdot_nt_fwd, _bdot_nt_bwd)


def _ln(r, g, b):
    mu = jnp.mean(r, axis=-1, keepdims=True)
    xc = r - mu
    var = jnp.mean(xc * xc, axis=-1, keepdims=True)
    return xc * lax.rsqrt(var + LN_EPS) * g + b


def _rows(shape):
    return lax.broadcasted_iota(jnp.int32, shape, 0)


def _lanes(shape):
    return lax.broadcasted_iota(jnp.int32, shape, 1)


def mm_nn(a, w, tn, name, mod=None, rows=None, out_dtype=F32, wt=False):
    s, k = a.shape
    n = w.shape[0] if wt else w.shape[1]
    ts = min(TS, s)

    def body(*refs):
        if mod is None:
            a_ref, w_ref, o_ref = refs
            h = a_ref[...]
        else:
            a_ref, m_ref, w_ref, o_ref = refs
            h = a_ref[...] * (1.0 + m_ref[rows[1]:rows[1] + 1, :]) + m_ref[rows[0]:rows[0] + 1, :]
        hb = h.astype(BF16)
        prod = _nt(hb, w_ref[...]) if wt else jnp.dot(hb, w_ref[...], preferred_element_type=F32)
        o_ref[...] = prod.astype(out_dtype)

    in_specs = [pl.BlockSpec((ts, k), lambda j, i: (i, 0))]
    args = [a]
    if mod is not None:
        in_specs.append(pl.BlockSpec((8, k), lambda j, i: (0, 0)))
        args.append(mod)
    in_specs.append(pl.BlockSpec((tn, k), lambda j, i: (j, 0)) if wt else pl.BlockSpec((k, tn), lambda j, i: (0, j)))
    args.append(w)
    return _call(body, name=name, grid=(n // tn, s // ts), in_specs=in_specs,
                 out_specs=pl.BlockSpec((ts, tn), lambda j, i: (i, j)),
                 out_shape=jax.ShapeDtypeStruct((s, n), out_dtype),
                 compiler_params=_params("arbitrary", "arbitrary"))(*args)


def mm_nt(dy, w, name):
    s, n = dy.shape
    k = w.shape[0]
    ts = min(TS, s)

    def body(dy_ref, w_ref, o_ref):
        o_ref[...] = _nt(dy_ref[...], w_ref[...])

    return _call(body, name=name, grid=(s // ts,),
                 in_specs=[pl.BlockSpec((ts, n), lambda i: (i, 0)),
                           pl.BlockSpec((k, n), lambda i: (0, 0))],
                 out_specs=pl.BlockSpec((ts, k), lambda i: (i, 0)),
                 out_shape=jax.ShapeDtypeStruct((s, k), F32),
                 compiler_params=_params("arbitrary"))(dy, w)


def mm_nt_mod(dy, w, x, dres, mod, rows, name):
    s, n = dy.shape
    k = w.shape[1]
    ts = min(TS, s)

    def body(dy_ref, w_ref, x_ref, r_ref, m_ref, dx_ref, dsh_ref, dsc_ref):
        @pl.when(pl.program_id(0) == 0)
        def _():
            dsh_ref[...] = jnp.zeros_like(dsh_ref)
            dsc_ref[...] = jnp.zeros_like(dsc_ref)

        dh = _nn(dy_ref[...], w_ref[...])
        dx_ref[...] = r_ref[...] + dh * (1.0 + m_ref[rows[1]:rows[1] + 1, :])
        dsh_ref[...] += jnp.sum(dh, axis=0, keepdims=True)
        dsc_ref[...] += jnp.sum(dh * x_ref[...], axis=0, keepdims=True)

    row = pl.BlockSpec((1, k), lambda i: (0, 0))
    tile = pl.BlockSpec((ts, k), lambda i: (i, 0))
    return _call(body, name=name, grid=(s // ts,),
                 in_specs=[pl.BlockSpec((ts, n), lambda i: (i, 0)),
                           pl.BlockSpec((n, k), lambda i: (0, 0)), tile, tile,
                           pl.BlockSpec((8, k), lambda i: (0, 0))],
                 out_specs=[tile, row, row],
                 out_shape=[jax.ShapeDtypeStruct((s, k), F32),
                            jax.ShapeDtypeStruct((1, k), F32),
                            jax.ShapeDtypeStruct((1, k), F32)],
                 compiler_params=_params("arbitrary"))(dy, w, x, dres, mod)


def mm_tn(a, dy, tn, name, mod=None, rows=None):
    s, k = a.shape
    n = dy.shape[1]
    ts = min(TS, s)
    ns = s // ts

    def body(*refs):
        if mod is None:
            a_ref, dy_ref, o_ref, acc = refs
            h = dy_ref[...]
        else:
            a_ref, dy_ref, m_ref, o_ref, acc = refs
            h = dy_ref[...] * (1.0 + m_ref[rows[1]:rows[1] + 1, :]) + m_ref[rows[0]:rows[0] + 1, :]
        i = pl.program_id(1)

        @pl.when(i == 0)
        def _():
            acc[...] = jnp.zeros_like(acc)

        acc[...] += _tn(a_ref[...], h)

        @pl.when(i == ns - 1)
        def _():
            o_ref[...] = acc[...].astype(BF16)

    in_specs = [pl.BlockSpec((ts, k), lambda j, i: (i, 0)), pl.BlockSpec((ts, tn), lambda j, i: (i, j))]
    args = [a, dy]
    if mod is not None:
        in_specs.append(pl.BlockSpec((8, tn), lambda j, i: (0, j)))
        args.append(mod)
    return _call(body, name=name, grid=(n // tn, s // ts), in_specs=in_specs,
                 out_specs=pl.BlockSpec((k, tn), lambda j, i: (0, j)),
                 out_shape=jax.ShapeDtypeStruct((k, n), BF16),
                 scratch_shapes=[pltpu.VMEM((k, tn), F32)],
                 compiler_params=_params("arbitrary", "arbitrary"))(*args)


def _post_fn(x, br, gate, lg, lb):
    return _ln(DN_ALPHA * x + (1.0 + gate) * br, lg, lb)


def post_fwd(x, br, mod, grow, lg, lb, name):
    s = x.shape[0]
    ts = min(TS, s)

    def body(x_ref, b_ref, m_ref, lg_ref, lb_ref, o_ref):
        o_ref[...] = _post_fn(x_ref[...], b_ref[...], m_ref[grow:grow + 1, :], lg_ref[...], lb_ref[...])

    tile = pl.BlockSpec((ts, D), lambda i: (i, 0))
    row = pl.BlockSpec((1, D), lambda i: (0, 0))
    return _call(body, name=name, grid=(s // ts,),
                 in_specs=[tile, tile, pl.BlockSpec((8, D), lambda i: (0, 0)), row, row],
                 out_specs=tile, out_shape=jax.ShapeDtypeStruct((s, D), F32),
                 compiler_params=_params("arbitrary"))(x, br, mod, lg, lb)


def post_bwd(x, br, mod, grow, lg, lb, dy, name):
    s = x.shape[0]
    ts = min(TS, s)

    def body(x_ref, b_ref, m_ref, lg_ref, lb_ref, dy_ref, dx_ref, db_ref, dg_ref, dlg_ref, dlb_ref):
        @pl.when(pl.program_id(0) == 0)
        def _():
            dg_ref[...] = jnp.zeros_like(dg_ref)
            dlg_ref[...] = jnp.zeros_like(dlg_ref)
            dlb_ref[...] = jnp.zeros_like(dlb_ref)

        _, vjp = jax.vjp(_post_fn, x_ref[...], b_ref[...], m_ref[grow:grow + 1, :], lg_ref[...], lb_ref[...])
        dx, db, dg, dlg, dlb = vjp(dy_ref[...])
        dx_ref[...] = dx
        db_ref[...] = db.astype(BF16)
        dg_ref[...] += dg
        dlg_ref[...] += dlg
        dlb_ref[...] += dlb

    tile = pl.BlockSpec((ts, D), lambda i: (i, 0))
    row = pl.BlockSpec((1, D), lambda i: (0, 0))
    rs = jax.ShapeDtypeStruct((1, D), F32)
    return _call(body, name=name, grid=(s // ts,),
                 in_specs=[tile, tile, pl.BlockSpec((8, D), lambda i: (0, 0)), row, row, tile],
                 out_specs=[tile, tile, row, row, row],
                 out_shape=[jax.ShapeDtypeStruct((s, D), F32), jax.ShapeDtypeStruct((s, D), BF16), rs, rs, rs],
                 compiler_params=_params("arbitrary"))(x, br, mod, lg, lb, dy)


def loss_kernel(y, target):
    s = y.shape[0]
    ts = min(TS, s)

    def body(y_ref, t_ref, l_ref, dy_ref):
        @pl.when(pl.program_id(0) == 0)
        def _():
            l_ref[...] = jnp.zeros_like(l_ref)

        err = y_ref[...] - t_ref[...]
        dy_ref[...] = err * (1.0 / D)
        per_tok = jnp.mean(err * err, axis=-1, keepdims=True)
        l_ref[...] += 0.5 * jnp.sum(per_tok)

    tile = pl.BlockSpec((ts, D), lambda i: (i, 0))
    return _call(body, name="loss", grid=(s // ts,), in_specs=[tile, tile],
                 out_specs=[pl.BlockSpec((8, 128), lambda i: (0, 0)), tile],
                 out_shape=[jax.ShapeDtypeStruct((8, 128), F32), jax.ShapeDtypeStruct((s, D), F32)],
                 compiler_params=_params("arbitrary"))(y, target)


def _gm_pair(u, v, lg, lb, w0, w1, bs0, bs1):
    t = u.shape[0]
    low = _lanes((t, 2 * HD)) < HD

    def head_mean(x):
        lo = jnp.sum(jnp.where(low, x, 0.0), axis=-1, keepdims=True)
        hi = jnp.sum(jnp.where(low, 0.0, x), axis=-1, keepdims=True)
        return jnp.where(low, lo, hi) * (1.0 / HD)

    xc = v - head_mean(v)
    vn = xc * lax.rsqrt(head_mean(xc * xc) + LN_EPS) * lg + lb
    v0 = jnp.where(low, vn, 0.0)
    v1 = jnp.where(low, 0.0, vn)
    causal = _rows((GM_C, GM_C)) >= _lanes((GM_C, GM_C))
    wm0 = jnp.where(causal, w0, 0.0)
    wm1 = jnp.where(causal, w1, 0.0)
    bias = jnp.where(_lanes((GM_C, 2 * HD)) < HD, bs0, bs1)
    chunks = []
    for n in range(t // GM_C):
        rs = slice(n * GM_C, (n + 1) * GM_C)
        chunks.append(u[rs] * (_bdot(wm0, v0[rs]) + _bdot(wm1, v1[rs]) + bias))
    return jnp.concatenate(chunks, axis=0)


def gm_fwd(p, lg, lb, ws, bst):
    s = p.shape[0]
    ts = min(TS_C, s)

    def body(u_ref, v_ref, lg_ref, lb_ref, ws_ref, bs_ref, o_ref):
        for j in range(GM_H // 2):
            sl = slice(j * 2 * HD, (j + 1) * 2 * HD)
            o_ref[:, sl] = _gm_pair(u_ref[:, sl], v_ref[:, sl], lg_ref[:, sl], lb_ref[:, sl], ws_ref[2 * j],
                                    ws_ref[2 * j + 1], bs_ref[:, 2 * j:2 * j + 1],
                                    bs_ref[:, 2 * j + 1:2 * j + 2]).astype(BF16)

    full = lambda shape: pl.BlockSpec(shape, lambda i: (0,) * len(shape))
    return _call(body, name="gm_fwd", grid=(s // ts,),
                 in_specs=[pl.BlockSpec((ts, GM_W), lambda i: (i, 0)), pl.BlockSpec((ts, GM_W), lambda i: (i, 1)),
                           full((1, GM_W)), full((1, GM_W)), full((GM_H, GM_C, GM_C)), full((GM_C, GM_H))],
                 out_specs=pl.BlockSpec((ts, GM_W), lambda i: (i, 0)),
                 out_shape=jax.ShapeDtypeStruct((s, GM_W), BF16),
                 compiler_params=_params("arbitrary"))(p, p, lg, lb, ws, bst)


def gm_bwd(p, lg, lb, ws, bst, dmix):
    s = p.shape[0]
    ts = min(TS_C, s)

    def body(u_ref, v_ref, lg_ref, lb_ref, ws_ref, bs_ref, dy_ref, duv_ref, dlg_ref, dlb_ref, dws_ref, dbs_ref):
        @pl.when(pl.program_id(0) == 0)
        def _():
            dlg_ref[...] = jnp.zeros_like(dlg_ref)
            dlb_ref[...] = jnp.zeros_like(dlb_ref)
            dws_ref[...] = jnp.zeros_like(dws_ref)
            dbs_ref[...] = jnp.zeros_like(dbs_ref)

        for j in range(GM_H // 2):
            sl = slice(j * 2 * HD, (j + 1) * 2 * HD)
            _, vjp = jax.vjp(_gm_pair, u_ref[:, sl], v_ref[:, sl], lg_ref[:, sl], lb_ref[:, sl], ws_ref[2 * j],
                             ws_ref[2 * j + 1], bs_ref[:, 2 * j:2 * j + 1], bs_ref[:, 2 * j + 1:2 * j + 2])
            du, dv, dlg, dlb, dw0, dw1, dbs0, dbs1 = vjp(dy_ref[:, sl])
            duv_ref[:, sl] = du.astype(BF16)
            duv_ref[:, GM_W + j * 2 * HD:GM_W + (j + 1) * 2 * HD] = dv.astype(BF16)
            dlg_ref[:, sl] += dlg
            dlb_ref[:, sl] += dlb
            dws_ref[2 * j] += dw0
            dws_ref[2 * j + 1] += dw1
            dbs_ref[:, 2 * j:2 * j + 1] += dbs0
            dbs_ref[:, 2 * j + 1:2 * j + 2] += dbs1

    full = lambda shape: pl.BlockSpec(shape, lambda i: (0,) * len(shape))
    return _call(body, name="gm_bwd", grid=(s // ts,),
                 in_specs=[pl.BlockSpec((ts, GM_W), lambda i: (i, 0)), pl.BlockSpec((ts, GM_W), lambda i: (i, 1)),
                           full((1, GM_W)), full((1, GM_W)), full((GM_H, GM_C, GM_C)), full((GM_C, GM_H)),
                           pl.BlockSpec((ts, GM_W), lambda i: (i, 0))],
                 out_specs=[pl.BlockSpec((ts, 2 * GM_W), lambda i: (i, 0)), full((1, GM_W)), full((1, GM_W)),
                            full((GM_H, GM_C, GM_C)), full((GM_C, GM_H))],
                 out_shape=[jax.ShapeDtypeStruct((s, 2 * GM_W), BF16), jax.ShapeDtypeStruct((1, GM_W), F32),
                            jax.ShapeDtypeStruct((1, GM_W), F32), jax.ShapeDtypeStruct((GM_H, GM_C, GM_C), F32),
                            jax.ShapeDtypeStruct((GM_C, GM_H), F32)],
                 compiler_params=_params("arbitrary"))(p, p, lg, lb, ws, bst, dmix)


def _s5_prep_fn(lr, li, ldt, bt):
    dt = jnp.exp(ldt)
    er = jnp.exp(lr * dt)
    ar = er * jnp.cos(li * dt)
    ai = er * jnp.sin(li * dt)
    den = lr * lr + li * li
    nr = ar - 1.0
    cr = (nr * lr + ai * li) / den
    ci = (ai * lr - nr * li) / den
    br, bi = bt[:, :S5_N], bt[:, S5_N:]
    return ar, ai, jnp.concatenate([cr * br - ci * bi, cr * bi + ci * br], axis=1)


def s5_prep_fwd(lr, li, ldt, bt):
    def body(lr_ref, li_ref, ldt_ref, bt_ref, a_ref, bb_ref):
        ar, ai, bb = _s5_prep_fn(lr_ref[...], li_ref[...], ldt_ref[...], bt_ref[...])
        a_ref[...] = jnp.concatenate([ar, ai, jnp.zeros((6, S5_N), F32)], axis=0)
        bb_ref[...] = bb

    return _call(body, name="s5_prep_fwd",
                 out_shape=[jax.ShapeDtypeStruct((8, S5_N), F32), jax.ShapeDtypeStruct((S5_H, 2 * S5_N), F32)])(lr, li, ldt, bt)


def s5_prep_bwd(lr, li, ldt, bt, da, dbb):
    def body(lr_ref, li_ref, ldt_ref, bt_ref, da_ref, dbb_ref, dlr_ref, dli_ref, dldt_ref, dbt_ref):
        _, vjp = jax.vjp(_s5_prep_fn, lr_ref[...], li_ref[...], ldt_ref[...], bt_ref[...])
        dlr, dli, dldt, dbt = vjp((da_ref[0:1, :], da_ref[1:2, :], dbb_ref[...]))
        dlr_ref[...] = dlr
        dli_ref[...] = dli
        dbt_ref[...] = dbt
        group = (_rows((S5_N, 128)) // S5_P == _lanes((S5_N, 128))).astype(F32)
        dldt_ref[...] = jnp.dot(jnp.broadcast_to(dldt, (8, S5_N)), group, precision=lax.Precision.HIGHEST,
                                preferred_element_type=F32)[0:1, :]

    r = jax.ShapeDtypeStruct((1, S5_N), F32)
    return _call(body, name="s5_prep_bwd",
                 out_shape=[r, r, jax.ShapeDtypeStruct((1, 128), F32),
                            jax.ShapeDtypeStruct((S5_H, 2 * S5_N), F32)])(lr, li, ldt, bt, da, dbb)


def _s5_out_fn(x, u, cbd, drow, wg, bg):
    y = _bdot_nt(x[:, :S5_N], cbd[:, :S5_N]) - _bdot_nt(x[:, S5_N:], cbd[:, S5_N:]) + drow * u
    y = jax.nn.gelu(y)
    gate = _bdot_nt(y, wg) + bg
    return y * jax.nn.sigmoid(gate)


def _scan_chunk(buf, ar, ai, cr, ci, reverse):
    t = buf.shape[0]

    def local(xr, xi, rows):
        within = _rows(xr.shape) % 8
        pr, pi = ar, ai
        for d in (1, 2, 4):
            keep = within < 8 - d if reverse else within >= d
            shift = rows - d if reverse else d
            sr = jnp.where(keep, pltpu.roll(xr, shift, 0), 0.0)
            si = jnp.where(keep, pltpu.roll(xi, shift, 0), 0.0)
            xr, xi = xr + pr * sr - pi * si, xi + pr * si + pi * sr
            pr, pi = pr * pr - pi * pi, 2.0 * pr * pi
        return xr, xi

    xr, xi = local(buf[:, :S5_N], buf[:, S5_N:], t)
    buf[:, :S5_N] = xr
    buf[:, S5_N:] = xi
    edge = _rows((8, S5_N)) == (7 if reverse else 0)
    pr8, pi8 = local(jnp.where(edge, ar, 0.0), jnp.where(edge, ai, 0.0), 8)

    def group(j, c):
        g = t // 8 - 1 - j if reverse else j
        rows = pl.ds(pl.multiple_of(g * 8, 8), 8)
        gr = buf[rows, :S5_N] + pr8 * c[0] - pi8 * c[1]
        gi = buf[rows, S5_N:] + pr8 * c[1] + pi8 * c[0]
        buf[rows, :S5_N] = gr
        buf[rows, S5_N:] = gi
        return (gr[0:1, :], gi[0:1, :]) if reverse else (gr[7:8, :], gi[7:8, :])

    return lax.fori_loop(0, t // 8, group, (cr, ci), unroll=4)


def s5_fwd(p, arow, bbd, cbd, drow, wg, bg):
    s = p.shape[0]
    t = min(T_S5, s)

    def body(u_ref, a_ref, bbd_ref, cbd_ref, d_ref, wg_ref, bg_ref, y_ref, st_ref, carry):
        @pl.when(pl.program_id(0) == 0)
        def _():
            carry[...] = jnp.zeros_like(carry)

        u = u_ref[...]
        st_ref[...] = _nn(u, bbd_ref[...])
        cr, ci = _scan_chunk(st_ref, a_ref[0:1, :], a_ref[1:2, :], carry[0:1, :S5_N], carry[0:1, S5_N:], False)
        carry[0:1, :S5_N] = cr
        carry[0:1, S5_N:] = ci
        y_ref[...] = _s5_out_fn(st_ref[...], u, cbd_ref[...], d_ref[...], wg_ref[...], bg_ref[...]).astype(BF16)

    full = lambda shape: pl.BlockSpec(shape, lambda i: (0,) * len(shape))
    return _call(body, name="s5_fwd", grid=(s // t,),
                 in_specs=[pl.BlockSpec((t, S5_W), lambda i: (i, 2)), full((8, S5_N)), full((S5_W, 2 * S5_N)),
                           full((S5_W, 2 * S5_N)), full((1, S5_W)), full((S5_W, S5_W)), full((1, S5_W))],
                 out_specs=[pl.BlockSpec((t, S5_W), lambda i: (i, 0)), pl.BlockSpec((t, 2 * S5_N), lambda i: (i, 0))],
                 out_shape=[jax.ShapeDtypeStruct((s, S5_W), BF16), jax.ShapeDtypeStruct((s, 2 * S5_N), F32)],
                 scratch_shapes=[pltpu.VMEM((8, 2 * S5_N), F32)],
                 compiler_params=_params("arbitrary"))(p, arow, bbd, cbd, drow, wg, bg)


def s5_bwd(p, st, arow, bbd, cbd, drow, wg, bg, dmix):
    s = p.shape[0]
    t = min(T_S5, s)
    nc = s // t

    def body(u_ref, st_ref, prev_ref, a_ref, bbd_ref, cbd_ref, d_ref, wg_ref, bg_ref, dy_ref,
             du_ref, da_ref, dbbd_ref, dcbd_ref, dd_ref, dwg_ref, dbg_ref, carry, gbuf):
        i = pl.program_id(0)

        @pl.when(i == 0)
        def _():
            carry[...] = jnp.zeros_like(carry)
            for r in (da_ref, dbbd_ref, dcbd_ref, dd_ref, dwg_ref, dbg_ref):
                r[...] = jnp.zeros_like(r)

        u = u_ref[...]
        x = st_ref[...]
        _, vjp = jax.vjp(_s5_out_fn, x, u, cbd_ref[...], d_ref[...], wg_ref[...], bg_ref[...])
        dx, du1, dcbd, dd, dwg, dbg = vjp(dy_ref[...])
        gbuf[...] = dx
        cr, ci = _scan_chunk(gbuf, a_ref[0:1, :], -a_ref[1:2, :], carry[0:1, :S5_N], carry[0:1, S5_N:], True)
        carry[0:1, :S5_N] = cr
        carry[0:1, S5_N:] = ci
        gr, gi = gbuf[:, :S5_N], gbuf[:, S5_N:]
        rid = _rows((t, S5_N))
        has_prev = (i < nc - 1).astype(F32)
        top_r = prev_ref[7:8, :S5_N] * has_prev
        top_i = prev_ref[7:8, S5_N:] * has_prev
        xpr = jnp.where(rid == 0, top_r, pltpu.roll(x[:, :S5_N], 1, 0))
        xpi = jnp.where(rid == 0, top_i, pltpu.roll(x[:, S5_N:], 1, 0))
        da_ref[0:1, :] += jnp.sum(xpr * gr + xpi * gi, axis=0, keepdims=True)
        da_ref[1:2, :] += jnp.sum(xpr * gi - xpi * gr, axis=0, keepdims=True)
        g = jnp.concatenate([gr, gi], axis=1)
        dbbd_ref[...] += _tn(u, g)
        du_ref[...] = (_nt(g, bbd_ref[...]) + du1).astype(BF16)
        dcbd_ref[...] += dcbd
        dd_ref[...] += dd
        dwg_ref[...] += dwg
        dbg_ref[...] += dbg

    full = lambda shape: pl.BlockSpec(shape, lambda i: (0,) * len(shape))
    rev = lambda col: (lambda i: (nc - 1 - i, col))
    prev_map = lambda i: (jnp.maximum((nc - 1 - i) * (t // 8) - 1, 0), 0)
    return _call(body, name="s5_bwd", grid=(nc,),
                 in_specs=[pl.BlockSpec((t, S5_W), rev(2)), pl.BlockSpec((t, 2 * S5_N), rev(0)),
                           pl.BlockSpec((8, 2 * S5_N), prev_map), full((8, S5_N)), full((S5_W, 2 * S5_N)),
                           full((S5_W, 2 * S5_N)), full((1, S5_W)), full((S5_W, S5_W)), full((1, S5_W)),
                           pl.BlockSpec((t, S5_W), rev(1))],
                 out_specs=[pl.BlockSpec((t, S5_W), rev(0)), full((8, S5_N)), full((S5_W, 2 * S5_N)),
                            full((S5_W, 2 * S5_N)), full((1, S5_W)), full((S5_W, S5_W)), full((1, S5_W))],
                 out_shape=[jax.ShapeDtypeStruct((s, S5_W), BF16), jax.ShapeDtypeStruct((8, S5_N), F32),
                            jax.ShapeDtypeStruct((S5_W, 2 * S5_N), F32), jax.ShapeDtypeStruct((S5_W, 2 * S5_N), F32),
                            jax.ShapeDtypeStruct((1, S5_W), F32), jax.ShapeDtypeStruct((S5_W, S5_W), F32),
                            jax.ShapeDtypeStruct((1, S5_W), F32)],
                 scratch_shapes=[pltpu.VMEM((8, 2 * S5_N), F32), pltpu.VMEM((t, 2 * S5_N), F32)],
                 compiler_params=_params("arbitrary"))(p, st, st, arow, bbd, cbd, drow, wg, bg, dmix)


def _cum_steps(s):
    return int(math.ceil(math.log2(s)))


V_BLK = (2 * GM_W + S5_W + 2 * FX_W) // 128


AUG = 2 * HD
BIAS_COL = HD
FQ_COL = HD + 3
PAIR_W = 256


def _split3(f):
    hi = f.astype(BF16).astype(F32)
    r = f - hi
    mid = r.astype(BF16).astype(F32)
    lo = (r - mid).astype(BF16).astype(F32)
    return hi, mid, lo


def fox_prep(p, bf):
    s = p.shape[0]
    ts = min(TS, s)
    scale = HD ** -0.5

    def body(q0_ref, q1_ref, k0_ref, k1_ref, v0_ref, v1_ref, f_ref, bf_ref,
             qa_ref, ka_ref, qat_ref, kat_ref, vt_ref, carry):
        @pl.when(pl.program_id(0) == 0)
        def _():
            carry[...] = jnp.zeros_like(carry)

        lane = _lanes((ts, 128))
        lf = jax.nn.log_sigmoid(f_ref[...] + bf_ref[...])
        acc = jnp.where(lane < FX_H, lf, 0.0)
        rid = _rows((ts, 128))
        for k in range(_cum_steps(ts)):
            d = 1 << k
            acc = acc + jnp.where(rid >= d, pltpu.roll(acc, d, 0), 0.0)
        acc = acc + carry[0:1, :]
        carry[0:1, :] = acc[ts - 1:ts, :]

        low = lane < HD
        for h in range(FX_H):
            blk, pos = divmod(h, 4)
            pair = slice((pos // 2) * 128, (pos // 2) * 128 + 128)
            hi, mid, lo = _split3(acc[:, h:h + 1])
            one = jnp.ones((ts, 1), F32)

            def augment(ref, cols):
                x = ref[:, pair]
                if pos % 2:
                    x = pltpu.roll(x, HD, 1)
                out = jnp.where(low, x, 0.0)
                for j, cval in enumerate(cols):
                    out = jnp.where(lane == HD + j, cval, out)
                return out

            qa = augment((q0_ref, q1_ref)[blk], (one, one, one, hi, mid, lo))
            qa = jnp.where(low, qa * scale, qa)
            ka = augment((k0_ref, k1_ref)[blk], (-hi, -mid, -lo, one, one, one))
            cs = slice(h * AUG, (h + 1) * AUG)
            qa_ref[:, cs] = qa.astype(BF16)
            ka_ref[:, cs] = ka.astype(BF16)
            qat_ref[cs, :] = jnp.transpose(qa).astype(BF16)
            kat_ref[cs, :] = jnp.transpose(ka).astype(BF16)
        for j in range(FX_H // 2):
            vref = (v0_ref, v1_ref)[j // 2]
            vt_ref[j * 128:(j + 1) * 128, :] = jnp.transpose(vref[:, (j % 2) * 128:(j % 2) * 128 + 128]).astype(BF16)

    q_blk = (2 * GM_W + S5_W) // PAIR_W
    col = lambda b: pl.BlockSpec((ts, PAIR_W), lambda i: (i, b))
    wide = FX_H * AUG
    return _call(body, name="fox_prep", grid=(s // ts,),
                 in_specs=[col(q_blk), col(q_blk + 1), col(q_blk + 2), col(q_blk + 3), col(q_blk + 4), col(q_blk + 5),
                           pl.BlockSpec((ts, 128), lambda i: (i, FF_COL // 128)), pl.BlockSpec((1, 128), lambda i: (0, 0))],
                 out_specs=[pl.BlockSpec((ts, wide), lambda i: (i, 0)), pl.BlockSpec((ts, wide), lambda i: (i, 0)),
                            pl.BlockSpec((wide, ts), lambda i: (0, i)), pl.BlockSpec((wide, ts), lambda i: (0, i)),
                            pl.BlockSpec((FX_W, ts), lambda i: (0, i))],
                 out_shape=[jax.ShapeDtypeStruct((s, wide), BF16), jax.ShapeDtypeStruct((s, wide), BF16),
                            jax.ShapeDtypeStruct((wide, s), BF16), jax.ShapeDtypeStruct((wide, s), BF16),
                            jax.ShapeDtypeStruct((FX_W, s), BF16)],
                 scratch_shapes=[pltpu.VMEM((8, 128), F32)],
                 compiler_params=_params("arbitrary"))(p, p, p, p, p, p, p, bf)


def fox_prep_grad(p, bf, dfq, dfk):
    s = p.shape[0]
    ts = min(TS, s)
    ns = s // ts

    def body(f_ref, bf_ref, dfq_ref, dfk_ref, df_ref, dbf_ref, carry):
        @pl.when(pl.program_id(0) == 0)
        def _():
            carry[...] = jnp.zeros_like(carry)
            dbf_ref[...] = jnp.zeros_like(dbf_ref)

        lane = _lanes((ts, 128))
        acc = jnp.zeros((ts, 128), F32)
        for h in range(FX_H):
            c = (h // 2) * 128 + h % 2
            acc = jnp.where(lane == h, dfq_ref[:, c:c + 1] + dfk_ref[:, c:c + 1], acc)
        rid = _rows((ts, 128))
        for k in range(_cum_steps(ts)):
            d = 1 << k
            acc = acc + jnp.where(rid < ts - d, pltpu.roll(acc, ts - d, 0), 0.0)
        acc = acc + carry[0:1, :]
        carry[0:1, :] = acc[0:1, :]
        z = f_ref[...] + bf_ref[...]
        df = jnp.where(lane < FX_H, acc * jax.nn.sigmoid(-z), 0.0)
        df_ref[...] = df.astype(BF16)
        dbf_ref[...] += jnp.sum(df, axis=0, keepdims=True)

    rev = lambda i: (ns - 1 - i, 0)
    return _call(body, name="fox_prep_grad", grid=(ns,),
                 in_specs=[pl.BlockSpec((ts, 128), lambda i: (ns - 1 - i, FF_COL // 128)),
                           pl.BlockSpec((1, 128), lambda i: (0, 0)),
                           pl.BlockSpec((ts, FX_W), rev), pl.BlockSpec((ts, FX_W), rev)],
                 out_specs=[pl.BlockSpec((ts, 128), rev), pl.BlockSpec((1, 128), lambda i: (0, 0))],
                 out_shape=[jax.ShapeDtypeStruct((s, 128), BF16), jax.ShapeDtypeStruct((1, 128), F32)],
                 scratch_shapes=[pltpu.VMEM((8, 128), F32)],
                 compiler_params=_params("arbitrary"))(p, bf, dfq, dfk)


def attn(qat, ka, vt):
    s = ka.shape[0]
    tq = min(TQ_FWD, s)
    nq = s // tq

    nh = ATTN_HEADS

    def body(qat_ref, ka_ref, vt_ref, o_ref, lse_ref):
        qi = pl.program_id(1)
        causal = _rows((tq, tq)) <= _lanes((tq, tq))
        lse_ref[...] = jnp.zeros_like(lse_ref)

        def step(kj, carry, masked):
            off = pl.multiple_of(kj * tq, tq)
            out = []
            for hh in range(nh):
                m, l, acc = carry[hh]
                st = jnp.dot(ka_ref[pl.ds(off, tq), hh * AUG:(hh + 1) * AUG], qat_ref[hh * AUG:(hh + 1) * AUG, :],
                             preferred_element_type=F32)
                if masked:
                    st = jnp.where(causal, st, NEG_INF)
                m_new = jnp.maximum(m, jnp.max(st, axis=0, keepdims=True))
                alpha = jnp.exp(m - m_new)
                pt = jnp.exp(st - m_new)
                v = vt_ref[hh * HD:(hh + 1) * HD, pl.ds(off, tq)]
                out.append((m_new, alpha * l + jnp.sum(pt, axis=0, keepdims=True),
                            alpha * acc + jnp.dot(v, pt.astype(BF16), preferred_element_type=F32)))
            return tuple(out)

        init = tuple((jnp.full((1, tq), NEG_INF, F32), jnp.zeros((1, tq), F32), jnp.zeros((HD, tq), F32))
                     for _ in range(nh))
        carry = lax.fori_loop(0, qi, lambda kj, c: step(kj, c, False), init)
        carry = step(qi, carry, True)
        for hh in range(nh):
            m, l, _ = carry[hh]
            lse_ref[hh // 2, hh % 2:hh % 2 + 1, :] = m + jnp.log(l)
        for j in range(nh // 2):
            pair = jnp.concatenate([carry[2 * j][2] / carry[2 * j][1], carry[2 * j + 1][2] / carry[2 * j + 1][1]], axis=0)
            o_ref[:, j * 128:(j + 1) * 128] = jnp.transpose(pair).astype(BF16)

    return _call(body, name="attn", grid=(FX_H // nh, nq),
                 in_specs=[pl.BlockSpec((nh * AUG, tq), lambda h, i: (h, i)),
                           pl.BlockSpec((s, nh * AUG), lambda h, i: (0, h)),
                           pl.BlockSpec((nh * HD, s), lambda h, i: (h, 0))],
                 out_specs=[pl.BlockSpec((tq, nh * HD), lambda h, i: (i, h)),
                            pl.BlockSpec((nh // 2, 8, tq), lambda h, i: (h, 0, i))],
                 out_shape=[jax.ShapeDtypeStruct((s, FX_W), BF16), jax.ShapeDtypeStruct((FX_H // 2, 8, s), F32)],
                 compiler_params=_params("arbitrary", "arbitrary"))(qat, ka, vt)


def attn_grad(qa, qat, ka, kat, p, o, lse, dmix):
    s = qa.shape[0]
    tq = min(TQ, s)
    nq = s // tq
    scale = HD ** -0.5

    def body(qa_ref, qat_ref, ka_ref, kat_ref, v_ref, o_ref, lse_ref, do_ref,
             dq_ref, dk_ref, dv_ref, dfq_ref, dfk_ref, dot_scr, delta, dqt):
        kj = pl.program_id(1)
        lane = _lanes((tq, 128))
        low = lane < HD
        causal = _rows((tq, tq)) <= _lanes((tq, tq))

        @pl.when(kj == 0)
        def _():
            dqt[...] = jnp.zeros_like(dqt)
            delta[...] = jnp.zeros_like(delta)

            def prep(c, _):
                rows = pl.ds(pl.multiple_of(c * tq, tq), tq)
                do = do_ref[rows, :]
                pt = jnp.transpose(do * o_ref[rows, :].astype(F32))
                delta[0:1, rows] = jnp.sum(pt[:HD], axis=0, keepdims=True)
                delta[1:2, rows] = jnp.sum(pt[HD:], axis=0, keepdims=True)
                dot_scr[:, rows] = jnp.transpose(do).astype(BF16)
                return 0

            lax.fori_loop(0, nq, prep, 0)

        v = v_ref[...]
        vms = [jnp.where(low, v, 0.0).astype(BF16), jnp.where(low, 0.0, v).astype(BF16)]

        def tile(qi, carry, masked):
            cols = pl.ds(pl.multiple_of(qi * tq, tq), tq)
            do = do_ref[cols, :].astype(BF16)
            out = []
            for hh in range(2):
                cs = slice(hh * AUG, (hh + 1) * AUG)
                dka, dv = carry[hh]
                st = jnp.dot(ka_ref[:, cs], qat_ref[cs, cols], preferred_element_type=F32)
                if masked:
                    st = jnp.where(causal, st, NEG_INF)
                pt = jnp.exp(st - lse_ref[0, hh:hh + 1, cols])
                dv = dv + jnp.dot(pt.astype(BF16), do, preferred_element_type=F32)
                dpt = jnp.dot(vms[hh], dot_scr[:, cols], preferred_element_type=F32)
                dsb = (pt * (dpt - delta[hh:hh + 1, cols])).astype(BF16)
                dka = dka + jnp.dot(dsb, qa_ref[cols, cs], preferred_element_type=F32)
                dqt[hh, :, cols] += jnp.dot(kat_ref[cs, :], dsb, preferred_element_type=F32)
                out.append((dka, dv))
            return tuple(out)

        init = tuple((jnp.zeros((tq, AUG), F32), jnp.zeros((tq, 128), F32)) for _ in range(2))
        carry = tile(kj, init, True)
        carry = lax.fori_loop(kj + 1, nq, lambda qi, c: tile(qi, c, False), carry)
        dks = [carry[0][0], carry[1][0]]
        dvs = [carry[0][1], carry[1][1]]
        dv_ref[...] = jnp.where(low, dvs[0], dvs[1]).astype(BF16)
        dk_ref[...] = jnp.where(low, dks[0], pltpu.roll(dks[1], HD, 1)).astype(BF16)
        dfk_ref[...] = jnp.where(lane == 0, -dks[0][:, BIAS_COL:BIAS_COL + 1],
                                 jnp.where(lane == 1, -dks[1][:, BIAS_COL:BIAS_COL + 1], 0.0))

        @pl.when(kj == nq - 1)
        def _():
            def finish(c, _):
                rows = pl.ds(pl.multiple_of(c * tq, tq), tq)
                t0 = jnp.transpose(dqt[0, :, rows])
                t1 = jnp.transpose(dqt[1, :, rows])
                dq_ref[rows, :] = (jnp.where(low, t0, pltpu.roll(t1, HD, 1)) * scale).astype(BF16)
                dfq_ref[rows, :] = jnp.where(lane == 0, t0[:, FQ_COL:FQ_COL + 1],
                                             jnp.where(lane == 1, t1[:, FQ_COL:FQ_COL + 1], 0.0))
                return 0

            lax.fori_loop(0, nq, finish, 0)

    seq128 = lambda blk: pl.BlockSpec((s, 128), lambda h, j: (0, blk + h))
    tile128 = pl.BlockSpec((tq, 128), lambda h, j: (j, h))
    out_b = jax.ShapeDtypeStruct((s, FX_W), BF16)
    out_f = jax.ShapeDtypeStruct((s, FX_W), F32)
    return _call(body, name="attn_grad", grid=(FX_H // 2, nq),
                 in_specs=[pl.BlockSpec((s, 2 * AUG), lambda h, j: (0, h)), pl.BlockSpec((2 * AUG, s), lambda h, j: (h, 0)),
                           pl.BlockSpec((tq, 2 * AUG), lambda h, j: (j, h)), pl.BlockSpec((2 * AUG, tq), lambda h, j: (h, j)),
                           pl.BlockSpec((tq, 128), lambda h, j: (j, V_BLK + h)), seq128(0),
                           pl.BlockSpec((1, 8, s), lambda h, j: (h, 0, 0)), seq128(4)],
                 out_specs=[seq128(0), tile128, tile128, seq128(0), tile128],
                 out_shape=[out_b, out_b, out_b, out_f, out_f],
                 scratch_shapes=[pltpu.VMEM((128, s), BF16), pltpu.VMEM((8, s), F32), pltpu.VMEM((2, AUG, s), F32)],
                 compiler_params=_params("arbitrary", "arbitrary"))(qa, qat, ka, kat, p, o, lse, dmix)


def _shift_down(a, prev8, k):
    r = pltpu.roll(a, k, 0)
    top = jnp.where(_rows(prev8.shape) < k, pltpu.roll(prev8, k, 0), r[0:8])
    return jnp.concatenate([top, r[8:]], axis=0)


def _shift_up(a, next8, k):
    t = a.shape[0]
    r = pltpu.roll(a, t - k, 0)
    bot = jnp.where(_rows(next8.shape) >= 8 - k, pltpu.roll(next8, 8 - k, 0), r[t - 8:t])
    return jnp.concatenate([r[:t - 8], bot], axis=0)


def _conv(a, prev8, cw, cb):
    return cb + cw[0:1, :] * _shift_down(a, prev8, 2) + cw[1:2, :] * _shift_down(a, prev8, 1) + cw[2:3, :] * a


GELU_K0 = math.sqrt(2.0 / math.pi)
GELU_K1 = GELU_K0 * 0.044715


def _gelu_parts(c):
    c2 = c * c
    return c2, 0.5 + 0.5 * jnp.tanh(c * (GELU_K0 + GELU_K1 * c2))


def conv_fwd(up, cw, cb):
    s = up.shape[0]
    ts = min(TS_C, s)

    def body(a_ref, g_ref, cw_ref, cb_ref, o_ref, c_ref, halo):
        @pl.when(pl.program_id(0) == 0)
        def _():
            halo[...] = jnp.zeros_like(halo)

        a = a_ref[...]
        c = _conv(a, halo[...], cw_ref[...], cb_ref[...])
        _, h = _gelu_parts(c)
        c_ref[...] = c
        o_ref[...] = (c * h * g_ref[...]).astype(BF16)
        halo[...] = a[ts - 8:ts, :]

    tile = pl.BlockSpec((ts, DFF), lambda i: (i, 0))
    return _call(body, name="conv_fwd", grid=(s // ts,),
                 in_specs=[tile, pl.BlockSpec((ts, DFF), lambda i: (i, 1)),
                           pl.BlockSpec((3, DFF), lambda i: (0, 0)), pl.BlockSpec((1, DFF), lambda i: (0, 0))],
                 out_specs=[tile, tile],
                 out_shape=[jax.ShapeDtypeStruct((s, DFF), BF16), jax.ShapeDtypeStruct((s, DFF), F32)],
                 scratch_shapes=[pltpu.VMEM((8, DFF), F32)],
                 compiler_params=_params("arbitrary"))(up, up, cw, cb)


def conv_bwd(up, c, cw, dact):
    s = up.shape[0]
    ts = min(TS_C, s)
    ns = s // ts

    def body(a_ref, g_ref, c_ref, cw_ref, dact_ref, dup_ref, dcw_ref, dcb_ref, halo):
        @pl.when(pl.program_id(0) == 0)
        def _():
            halo[...] = jnp.zeros_like(halo)
            dcw_ref[...] = jnp.zeros_like(dcw_ref)
            dcb_ref[...] = jnp.zeros_like(dcb_ref)

        a = a_ref[...]
        cw = cw_ref[...]
        cv = c_ref[...]
        dact = dact_ref[...]
        c2, h = _gelu_parts(cv)
        dup_ref[:, DFF:] = (dact * (cv * h)).astype(BF16)
        dgel = h + cv * (2.0 * h * (1.0 - h)) * (GELU_K0 + 3.0 * GELU_K1 * c2)
        dc = dact * g_ref[...] * dgel
        up1 = _shift_up(dc, halo[...], 1)
        up2 = _shift_up(dc, halo[...], 2)
        dup_ref[:, :DFF] = (cw[2:3, :] * dc + cw[1:2, :] * up1 + cw[0:1, :] * up2).astype(BF16)
        dcw_ref[0:1, :] += jnp.sum(a * up2, axis=0, keepdims=True)
        dcw_ref[1:2, :] += jnp.sum(a * up1, axis=0, keepdims=True)
        dcw_ref[2:3, :] += jnp.sum(a * dc, axis=0, keepdims=True)
        dcb_ref[...] += jnp.sum(dc, axis=0, keepdims=True)
        halo[...] = dc[0:8, :]

    rev = lambda col: (lambda i: (ns - 1 - i, col))
    return _call(body, name="conv_bwd", grid=(ns,),
                 in_specs=[pl.BlockSpec((ts, DFF), rev(0)), pl.BlockSpec((ts, DFF), rev(1)),
                           pl.BlockSpec((ts, DFF), rev(0)), pl.BlockSpec((3, DFF), lambda i: (0, 0)),
                           pl.BlockSpec((ts, DFF), rev(0))],
                 out_specs=[pl.BlockSpec((ts, 2 * DFF), rev(0)), pl.BlockSpec((3, DFF), lambda i: (0, 0)),
                            pl.BlockSpec((1, DFF), lambda i: (0, 0))],
                 out_shape=[jax.ShapeDtypeStruct((s, 2 * DFF), BF16), jax.ShapeDtypeStruct((3, DFF), F32),
                            jax.ShapeDtypeStruct((1, DFF), F32)],
                 scratch_shapes=[pltpu.VMEM((8, DFF), F32)],
                 compiler_params=_params("arbitrary"))(up, up, c, cw, dact)


def _blockdiag_expand(m):
    m4 = m.reshape(S5_H, 2, S5_G, S5_P)
    eye = jnp.eye(S5_G, dtype=bool)[:, None, None, :, None]
    return jnp.where(eye, m4[None], 0.0).reshape(S5_W, 2 * S5_N)


def _blockdiag_extract(mbd):
    m5 = mbd.reshape(S5_G, S5_H, 2, S5_G, S5_P)
    diag = jnp.stack([m5[g, :, :, g, :] for g in range(S5_G)], axis=2)
    return diag.reshape(S5_H, 2 * S5_N)


def _c_expand(c_re, c_im):
    c4 = jnp.stack([c_re, c_im], axis=2)
    eye = jnp.eye(S5_G, dtype=bool)[:, None, None, :, None]
    return jnp.where(eye, c4[:, :, :, None, :], 0.0).reshape(S5_W, 2 * S5_N)


def _c_extract(cbd):
    m5 = cbd.reshape(S5_G, S5_H, 2, S5_G, S5_P)
    d = jnp.stack([m5[g, :, :, g, :] for g in range(S5_G)], axis=0)
    return d[:, :, 0, :], d[:, :, 1, :]


def _glu_expand(w):
    eye = jnp.eye(S5_G, dtype=bool)[:, None, :, None]
    return jnp.where(eye, w[:, :, None, :], 0.0).reshape(S5_W, S5_W)


def _glu_extract(wbd):
    m4 = wbd.reshape(S5_G, S5_H, S5_G, S5_H)
    return jnp.stack([m4[g, :, g, :] for g in range(S5_G)], axis=0)


SMALL = ("b_f", "gm_ln_g", "gm_ln_b", "gm_w_s", "gm_b_s", "s5_lam_re", "s5_lam_im", "s5_log_dt", "s5_b_re", "s5_b_im",
         "s5_c_re", "s5_c_im", "s5_d", "s5_w_glu", "s5_b_glu", "ln1_g", "ln1_b", "conv_b", "ln2_g", "ln2_b")


def _layer_operands(sp, l):
    f = {}
    f["bf"] = jnp.pad(sp["b_f"][l][None, :], ((0, 0), (0, 128 - FX_H)))
    f["gm_lg"] = sp["gm_ln_g"][l].reshape(1, GM_W)
    f["gm_lb"] = sp["gm_ln_b"][l].reshape(1, GM_W)
    f["gm_ws"] = sp["gm_w_s"][l]
    f["gm_bst"] = sp["gm_b_s"][l].T
    f["lr"] = sp["s5_lam_re"][l].reshape(1, S5_N)
    f["li"] = sp["s5_lam_im"][l].reshape(1, S5_N)
    f["ldt"] = jnp.repeat(sp["s5_log_dt"][l], S5_P).reshape(1, S5_N)
    bt = lambda b: jnp.transpose(b, (2, 0, 1)).reshape(S5_H, S5_N)
    f["bt"] = jnp.concatenate([bt(sp["s5_b_re"][l]), bt(sp["s5_b_im"][l])], axis=1)
    f["cbd"] = _c_expand(sp["s5_c_re"][l], sp["s5_c_im"][l])
    f["drow"] = sp["s5_d"][l].reshape(1, S5_W)
    f["wg"] = _glu_expand(sp["s5_w_glu"][l])
    f["bg"] = sp["s5_b_glu"][l].reshape(1, S5_W)
    for n in ("ln1_g", "ln1_b", "ln2_g", "ln2_b"):
        f[n] = sp[n][l][None, :]
    f["cb"] = sp["conv_b"][l][None, :]
    return f


def layer_fwd_mix(x, mod, w_in, f):
    p = mm_nn(x, w_in, NP, "in_proj", mod=mod, rows=(0, 1), wt=True)
    ygm = gm_fwd(p, f["gm_lg"], f["gm_lb"], f["gm_ws"], f["gm_bst"])
    arow, bbt = s5_prep_fwd(f["lr"], f["li"], f["ldt"], f["bt"])
    bbd = _blockdiag_expand(bbt)
    ys5, st = s5_fwd(p, arow, bbd, f["cbd"], f["drow"], f["wg"], f["bg"])
    qa, ka, qat, kat, vt = fox_prep(p, f["bf"])
    yfx, lse = attn(qat, ka, vt)
    mixcat = jnp.concatenate([ygm, ys5, yfx], axis=1)
    return mixcat, dict(f=f, x=x, p=p, arow=arow, bbd=bbd, st=st, qa=qa, ka=ka, qat=qat, kat=kat, yfx=yfx, lse=lse,
                        mixcat=mixcat)


def layer_fwd_rest(x, mixcat, mod, w, f, saved):
    mix = mm_nn(mixcat, w["w_out"], D, "out_proj")
    x1 = post_fwd(x, mix, mod, 2, f["ln1_g"], f["ln1_b"], "post1_fwd")
    up = mm_nn(x1, w["w_up"], DFF // 2, "up_proj", mod=mod, rows=(3, 4), wt=True)
    act, conv = conv_fwd(up, w["conv_w"], f["cb"])
    ffn = mm_nn(act, w["w_down"], D, "down_proj")
    x2 = post_fwd(x1, ffn, mod, 5, f["ln2_g"], f["ln2_b"], "post2_fwd")
    return x2, dict(saved, mix=mix, x1=x1, up=up, conv=conv, act=act, ffn=ffn)


def layer_fwd(x, mod, w, f):
    mixcat, saved = layer_fwd_mix(x, mod, w["w_in"], f)
    return layer_fwd_rest(x, mixcat, mod, w, f, saved)


def layer_bwd_ffn(dx, sv, mod, w):
    f = sv["f"]
    dx1, dffn, dg2, dlg2, dlb2 = post_bwd(sv["x1"], sv["ffn"], mod, 5, f["ln2_g"], f["ln2_b"], dx, "post2_bwd")
    g_down = mm_tn(sv["act"], dffn, D // 2, "down_dw")
    dact = mm_nt(dffn, w["w_down"], "down_dx")
    dup, dcw, dcb = conv_bwd(sv["up"], sv["conv"], w["conv_w"], dact)
    g_up = mm_tn(dup, sv["x1"], D // 2, "up_dw", mod=mod, rows=(3, 4))
    dx1, dsh2, dsc2 = mm_nt_mod(dup, w["w_up"], sv["x1"], dx1, mod, (3, 4), "up_dx")
    return dx1, dict(w_up=g_up, w_down=g_down, conv_w=dcw), dict(dsh2=dsh2, dsc2=dsc2, dg2=dg2, conv_b=dcb[0],
                                                                 ln2_g=dlg2[0], ln2_b=dlb2[0])


def layer_bwd_mix(dx1, sv, mod, w, part):
    f = sv["f"]
    dx0, dmix, dg1, dlg1, dlb1 = post_bwd(sv["x"], sv["mix"], mod, 2, f["ln1_g"], f["ln1_b"], dx1, "post1_bwd")
    g_out = mm_tn(sv["mixcat"], dmix, D, "out_dw")
    dmc = mm_nt(dmix, w["w_out"], "out_dx")
    duv, dgm_lg, dgm_lb, dgm_ws, dgm_bst = gm_bwd(sv["p"], f["gm_lg"], f["gm_lb"], f["gm_ws"], f["gm_bst"], dmc)
    du5, da, dbbd, dcbd, dd5, dwg, dbg = s5_bwd(sv["p"], sv["st"], sv["arow"], sv["bbd"], f["cbd"], f["drow"],
                                                f["wg"], f["bg"], dmc)
    dlr, dli, dldt, dbt = s5_prep_bwd(f["lr"], f["li"], f["ldt"], f["bt"], da, _blockdiag_extract(dbbd))
    dq, dk, dv, dfq, dfk = attn_grad(sv["qa"], sv["qat"], sv["ka"], sv["kat"], sv["p"], sv["yfx"], sv["lse"], dmc)
    dff, dbf = fox_prep_grad(sv["p"], f["bf"], dfq, dfk)
    dp = jnp.concatenate([duv, du5, dq, dk, dv, dff], axis=1)
    g_in = mm_tn(dp, sv["x"], D, "in_dw", mod=mod, rows=(0, 1))
    dx, dsh1, dsc1 = mm_nt_mod(dp, w["w_in"], sv["x"], dx0, mod, (0, 1), "in_dx")

    dmod = jnp.concatenate([dsh1, dsc1, dg1, part["dsh2"], part["dsc2"], part["dg2"]], axis=0)
    dc_re, dc_im = _c_extract(dcbd)
    dbt4 = dbt.reshape(S5_H, 2, S5_G, S5_P)
    vals = dict(b_f=dbf[0, :FX_H], gm_ln_g=dgm_lg.reshape(GM_H, HD), gm_ln_b=dgm_lb.reshape(GM_H, HD),
                gm_w_s=dgm_ws, gm_b_s=dgm_bst.T, s5_lam_re=dlr.reshape(S5_G, S5_P),
                s5_lam_im=dli.reshape(S5_G, S5_P), s5_log_dt=dldt[0, :S5_G],
                s5_b_re=jnp.transpose(dbt4[:, 0], (1, 2, 0)), s5_b_im=jnp.transpose(dbt4[:, 1], (1, 2, 0)),
                s5_c_re=dc_re, s5_c_im=dc_im, s5_d=dd5.reshape(S5_G, S5_H), s5_w_glu=_glu_extract(dwg),
                s5_b_glu=dbg.reshape(S5_G, S5_H), ln1_g=dlg1[0], ln1_b=dlb1[0], conv_b=part["conv_b"],
                ln2_g=part["ln2_g"], ln2_b=part["ln2_b"])
    return dx, dict(w_in=g_in, w_out=g_out), vals, dmod


def layer_bwd(dx, sv, mod, w):
    dx1, g_ffn, part = layer_bwd_ffn(dx, sv, mod, w)
    dx, g_mix, vals, dmod = layer_bwd_mix(dx1, sv, mod, w, part)
    return dx, dict(g_ffn, **g_mix), vals, dmod


def local_step(x, target, mods, big, sp):
    saved = []
    for l in range(DEPTH):
        x, sv = layer_fwd(x, mods[l], big[l], _layer_operands(sp, l))
        saved.append(sv)
    loss_tile, dx = loss_kernel(x, target)
    gbig, vals, dmods = [None] * DEPTH, [None] * DEPTH, [None] * DEPTH
    for l in reversed(range(DEPTH)):
        dx, gbig[l], vals[l], dmods[l] = layer_bwd(dx, saved[l], mods[l], big[l])
    gsm = {n: jnp.stack([v[n] for v in vals]) for n in SMALL}
    return loss_tile, dx, gbig, gsm, jnp.stack(dmods)


def _my_index():
    return 4 * lax.axis_index("x") + 2 * lax.axis_index("y") + lax.axis_index("c")


def exchange(tensors, scatter, name):
    n = len(tensors)

    def body(*refs):
        ins, outs = refs[:n], refs[n:2 * n]
        send_sems, recv_sems, local_sems = refs[2 * n:]
        x, y, c = lax.axis_index("x"), lax.axis_index("y"), lax.axis_index("c")
        me = 4 * x + 2 * y + c
        local = []
        for t in range(n):
            cp = pltpu.make_async_copy(ins[t].at[me] if scatter else ins[t], outs[t].at[me], local_sems.at[t])
            cp.start()
            local.append(cp)
        remote = []
        for m in range(1, NDEV):
            px = 1 - x if m & 4 else x
            py = 1 - y if m & 2 else y
            pc = 1 - c if m & 1 else c
            peer = 4 * px + 2 * py + pc
            for t in range(n):
                k = t * (NDEV - 1) + m - 1
                cp = pltpu.make_async_remote_copy(
                    src_ref=ins[t].at[peer] if scatter else ins[t], dst_ref=outs[t].at[me],
                    send_sem=send_sems.at[k], recv_sem=recv_sems.at[k],
                    device_id=(px, py, pc), device_id_type=MESH_IDS)
                cp.start()
                remote.append(cp)
        for cp in remote:
            cp.wait()
        for cp in local:
            cp.wait()

    hbm = pl.BlockSpec(memory_space=pltpu.HBM)
    out_shape = [jax.ShapeDtypeStruct(t.shape if scatter else (NDEV,) + t.shape, t.dtype) for t in tensors]
    return _call(body, name=name, in_specs=[hbm] * n, out_specs=[hbm] * n, out_shape=out_shape,
                 scratch_shapes=[pltpu.SemaphoreType.DMA((n * (NDEV - 1),)), pltpu.SemaphoreType.DMA((n * (NDEV - 1),)),
                                 pltpu.SemaphoreType.DMA((n,))])(*tensors)


def _peers():
    x, y, c = lax.axis_index("x"), lax.axis_index("y"), lax.axis_index("c")
    out = []
    for m in range(1, NDEV):
        px = 1 - x if m & 4 else x
        py = 1 - y if m & 2 else y
        pc = 1 - c if m & 1 else c
        out.append(((px, py, pc), 4 * px + 2 * py + pc))
    return 4 * x + 2 * y + c, out


def _split_copies(v_refs, land_refs, send_sems, recv_sems, scatter):
    me, peers = _peers()
    return [pltpu.make_async_remote_copy(
        src_ref=v_ref.at[idx] if scatter else v_ref, dst_ref=land_ref.at[me],
        send_sem=send_sems.at[t * (NDEV - 1) + k], recv_sem=recv_sems.at[t * (NDEV - 1) + k],
        device_id=pos, device_id_type=MESH_IDS)
        for t, (v_ref, land_ref) in enumerate(zip(v_refs, land_refs)) for k, (pos, idx) in enumerate(peers)]


_HBM_SPEC = pl.BlockSpec(memory_space=pltpu.HBM)
_SEM_SPEC = pl.BlockSpec(memory_space=pltpu.SEMAPHORE)
_SPLIT_EFFECT = pltpu.SideEffectType.DATAFLOW_SIDE_EFFECTING


def exchange_start(tensors, scatter, name):
    n = len(tensors)
    land_shapes = [t.shape if scatter else (NDEV,) + t.shape for t in tensors]

    def body(*refs):
        v_refs, land_refs = refs[:n], refs[n:2 * n]
        send_sems, recv_sems = refs[2 * n], refs[2 * n + 1]
        token = refs[-1]
        for cp in _split_copies(v_refs, land_refs, send_sems, recv_sems, scatter):
            cp.start()
        token[...] = jnp.zeros_like(token)

    sems = pltpu.SemaphoreType.DMA((n * (NDEV - 1),))
    out = _call(
        body, name=name,
        out_shape=(sems, sems, *[pltpu.HBM(t.shape, t.dtype) for t in tensors],
                   *[pltpu.HBM(s, t.dtype) for s, t in zip(land_shapes, tensors)], jax.ShapeDtypeStruct((8, 128), F32)),
        in_specs=(_HBM_SPEC,) * (2 * n),
        out_specs=(_SEM_SPEC, _SEM_SPEC) + (_HBM_SPEC,) * (2 * n) + (pl.BlockSpec(memory_space=pltpu.VMEM),),
        input_output_aliases={i: i + 2 for i in range(2 * n)},
        compiler_params=pltpu.CompilerParams(has_side_effects=_SPLIT_EFFECT),
    )(*[pltpu.with_memory_space_constraint(t, pltpu.HBM) for t in tensors],
      *[pltpu.with_memory_space_constraint(lax.empty(s, t.dtype), pltpu.HBM) for s, t in zip(land_shapes, tensors)])
    return out[0], out[1], list(out[2:2 + n]), list(out[2 + n:2 + 2 * n]), out[-1]


def exchange_wait(started, after, scatter, name):
    send_sems, recv_sems, v_thru, land_thru, _ = started
    n = len(v_thru)

    def body(*refs):
        v_refs, land_refs = refs[:n], refs[n:2 * n]
        for cp in _split_copies(v_refs, land_refs, refs[2 * n], refs[2 * n + 1], scatter):
            cp.wait_send()
            cp.wait_recv()

    out = _call(
        body, name=name,
        out_shape=tuple(pltpu.HBM(t.shape, t.dtype) for t in v_thru + land_thru),
        in_specs=(_HBM_SPEC,) * (2 * n) + (_SEM_SPEC, _SEM_SPEC, pl.BlockSpec(memory_space=pl.ANY)),
        out_specs=(_HBM_SPEC,) * (2 * n), input_output_aliases={i: i for i in range(2 * n)},
        compiler_params=pltpu.CompilerParams(has_side_effects=_SPLIT_EFFECT),
    )(*v_thru, *land_thru, send_sems, recv_sems, after)
    return list(out[:n]), list(out[n:])


def mod_slices(c_all, w_ada, b_loc):
    nl, _, nc = w_ada.shape

    def body(c_ref, w_ref, b_ref, o_ref):
        cv = c_ref[...]
        o_ref[0] = _nn(cv * jax.nn.sigmoid(cv), w_ref[0]) + b_ref[0]

    return _call(body, name="mod_slices", grid=(nl,),
                 in_specs=[pl.BlockSpec((NDEV, D), lambda l: (0, 0)), pl.BlockSpec((1, D, nc), lambda l: (l, 0, 0)),
                           pl.BlockSpec((1, 1, nc), lambda l: (l, 0, 0))],
                 out_specs=pl.BlockSpec((1, NDEV, nc), lambda l: (l, 0, 0)),
                 out_shape=jax.ShapeDtypeStruct((nl, NDEV, nc), F32),
                 compiler_params=_params("arbitrary"))(c_all, w_ada, b_loc.reshape(nl, 1, nc))


def ada_grad(c_all, dm_loc):
    nl, _, nc = dm_loc.shape

    def body(c_ref, d_ref, o_ref):
        cv = c_ref[...]
        o_ref[0] = _tn(cv * jax.nn.sigmoid(cv), d_ref[0])

    return _call(body, name="ada_grad", grid=(nl,),
                 in_specs=[pl.BlockSpec((NDEV, D), lambda l: (0, 0)), pl.BlockSpec((1, NDEV, nc), lambda l: (l, 0, 0))],
                 out_specs=pl.BlockSpec((1, D, nc), lambda l: (l, 0, 0)),
                 out_shape=jax.ShapeDtypeStruct((nl, D, nc), F32),
                 compiler_params=_params("arbitrary"))(c_all, dm_loc)


def sum_chunks(chunks):
    r = chunks.shape[1]

    def body(c_ref, o_ref):
        acc = c_ref[0]
        for i in range(1, NDEV):
            acc = acc + c_ref[i]
        o_ref[...] = acc

    return _call(body, name="sum_chunks", out_shape=jax.ShapeDtypeStruct((r, 128), F32))(chunks)


def _row_tile(r):
    if r <= 256:
        return r
    for t in range(256, 7, -8):
        if r % t == 0:
            return t
    return r


def adamw(w, m, v, g=None, chunks=None, name="adamw"):
    r, cdim = w.shape
    tr = _row_tile(r)
    bc1 = 1.0 - ADAM_B1 ** ADAM_STEP
    bc2 = 1.0 - ADAM_B2 ** ADAM_STEP

    def body(g_ref, w_ref, m_ref, v_ref, go_ref, d_ref, mo_ref, vo_ref):
        if chunks is None:
            grad = g_ref[...]
        else:
            grad = g_ref[0].astype(F32)
            for i in range(1, NDEV):
                grad = grad + g_ref[i].astype(F32)
        mn = ADAM_B1 * m_ref[...] + (1.0 - ADAM_B1) * grad
        vn = ADAM_B2 * v_ref[...] + (1.0 - ADAM_B2) * (grad * grad)
        m_hat = mn / bc1
        v_hat = vn / bc2
        go_ref[...] = grad
        d_ref[...] = -ADAM_LR * (m_hat / (jnp.sqrt(v_hat) + ADAM_EPS) + ADAM_WD * w_ref[...])
        mo_ref[...] = mn
        vo_ref[...] = vn

    tile = pl.BlockSpec((tr, cdim), lambda i: (i, 0))
    gspec = tile if chunks is None else pl.BlockSpec((NDEV, tr, cdim), lambda i: (0, i, 0))
    shp = jax.ShapeDtypeStruct((r, cdim), F32)
    return _call(body, name=name, grid=(r // tr,), in_specs=[gspec, tile, tile, tile],
                 out_specs=[tile] * 4, out_shape=[shp] * 4,
                 compiler_params=_params("arbitrary"))(g if chunks is None else chunks, w, m, v)


def adamw_layers(w, m, v, chunks, name):
    nl, r, cdim = w.shape
    tr = _row_tile(r)
    bc1 = 1.0 - ADAM_B1 ** ADAM_STEP
    bc2 = 1.0 - ADAM_B2 ** ADAM_STEP
    outs = [lax.empty(w.shape, F32) for _ in range(4)]
    for l in range(nl):
        def body(g_ref, w_ref, m_ref, v_ref, p0, p1, p2, p3, go_ref, d_ref, mo_ref, vo_ref):
            grad = g_ref[0].astype(F32)
            for i in range(1, NDEV):
                grad = grad + g_ref[i].astype(F32)
            mn = ADAM_B1 * m_ref[...] + (1.0 - ADAM_B1) * grad
            vn = ADAM_B2 * v_ref[...] + (1.0 - ADAM_B2) * (grad * grad)
            go_ref[...] = grad
            d_ref[...] = -ADAM_LR * ((mn / bc1) / (jnp.sqrt(vn / bc2) + ADAM_EPS) + ADAM_WD * w_ref[...])
            mo_ref[...] = mn
            vo_ref[...] = vn

        tile = pl.BlockSpec((None, tr, cdim), lambda i, l=l: (l, i, 0))
        whole = pl.BlockSpec(memory_space=pl.ANY)
        outs = _call(body, name=f"{name}_{l}", grid=(r // tr,),
                     in_specs=[pl.BlockSpec((NDEV, tr, cdim), lambda i: (0, i, 0)), tile, tile, tile] + [whole] * 4,
                     out_specs=[tile] * 4, out_shape=[jax.ShapeDtypeStruct(w.shape, F32)] * 4,
                     input_output_aliases={4: 0, 5: 1, 6: 2, 7: 3},
                     compiler_params=_params("arbitrary"))(chunks[l], w, m, v, *outs)
    return outs


WEIGHTS = ("w_ada", "b_ada", "w_in", "b_f", "gm_ln_g", "gm_ln_b", "gm_w_s", "gm_b_s", "s5_lam_re", "s5_lam_im",
           "s5_log_dt", "s5_b_re", "s5_b_im", "s5_c_re", "s5_c_im", "s5_d", "s5_w_glu", "s5_b_glu", "w_out", "ln1_g",
           "ln1_b", "w_up", "conv_w", "conv_b", "w_down", "ln2_g", "ln2_b")
LARGE = ("w_in", "w_out", "w_up", "w_down")
TRANSPOSED = ("w_in", "w_up")
PACKED = ("b_ada",) + SMALL
PACK_SEG = 8 * 128


def _gather_cols(g):
    nd, nl, r, c = g.shape
    return jnp.transpose(g, (1, 2, 0, 3)).reshape(nl, r, nd * c)


def _chunk_cols(g):
    nl, r, c8 = g.shape
    return jnp.transpose(g.reshape(nl, r, NDEV, c8 // NDEV), (2, 0, 1, 3))


def _join_rows(g):
    nd, r, c = g.shape
    return g.reshape(nd * r, c)


def _split_rows(g):
    r8, c = g.shape
    return g.reshape(NDEV, r8 // NDEV, c)


def _pack(parts):
    segs = []
    for n in PACKED:
        flat = parts[n].reshape(-1)
        segs.append(jnp.pad(flat, (0, -flat.shape[0] % PACK_SEG)).reshape(-1, 128))
    rows = jnp.concatenate(segs, axis=0)
    return jnp.pad(rows, ((0, -rows.shape[0] % (NDEV * 8)), (0, 0)))


def _unpack(rows, shapes):
    out, off = {}, 0
    for n in PACKED:
        size = math.prod(shapes[n])
        nrows = -(-size // PACK_SEG) * 8
        out[n] = rows[off:off + nrows].reshape(-1)[:size].reshape(shapes[n])
        off += nrows
    return out


def kernel(x, c, w_ada, b_ada, w_in, b_f, gm_ln_g, gm_ln_b, gm_w_s, gm_b_s, s5_lam_re, s5_lam_im, s5_log_dt, s5_b_re, s5_b_im, s5_c_re, s5_c_im, s5_d, s5_w_glu, s5_b_glu, w_out, ln1_g, ln1_b, w_up, conv_w, conv_b, w_down, ln2_g, ln2_b, loss_target, m_w_ada, m_b_ada, m_w_in, m_b_f, m_gm_ln_g, m_gm_ln_b, m_gm_w_s, m_gm_b_s, m_s5_lam_re, m_s5_lam_im, m_s5_log_dt, m_s5_b_re, m_s5_b_im, m_s5_c_re, m_s5_c_im, m_s5_d, m_s5_w_glu, m_s5_b_glu, m_w_out, m_ln1_g, m_ln1_b, m_w_up, m_conv_w, m_conv_b, m_w_down, m_ln2_g, m_ln2_b, v_w_ada, v_b_ada, v_w_in, v_b_f, v_gm_ln_g, v_gm_ln_b, v_gm_w_s, v_gm_b_s, v_s5_lam_re, v_s5_lam_im, v_s5_log_dt, v_s5_b_re, v_s5_b_im, v_s5_c_re, v_s5_c_im, v_s5_d, v_s5_w_glu, v_s5_b_glu, v_w_out, v_ln1_g, v_ln1_b, v_w_up, v_conv_w, v_conv_b, v_w_down, v_ln2_g, v_ln2_b):
    given = dict(locals())
    wts = {n: given[n] for n in WEIGHTS}
    mom = {n: given["m_" + n] for n in WEIGHTS}
    var = {n: given["v_" + n] for n in WEIGHTS}
    nl = w_ada.shape[0]
    me = _my_index()
    ada_cols = w_ada.shape[2]

    (c_all,) = exchange([c], False, "gather_c")
    c_all = c_all.reshape(NDEV, D)
    b_loc = lax.dynamic_slice_in_dim(b_ada, me * ada_cols, ada_cols, axis=1)
    mod_part = mod_slices(c_all, w_ada, b_loc)

    mod_all, conv_all = exchange([mod_part, conv_w], False, "gather_mod")
    mod_mine = lax.dynamic_index_in_dim(mod_all, me, axis=2, keepdims=False)
    mods = jnp.transpose(mod_mine, (1, 0, 2)).reshape(nl, 6, D)
    mods = jnp.pad(mods, ((0, 0), (0, 2), (0, 0)))
    conv_full = _gather_cols(conv_all)
    sp = {n: wts[n] for n in SMALL}
    rowwise = {n: [jnp.swapaxes(a[n], 1, 2) if n in TRANSPOSED else a[n] for a in (wts, mom, var)] for n in LARGE}

    def joined(own, land):
        return _join_rows(lax.dynamic_update_index_in_dim(land, own, me, 0))

    def block(l, names):
        return [rowwise[n][0][l].astype(BF16) for n in names]

    def chunked(grads, names):
        return [_split_rows(grads[n]) for n in names]

    head, tail = LARGE[:1], LARGE[1:]
    got_head = exchange_start(block(0, head), False, "gather_start_0_in")
    rest = block(0, tail)
    rest[0] = rest[0] + got_head[4][0, 0].astype(BF16)
    got_tail = exchange_start(rest, False, "gather_start_0_rest")
    xl, saved, weights = x[0], [], []
    for l in range(nl):
        if l == 0:
            own, land = exchange_wait(got_head, got_tail[4], False, "gather_wait_0_in")
            w = {n: joined(o, g) for n, o, g in zip(head, own, land)}
        else:
            own, land = exchange_wait(started, xl, False, f"gather_wait_{l}")
            w = {n: joined(o, g) for n, o, g in zip(LARGE, own, land)}
        mod_l = mods[l]
        if l + 1 < nl:
            nxt, w["w_in"] = lax.optimization_barrier((block(l + 1, LARGE), w["w_in"]))
            started = exchange_start(nxt, False, f"gather_start_{l + 1}")
            mod_l = mod_l + started[4][0, 0]
        w["w_in"] = jnp.pad(w["w_in"], ((0, NP - D_IN), (0, 0)))
        f = _layer_operands(sp, l)
        mixcat, sv = layer_fwd_mix(xl, mod_l, w["w_in"], f)
        if l == 0:
            own, land = exchange_wait(got_tail, mixcat, False, "gather_wait_0_rest")
            w.update({n: joined(o, g) for n, o, g in zip(tail, own, land)})
        w["conv_w"] = conv_full[l]
        xl, sv = layer_fwd_rest(xl, mixcat, mod_l, w, f, sv)
        weights.append(w)
        saved.append(sv)

    loss_tile, dx = loss_kernel(xl, loss_target[0])

    ffn_names, mix_names = ("w_up", "w_down"), ("w_in", "w_out")
    scattering, vals, dmods, gconv = [None] * nl, [None] * nl, [None] * nl, [None] * nl
    token = jnp.zeros((), F32)
    for l in reversed(range(nl)):
        mod_l = mods[l] + token
        dx1, g_ffn, part = layer_bwd_ffn(dx, saved[l], mod_l, weights[l])
        gconv[l] = g_ffn["conv_w"]
        if l == 0:
            sent_ffn = exchange_start(chunked(g_ffn, ffn_names), True, "scatter_start_0_ffn")
            mod_l = mod_l + sent_ffn[4][0, 0]
        dx, g_mix, vals[l], dmods[l] = layer_bwd_mix(dx1, saved[l], mod_l, weights[l], part)
        g_mix["w_in"] = g_mix["w_in"][:D_IN]
        if l == 0:
            scattering[l] = [(ffn_names, sent_ffn),
                             (mix_names, exchange_start(chunked(g_mix, mix_names), True, "scatter_start_0_mix"))]
        else:
            sent = exchange_start(chunked(dict(g_ffn, **g_mix), LARGE), True, f"scatter_start_{l}")
            scattering[l] = [(LARGE, sent)]
            token = sent[4][0, 0]
    gx = dx
    dmods = jnp.stack(dmods)
    gsm = {n: jnp.stack([v[n] for v in vals]) for n in SMALL}

    gsm["b_ada"] = dmods.reshape(nl, 6 * D)
    packed = _pack(gsm).reshape(NDEV, -1, 128)
    conv_recv, small_recv = exchange([_chunk_cols(jnp.stack(gconv)), packed], True, "scatter_small")
    small_sum = sum_chunks(small_recv)
    small_all, dmod_all = exchange([small_sum, dmods.reshape(nl, 6 * D)], False, "gather_small")

    received = [dict() for _ in range(nl)]

    def arrive(l, k, after):
        names, sent = scattering[l][k]
        own, land = exchange_wait(sent, after, True, f"scatter_wait_{l}_{k}")
        for n, o, g in zip(names, own, land):
            mine = lax.dynamic_index_in_dim(o, me, 0, keepdims=False)
            received[l][n] = lax.dynamic_update_index_in_dim(g, mine, me, 0)
        return land[0]

    after = small_all
    for l in reversed(range(nl)):
        after = arrive(l, 0, after)
    out = {}
    def update(n):
        res = adamw_layers(*rowwise[n], [received[l][n] for l in range(nl)], "adamw_" + n)
        return [jnp.swapaxes(r, 1, 2) for r in res] if n in TRANSPOSED else res

    for n in ffn_names:
        out[n] = update(n)
    arrive(0, 1, out[ffn_names[-1]][0])
    for n in mix_names:
        out[n] = update(n)
    shp = conv_w.shape
    two_d = lambda a: a.reshape(shp[0] * shp[1], shp[2])
    res = adamw(two_d(conv_w), two_d(m_conv_w), two_d(v_conv_w),
                chunks=conv_recv.reshape(NDEV, shp[0] * shp[1], shp[2]), name="adamw_conv_w")
    out["conv_w"] = [r.reshape(shp) for r in res]

    dm_loc = lax.dynamic_slice_in_dim(dmod_all, me * ada_cols, ada_cols, axis=2)
    g_ada = ada_grad(c_all, jnp.transpose(dm_loc, (1, 0, 2)))
    two_d = lambda a: a.reshape(nl * D, ada_cols)
    res = adamw(two_d(w_ada), two_d(m_w_ada), two_d(v_w_ada), g=two_d(g_ada), name="adamw_w_ada")
    out["w_ada"] = [r.reshape(w_ada.shape) for r in res]

    shapes = {n: wts[n].shape for n in PACKED}
    res = adamw(_pack(wts), _pack(mom), _pack(var), g=small_all.reshape(-1, 128), name="adamw_small")
    unpacked = [_unpack(r, shapes) for r in res]
    for n in PACKED:
        out[n] = [u[n] for u in unpacked]

    loss = lax.psum(loss_tile[0, 0], ("x", "y", "c"))
    return (loss, gx[None], *[out[n][0] for n in WEIGHTS], *[out[n][1] for n in WEIGHTS],
            *[out[n][2] for n in WEIGHTS], *[out[n][3] for n in WEIGHTS])
```

```python
import math

import jax
import jax.numpy as jnp
from jax import lax
from jax.experimental import pallas as pl
from jax.experimental.pallas import tpu as pltpu

F32 = jnp.float32
BF16 = jnp.bfloat16
MESH_IDS = pl.DeviceIdType.MESH

D = 1024
SEQ = 4096
DEPTH = 4
NDEV = 8
HD = 64
GM_W = 256
GM_H = 4
GM_C = 128
S5_W = 256
S5_G = 16
S5_H = 16
S5_P = 64
S5_N = S5_G * S5_P
FX_W = 512
FX_H = 8
D_IN = 2 * GM_W + S5_W + 3 * FX_W + FX_H
NP = 2432
FF_COL = 2304
DFF = 2816
LN_EPS = 1e-5
DN_ALPHA = (2.0 * DEPTH) ** 0.25
NEG_INF = -1e30
ADAM_LR = 0.001
ADAM_B1 = 0.9
ADAM_B2 = 0.999
ADAM_EPS = 1e-08
ADAM_WD = 0.01
ADAM_STEP = 10

V7X_VMEM_LIMIT = 56 * 1024 * 1024
TS = 512
TS_C = 256
T_S5 = 512
TQ = 512
TQ_FWD = 1024
ATTN_HEADS = 2


def _call(body, **kw):
    return pl.pallas_call(body, **kw)


def _params(*sem):
    return pltpu.CompilerParams(dimension_semantics=sem if sem else None,
                                vmem_limit_bytes=V7X_VMEM_LIMIT)


def _nn(a, b):
    return jnp.dot(a.astype(BF16), b.astype(BF16), preferred_element_type=F32)


def _nt(a, b):
    return lax.dot_general(a.astype(BF16), b.astype(BF16), (((1,), (1,)), ((), ())),
                           preferred_element_type=F32)


def _tn(a, b):
    return lax.dot_general(a.astype(BF16), b.astype(BF16), (((0,), (0,)), ((), ())),
                           preferred_element_type=F32)


@jax.custom_vjp
def _bdot(a, b):
    return _nn(a, b)


def _bdot_fwd(a, b):
    return _nn(a, b), (a, b)


def _bdot_bwd(res, g):
    a, b = res
    return _nt(g, b), _tn(a, g)


_bdot.defvjp(_bdot_fwd, _bdot_bwd)


@jax.custom_vjp
def _bdot_nt(a, b):
    return _nt(a, b)


def _bdot_nt_fwd(a, b):
    return _nt(a, b), (a, b)


def _bdot_nt_bwd(res, g):
    a, b = res
    return _nn(g, b), _tn(g, a)


_bdot_nt.defvjp(_bdot_nt_fwd, _bdot_nt_bwd)


def _ln(r, g, b):
    mu = jnp.mean(r, axis=-1, keepdims=True)
    xc = r - mu
    var = jnp.mean(xc * xc, axis=-1, keepdims=True)
    return xc * lax.rsqrt(var + LN_EPS) * g + b


def _rows(shape):
    return lax.broadcasted_iota(jnp.int32, shape, 0)


def _lanes(shape):
    return lax.broadcasted_iota(jnp.int32, shape, 1)


def mm_mod(a, wt, tn, name, mod, rows):
    s, k = a.shape
    n = wt.shape[0]
    ts = min(TS, s)

    def body(a_ref, m_ref, w_ref, o_ref):
        h = a_ref[...] * (1.0 + m_ref[rows[1]:rows[1] + 1, :]) + m_ref[rows[0]:rows[0] + 1, :]
        o_ref[...] = _nt(h, w_ref[...])

    return _call(body, name=name, grid=(n // tn, s // ts),
                 in_specs=[pl.BlockSpec((ts, k), lambda j, i: (i, 0)), pl.BlockSpec((8, k), lambda j, i: (0, 0)),
                           pl.BlockSpec((tn, k), lambda j, i: (j, 0))],
                 out_specs=pl.BlockSpec((ts, tn), lambda j, i: (i, j)),
                 out_shape=jax.ShapeDtypeStruct((s, n), F32),
                 compiler_params=_params("arbitrary", "arbitrary"))(a, mod, wt)


def mm_nt(dy, w, name):
    s, n = dy.shape
    k = w.shape[0]
    ts = min(TS, s)

    def body(dy_ref, w_ref, o_ref):
        o_ref[...] = _nt(dy_ref[...], w_ref[...])

    return _call(body, name=name, grid=(s // ts,),
                 in_specs=[pl.BlockSpec((ts, n), lambda i: (i, 0)),
                           pl.BlockSpec((k, n), lambda i: (0, 0))],
                 out_specs=pl.BlockSpec((ts, k), lambda i: (i, 0)),
                 out_shape=jax.ShapeDtypeStruct((s, k), F32),
                 compiler_params=_params("arbitrary"))(dy, w)


def mm_nt_mod(dy, w, x, dres, mod, rows, name):
    s, n = dy.shape
    k = w.shape[1]
    ts = min(TS, s)

    def body(dy_ref, w_ref, x_ref, r_ref, m_ref, dx_ref, dsh_ref, dsc_ref):
        @pl.when(pl.program_id(0) == 0)
        def _():
            dsh_ref[...] = jnp.zeros_like(dsh_ref)
            dsc_ref[...] = jnp.zeros_like(dsc_ref)

        dh = _nn(dy_ref[...], w_ref[...])
        dx_ref[...] = r_ref[...] + dh * (1.0 + m_ref[rows[1]:rows[1] + 1, :])
        dsh_ref[...] += jnp.sum(dh, axis=0, keepdims=True)
        dsc_ref[...] += jnp.sum(dh * x_ref[...], axis=0, keepdims=True)

    row = pl.BlockSpec((1, k), lambda i: (0, 0))
    tile = pl.BlockSpec((ts, k), lambda i: (i, 0))
    return _call(body, name=name, grid=(s // ts,),
                 in_specs=[pl.BlockSpec((ts, n), lambda i: (i, 0)),
                           pl.BlockSpec((n, k), lambda i: (0, 0)), tile, tile,
                           pl.BlockSpec((8, k), lambda i: (0, 0))],
                 out_specs=[tile, row, row],
                 out_shape=[jax.ShapeDtypeStruct((s, k), F32),
                            jax.ShapeDtypeStruct((1, k), F32),
                            jax.ShapeDtypeStruct((1, k), F32)],
                 compiler_params=_params("arbitrary"))(dy, w, x, dres, mod)


def mm_tn(a, dy, tn, name, mod=None, rows=None):
    s, k = a.shape
    n = dy.shape[1]
    ts = min(TS, s)
    ns = s // ts

    def body(*refs):
        if mod is None:
            a_ref, dy_ref, o_ref, acc = refs
            h = dy_ref[...]
        else:
            a_ref, dy_ref, m_ref, o_ref, acc = refs
            h = dy_ref[...] * (1.0 + m_ref[rows[1]:rows[1] + 1, :]) + m_ref[rows[0]:rows[0] + 1, :]
        i = pl.program_id(1)

        @pl.when(i == 0)
        def _():
            acc[...] = jnp.zeros_like(acc)

        acc[...] += _tn(a_ref[...], h)

        @pl.when(i == ns - 1)
        def _():
            o_ref[...] = acc[...].astype(BF16)

    in_specs = [pl.BlockSpec((ts, k), lambda j, i: (i, 0)), pl.BlockSpec((ts, tn), lambda j, i: (i, j))]
    args = [a, dy]
    if mod is not None:
        in_specs.append(pl.BlockSpec((8, tn), lambda j, i: (0, j)))
        args.append(mod)
    return _call(body, name=name, grid=(n // tn, s // ts), in_specs=in_specs,
                 out_specs=pl.BlockSpec((k, tn), lambda j, i: (0, j)),
                 out_shape=jax.ShapeDtypeStruct((k, n), BF16),
                 scratch_shapes=[pltpu.VMEM((k, tn), F32)],
                 compiler_params=_params("arbitrary", "arbitrary"))(*args)


def _post_fn(x, br, gate, lg, lb):
    return _ln(DN_ALPHA * x + (1.0 + gate) * br, lg, lb)


def proj_post(a, w, x, mod, grow, lg, lb, name):
    s, k = a.shape
    ts = min(TS, s)

    def body(a_ref, w_ref, x_ref, m_ref, lg_ref, lb_ref, b_ref, o_ref):
        br = jnp.dot(a_ref[...], w_ref[...], preferred_element_type=F32)
        b_ref[...] = br
        o_ref[...] = _post_fn(x_ref[...], br, m_ref[grow:grow + 1, :], lg_ref[...], lb_ref[...])

    tile = pl.BlockSpec((ts, D), lambda i: (i, 0))
    row = pl.BlockSpec((1, D), lambda i: (0, 0))
    out = jax.ShapeDtypeStruct((s, D), F32)
    return _call(body, name=name, grid=(s // ts,),
                 in_specs=[pl.BlockSpec((ts, k), lambda i: (i, 0)), pl.BlockSpec((k, D), lambda i: (0, 0)), tile,
                           pl.BlockSpec((8, D), lambda i: (0, 0)), row, row],
                 out_specs=[tile, tile], out_shape=[out, out],
                 compiler_params=_params("arbitrary"))(a, w, x, mod, lg, lb)


def post_bwd(x, br, mod, grow, lg, lb, dy, name):
    s = x.shape[0]
    ts = min(TS, s)

    def body(x_ref, b_ref, m_ref, lg_ref, lb_ref, dy_ref, dx_ref, db_ref, dg_ref, dlg_ref, dlb_ref):
        @pl.when(pl.program_id(0) == 0)
        def _():
            dg_ref[...] = jnp.zeros_like(dg_ref)
            dlg_ref[...] = jnp.zeros_like(dlg_ref)
            dlb_ref[...] = jnp.zeros_like(dlb_ref)

        _, vjp = jax.vjp(_post_fn, x_ref[...], b_ref[...], m_ref[grow:grow + 1, :], lg_ref[...], lb_ref[...])
        dx, db, dg, dlg, dlb = vjp(dy_ref[...])
        dx_ref[...] = dx
        db_ref[...] = db.astype(BF16)
        dg_ref[...] += dg
        dlg_ref[...] += dlg
        dlb_ref[...] += dlb

    tile = pl.BlockSpec((ts, D), lambda i: (i, 0))
    row = pl.BlockSpec((1, D), lambda i: (0, 0))
    rs = jax.ShapeDtypeStruct((1, D), F32)
    return _call(body, name=name, grid=(s // ts,),
                 in_specs=[tile, tile, pl.BlockSpec((8, D), lambda i: (0, 0)), row, row, tile],
                 out_specs=[tile, tile, row, row, row],
                 out_shape=[jax.ShapeDtypeStruct((s, D), F32), jax.ShapeDtypeStruct((s, D), BF16), rs, rs, rs],
                 compiler_params=_params("arbitrary"))(x, br, mod, lg, lb, dy)


def loss_kernel(y, target):
    s = y.shape[0]
    ts = min(TS, s)

    def body(y_ref, t_ref, l_ref, dy_ref):
        @pl.when(pl.program_id(0) == 0)
        def _():
            l_ref[...] = jnp.zeros_like(l_ref)

        err = y_ref[...] - t_ref[...]
        dy_ref[...] = err * (1.0 / D)
        per_tok = jnp.mean(err * err, axis=-1, keepdims=True)
        l_ref[...] += 0.5 * jnp.sum(per_tok)

    tile = pl.BlockSpec((ts, D), lambda i: (i, 0))
    return _call(body, name="loss", grid=(s // ts,), in_specs=[tile, tile],
                 out_specs=[pl.BlockSpec((8, 128), lambda i: (0, 0)), tile],
                 out_shape=[jax.ShapeDtypeStruct((8, 128), F32), jax.ShapeDtypeStruct((s, D), F32)],
                 compiler_params=_params("arbitrary"))(y, target)


def _gm_pair(u, v, lg, lb, w0, w1, bs0, bs1):
    t = u.shape[0]
    low = _lanes((t, 2 * HD)) < HD

    def head_mean(x):
        lo = jnp.sum(jnp.where(low, x, 0.0), axis=-1, keepdims=True)
        hi = jnp.sum(jnp.where(low, 0.0, x), axis=-1, keepdims=True)
        return jnp.where(low, lo, hi) * (1.0 / HD)

    xc = v - head_mean(v)
    vn = xc * lax.rsqrt(head_mean(xc * xc) + LN_EPS) * lg + lb
    v0 = jnp.where(low, vn, 0.0)
    v1 = jnp.where(low, 0.0, vn)
    causal = _rows((GM_C, GM_C)) >= _lanes((GM_C, GM_C))
    wm0 = jnp.where(causal, w0, 0.0)
    wm1 = jnp.where(causal, w1, 0.0)
    bias = jnp.where(_lanes((GM_C, 2 * HD)) < HD, bs0, bs1)
    chunks = []
    for n in range(t // GM_C):
        rs = slice(n * GM_C, (n + 1) * GM_C)
        chunks.append(u[rs] * (_bdot(wm0, v0[rs]) + _bdot(wm1, v1[rs]) + bias))
    return jnp.concatenate(chunks, axis=0)


def gm_fwd(p, lg, lb, ws, bst):
    s = p.shape[0]
    ts = min(TS_C, s)

    def body(u_ref, v_ref, lg_ref, lb_ref, ws_ref, bs_ref, o_ref):
        for j in range(GM_H // 2):
            sl = slice(j * 2 * HD, (j + 1) * 2 * HD)
            o_ref[:, sl] = _gm_pair(u_ref[:, sl], v_ref[:, sl], lg_ref[:, sl], lb_ref[:, sl], ws_ref[2 * j],
                                    ws_ref[2 * j + 1], bs_ref[:, 2 * j:2 * j + 1],
                                    bs_ref[:, 2 * j + 1:2 * j + 2]).astype(BF16)

    full = lambda shape: pl.BlockSpec(shape, lambda i: (0,) * len(shape))
    return _call(body, name="gm_fwd", grid=(s // ts,),
                 in_specs=[pl.BlockSpec((ts, GM_W), lambda i: (i, 0)), pl.BlockSpec((ts, GM_W), lambda i: (i, 1)),
                           full((1, GM_W)), full((1, GM_W)), full((GM_H, GM_C, GM_C)), full((GM_C, GM_H))],
                 out_specs=pl.BlockSpec((ts, GM_W), lambda i: (i, 0)),
                 out_shape=jax.ShapeDtypeStruct((s, GM_W), BF16),
                 compiler_params=_params("arbitrary"))(p, p, lg, lb, ws, bst)


def gm_bwd(p, lg, lb, ws, bst, dmix):
    s = p.shape[0]
    ts = min(TS_C, s)

    def body(u_ref, v_ref, lg_ref, lb_ref, ws_ref, bs_ref, dy_ref, duv_ref, dlg_ref, dlb_ref, dws_ref, dbs_ref):
        @pl.when(pl.program_id(0) == 0)
        def _():
            dlg_ref[...] = jnp.zeros_like(dlg_ref)
            dlb_ref[...] = jnp.zeros_like(dlb_ref)
            dws_ref[...] = jnp.zeros_like(dws_ref)
            dbs_ref[...] = jnp.zeros_like(dbs_ref)

        for j in range(GM_H // 2):
            sl = slice(j * 2 * HD, (j + 1) * 2 * HD)
            _, vjp = jax.vjp(_gm_pair, u_ref[:, sl], v_ref[:, sl], lg_ref[:, sl], lb_ref[:, sl], ws_ref[2 * j],
                             ws_ref[2 * j + 1], bs_ref[:, 2 * j:2 * j + 1], bs_ref[:, 2 * j + 1:2 * j + 2])
            du, dv, dlg, dlb, dw0, dw1, dbs0, dbs1 = vjp(dy_ref[:, sl])
            duv_ref[:, sl] = du.astype(BF16)
            duv_ref[:, GM_W + j * 2 * HD:GM_W + (j + 1) * 2 * HD] = dv.astype(BF16)
            dlg_ref[:, sl] += dlg
            dlb_ref[:, sl] += dlb
            dws_ref[2 * j] += dw0
            dws_ref[2 * j + 1] += dw1
            dbs_ref[:, 2 * j:2 * j + 1] += dbs0
            dbs_ref[:, 2 * j + 1:2 * j + 2] += dbs1

    full = lambda shape: pl.BlockSpec(shape, lambda i: (0,) * len(shape))
    return _call(body, name="gm_bwd", grid=(s // ts,),
                 in_specs=[pl.BlockSpec((ts, GM_W), lambda i: (i, 0)), pl.BlockSpec((ts, GM_W), lambda i: (i, 1)),
                           full((1, GM_W)), full((1, GM_W)), full((GM_H, GM_C, GM_C)), full((GM_C, GM_H)),
                           pl.BlockSpec((ts, GM_W), lambda i: (i, 0))],
                 out_specs=[pl.BlockSpec((ts, 2 * GM_W), lambda i: (i, 0)), full((1, GM_W)), full((1, GM_W)),
                            full((GM_H, GM_C, GM_C)), full((GM_C, GM_H))],
                 out_shape=[jax.ShapeDtypeStruct((s, 2 * GM_W), BF16), jax.ShapeDtypeStruct((1, GM_W), F32),
                            jax.ShapeDtypeStruct((1, GM_W), F32), jax.ShapeDtypeStruct((GM_H, GM_C, GM_C), F32),
                            jax.ShapeDtypeStruct((GM_C, GM_H), F32)],
                 compiler_params=_params("arbitrary"))(p, p, lg, lb, ws, bst, dmix)


def _s5_prep_fn(lr, li, ldt, bt):
    dt = jnp.exp(ldt)
    er = jnp.exp(lr * dt)
    ar = er * jnp.cos(li * dt)
    ai = er * jnp.sin(li * dt)
    den = lr * lr + li * li
    nr = ar - 1.0
    cr = (nr * lr + ai * li) / den
    ci = (ai * lr - nr * li) / den
    br, bi = bt[:, :S5_N], bt[:, S5_N:]
    return ar, ai, jnp.concatenate([cr * br - ci * bi, cr * bi + ci * br], axis=1)


def s5_prep_fwd(lr, li, ldt, bt):
    def body(lr_ref, li_ref, ldt_ref, bt_ref, a_ref, bb_ref):
        ar, ai, bb = _s5_prep_fn(lr_ref[...], li_ref[...], ldt_ref[...], bt_ref[...])
        a_ref[...] = jnp.concatenate([ar, ai, jnp.zeros((6, S5_N), F32)], axis=0)
        bb_ref[...] = bb

    return _call(body, name="s5_prep_fwd",
                 out_shape=[jax.ShapeDtypeStruct((8, S5_N), F32), jax.ShapeDtypeStruct((S5_H, 2 * S5_N), F32)])(lr, li, ldt, bt)


def s5_prep_bwd(lr, li, ldt, bt, da, dbb):
    def body(lr_ref, li_ref, ldt_ref, bt_ref, da_ref, dbb_ref, dlr_ref, dli_ref, dldt_ref, dbt_ref):
        _, vjp = jax.vjp(_s5_prep_fn, lr_ref[...], li_ref[...], ldt_ref[...], bt_ref[...])
        dlr, dli, dldt, dbt = vjp((da_ref[0:1, :], da_ref[1:2, :], dbb_ref[...]))
        dlr_ref[...] = dlr
        dli_ref[...] = dli
        dbt_ref[...] = dbt
        group = (_rows((S5_N, 128)) // S5_P == _lanes((S5_N, 128))).astype(F32)
        dldt_ref[...] = jnp.dot(jnp.broadcast_to(dldt, (8, S5_N)), group, precision=lax.Precision.HIGHEST,
                                preferred_element_type=F32)[0:1, :]

    r = jax.ShapeDtypeStruct((1, S5_N), F32)
    return _call(body, name="s5_prep_bwd",
                 out_shape=[r, r, jax.ShapeDtypeStruct((1, 128), F32),
                            jax.ShapeDtypeStruct((S5_H, 2 * S5_N), F32)])(lr, li, ldt, bt, da, dbb)


def _s5_out_fn(x, u, cbd, drow, wg, bg):
    y = _bdot_nt(x[:, :S5_N], cbd[:, :S5_N]) - _bdot_nt(x[:, S5_N:], cbd[:, S5_N:]) + drow * u
    y = jax.nn.gelu(y)
    gate = _bdot_nt(y, wg) + bg
    return y * jax.nn.sigmoid(gate)


def _scan_chunk(buf, ar, ai, cr, ci, reverse):
    t = buf.shape[0]

    def local(xr, xi, rows):
        within = _rows(xr.shape) % 8
        pr, pi = ar, ai
        for d in (1, 2, 4):
            keep = within < 8 - d if reverse else within >= d
            shift = rows - d if reverse else d
            sr = jnp.where(keep, pltpu.roll(xr, shift, 0), 0.0)
            si = jnp.where(keep, pltpu.roll(xi, shift, 0), 0.0)
            xr, xi = xr + pr * sr - pi * si, xi + pr * si + pi * sr
            pr, pi = pr * pr - pi * pi, 2.0 * pr * pi
        return xr, xi

    xr, xi = local(buf[:, :S5_N], buf[:, S5_N:], t)
    buf[:, :S5_N] = xr
    buf[:, S5_N:] = xi
    edge = _rows((8, S5_N)) == (7 if reverse else 0)
    pr8, pi8 = local(jnp.where(edge, ar, 0.0), jnp.where(edge, ai, 0.0), 8)

    def group(j, c):
        g = t // 8 - 1 - j if reverse else j
        rows = pl.ds(pl.multiple_of(g * 8, 8), 8)
        gr = buf[rows, :S5_N] + pr8 * c[0] - pi8 * c[1]
        gi = buf[rows, S5_N:] + pr8 * c[1] + pi8 * c[0]
        buf[rows, :S5_N] = gr
        buf[rows, S5_N:] = gi
        return (gr[0:1, :], gi[0:1, :]) if reverse else (gr[7:8, :], gi[7:8, :])

    return lax.fori_loop(0, t // 8, group, (cr, ci), unroll=4)


def s5_fwd(p, arow, bbd, cbd, drow, wg, bg):
    s = p.shape[0]
    t = min(T_S5, s)

    def body(u_ref, a_ref, bbd_ref, cbd_ref, d_ref, wg_ref, bg_ref, y_ref, st_ref, carry):
        @pl.when(pl.program_id(0) == 0)
        def _():
            carry[...] = jnp.zeros_like(carry)

        u = u_ref[...]
        st_ref[...] = _nn(u, bbd_ref[...])
        cr, ci = _scan_chunk(st_ref, a_ref[0:1, :], a_ref[1:2, :], carry[0:1, :S5_N], carry[0:1, S5_N:], False)
        carry[0:1, :S5_N] = cr
        carry[0:1, S5_N:] = ci
        y_ref[...] = _s5_out_fn(st_ref[...], u, cbd_ref[...], d_ref[...], wg_ref[...], bg_ref[...]).astype(BF16)

    full = lambda shape: pl.BlockSpec(shape, lambda i: (0,) * len(shape))
    return _call(body, name="s5_fwd", grid=(s // t,),
                 in_specs=[pl.BlockSpec((t, S5_W), lambda i: (i, 2)), full((8, S5_N)), full((S5_W, 2 * S5_N)),
                           full((S5_W, 2 * S5_N)), full((1, S5_W)), full((S5_W, S5_W)), full((1, S5_W))],
                 out_specs=[pl.BlockSpec((t, S5_W), lambda i: (i, 0)), pl.BlockSpec((t, 2 * S5_N), lambda i: (i, 0))],
                 out_shape=[jax.ShapeDtypeStruct((s, S5_W), BF16), jax.ShapeDtypeStruct((s, 2 * S5_N), F32)],
                 scratch_shapes=[pltpu.VMEM((8, 2 * S5_N), F32)],
                 compiler_params=_params("arbitrary"))(p, arow, bbd, cbd, drow, wg, bg)


def s5_bwd(p, st, arow, bbd, cbd, drow, wg, bg, dmix):
    s = p.shape[0]
    t = min(T_S5, s)
    nc = s // t

    def body(u_ref, st_ref, prev_ref, a_ref, bbd_ref, cbd_ref, d_ref, wg_ref, bg_ref, dy_ref,
             du_ref, da_ref, dbbd_ref, dcbd_ref, dd_ref, dwg_ref, dbg_ref, carry, gbuf):
        i = pl.program_id(0)

        @pl.when(i == 0)
        def _():
            carry[...] = jnp.zeros_like(carry)
            for r in (da_ref, dbbd_ref, dcbd_ref, dd_ref, dwg_ref, dbg_ref):
                r[...] = jnp.zeros_like(r)

        u = u_ref[...]
        x = st_ref[...]
        _, vjp = jax.vjp(_s5_out_fn, x, u, cbd_ref[...], d_ref[...], wg_ref[...], bg_ref[...])
        dx, du1, dcbd, dd, dwg, dbg = vjp(dy_ref[...])
        gbuf[...] = dx
        cr, ci = _scan_chunk(gbuf, a_ref[0:1, :], -a_ref[1:2, :], carry[0:1, :S5_N], carry[0:1, S5_N:], True)
        carry[0:1, :S5_N] = cr
        carry[0:1, S5_N:] = ci
        gr, gi = gbuf[:, :S5_N], gbuf[:, S5_N:]
        rid = _rows((t, S5_N))
        has_prev = (i < nc - 1).astype(F32)
        top_r = prev_ref[7:8, :S5_N] * has_prev
        top_i = prev_ref[7:8, S5_N:] * has_prev
        xpr = jnp.where(rid == 0, top_r, pltpu.roll(x[:, :S5_N], 1, 0))
        xpi = jnp.where(rid == 0, top_i, pltpu.roll(x[:, S5_N:], 1, 0))
        da_ref[0:1, :] += jnp.sum(xpr * gr + xpi * gi, axis=0, keepdims=True)
        da_ref[1:2, :] += jnp.sum(xpr * gi - xpi * gr, axis=0, keepdims=True)
        g = jnp.concatenate([gr, gi], axis=1)
        dbbd_ref[...] += _tn(u, g)
        du_ref[...] = (_nt(g, bbd_ref[...]) + du1).astype(BF16)
        dcbd_ref[...] += dcbd
        dd_ref[...] += dd
        dwg_ref[...] += dwg
        dbg_ref[...] += dbg

    full = lambda shape: pl.BlockSpec(shape, lambda i: (0,) * len(shape))
    rev = lambda col: (lambda i: (nc - 1 - i, col))
    prev_map = lambda i: (jnp.maximum((nc - 1 - i) * (t // 8) - 1, 0), 0)
    return _call(body, name="s5_bwd", grid=(nc,),
                 in_specs=[pl.BlockSpec((t, S5_W), rev(2)), pl.BlockSpec((t, 2 * S5_N), rev(0)),
                           pl.BlockSpec((8, 2 * S5_N), prev_map), full((8, S5_N)), full((S5_W, 2 * S5_N)),
                           full((S5_W, 2 * S5_N)), full((1, S5_W)), full((S5_W, S5_W)), full((1, S5_W)),
                           pl.BlockSpec((t, S5_W), rev(1))],
                 out_specs=[pl.BlockSpec((t, S5_W), rev(0)), full((8, S5_N)), full((S5_W, 2 * S5_N)),
                            full((S5_W, 2 * S5_N)), full((1, S5_W)), full((S5_W, S5_W)), full((1, S5_W))],
                 out_shape=[jax.ShapeDtypeStruct((s, S5_W), BF16), jax.ShapeDtypeStruct((8, S5_N), F32),
                            jax.ShapeDtypeStruct((S5_W, 2 * S5_N), F32), jax.ShapeDtypeStruct((S5_W, 2 * S5_N), F32),
                            jax.ShapeDtypeStruct((1, S5_W), F32), jax.ShapeDtypeStruct((S5_W, S5_W), F32),
                            jax.ShapeDtypeStruct((1, S5_W), F32)],
                 scratch_shapes=[pltpu.VMEM((8, 2 * S5_N), F32), pltpu.VMEM((t, 2 * S5_N), F32)],
                 compiler_params=_params("arbitrary"))(p, st, st, arow, bbd, cbd, drow, wg, bg, dmix)


def _cum_steps(s):
    return int(math.ceil(math.log2(s)))


V_BLK = (2 * GM_W + S5_W + 2 * FX_W) // 128


AUG = 2 * HD
BIAS_COL = HD
FQ_COL = HD + 3
PAIR_W = 256


def _split3(f):
    hi = f.astype(BF16).astype(F32)
    r = f - hi
    mid = r.astype(BF16).astype(F32)
    lo = (r - mid).astype(BF16).astype(F32)
    return hi, mid, lo


def fox_prep(p, bf):
    s = p.shape[0]
    ts = min(TS, s)
    scale = HD ** -0.5

    def body(q0_ref, q1_ref, k0_ref, k1_ref, v0_ref, v1_ref, f_ref, bf_ref,
             qa_ref, ka_ref, qat_ref, kat_ref, vt_ref, carry):
        @pl.when(pl.program_id(0) == 0)
        def _():
            carry[...] = jnp.zeros_like(carry)

        lane = _lanes((ts, 128))
        lf = jax.nn.log_sigmoid(f_ref[...] + bf_ref[...])
        acc = jnp.where(lane < FX_H, lf, 0.0)
        rid = _rows((ts, 128))
        for k in range(_cum_steps(ts)):
            d = 1 << k
            acc = acc + jnp.where(rid >= d, pltpu.roll(acc, d, 0), 0.0)
        acc = acc + carry[0:1, :]
        carry[0:1, :] = acc[ts - 1:ts, :]

        low = lane < HD
        for h in range(FX_H):
            blk, pos = divmod(h, 4)
            pair = slice((pos // 2) * 128, (pos // 2) * 128 + 128)
            hi, mid, lo = _split3(acc[:, h:h + 1])
            one = jnp.ones((ts, 1), F32)

            def augment(ref, cols):
                x = ref[:, pair]
                if pos % 2:
                    x = pltpu.roll(x, HD, 1)
                out = jnp.where(low, x, 0.0)
                for j, cval in enumerate(cols):
                    out = jnp.where(lane == HD + j, cval, out)
                return out

            qa = augment((q0_ref, q1_ref)[blk], (one, one, one, hi, mid, lo))
            qa = jnp.where(low, qa * scale, qa)
            ka = augment((k0_ref, k1_ref)[blk], (-hi, -mid, -lo, one, one, one))
            cs = slice(h * AUG, (h + 1) * AUG)
            qa_ref[:, cs] = qa.astype(BF16)
            ka_ref[:, cs] = ka.astype(BF16)
            qat_ref[cs, :] = jnp.transpose(qa).astype(BF16)
            kat_ref[cs, :] = jnp.transpose(ka).astype(BF16)
        for j in range(FX_H // 2):
            vref = (v0_ref, v1_ref)[j // 2]
            vt_ref[j * 128:(j + 1) * 128, :] = jnp.transpose(vref[:, (j % 2) * 128:(j % 2) * 128 + 128]).astype(BF16)

    q_blk = (2 * GM_W + S5_W) // PAIR_W
    col = lambda b: pl.BlockSpec((ts, PAIR_W), lambda i: (i, b))
    wide = FX_H * AUG
    return _call(body, name="fox_prep", grid=(s // ts,),
                 in_specs=[col(q_blk), col(q_blk + 1), col(q_blk + 2), col(q_blk + 3), col(q_blk + 4), col(q_blk + 5),
                           pl.BlockSpec((ts, 128), lambda i: (i, FF_COL // 128)), pl.BlockSpec((1, 128), lambda i: (0, 0))],
                 out_specs=[pl.BlockSpec((ts, wide), lambda i: (i, 0)), pl.BlockSpec((ts, wide), lambda i: (i, 0)),
                            pl.BlockSpec((wide, ts), lambda i: (0, i)), pl.BlockSpec((wide, ts), lambda i: (0, i)),
                            pl.BlockSpec((FX_W, ts), lambda i: (0, i))],
                 out_shape=[jax.ShapeDtypeStruct((s, wide), BF16), jax.ShapeDtypeStruct((s, wide), BF16),
                            jax.ShapeDtypeStruct((wide, s), BF16), jax.ShapeDtypeStruct((wide, s), BF16),
                            jax.ShapeDtypeStruct((FX_W, s), BF16)],
                 scratch_shapes=[pltpu.VMEM((8, 128), F32)],
                 compiler_params=_params("arbitrary"))(p, p, p, p, p, p, p, bf)


def fox_prep_grad(p, bf, dfq, dfk):
    s = p.shape[0]
    ts = min(TS, s)
    ns = s // ts

    def body(f_ref, bf_ref, dfq_ref, dfk_ref, df_ref, dbf_ref, carry):
        @pl.when(pl.program_id(0) == 0)
        def _():
            carry[...] = jnp.zeros_like(carry)
            dbf_ref[...] = jnp.zeros_like(dbf_ref)

        lane = _lanes((ts, 128))
        acc = jnp.zeros((ts, 128), F32)
        for h in range(FX_H):
            c = (h // 2) * 128 + h % 2
            acc = jnp.where(lane == h, dfq_ref[:, c:c + 1] + dfk_ref[:, c:c + 1], acc)
        rid = _rows((ts, 128))
        for k in range(_cum_steps(ts)):
            d = 1 << k
            acc = acc + jnp.where(rid < ts - d, pltpu.roll(acc, ts - d, 0), 0.0)
        acc = acc + carry[0:1, :]
        carry[0:1, :] = acc[0:1, :]
        z = f_ref[...] + bf_ref[...]
        df = jnp.where(lane < FX_H, acc * jax.nn.sigmoid(-z), 0.0)
        df_ref[...] = df.astype(BF16)
        dbf_ref[...] += jnp.sum(df, axis=0, keepdims=True)

    rev = lambda i: (ns - 1 - i, 0)
    return _call(body, name="fox_prep_grad", grid=(ns,),
                 in_specs=[pl.BlockSpec((ts, 128), lambda i: (ns - 1 - i, FF_COL // 128)),
                           pl.BlockSpec((1, 128), lambda i: (0, 0)),
                           pl.BlockSpec((ts, FX_W), rev), pl.BlockSpec((ts, FX_W), rev)],
                 out_specs=[pl.BlockSpec((ts, 128), rev), pl.BlockSpec((1, 128), lambda i: (0, 0))],
                 out_shape=[jax.ShapeDtypeStruct((s, 128), BF16), jax.ShapeDtypeStruct((1, 128), F32)],
                 scratch_shapes=[pltpu.VMEM((8, 128), F32)],
                 compiler_params=_params("arbitrary"))(p, bf, dfq, dfk)


def attn(qat, ka, vt):
    s = ka.shape[0]
    tq = min(TQ_FWD, s)
    nq = s // tq

    nh = ATTN_HEADS

    def body(qat_ref, ka_ref, vt_ref, o_ref, lse_ref):
        qi = pl.program_id(1)
        half = tq // 2
        lse_ref[...] = jnp.zeros_like(lse_ref)

        def update(m, l, acc, st, v):
            m_new = jnp.maximum(m, jnp.max(st, axis=0, keepdims=True))
            alpha = jnp.exp(m - m_new)
            pt = jnp.exp(st - m_new)
            return (m_new, alpha * l + jnp.sum(pt, axis=0, keepdims=True),
                    alpha * acc + jnp.dot(v, pt.astype(BF16), preferred_element_type=F32))

        def step(kj, carry):
            off = pl.multiple_of(kj * tq, tq)
            out = []
            for hh in range(nh):
                cs = slice(hh * AUG, (hh + 1) * AUG)
                st = jnp.dot(ka_ref[pl.ds(off, tq), cs], qat_ref[cs, :], preferred_element_type=F32)
                out.append(update(*carry[hh], st, vt_ref[hh * HD:(hh + 1) * HD, pl.ds(off, tq)]))
            return tuple(out)

        def diagonal(carry):
            off = pl.multiple_of(qi * tq, tq)
            off2 = pl.multiple_of(off + half, half)
            lower = _rows((half, tq)) <= _lanes((half, tq))
            out = []
            for hh in range(nh):
                cs = slice(hh * AUG, (hh + 1) * AUG)
                hs = slice(hh * HD, (hh + 1) * HD)
                st = jnp.dot(ka_ref[pl.ds(off, half), cs], qat_ref[cs, :], preferred_element_type=F32)
                m, l, acc = update(*carry[hh], jnp.where(lower, st, NEG_INF), vt_ref[hs, pl.ds(off, half)])
                st = jnp.dot(ka_ref[pl.ds(off2, half), cs], qat_ref[cs, half:], preferred_element_type=F32)
                mr, lr, ar = update(m[:, half:], l[:, half:], acc[:, half:], jnp.where(lower[:, :half], st, NEG_INF),
                                    vt_ref[hs, pl.ds(off2, half)])
                out.append((jnp.concatenate([m[:, :half], mr], axis=1), jnp.concatenate([l[:, :half], lr], axis=1),
                            jnp.concatenate([acc[:, :half], ar], axis=1)))
            return tuple(out)

        init = tuple((jnp.full((1, tq), NEG_INF, F32), jnp.zeros((1, tq), F32), jnp.zeros((HD, tq), F32))
                     for _ in range(nh))
        carry = diagonal(lax.fori_loop(0, qi, step, init))
        for hh in range(nh):
            m, l, _ = carry[hh]
            lse_ref[hh // 2, hh % 2:hh % 2 + 1, :] = m + jnp.log(l)
        for j in range(nh // 2):
            pair = jnp.concatenate([carry[2 * j][2] / carry[2 * j][1], carry[2 * j + 1][2] / carry[2 * j + 1][1]], axis=0)
            o_ref[:, j * 128:(j + 1) * 128] = jnp.transpose(pair).astype(BF16)

    return _call(body, name="attn", grid=(FX_H // nh, nq),
                 in_specs=[pl.BlockSpec((nh * AUG, tq), lambda h, i: (h, i)),
                           pl.BlockSpec((s, nh * AUG), lambda h, i: (0, h)),
                           pl.BlockSpec((nh * HD, s), lambda h, i: (h, 0))],
                 out_specs=[pl.BlockSpec((tq, nh * HD), lambda h, i: (i, h)),
                            pl.BlockSpec((nh // 2, 8, tq), lambda h, i: (h, 0, i))],
                 out_shape=[jax.ShapeDtypeStruct((s, FX_W), BF16), jax.ShapeDtypeStruct((FX_H // 2, 8, s), F32)],
                 compiler_params=_params("arbitrary", "arbitrary"))(qat, ka, vt)


def attn_grad(qa, qat, ka, kat, p, o, lse, dmix):
    s = qa.shape[0]
    tq = min(TQ, s)
    nq = s // tq
    scale = HD ** -0.5

    def body(qa_ref, qat_ref, ka_ref, kat_ref, v_ref, o_ref, lse_ref, do_ref,
             dq_ref, dk_ref, dv_ref, dfq_ref, dfk_ref, dot_scr, delta, dqt):
        kj = pl.program_id(1)
        lane = _lanes((tq, 128))
        low = lane < HD
        causal = _rows((tq, tq)) <= _lanes((tq, tq))

        @pl.when(kj == 0)
        def _():
            dqt[...] = jnp.zeros_like(dqt)
            delta[...] = jnp.zeros_like(delta)

            def prep(c, _):
                rows = pl.ds(pl.multiple_of(c * tq, tq), tq)
                do = do_ref[rows, :]
                pt = jnp.transpose(do * o_ref[rows, :].astype(F32))
                delta[0:1, rows] = jnp.sum(pt[:HD], axis=0, keepdims=True)
                delta[1:2, rows] = jnp.sum(pt[HD:], axis=0, keepdims=True)
                dot_scr[:, rows] = jnp.transpose(do).astype(BF16)
                return 0

            lax.fori_loop(0, nq, prep, 0)

        v = v_ref[...]
        vms = [jnp.where(low, v, 0.0).astype(BF16), jnp.where(low, 0.0, v).astype(BF16)]

        def tile(qi, carry, masked):
            cols = pl.ds(pl.multiple_of(qi * tq, tq), tq)
            do = do_ref[cols, :].astype(BF16)
            out = []
            for hh in range(2):
                cs = slice(hh * AUG, (hh + 1) * AUG)
                dka, dv = carry[hh]
                st = jnp.dot(ka_ref[:, cs], qat_ref[cs, cols], preferred_element_type=F32)
                if masked:
                    st = jnp.where(causal, st, NEG_INF)
                pt = jnp.exp(st - lse_ref[0, hh:hh + 1, cols])
                dv = dv + jnp.dot(pt.astype(BF16), do, preferred_element_type=F32)
                dpt = jnp.dot(vms[hh], dot_scr[:, cols], preferred_element_type=F32)
                dsb = (pt * (dpt - delta[hh:hh + 1, cols])).astype(BF16)
                dka = dka + jnp.dot(dsb, qa_ref[cols, cs], preferred_element_type=F32)
                dqt[hh, :, cols] += jnp.dot(kat_ref[cs, :], dsb, preferred_element_type=F32)
                out.append((dka, dv))
            return tuple(out)

        init = tuple((jnp.zeros((tq, AUG), F32), jnp.zeros((tq, 128), F32)) for _ in range(2))
        carry = tile(kj, init, True)
        carry = lax.fori_loop(kj + 1, nq, lambda qi, c: tile(qi, c, False), carry)
        dks = [carry[0][0], carry[1][0]]
        dvs = [carry[0][1], carry[1][1]]
        dv_ref[...] = jnp.where(low, dvs[0], dvs[1]).astype(BF16)
        dk_ref[...] = jnp.where(low, dks[0], pltpu.roll(dks[1], HD, 1)).astype(BF16)
        dfk_ref[...] = jnp.where(lane == 0, -dks[0][:, BIAS_COL:BIAS_COL + 1],
                                 jnp.where(lane == 1, -dks[1][:, BIAS_COL:BIAS_COL + 1], 0.0))

        @pl.when(kj == nq - 1)
        def _():
            def finish(c, _):
                rows = pl.ds(pl.multiple_of(c * tq, tq), tq)
                t0 = jnp.transpose(dqt[0, :, rows])
                t1 = jnp.transpose(dqt[1, :, rows])
                dq_ref[rows, :] = (jnp.where(low, t0, pltpu.roll(t1, HD, 1)) * scale).astype(BF16)
                dfq_ref[rows, :] = jnp.where(lane == 0, t0[:, FQ_COL:FQ_COL + 1],
                                             jnp.where(lane == 1, t1[:, FQ_COL:FQ_COL + 1], 0.0))
                return 0

            lax.fori_loop(0, nq, finish, 0)

    seq128 = lambda blk: pl.BlockSpec((s, 128), lambda h, j: (0, blk + h))
    tile128 = pl.BlockSpec((tq, 128), lambda h, j: (j, h))
    out_b = jax.ShapeDtypeStruct((s, FX_W), BF16)
    out_f = jax.ShapeDtypeStruct((s, FX_W), F32)
    return _call(body, name="attn_grad", grid=(FX_H // 2, nq),
                 in_specs=[pl.BlockSpec((s, 2 * AUG), lambda h, j: (0, h)), pl.BlockSpec((2 * AUG, s), lambda h, j: (h, 0)),
                           pl.BlockSpec((tq, 2 * AUG), lambda h, j: (j, h)), pl.BlockSpec((2 * AUG, tq), lambda h, j: (h, j)),
                           pl.BlockSpec((tq, 128), lambda h, j: (j, V_BLK + h)), seq128(0),
                           pl.BlockSpec((1, 8, s), lambda h, j: (h, 0, 0)), seq128(4)],
                 out_specs=[seq128(0), tile128, tile128, seq128(0), tile128],
                 out_shape=[out_b, out_b, out_b, out_f, out_f],
                 scratch_shapes=[pltpu.VMEM((128, s), BF16), pltpu.VMEM((8, s), F32), pltpu.VMEM((2, AUG, s), F32)],
                 compiler_params=_params("arbitrary", "arbitrary"))(qa, qat, ka, kat, p, o, lse, dmix)


def _shift_down(a, prev8, k):
    r = pltpu.roll(a, k, 0)
    top = jnp.where(_rows(prev8.shape) < k, pltpu.roll(prev8, k, 0), r[0:8])
    return jnp.concatenate([top, r[8:]], axis=0)


def _shift_up(a, next8, k):
    t = a.shape[0]
    r = pltpu.roll(a, t - k, 0)
    bot = jnp.where(_rows(next8.shape) >= 8 - k, pltpu.roll(next8, 8 - k, 0), r[t - 8:t])
    return jnp.concatenate([r[:t - 8], bot], axis=0)


def _conv(a, prev8, cw, cb):
    return cb + cw[0:1, :] * _shift_down(a, prev8, 2) + cw[1:2, :] * _shift_down(a, prev8, 1) + cw[2:3, :] * a


GELU_K0 = math.sqrt(2.0 / math.pi)
GELU_K1 = GELU_K0 * 0.044715


def _gelu_parts(c):
    c2 = c * c
    return c2, 0.5 + 0.5 * jnp.tanh(c * (GELU_K0 + GELU_K1 * c2))


def conv_fwd(up, cw, cb):
    s = up.shape[0]
    ts = min(TS_C, s)

    def body(a_ref, g_ref, cw_ref, cb_ref, o_ref, c_ref, halo):
        @pl.when(pl.program_id(0) == 0)
        def _():
            halo[...] = jnp.zeros_like(halo)

        a = a_ref[...]
        c = _conv(a, halo[...], cw_ref[...], cb_ref[...])
        _, h = _gelu_parts(c)
        c_ref[...] = c
        o_ref[...] = (c * h * g_ref[...]).astype(BF16)
        halo[...] = a[ts - 8:ts, :]

    tile = pl.BlockSpec((ts, DFF), lambda i: (i, 0))
    return _call(body, name="conv_fwd", grid=(s // ts,),
                 in_specs=[tile, pl.BlockSpec((ts, DFF), lambda i: (i, 1)),
                           pl.BlockSpec((3, DFF), lambda i: (0, 0)), pl.BlockSpec((1, DFF), lambda i: (0, 0))],
                 out_specs=[tile, tile],
                 out_shape=[jax.ShapeDtypeStruct((s, DFF), BF16), jax.ShapeDtypeStruct((s, DFF), F32)],
                 scratch_shapes=[pltpu.VMEM((8, DFF), F32)],
                 compiler_params=_params("arbitrary"))(up, up, cw, cb)


def conv_bwd(up, c, cw, dact):
    s = up.shape[0]
    ts = min(TS_C, s)
    ns = s // ts

    def body(a_ref, g_ref, c_ref, cw_ref, dact_ref, dup_ref, dcw_ref, dcb_ref, halo):
        @pl.when(pl.program_id(0) == 0)
        def _():
            halo[...] = jnp.zeros_like(halo)
            dcw_ref[...] = jnp.zeros_like(dcw_ref)
            dcb_ref[...] = jnp.zeros_like(dcb_ref)

        a = a_ref[...]
        cw = cw_ref[...]
        cv = c_ref[...]
        dact = dact_ref[...]
        c2, h = _gelu_parts(cv)
        dup_ref[:, DFF:] = (dact * (cv * h)).astype(BF16)
        dgel = h + cv * (2.0 * h * (1.0 - h)) * (GELU_K0 + 3.0 * GELU_K1 * c2)
        dc = dact * g_ref[...] * dgel
        up1 = _shift_up(dc, halo[...], 1)
        up2 = _shift_up(dc, halo[...], 2)
        dup_ref[:, :DFF] = (cw[2:3, :] * dc + cw[1:2, :] * up1 + cw[0:1, :] * up2).astype(BF16)
        dcw_ref[0:1, :] += jnp.sum(a * up2, axis=0, keepdims=True)
        dcw_ref[1:2, :] += jnp.sum(a * up1, axis=0, keepdims=True)
        dcw_ref[2:3, :] += jnp.sum(a * dc, axis=0, keepdims=True)
        dcb_ref[...] += jnp.sum(dc, axis=0, keepdims=True)
        halo[...] = dc[0:8, :]

    rev = lambda col: (lambda i: (ns - 1 - i, col))
    return _call(body, name="conv_bwd", grid=(ns,),
                 in_specs=[pl.BlockSpec((ts, DFF), rev(0)), pl.BlockSpec((ts, DFF), rev(1)),
                           pl.BlockSpec((ts, DFF), rev(0)), pl.BlockSpec((3, DFF), lambda i: (0, 0)),
                           pl.BlockSpec((ts, DFF), rev(0))],
                 out_specs=[pl.BlockSpec((ts, 2 * DFF), rev(0)), pl.BlockSpec((3, DFF), lambda i: (0, 0)),
                            pl.BlockSpec((1, DFF), lambda i: (0, 0))],
                 out_shape=[jax.ShapeDtypeStruct((s, 2 * DFF), BF16), jax.ShapeDtypeStruct((3, DFF), F32),
                            jax.ShapeDtypeStruct((1, DFF), F32)],
                 scratch_shapes=[pltpu.VMEM((8, DFF), F32)],
                 compiler_params=_params("arbitrary"))(up, up, c, cw, dact)


def _blockdiag_expand(m):
    m4 = m.reshape(S5_H, 2, S5_G, S5_P)
    eye = jnp.eye(S5_G, dtype=bool)[:, None, None, :, None]
    return jnp.where(eye, m4[None], 0.0).reshape(S5_W, 2 * S5_N)


def _blockdiag_extract(mbd):
    m5 = mbd.reshape(S5_G, S5_H, 2, S5_G, S5_P)
    diag = jnp.stack([m5[g, :, :, g, :] for g in range(S5_G)], axis=2)
    return diag.reshape(S5_H, 2 * S5_N)


def _c_expand(c_re, c_im):
    c4 = jnp.stack([c_re, c_im], axis=2)
    eye = jnp.eye(S5_G, dtype=bool)[:, None, None, :, None]
    return jnp.where(eye, c4[:, :, :, None, :], 0.0).reshape(S5_W, 2 * S5_N)


def _c_extract(cbd):
    m5 = cbd.reshape(S5_G, S5_H, 2, S5_G, S5_P)
    d = jnp.stack([m5[g, :, :, g, :] for g in range(S5_G)], axis=0)
    return d[:, :, 0, :], d[:, :, 1, :]


def _glu_expand(w):
    eye = jnp.eye(S5_G, dtype=bool)[:, None, :, None]
    return jnp.where(eye, w[:, :, None, :], 0.0).reshape(S5_W, S5_W)


def _glu_extract(wbd):
    m4 = wbd.reshape(S5_G, S5_H, S5_G, S5_H)
    return jnp.stack([m4[g, :, g, :] for g in range(S5_G)], axis=0)


SMALL = ("b_f", "gm_ln_g", "gm_ln_b", "gm_w_s", "gm_b_s", "s5_lam_re", "s5_lam_im", "s5_log_dt", "s5_b_re", "s5_b_im",
         "s5_c_re", "s5_c_im", "s5_d", "s5_w_glu", "s5_b_glu", "ln1_g", "ln1_b", "conv_b", "ln2_g", "ln2_b")


def _layer_operands(sp, l):
    f = {}
    f["bf"] = jnp.pad(sp["b_f"][l][None, :], ((0, 0), (0, 128 - FX_H)))
    f["gm_lg"] = sp["gm_ln_g"][l].reshape(1, GM_W)
    f["gm_lb"] = sp["gm_ln_b"][l].reshape(1, GM_W)
    f["gm_ws"] = sp["gm_w_s"][l]
    f["gm_bst"] = sp["gm_b_s"][l].T
    f["lr"] = sp["s5_lam_re"][l].reshape(1, S5_N)
    f["li"] = sp["s5_lam_im"][l].reshape(1, S5_N)
    f["ldt"] = jnp.repeat(sp["s5_log_dt"][l], S5_P).reshape(1, S5_N)
    bt = lambda b: jnp.transpose(b, (2, 0, 1)).reshape(S5_H, S5_N)
    f["bt"] = jnp.concatenate([bt(sp["s5_b_re"][l]), bt(sp["s5_b_im"][l])], axis=1)
    f["cbd"] = _c_expand(sp["s5_c_re"][l], sp["s5_c_im"][l])
    f["drow"] = sp["s5_d"][l].reshape(1, S5_W)
    f["wg"] = _glu_expand(sp["s5_w_glu"][l])
    f["bg"] = sp["s5_b_glu"][l].reshape(1, S5_W)
    for n in ("ln1_g", "ln1_b", "ln2_g", "ln2_b"):
        f[n] = sp[n][l][None, :]
    f["cb"] = sp["conv_b"][l][None, :]
    return f


def layer_fwd_mix(x, mod, w_in, f):
    p = mm_mod(x, w_in, NP, "in_proj", mod, (0, 1))
    ygm = gm_fwd(p, f["gm_lg"], f["gm_lb"], f["gm_ws"], f["gm_bst"])
    arow, bbt = s5_prep_fwd(f["lr"], f["li"], f["ldt"], f["bt"])
    bbd = _blockdiag_expand(bbt)
    ys5, st = s5_fwd(p, arow, bbd, f["cbd"], f["drow"], f["wg"], f["bg"])
    qa, ka, qat, kat, vt = fox_prep(p, f["bf"])
    yfx, lse = attn(qat, ka, vt)
    mixcat = jnp.concatenate([ygm, ys5, yfx], axis=1)
    return mixcat, dict(f=f, x=x, p=p, arow=arow, bbd=bbd, st=st, qa=qa, ka=ka, qat=qat, kat=kat, yfx=yfx, lse=lse,
                        mixcat=mixcat)


def layer_fwd_rest(x, mixcat, mod, w, f, saved):
    mix, x1 = proj_post(mixcat, w["w_out"], x, mod, 2, f["ln1_g"], f["ln1_b"], "out_proj")
    up = mm_mod(x1, w["w_up"], DFF, "up_proj", mod, (3, 4))
    act, conv = conv_fwd(up, w["conv_w"], f["cb"])
    ffn, x2 = proj_post(act, w["w_down"], x1, mod, 5, f["ln2_g"], f["ln2_b"], "down_proj")
    return x2, dict(saved, mix=mix, x1=x1, up=up, conv=conv, act=act, ffn=ffn)


def layer_fwd(x, mod, w, f):
    mixcat, saved = layer_fwd_mix(x, mod, w["w_in"], f)
    return layer_fwd_rest(x, mixcat, mod, w, f, saved)


def layer_bwd_ffn(dx, sv, mod, w):
    f = sv["f"]
    dx1, dffn, dg2, dlg2, dlb2 = post_bwd(sv["x1"], sv["ffn"], mod, 5, f["ln2_g"], f["ln2_b"], dx, "post2_bwd")
    g_down = mm_tn(sv["act"], dffn, D // 2, "down_dw")
    dact = mm_nt(dffn, w["w_down"], "down_dx")
    dup, dcw, dcb = conv_bwd(sv["up"], sv["conv"], w["conv_w"], dact)
    g_up = mm_tn(dup, sv["x1"], D // 2, "up_dw", mod=mod, rows=(3, 4))
    dx1, dsh2, dsc2 = mm_nt_mod(dup, w["w_up"], sv["x1"], dx1, mod, (3, 4), "up_dx")
    return dx1, dict(w_up=g_up, w_down=g_down, conv_w=dcw), dict(dsh2=dsh2, dsc2=dsc2, dg2=dg2, conv_b=dcb[0],
                                                                 ln2_g=dlg2[0], ln2_b=dlb2[0])


def layer_bwd_mix(dx1, sv, mod, w, part):
    f = sv["f"]
    dx0, dmix, dg1, dlg1, dlb1 = post_bwd(sv["x"], sv["mix"], mod, 2, f["ln1_g"], f["ln1_b"], dx1, "post1_bwd")
    g_out = mm_tn(sv["mixcat"], dmix, D, "out_dw")
    dmc = mm_nt(dmix, w["w_out"], "out_dx")
    duv, dgm_lg, dgm_lb, dgm_ws, dgm_bst = gm_bwd(sv["p"], f["gm_lg"], f["gm_lb"], f["gm_ws"], f["gm_bst"], dmc)
    du5, da, dbbd, dcbd, dd5, dwg, dbg = s5_bwd(sv["p"], sv["st"], sv["arow"], sv["bbd"], f["cbd"], f["drow"],
                                                f["wg"], f["bg"], dmc)
    dlr, dli, dldt, dbt = s5_prep_bwd(f["lr"], f["li"], f["ldt"], f["bt"], da, _blockdiag_extract(dbbd))
    dq, dk, dv, dfq, dfk = attn_grad(sv["qa"], sv["qat"], sv["ka"], sv["kat"], sv["p"], sv["yfx"], sv["lse"], dmc)
    dff, dbf = fox_prep_grad(sv["p"], f["bf"], dfq, dfk)
    dp = jnp.concatenate([duv, du5, dq, dk, dv, dff], axis=1)
    g_in = mm_tn(dp, sv["x"], D, "in_dw", mod=mod, rows=(0, 1))
    dx, dsh1, dsc1 = mm_nt_mod(dp, w["w_in"], sv["x"], dx0, mod, (0, 1), "in_dx")

    dmod = jnp.concatenate([dsh1, dsc1, dg1, part["dsh2"], part["dsc2"], part["dg2"]], axis=0)
    dc_re, dc_im = _c_extract(dcbd)
    dbt4 = dbt.reshape(S5_H, 2, S5_G, S5_P)
    vals = dict(b_f=dbf[0, :FX_H], gm_ln_g=dgm_lg.reshape(GM_H, HD), gm_ln_b=dgm_lb.reshape(GM_H, HD),
                gm_w_s=dgm_ws, gm_b_s=dgm_bst.T, s5_lam_re=dlr.reshape(S5_G, S5_P),
                s5_lam_im=dli.reshape(S5_G, S5_P), s5_log_dt=dldt[0, :S5_G],
                s5_b_re=jnp.transpose(dbt4[:, 0], (1, 2, 0)), s5_b_im=jnp.transpose(dbt4[:, 1], (1, 2, 0)),
                s5_c_re=dc_re, s5_c_im=dc_im, s5_d=dd5.reshape(S5_G, S5_H), s5_w_glu=_glu_extract(dwg),
                s5_b_glu=dbg.reshape(S5_G, S5_H), ln1_g=dlg1[0], ln1_b=dlb1[0], conv_b=part["conv_b"],
                ln2_g=part["ln2_g"], ln2_b=part["ln2_b"])
    return dx, dict(w_in=g_in, w_out=g_out), vals, dmod


def layer_bwd(dx, sv, mod, w):
    dx1, g_ffn, part = layer_bwd_ffn(dx, sv, mod, w)
    dx, g_mix, vals, dmod = layer_bwd_mix(dx1, sv, mod, w, part)
    return dx, dict(g_ffn, **g_mix), vals, dmod


def local_step(x, target, mods, big, sp):
    saved = []
    for l in range(DEPTH):
        x, sv = layer_fwd(x, mods[l], big[l], _layer_operands(sp, l))
        saved.append(sv)
    loss_tile, dx = loss_kernel(x, target)
    gbig, vals, dmods = [None] * DEPTH, [None] * DEPTH, [None] * DEPTH
    for l in reversed(range(DEPTH)):
        dx, gbig[l], vals[l], dmods[l] = layer_bwd(dx, saved[l], mods[l], big[l])
    gsm = {n: jnp.stack([v[n] for v in vals]) for n in SMALL}
    return loss_tile, dx, gbig, gsm, jnp.stack(dmods)


def _my_index():
    return 4 * lax.axis_index("x") + 2 * lax.axis_index("y") + lax.axis_index("c")


def exchange(tensors, scatter, name):
    n = len(tensors)

    def body(*refs):
        ins, outs = refs[:n], refs[n:2 * n]
        send_sems, recv_sems, local_sems = refs[2 * n:]
        x, y, c = lax.axis_index("x"), lax.axis_index("y"), lax.axis_index("c")
        me = 4 * x + 2 * y + c
        local = []
        for t in range(n):
            cp = pltpu.make_async_copy(ins[t].at[me] if scatter else ins[t], outs[t].at[me], local_sems.at[t])
            cp.start()
            local.append(cp)
        remote = []
        for m in range(1, NDEV):
            px = 1 - x if m & 4 else x
            py = 1 - y if m & 2 else y
            pc = 1 - c if m & 1 else c
            peer = 4 * px + 2 * py + pc
            for t in range(n):
                k = t * (NDEV - 1) + m - 1
                cp = pltpu.make_async_remote_copy(
                    src_ref=ins[t].at[peer] if scatter else ins[t], dst_ref=outs[t].at[me],
                    send_sem=send_sems.at[k], recv_sem=recv_sems.at[k],
                    device_id=(px, py, pc), device_id_type=MESH_IDS)
                cp.start()
                remote.append(cp)
        for cp in remote:
            cp.wait()
        for cp in local:
            cp.wait()

    hbm = pl.BlockSpec(memory_space=pltpu.HBM)
    out_shape = [jax.ShapeDtypeStruct(t.shape if scatter else (NDEV,) + t.shape, t.dtype) for t in tensors]
    return _call(body, name=name, in_specs=[hbm] * n, out_specs=[hbm] * n, out_shape=out_shape,
                 scratch_shapes=[pltpu.SemaphoreType.DMA((n * (NDEV - 1),)), pltpu.SemaphoreType.DMA((n * (NDEV - 1),)),
                                 pltpu.SemaphoreType.DMA((n,))])(*tensors)


def _peers():
    x, y, c = lax.axis_index("x"), lax.axis_index("y"), lax.axis_index("c")
    out = []
    for m in range(1, NDEV):
        px = 1 - x if m & 4 else x
        py = 1 - y if m & 2 else y
        pc = 1 - c if m & 1 else c
        out.append(((px, py, pc), 4 * px + 2 * py + pc))
    return 4 * x + 2 * y + c, out


def _split_copies(v_refs, land_refs, send_sems, recv_sems, scatter):
    me, peers = _peers()
    return [pltpu.make_async_remote_copy(
        src_ref=v_ref.at[idx] if scatter else v_ref, dst_ref=land_ref.at[me],
        send_sem=send_sems.at[t * (NDEV - 1) + k], recv_sem=recv_sems.at[t * (NDEV - 1) + k],
        device_id=pos, device_id_type=MESH_IDS)
        for t, (v_ref, land_ref) in enumerate(zip(v_refs, land_refs)) for k, (pos, idx) in enumerate(peers)]


_HBM_SPEC = pl.BlockSpec(memory_space=pltpu.HBM)
_SEM_SPEC = pl.BlockSpec(memory_space=pltpu.SEMAPHORE)
_SPLIT_EFFECT = pltpu.SideEffectType.DATAFLOW_SIDE_EFFECTING


def exchange_start(tensors, scatter, name):
    n = len(tensors)
    land_shapes = [t.shape if scatter else (NDEV,) + t.shape for t in tensors]

    def body(*refs):
        v_refs, land_refs = refs[:n], refs[n:2 * n]
        send_sems, recv_sems = refs[2 * n], refs[2 * n + 1]
        token = refs[-1]
        for cp in _split_copies(v_refs, land_refs, send_sems, recv_sems, scatter):
            cp.start()
        token[...] = jnp.zeros_like(token)

    sems = pltpu.SemaphoreType.DMA((n * (NDEV - 1),))
    out = _call(
        body, name=name,
        out_shape=(sems, sems, *[pltpu.HBM(t.shape, t.dtype) for t in tensors],
                   *[pltpu.HBM(s, t.dtype) for s, t in zip(land_shapes, tensors)], jax.ShapeDtypeStruct((8, 128), F32)),
        in_specs=(_HBM_SPEC,) * (2 * n),
        out_specs=(_SEM_SPEC, _SEM_SPEC) + (_HBM_SPEC,) * (2 * n) + (pl.BlockSpec(memory_space=pltpu.VMEM),),
        input_output_aliases={i: i + 2 for i in range(2 * n)},
        compiler_params=pltpu.CompilerParams(has_side_effects=_SPLIT_EFFECT),
    )(*[pltpu.with_memory_space_constraint(t, pltpu.HBM) for t in tensors],
      *[pltpu.with_memory_space_constraint(lax.empty(s, t.dtype), pltpu.HBM) for s, t in zip(land_shapes, tensors)])
    return out[0], out[1], list(out[2:2 + n]), list(out[2 + n:2 + 2 * n]), out[-1]


def exchange_wait(started, after, scatter, name):
    send_sems, recv_sems, v_thru, land_thru, _ = started
    n = len(v_thru)

    def body(*refs):
        v_refs, land_refs = refs[:n], refs[n:2 * n]
        for cp in _split_copies(v_refs, land_refs, refs[2 * n], refs[2 * n + 1], scatter):
            cp.wait_send()
            cp.wait_recv()

    out = _call(
        body, name=name,
        out_shape=tuple(pltpu.HBM(t.shape, t.dtype) for t in v_thru + land_thru),
        in_specs=(_HBM_SPEC,) * (2 * n) + (_SEM_SPEC, _SEM_SPEC, pl.BlockSpec(memory_space=pl.ANY)),
        out_specs=(_HBM_SPEC,) * (2 * n), input_output_aliases={i: i for i in range(2 * n)},
        compiler_params=pltpu.CompilerParams(has_side_effects=_SPLIT_EFFECT),
    )(*v_thru, *land_thru, send_sems, recv_sems, after)
    return list(out[:n]), list(out[n:])


def mod_slices(c_all, w_ada, b_loc):
    nl, _, nc = w_ada.shape

    def body(c_ref, w_ref, b_ref, o_ref):
        cv = c_ref[...]
        o_ref[0] = _nn(cv * jax.nn.sigmoid(cv), w_ref[0]) + b_ref[0]

    return _call(body, name="mod_slices", grid=(nl,),
                 in_specs=[pl.BlockSpec((NDEV, D), lambda l: (0, 0)), pl.BlockSpec((1, D, nc), lambda l: (l, 0, 0)),
                           pl.BlockSpec((1, 1, nc), lambda l: (l, 0, 0))],
                 out_specs=pl.BlockSpec((1, NDEV, nc), lambda l: (l, 0, 0)),
                 out_shape=jax.ShapeDtypeStruct((nl, NDEV, nc), F32),
                 compiler_params=_params("arbitrary"))(c_all, w_ada, b_loc.reshape(nl, 1, nc))


def ada_grad(c_all, dm_loc):
    nl, _, nc = dm_loc.shape

    def body(c_ref, d_ref, o_ref):
        cv = c_ref[...]
        o_ref[0] = _tn(cv * jax.nn.sigmoid(cv), d_ref[0])

    return _call(body, name="ada_grad", grid=(nl,),
                 in_specs=[pl.BlockSpec((NDEV, D), lambda l: (0, 0)), pl.BlockSpec((1, NDEV, nc), lambda l: (l, 0, 0))],
                 out_specs=pl.BlockSpec((1, D, nc), lambda l: (l, 0, 0)),
                 out_shape=jax.ShapeDtypeStruct((nl, D, nc), F32),
                 compiler_params=_params("arbitrary"))(c_all, dm_loc)


def sum_chunks(chunks):
    r = chunks.shape[1]

    def body(c_ref, o_ref):
        acc = c_ref[0]
        for i in range(1, NDEV):
            acc = acc + c_ref[i]
        o_ref[...] = acc

    return _call(body, name="sum_chunks", out_shape=jax.ShapeDtypeStruct((r, 128), F32))(chunks)


def _row_tile(r):
    if r <= 256:
        return r
    for t in range(256, 7, -8):
        if r % t == 0:
            return t
    return r


def adamw(w, m, v, g=None, chunks=None, name="adamw"):
    r, cdim = w.shape
    tr = _row_tile(r)
    bc1 = 1.0 - ADAM_B1 ** ADAM_STEP
    bc2 = 1.0 - ADAM_B2 ** ADAM_STEP

    def body(g_ref, w_ref, m_ref, v_ref, go_ref, d_ref, mo_ref, vo_ref):
        if chunks is None:
            grad = g_ref[...]
        else:
            grad = g_ref[0].astype(F32)
            for i in range(1, NDEV):
                grad = grad + g_ref[i].astype(F32)
        mn = ADAM_B1 * m_ref[...] + (1.0 - ADAM_B1) * grad
        vn = ADAM_B2 * v_ref[...] + (1.0 - ADAM_B2) * (grad * grad)
        m_hat = mn / bc1
        v_hat = vn / bc2
        go_ref[...] = grad
        d_ref[...] = -ADAM_LR * (m_hat / (jnp.sqrt(v_hat) + ADAM_EPS) + ADAM_WD * w_ref[...])
        mo_ref[...] = mn
        vo_ref[...] = vn

    tile = pl.BlockSpec((tr, cdim), lambda i: (i, 0))
    gspec = tile if chunks is None else pl.BlockSpec((NDEV, tr, cdim), lambda i: (0, i, 0))
    shp = jax.ShapeDtypeStruct((r, cdim), F32)
    return _call(body, name=name, grid=(r // tr,), in_specs=[gspec, tile, tile, tile],
                 out_specs=[tile] * 4, out_shape=[shp] * 4,
                 compiler_params=_params("arbitrary"))(g if chunks is None else chunks, w, m, v)


def adamw_layers(w, m, v, chunks, name):
    nl, r, cdim = w.shape
    tr = _row_tile(r)
    bc1 = 1.0 - ADAM_B1 ** ADAM_STEP
    bc2 = 1.0 - ADAM_B2 ** ADAM_STEP
    outs = [lax.empty(w.shape, F32) for _ in range(4)]
    for l in range(nl):
        def body(g_ref, w_ref, m_ref, v_ref, p0, p1, p2, p3, go_ref, d_ref, mo_ref, vo_ref):
            grad = g_ref[0].astype(F32)
            for i in range(1, NDEV):
                grad = grad + g_ref[i].astype(F32)
            mn = ADAM_B1 * m_ref[...] + (1.0 - ADAM_B1) * grad
            vn = ADAM_B2 * v_ref[...] + (1.0 - ADAM_B2) * (grad * grad)
            go_ref[...] = grad
            d_ref[...] = -ADAM_LR * ((mn / bc1) / (jnp.sqrt(vn / bc2) + ADAM_EPS) + ADAM_WD * w_ref[...])
            mo_ref[...] = mn
            vo_ref[...] = vn

        tile = pl.BlockSpec((None, tr, cdim), lambda i, l=l: (l, i, 0))
        whole = pl.BlockSpec(memory_space=pl.ANY)
        outs = _call(body, name=f"{name}_{l}", grid=(r // tr,),
                     in_specs=[pl.BlockSpec((NDEV, tr, cdim), lambda i: (0, i, 0)), tile, tile, tile] + [whole] * 4,
                     out_specs=[tile] * 4, out_shape=[jax.ShapeDtypeStruct(w.shape, F32)] * 4,
                     input_output_aliases={4: 0, 5: 1, 6: 2, 7: 3},
                     compiler_params=_params("arbitrary"))(chunks[l], w, m, v, *outs)
    return outs


WEIGHTS = ("w_ada", "b_ada", "w_in", "b_f", "gm_ln_g", "gm_ln_b", "gm_w_s", "gm_b_s", "s5_lam_re", "s5_lam_im",
           "s5_log_dt", "s5_b_re", "s5_b_im", "s5_c_re", "s5_c_im", "s5_d", "s5_w_glu", "s5_b_glu", "w_out", "ln1_g",
           "ln1_b", "w_up", "conv_w", "conv_b", "w_down", "ln2_g", "ln2_b")
LARGE = ("w_in", "w_out", "w_up", "w_down")
TRANSPOSED = ("w_in", "w_up")
PACKED = ("b_ada",) + SMALL
PACK_SEG = 8 * 128


def _gather_cols(g):
    nd, nl, r, c = g.shape
    return jnp.transpose(g, (1, 2, 0, 3)).reshape(nl, r, nd * c)


def _chunk_cols(g):
    nl, r, c8 = g.shape
    return jnp.transpose(g.reshape(nl, r, NDEV, c8 // NDEV), (2, 0, 1, 3))


def _join_rows(g):
    nd, r, c = g.shape
    return g.reshape(nd * r, c)


def _split_rows(g):
    r8, c = g.shape
    return g.reshape(NDEV, r8 // NDEV, c)


def _pack(parts):
    segs = []
    for n in PACKED:
        flat = parts[n].reshape(-1)
        segs.append(jnp.pad(flat, (0, -flat.shape[0] % PACK_SEG)).reshape(-1, 128))
    rows = jnp.concatenate(segs, axis=0)
    return jnp.pad(rows, ((0, -rows.shape[0] % (NDEV * 8)), (0, 0)))


def _unpack(rows, shapes):
    out, off = {}, 0
    for n in PACKED:
        size = math.prod(shapes[n])
        nrows = -(-size // PACK_SEG) * 8
        out[n] = rows[off:off + nrows].reshape(-1)[:size].reshape(shapes[n])
        off += nrows
    return out


def kernel(x, c, w_ada, b_ada, w_in, b_f, gm_ln_g, gm_ln_b, gm_w_s, gm_b_s, s5_lam_re, s5_lam_im, s5_log_dt, s5_b_re, s5_b_im, s5_c_re, s5_c_im, s5_d, s5_w_glu, s5_b_glu, w_out, ln1_g, ln1_b, w_up, conv_w, conv_b, w_down, ln2_g, ln2_b, loss_target, m_w_ada, m_b_ada, m_w_in, m_b_f, m_gm_ln_g, m_gm_ln_b, m_gm_w_s, m_gm_b_s, m_s5_lam_re, m_s5_lam_im, m_s5_log_dt, m_s5_b_re, m_s5_b_im, m_s5_c_re, m_s5_c_im, m_s5_d, m_s5_w_glu, m_s5_b_glu, m_w_out, m_ln1_g, m_ln1_b, m_w_up, m_conv_w, m_conv_b, m_w_down, m_ln2_g, m_ln2_b, v_w_ada, v_b_ada, v_w_in, v_b_f, v_gm_ln_g, v_gm_ln_b, v_gm_w_s, v_gm_b_s, v_s5_lam_re, v_s5_lam_im, v_s5_log_dt, v_s5_b_re, v_s5_b_im, v_s5_c_re, v_s5_c_im, v_s5_d, v_s5_w_glu, v_s5_b_glu, v_w_out, v_ln1_g, v_ln1_b, v_w_up, v_conv_w, v_conv_b, v_w_down, v_ln2_g, v_ln2_b):
    given = dict(locals())
    wts = {n: given[n] for n in WEIGHTS}
    mom = {n: given["m_" + n] for n in WEIGHTS}
    var = {n: given["v_" + n] for n in WEIGHTS}
    nl = w_ada.shape[0]
    me = _my_index()
    ada_cols = w_ada.shape[2]

    (c_all,) = exchange([c], False, "gather_c")
    c_all = c_all.reshape(NDEV, D)
    b_loc = lax.dynamic_slice_in_dim(b_ada, me * ada_cols, ada_cols, axis=1)
    mod_part = mod_slices(c_all, w_ada, b_loc)

    mod_all, conv_all = exchange([mod_part, conv_w], False, "gather_mod")
    mod_mine = lax.dynamic_index_in_dim(mod_all, me, axis=2, keepdims=False)
    mods = jnp.transpose(mod_mine, (1, 0, 2)).reshape(nl, 6, D)
    mods = jnp.pad(mods, ((0, 0), (0, 2), (0, 0)))
    conv_full = _gather_cols(conv_all)
    sp = {n: wts[n] for n in SMALL}
    rowwise = {n: [jnp.swapaxes(a[n], 1, 2) if n in TRANSPOSED else a[n] for a in (wts, mom, var)] for n in LARGE}

    def joined(own, land):
        return _join_rows(lax.dynamic_update_index_in_dim(land, own, me, 0))

    def block(l, names):
        return [rowwise[n][0][l].astype(BF16) for n in names]

    def chunked(grads, names):
        return [_split_rows(grads[n]) for n in names]

    head, tail = LARGE[:1], LARGE[1:]
    got_head = exchange_start(block(0, head), False, "gather_start_0_in")
    rest = block(0, tail)
    rest[0] = rest[0] + got_head[4][0, 0].astype(BF16)
    got_tail = exchange_start(rest, False, "gather_start_0_rest")
    xl, saved, weights = x[0], [], []
    for l in range(nl):
        if l == 0:
            own, land = exchange_wait(got_head, got_tail[4], False, "gather_wait_0_in")
            w = {n: joined(o, g) for n, o, g in zip(head, own, land)}
        else:
            own, land = exchange_wait(started, xl, False, f"gather_wait_{l}")
            w = {n: joined(o, g) for n, o, g in zip(LARGE, own, land)}
        mod_l = mods[l]
        if l + 1 < nl:
            nxt, w["w_in"] = lax.optimization_barrier((block(l + 1, LARGE), w["w_in"]))
            started = exchange_start(nxt, False, f"gather_start_{l + 1}")
            mod_l = mod_l + started[4][0, 0]
        w["w_in"] = jnp.pad(w["w_in"], ((0, NP - D_IN), (0, 0)))
        f = _layer_operands(sp, l)
        mixcat, sv = layer_fwd_mix(xl, mod_l, w["w_in"], f)
        if l == 0:
            own, land = exchange_wait(got_tail, mixcat, False, "gather_wait_0_rest")
            w.update({n: joined(o, g) for n, o, g in zip(tail, own, land)})
        w["conv_w"] = conv_full[l]
        xl, sv = layer_fwd_rest(xl, mixcat, mod_l, w, f, sv)
        weights.append(w)
        saved.append(sv)

    loss_tile, dx = loss_kernel(xl, loss_target[0])

    ffn_names, mix_names = ("w_up", "w_down"), ("w_in", "w_out")
    scattering, vals, dmods, gconv = [None] * nl, [None] * nl, [None] * nl, [None] * nl
    token = jnp.zeros((), F32)
    for l in reversed(range(nl)):
        mod_l = mods[l] + token
        dx1, g_ffn, part = layer_bwd_ffn(dx, saved[l], mod_l, weights[l])
        gconv[l] = g_ffn["conv_w"]
        if l == 0:
            sent_ffn = exchange_start(chunked(g_ffn, ffn_names), True, "scatter_start_0_ffn")
            mod_l = mod_l + sent_ffn[4][0, 0]
        dx, g_mix, vals[l], dmods[l] = layer_bwd_mix(dx1, saved[l], mod_l, weights[l], part)
        g_mix["w_in"] = g_mix["w_in"][:D_IN]
        if l == 0:
            scattering[l] = [(ffn_names, sent_ffn),
                             (mix_names, exchange_start(chunked(g_mix, mix_names), True, "scatter_start_0_mix"))]
        else:
            sent = exchange_start(chunked(dict(g_ffn, **g_mix), LARGE), True, f"scatter_start_{l}")
            scattering[l] = [(LARGE, sent)]
            token = sent[4][0, 0]
    gx = dx
    dmods = jnp.stack(dmods)
    gsm = {n: jnp.stack([v[n] for v in vals]) for n in SMALL}

    gsm["b_ada"] = dmods.reshape(nl, 6 * D)
    packed = _pack(gsm).reshape(NDEV, -1, 128)
    conv_recv, small_recv = exchange([_chunk_cols(jnp.stack(gconv)), packed], True, "scatter_small")
    small_sum = sum_chunks(small_recv)
    small_all, dmod_all = exchange([small_sum, dmods.reshape(nl, 6 * D)], False, "gather_small")

    received = [dict() for _ in range(nl)]

    def arrive(l, k, after):
        names, sent = scattering[l][k]
        own, land = exchange_wait(sent, after, True, f"scatter_wait_{l}_{k}")
        for n, o, g in zip(names, own, land):
            mine = lax.dynamic_index_in_dim(o, me, 0, keepdims=False)
            received[l][n] = lax.dynamic_update_index_in_dim(g, mine, me, 0)
        return land[0]

    after = small_all
    for l in reversed(range(nl)):
        after = arrive(l, 0, after)
    out = {}
    def update(n):
        res = adamw_layers(*rowwise[n], [received[l][n] for l in range(nl)], "adamw_" + n)
        return [jnp.swapaxes(r, 1, 2) for r in res] if n in TRANSPOSED else res

    for n in ffn_names:
        out[n] = update(n)
    arrive(0, 1, out[ffn_names[-1]][0])
    for n in mix_names:
        out[n] = update(n)
    shp = conv_w.shape
    two_d = lambda a: a.reshape(shp[0] * shp[1], shp[2])
    res = adamw(two_d(conv_w), two_d(m_conv_w), two_d(v_conv_w),
                chunks=conv_recv.reshape(NDEV, shp[0] * shp[1], shp[2]), name="adamw_conv_w")
    out["conv_w"] = [r.reshape(shp) for r in res]

    dm_loc = lax.dynamic_slice_in_dim(dmod_all, me * ada_cols, ada_cols, axis=2)
    g_ada = ada_grad(c_all, jnp.transpose(dm_loc, (1, 0, 2)))
    two_d = lambda a: a.reshape(nl * D, ada_cols)
    res = adamw(two_d(w_ada), two_d(m_w_ada), two_d(v_w_ada), g=two_d(g_ada), name="adamw_w_ada")
    out["w_ada"] = [r.reshape(w_ada.shape) for r in res]

    shapes = {n: wts[n].shape for n in PACKED}
    res = adamw(_pack(wts), _pack(mom), _pack(var), g=small_all.reshape(-1, 128), name="adamw_small")
    unpacked = [_unpack(r, shapes) for r in res]
    for n in PACKED:
        out[n] = [u[n] for u in unpacked]

    loss = lax.psum(loss_tile[0, 0], ("x", "y", "c"))
    return (loss, gx[None], *[out[n][0] for n in WEIGHTS], *[out[n][1] for n in WEIGHTS],
            *[out[n][2] for n in WEIGHTS], *[out[n][3] for n in WEIGHTS])
```

```python
import math

import jax
import jax.numpy as jnp
from jax import lax
from jax.experimental import pallas as pl
from jax.experimental.pallas import tpu as pltpu

F32 = jnp.float32
BF16 = jnp.bfloat16
MESH_IDS = pl.DeviceIdType.MESH

D = 1024
SEQ = 4096
DEPTH = 4
NDEV = 8
HD = 64
GM_W = 256
GM_H = 4
GM_C = 128
S5_W = 256
S5_G = 16
S5_H = 16
S5_P = 64
S5_N = S5_G * S5_P
FX_W = 512
FX_H = 8
D_IN = 2 * GM_W + S5_W + 3 * FX_W + FX_H
NP = 2432
FF_COL = 2304
DFF = 2816
LN_EPS = 1e-5
DN_ALPHA = (2.0 * DEPTH) ** 0.25
NEG_INF = -1e30
ADAM_LR = 0.001
ADAM_B1 = 0.9
ADAM_B2 = 0.999
ADAM_EPS = 1e-08
ADAM_WD = 0.01
ADAM_STEP = 10

V7X_VMEM_LIMIT = 56 * 1024 * 1024
TS = 512
TS_C = 256
T_S5 = 512
TQ = 512
TQ_FWD = 1024
ATTN_HEADS = 2


def _call(body, **kw):
    return pl.pallas_call(body, **kw)


def _params(*sem):
    return pltpu.CompilerParams(dimension_semantics=sem if sem else None,
                                vmem_limit_bytes=V7X_VMEM_LIMIT)


def _nn(a, b):
    return jnp.dot(a.astype(BF16), b.astype(BF16), preferred_element_type=F32)


def _nt(a, b):
    return lax.dot_general(a.astype(BF16), b.astype(BF16), (((1,), (1,)), ((), ())),
                           preferred_element_type=F32)


def _tn(a, b):
    return lax.dot_general(a.astype(BF16), b.astype(BF16), (((0,), (0,)), ((), ())),
                           preferred_element_type=F32)


@jax.custom_vjp
def _bdot(a, b):
    return _nn(a, b)


def _bdot_fwd(a, b):
    return _nn(a, b), (a, b)


def _bdot_bwd(res, g):
    a, b = res
    return _nt(g, b), _tn(a, g)


_bdot.defvjp(_bdot_fwd, _bdot_bwd)


@jax.custom_vjp
def _bdot_nt(a, b):
    return _nt(a, b)


def _bdot_nt_fwd(a, b):
    return _nt(a, b), (a, b)


def _bdot_nt_bwd(res, g):
    a, b = res
    return _nn(g, b), _tn(g, a)


_bdot_nt.defvjp(_bdot_nt_fwd, _bdot_nt_bwd)


def _ln(r, g, b):
    mu = jnp.mean(r, axis=-1, keepdims=True)
    xc = r - mu
    var = jnp.mean(xc * xc, axis=-1, keepdims=True)
    return xc * lax.rsqrt(var + LN_EPS) * g + b


def _rows(shape):
    return lax.broadcasted_iota(jnp.int32, shape, 0)


def _lanes(shape):
    return lax.broadcasted_iota(jnp.int32, shape, 1)


def mm_mod(a, wt, tn, name, mod, rows):
    s, k = a.shape
    n = wt.shape[0]
    ts = min(TS, s)

    def body(a_ref, m_ref, w_ref, o_ref):
        h = a_ref[...] * (1.0 + m_ref[rows[1]:rows[1] + 1, :]) + m_ref[rows[0]:rows[0] + 1, :]
        o_ref[...] = _nt(h, w_ref[...])

    return _call(body, name=name, grid=(n // tn, s // ts),
                 in_specs=[pl.BlockSpec((ts, k), lambda j, i: (i, 0)), pl.BlockSpec((8, k), lambda j, i: (0, 0)),
                           pl.BlockSpec((tn, k), lambda j, i: (j, 0))],
                 out_specs=pl.BlockSpec((ts, tn), lambda j, i: (i, j)),
                 out_shape=jax.ShapeDtypeStruct((s, n), F32),
                 compiler_params=_params("arbitrary", "arbitrary"))(a, mod, wt)


def mm_nt(dy, w, name):
    s, n = dy.shape
    k = w.shape[0]
    ts = min(TS, s)

    def body(dy_ref, w_ref, o_ref):
        o_ref[...] = _nt(dy_ref[...], w_ref[...])

    return _call(body, name=name, grid=(s // ts,),
                 in_specs=[pl.BlockSpec((ts, n), lambda i: (i, 0)),
                           pl.BlockSpec((k, n), lambda i: (0, 0))],
                 out_specs=pl.BlockSpec((ts, k), lambda i: (i, 0)),
                 out_shape=jax.ShapeDtypeStruct((s, k), F32),
                 compiler_params=_params("arbitrary"))(dy, w)


def mm_nt_mod(dy, w, x, dres, mod, rows, name):
    s, n = dy.shape
    k = w.shape[1]
    ts = min(TS, s)

    def body(dy_ref, w_ref, x_ref, r_ref, m_ref, dx_ref, dsh_ref, dsc_ref):
        @pl.when(pl.program_id(0) == 0)
        def _():
            dsh_ref[...] = jnp.zeros_like(dsh_ref)
            dsc_ref[...] = jnp.zeros_like(dsc_ref)

        dh = _nn(dy_ref[...], w_ref[...])
        dx_ref[...] = r_ref[...] + dh * (1.0 + m_ref[rows[1]:rows[1] + 1, :])
        dsh_ref[...] += jnp.sum(dh, axis=0, keepdims=True)
        dsc_ref[...] += jnp.sum(dh * x_ref[...], axis=0, keepdims=True)

    row = pl.BlockSpec((1, k), lambda i: (0, 0))
    tile = pl.BlockSpec((ts, k), lambda i: (i, 0))
    return _call(body, name=name, grid=(s // ts,),
                 in_specs=[pl.BlockSpec((ts, n), lambda i: (i, 0)),
                           pl.BlockSpec((n, k), lambda i: (0, 0)), tile, tile,
                           pl.BlockSpec((8, k), lambda i: (0, 0))],
                 out_specs=[tile, row, row],
                 out_shape=[jax.ShapeDtypeStruct((s, k), F32),
                            jax.ShapeDtypeStruct((1, k), F32),
                            jax.ShapeDtypeStruct((1, k), F32)],
                 compiler_params=_params("arbitrary"))(dy, w, x, dres, mod)


def mm_tn(a, dy, tn, name, mod=None, rows=None):
    s, k = a.shape
    n = dy.shape[1]
    ts = min(TS, s)
    ns = s // ts

    def body(*refs):
        if mod is None:
            a_ref, dy_ref, o_ref, acc = refs
            h = dy_ref[...]
        else:
            a_ref, dy_ref, m_ref, o_ref, acc = refs
            h = dy_ref[...] * (1.0 + m_ref[rows[1]:rows[1] + 1, :]) + m_ref[rows[0]:rows[0] + 1, :]
        i = pl.program_id(1)

        @pl.when(i == 0)
        def _():
            acc[...] = jnp.zeros_like(acc)

        acc[...] += _tn(a_ref[...], h)

        @pl.when(i == ns - 1)
        def _():
            o_ref[...] = acc[...].astype(BF16)

    in_specs = [pl.BlockSpec((ts, k), lambda j, i: (i, 0)), pl.BlockSpec((ts, tn), lambda j, i: (i, j))]
    args = [a, dy]
    if mod is not None:
        in_specs.append(pl.BlockSpec((8, tn), lambda j, i: (0, j)))
        args.append(mod)
    return _call(body, name=name, grid=(n // tn, s // ts), in_specs=in_specs,
                 out_specs=pl.BlockSpec((k, tn), lambda j, i: (0, j)),
                 out_shape=jax.ShapeDtypeStruct((k, n), BF16),
                 scratch_shapes=[pltpu.VMEM((k, tn), F32)],
                 compiler_params=_params("arbitrary", "arbitrary"))(*args)


def _post_fn(x, br, gate, lg, lb):
    return _ln(DN_ALPHA * x + (1.0 + gate) * br, lg, lb)


def proj_post(a, w, x, mod, grow, lg, lb, name):
    s, k = a.shape
    ts = min(TS, s)

    def body(a_ref, w_ref, x_ref, m_ref, lg_ref, lb_ref, b_ref, o_ref):
        br = jnp.dot(a_ref[...], w_ref[...], preferred_element_type=F32)
        b_ref[...] = br
        o_ref[...] = _post_fn(x_ref[...], br, m_ref[grow:grow + 1, :], lg_ref[...], lb_ref[...])

    tile = pl.BlockSpec((ts, D), lambda i: (i, 0))
    row = pl.BlockSpec((1, D), lambda i: (0, 0))
    out = jax.ShapeDtypeStruct((s, D), F32)
    return _call(body, name=name, grid=(s // ts,),
                 in_specs=[pl.BlockSpec((ts, k), lambda i: (i, 0)), pl.BlockSpec((k, D), lambda i: (0, 0)), tile,
                           pl.BlockSpec((8, D), lambda i: (0, 0)), row, row],
                 out_specs=[tile, tile], out_shape=[out, out],
                 compiler_params=_params("arbitrary"))(a, w, x, mod, lg, lb)


def post_bwd(x, br, mod, grow, lg, lb, dy, name):
    s = x.shape[0]
    ts = min(TS, s)

    def body(x_ref, b_ref, m_ref, lg_ref, lb_ref, dy_ref, dx_ref, db_ref, dg_ref, dlg_ref, dlb_ref):
        @pl.when(pl.program_id(0) == 0)
        def _():
            dg_ref[...] = jnp.zeros_like(dg_ref)
            dlg_ref[...] = jnp.zeros_like(dlg_ref)
            dlb_ref[...] = jnp.zeros_like(dlb_ref)

        _, vjp = jax.vjp(_post_fn, x_ref[...], b_ref[...], m_ref[grow:grow + 1, :], lg_ref[...], lb_ref[...])
        dx, db, dg, dlg, dlb = vjp(dy_ref[...])
        dx_ref[...] = dx
        db_ref[...] = db.astype(BF16)
        dg_ref[...] += dg
        dlg_ref[...] += dlg
        dlb_ref[...] += dlb

    tile = pl.BlockSpec((ts, D), lambda i: (i, 0))
    row = pl.BlockSpec((1, D), lambda i: (0, 0))
    rs = jax.ShapeDtypeStruct((1, D), F32)
    return _call(body, name=name, grid=(s // ts,),
                 in_specs=[tile, tile, pl.BlockSpec((8, D), lambda i: (0, 0)), row, row, tile],
                 out_specs=[tile, tile, row, row, row],
                 out_shape=[jax.ShapeDtypeStruct((s, D), F32), jax.ShapeDtypeStruct((s, D), BF16), rs, rs, rs],
                 compiler_params=_params("arbitrary"))(x, br, mod, lg, lb, dy)


def loss_kernel(y, target):
    s = y.shape[0]
    ts = min(TS, s)

    def body(y_ref, t_ref, l_ref, dy_ref):
        @pl.when(pl.program_id(0) == 0)
        def _():
            l_ref[...] = jnp.zeros_like(l_ref)

        err = y_ref[...] - t_ref[...]
        dy_ref[...] = err * (1.0 / D)
        per_tok = jnp.mean(err * err, axis=-1, keepdims=True)
        l_ref[...] += 0.5 * jnp.sum(per_tok)

    tile = pl.BlockSpec((ts, D), lambda i: (i, 0))
    return _call(body, name="loss", grid=(s // ts,), in_specs=[tile, tile],
                 out_specs=[pl.BlockSpec((8, 128), lambda i: (0, 0)), tile],
                 out_shape=[jax.ShapeDtypeStruct((8, 128), F32), jax.ShapeDtypeStruct((s, D), F32)],
                 compiler_params=_params("arbitrary"))(y, target)


def _gm_pair(u, v, lg, lb, w0, w1, bs0, bs1):
    t = u.shape[0]
    low = _lanes((t, 2 * HD)) < HD

    def head_mean(x):
        lo = jnp.sum(jnp.where(low, x, 0.0), axis=-1, keepdims=True)
        hi = jnp.sum(jnp.where(low, 0.0, x), axis=-1, keepdims=True)
        return jnp.where(low, lo, hi) * (1.0 / HD)

    xc = v - head_mean(v)
    vn = xc * lax.rsqrt(head_mean(xc * xc) + LN_EPS) * lg + lb
    v0 = jnp.where(low, vn, 0.0)
    v1 = jnp.where(low, 0.0, vn)
    causal = _rows((GM_C, GM_C)) >= _lanes((GM_C, GM_C))
    wm0 = jnp.where(causal, w0, 0.0)
    wm1 = jnp.where(causal, w1, 0.0)
    bias = jnp.where(_lanes((GM_C, 2 * HD)) < HD, bs0, bs1)
    chunks = []
    for n in range(t // GM_C):
        rs = slice(n * GM_C, (n + 1) * GM_C)
        chunks.append(u[rs] * (_bdot(wm0, v0[rs]) + _bdot(wm1, v1[rs]) + bias))
    return jnp.concatenate(chunks, axis=0)


def gm_fwd(p, lg, lb, ws, bst):
    s = p.shape[0]
    ts = min(TS_C, s)

    def body(u_ref, v_ref, lg_ref, lb_ref, ws_ref, bs_ref, o_ref):
        for j in range(GM_H // 2):
            sl = slice(j * 2 * HD, (j + 1) * 2 * HD)
            o_ref[:, sl] = _gm_pair(u_ref[:, sl], v_ref[:, sl], lg_ref[:, sl], lb_ref[:, sl], ws_ref[2 * j],
                                    ws_ref[2 * j + 1], bs_ref[:, 2 * j:2 * j + 1],
                                    bs_ref[:, 2 * j + 1:2 * j + 2]).astype(BF16)

    full = lambda shape: pl.BlockSpec(shape, lambda i: (0,) * len(shape))
    return _call(body, name="gm_fwd", grid=(s // ts,),
                 in_specs=[pl.BlockSpec((ts, GM_W), lambda i: (i, 0)), pl.BlockSpec((ts, GM_W), lambda i: (i, 1)),
                           full((1, GM_W)), full((1, GM_W)), full((GM_H, GM_C, GM_C)), full((GM_C, GM_H))],
                 out_specs=pl.BlockSpec((ts, GM_W), lambda i: (i, 0)),
                 out_shape=jax.ShapeDtypeStruct((s, GM_W), BF16),
                 compiler_params=_params("arbitrary"))(p, p, lg, lb, ws, bst)


def gm_bwd(p, lg, lb, ws, bst, dmix):
    s = p.shape[0]
    ts = min(TS_C, s)

    def body(u_ref, v_ref, lg_ref, lb_ref, ws_ref, bs_ref, dy_ref, duv_ref, dlg_ref, dlb_ref, dws_ref, dbs_ref):
        @pl.when(pl.program_id(0) == 0)
        def _():
            dlg_ref[...] = jnp.zeros_like(dlg_ref)
            dlb_ref[...] = jnp.zeros_like(dlb_ref)
            dws_ref[...] = jnp.zeros_like(dws_ref)
            dbs_ref[...] = jnp.zeros_like(dbs_ref)

        for j in range(GM_H // 2):
            sl = slice(j * 2 * HD, (j + 1) * 2 * HD)
            _, vjp = jax.vjp(_gm_pair, u_ref[:, sl], v_ref[:, sl], lg_ref[:, sl], lb_ref[:, sl], ws_ref[2 * j],
                             ws_ref[2 * j + 1], bs_ref[:, 2 * j:2 * j + 1], bs_ref[:, 2 * j + 1:2 * j + 2])
            du, dv, dlg, dlb, dw0, dw1, dbs0, dbs1 = vjp(dy_ref[:, sl])
            duv_ref[:, sl] = du.astype(BF16)
            duv_ref[:, GM_W + j * 2 * HD:GM_W + (j + 1) * 2 * HD] = dv.astype(BF16)
            dlg_ref[:, sl] += dlg
            dlb_ref[:, sl] += dlb
            dws_ref[2 * j] += dw0
            dws_ref[2 * j + 1] += dw1
            dbs_ref[:, 2 * j:2 * j + 1] += dbs0
            dbs_ref[:, 2 * j + 1:2 * j + 2] += dbs1

    full = lambda shape: pl.BlockSpec(shape, lambda i: (0,) * len(shape))
    return _call(body, name="gm_bwd", grid=(s // ts,),
                 in_specs=[pl.BlockSpec((ts, GM_W), lambda i: (i, 0)), pl.BlockSpec((ts, GM_W), lambda i: (i, 1)),
                           full((1, GM_W)), full((1, GM_W)), full((GM_H, GM_C, GM_C)), full((GM_C, GM_H)),
                           pl.BlockSpec((ts, GM_W), lambda i: (i, 0))],
                 out_specs=[pl.BlockSpec((ts, 2 * GM_W), lambda i: (i, 0)), full((1, GM_W)), full((1, GM_W)),
                            full((GM_H, GM_C, GM_C)), full((GM_C, GM_H))],
                 out_shape=[jax.ShapeDtypeStruct((s, 2 * GM_W), BF16), jax.ShapeDtypeStruct((1, GM_W), F32),
                            jax.ShapeDtypeStruct((1, GM_W), F32), jax.ShapeDtypeStruct((GM_H, GM_C, GM_C), F32),
                            jax.ShapeDtypeStruct((GM_C, GM_H), F32)],
                 compiler_params=_params("arbitrary"))(p, p, lg, lb, ws, bst, dmix)


def _s5_prep_fn(lr, li, ldt, bt):
    dt = jnp.exp(ldt)
    er = jnp.exp(lr * dt)
    ar = er * jnp.cos(li * dt)
    ai = er * jnp.sin(li * dt)
    den = lr * lr + li * li
    nr = ar - 1.0
    cr = (nr * lr + ai * li) / den
    ci = (ai * lr - nr * li) / den
    br, bi = bt[:, :S5_N], bt[:, S5_N:]
    return ar, ai, jnp.concatenate([cr * br - ci * bi, cr * bi + ci * br], axis=1)


def s5_prep_fwd(lr, li, ldt, bt):
    def body(lr_ref, li_ref, ldt_ref, bt_ref, a_ref, bb_ref):
        ar, ai, bb = _s5_prep_fn(lr_ref[...], li_ref[...], ldt_ref[...], bt_ref[...])
        a_ref[...] = jnp.concatenate([ar, ai, jnp.zeros((6, S5_N), F32)], axis=0)
        bb_ref[...] = bb

    return _call(body, name="s5_prep_fwd",
                 out_shape=[jax.ShapeDtypeStruct((8, S5_N), F32), jax.ShapeDtypeStruct((S5_H, 2 * S5_N), F32)])(lr, li, ldt, bt)


def s5_prep_bwd(lr, li, ldt, bt, da, dbb):
    def body(lr_ref, li_ref, ldt_ref, bt_ref, da_ref, dbb_ref, dlr_ref, dli_ref, dldt_ref, dbt_ref):
        _, vjp = jax.vjp(_s5_prep_fn, lr_ref[...], li_ref[...], ldt_ref[...], bt_ref[...])
        dlr, dli, dldt, dbt = vjp((da_ref[0:1, :], da_ref[1:2, :], dbb_ref[...]))
        dlr_ref[...] = dlr
        dli_ref[...] = dli
        dbt_ref[...] = dbt
        group = (_rows((S5_N, 128)) // S5_P == _lanes((S5_N, 128))).astype(F32)
        dldt_ref[...] = jnp.dot(jnp.broadcast_to(dldt, (8, S5_N)), group, precision=lax.Precision.HIGHEST,
                                preferred_element_type=F32)[0:1, :]

    r = jax.ShapeDtypeStruct((1, S5_N), F32)
    return _call(body, name="s5_prep_bwd",
                 out_shape=[r, r, jax.ShapeDtypeStruct((1, 128), F32),
                            jax.ShapeDtypeStruct((S5_H, 2 * S5_N), F32)])(lr, li, ldt, bt, da, dbb)


def _s5_out_fn(x, u, cbd, drow, wg, bg):
    y = _bdot_nt(x[:, :S5_N], cbd[:, :S5_N]) - _bdot_nt(x[:, S5_N:], cbd[:, S5_N:]) + drow * u
    y = jax.nn.gelu(y)
    gate = _bdot_nt(y, wg) + bg
    return y * jax.nn.sigmoid(gate)


def _scan_chunk(buf, ar, ai, cr, ci, reverse):
    t = buf.shape[0]

    def local(xr, xi, rows):
        within = _rows(xr.shape) % 8
        pr, pi = ar, ai
        for d in (1, 2, 4):
            keep = within < 8 - d if reverse else within >= d
            shift = rows - d if reverse else d
            sr = jnp.where(keep, pltpu.roll(xr, shift, 0), 0.0)
            si = jnp.where(keep, pltpu.roll(xi, shift, 0), 0.0)
            xr, xi = xr + pr * sr - pi * si, xi + pr * si + pi * sr
            pr, pi = pr * pr - pi * pi, 2.0 * pr * pi
        return xr, xi

    xr, xi = local(buf[:, :S5_N], buf[:, S5_N:], t)
    buf[:, :S5_N] = xr
    buf[:, S5_N:] = xi
    edge = _rows((8, S5_N)) == (7 if reverse else 0)
    pr8, pi8 = local(jnp.where(edge, ar, 0.0), jnp.where(edge, ai, 0.0), 8)

    def group(j, c):
        g = t // 8 - 1 - j if reverse else j
        rows = pl.ds(pl.multiple_of(g * 8, 8), 8)
        gr = buf[rows, :S5_N] + pr8 * c[0] - pi8 * c[1]
        gi = buf[rows, S5_N:] + pr8 * c[1] + pi8 * c[0]
        buf[rows, :S5_N] = gr
        buf[rows, S5_N:] = gi
        return (gr[0:1, :], gi[0:1, :]) if reverse else (gr[7:8, :], gi[7:8, :])

    return lax.fori_loop(0, t // 8, group, (cr, ci), unroll=4)


def s5_fwd(p, arow, bbd, cbd, drow, wg, bg):
    s = p.shape[0]
    t = min(T_S5, s)

    def body(u_ref, a_ref, bbd_ref, cbd_ref, d_ref, wg_ref, bg_ref, y_ref, st_ref, carry):
        @pl.when(pl.program_id(0) == 0)
        def _():
            carry[...] = jnp.zeros_like(carry)

        u = u_ref[...]
        st_ref[...] = _nn(u, bbd_ref[...])
        cr, ci = _scan_chunk(st_ref, a_ref[0:1, :], a_ref[1:2, :], carry[0:1, :S5_N], carry[0:1, S5_N:], False)
        carry[0:1, :S5_N] = cr
        carry[0:1, S5_N:] = ci
        y_ref[...] = _s5_out_fn(st_ref[...], u, cbd_ref[...], d_ref[...], wg_ref[...], bg_ref[...]).astype(BF16)

    full = lambda shape: pl.BlockSpec(shape, lambda i: (0,) * len(shape))
    return _call(body, name="s5_fwd", grid=(s // t,),
                 in_specs=[pl.BlockSpec((t, S5_W), lambda i: (i, 2)), full((8, S5_N)), full((S5_W, 2 * S5_N)),
                           full((S5_W, 2 * S5_N)), full((1, S5_W)), full((S5_W, S5_W)), full((1, S5_W))],
                 out_specs=[pl.BlockSpec((t, S5_W), lambda i: (i, 0)), pl.BlockSpec((t, 2 * S5_N), lambda i: (i, 0))],
                 out_shape=[jax.ShapeDtypeStruct((s, S5_W), BF16), jax.ShapeDtypeStruct((s, 2 * S5_N), F32)],
                 scratch_shapes=[pltpu.VMEM((8, 2 * S5_N), F32)],
                 compiler_params=_params("arbitrary"))(p, arow, bbd, cbd, drow, wg, bg)


def s5_bwd(p, st, arow, bbd, cbd, drow, wg, bg, dmix):
    s = p.shape[0]
    t = min(T_S5, s)
    nc = s // t

    def body(u_ref, st_ref, prev_ref, a_ref, bbd_ref, cbd_ref, d_ref, wg_ref, bg_ref, dy_ref,
             du_ref, da_ref, dbbd_ref, dcbd_ref, dd_ref, dwg_ref, dbg_ref, carry, gbuf):
        i = pl.program_id(0)

        @pl.when(i == 0)
        def _():
            carry[...] = jnp.zeros_like(carry)
            for r in (da_ref, dbbd_ref, dcbd_ref, dd_ref, dwg_ref, dbg_ref):
                r[...] = jnp.zeros_like(r)

        u = u_ref[...]
        x = st_ref[...]
        _, vjp = jax.vjp(_s5_out_fn, x, u, cbd_ref[...], d_ref[...], wg_ref[...], bg_ref[...])
        dx, du1, dcbd, dd, dwg, dbg = vjp(dy_ref[...])
        gbuf[...] = dx
        cr, ci = _scan_chunk(gbuf, a_ref[0:1, :], -a_ref[1:2, :], carry[0:1, :S5_N], carry[0:1, S5_N:], True)
        carry[0:1, :S5_N] = cr
        carry[0:1, S5_N:] = ci
        gr, gi = gbuf[:, :S5_N], gbuf[:, S5_N:]
        rid = _rows((t, S5_N))
        has_prev = (i < nc - 1).astype(F32)
        top_r = prev_ref[7:8, :S5_N] * has_prev
        top_i = prev_ref[7:8, S5_N:] * has_prev
        xpr = jnp.where(rid == 0, top_r, pltpu.roll(x[:, :S5_N], 1, 0))
        xpi = jnp.where(rid == 0, top_i, pltpu.roll(x[:, S5_N:], 1, 0))
        da_ref[0:1, :] += jnp.sum(xpr * gr + xpi * gi, axis=0, keepdims=True)
        da_ref[1:2, :] += jnp.sum(xpr * gi - xpi * gr, axis=0, keepdims=True)
        g = jnp.concatenate([gr, gi], axis=1)
        dbbd_ref[...] += _tn(u, g)
        du_ref[...] = (_nt(g, bbd_ref[...]) + du1).astype(BF16)
        dcbd_ref[...] += dcbd
        dd_ref[...] += dd
        dwg_ref[...] += dwg
        dbg_ref[...] += dbg

    full = lambda shape: pl.BlockSpec(shape, lambda i: (0,) * len(shape))
    rev = lambda col: (lambda i: (nc - 1 - i, col))
    prev_map = lambda i: (jnp.maximum((nc - 1 - i) * (t // 8) - 1, 0), 0)
    return _call(body, name="s5_bwd", grid=(nc,),
                 in_specs=[pl.BlockSpec((t, S5_W), rev(2)), pl.BlockSpec((t, 2 * S5_N), rev(0)),
                           pl.BlockSpec((8, 2 * S5_N), prev_map), full((8, S5_N)), full((S5_W, 2 * S5_N)),
                           full((S5_W, 2 * S5_N)), full((1, S5_W)), full((S5_W, S5_W)), full((1, S5_W)),
                           pl.BlockSpec((t, S5_W), rev(1))],
                 out_specs=[pl.BlockSpec((t, S5_W), rev(0)), full((8, S5_N)), full((S5_W, 2 * S5_N)),
                            full((S5_W, 2 * S5_N)), full((1, S5_W)), full((S5_W, S5_W)), full((1, S5_W))],
                 out_shape=[jax.ShapeDtypeStruct((s, S5_W), BF16), jax.ShapeDtypeStruct((8, S5_N), F32),
                            jax.ShapeDtypeStruct((S5_W, 2 * S5_N), F32), jax.ShapeDtypeStruct((S5_W, 2 * S5_N), F32),
                            jax.ShapeDtypeStruct((1, S5_W), F32), jax.ShapeDtypeStruct((S5_W, S5_W), F32),
                            jax.ShapeDtypeStruct((1, S5_W), F32)],
                 scratch_shapes=[pltpu.VMEM((8, 2 * S5_N), F32), pltpu.VMEM((t, 2 * S5_N), F32)],
                 compiler_params=_params("arbitrary"))(p, st, st, arow, bbd, cbd, drow, wg, bg, dmix)


def _cum_steps(s):
    return int(math.ceil(math.log2(s)))


V_BLK = (2 * GM_W + S5_W + 2 * FX_W) // 128


AUG = 2 * HD
BIAS_COL = HD
FQ_COL = HD + 3
PAIR_W = 256


def _split3(f):
    hi = f.astype(BF16).astype(F32)
    r = f - hi
    mid = r.astype(BF16).astype(F32)
    lo = (r - mid).astype(BF16).astype(F32)
    return hi, mid, lo


def fox_prep(p, bf):
    s = p.shape[0]
    ts = min(TS, s)
    scale = HD ** -0.5

    def body(q0_ref, q1_ref, k0_ref, k1_ref, v0_ref, v1_ref, f_ref, bf_ref,
             qa_ref, ka_ref, qat_ref, kat_ref, vt_ref, carry):
        @pl.when(pl.program_id(0) == 0)
        def _():
            carry[...] = jnp.zeros_like(carry)

        lane = _lanes((ts, 128))
        lf = jax.nn.log_sigmoid(f_ref[...] + bf_ref[...])
        acc = jnp.where(lane < FX_H, lf, 0.0)
        rid = _rows((ts, 128))
        for k in range(_cum_steps(ts)):
            d = 1 << k
            acc = acc + jnp.where(rid >= d, pltpu.roll(acc, d, 0), 0.0)
        acc = acc + carry[0:1, :]
        carry[0:1, :] = acc[ts - 1:ts, :]

        low = lane < HD
        for h in range(FX_H):
            blk, pos = divmod(h, 4)
            pair = slice((pos // 2) * 128, (pos // 2) * 128 + 128)
            hi, mid, lo = _split3(acc[:, h:h + 1])
            one = jnp.ones((ts, 1), F32)

            def augment(ref, cols):
                x = ref[:, pair]
                if pos % 2:
                    x = pltpu.roll(x, HD, 1)
                out = jnp.where(low, x, 0.0)
                for j, cval in enumerate(cols):
                    out = jnp.where(lane == HD + j, cval, out)
                return out

            qa = augment((q0_ref, q1_ref)[blk], (one, one, one, hi, mid, lo))
            qa = jnp.where(low, qa * scale, qa)
            ka = augment((k0_ref, k1_ref)[blk], (-hi, -mid, -lo, one, one, one))
            cs = slice(h * AUG, (h + 1) * AUG)
            qa_ref[:, cs] = qa.astype(BF16)
            ka_ref[:, cs] = ka.astype(BF16)
            qat_ref[cs, :] = jnp.transpose(qa).astype(BF16)
            kat_ref[cs, :] = jnp.transpose(ka).astype(BF16)
        for j in range(FX_H // 2):
            vref = (v0_ref, v1_ref)[j // 2]
            vt_ref[j * 128:(j + 1) * 128, :] = jnp.transpose(vref[:, (j % 2) * 128:(j % 2) * 128 + 128]).astype(BF16)

    q_blk = (2 * GM_W + S5_W) // PAIR_W
    col = lambda b: pl.BlockSpec((ts, PAIR_W), lambda i: (i, b))
    wide = FX_H * AUG
    return _call(body, name="fox_prep", grid=(s // ts,),
                 in_specs=[col(q_blk), col(q_blk + 1), col(q_blk + 2), col(q_blk + 3), col(q_blk + 4), col(q_blk + 5),
                           pl.BlockSpec((ts, 128), lambda i: (i, FF_COL // 128)), pl.BlockSpec((1, 128), lambda i: (0, 0))],
                 out_specs=[pl.BlockSpec((ts, wide), lambda i: (i, 0)), pl.BlockSpec((ts, wide), lambda i: (i, 0)),
                            pl.BlockSpec((wide, ts), lambda i: (0, i)), pl.BlockSpec((wide, ts), lambda i: (0, i)),
                            pl.BlockSpec((FX_W, ts), lambda i: (0, i))],
                 out_shape=[jax.ShapeDtypeStruct((s, wide), BF16), jax.ShapeDtypeStruct((s, wide), BF16),
                            jax.ShapeDtypeStruct((wide, s), BF16), jax.ShapeDtypeStruct((wide, s), BF16),
                            jax.ShapeDtypeStruct((FX_W, s), BF16)],
                 scratch_shapes=[pltpu.VMEM((8, 128), F32)],
                 compiler_params=_params("arbitrary"))(p, p, p, p, p, p, p, bf)


def fox_prep_grad(p, bf, dfq, dfk):
    s = p.shape[0]
    ts = min(TS, s)
    ns = s // ts

    def body(f_ref, bf_ref, dfq_ref, dfk_ref, df_ref, dbf_ref, carry):
        @pl.when(pl.program_id(0) == 0)
        def _():
            carry[...] = jnp.zeros_like(carry)
            dbf_ref[...] = jnp.zeros_like(dbf_ref)

        lane = _lanes((ts, 128))
        acc = jnp.zeros((ts, 128), F32)
        for h in range(FX_H):
            c = (h // 2) * 128 + h % 2
            acc = jnp.where(lane == h, dfq_ref[:, c:c + 1] + dfk_ref[:, c:c + 1], acc)
        rid = _rows((ts, 128))
        for k in range(_cum_steps(ts)):
            d = 1 << k
            acc = acc + jnp.where(rid < ts - d, pltpu.roll(acc, ts - d, 0), 0.0)
        acc = acc + carry[0:1, :]
        carry[0:1, :] = acc[0:1, :]
        z = f_ref[...] + bf_ref[...]
        df = jnp.where(lane < FX_H, acc * jax.nn.sigmoid(-z), 0.0)
        df_ref[...] = df.astype(BF16)
        dbf_ref[...] += jnp.sum(df, axis=0, keepdims=True)

    rev = lambda i: (ns - 1 - i, 0)
    return _call(body, name="fox_prep_grad", grid=(ns,),
                 in_specs=[pl.BlockSpec((ts, 128), lambda i: (ns - 1 - i, FF_COL // 128)),
                           pl.BlockSpec((1, 128), lambda i: (0, 0)),
                           pl.BlockSpec((ts, FX_W), rev), pl.BlockSpec((ts, FX_W), rev)],
                 out_specs=[pl.BlockSpec((ts, 128), rev), pl.BlockSpec((1, 128), lambda i: (0, 0))],
                 out_shape=[jax.ShapeDtypeStruct((s, 128), BF16), jax.ShapeDtypeStruct((1, 128), F32)],
                 scratch_shapes=[pltpu.VMEM((8, 128), F32)],
                 compiler_params=_params("arbitrary"))(p, bf, dfq, dfk)


def attn(qat, ka, vt):
    s = ka.shape[0]
    tq = min(TQ_FWD, s)
    nq = s // tq

    nh = ATTN_HEADS

    def body(qat_ref, ka_ref, vt_ref, o_ref, lse_ref):
        qi = pl.program_id(1)
        half = tq // 2
        lse_ref[...] = jnp.zeros_like(lse_ref)

        def update(m, l, acc, st, v):
            m_new = jnp.maximum(m, jnp.max(st, axis=0, keepdims=True))
            alpha = jnp.exp(m - m_new)
            pt = jnp.exp(st - m_new)
            return (m_new, alpha * l + jnp.sum(pt, axis=0, keepdims=True),
                    alpha * acc + jnp.dot(v, pt.astype(BF16), preferred_element_type=F32))

        def step(kj, carry):
            off = pl.multiple_of(kj * tq, tq)
            out = []
            for hh in range(nh):
                cs = slice(hh * AUG, (hh + 1) * AUG)
                st = jnp.dot(ka_ref[pl.ds(off, tq), cs], qat_ref[cs, :], preferred_element_type=F32)
                out.append(update(*carry[hh], st, vt_ref[hh * HD:(hh + 1) * HD, pl.ds(off, tq)]))
            return tuple(out)

        def diagonal(carry):
            off = pl.multiple_of(qi * tq, tq)
            off2 = pl.multiple_of(off + half, half)
            lower = _rows((half, tq)) <= _lanes((half, tq))
            out = []
            for hh in range(nh):
                cs = slice(hh * AUG, (hh + 1) * AUG)
                hs = slice(hh * HD, (hh + 1) * HD)
                st = jnp.dot(ka_ref[pl.ds(off, half), cs], qat_ref[cs, :], preferred_element_type=F32)
                m, l, acc = update(*carry[hh], jnp.where(lower, st, NEG_INF), vt_ref[hs, pl.ds(off, half)])
                st = jnp.dot(ka_ref[pl.ds(off2, half), cs], qat_ref[cs, half:], preferred_element_type=F32)
                mr, lr, ar = update(m[:, half:], l[:, half:], acc[:, half:], jnp.where(lower[:, :half], st, NEG_INF),
                                    vt_ref[hs, pl.ds(off2, half)])
                out.append((jnp.concatenate([m[:, :half], mr], axis=1), jnp.concatenate([l[:, :half], lr], axis=1),
                            jnp.concatenate([acc[:, :half], ar], axis=1)))
            return tuple(out)

        init = tuple((jnp.full((1, tq), NEG_INF, F32), jnp.zeros((1, tq), F32), jnp.zeros((HD, tq), F32))
                     for _ in range(nh))
        carry = diagonal(lax.fori_loop(0, qi, step, init))
        for hh in range(nh):
            m, l, _ = carry[hh]
            lse_ref[hh // 2, hh % 2:hh % 2 + 1, :] = m + jnp.log(l)
        for j in range(nh // 2):
            pair = jnp.concatenate([carry[2 * j][2] / carry[2 * j][1], carry[2 * j + 1][2] / carry[2 * j + 1][1]], axis=0)
            o_ref[:, j * 128:(j + 1) * 128] = jnp.transpose(pair).astype(BF16)

    return _call(body, name="attn", grid=(FX_H // nh, nq),
                 in_specs=[pl.BlockSpec((nh * AUG, tq), lambda h, i: (h, i)),
                           pl.BlockSpec((s, nh * AUG), lambda h, i: (0, h)),
                           pl.BlockSpec((nh * HD, s), lambda h, i: (h, 0))],
                 out_specs=[pl.BlockSpec((tq, nh * HD), lambda h, i: (i, h)),
                            pl.BlockSpec((nh // 2, 8, tq), lambda h, i: (h, 0, i))],
                 out_shape=[jax.ShapeDtypeStruct((s, FX_W), BF16), jax.ShapeDtypeStruct((FX_H // 2, 8, s), F32)],
                 compiler_params=_params("arbitrary", "arbitrary"))(qat, ka, vt)


def attn_grad(qa, qat, ka, kat, p, o, lse, dmix):
    s = qa.shape[0]
    tq = min(TQ, s)
    nq = s // tq
    scale = HD ** -0.5

    def body(qa_ref, qat_ref, ka_ref, kat_ref, v_ref, o_ref, lse_ref, do_ref,
             dq_ref, dk_ref, dv_ref, dfq_ref, dfk_ref, dot_scr, delta, dqt):
        kj = pl.program_id(1)
        lane = _lanes((tq, 128))
        low = lane < HD

        @pl.when(kj == 0)
        def _():
            dqt[...] = jnp.zeros_like(dqt)
            delta[...] = jnp.zeros_like(delta)

            def prep(c, _):
                rows = pl.ds(pl.multiple_of(c * tq, tq), tq)
                do = do_ref[rows, :]
                pt = jnp.transpose(do * o_ref[rows, :].astype(F32))
                delta[0:1, rows] = jnp.sum(pt[:HD], axis=0, keepdims=True)
                delta[1:2, rows] = jnp.sum(pt[HD:], axis=0, keepdims=True)
                dot_scr[:, rows] = jnp.transpose(do).astype(BF16)
                return 0

            lax.fori_loop(0, nq, prep, 0)

        v = v_ref[...]
        vms = [jnp.where(low, v, 0.0).astype(BF16), jnp.where(low, 0.0, v).astype(BF16)]

        def block(carry, qoff, qn, kn, mask):
            cols = pl.ds(pl.multiple_of(qoff, half), qn)
            do = do_ref[cols, :].astype(BF16)
            out = []
            for hh in range(2):
                cs = slice(hh * AUG, (hh + 1) * AUG)
                dka, dv = carry[hh]
                st = jnp.dot(ka_ref[:kn, cs], qat_ref[cs, cols], preferred_element_type=F32)
                if mask is not None:
                    st = jnp.where(mask, st, NEG_INF)
                pt = jnp.exp(st - lse_ref[0, hh:hh + 1, cols])
                ddv = jnp.dot(pt.astype(BF16), do, preferred_element_type=F32)
                dpt = jnp.dot(vms[hh][:kn], dot_scr[:, cols], preferred_element_type=F32)
                dsb = (pt * (dpt - delta[hh:hh + 1, cols])).astype(BF16)
                ddka = jnp.dot(dsb, qa_ref[cols, cs], preferred_element_type=F32)
                dqt[hh, :, cols] += jnp.dot(kat_ref[cs, :kn], dsb, preferred_element_type=F32)
                if kn == tq:
                    out.append((dka + ddka, dv + ddv))
                else:
                    out.append((jnp.concatenate([dka[:kn] + ddka, dka[kn:]], axis=0),
                                jnp.concatenate([dv[:kn] + ddv, dv[kn:]], axis=0)))
            return tuple(out)

        half = tq // 2
        koff = kj * tq
        init = tuple((jnp.zeros((tq, AUG), F32), jnp.zeros((tq, 128), F32)) for _ in range(2))
        carry = block(init, koff + half, half, tq, _rows((tq, half)) <= _lanes((tq, half)) + half)
        carry = block(carry, koff, half, half, _rows((half, half)) <= _lanes((half, half)))
        carry = lax.fori_loop(kj + 1, nq, lambda qi, c: block(c, qi * tq, tq, tq, None), carry)
        dks = [carry[0][0], carry[1][0]]
        dvs = [carry[0][1], carry[1][1]]
        dv_ref[...] = jnp.where(low, dvs[0], dvs[1]).astype(BF16)
        dk_ref[...] = jnp.where(low, dks[0], pltpu.roll(dks[1], HD, 1)).astype(BF16)
        dfk_ref[...] = jnp.where(lane == 0, -dks[0][:, BIAS_COL:BIAS_COL + 1],
                                 jnp.where(lane == 1, -dks[1][:, BIAS_COL:BIAS_COL + 1], 0.0))

        @pl.when(kj == nq - 1)
        def _():
            def finish(c, _):
                rows = pl.ds(pl.multiple_of(c * tq, tq), tq)
                t0 = jnp.transpose(dqt[0, :, rows])
                t1 = jnp.transpose(dqt[1, :, rows])
                dq_ref[rows, :] = (jnp.where(low, t0, pltpu.roll(t1, HD, 1)) * scale).astype(BF16)
                dfq_ref[rows, :] = jnp.where(lane == 0, t0[:, FQ_COL:FQ_COL + 1],
                                             jnp.where(lane == 1, t1[:, FQ_COL:FQ_COL + 1], 0.0))
                return 0

            lax.fori_loop(0, nq, finish, 0)

    seq128 = lambda blk: pl.BlockSpec((s, 128), lambda h, j: (0, blk + h))
    tile128 = pl.BlockSpec((tq, 128), lambda h, j: (j, h))
    out_b = jax.ShapeDtypeStruct((s, FX_W), BF16)
    out_f = jax.ShapeDtypeStruct((s, FX_W), F32)
    return _call(body, name="attn_grad", grid=(FX_H // 2, nq),
                 in_specs=[pl.BlockSpec((s, 2 * AUG), lambda h, j: (0, h)), pl.BlockSpec((2 * AUG, s), lambda h, j: (h, 0)),
                           pl.BlockSpec((tq, 2 * AUG), lambda h, j: (j, h)), pl.BlockSpec((2 * AUG, tq), lambda h, j: (h, j)),
                           pl.BlockSpec((tq, 128), lambda h, j: (j, V_BLK + h)), seq128(0),
                           pl.BlockSpec((1, 8, s), lambda h, j: (h, 0, 0)), seq128(4)],
                 out_specs=[seq128(0), tile128, tile128, seq128(0), tile128],
                 out_shape=[out_b, out_b, out_b, out_f, out_f],
                 scratch_shapes=[pltpu.VMEM((128, s), BF16), pltpu.VMEM((8, s), F32), pltpu.VMEM((2, AUG, s), F32)],
                 compiler_params=_params("arbitrary", "arbitrary"))(qa, qat, ka, kat, p, o, lse, dmix)


def _shift_down(a, prev8, k):
    r = pltpu.roll(a, k, 0)
    top = jnp.where(_rows(prev8.shape) < k, pltpu.roll(prev8, k, 0), r[0:8])
    return jnp.concatenate([top, r[8:]], axis=0)


def _shift_up(a, next8, k):
    t = a.shape[0]
    r = pltpu.roll(a, t - k, 0)
    bot = jnp.where(_rows(next8.shape) >= 8 - k, pltpu.roll(next8, 8 - k, 0), r[t - 8:t])
    return jnp.concatenate([r[:t - 8], bot], axis=0)


def _conv(a, prev8, cw, cb):
    return cb + cw[0:1, :] * _shift_down(a, prev8, 2) + cw[1:2, :] * _shift_down(a, prev8, 1) + cw[2:3, :] * a


GELU_K0 = math.sqrt(2.0 / math.pi)
GELU_K1 = GELU_K0 * 0.044715


def _gelu_parts(c):
    c2 = c * c
    return c2, 0.5 + 0.5 * jnp.tanh(c * (GELU_K0 + GELU_K1 * c2))


def conv_fwd(up, cw, cb):
    s = up.shape[0]
    ts = min(TS_C, s)

    def body(a_ref, g_ref, cw_ref, cb_ref, o_ref, c_ref, halo):
        @pl.when(pl.program_id(0) == 0)
        def _():
            halo[...] = jnp.zeros_like(halo)

        a = a_ref[...]
        c = _conv(a, halo[...], cw_ref[...], cb_ref[...])
        _, h = _gelu_parts(c)
        c_ref[...] = c
        o_ref[...] = (c * h * g_ref[...]).astype(BF16)
        halo[...] = a[ts - 8:ts, :]

    tile = pl.BlockSpec((ts, DFF), lambda i: (i, 0))
    return _call(body, name="conv_fwd", grid=(s // ts,),
                 in_specs=[tile, pl.BlockSpec((ts, DFF), lambda i: (i, 1)),
                           pl.BlockSpec((3, DFF), lambda i: (0, 0)), pl.BlockSpec((1, DFF), lambda i: (0, 0))],
                 out_specs=[tile, tile],
                 out_shape=[jax.ShapeDtypeStruct((s, DFF), BF16), jax.ShapeDtypeStruct((s, DFF), F32)],
                 scratch_shapes=[pltpu.VMEM((8, DFF), F32)],
                 compiler_params=_params("arbitrary"))(up, up, cw, cb)


def conv_bwd(up, c, cw, dact):
    s = up.shape[0]
    ts = min(TS_C, s)
    ns = s // ts

    def body(a_ref, g_ref, c_ref, cw_ref, dact_ref, dup_ref, dcw_ref, dcb_ref, halo):
        @pl.when(pl.program_id(0) == 0)
        def _():
            halo[...] = jnp.zeros_like(halo)
            dcw_ref[...] = jnp.zeros_like(dcw_ref)
            dcb_ref[...] = jnp.zeros_like(dcb_ref)

        a = a_ref[...]
        cw = cw_ref[...]
        cv = c_ref[...]
        dact = dact_ref[...]
        c2, h = _gelu_parts(cv)
        dup_ref[:, DFF:] = (dact * (cv * h)).astype(BF16)
        dgel = h + cv * (2.0 * h * (1.0 - h)) * (GELU_K0 + 3.0 * GELU_K1 * c2)
        dc = dact * g_ref[...] * dgel
        up1 = _shift_up(dc, halo[...], 1)
        up2 = _shift_up(dc, halo[...], 2)
        dup_ref[:, :DFF] = (cw[2:3, :] * dc + cw[1:2, :] * up1 + cw[0:1, :] * up2).astype(BF16)
        dcw_ref[0:1, :] += jnp.sum(a * up2, axis=0, keepdims=True)
        dcw_ref[1:2, :] += jnp.sum(a * up1, axis=0, keepdims=True)
        dcw_ref[2:3, :] += jnp.sum(a * dc, axis=0, keepdims=True)
        dcb_ref[...] += jnp.sum(dc, axis=0, keepdims=True)
        halo[...] = dc[0:8, :]

    rev = lambda col: (lambda i: (ns - 1 - i, col))
    return _call(body, name="conv_bwd", grid=(ns,),
                 in_specs=[pl.BlockSpec((ts, DFF), rev(0)), pl.BlockSpec((ts, DFF), rev(1)),
                           pl.BlockSpec((ts, DFF), rev(0)), pl.BlockSpec((3, DFF), lambda i: (0, 0)),
                           pl.BlockSpec((ts, DFF), rev(0))],
                 out_specs=[pl.BlockSpec((ts, 2 * DFF), rev(0)), pl.BlockSpec((3, DFF), lambda i: (0, 0)),
                            pl.BlockSpec((1, DFF), lambda i: (0, 0))],
                 out_shape=[jax.ShapeDtypeStruct((s, 2 * DFF), BF16), jax.ShapeDtypeStruct((3, DFF), F32),
                            jax.ShapeDtypeStruct((1, DFF), F32)],
                 scratch_shapes=[pltpu.VMEM((8, DFF), F32)],
                 compiler_params=_params("arbitrary"))(up, up, c, cw, dact)


def _blockdiag_expand(m):
    m4 = m.reshape(S5_H, 2, S5_G, S5_P)
    eye = jnp.eye(S5_G, dtype=bool)[:, None, None, :, None]
    return jnp.where(eye, m4[None], 0.0).reshape(S5_W, 2 * S5_N)


def _blockdiag_extract(mbd):
    m5 = mbd.reshape(S5_G, S5_H, 2, S5_G, S5_P)
    diag = jnp.stack([m5[g, :, :, g, :] for g in range(S5_G)], axis=2)
    return diag.reshape(S5_H, 2 * S5_N)


def _c_expand(c_re, c_im):
    c4 = jnp.stack([c_re, c_im], axis=2)
    eye = jnp.eye(S5_G, dtype=bool)[:, None, None, :, None]
    return jnp.where(eye, c4[:, :, :, None, :], 0.0).reshape(S5_W, 2 * S5_N)


def _c_extract(cbd):
    m5 = cbd.reshape(S5_G, S5_H, 2, S5_G, S5_P)
    d = jnp.stack([m5[g, :, :, g, :] for g in range(S5_G)], axis=0)
    return d[:, :, 0, :], d[:, :, 1, :]


def _glu_expand(w):
    eye = jnp.eye(S5_G, dtype=bool)[:, None, :, None]
    return jnp.where(eye, w[:, :, None, :], 0.0).reshape(S5_W, S5_W)


def _glu_extract(wbd):
    m4 = wbd.reshape(S5_G, S5_H, S5_G, S5_H)
    return jnp.stack([m4[g, :, g, :] for g in range(S5_G)], axis=0)


SMALL = ("b_f", "gm_ln_g", "gm_ln_b", "gm_w_s", "gm_b_s", "s5_lam_re", "s5_lam_im", "s5_log_dt", "s5_b_re", "s5_b_im",
         "s5_c_re", "s5_c_im", "s5_d", "s5_w_glu", "s5_b_glu", "ln1_g", "ln1_b", "conv_b", "ln2_g", "ln2_b")


def _layer_operands(sp, l):
    f = {}
    f["bf"] = jnp.pad(sp["b_f"][l][None, :], ((0, 0), (0, 128 - FX_H)))
    f["gm_lg"] = sp["gm_ln_g"][l].reshape(1, GM_W)
    f["gm_lb"] = sp["gm_ln_b"][l].reshape(1, GM_W)
    f["gm_ws"] = sp["gm_w_s"][l]
    f["gm_bst"] = sp["gm_b_s"][l].T
    f["lr"] = sp["s5_lam_re"][l].reshape(1, S5_N)
    f["li"] = sp["s5_lam_im"][l].reshape(1, S5_N)
    f["ldt"] = jnp.repeat(sp["s5_log_dt"][l], S5_P).reshape(1, S5_N)
    bt = lambda b: jnp.transpose(b, (2, 0, 1)).reshape(S5_H, S5_N)
    f["bt"] = jnp.concatenate([bt(sp["s5_b_re"][l]), bt(sp["s5_b_im"][l])], axis=1)
    f["cbd"] = _c_expand(sp["s5_c_re"][l], sp["s5_c_im"][l])
    f["drow"] = sp["s5_d"][l].reshape(1, S5_W)
    f["wg"] = _glu_expand(sp["s5_w_glu"][l])
    f["bg"] = sp["s5_b_glu"][l].reshape(1, S5_W)
    for n in ("ln1_g", "ln1_b", "ln2_g", "ln2_b"):
        f[n] = sp[n][l][None, :]
    f["cb"] = sp["conv_b"][l][None, :]
    return f


def layer_fwd_mix(x, mod, w_in, f):
    p = mm_mod(x, w_in, NP, "in_proj", mod, (0, 1))
    ygm = gm_fwd(p, f["gm_lg"], f["gm_lb"], f["gm_ws"], f["gm_bst"])
    arow, bbt = s5_prep_fwd(f["lr"], f["li"], f["ldt"], f["bt"])
    bbd = _blockdiag_expand(bbt)
    ys5, st = s5_fwd(p, arow, bbd, f["cbd"], f["drow"], f["wg"], f["bg"])
    qa, ka, qat, kat, vt = fox_prep(p, f["bf"])
    yfx, lse = attn(qat, ka, vt)
    mixcat = jnp.concatenate([ygm, ys5, yfx], axis=1)
    return mixcat, dict(f=f, x=x, p=p, arow=arow, bbd=bbd, st=st, qa=qa, ka=ka, qat=qat, kat=kat, yfx=yfx, lse=lse,
                        mixcat=mixcat)


def layer_fwd_rest(x, mixcat, mod, w, f, saved):
    mix, x1 = proj_post(mixcat, w["w_out"], x, mod, 2, f["ln1_g"], f["ln1_b"], "out_proj")
    up = mm_mod(x1, w["w_up"], DFF, "up_proj", mod, (3, 4))
    act, conv = conv_fwd(up, w["conv_w"], f["cb"])
    ffn, x2 = proj_post(act, w["w_down"], x1, mod, 5, f["ln2_g"], f["ln2_b"], "down_proj")
    return x2, dict(saved, mix=mix, x1=x1, up=up, conv=conv, act=act, ffn=ffn)


def layer_fwd(x, mod, w, f):
    mixcat, saved = layer_fwd_mix(x, mod, w["w_in"], f)
    return layer_fwd_rest(x, mixcat, mod, w, f, saved)


def layer_bwd_ffn(dx, sv, mod, w):
    f = sv["f"]
    dx1, dffn, dg2, dlg2, dlb2 = post_bwd(sv["x1"], sv["ffn"], mod, 5, f["ln2_g"], f["ln2_b"], dx, "post2_bwd")
    g_down = mm_tn(sv["act"], dffn, D // 2, "down_dw")
    dact = mm_nt(dffn, w["w_down"], "down_dx")
    dup, dcw, dcb = conv_bwd(sv["up"], sv["conv"], w["conv_w"], dact)
    g_up = mm_tn(dup, sv["x1"], D // 2, "up_dw", mod=mod, rows=(3, 4))
    dx1, dsh2, dsc2 = mm_nt_mod(dup, w["w_up"], sv["x1"], dx1, mod, (3, 4), "up_dx")
    return dx1, dict(w_up=g_up, w_down=g_down, conv_w=dcw), dict(dsh2=dsh2, dsc2=dsc2, dg2=dg2, conv_b=dcb[0],
                                                                 ln2_g=dlg2[0], ln2_b=dlb2[0])


def layer_bwd_mix(dx1, sv, mod, w, part):
    f = sv["f"]
    dx0, dmix, dg1, dlg1, dlb1 = post_bwd(sv["x"], sv["mix"], mod, 2, f["ln1_g"], f["ln1_b"], dx1, "post1_bwd")
    g_out = mm_tn(sv["mixcat"], dmix, D, "out_dw")
    dmc = mm_nt(dmix, w["w_out"], "out_dx")
    duv, dgm_lg, dgm_lb, dgm_ws, dgm_bst = gm_bwd(sv["p"], f["gm_lg"], f["gm_lb"], f["gm_ws"], f["gm_bst"], dmc)
    du5, da, dbbd, dcbd, dd5, dwg, dbg = s5_bwd(sv["p"], sv["st"], sv["arow"], sv["bbd"], f["cbd"], f["drow"],
                                                f["wg"], f["bg"], dmc)
    dlr, dli, dldt, dbt = s5_prep_bwd(f["lr"], f["li"], f["ldt"], f["bt"], da, _blockdiag_extract(dbbd))
    dq, dk, dv, dfq, dfk = attn_grad(sv["qa"], sv["qat"], sv["ka"], sv["kat"], sv["p"], sv["yfx"], sv["lse"], dmc)
    dff, dbf = fox_prep_grad(sv["p"], f["bf"], dfq, dfk)
    dp = jnp.concatenate([duv, du5, dq, dk, dv, dff], axis=1)
    g_in = mm_tn(dp, sv["x"], D, "in_dw", mod=mod, rows=(0, 1))
    dx, dsh1, dsc1 = mm_nt_mod(dp, w["w_in"], sv["x"], dx0, mod, (0, 1), "in_dx")

    dmod = jnp.concatenate([dsh1, dsc1, dg1, part["dsh2"], part["dsc2"], part["dg2"]], axis=0)
    dc_re, dc_im = _c_extract(dcbd)
    dbt4 = dbt.reshape(S5_H, 2, S5_G, S5_P)
    vals = dict(b_f=dbf[0, :FX_H], gm_ln_g=dgm_lg.reshape(GM_H, HD), gm_ln_b=dgm_lb.reshape(GM_H, HD),
                gm_w_s=dgm_ws, gm_b_s=dgm_bst.T, s5_lam_re=dlr.reshape(S5_G, S5_P),
                s5_lam_im=dli.reshape(S5_G, S5_P), s5_log_dt=dldt[0, :S5_G],
                s5_b_re=jnp.transpose(dbt4[:, 0], (1, 2, 0)), s5_b_im=jnp.transpose(dbt4[:, 1], (1, 2, 0)),
                s5_c_re=dc_re, s5_c_im=dc_im, s5_d=dd5.reshape(S5_G, S5_H), s5_w_glu=_glu_extract(dwg),
                s5_b_glu=dbg.reshape(S5_G, S5_H), ln1_g=dlg1[0], ln1_b=dlb1[0], conv_b=part["conv_b"],
                ln2_g=part["ln2_g"], ln2_b=part["ln2_b"])
    return dx, dict(w_in=g_in, w_out=g_out), vals, dmod


def layer_bwd(dx, sv, mod, w):
    dx1, g_ffn, part = layer_bwd_ffn(dx, sv, mod, w)
    dx, g_mix, vals, dmod = layer_bwd_mix(dx1, sv, mod, w, part)
    return dx, dict(g_ffn, **g_mix), vals, dmod


def local_step(x, target, mods, big, sp):
    saved = []
    for l in range(DEPTH):
        x, sv = layer_fwd(x, mods[l], big[l], _layer_operands(sp, l))
        saved.append(sv)
    loss_tile, dx = loss_kernel(x, target)
    gbig, vals, dmods = [None] * DEPTH, [None] * DEPTH, [None] * DEPTH
    for l in reversed(range(DEPTH)):
        dx, gbig[l], vals[l], dmods[l] = layer_bwd(dx, saved[l], mods[l], big[l])
    gsm = {n: jnp.stack([v[n] for v in vals]) for n in SMALL}
    return loss_tile, dx, gbig, gsm, jnp.stack(dmods)


def _my_index():
    return 4 * lax.axis_index("x") + 2 * lax.axis_index("y") + lax.axis_index("c")


def exchange(tensors, scatter, name):
    n = len(tensors)

    def body(*refs):
        ins, outs = refs[:n], refs[n:2 * n]
        send_sems, recv_sems, local_sems = refs[2 * n:]
        x, y, c = lax.axis_index("x"), lax.axis_index("y"), lax.axis_index("c")
        me = 4 * x + 2 * y + c
        local = []
        for t in range(n):
            cp = pltpu.make_async_copy(ins[t].at[me] if scatter else ins[t], outs[t].at[me], local_sems.at[t])
            cp.start()
            local.append(cp)
        remote = []
        for m in range(1, NDEV):
            px = 1 - x if m & 4 else x
            py = 1 - y if m & 2 else y
            pc = 1 - c if m & 1 else c
            peer = 4 * px + 2 * py + pc
            for t in range(n):
                k = t * (NDEV - 1) + m - 1
                cp = pltpu.make_async_remote_copy(
                    src_ref=ins[t].at[peer] if scatter else ins[t], dst_ref=outs[t].at[me],
                    send_sem=send_sems.at[k], recv_sem=recv_sems.at[k],
                    device_id=(px, py, pc), device_id_type=MESH_IDS)
                cp.start()
                remote.append(cp)
        for cp in remote:
            cp.wait()
        for cp in local:
            cp.wait()

    hbm = pl.BlockSpec(memory_space=pltpu.HBM)
    out_shape = [jax.ShapeDtypeStruct(t.shape if scatter else (NDEV,) + t.shape, t.dtype) for t in tensors]
    return _call(body, name=name, in_specs=[hbm] * n, out_specs=[hbm] * n, out_shape=out_shape,
                 scratch_shapes=[pltpu.SemaphoreType.DMA((n * (NDEV - 1),)), pltpu.SemaphoreType.DMA((n * (NDEV - 1),)),
                                 pltpu.SemaphoreType.DMA((n,))])(*tensors)


def _peers():
    x, y, c = lax.axis_index("x"), lax.axis_index("y"), lax.axis_index("c")
    out = []
    for m in range(1, NDEV):
        px = 1 - x if m & 4 else x
        py = 1 - y if m & 2 else y
        pc = 1 - c if m & 1 else c
        out.append(((px, py, pc), 4 * px + 2 * py + pc))
    return 4 * x + 2 * y + c, out


def _split_copies(v_refs, land_refs, send_sems, recv_sems, scatter):
    me, peers = _peers()
    return [pltpu.make_async_remote_copy(
        src_ref=v_ref.at[idx] if scatter else v_ref, dst_ref=land_ref.at[me],
        send_sem=send_sems.at[t * (NDEV - 1) + k], recv_sem=recv_sems.at[t * (NDEV - 1) + k],
        device_id=pos, device_id_type=MESH_IDS)
        for t, (v_ref, land_ref) in enumerate(zip(v_refs, land_refs)) for k, (pos, idx) in enumerate(peers)]


_HBM_SPEC = pl.BlockSpec(memory_space=pltpu.HBM)
_SEM_SPEC = pl.BlockSpec(memory_space=pltpu.SEMAPHORE)
_SPLIT_EFFECT = pltpu.SideEffectType.DATAFLOW_SIDE_EFFECTING


def exchange_start(tensors, scatter, name):
    n = len(tensors)
    land_shapes = [t.shape if scatter else (NDEV,) + t.shape for t in tensors]

    def body(*refs):
        v_refs, land_refs = refs[:n], refs[n:2 * n]
        send_sems, recv_sems = refs[2 * n], refs[2 * n + 1]
        token = refs[-1]
        for cp in _split_copies(v_refs, land_refs, send_sems, recv_sems, scatter):
            cp.start()
        token[...] = jnp.zeros_like(token)

    sems = pltpu.SemaphoreType.DMA((n * (NDEV - 1),))
    out = _call(
        body, name=name,
        out_shape=(sems, sems, *[pltpu.HBM(t.shape, t.dtype) for t in tensors],
                   *[pltpu.HBM(s, t.dtype) for s, t in zip(land_shapes, tensors)], jax.ShapeDtypeStruct((8, 128), F32)),
        in_specs=(_HBM_SPEC,) * (2 * n),
        out_specs=(_SEM_SPEC, _SEM_SPEC) + (_HBM_SPEC,) * (2 * n) + (pl.BlockSpec(memory_space=pltpu.VMEM),),
        input_output_aliases={i: i + 2 for i in range(2 * n)},
        compiler_params=pltpu.CompilerParams(has_side_effects=_SPLIT_EFFECT),
    )(*[pltpu.with_memory_space_constraint(t, pltpu.HBM) for t in tensors],
      *[pltpu.with_memory_space_constraint(lax.empty(s, t.dtype), pltpu.HBM) for s, t in zip(land_shapes, tensors)])
    return out[0], out[1], list(out[2:2 + n]), list(out[2 + n:2 + 2 * n]), out[-1]


def exchange_wait(started, after, scatter, name):
    send_sems, recv_sems, v_thru, land_thru, _ = started
    n = len(v_thru)

    def body(*refs):
        v_refs, land_refs = refs[:n], refs[n:2 * n]
        for cp in _split_copies(v_refs, land_refs, refs[2 * n], refs[2 * n + 1], scatter):
            cp.wait_send()
            cp.wait_recv()

    out = _call(
        body, name=name,
        out_shape=tuple(pltpu.HBM(t.shape, t.dtype) for t in v_thru + land_thru),
        in_specs=(_HBM_SPEC,) * (2 * n) + (_SEM_SPEC, _SEM_SPEC, pl.BlockSpec(memory_space=pl.ANY)),
        out_specs=(_HBM_SPEC,) * (2 * n), input_output_aliases={i: i for i in range(2 * n)},
        compiler_params=pltpu.CompilerParams(has_side_effects=_SPLIT_EFFECT),
    )(*v_thru, *land_thru, send_sems, recv_sems, after)
    return list(out[:n]), list(out[n:])


def mod_slices(c_all, w_ada, b_loc):
    nl, _, nc = w_ada.shape

    def body(c_ref, w_ref, b_ref, o_ref):
        cv = c_ref[...]
        o_ref[0] = _nn(cv * jax.nn.sigmoid(cv), w_ref[0]) + b_ref[0]

    return _call(body, name="mod_slices", grid=(nl,),
                 in_specs=[pl.BlockSpec((NDEV, D), lambda l: (0, 0)), pl.BlockSpec((1, D, nc), lambda l: (l, 0, 0)),
                           pl.BlockSpec((1, 1, nc), lambda l: (l, 0, 0))],
                 out_specs=pl.BlockSpec((1, NDEV, nc), lambda l: (l, 0, 0)),
                 out_shape=jax.ShapeDtypeStruct((nl, NDEV, nc), F32),
                 compiler_params=_params("arbitrary"))(c_all, w_ada, b_loc.reshape(nl, 1, nc))


def ada_grad(c_all, dm_loc):
    nl, _, nc = dm_loc.shape

    def body(c_ref, d_ref, o_ref):
        cv = c_ref[...]
        o_ref[0] = _tn(cv * jax.nn.sigmoid(cv), d_ref[0])

    return _call(body, name="ada_grad", grid=(nl,),
                 in_specs=[pl.BlockSpec((NDEV, D), lambda l: (0, 0)), pl.BlockSpec((1, NDEV, nc), lambda l: (l, 0, 0))],
                 out_specs=pl.BlockSpec((1, D, nc), lambda l: (l, 0, 0)),
                 out_shape=jax.ShapeDtypeStruct((nl, D, nc), F32),
                 compiler_params=_params("arbitrary"))(c_all, dm_loc)


def sum_chunks(chunks):
    r = chunks.shape[1]

    def body(c_ref, o_ref):
        acc = c_ref[0]
        for i in range(1, NDEV):
            acc = acc + c_ref[i]
        o_ref[...] = acc

    return _call(body, name="sum_chunks", out_shape=jax.ShapeDtypeStruct((r, 128), F32))(chunks)


def _row_tile(r):
    if r <= 256:
        return r
    for t in range(256, 7, -8):
        if r % t == 0:
            return t
    return r


def adamw(w, m, v, g=None, chunks=None, name="adamw"):
    r, cdim = w.shape
    tr = _row_tile(r)
    bc1 = 1.0 - ADAM_B1 ** ADAM_STEP
    bc2 = 1.0 - ADAM_B2 ** ADAM_STEP

    def body(g_ref, w_ref, m_ref, v_ref, go_ref, d_ref, mo_ref, vo_ref):
        if chunks is None:
            grad = g_ref[...]
        else:
            grad = g_ref[0].astype(F32)
            for i in range(1, NDEV):
                grad = grad + g_ref[i].astype(F32)
        mn = ADAM_B1 * m_ref[...] + (1.0 - ADAM_B1) * grad
        vn = ADAM_B2 * v_ref[...] + (1.0 - ADAM_B2) * (grad * grad)
        m_hat = mn / bc1
        v_hat = vn / bc2
        go_ref[...] = grad
        d_ref[...] = -ADAM_LR * (m_hat / (jnp.sqrt(v_hat) + ADAM_EPS) + ADAM_WD * w_ref[...])
        mo_ref[...] = mn
        vo_ref[...] = vn

    tile = pl.BlockSpec((tr, cdim), lambda i: (i, 0))
    gspec = tile if chunks is None else pl.BlockSpec((NDEV, tr, cdim), lambda i: (0, i, 0))
    shp = jax.ShapeDtypeStruct((r, cdim), F32)
    return _call(body, name=name, grid=(r // tr,), in_specs=[gspec, tile, tile, tile],
                 out_specs=[tile] * 4, out_shape=[shp] * 4,
                 compiler_params=_params("arbitrary"))(g if chunks is None else chunks, w, m, v)


def adamw_layers(w, m, v, chunks, name):
    nl, r, cdim = w.shape
    tr = _row_tile(r)
    bc1 = 1.0 - ADAM_B1 ** ADAM_STEP
    bc2 = 1.0 - ADAM_B2 ** ADAM_STEP
    outs = [lax.empty(w.shape, F32) for _ in range(4)]
    for l in range(nl):
        def body(g_ref, w_ref, m_ref, v_ref, p0, p1, p2, p3, go_ref, d_ref, mo_ref, vo_ref):
            grad = g_ref[0].astype(F32)
            for i in range(1, NDEV):
                grad = grad + g_ref[i].astype(F32)
            mn = ADAM_B1 * m_ref[...] + (1.0 - ADAM_B1) * grad
            vn = ADAM_B2 * v_ref[...] + (1.0 - ADAM_B2) * (grad * grad)
            go_ref[...] = grad
            d_ref[...] = -ADAM_LR * ((mn / bc1) / (jnp.sqrt(vn / bc2) + ADAM_EPS) + ADAM_WD * w_ref[...])
            mo_ref[...] = mn
            vo_ref[...] = vn

        tile = pl.BlockSpec((None, tr, cdim), lambda i, l=l: (l, i, 0))
        whole = pl.BlockSpec(memory_space=pl.ANY)
        outs = _call(body, name=f"{name}_{l}", grid=(r // tr,),
                     in_specs=[pl.BlockSpec((NDEV, tr, cdim), lambda i: (0, i, 0)), tile, tile, tile] + [whole] * 4,
                     out_specs=[tile] * 4, out_shape=[jax.ShapeDtypeStruct(w.shape, F32)] * 4,
                     input_output_aliases={4: 0, 5: 1, 6: 2, 7: 3},
                     compiler_params=_params("arbitrary"))(chunks[l], w, m, v, *outs)
    return outs


WEIGHTS = ("w_ada", "b_ada", "w_in", "b_f", "gm_ln_g", "gm_ln_b", "gm_w_s", "gm_b_s", "s5_lam_re", "s5_lam_im",
           "s5_log_dt", "s5_b_re", "s5_b_im", "s5_c_re", "s5_c_im", "s5_d", "s5_w_glu", "s5_b_glu", "w_out", "ln1_g",
           "ln1_b", "w_up", "conv_w", "conv_b", "w_down", "ln2_g", "ln2_b")
LARGE = ("w_in", "w_out", "w_up", "w_down")
TRANSPOSED = ("w_in", "w_up")
PACKED = ("b_ada",) + SMALL
PACK_SEG = 8 * 128


def _gather_cols(g):
    nd, nl, r, c = g.shape
    return jnp.transpose(g, (1, 2, 0, 3)).reshape(nl, r, nd * c)


def _chunk_cols(g):
    nl, r, c8 = g.shape
    return jnp.transpose(g.reshape(nl, r, NDEV, c8 // NDEV), (2, 0, 1, 3))


def _join_rows(g):
    nd, r, c = g.shape
    return g.reshape(nd * r, c)


def _split_rows(g):
    r8, c = g.shape
    return g.reshape(NDEV, r8 // NDEV, c)


def _pack(parts):
    segs = []
    for n in PACKED:
        flat = parts[n].reshape(-1)
        segs.append(jnp.pad(flat, (0, -flat.shape[0] % PACK_SEG)).reshape(-1, 128))
    rows = jnp.concatenate(segs, axis=0)
    return jnp.pad(rows, ((0, -rows.shape[0] % (NDEV * 8)), (0, 0)))


def _unpack(rows, shapes):
    out, off = {}, 0
    for n in PACKED:
        size = math.prod(shapes[n])
        nrows = -(-size // PACK_SEG) * 8
        out[n] = rows[off:off + nrows].reshape(-1)[:size].reshape(shapes[n])
        off += nrows
    return out


def kernel(x, c, w_ada, b_ada, w_in, b_f, gm_ln_g, gm_ln_b, gm_w_s, gm_b_s, s5_lam_re, s5_lam_im, s5_log_dt, s5_b_re, s5_b_im, s5_c_re, s5_c_im, s5_d, s5_w_glu, s5_b_glu, w_out, ln1_g, ln1_b, w_up, conv_w, conv_b, w_down, ln2_g, ln2_b, loss_target, m_w_ada, m_b_ada, m_w_in, m_b_f, m_gm_ln_g, m_gm_ln_b, m_gm_w_s, m_gm_b_s, m_s5_lam_re, m_s5_lam_im, m_s5_log_dt, m_s5_b_re, m_s5_b_im, m_s5_c_re, m_s5_c_im, m_s5_d, m_s5_w_glu, m_s5_b_glu, m_w_out, m_ln1_g, m_ln1_b, m_w_up, m_conv_w, m_conv_b, m_w_down, m_ln2_g, m_ln2_b, v_w_ada, v_b_ada, v_w_in, v_b_f, v_gm_ln_g, v_gm_ln_b, v_gm_w_s, v_gm_b_s, v_s5_lam_re, v_s5_lam_im, v_s5_log_dt, v_s5_b_re, v_s5_b_im, v_s5_c_re, v_s5_c_im, v_s5_d, v_s5_w_glu, v_s5_b_glu, v_w_out, v_ln1_g, v_ln1_b, v_w_up, v_conv_w, v_conv_b, v_w_down, v_ln2_g, v_ln2_b):
    given = dict(locals())
    wts = {n: given[n] for n in WEIGHTS}
    mom = {n: given["m_" + n] for n in WEIGHTS}
    var = {n: given["v_" + n] for n in WEIGHTS}
    nl = w_ada.shape[0]
    me = _my_index()
    ada_cols = w_ada.shape[2]

    (c_all,) = exchange([c], False, "gather_c")
    c_all = c_all.reshape(NDEV, D)
    b_loc = lax.dynamic_slice_in_dim(b_ada, me * ada_cols, ada_cols, axis=1)
    mod_part = mod_slices(c_all, w_ada, b_loc)

    mod_all, conv_all = exchange([mod_part, conv_w], False, "gather_mod")
    mod_mine = lax.dynamic_index_in_dim(mod_all, me, axis=2, keepdims=False)
    mods = jnp.transpose(mod_mine, (1, 0, 2)).reshape(nl, 6, D)
    mods = jnp.pad(mods, ((0, 0), (0, 2), (0, 0)))
    conv_full = _gather_cols(conv_all)
    sp = {n: wts[n] for n in SMALL}
    rowwise = {n: [jnp.swapaxes(a[n], 1, 2) if n in TRANSPOSED else a[n] for a in (wts, mom, var)] for n in LARGE}

    def joined(own, land):
        return _join_rows(lax.dynamic_update_index_in_dim(land, own, me, 0))

    def block(l, names):
        return [rowwise[n][0][l].astype(BF16) for n in names]

    def chunked(grads, names):
        return [_split_rows(grads[n]) for n in names]

    head, tail = LARGE[:1], LARGE[1:]
    got_head = exchange_start(block(0, head), False, "gather_start_0_in")
    rest = block(0, tail)
    rest[0] = rest[0] + got_head[4][0, 0].astype(BF16)
    got_tail = exchange_start(rest, False, "gather_start_0_rest")
    xl, saved, weights = x[0], [], []
    for l in range(nl):
        if l == 0:
            own, land = exchange_wait(got_head, got_tail[4], False, "gather_wait_0_in")
            w = {n: joined(o, g) for n, o, g in zip(head, own, land)}
        else:
            own, land = exchange_wait(started, xl, False, f"gather_wait_{l}")
            w = {n: joined(o, g) for n, o, g in zip(LARGE, own, land)}
        mod_l = mods[l]
        if l + 1 < nl:
            nxt, w["w_in"] = lax.optimization_barrier((block(l + 1, LARGE), w["w_in"]))
            started = exchange_start(nxt, False, f"gather_start_{l + 1}")
            mod_l = mod_l + started[4][0, 0]
        w["w_in"] = jnp.pad(w["w_in"], ((0, NP - D_IN), (0, 0)))
        f = _layer_operands(sp, l)
        mixcat, sv = layer_fwd_mix(xl, mod_l, w["w_in"], f)
        if l == 0:
            own, land = exchange_wait(got_tail, mixcat, False, "gather_wait_0_rest")
            w.update({n: joined(o, g) for n, o, g in zip(tail, own, land)})
        w["conv_w"] = conv_full[l]
        xl, sv = layer_fwd_rest(xl, mixcat, mod_l, w, f, sv)
        weights.append(w)
        saved.append(sv)

    loss_tile, dx = loss_kernel(xl, loss_target[0])

    ffn_names, mix_names = ("w_up", "w_down"), ("w_in", "w_out")
    scattering, vals, dmods, gconv = [None] * nl, [None] * nl, [None] * nl, [None] * nl
    token = jnp.zeros((), F32)
    for l in reversed(range(nl)):
        mod_l = mods[l] + token
        dx1, g_ffn, part = layer_bwd_ffn(dx, saved[l], mod_l, weights[l])
        gconv[l] = g_ffn["conv_w"]
        if l == 0:
            sent_ffn = exchange_start(chunked(g_ffn, ffn_names), True, "scatter_start_0_ffn")
            mod_l = mod_l + sent_ffn[4][0, 0]
        dx, g_mix, vals[l], dmods[l] = layer_bwd_mix(dx1, saved[l], mod_l, weights[l], part)
        g_mix["w_in"] = g_mix["w_in"][:D_IN]
        if l == 0:
            scattering[l] = [(ffn_names, sent_ffn),
                             (mix_names, exchange_start(chunked(g_mix, mix_names), True, "scatter_start_0_mix"))]
        else:
            sent = exchange_start(chunked(dict(g_ffn, **g_mix), LARGE), True, f"scatter_start_{l}")
            scattering[l] = [(LARGE, sent)]
            token = sent[4][0, 0]
    gx = dx
    dmods = jnp.stack(dmods)
    gsm = {n: jnp.stack([v[n] for v in vals]) for n in SMALL}

    gsm["b_ada"] = dmods.reshape(nl, 6 * D)
    packed = _pack(gsm).reshape(NDEV, -1, 128)
    conv_recv, small_recv = exchange([_chunk_cols(jnp.stack(gconv)), packed], True, "scatter_small")
    small_sum = sum_chunks(small_recv)
    small_all, dmod_all = exchange([small_sum, dmods.reshape(nl, 6 * D)], False, "gather_small")

    received = [dict() for _ in range(nl)]

    def arrive(l, k, after):
        names, sent = scattering[l][k]
        own, land = exchange_wait(sent, after, True, f"scatter_wait_{l}_{k}")
        for n, o, g in zip(names, own, land):
            mine = lax.dynamic_index_in_dim(o, me, 0, keepdims=False)
            received[l][n] = lax.dynamic_update_index_in_dim(g, mine, me, 0)
        return land[0]

    after = small_all
    for l in reversed(range(nl)):
        after = arrive(l, 0, after)
    out = {}
    def update(n):
        res = adamw_layers(*rowwise[n], [received[l][n] for l in range(nl)], "adamw_" + n)
        return [jnp.swapaxes(r, 1, 2) for r in res] if n in TRANSPOSED else res

    for n in ffn_names:
        out[n] = update(n)
    arrive(0, 1, out[ffn_names[-1]][0])
    for n in mix_names:
        out[n] = update(n)
    shp = conv_w.shape
    two_d = lambda a: a.reshape(shp[0] * shp[1], shp[2])
    res = adamw(two_d(conv_w), two_d(m_conv_w), two_d(v_conv_w),
                chunks=conv_recv.reshape(NDEV, shp[0] * shp[1], shp[2]), name="adamw_conv_w")
    out["conv_w"] = [r.reshape(shp) for r in res]

    dm_loc = lax.dynamic_slice_in_dim(dmod_all, me * ada_cols, ada_cols, axis=2)
    g_ada = ada_grad(c_all, jnp.transpose(dm_loc, (1, 0, 2)))
    two_d = lambda a: a.reshape(nl * D, ada_cols)
    res = adamw(two_d(w_ada), two_d(m_w_ada), two_d(v_w_ada), g=two_d(g_ada), name="adamw_w_ada")
    out["w_ada"] = [r.reshape(w_ada.shape) for r in res]

    shapes = {n: wts[n].shape for n in PACKED}
    res = adamw(_pack(wts), _pack(mom), _pack(var), g=small_all.reshape(-1, 128), name="adamw_small")
    unpacked = [_unpack(r, shapes) for r in res]
    for n in PACKED:
        out[n] = [u[n] for u in unpacked]

    loss = lax.psum(loss_tile[0, 0], ("x", "y", "c"))
    return (loss, gx[None], *[out[n][0] for n in WEIGHTS], *[out[n][1] for n in WEIGHTS],
            *[out[n][2] for n in WEIGHTS], *[out[n][3] for n in WEIGHTS])
```

```python
import math

import jax
import jax.numpy as jnp
from jax import lax
from jax.experimental import pallas as pl
from jax.experimental.pallas import tpu as pltpu

F32 = jnp.float32
BF16 = jnp.bfloat16
MESH_IDS = pl.DeviceIdType.MESH

D = 1024
SEQ = 4096
DEPTH = 4
NDEV = 8
HD = 64
GM_W = 256
GM_H = 4
GM_C = 128
S5_W = 256
S5_G = 16
S5_H = 16
S5_P = 64
S5_N = S5_G * S5_P
FX_W = 512
FX_H = 8
D_IN = 2 * GM_W + S5_W + 3 * FX_W + FX_H
NP = 2432
FF_COL = 2304
DFF = 2816
LN_EPS = 1e-5
DN_ALPHA = (2.0 * DEPTH) ** 0.25
NEG_INF = -1e30
ADAM_LR = 0.001
ADAM_B1 = 0.9
ADAM_B2 = 0.999
ADAM_EPS = 1e-08
ADAM_WD = 0.01
ADAM_STEP = 10

V7X_VMEM_LIMIT = 56 * 1024 * 1024
TS = 512
TS_C = 256
T_S5 = 512
TQ = 512
TQ_FWD = 1024
ATTN_HEADS = 2


def _call(body, **kw):
    return pl.pallas_call(body, **kw)


def _params(*sem):
    return pltpu.CompilerParams(dimension_semantics=sem if sem else None,
                                vmem_limit_bytes=V7X_VMEM_LIMIT)


def _nn(a, b):
    return jnp.dot(a.astype(BF16), b.astype(BF16), preferred_element_type=F32)


def _nt(a, b):
    return lax.dot_general(a.astype(BF16), b.astype(BF16), (((1,), (1,)), ((), ())),
                           preferred_element_type=F32)


def _tn(a, b):
    return lax.dot_general(a.astype(BF16), b.astype(BF16), (((0,), (0,)), ((), ())),
                           preferred_element_type=F32)


@jax.custom_vjp
def _bdot(a, b):
    return _nn(a, b)


def _bdot_fwd(a, b):
    return _nn(a, b), (a, b)


def _bdot_bwd(res, g):
    a, b = res
    return _nt(g, b), _tn(a, g)


_bdot.defvjp(_bdot_fwd, _bdot_bwd)


@jax.custom_vjp
def _bdot_nt(a, b):
    return _nt(a, b)


def _bdot_nt_fwd(a, b):
    return _nt(a, b), (a, b)


def _bdot_nt_bwd(res, g):
    a, b = res
    return _nn(g, b), _tn(g, a)


_bdot_nt.defvjp(_bdot_nt_fwd, _bdot_nt_bwd)


def _ln(r, g, b):
    mu = jnp.mean(r, axis=-1, keepdims=True)
    xc = r - mu
    var = jnp.mean(xc * xc, axis=-1, keepdims=True)
    return xc * lax.rsqrt(var + LN_EPS) * g + b


def _rows(shape):
    return lax.broadcasted_iota(jnp.int32, shape, 0)


def _lanes(shape):
    return lax.broadcasted_iota(jnp.int32, shape, 1)


def mm_mod(a, wt, tn, name, mod, rows):
    s, k = a.shape
    n = wt.shape[0]
    ts = min(TS, s)

    def body(a_ref, m_ref, w_ref, o_ref):
        h = a_ref[...] * (1.0 + m_ref[rows[1]:rows[1] + 1, :]) + m_ref[rows[0]:rows[0] + 1, :]
        o_ref[...] = _nt(h, w_ref[...])

    return _call(body, name=name, grid=(n // tn, s // ts),
                 in_specs=[pl.BlockSpec((ts, k), lambda j, i: (i, 0)), pl.BlockSpec((8, k), lambda j, i: (0, 0)),
                           pl.BlockSpec((tn, k), lambda j, i: (j, 0))],
                 out_specs=pl.BlockSpec((ts, tn), lambda j, i: (i, j)),
                 out_shape=jax.ShapeDtypeStruct((s, n), F32),
                 compiler_params=_params("arbitrary", "arbitrary"))(a, mod, wt)


def mm_nt(dy, w, name):
    s, n = dy.shape
    k = w.shape[0]
    ts = min(TS, s)

    def body(dy_ref, w_ref, o_ref):
        o_ref[...] = _nt(dy_ref[...], w_ref[...])

    return _call(body, name=name, grid=(s // ts,),
                 in_specs=[pl.BlockSpec((ts, n), lambda i: (i, 0)),
                           pl.BlockSpec((k, n), lambda i: (0, 0))],
                 out_specs=pl.BlockSpec((ts, k), lambda i: (i, 0)),
                 out_shape=jax.ShapeDtypeStruct((s, k), F32),
                 compiler_params=_params("arbitrary"))(dy, w)


def mm_nt_mod(dy, w, x, dres, mod, rows, name):
    s, n = dy.shape
    k = w.shape[1]
    ts = min(TS, s)

    def body(dy_ref, w_ref, x_ref, r_ref, m_ref, dx_ref, dsh_ref, dsc_ref):
        @pl.when(pl.program_id(0) == 0)
        def _():
            dsh_ref[...] = jnp.zeros_like(dsh_ref)
            dsc_ref[...] = jnp.zeros_like(dsc_ref)

        dh = _nn(dy_ref[...], w_ref[...])
        dx_ref[...] = r_ref[...] + dh * (1.0 + m_ref[rows[1]:rows[1] + 1, :])
        dsh_ref[...] += jnp.sum(dh, axis=0, keepdims=True)
        dsc_ref[...] += jnp.sum(dh * x_ref[...], axis=0, keepdims=True)

    row = pl.BlockSpec((1, k), lambda i: (0, 0))
    tile = pl.BlockSpec((ts, k), lambda i: (i, 0))
    return _call(body, name=name, grid=(s // ts,),
                 in_specs=[pl.BlockSpec((ts, n), lambda i: (i, 0)),
                           pl.BlockSpec((n, k), lambda i: (0, 0)), tile, tile,
                           pl.BlockSpec((8, k), lambda i: (0, 0))],
                 out_specs=[tile, row, row],
                 out_shape=[jax.ShapeDtypeStruct((s, k), F32),
                            jax.ShapeDtypeStruct((1, k), F32),
                            jax.ShapeDtypeStruct((1, k), F32)],
                 compiler_params=_params("arbitrary"))(dy, w, x, dres, mod)


def mm_tn(a, dy, tn, name, mod=None, rows=None):
    s, k = a.shape
    n = dy.shape[1]
    ts = min(TS, s)
    ns = s // ts

    def body(*refs):
        if mod is None:
            a_ref, dy_ref, o_ref, acc = refs
            h = dy_ref[...]
        else:
            a_ref, dy_ref, m_ref, o_ref, acc = refs
            h = dy_ref[...] * (1.0 + m_ref[rows[1]:rows[1] + 1, :]) + m_ref[rows[0]:rows[0] + 1, :]
        i = pl.program_id(1)

        @pl.when(i == 0)
        def _():
            acc[...] = jnp.zeros_like(acc)

        acc[...] += _tn(a_ref[...], h)

        @pl.when(i == ns - 1)
        def _():
            o_ref[...] = acc[...].astype(BF16)

    in_specs = [pl.BlockSpec((ts, k), lambda j, i: (i, 0)), pl.BlockSpec((ts, tn), lambda j, i: (i, j))]
    args = [a, dy]
    if mod is not None:
        in_specs.append(pl.BlockSpec((8, tn), lambda j, i: (0, j)))
        args.append(mod)
    return _call(body, name=name, grid=(n // tn, s // ts), in_specs=in_specs,
                 out_specs=pl.BlockSpec((k, tn), lambda j, i: (0, j)),
                 out_shape=jax.ShapeDtypeStruct((k, n), BF16),
                 scratch_shapes=[pltpu.VMEM((k, tn), F32)],
                 compiler_params=_params("arbitrary", "arbitrary"))(*args)


def _post_fn(x, br, gate, lg, lb):
    return _ln(DN_ALPHA * x + (1.0 + gate) * br, lg, lb)


def proj_post(a, w, x, mod, grow, lg, lb, name):
    s, k = a.shape
    ts = min(TS, s)

    def body(a_ref, w_ref, x_ref, m_ref, lg_ref, lb_ref, b_ref, o_ref):
        br = jnp.dot(a_ref[...], w_ref[...], preferred_element_type=F32)
        b_ref[...] = br
        o_ref[...] = _post_fn(x_ref[...], br, m_ref[grow:grow + 1, :], lg_ref[...], lb_ref[...])

    tile = pl.BlockSpec((ts, D), lambda i: (i, 0))
    row = pl.BlockSpec((1, D), lambda i: (0, 0))
    out = jax.ShapeDtypeStruct((s, D), F32)
    return _call(body, name=name, grid=(s // ts,),
                 in_specs=[pl.BlockSpec((ts, k), lambda i: (i, 0)), pl.BlockSpec((k, D), lambda i: (0, 0)), tile,
                           pl.BlockSpec((8, D), lambda i: (0, 0)), row, row],
                 out_specs=[tile, tile], out_shape=[out, out],
                 compiler_params=_params("arbitrary"))(a, w, x, mod, lg, lb)


def post_bwd(x, br, mod, grow, lg, lb, dy, name):
    s = x.shape[0]
    ts = min(TS, s)

    def body(x_ref, b_ref, m_ref, lg_ref, lb_ref, dy_ref, dx_ref, db_ref, dg_ref, dlg_ref, dlb_ref):
        @pl.when(pl.program_id(0) == 0)
        def _():
            dg_ref[...] = jnp.zeros_like(dg_ref)
            dlg_ref[...] = jnp.zeros_like(dlg_ref)
            dlb_ref[...] = jnp.zeros_like(dlb_ref)

        _, vjp = jax.vjp(_post_fn, x_ref[...], b_ref[...], m_ref[grow:grow + 1, :], lg_ref[...], lb_ref[...])
        dx, db, dg, dlg, dlb = vjp(dy_ref[...])
        dx_ref[...] = dx
        db_ref[...] = db.astype(BF16)
        dg_ref[...] += dg
        dlg_ref[...] += dlg
        dlb_ref[...] += dlb

    tile = pl.BlockSpec((ts, D), lambda i: (i, 0))
    row = pl.BlockSpec((1, D), lambda i: (0, 0))
    rs = jax.ShapeDtypeStruct((1, D), F32)
    return _call(body, name=name, grid=(s // ts,),
                 in_specs=[tile, tile, pl.BlockSpec((8, D), lambda i: (0, 0)), row, row, tile],
                 out_specs=[tile, tile, row, row, row],
                 out_shape=[jax.ShapeDtypeStruct((s, D), F32), jax.ShapeDtypeStruct((s, D), BF16), rs, rs, rs],
                 compiler_params=_params("arbitrary"))(x, br, mod, lg, lb, dy)


def loss_kernel(y, target):
    s = y.shape[0]
    ts = min(TS, s)

    def body(y_ref, t_ref, l_ref, dy_ref):
        @pl.when(pl.program_id(0) == 0)
        def _():
            l_ref[...] = jnp.zeros_like(l_ref)

        err = y_ref[...] - t_ref[...]
        dy_ref[...] = err * (1.0 / D)
        per_tok = jnp.mean(err * err, axis=-1, keepdims=True)
        l_ref[...] += 0.5 * jnp.sum(per_tok)

    tile = pl.BlockSpec((ts, D), lambda i: (i, 0))
    return _call(body, name="loss", grid=(s // ts,), in_specs=[tile, tile],
                 out_specs=[pl.BlockSpec((8, 128), lambda i: (0, 0)), tile],
                 out_shape=[jax.ShapeDtypeStruct((8, 128), F32), jax.ShapeDtypeStruct((s, D), F32)],
                 compiler_params=_params("arbitrary"))(y, target)


def _gm_pair(u, v, lg, lb, w0, w1, bs0, bs1):
    t = u.shape[0]
    low = _lanes((t, 2 * HD)) < HD

    def head_mean(x):
        lo = jnp.sum(jnp.where(low, x, 0.0), axis=-1, keepdims=True)
        hi = jnp.sum(jnp.where(low, 0.0, x), axis=-1, keepdims=True)
        return jnp.where(low, lo, hi) * (1.0 / HD)

    xc = v - head_mean(v)
    vn = xc * lax.rsqrt(head_mean(xc * xc) + LN_EPS) * lg + lb
    v0 = jnp.where(low, vn, 0.0)
    v1 = jnp.where(low, 0.0, vn)
    causal = _rows((GM_C, GM_C)) >= _lanes((GM_C, GM_C))
    wm0 = jnp.where(causal, w0, 0.0)
    wm1 = jnp.where(causal, w1, 0.0)
    bias = jnp.where(_lanes((GM_C, 2 * HD)) < HD, bs0, bs1)
    chunks = []
    for n in range(t // GM_C):
        rs = slice(n * GM_C, (n + 1) * GM_C)
        chunks.append(u[rs] * (_bdot(wm0, v0[rs]) + _bdot(wm1, v1[rs]) + bias))
    return jnp.concatenate(chunks, axis=0)


def gm_fwd(p, lg, lb, ws, bst):
    s = p.shape[0]
    ts = min(TS_C, s)

    def body(u_ref, v_ref, lg_ref, lb_ref, ws_ref, bs_ref, o_ref):
        for j in range(GM_H // 2):
            sl = slice(j * 2 * HD, (j + 1) * 2 * HD)
            o_ref[:, sl] = _gm_pair(u_ref[:, sl], v_ref[:, sl], lg_ref[:, sl], lb_ref[:, sl], ws_ref[2 * j],
                                    ws_ref[2 * j + 1], bs_ref[:, 2 * j:2 * j + 1],
                                    bs_ref[:, 2 * j + 1:2 * j + 2]).astype(BF16)

    full = lambda shape: pl.BlockSpec(shape, lambda i: (0,) * len(shape))
    return _call(body, name="gm_fwd", grid=(s // ts,),
                 in_specs=[pl.BlockSpec((ts, GM_W), lambda i: (i, 0)), pl.BlockSpec((ts, GM_W), lambda i: (i, 1)),
                           full((1, GM_W)), full((1, GM_W)), full((GM_H, GM_C, GM_C)), full((GM_C, GM_H))],
                 out_specs=pl.BlockSpec((ts, GM_W), lambda i: (i, 0)),
                 out_shape=jax.ShapeDtypeStruct((s, GM_W), BF16),
                 compiler_params=_params("arbitrary"))(p, p, lg, lb, ws, bst)


def gm_bwd(p, lg, lb, ws, bst, dmix):
    s = p.shape[0]
    ts = min(TS_C, s)

    def body(u_ref, v_ref, lg_ref, lb_ref, ws_ref, bs_ref, dy_ref, duv_ref, dlg_ref, dlb_ref, dws_ref, dbs_ref):
        @pl.when(pl.program_id(0) == 0)
        def _():
            dlg_ref[...] = jnp.zeros_like(dlg_ref)
            dlb_ref[...] = jnp.zeros_like(dlb_ref)
            dws_ref[...] = jnp.zeros_like(dws_ref)
            dbs_ref[...] = jnp.zeros_like(dbs_ref)

        for j in range(GM_H // 2):
            sl = slice(j * 2 * HD, (j + 1) * 2 * HD)
            _, vjp = jax.vjp(_gm_pair, u_ref[:, sl], v_ref[:, sl], lg_ref[:, sl], lb_ref[:, sl], ws_ref[2 * j],
                             ws_ref[2 * j + 1], bs_ref[:, 2 * j:2 * j + 1], bs_ref[:, 2 * j + 1:2 * j + 2])
            du, dv, dlg, dlb, dw0, dw1, dbs0, dbs1 = vjp(dy_ref[:, sl])
            duv_ref[:, sl] = du.astype(BF16)
            duv_ref[:, GM_W + j * 2 * HD:GM_W + (j + 1) * 2 * HD] = dv.astype(BF16)
            dlg_ref[:, sl] += dlg
            dlb_ref[:, sl] += dlb
            dws_ref[2 * j] += dw0
            dws_ref[2 * j + 1] += dw1
            dbs_ref[:, 2 * j:2 * j + 1] += dbs0
            dbs_ref[:, 2 * j + 1:2 * j + 2] += dbs1

    full = lambda shape: pl.BlockSpec(shape, lambda i: (0,) * len(shape))
    return _call(body, name="gm_bwd", grid=(s // ts,),
                 in_specs=[pl.BlockSpec((ts, GM_W), lambda i: (i, 0)), pl.BlockSpec((ts, GM_W), lambda i: (i, 1)),
                           full((1, GM_W)), full((1, GM_W)), full((GM_H, GM_C, GM_C)), full((GM_C, GM_H)),
                           pl.BlockSpec((ts, GM_W), lambda i: (i, 0))],
                 out_specs=[pl.BlockSpec((ts, 2 * GM_W), lambda i: (i, 0)), full((1, GM_W)), full((1, GM_W)),
                            full((GM_H, GM_C, GM_C)), full((GM_C, GM_H))],
                 out_shape=[jax.ShapeDtypeStruct((s, 2 * GM_W), BF16), jax.ShapeDtypeStruct((1, GM_W), F32),
                            jax.ShapeDtypeStruct((1, GM_W), F32), jax.ShapeDtypeStruct((GM_H, GM_C, GM_C), F32),
                            jax.ShapeDtypeStruct((GM_C, GM_H), F32)],
                 compiler_params=_params("arbitrary"))(p, p, lg, lb, ws, bst, dmix)


def _s5_prep_fn(lr, li, ldt, bt):
    dt = jnp.exp(ldt)
    er = jnp.exp(lr * dt)
    ar = er * jnp.cos(li * dt)
    ai = er * jnp.sin(li * dt)
    den = lr * lr + li * li
    nr = ar - 1.0
    cr = (nr * lr + ai * li) / den
    ci = (ai * lr - nr * li) / den
    br, bi = bt[:, :S5_N], bt[:, S5_N:]
    return ar, ai, jnp.concatenate([cr * br - ci * bi, cr * bi + ci * br], axis=1)


def s5_prep_fwd(lr, li, ldt, bt):
    def body(lr_ref, li_ref, ldt_ref, bt_ref, a_ref, bb_ref):
        ar, ai, bb = _s5_prep_fn(lr_ref[...], li_ref[...], ldt_ref[...], bt_ref[...])
        a_ref[...] = jnp.concatenate([ar, ai, jnp.zeros((6, S5_N), F32)], axis=0)
        bb_ref[...] = bb

    return _call(body, name="s5_prep_fwd",
                 out_shape=[jax.ShapeDtypeStruct((8, S5_N), F32), jax.ShapeDtypeStruct((S5_H, 2 * S5_N), F32)])(lr, li, ldt, bt)


def s5_prep_bwd(lr, li, ldt, bt, da, dbb):
    def body(lr_ref, li_ref, ldt_ref, bt_ref, da_ref, dbb_ref, dlr_ref, dli_ref, dldt_ref, dbt_ref):
        _, vjp = jax.vjp(_s5_prep_fn, lr_ref[...], li_ref[...], ldt_ref[...], bt_ref[...])
        dlr, dli, dldt, dbt = vjp((da_ref[0:1, :], da_ref[1:2, :], dbb_ref[...]))
        dlr_ref[...] = dlr
        dli_ref[...] = dli
        dbt_ref[...] = dbt
        group = (_rows((S5_N, 128)) // S5_P == _lanes((S5_N, 128))).astype(F32)
        dldt_ref[...] = jnp.dot(jnp.broadcast_to(dldt, (8, S5_N)), group, precision=lax.Precision.HIGHEST,
                                preferred_element_type=F32)[0:1, :]

    r = jax.ShapeDtypeStruct((1, S5_N), F32)
    return _call(body, name="s5_prep_bwd",
                 out_shape=[r, r, jax.ShapeDtypeStruct((1, 128), F32),
                            jax.ShapeDtypeStruct((S5_H, 2 * S5_N), F32)])(lr, li, ldt, bt, da, dbb)


def _s5_out_fn(x, u, cbd, drow, wg, bg):
    y = _bdot_nt(x[:, :S5_N], cbd[:, :S5_N]) - _bdot_nt(x[:, S5_N:], cbd[:, S5_N:]) + drow * u
    y = jax.nn.gelu(y)
    gate = _bdot_nt(y, wg) + bg
    return y * jax.nn.sigmoid(gate)


def _scan_chunk(buf, ar, ai, cr, ci, reverse):
    t = buf.shape[0]

    def local(xr, xi, rows):
        within = _rows(xr.shape) % 8
        pr, pi = ar, ai
        for d in (1, 2, 4):
            keep = within < 8 - d if reverse else within >= d
            shift = rows - d if reverse else d
            sr = jnp.where(keep, pltpu.roll(xr, shift, 0), 0.0)
            si = jnp.where(keep, pltpu.roll(xi, shift, 0), 0.0)
            xr, xi = xr + pr * sr - pi * si, xi + pr * si + pi * sr
            pr, pi = pr * pr - pi * pi, 2.0 * pr * pi
        return xr, xi

    xr, xi = local(buf[:, :S5_N], buf[:, S5_N:], t)
    buf[:, :S5_N] = xr
    buf[:, S5_N:] = xi
    edge = _rows((8, S5_N)) == (7 if reverse else 0)
    pr8, pi8 = local(jnp.where(edge, ar, 0.0), jnp.where(edge, ai, 0.0), 8)

    def group(j, c):
        g = t // 8 - 1 - j if reverse else j
        rows = pl.ds(pl.multiple_of(g * 8, 8), 8)
        gr = buf[rows, :S5_N] + pr8 * c[0] - pi8 * c[1]
        gi = buf[rows, S5_N:] + pr8 * c[1] + pi8 * c[0]
        buf[rows, :S5_N] = gr
        buf[rows, S5_N:] = gi
        return (gr[0:1, :], gi[0:1, :]) if reverse else (gr[7:8, :], gi[7:8, :])

    return lax.fori_loop(0, t // 8, group, (cr, ci), unroll=4)


def s5_fwd(p, arow, bbd, cbd, drow, wg, bg):
    s = p.shape[0]
    t = min(T_S5, s)

    def body(u_ref, a_ref, bbd_ref, cbd_ref, d_ref, wg_ref, bg_ref, y_ref, st_ref, carry):
        @pl.when(pl.program_id(0) == 0)
        def _():
            carry[...] = jnp.zeros_like(carry)

        u = u_ref[...]
        st_ref[...] = _nn(u, bbd_ref[...])
        cr, ci = _scan_chunk(st_ref, a_ref[0:1, :], a_ref[1:2, :], carry[0:1, :S5_N], carry[0:1, S5_N:], False)
        carry[0:1, :S5_N] = cr
        carry[0:1, S5_N:] = ci
        y_ref[...] = _s5_out_fn(st_ref[...], u, cbd_ref[...], d_ref[...], wg_ref[...], bg_ref[...]).astype(BF16)

    full = lambda shape: pl.BlockSpec(shape, lambda i: (0,) * len(shape))
    return _call(body, name="s5_fwd", grid=(s // t,),
                 in_specs=[pl.BlockSpec((t, S5_W), lambda i: (i, 2)), full((8, S5_N)), full((S5_W, 2 * S5_N)),
                           full((S5_W, 2 * S5_N)), full((1, S5_W)), full((S5_W, S5_W)), full((1, S5_W))],
                 out_specs=[pl.BlockSpec((t, S5_W), lambda i: (i, 0)), pl.BlockSpec((t, 2 * S5_N), lambda i: (i, 0))],
                 out_shape=[jax.ShapeDtypeStruct((s, S5_W), BF16), jax.ShapeDtypeStruct((s, 2 * S5_N), F32)],
                 scratch_shapes=[pltpu.VMEM((8, 2 * S5_N), F32)],
                 compiler_params=_params("arbitrary"))(p, arow, bbd, cbd, drow, wg, bg)


def s5_bwd(p, st, arow, bbd, cbd, drow, wg, bg, dmix):
    s = p.shape[0]
    t = min(T_S5, s)
    nc = s // t

    def body(u_ref, st_ref, prev_ref, a_ref, bbd_ref, cbd_ref, d_ref, wg_ref, bg_ref, dy_ref,
             du_ref, da_ref, dbbd_ref, dcbd_ref, dd_ref, dwg_ref, dbg_ref, carry, gbuf):
        i = pl.program_id(0)

        @pl.when(i == 0)
        def _():
            carry[...] = jnp.zeros_like(carry)
            for r in (da_ref, dbbd_ref, dcbd_ref, dd_ref, dwg_ref, dbg_ref):
                r[...] = jnp.zeros_like(r)

        u = u_ref[...]
        x = st_ref[...]
        _, vjp = jax.vjp(_s5_out_fn, x, u, cbd_ref[...], d_ref[...], wg_ref[...], bg_ref[...])
        dx, du1, dcbd, dd, dwg, dbg = vjp(dy_ref[...])
        gbuf[...] = dx
        cr, ci = _scan_chunk(gbuf, a_ref[0:1, :], -a_ref[1:2, :], carry[0:1, :S5_N], carry[0:1, S5_N:], True)
        carry[0:1, :S5_N] = cr
        carry[0:1, S5_N:] = ci
        gr, gi = gbuf[:, :S5_N], gbuf[:, S5_N:]
        rid = _rows((t, S5_N))
        has_prev = (i < nc - 1).astype(F32)
        top_r = prev_ref[7:8, :S5_N] * has_prev
        top_i = prev_ref[7:8, S5_N:] * has_prev
        xpr = jnp.where(rid == 0, top_r, pltpu.roll(x[:, :S5_N], 1, 0))
        xpi = jnp.where(rid == 0, top_i, pltpu.roll(x[:, S5_N:], 1, 0))
        da_ref[0:1, :] += jnp.sum(xpr * gr + xpi * gi, axis=0, keepdims=True)
        da_ref[1:2, :] += jnp.sum(xpr * gi - xpi * gr, axis=0, keepdims=True)
        g = jnp.concatenate([gr, gi], axis=1)
        dbbd_ref[...] += _tn(u, g)
        du_ref[...] = (_nt(g, bbd_ref[...]) + du1).astype(BF16)
        dcbd_ref[...] += dcbd
        dd_ref[...] += dd
        dwg_ref[...] += dwg
        dbg_ref[...] += dbg

    full = lambda shape: pl.BlockSpec(shape, lambda i: (0,) * len(shape))
    rev = lambda col: (lambda i: (nc - 1 - i, col))
    prev_map = lambda i: (jnp.maximum((nc - 1 - i) * (t // 8) - 1, 0), 0)
    return _call(body, name="s5_bwd", grid=(nc,),
                 in_specs=[pl.BlockSpec((t, S5_W), rev(2)), pl.BlockSpec((t, 2 * S5_N), rev(0)),
                           pl.BlockSpec((8, 2 * S5_N), prev_map), full((8, S5_N)), full((S5_W, 2 * S5_N)),
                           full((S5_W, 2 * S5_N)), full((1, S5_W)), full((S5_W, S5_W)), full((1, S5_W)),
                           pl.BlockSpec((t, S5_W), rev(1))],
                 out_specs=[pl.BlockSpec((t, S5_W), rev(0)), full((8, S5_N)), full((S5_W, 2 * S5_N)),
                            full((S5_W, 2 * S5_N)), full((1, S5_W)), full((S5_W, S5_W)), full((1, S5_W))],
                 out_shape=[jax.ShapeDtypeStruct((s, S5_W), BF16), jax.ShapeDtypeStruct((8, S5_N), F32),
                            jax.ShapeDtypeStruct((S5_W, 2 * S5_N), F32), jax.ShapeDtypeStruct((S5_W, 2 * S5_N), F32),
                            jax.ShapeDtypeStruct((1, S5_W), F32), jax.ShapeDtypeStruct((S5_W, S5_W), F32),
                            jax.ShapeDtypeStruct((1, S5_W), F32)],
                 scratch_shapes=[pltpu.VMEM((8, 2 * S5_N), F32), pltpu.VMEM((t, 2 * S5_N), F32)],
                 compiler_params=_params("arbitrary"))(p, st, st, arow, bbd, cbd, drow, wg, bg, dmix)


def _cum_steps(s):
    return int(math.ceil(math.log2(s)))


V_BLK = (2 * GM_W + S5_W + 2 * FX_W) // 128


AUG = 2 * HD
BIAS_COL = HD
FQ_COL = HD + 3
PAIR_W = 256


def _split3(f):
    hi = f.astype(BF16).astype(F32)
    r = f - hi
    mid = r.astype(BF16).astype(F32)
    lo = (r - mid).astype(BF16).astype(F32)
    return hi, mid, lo


def fox_prep(p, bf):
    s = p.shape[0]
    ts = min(TS, s)
    scale = HD ** -0.5

    def body(q0_ref, q1_ref, k0_ref, k1_ref, v0_ref, v1_ref, f_ref, bf_ref,
             qa_ref, ka_ref, qat_ref, kat_ref, vt_ref, carry):
        @pl.when(pl.program_id(0) == 0)
        def _():
            carry[...] = jnp.zeros_like(carry)

        lane = _lanes((ts, 128))
        lf = jax.nn.log_sigmoid(f_ref[...] + bf_ref[...])
        acc = jnp.where(lane < FX_H, lf, 0.0)
        rid = _rows((ts, 128))
        for k in range(_cum_steps(ts)):
            d = 1 << k
            acc = acc + jnp.where(rid >= d, pltpu.roll(acc, d, 0), 0.0)
        acc = acc + carry[0:1, :]
        carry[0:1, :] = acc[ts - 1:ts, :]

        low = lane < HD
        for h in range(FX_H):
            blk, pos = divmod(h, 4)
            pair = slice((pos // 2) * 128, (pos // 2) * 128 + 128)
            hi, mid, lo = _split3(acc[:, h:h + 1])
            one = jnp.ones((ts, 1), F32)

            def augment(ref, cols):
                x = ref[:, pair]
                if pos % 2:
                    x = pltpu.roll(x, HD, 1)
                out = jnp.where(low, x, 0.0)
                for j, cval in enumerate(cols):
                    out = jnp.where(lane == HD + j, cval, out)
                return out

            qa = augment((q0_ref, q1_ref)[blk], (one, one, one, hi, mid, lo))
            qa = jnp.where(low, qa * scale, qa)
            ka = augment((k0_ref, k1_ref)[blk], (-hi, -mid, -lo, one, one, one))
            cs = slice(h * AUG, (h + 1) * AUG)
            qa_ref[:, cs] = qa.astype(BF16)
            ka_ref[:, cs] = ka.astype(BF16)
            qat_ref[cs, :] = jnp.transpose(qa).astype(BF16)
            kat_ref[cs, :] = jnp.transpose(ka).astype(BF16)
        for j in range(FX_H // 2):
            vref = (v0_ref, v1_ref)[j // 2]
            vt_ref[j * 128:(j + 1) * 128, :] = jnp.transpose(vref[:, (j % 2) * 128:(j % 2) * 128 + 128]).astype(BF16)

    q_blk = (2 * GM_W + S5_W) // PAIR_W
    col = lambda b: pl.BlockSpec((ts, PAIR_W), lambda i: (i, b))
    wide = FX_H * AUG
    return _call(body, name="fox_prep", grid=(s // ts,),
                 in_specs=[col(q_blk), col(q_blk + 1), col(q_blk + 2), col(q_blk + 3), col(q_blk + 4), col(q_blk + 5),
                           pl.BlockSpec((ts, 128), lambda i: (i, FF_COL // 128)), pl.BlockSpec((1, 128), lambda i: (0, 0))],
                 out_specs=[pl.BlockSpec((ts, wide), lambda i: (i, 0)), pl.BlockSpec((ts, wide), lambda i: (i, 0)),
                            pl.BlockSpec((wide, ts), lambda i: (0, i)), pl.BlockSpec((wide, ts), lambda i: (0, i)),
                            pl.BlockSpec((FX_W, ts), lambda i: (0, i))],
                 out_shape=[jax.ShapeDtypeStruct((s, wide), BF16), jax.ShapeDtypeStruct((s, wide), BF16),
                            jax.ShapeDtypeStruct((wide, s), BF16), jax.ShapeDtypeStruct((wide, s), BF16),
                            jax.ShapeDtypeStruct((FX_W, s), BF16)],
                 scratch_shapes=[pltpu.VMEM((8, 128), F32)],
                 compiler_params=_params("arbitrary"))(p, p, p, p, p, p, p, bf)


def fox_prep_grad(p, bf, dfq, dfk):
    s = p.shape[0]
    ts = min(TS, s)
    ns = s // ts

    def body(f_ref, bf_ref, dfq_ref, dfk_ref, df_ref, dbf_ref, carry):
        @pl.when(pl.program_id(0) == 0)
        def _():
            carry[...] = jnp.zeros_like(carry)
            dbf_ref[...] = jnp.zeros_like(dbf_ref)

        lane = _lanes((ts, 128))
        acc = jnp.zeros((ts, 128), F32)
        for h in range(FX_H):
            c = (h // 2) * 128 + h % 2
            acc = jnp.where(lane == h, dfq_ref[:, c:c + 1] + dfk_ref[:, c:c + 1], acc)
        rid = _rows((ts, 128))
        for k in range(_cum_steps(ts)):
            d = 1 << k
            acc = acc + jnp.where(rid < ts - d, pltpu.roll(acc, ts - d, 0), 0.0)
        acc = acc + carry[0:1, :]
        carry[0:1, :] = acc[0:1, :]
        z = f_ref[...] + bf_ref[...]
        df = jnp.where(lane < FX_H, acc * jax.nn.sigmoid(-z), 0.0)
        df_ref[...] = df.astype(BF16)
        dbf_ref[...] += jnp.sum(df, axis=0, keepdims=True)

    rev = lambda i: (ns - 1 - i, 0)
    return _call(body, name="fox_prep_grad", grid=(ns,),
                 in_specs=[pl.BlockSpec((ts, 128), lambda i: (ns - 1 - i, FF_COL // 128)),
                           pl.BlockSpec((1, 128), lambda i: (0, 0)),
                           pl.BlockSpec((ts, FX_W), rev), pl.BlockSpec((ts, FX_W), rev)],
                 out_specs=[pl.BlockSpec((ts, 128), rev), pl.BlockSpec((1, 128), lambda i: (0, 0))],
                 out_shape=[jax.ShapeDtypeStruct((s, 128), BF16), jax.ShapeDtypeStruct((1, 128), F32)],
                 scratch_shapes=[pltpu.VMEM((8, 128), F32)],
                 compiler_params=_params("arbitrary"))(p, bf, dfq, dfk)


def attn(qat, ka, vt):
    s = ka.shape[0]
    tq = min(TQ_FWD, s)
    nq = s // tq

    nh = ATTN_HEADS

    def body(qat_ref, ka_ref, vt_ref, o_ref, lse_ref):
        qi = pl.program_id(1)
        half = tq // 2
        lse_ref[...] = jnp.zeros_like(lse_ref)

        def update(m, l, acc, st, v):
            m_new = jnp.maximum(m, jnp.max(st, axis=0, keepdims=True))
            alpha = jnp.exp(m - m_new)
            pt = jnp.exp(st - m_new)
            return (m_new, alpha * l + jnp.sum(pt, axis=0, keepdims=True),
                    alpha * acc + jnp.dot(v, pt.astype(BF16), preferred_element_type=F32))

        def step(kj, carry):
            off = pl.multiple_of(kj * tq, tq)
            out = []
            for hh in range(nh):
                cs = slice(hh * AUG, (hh + 1) * AUG)
                st = jnp.dot(ka_ref[pl.ds(off, tq), cs], qat_ref[cs, :], preferred_element_type=F32)
                out.append(update(*carry[hh], st, vt_ref[hh * HD:(hh + 1) * HD, pl.ds(off, tq)]))
            return tuple(out)

        def diagonal(carry):
            off = pl.multiple_of(qi * tq, tq)
            off2 = pl.multiple_of(off + half, half)
            lower = _rows((half, tq)) <= _lanes((half, tq))
            out = []
            for hh in range(nh):
                cs = slice(hh * AUG, (hh + 1) * AUG)
                hs = slice(hh * HD, (hh + 1) * HD)
                st = jnp.dot(ka_ref[pl.ds(off, half), cs], qat_ref[cs, :], preferred_element_type=F32)
                m, l, acc = update(*carry[hh], jnp.where(lower, st, NEG_INF), vt_ref[hs, pl.ds(off, half)])
                st = jnp.dot(ka_ref[pl.ds(off2, half), cs], qat_ref[cs, half:], preferred_element_type=F32)
                mr, lr, ar = update(m[:, half:], l[:, half:], acc[:, half:], jnp.where(lower[:, :half], st, NEG_INF),
                                    vt_ref[hs, pl.ds(off2, half)])
                out.append((jnp.concatenate([m[:, :half], mr], axis=1), jnp.concatenate([l[:, :half], lr], axis=1),
                            jnp.concatenate([acc[:, :half], ar], axis=1)))
            return tuple(out)

        init = tuple((jnp.full((1, tq), NEG_INF, F32), jnp.zeros((1, tq), F32), jnp.zeros((HD, tq), F32))
                     for _ in range(nh))
        carry = diagonal(lax.fori_loop(0, qi, step, init))
        for hh in range(nh):
            m, l, _ = carry[hh]
            lse_ref[hh // 2, hh % 2:hh % 2 + 1, :] = m + jnp.log(l)
        for j in range(nh // 2):
            pair = jnp.concatenate([carry[2 * j][2] / carry[2 * j][1], carry[2 * j + 1][2] / carry[2 * j + 1][1]], axis=0)
            o_ref[:, j * 128:(j + 1) * 128] = jnp.transpose(pair).astype(BF16)

    return _call(body, name="attn", grid=(FX_H // nh, nq),
                 in_specs=[pl.BlockSpec((nh * AUG, tq), lambda h, i: (h, i)),
                           pl.BlockSpec((s, nh * AUG), lambda h, i: (0, h)),
                           pl.BlockSpec((nh * HD, s), lambda h, i: (h, 0))],
                 out_specs=[pl.BlockSpec((tq, nh * HD), lambda h, i: (i, h)),
                            pl.BlockSpec((nh // 2, 8, tq), lambda h, i: (h, 0, i))],
                 out_shape=[jax.ShapeDtypeStruct((s, FX_W), BF16), jax.ShapeDtypeStruct((FX_H // 2, 8, s), F32)],
                 compiler_params=_params("arbitrary", "arbitrary"))(qat, ka, vt)


def attn_grad(qa, qat, ka, kat, p, o, lse, dmix):
    s = qa.shape[0]
    tq = min(TQ, s)
    nq = s // tq
    scale = HD ** -0.5

    def body(qa_ref, qat_ref, ka_ref, kat_ref, v_ref, o_ref, lse_ref, do_ref,
             dq_ref, dk_ref, dv_ref, dfq_ref, dfk_ref, dot_scr, delta, dqt):
        kj = pl.program_id(1)
        lane = _lanes((tq, 128))
        low = lane < HD
        causal = _rows((tq, tq)) <= _lanes((tq, tq))

        @pl.when(kj == 0)
        def _():
            dqt[...] = jnp.zeros_like(dqt)
            delta[...] = jnp.zeros_like(delta)

            def prep(c, _):
                rows = pl.ds(pl.multiple_of(c * tq, tq), tq)
                do = do_ref[rows, :]
                pt = jnp.transpose(do * o_ref[rows, :].astype(F32))
                delta[0:1, rows] = jnp.sum(pt[:HD], axis=0, keepdims=True)
                delta[1:2, rows] = jnp.sum(pt[HD:], axis=0, keepdims=True)
                dot_scr[:, rows] = jnp.transpose(do).astype(BF16)
                return 0

            lax.fori_loop(0, nq, prep, 0)

        v = v_ref[...]
        vms = [jnp.where(low, v, 0.0).astype(BF16), jnp.where(low, 0.0, v).astype(BF16)]

        def tile(qi, carry, masked):
            cols = pl.ds(pl.multiple_of(qi * tq, tq), tq)
            do = do_ref[cols, :].astype(BF16)
            out = []
            for hh in range(2):
                cs = slice(hh * AUG, (hh + 1) * AUG)
                dka, dv = carry[hh]
                st = jnp.dot(ka_ref[:, cs], qat_ref[cs, cols], preferred_element_type=F32)
                if masked:
                    st = jnp.where(causal, st, NEG_INF)
                pt = jnp.exp(st - lse_ref[0, hh:hh + 1, cols])
                dv = dv + jnp.dot(pt.astype(BF16), do, preferred_element_type=F32)
                dpt = jnp.dot(vms[hh], dot_scr[:, cols], preferred_element_type=F32)
                dsb = (pt * (dpt - delta[hh:hh + 1, cols])).astype(BF16)
                dka = dka + jnp.dot(dsb, qa_ref[cols, cs], preferred_element_type=F32)
                dqt[hh, :, cols] += jnp.dot(kat_ref[cs, :], dsb, preferred_element_type=F32)
                out.append((dka, dv))
            return tuple(out)

        init = tuple((jnp.zeros((tq, AUG), F32), jnp.zeros((tq, 128), F32)) for _ in range(2))
        carry = tile(kj, init, True)
        carry = lax.fori_loop(kj + 1, nq, lambda qi, c: tile(qi, c, False), carry)
        dks = [carry[0][0], carry[1][0]]
        dvs = [carry[0][1], carry[1][1]]
        dv_ref[...] = jnp.where(low, dvs[0], dvs[1]).astype(BF16)
        dk_ref[...] = jnp.where(low, dks[0], pltpu.roll(dks[1], HD, 1)).astype(BF16)
        dfk_ref[...] = jnp.where(lane == 0, -dks[0][:, BIAS_COL:BIAS_COL + 1],
                                 jnp.where(lane == 1, -dks[1][:, BIAS_COL:BIAS_COL + 1], 0.0))

        @pl.when(kj == nq - 1)
        def _():
            def finish(c, _):
                rows = pl.ds(pl.multiple_of(c * tq, tq), tq)
                t0 = jnp.transpose(dqt[0, :, rows])
                t1 = jnp.transpose(dqt[1, :, rows])
                dq_ref[rows, :] = (jnp.where(low, t0, pltpu.roll(t1, HD, 1)) * scale).astype(BF16)
                dfq_ref[rows, :] = jnp.where(lane == 0, t0[:, FQ_COL:FQ_COL + 1],
                                             jnp.where(lane == 1, t1[:, FQ_COL:FQ_COL + 1], 0.0))
                return 0

            lax.fori_loop(0, nq, finish, 0)

    seq128 = lambda blk: pl.BlockSpec((s, 128), lambda h, j: (0, blk + h))
    tile128 = pl.BlockSpec((tq, 128), lambda h, j: (j, h))
    out_b = jax.ShapeDtypeStruct((s, FX_W), BF16)
    out_f = jax.ShapeDtypeStruct((s, FX_W), F32)
    return _call(body, name="attn_grad", grid=(FX_H // 2, nq),
                 in_specs=[pl.BlockSpec((s, 2 * AUG), lambda h, j: (0, h)), pl.BlockSpec((2 * AUG, s), lambda h, j: (h, 0)),
                           pl.BlockSpec((tq, 2 * AUG), lambda h, j: (j, h)), pl.BlockSpec((2 * AUG, tq), lambda h, j: (h, j)),
                           pl.BlockSpec((tq, 128), lambda h, j: (j, V_BLK + h)), seq128(0),
                           pl.BlockSpec((1, 8, s), lambda h, j: (h, 0, 0)), seq128(4)],
                 out_specs=[seq128(0), tile128, tile128, seq128(0), tile128],
                 out_shape=[out_b, out_b, out_b, out_f, out_f],
                 scratch_shapes=[pltpu.VMEM((128, s), BF16), pltpu.VMEM((8, s), F32), pltpu.VMEM((2, AUG, s), F32)],
                 compiler_params=_params("arbitrary", "arbitrary"))(qa, qat, ka, kat, p, o, lse, dmix)


def _shift_down(a, prev8, k):
    r = pltpu.roll(a, k, 0)
    top = jnp.where(_rows(prev8.shape) < k, pltpu.roll(prev8, k, 0), r[0:8])
    return jnp.concatenate([top, r[8:]], axis=0)


def _shift_up(a, next8, k):
    t = a.shape[0]
    r = pltpu.roll(a, t - k, 0)
    bot = jnp.where(_rows(next8.shape) >= 8 - k, pltpu.roll(next8, 8 - k, 0), r[t - 8:t])
    return jnp.concatenate([r[:t - 8], bot], axis=0)


def _conv(a, prev8, cw, cb):
    return cb + cw[0:1, :] * _shift_down(a, prev8, 2) + cw[1:2, :] * _shift_down(a, prev8, 1) + cw[2:3, :] * a


GELU_K0 = math.sqrt(2.0 / math.pi)
GELU_K1 = GELU_K0 * 0.044715


def _gelu_parts(c):
    c2 = c * c
    return c2, 0.5 + 0.5 * jnp.tanh(c * (GELU_K0 + GELU_K1 * c2))


def conv_fwd(up, cw, cb):
    s = up.shape[0]
    ts = min(TS_C, s)

    def body(a_ref, g_ref, cw_ref, cb_ref, o_ref, c_ref, halo):
        @pl.when(pl.program_id(0) == 0)
        def _():
            halo[...] = jnp.zeros_like(halo)

        a = a_ref[...]
        c = _conv(a, halo[...], cw_ref[...], cb_ref[...])
        _, h = _gelu_parts(c)
        c_ref[...] = c
        o_ref[...] = (c * h * g_ref[...]).astype(BF16)
        halo[...] = a[ts - 8:ts, :]

    tile = pl.BlockSpec((ts, DFF), lambda i: (i, 0))
    return _call(body, name="conv_fwd", grid=(s // ts,),
                 in_specs=[tile, pl.BlockSpec((ts, DFF), lambda i: (i, 1)),
                           pl.BlockSpec((3, DFF), lambda i: (0, 0)), pl.BlockSpec((1, DFF), lambda i: (0, 0))],
                 out_specs=[tile, tile],
                 out_shape=[jax.ShapeDtypeStruct((s, DFF), BF16), jax.ShapeDtypeStruct((s, DFF), F32)],
                 scratch_shapes=[pltpu.VMEM((8, DFF), F32)],
                 compiler_params=_params("arbitrary"))(up, up, cw, cb)


def conv_bwd(up, c, cw, dact):
    s = up.shape[0]
    ts = min(TS_C, s)
    ns = s // ts

    def body(a_ref, g_ref, c_ref, cw_ref, dact_ref, dup_ref, dcw_ref, dcb_ref, halo):
        @pl.when(pl.program_id(0) == 0)
        def _():
            halo[...] = jnp.zeros_like(halo)
            dcw_ref[...] = jnp.zeros_like(dcw_ref)
            dcb_ref[...] = jnp.zeros_like(dcb_ref)

        a = a_ref[...]
        cw = cw_ref[...]
        cv = c_ref[...]
        dact = dact_ref[...]
        c2, h = _gelu_parts(cv)
        dup_ref[:, DFF:] = (dact * (cv * h)).astype(BF16)
        dgel = h + cv * (2.0 * h * (1.0 - h)) * (GELU_K0 + 3.0 * GELU_K1 * c2)
        dc = dact * g_ref[...] * dgel
        up1 = _shift_up(dc, halo[...], 1)
        up2 = _shift_up(dc, halo[...], 2)
        dup_ref[:, :DFF] = (cw[2:3, :] * dc + cw[1:2, :] * up1 + cw[0:1, :] * up2).astype(BF16)
        dcw_ref[0:1, :] += jnp.sum(a * up2, axis=0, keepdims=True)
        dcw_ref[1:2, :] += jnp.sum(a * up1, axis=0, keepdims=True)
        dcw_ref[2:3, :] += jnp.sum(a * dc, axis=0, keepdims=True)
        dcb_ref[...] += jnp.sum(dc, axis=0, keepdims=True)
        halo[...] = dc[0:8, :]

    rev = lambda col: (lambda i: (ns - 1 - i, col))
    return _call(body, name="conv_bwd", grid=(ns,),
                 in_specs=[pl.BlockSpec((ts, DFF), rev(0)), pl.BlockSpec((ts, DFF), rev(1)),
                           pl.BlockSpec((ts, DFF), rev(0)), pl.BlockSpec((3, DFF), lambda i: (0, 0)),
                           pl.BlockSpec((ts, DFF), rev(0))],
                 out_specs=[pl.BlockSpec((ts, 2 * DFF), rev(0)), pl.BlockSpec((3, DFF), lambda i: (0, 0)),
                            pl.BlockSpec((1, DFF), lambda i: (0, 0))],
                 out_shape=[jax.ShapeDtypeStruct((s, 2 * DFF), BF16), jax.ShapeDtypeStruct((3, DFF), F32),
                            jax.ShapeDtypeStruct((1, DFF), F32)],
                 scratch_shapes=[pltpu.VMEM((8, DFF), F32)],
                 compiler_params=_params("arbitrary"))(up, up, c, cw, dact)


def _blockdiag_expand(m):
    m4 = m.reshape(S5_H, 2, S5_G, S5_P)
    eye = jnp.eye(S5_G, dtype=bool)[:, None, None, :, None]
    return jnp.where(eye, m4[None], 0.0).reshape(S5_W, 2 * S5_N)


def _blockdiag_extract(mbd):
    m5 = mbd.reshape(S5_G, S5_H, 2, S5_G, S5_P)
    diag = jnp.stack([m5[g, :, :, g, :] for g in range(S5_G)], axis=2)
    return diag.reshape(S5_H, 2 * S5_N)


def _c_expand(c_re, c_im):
    c4 = jnp.stack([c_re, c_im], axis=2)
    eye = jnp.eye(S5_G, dtype=bool)[:, None, None, :, None]
    return jnp.where(eye, c4[:, :, :, None, :], 0.0).reshape(S5_W, 2 * S5_N)


def _c_extract(cbd):
    m5 = cbd.reshape(S5_G, S5_H, 2, S5_G, S5_P)
    d = jnp.stack([m5[g, :, :, g, :] for g in range(S5_G)], axis=0)
    return d[:, :, 0, :], d[:, :, 1, :]


def _glu_expand(w):
    eye = jnp.eye(S5_G, dtype=bool)[:, None, :, None]
    return jnp.where(eye, w[:, :, None, :], 0.0).reshape(S5_W, S5_W)


def _glu_extract(wbd):
    m4 = wbd.reshape(S5_G, S5_H, S5_G, S5_H)
    return jnp.stack([m4[g, :, g, :] for g in range(S5_G)], axis=0)


SMALL = ("b_f", "gm_ln_g", "gm_ln_b", "gm_w_s", "gm_b_s", "s5_lam_re", "s5_lam_im", "s5_log_dt", "s5_b_re", "s5_b_im",
         "s5_c_re", "s5_c_im", "s5_d", "s5_w_glu", "s5_b_glu", "ln1_g", "ln1_b", "conv_b", "ln2_g", "ln2_b")


def _layer_operands(sp, l):
    f = {}
    f["bf"] = jnp.pad(sp["b_f"][l][None, :], ((0, 0), (0, 128 - FX_H)))
    f["gm_lg"] = sp["gm_ln_g"][l].reshape(1, GM_W)
    f["gm_lb"] = sp["gm_ln_b"][l].reshape(1, GM_W)
    f["gm_ws"] = sp["gm_w_s"][l]
    f["gm_bst"] = sp["gm_b_s"][l].T
    f["lr"] = sp["s5_lam_re"][l].reshape(1, S5_N)
    f["li"] = sp["s5_lam_im"][l].reshape(1, S5_N)
    f["ldt"] = jnp.repeat(sp["s5_log_dt"][l], S5_P).reshape(1, S5_N)
    bt = lambda b: jnp.transpose(b, (2, 0, 1)).reshape(S5_H, S5_N)
    f["bt"] = jnp.concatenate([bt(sp["s5_b_re"][l]), bt(sp["s5_b_im"][l])], axis=1)
    f["cbd"] = _c_expand(sp["s5_c_re"][l], sp["s5_c_im"][l])
    f["drow"] = sp["s5_d"][l].reshape(1, S5_W)
    f["wg"] = _glu_expand(sp["s5_w_glu"][l])
    f["bg"] = sp["s5_b_glu"][l].reshape(1, S5_W)
    for n in ("ln1_g", "ln1_b", "ln2_g", "ln2_b"):
        f[n] = sp[n][l][None, :]
    f["cb"] = sp["conv_b"][l][None, :]
    return f


def layer_fwd_mix(x, mod, w_in, f):
    p = mm_mod(x, w_in, NP, "in_proj", mod, (0, 1))
    ygm = gm_fwd(p, f["gm_lg"], f["gm_lb"], f["gm_ws"], f["gm_bst"])
    arow, bbt = s5_prep_fwd(f["lr"], f["li"], f["ldt"], f["bt"])
    bbd = _blockdiag_expand(bbt)
    ys5, st = s5_fwd(p, arow, bbd, f["cbd"], f["drow"], f["wg"], f["bg"])
    qa, ka, qat, kat, vt = fox_prep(p, f["bf"])
    yfx, lse = attn(qat, ka, vt)
    mixcat = jnp.concatenate([ygm, ys5, yfx], axis=1)
    return mixcat, dict(f=f, x=x, p=p, arow=arow, bbd=bbd, st=st, qa=qa, ka=ka, qat=qat, kat=kat, yfx=yfx, lse=lse,
                        mixcat=mixcat)


def layer_fwd_rest(x, mixcat, mod, w, f, saved):
    mix, x1 = proj_post(mixcat, w["w_out"], x, mod, 2, f["ln1_g"], f["ln1_b"], "out_proj")
    up = mm_mod(x1, w["w_up"], DFF, "up_proj", mod, (3, 4))
    act, conv = conv_fwd(up, w["conv_w"], f["cb"])
    ffn, x2 = proj_post(act, w["w_down"], x1, mod, 5, f["ln2_g"], f["ln2_b"], "down_proj")
    return x2, dict(saved, mix=mix, x1=x1, up=up, conv=conv, act=act, ffn=ffn)


def layer_fwd(x, mod, w, f):
    mixcat, saved = layer_fwd_mix(x, mod, w["w_in"], f)
    return layer_fwd_rest(x, mixcat, mod, w, f, saved)


def layer_bwd_ffn(dx, sv, mod, w):
    f = sv["f"]
    dx1, dffn, dg2, dlg2, dlb2 = post_bwd(sv["x1"], sv["ffn"], mod, 5, f["ln2_g"], f["ln2_b"], dx, "post2_bwd")
    g_down = mm_tn(sv["act"], dffn, D, "down_dw")
    dact = mm_nt(dffn, w["w_down"], "down_dx")
    dup, dcw, dcb = conv_bwd(sv["up"], sv["conv"], w["conv_w"], dact)
    g_up = mm_tn(dup, sv["x1"], D // 2, "up_dw", mod=mod, rows=(3, 4))
    dx1, dsh2, dsc2 = mm_nt_mod(dup, w["w_up"], sv["x1"], dx1, mod, (3, 4), "up_dx")
    return dx1, dict(w_up=g_up, w_down=g_down, conv_w=dcw), dict(dsh2=dsh2, dsc2=dsc2, dg2=dg2, conv_b=dcb[0],
                                                                 ln2_g=dlg2[0], ln2_b=dlb2[0])


def layer_bwd_mix(dx1, sv, mod, w, part):
    f = sv["f"]
    dx0, dmix, dg1, dlg1, dlb1 = post_bwd(sv["x"], sv["mix"], mod, 2, f["ln1_g"], f["ln1_b"], dx1, "post1_bwd")
    g_out = mm_tn(sv["mixcat"], dmix, D, "out_dw")
    dmc = mm_nt(dmix, w["w_out"], "out_dx")
    duv, dgm_lg, dgm_lb, dgm_ws, dgm_bst = gm_bwd(sv["p"], f["gm_lg"], f["gm_lb"], f["gm_ws"], f["gm_bst"], dmc)
    du5, da, dbbd, dcbd, dd5, dwg, dbg = s5_bwd(sv["p"], sv["st"], sv["arow"], sv["bbd"], f["cbd"], f["drow"],
                                                f["wg"], f["bg"], dmc)
    dlr, dli, dldt, dbt = s5_prep_bwd(f["lr"], f["li"], f["ldt"], f["bt"], da, _blockdiag_extract(dbbd))
    dq, dk, dv, dfq, dfk = attn_grad(sv["qa"], sv["qat"], sv["ka"], sv["kat"], sv["p"], sv["yfx"], sv["lse"], dmc)
    dff, dbf = fox_prep_grad(sv["p"], f["bf"], dfq, dfk)
    dp = jnp.concatenate([duv, du5, dq, dk, dv, dff], axis=1)
    g_in = mm_tn(dp, sv["x"], D, "in_dw", mod=mod, rows=(0, 1))
    dx, dsh1, dsc1 = mm_nt_mod(dp, w["w_in"], sv["x"], dx0, mod, (0, 1), "in_dx")

    dmod = jnp.concatenate([dsh1, dsc1, dg1, part["dsh2"], part["dsc2"], part["dg2"]], axis=0)
    dc_re, dc_im = _c_extract(dcbd)
    dbt4 = dbt.reshape(S5_H, 2, S5_G, S5_P)
    vals = dict(b_f=dbf[0, :FX_H], gm_ln_g=dgm_lg.reshape(GM_H, HD), gm_ln_b=dgm_lb.reshape(GM_H, HD),
                gm_w_s=dgm_ws, gm_b_s=dgm_bst.T, s5_lam_re=dlr.reshape(S5_G, S5_P),
                s5_lam_im=dli.reshape(S5_G, S5_P), s5_log_dt=dldt[0, :S5_G],
                s5_b_re=jnp.transpose(dbt4[:, 0], (1, 2, 0)), s5_b_im=jnp.transpose(dbt4[:, 1], (1, 2, 0)),
                s5_c_re=dc_re, s5_c_im=dc_im, s5_d=dd5.reshape(S5_G, S5_H), s5_w_glu=_glu_extract(dwg),
                s5_b_glu=dbg.reshape(S5_G, S5_H), ln1_g=dlg1[0], ln1_b=dlb1[0], conv_b=part["conv_b"],
                ln2_g=part["ln2_g"], ln2_b=part["ln2_b"])
    return dx, dict(w_in=g_in, w_out=g_out), vals, dmod


def layer_bwd(dx, sv, mod, w):
    dx1, g_ffn, part = layer_bwd_ffn(dx, sv, mod, w)
    dx, g_mix, vals, dmod = layer_bwd_mix(dx1, sv, mod, w, part)
    return dx, dict(g_ffn, **g_mix), vals, dmod


def local_step(x, target, mods, big, sp):
    saved = []
    for l in range(DEPTH):
        x, sv = layer_fwd(x, mods[l], big[l], _layer_operands(sp, l))
        saved.append(sv)
    loss_tile, dx = loss_kernel(x, target)
    gbig, vals, dmods = [None] * DEPTH, [None] * DEPTH, [None] * DEPTH
    for l in reversed(range(DEPTH)):
        dx, gbig[l], vals[l], dmods[l] = layer_bwd(dx, saved[l], mods[l], big[l])
    gsm = {n: jnp.stack([v[n] for v in vals]) for n in SMALL}
    return loss_tile, dx, gbig, gsm, jnp.stack(dmods)


def _my_index():
    return 4 * lax.axis_index("x") + 2 * lax.axis_index("y") + lax.axis_index("c")


def exchange(tensors, scatter, name):
    n = len(tensors)

    def body(*refs):
        ins, outs = refs[:n], refs[n:2 * n]
        send_sems, recv_sems, local_sems = refs[2 * n:]
        x, y, c = lax.axis_index("x"), lax.axis_index("y"), lax.axis_index("c")
        me = 4 * x + 2 * y + c
        local = []
        for t in range(n):
            cp = pltpu.make_async_copy(ins[t].at[me] if scatter else ins[t], outs[t].at[me], local_sems.at[t])
            cp.start()
            local.append(cp)
        remote = []
        for m in range(1, NDEV):
            px = 1 - x if m & 4 else x
            py = 1 - y if m & 2 else y
            pc = 1 - c if m & 1 else c
            peer = 4 * px + 2 * py + pc
            for t in range(n):
                k = t * (NDEV - 1) + m - 1
                cp = pltpu.make_async_remote_copy(
                    src_ref=ins[t].at[peer] if scatter else ins[t], dst_ref=outs[t].at[me],
                    send_sem=send_sems.at[k], recv_sem=recv_sems.at[k],
                    device_id=(px, py, pc), device_id_type=MESH_IDS)
                cp.start()
                remote.append(cp)
        for cp in remote:
            cp.wait()
        for cp in local:
            cp.wait()

    hbm = pl.BlockSpec(memory_space=pltpu.HBM)
    out_shape = [jax.ShapeDtypeStruct(t.shape if scatter else (NDEV,) + t.shape, t.dtype) for t in tensors]
    return _call(body, name=name, in_specs=[hbm] * n, out_specs=[hbm] * n, out_shape=out_shape,
                 scratch_shapes=[pltpu.SemaphoreType.DMA((n * (NDEV - 1),)), pltpu.SemaphoreType.DMA((n * (NDEV - 1),)),
                                 pltpu.SemaphoreType.DMA((n,))])(*tensors)


def _peers():
    x, y, c = lax.axis_index("x"), lax.axis_index("y"), lax.axis_index("c")
    out = []
    for m in range(1, NDEV):
        px = 1 - x if m & 4 else x
        py = 1 - y if m & 2 else y
        pc = 1 - c if m & 1 else c
        out.append(((px, py, pc), 4 * px + 2 * py + pc))
    return 4 * x + 2 * y + c, out


def _split_copies(v_refs, land_refs, send_sems, recv_sems, scatter):
    me, peers = _peers()
    return [pltpu.make_async_remote_copy(
        src_ref=v_ref.at[idx] if scatter else v_ref, dst_ref=land_ref.at[me],
        send_sem=send_sems.at[t * (NDEV - 1) + k], recv_sem=recv_sems.at[t * (NDEV - 1) + k],
        device_id=pos, device_id_type=MESH_IDS)
        for t, (v_ref, land_ref) in enumerate(zip(v_refs, land_refs)) for k, (pos, idx) in enumerate(peers)]


_HBM_SPEC = pl.BlockSpec(memory_space=pltpu.HBM)
_SEM_SPEC = pl.BlockSpec(memory_space=pltpu.SEMAPHORE)
_SPLIT_EFFECT = pltpu.SideEffectType.DATAFLOW_SIDE_EFFECTING


def exchange_start(tensors, scatter, name):
    n = len(tensors)
    land_shapes = [t.shape if scatter else (NDEV,) + t.shape for t in tensors]

    def body(*refs):
        v_refs, land_refs = refs[:n], refs[n:2 * n]
        send_sems, recv_sems = refs[2 * n], refs[2 * n + 1]
        token = refs[-1]
        for cp in _split_copies(v_refs, land_refs, send_sems, recv_sems, scatter):
            cp.start()
        token[...] = jnp.zeros_like(token)

    sems = pltpu.SemaphoreType.DMA((n * (NDEV - 1),))
    out = _call(
        body, name=name,
        out_shape=(sems, sems, *[pltpu.HBM(t.shape, t.dtype) for t in tensors],
                   *[pltpu.HBM(s, t.dtype) for s, t in zip(land_shapes, tensors)], jax.ShapeDtypeStruct((8, 128), F32)),
        in_specs=(_HBM_SPEC,) * (2 * n),
        out_specs=(_SEM_SPEC, _SEM_SPEC) + (_HBM_SPEC,) * (2 * n) + (pl.BlockSpec(memory_space=pltpu.VMEM),),
        input_output_aliases={i: i + 2 for i in range(2 * n)},
        compiler_params=pltpu.CompilerParams(has_side_effects=_SPLIT_EFFECT),
    )(*[pltpu.with_memory_space_constraint(t, pltpu.HBM) for t in tensors],
      *[pltpu.with_memory_space_constraint(lax.empty(s, t.dtype), pltpu.HBM) for s, t in zip(land_shapes, tensors)])
    return out[0], out[1], list(out[2:2 + n]), list(out[2 + n:2 + 2 * n]), out[-1]


def exchange_wait(started, after, scatter, name):
    send_sems, recv_sems, v_thru, land_thru, _ = started
    n = len(v_thru)

    def body(*refs):
        v_refs, land_refs = refs[:n], refs[n:2 * n]
        for cp in _split_copies(v_refs, land_refs, refs[2 * n], refs[2 * n + 1], scatter):
            cp.wait_send()
            cp.wait_recv()

    out = _call(
        body, name=name,
        out_shape=tuple(pltpu.HBM(t.shape, t.dtype) for t in v_thru + land_thru),
        in_specs=(_HBM_SPEC,) * (2 * n) + (_SEM_SPEC, _SEM_SPEC, pl.BlockSpec(memory_space=pl.ANY)),
        out_specs=(_HBM_SPEC,) * (2 * n), input_output_aliases={i: i for i in range(2 * n)},
        compiler_params=pltpu.CompilerParams(has_side_effects=_SPLIT_EFFECT),
    )(*v_thru, *land_thru, send_sems, recv_sems, after)
    return list(out[:n]), list(out[n:])


def mod_slices(c_all, w_ada, b_loc):
    nl, _, nc = w_ada.shape

    def body(c_ref, w_ref, b_ref, o_ref):
        cv = c_ref[...]
        o_ref[0] = _nn(cv * jax.nn.sigmoid(cv), w_ref[0]) + b_ref[0]

    return _call(body, name="mod_slices", grid=(nl,),
                 in_specs=[pl.BlockSpec((NDEV, D), lambda l: (0, 0)), pl.BlockSpec((1, D, nc), lambda l: (l, 0, 0)),
                           pl.BlockSpec((1, 1, nc), lambda l: (l, 0, 0))],
                 out_specs=pl.BlockSpec((1, NDEV, nc), lambda l: (l, 0, 0)),
                 out_shape=jax.ShapeDtypeStruct((nl, NDEV, nc), F32),
                 compiler_params=_params("arbitrary"))(c_all, w_ada, b_loc.reshape(nl, 1, nc))


def ada_grad(c_all, dm_loc):
    nl, _, nc = dm_loc.shape

    def body(c_ref, d_ref, o_ref):
        cv = c_ref[...]
        o_ref[0] = _tn(cv * jax.nn.sigmoid(cv), d_ref[0])

    return _call(body, name="ada_grad", grid=(nl,),
                 in_specs=[pl.BlockSpec((NDEV, D), lambda l: (0, 0)), pl.BlockSpec((1, NDEV, nc), lambda l: (l, 0, 0))],
                 out_specs=pl.BlockSpec((1, D, nc), lambda l: (l, 0, 0)),
                 out_shape=jax.ShapeDtypeStruct((nl, D, nc), F32),
                 compiler_params=_params("arbitrary"))(c_all, dm_loc)


def sum_chunks(chunks):
    r = chunks.shape[1]

    def body(c_ref, o_ref):
        acc = c_ref[0]
        for i in range(1, NDEV):
            acc = acc + c_ref[i]
        o_ref[...] = acc

    return _call(body, name="sum_chunks", out_shape=jax.ShapeDtypeStruct((r, 128), F32))(chunks)


def _row_tile(r):
    if r <= 256:
        return r
    for t in range(256, 7, -8):
        if r % t == 0:
            return t
    return r


def adamw(w, m, v, g=None, chunks=None, name="adamw"):
    r, cdim = w.shape
    tr = _row_tile(r)
    bc1 = 1.0 - ADAM_B1 ** ADAM_STEP
    bc2 = 1.0 - ADAM_B2 ** ADAM_STEP

    def body(g_ref, w_ref, m_ref, v_ref, go_ref, d_ref, mo_ref, vo_ref):
        if chunks is None:
            grad = g_ref[...]
        else:
            grad = g_ref[0].astype(F32)
            for i in range(1, NDEV):
                grad = grad + g_ref[i].astype(F32)
        mn = ADAM_B1 * m_ref[...] + (1.0 - ADAM_B1) * grad
        vn = ADAM_B2 * v_ref[...] + (1.0 - ADAM_B2) * (grad * grad)
        m_hat = mn / bc1
        v_hat = vn / bc2
        go_ref[...] = grad
        d_ref[...] = -ADAM_LR * (m_hat / (jnp.sqrt(v_hat) + ADAM_EPS) + ADAM_WD * w_ref[...])
        mo_ref[...] = mn
        vo_ref[...] = vn

    tile = pl.BlockSpec((tr, cdim), lambda i: (i, 0))
    gspec = tile if chunks is None else pl.BlockSpec((NDEV, tr, cdim), lambda i: (0, i, 0))
    shp = jax.ShapeDtypeStruct((r, cdim), F32)
    return _call(body, name=name, grid=(r // tr,), in_specs=[gspec, tile, tile, tile],
                 out_specs=[tile] * 4, out_shape=[shp] * 4,
                 compiler_params=_params("arbitrary"))(g if chunks is None else chunks, w, m, v)


def adamw_layers(w, m, v, chunks, name):
    nl, r, cdim = w.shape
    tr = _row_tile(r)
    bc1 = 1.0 - ADAM_B1 ** ADAM_STEP
    bc2 = 1.0 - ADAM_B2 ** ADAM_STEP
    outs = [lax.empty(w.shape, F32) for _ in range(4)]
    for l in range(nl):
        def body(g_ref, w_ref, m_ref, v_ref, p0, p1, p2, p3, go_ref, d_ref, mo_ref, vo_ref):
            grad = g_ref[0].astype(F32)
            for i in range(1, NDEV):
                grad = grad + g_ref[i].astype(F32)
            mn = ADAM_B1 * m_ref[...] + (1.0 - ADAM_B1) * grad
            vn = ADAM_B2 * v_ref[...] + (1.0 - ADAM_B2) * (grad * grad)
            go_ref[...] = grad
            d_ref[...] = -ADAM_LR * ((mn / bc1) / (jnp.sqrt(vn / bc2) + ADAM_EPS) + ADAM_WD * w_ref[...])
            mo_ref[...] = mn
            vo_ref[...] = vn

        tile = pl.BlockSpec((None, tr, cdim), lambda i, l=l: (l, i, 0))
        whole = pl.BlockSpec(memory_space=pl.ANY)
        outs = _call(body, name=f"{name}_{l}", grid=(r // tr,),
                     in_specs=[pl.BlockSpec((NDEV, tr, cdim), lambda i: (0, i, 0)), tile, tile, tile] + [whole] * 4,
                     out_specs=[tile] * 4, out_shape=[jax.ShapeDtypeStruct(w.shape, F32)] * 4,
                     input_output_aliases={4: 0, 5: 1, 6: 2, 7: 3},
                     compiler_params=_params("arbitrary"))(chunks[l], w, m, v, *outs)
    return outs


WEIGHTS = ("w_ada", "b_ada", "w_in", "b_f", "gm_ln_g", "gm_ln_b", "gm_w_s", "gm_b_s", "s5_lam_re", "s5_lam_im",
           "s5_log_dt", "s5_b_re", "s5_b_im", "s5_c_re", "s5_c_im", "s5_d", "s5_w_glu", "s5_b_glu", "w_out", "ln1_g",
           "ln1_b", "w_up", "conv_w", "conv_b", "w_down", "ln2_g", "ln2_b")
LARGE = ("w_in", "w_out", "w_up", "w_down")
TRANSPOSED = ("w_in", "w_up")
PACKED = ("b_ada",) + SMALL
PACK_SEG = 8 * 128


def _gather_cols(g):
    nd, nl, r, c = g.shape
    return jnp.transpose(g, (1, 2, 0, 3)).reshape(nl, r, nd * c)


def _chunk_cols(g):
    nl, r, c8 = g.shape
    return jnp.transpose(g.reshape(nl, r, NDEV, c8 // NDEV), (2, 0, 1, 3))


def _join_rows(g):
    nd, r, c = g.shape
    return g.reshape(nd * r, c)


def _split_rows(g):
    r8, c = g.shape
    return g.reshape(NDEV, r8 // NDEV, c)


def _pack(parts):
    segs = []
    for n in PACKED:
        flat = parts[n].reshape(-1)
        segs.append(jnp.pad(flat, (0, -flat.shape[0] % PACK_SEG)).reshape(-1, 128))
    rows = jnp.concatenate(segs, axis=0)
    return jnp.pad(rows, ((0, -rows.shape[0] % (NDEV * 8)), (0, 0)))


def _unpack(rows, shapes):
    out, off = {}, 0
    for n in PACKED:
        size = math.prod(shapes[n])
        nrows = -(-size // PACK_SEG) * 8
        out[n] = rows[off:off + nrows].reshape(-1)[:size].reshape(shapes[n])
        off += nrows
    return out


def kernel(x, c, w_ada, b_ada, w_in, b_f, gm_ln_g, gm_ln_b, gm_w_s, gm_b_s, s5_lam_re, s5_lam_im, s5_log_dt, s5_b_re, s5_b_im, s5_c_re, s5_c_im, s5_d, s5_w_glu, s5_b_glu, w_out, ln1_g, ln1_b, w_up, conv_w, conv_b, w_down, ln2_g, ln2_b, loss_target, m_w_ada, m_b_ada, m_w_in, m_b_f, m_gm_ln_g, m_gm_ln_b, m_gm_w_s, m_gm_b_s, m_s5_lam_re, m_s5_lam_im, m_s5_log_dt, m_s5_b_re, m_s5_b_im, m_s5_c_re, m_s5_c_im, m_s5_d, m_s5_w_glu, m_s5_b_glu, m_w_out, m_ln1_g, m_ln1_b, m_w_up, m_conv_w, m_conv_b, m_w_down, m_ln2_g, m_ln2_b, v_w_ada, v_b_ada, v_w_in, v_b_f, v_gm_ln_g, v_gm_ln_b, v_gm_w_s, v_gm_b_s, v_s5_lam_re, v_s5_lam_im, v_s5_log_dt, v_s5_b_re, v_s5_b_im, v_s5_c_re, v_s5_c_im, v_s5_d, v_s5_w_glu, v_s5_b_glu, v_w_out, v_ln1_g, v_ln1_b, v_w_up, v_conv_w, v_conv_b, v_w_down, v_ln2_g, v_ln2_b):
    given = dict(locals())
    wts = {n: given[n] for n in WEIGHTS}
    mom = {n: given["m_" + n] for n in WEIGHTS}
    var = {n: given["v_" + n] for n in WEIGHTS}
    nl = w_ada.shape[0]
    me = _my_index()
    ada_cols = w_ada.shape[2]

    (c_all,) = exchange([c], False, "gather_c")
    c_all = c_all.reshape(NDEV, D)
    b_loc = lax.dynamic_slice_in_dim(b_ada, me * ada_cols, ada_cols, axis=1)
    mod_part = mod_slices(c_all, w_ada, b_loc)

    mod_all, conv_all = exchange([mod_part, conv_w], False, "gather_mod")
    mod_mine = lax.dynamic_index_in_dim(mod_all, me, axis=2, keepdims=False)
    mods = jnp.transpose(mod_mine, (1, 0, 2)).reshape(nl, 6, D)
    mods = jnp.pad(mods, ((0, 0), (0, 2), (0, 0)))
    conv_full = _gather_cols(conv_all)
    sp = {n: wts[n] for n in SMALL}
    rowwise = {n: [jnp.swapaxes(a[n], 1, 2) if n in TRANSPOSED else a[n] for a in (wts, mom, var)] for n in LARGE}

    def joined(own, land):
        return _join_rows(lax.dynamic_update_index_in_dim(land, own, me, 0))

    def block(l, names):
        return [rowwise[n][0][l].astype(BF16) for n in names]

    def chunked(grads, names):
        return [_split_rows(grads[n]) for n in names]

    head, tail = LARGE[:1], LARGE[1:]
    got_head = exchange_start(block(0, head), False, "gather_start_0_in")
    rest = block(0, tail)
    rest[0] = rest[0] + got_head[4][0, 0].astype(BF16)
    got_tail = exchange_start(rest, False, "gather_start_0_rest")
    xl, saved, weights = x[0], [], []
    for l in range(nl):
        if l == 0:
            own, land = exchange_wait(got_head, got_tail[4], False, "gather_wait_0_in")
            w = {n: joined(o, g) for n, o, g in zip(head, own, land)}
        else:
            own, land = exchange_wait(started, xl, False, f"gather_wait_{l}")
            w = {n: joined(o, g) for n, o, g in zip(LARGE, own, land)}
        mod_l = mods[l]
        if l + 1 < nl:
            nxt, w["w_in"] = lax.optimization_barrier((block(l + 1, LARGE), w["w_in"]))
            started = exchange_start(nxt, False, f"gather_start_{l + 1}")
            mod_l = mod_l + started[4][0, 0]
        w["w_in"] = jnp.pad(w["w_in"], ((0, NP - D_IN), (0, 0)))
        f = _layer_operands(sp, l)
        mixcat, sv = layer_fwd_mix(xl, mod_l, w["w_in"], f)
        if l == 0:
            own, land = exchange_wait(got_tail, mixcat, False, "gather_wait_0_rest")
            w.update({n: joined(o, g) for n, o, g in zip(tail, own, land)})
        w["conv_w"] = conv_full[l]
        xl, sv = layer_fwd_rest(xl, mixcat, mod_l, w, f, sv)
        weights.append(w)
        saved.append(sv)

    loss_tile, dx = loss_kernel(xl, loss_target[0])

    ffn_names, mix_names = ("w_up", "w_down"), ("w_in", "w_out")
    scattering, vals, dmods, gconv = [None] * nl, [None] * nl, [None] * nl, [None] * nl
    token = jnp.zeros((), F32)
    for l in reversed(range(nl)):
        mod_l = mods[l] + token
        dx1, g_ffn, part = layer_bwd_ffn(dx, saved[l], mod_l, weights[l])
        gconv[l] = g_ffn["conv_w"]
        if l == 0:
            sent_ffn = exchange_start(chunked(g_ffn, ffn_names), True, "scatter_start_0_ffn")
            mod_l = mod_l + sent_ffn[4][0, 0]
        dx, g_mix, vals[l], dmods[l] = layer_bwd_mix(dx1, saved[l], mod_l, weights[l], part)
        g_mix["w_in"] = g_mix["w_in"][:D_IN]
        if l == 0:
            scattering[l] = [(ffn_names, sent_ffn),
                             (mix_names, exchange_start(chunked(g_mix, mix_names), True, "scatter_start_0_mix"))]
        else:
            sent = exchange_start(chunked(dict(g_ffn, **g_mix), LARGE), True, f"scatter_start_{l}")
            scattering[l] = [(LARGE, sent)]
            token = sent[4][0, 0]
    gx = dx
    dmods = jnp.stack(dmods)
    gsm = {n: jnp.stack([v[n] for v in vals]) for n in SMALL}

    gsm["b_ada"] = dmods.reshape(nl, 6 * D)
    packed = _pack(gsm).reshape(NDEV, -1, 128)
    conv_recv, small_recv = exchange([_chunk_cols(jnp.stack(gconv)), packed], True, "scatter_small")
    small_sum = sum_chunks(small_recv)
    small_all, dmod_all = exchange([small_sum, dmods.reshape(nl, 6 * D)], False, "gather_small")

    received = [dict() for _ in range(nl)]

    def arrive(l, k, after):
        names, sent = scattering[l][k]
        own, land = exchange_wait(sent, after, True, f"scatter_wait_{l}_{k}")
        for n, o, g in zip(names, own, land):
            mine = lax.dynamic_index_in_dim(o, me, 0, keepdims=False)
            received[l][n] = lax.dynamic_update_index_in_dim(g, mine, me, 0)
        return land[0]

    after = small_all
    for l in reversed(range(nl)):
        after = arrive(l, 0, after)
    out = {}
    def update(n):
        res = adamw_layers(*rowwise[n], [received[l][n] for l in range(nl)], "adamw_" + n)
        return [jnp.swapaxes(r, 1, 2) for r in res] if n in TRANSPOSED else res

    for n in ffn_names:
        out[n] = update(n)
    arrive(0, 1, out[ffn_names[-1]][0])
    for n in mix_names:
        out[n] = update(n)
    shp = conv_w.shape
    two_d = lambda a: a.reshape(shp[0] * shp[1], shp[2])
    res = adamw(two_d(conv_w), two_d(m_conv_w), two_d(v_conv_w),
                chunks=conv_recv.reshape(NDEV, shp[0] * shp[1], shp[2]), name="adamw_conv_w")
    out["conv_w"] = [r.reshape(shp) for r in res]

    dm_loc = lax.dynamic_slice_in_dim(dmod_all, me * ada_cols, ada_cols, axis=2)
    g_ada = ada_grad(c_all, jnp.transpose(dm_loc, (1, 0, 2)))
    two_d = lambda a: a.reshape(nl * D, ada_cols)
    res = adamw(two_d(w_ada), two_d(m_w_ada), two_d(v_w_ada), g=two_d(g_ada), name="adamw_w_ada")
    out["w_ada"] = [r.reshape(w_ada.shape) for r in res]

    shapes = {n: wts[n].shape for n in PACKED}
    res = adamw(_pack(wts), _pack(mom), _pack(var), g=small_all.reshape(-1, 128), name="adamw_small")
    unpacked = [_unpack(r, shapes) for r in res]
    for n in PACKED:
        out[n] = [u[n] for u in unpacked]

    loss = lax.psum(loss_tile[0, 0], ("x", "y", "c"))
    return (loss, gx[None], *[out[n][0] for n in WEIGHTS], *[out[n][1] for n in WEIGHTS],
            *[out[n][2] for n in WEIGHTS], *[out[n][3] for n in WEIGHTS])
```

```python
import math

import jax
import jax.numpy as jnp
from jax import lax
from jax.experimental import pallas as pl
from jax.experimental.pallas import tpu as pltpu

F32 = jnp.float32
BF16 = jnp.bfloat16
MESH_IDS = pl.DeviceIdType.MESH

D = 1024
SEQ = 4096
DEPTH = 4
NDEV = 8
HD = 64
GM_W = 256
GM_H = 4
GM_C = 128
S5_W = 256
S5_G = 16
S5_H = 16
S5_P = 64
S5_N = S5_G * S5_P
FX_W = 512
FX_H = 8
D_IN = 2 * GM_W + S5_W + 3 * FX_W + FX_H
NP = 2432
FF_COL = 2304
DFF = 2816
LN_EPS = 1e-5
DN_ALPHA = (2.0 * DEPTH) ** 0.25
NEG_INF = -1e30
ADAM_LR = 0.001
ADAM_B1 = 0.9
ADAM_B2 = 0.999
ADAM_EPS = 1e-08
ADAM_WD = 0.01
ADAM_STEP = 10

V7X_VMEM_LIMIT = 56 * 1024 * 1024
TS = 512
TS_C = 256
T_S5 = 512
TQ = 512
TQ_FWD = 1024
ATTN_HEADS = 2


def _call(body, **kw):
    return pl.pallas_call(body, **kw)


def _params(*sem):
    return pltpu.CompilerParams(dimension_semantics=sem if sem else None,
                                vmem_limit_bytes=V7X_VMEM_LIMIT)


def _nn(a, b):
    return jnp.dot(a.astype(BF16), b.astype(BF16), preferred_element_type=F32)


def _nt(a, b):
    return lax.dot_general(a.astype(BF16), b.astype(BF16), (((1,), (1,)), ((), ())),
                           preferred_element_type=F32)


def _tn(a, b):
    return lax.dot_general(a.astype(BF16), b.astype(BF16), (((0,), (0,)), ((), ())),
                           preferred_element_type=F32)


@jax.custom_vjp
def _bdot(a, b):
    return _nn(a, b)


def _bdot_fwd(a, b):
    return _nn(a, b), (a, b)


def _bdot_bwd(res, g):
    a, b = res
    return _nt(g, b), _tn(a, g)


_bdot.defvjp(_bdot_fwd, _bdot_bwd)


@jax.custom_vjp
def _bdot_nt(a, b):
    return _nt(a, b)


def _bdot_nt_fwd(a, b):
    return _nt(a, b), (a, b)


def _bdot_nt_bwd(res, g):
    a, b = res
    return _nn(g, b), _tn(g, a)


_bdot_nt.defvjp(_bdot_nt_fwd, _bdot_nt_bwd)


def _ln(r, g, b):
    mu = jnp.mean(r, axis=-1, keepdims=True)
    xc = r - mu
    var = jnp.mean(xc * xc, axis=-1, keepdims=True)
    return xc * lax.rsqrt(var + LN_EPS) * g + b


def _rows(shape):
    return lax.broadcasted_iota(jnp.int32, shape, 0)


def _lanes(shape):
    return lax.broadcasted_iota(jnp.int32, shape, 1)


def mm_mod(a, wt, tn, name, mod, rows):
    s, k = a.shape
    n = wt.shape[0]
    ts = min(TS, s)

    def body(a_ref, m_ref, w_ref, o_ref):
        h = a_ref[...] * (1.0 + m_ref[rows[1]:rows[1] + 1, :]) + m_ref[rows[0]:rows[0] + 1, :]
        o_ref[...] = _nt(h, w_ref[...])

    return _call(body, name=name, grid=(n // tn, s // ts),
                 in_specs=[pl.BlockSpec((ts, k), lambda j, i: (i, 0)), pl.BlockSpec((8, k), lambda j, i: (0, 0)),
                           pl.BlockSpec((tn, k), lambda j, i: (j, 0))],
                 out_specs=pl.BlockSpec((ts, tn), lambda j, i: (i, j)),
                 out_shape=jax.ShapeDtypeStruct((s, n), F32),
                 compiler_params=_params("arbitrary", "arbitrary"))(a, mod, wt)


def mm_nt(dy, w, name):
    s, n = dy.shape
    k = w.shape[0]
    ts = min(TS, s)

    def body(dy_ref, w_ref, o_ref):
        o_ref[...] = _nt(dy_ref[...], w_ref[...])

    return _call(body, name=name, grid=(s // ts,),
                 in_specs=[pl.BlockSpec((ts, n), lambda i: (i, 0)),
                           pl.BlockSpec((k, n), lambda i: (0, 0))],
                 out_specs=pl.BlockSpec((ts, k), lambda i: (i, 0)),
                 out_shape=jax.ShapeDtypeStruct((s, k), F32),
                 compiler_params=_params("arbitrary"))(dy, w)


def mm_nt_mod(dy, w, x, dres, mod, rows, name):
    s, n = dy.shape
    k = w.shape[1]
    ts = min(TS, s)

    def body(dy_ref, w_ref, x_ref, r_ref, m_ref, dx_ref, dsh_ref, dsc_ref):
        @pl.when(pl.program_id(0) == 0)
        def _():
            dsh_ref[...] = jnp.zeros_like(dsh_ref)
            dsc_ref[...] = jnp.zeros_like(dsc_ref)

        dh = _nn(dy_ref[...], w_ref[...])
        dx_ref[...] = r_ref[...] + dh * (1.0 + m_ref[rows[1]:rows[1] + 1, :])
        dsh_ref[...] += jnp.sum(dh, axis=0, keepdims=True)
        dsc_ref[...] += jnp.sum(dh * x_ref[...], axis=0, keepdims=True)

    row = pl.BlockSpec((1, k), lambda i: (0, 0))
    tile = pl.BlockSpec((ts, k), lambda i: (i, 0))
    return _call(body, name=name, grid=(s // ts,),
                 in_specs=[pl.BlockSpec((ts, n), lambda i: (i, 0)),
                           pl.BlockSpec((n, k), lambda i: (0, 0)), tile, tile,
                           pl.BlockSpec((8, k), lambda i: (0, 0))],
                 out_specs=[tile, row, row],
                 out_shape=[jax.ShapeDtypeStruct((s, k), F32),
                            jax.ShapeDtypeStruct((1, k), F32),
                            jax.ShapeDtypeStruct((1, k), F32)],
                 compiler_params=_params("arbitrary"))(dy, w, x, dres, mod)


def mm_tn(a, dy, tn, name, mod=None, rows=None):
    s, k = a.shape
    n = dy.shape[1]
    ts = min(TS, s)
    ns = s // ts

    def body(*refs):
        if mod is None:
            a_ref, dy_ref, o_ref, acc = refs
            h = dy_ref[...]
        else:
            a_ref, dy_ref, m_ref, o_ref, acc = refs
            h = dy_ref[...] * (1.0 + m_ref[rows[1]:rows[1] + 1, :]) + m_ref[rows[0]:rows[0] + 1, :]
        i = pl.program_id(1)

        @pl.when(i == 0)
        def _():
            acc[...] = jnp.zeros_like(acc)

        acc[...] += _tn(a_ref[...], h)

        @pl.when(i == ns - 1)
        def _():
            o_ref[...] = acc[...].astype(BF16)

    in_specs = [pl.BlockSpec((ts, k), lambda j, i: (i, 0)), pl.BlockSpec((ts, tn), lambda j, i: (i, j))]
    args = [a, dy]
    if mod is not None:
        in_specs.append(pl.BlockSpec((8, tn), lambda j, i: (0, j)))
        args.append(mod)
    return _call(body, name=name, grid=(n // tn, s // ts), in_specs=in_specs,
                 out_specs=pl.BlockSpec((k, tn), lambda j, i: (0, j)),
                 out_shape=jax.ShapeDtypeStruct((k, n), BF16),
                 scratch_shapes=[pltpu.VMEM((k, tn), F32)],
                 compiler_params=_params("arbitrary", "arbitrary"))(*args)


def _post_fn(x, br, gate, lg, lb):
    return _ln(DN_ALPHA * x + (1.0 + gate) * br, lg, lb)


def proj_post(a, w, x, mod, grow, lg, lb, name):
    s, k = a.shape
    ts = min(TS, s)

    def body(a_ref, w_ref, x_ref, m_ref, lg_ref, lb_ref, b_ref, o_ref):
        br = jnp.dot(a_ref[...], w_ref[...], preferred_element_type=F32)
        b_ref[...] = br
        o_ref[...] = _post_fn(x_ref[...], br, m_ref[grow:grow + 1, :], lg_ref[...], lb_ref[...])

    tile = pl.BlockSpec((ts, D), lambda i: (i, 0))
    row = pl.BlockSpec((1, D), lambda i: (0, 0))
    out = jax.ShapeDtypeStruct((s, D), F32)
    return _call(body, name=name, grid=(s // ts,),
                 in_specs=[pl.BlockSpec((ts, k), lambda i: (i, 0)), pl.BlockSpec((k, D), lambda i: (0, 0)), tile,
                           pl.BlockSpec((8, D), lambda i: (0, 0)), row, row],
                 out_specs=[tile, tile], out_shape=[out, out],
                 compiler_params=_params("arbitrary"))(a, w, x, mod, lg, lb)


def post_bwd(x, br, mod, grow, lg, lb, dy, name):
    s = x.shape[0]
    ts = min(TS, s)

    def body(x_ref, b_ref, m_ref, lg_ref, lb_ref, dy_ref, dx_ref, db_ref, dg_ref, dlg_ref, dlb_ref):
        @pl.when(pl.program_id(0) == 0)
        def _():
            dg_ref[...] = jnp.zeros_like(dg_ref)
            dlg_ref[...] = jnp.zeros_like(dlg_ref)
            dlb_ref[...] = jnp.zeros_like(dlb_ref)

        _, vjp = jax.vjp(_post_fn, x_ref[...], b_ref[...], m_ref[grow:grow + 1, :], lg_ref[...], lb_ref[...])
        dx, db, dg, dlg, dlb = vjp(dy_ref[...])
        dx_ref[...] = dx
        db_ref[...] = db.astype(BF16)
        dg_ref[...] += dg
        dlg_ref[...] += dlg
        dlb_ref[...] += dlb

    tile = pl.BlockSpec((ts, D), lambda i: (i, 0))
    row = pl.BlockSpec((1, D), lambda i: (0, 0))
    rs = jax.ShapeDtypeStruct((1, D), F32)
    return _call(body, name=name, grid=(s // ts,),
                 in_specs=[tile, tile, pl.BlockSpec((8, D), lambda i: (0, 0)), row, row, tile],
                 out_specs=[tile, tile, row, row, row],
                 out_shape=[jax.ShapeDtypeStruct((s, D), F32), jax.ShapeDtypeStruct((s, D), BF16), rs, rs, rs],
                 compiler_params=_params("arbitrary"))(x, br, mod, lg, lb, dy)


def loss_kernel(y, target):
    s = y.shape[0]
    ts = min(TS, s)

    def body(y_ref, t_ref, l_ref, dy_ref):
        @pl.when(pl.program_id(0) == 0)
        def _():
            l_ref[...] = jnp.zeros_like(l_ref)

        err = y_ref[...] - t_ref[...]
        dy_ref[...] = err * (1.0 / D)
        per_tok = jnp.mean(err * err, axis=-1, keepdims=True)
        l_ref[...] += 0.5 * jnp.sum(per_tok)

    tile = pl.BlockSpec((ts, D), lambda i: (i, 0))
    return _call(body, name="loss", grid=(s // ts,), in_specs=[tile, tile],
                 out_specs=[pl.BlockSpec((8, 128), lambda i: (0, 0)), tile],
                 out_shape=[jax.ShapeDtypeStruct((8, 128), F32), jax.ShapeDtypeStruct((s, D), F32)],
                 compiler_params=_params("arbitrary"))(y, target)


def _gm_pair(u, v, lg, lb, w0, w1, bs0, bs1):
    t = u.shape[0]
    low = _lanes((t, 2 * HD)) < HD

    def head_mean(x):
        lo = jnp.sum(jnp.where(low, x, 0.0), axis=-1, keepdims=True)
        hi = jnp.sum(jnp.where(low, 0.0, x), axis=-1, keepdims=True)
        return jnp.where(low, lo, hi) * (1.0 / HD)

    xc = v - head_mean(v)
    vn = xc * lax.rsqrt(head_mean(xc * xc) + LN_EPS) * lg + lb
    v0 = jnp.where(low, vn, 0.0)
    v1 = jnp.where(low, 0.0, vn)
    causal = _rows((GM_C, GM_C)) >= _lanes((GM_C, GM_C))
    wm0 = jnp.where(causal, w0, 0.0)
    wm1 = jnp.where(causal, w1, 0.0)
    bias = jnp.where(_lanes((GM_C, 2 * HD)) < HD, bs0, bs1)
    chunks = []
    for n in range(t // GM_C):
        rs = slice(n * GM_C, (n + 1) * GM_C)
        chunks.append(u[rs] * (_bdot(wm0, v0[rs]) + _bdot(wm1, v1[rs]) + bias))
    return jnp.concatenate(chunks, axis=0)


def gm_fwd(p, lg, lb, ws, bst):
    s = p.shape[0]
    ts = min(TS_C, s)

    def body(u_ref, v_ref, lg_ref, lb_ref, ws_ref, bs_ref, o_ref):
        for j in range(GM_H // 2):
            sl = slice(j * 2 * HD, (j + 1) * 2 * HD)
            o_ref[:, sl] = _gm_pair(u_ref[:, sl], v_ref[:, sl], lg_ref[:, sl], lb_ref[:, sl], ws_ref[2 * j],
                                    ws_ref[2 * j + 1], bs_ref[:, 2 * j:2 * j + 1],
                                    bs_ref[:, 2 * j + 1:2 * j + 2]).astype(BF16)

    full = lambda shape: pl.BlockSpec(shape, lambda i: (0,) * len(shape))
    return _call(body, name="gm_fwd", grid=(s // ts,),
                 in_specs=[pl.BlockSpec((ts, GM_W), lambda i: (i, 0)), pl.BlockSpec((ts, GM_W), lambda i: (i, 1)),
                           full((1, GM_W)), full((1, GM_W)), full((GM_H, GM_C, GM_C)), full((GM_C, GM_H))],
                 out_specs=pl.BlockSpec((ts, GM_W), lambda i: (i, 0)),
                 out_shape=jax.ShapeDtypeStruct((s, GM_W), BF16),
                 compiler_params=_params("arbitrary"))(p, p, lg, lb, ws, bst)


def gm_bwd(p, lg, lb, ws, bst, dmix):
    s = p.shape[0]
    ts = min(TS_C, s)

    def body(u_ref, v_ref, lg_ref, lb_ref, ws_ref, bs_ref, dy_ref, duv_ref, dlg_ref, dlb_ref, dws_ref, dbs_ref):
        @pl.when(pl.program_id(0) == 0)
        def _():
            dlg_ref[...] = jnp.zeros_like(dlg_ref)
            dlb_ref[...] = jnp.zeros_like(dlb_ref)
            dws_ref[...] = jnp.zeros_like(dws_ref)
            dbs_ref[...] = jnp.zeros_like(dbs_ref)

        for j in range(GM_H // 2):
            sl = slice(j * 2 * HD, (j + 1) * 2 * HD)
            _, vjp = jax.vjp(_gm_pair, u_ref[:, sl], v_ref[:, sl], lg_ref[:, sl], lb_ref[:, sl], ws_ref[2 * j],
                             ws_ref[2 * j + 1], bs_ref[:, 2 * j:2 * j + 1], bs_ref[:, 2 * j + 1:2 * j + 2])
            du, dv, dlg, dlb, dw0, dw1, dbs0, dbs1 = vjp(dy_ref[:, sl])
            duv_ref[:, sl] = du.astype(BF16)
            duv_ref[:, GM_W + j * 2 * HD:GM_W + (j + 1) * 2 * HD] = dv.astype(BF16)
            dlg_ref[:, sl] += dlg
            dlb_ref[:, sl] += dlb
            dws_ref[2 * j] += dw0
            dws_ref[2 * j + 1] += dw1
            dbs_ref[:, 2 * j:2 * j + 1] += dbs0
            dbs_ref[:, 2 * j + 1:2 * j + 2] += dbs1

    full = lambda shape: pl.BlockSpec(shape, lambda i: (0,) * len(shape))
    return _call(body, name="gm_bwd", grid=(s // ts,),
                 in_specs=[pl.BlockSpec((ts, GM_W), lambda i: (i, 0)), pl.BlockSpec((ts, GM_W), lambda i: (i, 1)),
                           full((1, GM_W)), full((1, GM_W)), full((GM_H, GM_C, GM_C)), full((GM_C, GM_H)),
                           pl.BlockSpec((ts, GM_W), lambda i: (i, 0))],
                 out_specs=[pl.BlockSpec((ts, 2 * GM_W), lambda i: (i, 0)), full((1, GM_W)), full((1, GM_W)),
                            full((GM_H, GM_C, GM_C)), full((GM_C, GM_H))],
                 out_shape=[jax.ShapeDtypeStruct((s, 2 * GM_W), BF16), jax.ShapeDtypeStruct((1, GM_W), F32),
                            jax.ShapeDtypeStruct((1, GM_W), F32), jax.ShapeDtypeStruct((GM_H, GM_C, GM_C), F32),
                            jax.ShapeDtypeStruct((GM_C, GM_H), F32)],
                 compiler_params=_params("arbitrary"))(p, p, lg, lb, ws, bst, dmix)


def _s5_prep_fn(lr, li, ldt, bt):
    dt = jnp.exp(ldt)
    er = jnp.exp(lr * dt)
    ar = er * jnp.cos(li * dt)
    ai = er * jnp.sin(li * dt)
    den = lr * lr + li * li
    nr = ar - 1.0
    cr = (nr * lr + ai * li) / den
    ci = (ai * lr - nr * li) / den
    br, bi = bt[:, :S5_N], bt[:, S5_N:]
    return ar, ai, jnp.concatenate([cr * br - ci * bi, cr * bi + ci * br], axis=1)


def s5_prep_fwd(lr, li, ldt, bt):
    def body(lr_ref, li_ref, ldt_ref, bt_ref, a_ref, bb_ref):
        ar, ai, bb = _s5_prep_fn(lr_ref[...], li_ref[...], ldt_ref[...], bt_ref[...])
        a_ref[...] = jnp.concatenate([ar, ai, jnp.zeros((6, S5_N), F32)], axis=0)
        bb_ref[...] = bb

    return _call(body, name="s5_prep_fwd",
                 out_shape=[jax.ShapeDtypeStruct((8, S5_N), F32), jax.ShapeDtypeStruct((S5_H, 2 * S5_N), F32)])(lr, li, ldt, bt)


def s5_prep_bwd(lr, li, ldt, bt, da, dbb):
    def body(lr_ref, li_ref, ldt_ref, bt_ref, da_ref, dbb_ref, dlr_ref, dli_ref, dldt_ref, dbt_ref):
        _, vjp = jax.vjp(_s5_prep_fn, lr_ref[...], li_ref[...], ldt_ref[...], bt_ref[...])
        dlr, dli, dldt, dbt = vjp((da_ref[0:1, :], da_ref[1:2, :], dbb_ref[...]))
        dlr_ref[...] = dlr
        dli_ref[...] = dli
        dbt_ref[...] = dbt
        group = (_rows((S5_N, 128)) // S5_P == _lanes((S5_N, 128))).astype(F32)
        dldt_ref[...] = jnp.dot(jnp.broadcast_to(dldt, (8, S5_N)), group, precision=lax.Precision.HIGHEST,
                                preferred_element_type=F32)[0:1, :]

    r = jax.ShapeDtypeStruct((1, S5_N), F32)
    return _call(body, name="s5_prep_bwd",
                 out_shape=[r, r, jax.ShapeDtypeStruct((1, 128), F32),
                            jax.ShapeDtypeStruct((S5_H, 2 * S5_N), F32)])(lr, li, ldt, bt, da, dbb)


def _s5_out_fn(x, u, cbd, drow, wg, bg):
    y = _bdot_nt(x[:, :S5_N], cbd[:, :S5_N]) - _bdot_nt(x[:, S5_N:], cbd[:, S5_N:]) + drow * u
    y = jax.nn.gelu(y)
    gate = _bdot_nt(y, wg) + bg
    return y * jax.nn.sigmoid(gate)


def _scan_chunk(buf, ar, ai, cr, ci, reverse):
    t = buf.shape[0]

    def local(xr, xi, rows):
        within = _rows(xr.shape) % 8
        pr, pi = ar, ai
        for d in (1, 2, 4):
            keep = within < 8 - d if reverse else within >= d
            shift = rows - d if reverse else d
            sr = jnp.where(keep, pltpu.roll(xr, shift, 0), 0.0)
            si = jnp.where(keep, pltpu.roll(xi, shift, 0), 0.0)
            xr, xi = xr + pr * sr - pi * si, xi + pr * si + pi * sr
            pr, pi = pr * pr - pi * pi, 2.0 * pr * pi
        return xr, xi

    xr, xi = local(buf[:, :S5_N], buf[:, S5_N:], t)
    buf[:, :S5_N] = xr
    buf[:, S5_N:] = xi
    edge = _rows((8, S5_N)) == (7 if reverse else 0)
    pr8, pi8 = local(jnp.where(edge, ar, 0.0), jnp.where(edge, ai, 0.0), 8)

    def group(j, c):
        g = t // 8 - 1 - j if reverse else j
        rows = pl.ds(pl.multiple_of(g * 8, 8), 8)
        gr = buf[rows, :S5_N] + pr8 * c[0] - pi8 * c[1]
        gi = buf[rows, S5_N:] + pr8 * c[1] + pi8 * c[0]
        buf[rows, :S5_N] = gr
        buf[rows, S5_N:] = gi
        return (gr[0:1, :], gi[0:1, :]) if reverse else (gr[7:8, :], gi[7:8, :])

    return lax.fori_loop(0, t // 8, group, (cr, ci), unroll=4)


def s5_fwd(p, arow, bbd, cbd, drow, wg, bg):
    s = p.shape[0]
    t = min(T_S5, s)

    def body(u_ref, a_ref, bbd_ref, cbd_ref, d_ref, wg_ref, bg_ref, y_ref, st_ref, carry):
        @pl.when(pl.program_id(0) == 0)
        def _():
            carry[...] = jnp.zeros_like(carry)

        u = u_ref[...]
        st_ref[...] = _nn(u, bbd_ref[...])
        cr, ci = _scan_chunk(st_ref, a_ref[0:1, :], a_ref[1:2, :], carry[0:1, :S5_N], carry[0:1, S5_N:], False)
        carry[0:1, :S5_N] = cr
        carry[0:1, S5_N:] = ci
        y_ref[...] = _s5_out_fn(st_ref[...], u, cbd_ref[...], d_ref[...], wg_ref[...], bg_ref[...]).astype(BF16)

    full = lambda shape: pl.BlockSpec(shape, lambda i: (0,) * len(shape))
    return _call(body, name="s5_fwd", grid=(s // t,),
                 in_specs=[pl.BlockSpec((t, S5_W), lambda i: (i, 2)), full((8, S5_N)), full((S5_W, 2 * S5_N)),
                           full((S5_W, 2 * S5_N)), full((1, S5_W)), full((S5_W, S5_W)), full((1, S5_W))],
                 out_specs=[pl.BlockSpec((t, S5_W), lambda i: (i, 0)), pl.BlockSpec((t, 2 * S5_N), lambda i: (i, 0))],
                 out_shape=[jax.ShapeDtypeStruct((s, S5_W), BF16), jax.ShapeDtypeStruct((s, 2 * S5_N), F32)],
                 scratch_shapes=[pltpu.VMEM((8, 2 * S5_N), F32)],
                 compiler_params=_params("arbitrary"))(p, arow, bbd, cbd, drow, wg, bg)


def s5_bwd(p, st, arow, bbd, cbd, drow, wg, bg, dmix):
    s = p.shape[0]
    t = min(T_S5, s)
    nc = s // t

    def body(u_ref, st_ref, prev_ref, a_ref, bbd_ref, cbd_ref, d_ref, wg_ref, bg_ref, dy_ref,
             du_ref, da_ref, dbbd_ref, dcbd_ref, dd_ref, dwg_ref, dbg_ref, carry, gbuf):
        i = pl.program_id(0)

        @pl.when(i == 0)
        def _():
            carry[...] = jnp.zeros_like(carry)
            for r in (da_ref, dbbd_ref, dcbd_ref, dd_ref, dwg_ref, dbg_ref):
                r[...] = jnp.zeros_like(r)

        u = u_ref[...]
        x = st_ref[...]
        _, vjp = jax.vjp(_s5_out_fn, x, u, cbd_ref[...], d_ref[...], wg_ref[...], bg_ref[...])
        dx, du1, dcbd, dd, dwg, dbg = vjp(dy_ref[...])
        gbuf[...] = dx
        cr, ci = _scan_chunk(gbuf, a_ref[0:1, :], -a_ref[1:2, :], carry[0:1, :S5_N], carry[0:1, S5_N:], True)
        carry[0:1, :S5_N] = cr
        carry[0:1, S5_N:] = ci
        gr, gi = gbuf[:, :S5_N], gbuf[:, S5_N:]
        rid = _rows((t, S5_N))
        has_prev = (i < nc - 1).astype(F32)
        top_r = prev_ref[7:8, :S5_N] * has_prev
        top_i = prev_ref[7:8, S5_N:] * has_prev
        xpr = jnp.where(rid == 0, top_r, pltpu.roll(x[:, :S5_N], 1, 0))
        xpi = jnp.where(rid == 0, top_i, pltpu.roll(x[:, S5_N:], 1, 0))
        da_ref[0:1, :] += jnp.sum(xpr * gr + xpi * gi, axis=0, keepdims=True)
        da_ref[1:2, :] += jnp.sum(xpr * gi - xpi * gr, axis=0, keepdims=True)
        g = jnp.concatenate([gr, gi], axis=1)
        dbbd_ref[...] += _tn(u, g)
        du_ref[...] = (_nt(g, bbd_ref[...]) + du1).astype(BF16)
        dcbd_ref[...] += dcbd
        dd_ref[...] += dd
        dwg_ref[...] += dwg
        dbg_ref[...] += dbg

    full = lambda shape: pl.BlockSpec(shape, lambda i: (0,) * len(shape))
    rev = lambda col: (lambda i: (nc - 1 - i, col))
    prev_map = lambda i: (jnp.maximum((nc - 1 - i) * (t // 8) - 1, 0), 0)
    return _call(body, name="s5_bwd", grid=(nc,),
                 in_specs=[pl.BlockSpec((t, S5_W), rev(2)), pl.BlockSpec((t, 2 * S5_N), rev(0)),
                           pl.BlockSpec((8, 2 * S5_N), prev_map), full((8, S5_N)), full((S5_W, 2 * S5_N)),
                           full((S5_W, 2 * S5_N)), full((1, S5_W)), full((S5_W, S5_W)), full((1, S5_W)),
                           pl.BlockSpec((t, S5_W), rev(1))],
                 out_specs=[pl.BlockSpec((t, S5_W), rev(0)), full((8, S5_N)), full((S5_W, 2 * S5_N)),
                            full((S5_W, 2 * S5_N)), full((1, S5_W)), full((S5_W, S5_W)), full((1, S5_W))],
                 out_shape=[jax.ShapeDtypeStruct((s, S5_W), BF16), jax.ShapeDtypeStruct((8, S5_N), F32),
                            jax.ShapeDtypeStruct((S5_W, 2 * S5_N), F32), jax.ShapeDtypeStruct((S5_W, 2 * S5_N), F32),
                            jax.ShapeDtypeStruct((1, S5_W), F32), jax.ShapeDtypeStruct((S5_W, S5_W), F32),
                            jax.ShapeDtypeStruct((1, S5_W), F32)],
                 scratch_shapes=[pltpu.VMEM((8, 2 * S5_N), F32), pltpu.VMEM((t, 2 * S5_N), F32)],
                 compiler_params=_params("arbitrary"))(p, st, st, arow, bbd, cbd, drow, wg, bg, dmix)


def _cum_steps(s):
    return int(math.ceil(math.log2(s)))


V_BLK = (2 * GM_W + S5_W + 2 * FX_W) // 128


AUG = 2 * HD
BIAS_COL = HD
FQ_COL = HD + 3
PAIR_W = 256


def _split3(f):
    hi = f.astype(BF16).astype(F32)
    r = f - hi
    mid = r.astype(BF16).astype(F32)
    lo = (r - mid).astype(BF16).astype(F32)
    return hi, mid, lo


def fox_prep(p, bf):
    s = p.shape[0]
    ts = min(TS, s)
    scale = HD ** -0.5

    def body(q0_ref, q1_ref, k0_ref, k1_ref, v0_ref, v1_ref, f_ref, bf_ref,
             qa_ref, ka_ref, qat_ref, kat_ref, vt_ref, carry):
        @pl.when(pl.program_id(0) == 0)
        def _():
            carry[...] = jnp.zeros_like(carry)

        lane = _lanes((ts, 128))
        lf = jax.nn.log_sigmoid(f_ref[...] + bf_ref[...])
        acc = jnp.where(lane < FX_H, lf, 0.0)
        rid = _rows((ts, 128))
        for k in range(_cum_steps(ts)):
            d = 1 << k
            acc = acc + jnp.where(rid >= d, pltpu.roll(acc, d, 0), 0.0)
        acc = acc + carry[0:1, :]
        carry[0:1, :] = acc[ts - 1:ts, :]

        low = lane < HD
        for h in range(FX_H):
            blk, pos = divmod(h, 4)
            pair = slice((pos // 2) * 128, (pos // 2) * 128 + 128)
            hi, mid, lo = _split3(acc[:, h:h + 1])
            one = jnp.ones((ts, 1), F32)

            def augment(ref, cols):
                x = ref[:, pair]
                if pos % 2:
                    x = pltpu.roll(x, HD, 1)
                out = jnp.where(low, x, 0.0)
                for j, cval in enumerate(cols):
                    out = jnp.where(lane == HD + j, cval, out)
                return out

            qa = augment((q0_ref, q1_ref)[blk], (one, one, one, hi, mid, lo))
            qa = jnp.where(low, qa * scale, qa)
            ka = augment((k0_ref, k1_ref)[blk], (-hi, -mid, -lo, one, one, one))
            cs = slice(h * AUG, (h + 1) * AUG)
            qa_ref[:, cs] = qa.astype(BF16)
            ka_ref[:, cs] = ka.astype(BF16)
            qat_ref[cs, :] = jnp.transpose(qa).astype(BF16)
            kat_ref[cs, :] = jnp.transpose(ka).astype(BF16)
        for j in range(FX_H // 2):
            vref = (v0_ref, v1_ref)[j // 2]
            vt_ref[j * 128:(j + 1) * 128, :] = jnp.transpose(vref[:, (j % 2) * 128:(j % 2) * 128 + 128]).astype(BF16)

    q_blk = (2 * GM_W + S5_W) // PAIR_W
    col = lambda b: pl.BlockSpec((ts, PAIR_W), lambda i: (i, b))
    wide = FX_H * AUG
    return _call(body, name="fox_prep", grid=(s // ts,),
                 in_specs=[col(q_blk), col(q_blk + 1), col(q_blk + 2), col(q_blk + 3), col(q_blk + 4), col(q_blk + 5),
                           pl.BlockSpec((ts, 128), lambda i: (i, FF_COL // 128)), pl.BlockSpec((1, 128), lambda i: (0, 0))],
                 out_specs=[pl.BlockSpec((ts, wide), lambda i: (i, 0)), pl.BlockSpec((ts, wide), lambda i: (i, 0)),
                            pl.BlockSpec((wide, ts), lambda i: (0, i)), pl.BlockSpec((wide, ts), lambda i: (0, i)),
                            pl.BlockSpec((FX_W, ts), lambda i: (0, i))],
                 out_shape=[jax.ShapeDtypeStruct((s, wide), BF16), jax.ShapeDtypeStruct((s, wide), BF16),
                            jax.ShapeDtypeStruct((wide, s), BF16), jax.ShapeDtypeStruct((wide, s), BF16),
                            jax.ShapeDtypeStruct((FX_W, s), BF16)],
                 scratch_shapes=[pltpu.VMEM((8, 128), F32)],
                 compiler_params=_params("arbitrary"))(p, p, p, p, p, p, p, bf)


def fox_prep_grad(p, bf, dfq, dfk):
    s = p.shape[0]
    ts = min(TS, s)
    ns = s // ts

    def body(f_ref, bf_ref, dfq_ref, dfk_ref, df_ref, dbf_ref, carry):
        @pl.when(pl.program_id(0) == 0)
        def _():
            carry[...] = jnp.zeros_like(carry)
            dbf_ref[...] = jnp.zeros_like(dbf_ref)

        lane = _lanes((ts, 128))
        acc = jnp.zeros((ts, 128), F32)
        for h in range(FX_H):
            c = (h // 2) * 128 + h % 2
            acc = jnp.where(lane == h, dfq_ref[:, c:c + 1] + dfk_ref[:, c:c + 1], acc)
        rid = _rows((ts, 128))
        for k in range(_cum_steps(ts)):
            d = 1 << k
            acc = acc + jnp.where(rid < ts - d, pltpu.roll(acc, ts - d, 0), 0.0)
        acc = acc + carry[0:1, :]
        carry[0:1, :] = acc[0:1, :]
        z = f_ref[...] + bf_ref[...]
        df = jnp.where(lane < FX_H, acc * jax.nn.sigmoid(-z), 0.0)
        df_ref[...] = df.astype(BF16)
        dbf_ref[...] += jnp.sum(df, axis=0, keepdims=True)

    rev = lambda i: (ns - 1 - i, 0)
    return _call(body, name="fox_prep_grad", grid=(ns,),
                 in_specs=[pl.BlockSpec((ts, 128), lambda i: (ns - 1 - i, FF_COL // 128)),
                           pl.BlockSpec((1, 128), lambda i: (0, 0)),
                           pl.BlockSpec((ts, FX_W), rev), pl.BlockSpec((ts, FX_W), rev)],
                 out_specs=[pl.BlockSpec((ts, 128), rev), pl.BlockSpec((1, 128), lambda i: (0, 0))],
                 out_shape=[jax.ShapeDtypeStruct((s, 128), BF16), jax.ShapeDtypeStruct((1, 128), F32)],
                 scratch_shapes=[pltpu.VMEM((8, 128), F32)],
                 compiler_params=_params("arbitrary"))(p, bf, dfq, dfk)


def attn(qat, ka, vt):
    s = ka.shape[0]
    tq = min(TQ_FWD, s)
    nq = s // tq

    nh = ATTN_HEADS

    def body(qat_ref, ka_ref, vt_ref, o_ref, lse_ref):
        qi = pl.program_id(1)
        half = tq // 2
        lse_ref[...] = jnp.zeros_like(lse_ref)

        def update(m, l, acc, st, v):
            m_new = jnp.maximum(m, jnp.max(st, axis=0, keepdims=True))
            alpha = jnp.exp(m - m_new)
            pt = jnp.exp(st - m_new)
            return (m_new, alpha * l + jnp.sum(pt, axis=0, keepdims=True),
                    alpha * acc + jnp.dot(v, pt.astype(BF16), preferred_element_type=F32))

        def step(kj, carry):
            off = pl.multiple_of(kj * tq, tq)
            out = []
            for hh in range(nh):
                cs = slice(hh * AUG, (hh + 1) * AUG)
                st = jnp.dot(ka_ref[pl.ds(off, tq), cs], qat_ref[cs, :], preferred_element_type=F32)
                out.append(update(*carry[hh], st, vt_ref[hh * HD:(hh + 1) * HD, pl.ds(off, tq)]))
            return tuple(out)

        def diagonal(carry):
            off = pl.multiple_of(qi * tq, tq)
            off2 = pl.multiple_of(off + half, half)
            lower = _rows((half, tq)) <= _lanes((half, tq))
            out = []
            for hh in range(nh):
                cs = slice(hh * AUG, (hh + 1) * AUG)
                hs = slice(hh * HD, (hh + 1) * HD)
                st = jnp.dot(ka_ref[pl.ds(off, half), cs], qat_ref[cs, :], preferred_element_type=F32)
                m, l, acc = update(*carry[hh], jnp.where(lower, st, NEG_INF), vt_ref[hs, pl.ds(off, half)])
                st = jnp.dot(ka_ref[pl.ds(off2, half), cs], qat_ref[cs, half:], preferred_element_type=F32)
                mr, lr, ar = update(m[:, half:], l[:, half:], acc[:, half:], jnp.where(lower[:, :half], st, NEG_INF),
                                    vt_ref[hs, pl.ds(off2, half)])
                out.append((jnp.concatenate([m[:, :half], mr], axis=1), jnp.concatenate([l[:, :half], lr], axis=1),
                            jnp.concatenate([acc[:, :half], ar], axis=1)))
            return tuple(out)

        init = tuple((jnp.full((1, tq), NEG_INF, F32), jnp.zeros((1, tq), F32), jnp.zeros((HD, tq), F32))
                     for _ in range(nh))
        carry = diagonal(lax.fori_loop(0, qi, step, init))
        for hh in range(nh):
            m, l, _ = carry[hh]
            lse_ref[hh // 2, hh % 2:hh % 2 + 1, :] = m + jnp.log(l)
        for j in range(nh // 2):
            pair = jnp.concatenate([carry[2 * j][2] / carry[2 * j][1], carry[2 * j + 1][2] / carry[2 * j + 1][1]], axis=0)
            o_ref[:, j * 128:(j + 1) * 128] = jnp.transpose(pair).astype(BF16)

    return _call(body, name="attn", grid=(FX_H // nh, nq),
                 in_specs=[pl.BlockSpec((nh * AUG, tq), lambda h, i: (h, i)),
                           pl.BlockSpec((s, nh * AUG), lambda h, i: (0, h)),
                           pl.BlockSpec((nh * HD, s), lambda h, i: (h, 0))],
                 out_specs=[pl.BlockSpec((tq, nh * HD), lambda h, i: (i, h)),
                            pl.BlockSpec((nh // 2, 8, tq), lambda h, i: (h, 0, i))],
                 out_shape=[jax.ShapeDtypeStruct((s, FX_W), BF16), jax.ShapeDtypeStruct((FX_H // 2, 8, s), F32)],
                 compiler_params=_params("arbitrary", "arbitrary"))(qat, ka, vt)


def attn_grad(qa, qat, ka, kat, p, o, lse, dmix):
    s = qa.shape[0]
    tq = min(TQ, s)
    nq = s // tq
    scale = HD ** -0.5

    def body(qa_ref, qat_ref, ka_ref, kat_ref, v_ref, o_ref, lse_ref, do_ref,
             dq_ref, dk_ref, dv_ref, dfq_ref, dfk_ref, dot_scr, delta, dqt):
        kj = pl.program_id(1)
        lane = _lanes((tq, 128))
        low = lane < HD
        causal = _rows((tq, tq)) <= _lanes((tq, tq))

        @pl.when(kj == 0)
        def _():
            dqt[...] = jnp.zeros_like(dqt)
            delta[...] = jnp.zeros_like(delta)

            def prep(c, _):
                rows = pl.ds(pl.multiple_of(c * tq, tq), tq)
                do = do_ref[rows, :]
                pt = jnp.transpose(do * o_ref[rows, :].astype(F32))
                delta[0:1, rows] = jnp.sum(pt[:HD], axis=0, keepdims=True)
                delta[1:2, rows] = jnp.sum(pt[HD:], axis=0, keepdims=True)
                dot_scr[:, rows] = jnp.transpose(do).astype(BF16)
                return 0

            lax.fori_loop(0, nq, prep, 0)

        v = v_ref[...]
        vms = [jnp.where(low, v, 0.0).astype(BF16), jnp.where(low, 0.0, v).astype(BF16)]

        def tile(qi, carry, masked):
            cols = pl.ds(pl.multiple_of(qi * tq, tq), tq)
            do = do_ref[cols, :].astype(BF16)
            out = []
            for hh in range(2):
                cs = slice(hh * AUG, (hh + 1) * AUG)
                dka, dv = carry[hh]
                st = jnp.dot(ka_ref[:, cs], qat_ref[cs, cols], preferred_element_type=F32)
                if masked:
                    st = jnp.where(causal, st, NEG_INF)
                pt = jnp.exp(st - lse_ref[0, hh:hh + 1, cols])
                dv = dv + jnp.dot(pt.astype(BF16), do, preferred_element_type=F32)
                dpt = jnp.dot(vms[hh], dot_scr[:, cols], preferred_element_type=F32)
                dsb = (pt * (dpt - delta[hh:hh + 1, cols])).astype(BF16)
                dka = dka + jnp.dot(dsb, qa_ref[cols, cs], preferred_element_type=F32)
                dqt[hh, :, cols] += jnp.dot(kat_ref[cs, :], dsb, preferred_element_type=F32)
                out.append((dka, dv))
            return tuple(out)

        init = tuple((jnp.zeros((tq, AUG), F32), jnp.zeros((tq, 128), F32)) for _ in range(2))
        carry = tile(kj, init, True)
        carry = lax.fori_loop(kj + 1, nq, lambda qi, c: tile(qi, c, False), carry)
        dks = [carry[0][0], carry[1][0]]
        dvs = [carry[0][1], carry[1][1]]
        dv_ref[...] = jnp.where(low, dvs[0], dvs[1]).astype(BF16)
        dk_ref[...] = jnp.where(low, dks[0], pltpu.roll(dks[1], HD, 1)).astype(BF16)
        dfk_ref[...] = jnp.where(lane == 0, -dks[0][:, BIAS_COL:BIAS_COL + 1],
                                 jnp.where(lane == 1, -dks[1][:, BIAS_COL:BIAS_COL + 1], 0.0))

        @pl.when(kj == nq - 1)
        def _():
            def finish(c, _):
                rows = pl.ds(pl.multiple_of(c * tq, tq), tq)
                t0 = jnp.transpose(dqt[0, :, rows])
                t1 = jnp.transpose(dqt[1, :, rows])
                dq_ref[rows, :] = (jnp.where(low, t0, pltpu.roll(t1, HD, 1)) * scale).astype(BF16)
                dfq_ref[rows, :] = jnp.where(lane == 0, t0[:, FQ_COL:FQ_COL + 1],
                                             jnp.where(lane == 1, t1[:, FQ_COL:FQ_COL + 1], 0.0))
                return 0

            lax.fori_loop(0, nq, finish, 0)

    seq128 = lambda blk: pl.BlockSpec((s, 128), lambda h, j: (0, blk + h))
    tile128 = pl.BlockSpec((tq, 128), lambda h, j: (j, h))
    out_b = jax.ShapeDtypeStruct((s, FX_W), BF16)
    out_f = jax.ShapeDtypeStruct((s, FX_W), F32)
    return _call(body, name="attn_grad", grid=(FX_H // 2, nq),
                 in_specs=[pl.BlockSpec((s, 2 * AUG), lambda h, j: (0, h)), pl.BlockSpec((2 * AUG, s), lambda h, j: (h, 0)),
                           pl.BlockSpec((tq, 2 * AUG), lambda h, j: (j, h)), pl.BlockSpec((2 * AUG, tq), lambda h, j: (h, j)),
                           pl.BlockSpec((tq, 128), lambda h, j: (j, V_BLK + h)), seq128(0),
                           pl.BlockSpec((1, 8, s), lambda h, j: (h, 0, 0)), seq128(4)],
                 out_specs=[seq128(0), tile128, tile128, seq128(0), tile128],
                 out_shape=[out_b, out_b, out_b, out_f, out_f],
                 scratch_shapes=[pltpu.VMEM((128, s), BF16), pltpu.VMEM((8, s), F32), pltpu.VMEM((2, AUG, s), F32)],
                 compiler_params=_params("arbitrary", "arbitrary"))(qa, qat, ka, kat, p, o, lse, dmix)


def _shift_down(a, prev8, k):
    r = pltpu.roll(a, k, 0)
    top = jnp.where(_rows(prev8.shape) < k, pltpu.roll(prev8, k, 0), r[0:8])
    return jnp.concatenate([top, r[8:]], axis=0)


def _shift_up(a, next8, k):
    t = a.shape[0]
    r = pltpu.roll(a, t - k, 0)
    bot = jnp.where(_rows(next8.shape) >= 8 - k, pltpu.roll(next8, 8 - k, 0), r[t - 8:t])
    return jnp.concatenate([r[:t - 8], bot], axis=0)


def _conv(a, prev8, cw, cb):
    return cb + cw[0:1, :] * _shift_down(a, prev8, 2) + cw[1:2, :] * _shift_down(a, prev8, 1) + cw[2:3, :] * a


GELU_K0 = math.sqrt(2.0 / math.pi)
GELU_K1 = GELU_K0 * 0.044715


def _gelu_parts(c):
    c2 = c * c
    return c2, 0.5 + 0.5 * jnp.tanh(c * (GELU_K0 + GELU_K1 * c2))


def conv_fwd(up, cw, cb):
    s = up.shape[0]
    ts = min(TS_C, s)

    def body(a_ref, g_ref, cw_ref, cb_ref, o_ref, c_ref, halo):
        @pl.when(pl.program_id(0) == 0)
        def _():
            halo[...] = jnp.zeros_like(halo)

        a = a_ref[...]
        c = _conv(a, halo[...], cw_ref[...], cb_ref[...])
        _, h = _gelu_parts(c)
        c_ref[...] = c
        o_ref[...] = (c * h * g_ref[...]).astype(BF16)
        halo[...] = a[ts - 8:ts, :]

    tile = pl.BlockSpec((ts, DFF), lambda i: (i, 0))
    return _call(body, name="conv_fwd", grid=(s // ts,),
                 in_specs=[tile, pl.BlockSpec((ts, DFF), lambda i: (i, 1)),
                           pl.BlockSpec((3, DFF), lambda i: (0, 0)), pl.BlockSpec((1, DFF), lambda i: (0, 0))],
                 out_specs=[tile, tile],
                 out_shape=[jax.ShapeDtypeStruct((s, DFF), BF16), jax.ShapeDtypeStruct((s, DFF), F32)],
                 scratch_shapes=[pltpu.VMEM((8, DFF), F32)],
                 compiler_params=_params("arbitrary"))(up, up, cw, cb)


def conv_bwd(up, c, cw, dact):
    s = up.shape[0]
    ts = min(TS_C, s)
    ns = s // ts

    def body(a_ref, g_ref, c_ref, cw_ref, dact_ref, dup_ref, dcw_ref, dcb_ref, halo):
        @pl.when(pl.program_id(0) == 0)
        def _():
            halo[...] = jnp.zeros_like(halo)
            dcw_ref[...] = jnp.zeros_like(dcw_ref)
            dcb_ref[...] = jnp.zeros_like(dcb_ref)

        a = a_ref[...]
        cw = cw_ref[...]
        cv = c_ref[...]
        dact = dact_ref[...]
        c2, h = _gelu_parts(cv)
        dup_ref[:, DFF:] = (dact * (cv * h)).astype(BF16)
        dgel = h + cv * (2.0 * h * (1.0 - h)) * (GELU_K0 + 3.0 * GELU_K1 * c2)
        dc = dact * g_ref[...] * dgel
        up1 = _shift_up(dc, halo[...], 1)
        up2 = _shift_up(dc, halo[...], 2)
        dup_ref[:, :DFF] = (cw[2:3, :] * dc + cw[1:2, :] * up1 + cw[0:1, :] * up2).astype(BF16)
        dcw_ref[0:1, :] += jnp.sum(a * up2, axis=0, keepdims=True)
        dcw_ref[1:2, :] += jnp.sum(a * up1, axis=0, keepdims=True)
        dcw_ref[2:3, :] += jnp.sum(a * dc, axis=0, keepdims=True)
        dcb_ref[...] += jnp.sum(dc, axis=0, keepdims=True)
        halo[...] = dc[0:8, :]

    rev = lambda col: (lambda i: (ns - 1 - i, col))
    return _call(body, name="conv_bwd", grid=(ns,),
                 in_specs=[pl.BlockSpec((ts, DFF), rev(0)), pl.BlockSpec((ts, DFF), rev(1)),
                           pl.BlockSpec((ts, DFF), rev(0)), pl.BlockSpec((3, DFF), lambda i: (0, 0)),
                           pl.BlockSpec((ts, DFF), rev(0))],
                 out_specs=[pl.BlockSpec((ts, 2 * DFF), rev(0)), pl.BlockSpec((3, DFF), lambda i: (0, 0)),
                            pl.BlockSpec((1, DFF), lambda i: (0, 0))],
                 out_shape=[jax.ShapeDtypeStruct((s, 2 * DFF), BF16), jax.ShapeDtypeStruct((3, DFF), F32),
                            jax.ShapeDtypeStruct((1, DFF), F32)],
                 scratch_shapes=[pltpu.VMEM((8, DFF), F32)],
                 compiler_params=_params("arbitrary"))(up, up, c, cw, dact)


def _blockdiag_expand(m):
    m4 = m.reshape(S5_H, 2, S5_G, S5_P)
    eye = jnp.eye(S5_G, dtype=bool)[:, None, None, :, None]
    return jnp.where(eye, m4[None], 0.0).reshape(S5_W, 2 * S5_N)


def _blockdiag_extract(mbd):
    m5 = mbd.reshape(S5_G, S5_H, 2, S5_G, S5_P)
    diag = jnp.stack([m5[g, :, :, g, :] for g in range(S5_G)], axis=2)
    return diag.reshape(S5_H, 2 * S5_N)


def _c_expand(c_re, c_im):
    c4 = jnp.stack([c_re, c_im], axis=2)
    eye = jnp.eye(S5_G, dtype=bool)[:, None, None, :, None]
    return jnp.where(eye, c4[:, :, :, None, :], 0.0).reshape(S5_W, 2 * S5_N)


def _c_extract(cbd):
    m5 = cbd.reshape(S5_G, S5_H, 2, S5_G, S5_P)
    d = jnp.stack([m5[g, :, :, g, :] for g in range(S5_G)], axis=0)
    return d[:, :, 0, :], d[:, :, 1, :]


def _glu_expand(w):
    eye = jnp.eye(S5_G, dtype=bool)[:, None, :, None]
    return jnp.where(eye, w[:, :, None, :], 0.0).reshape(S5_W, S5_W)


def _glu_extract(wbd):
    m4 = wbd.reshape(S5_G, S5_H, S5_G, S5_H)
    return jnp.stack([m4[g, :, g, :] for g in range(S5_G)], axis=0)


SMALL = ("b_f", "gm_ln_g", "gm_ln_b", "gm_w_s", "gm_b_s", "s5_lam_re", "s5_lam_im", "s5_log_dt", "s5_b_re", "s5_b_im",
         "s5_c_re", "s5_c_im", "s5_d", "s5_w_glu", "s5_b_glu", "ln1_g", "ln1_b", "conv_b", "ln2_g", "ln2_b")


def _layer_operands(sp, l):
    f = {}
    f["bf"] = jnp.pad(sp["b_f"][l][None, :], ((0, 0), (0, 128 - FX_H)))
    f["gm_lg"] = sp["gm_ln_g"][l].reshape(1, GM_W)
    f["gm_lb"] = sp["gm_ln_b"][l].reshape(1, GM_W)
    f["gm_ws"] = sp["gm_w_s"][l]
    f["gm_bst"] = sp["gm_b_s"][l].T
    f["lr"] = sp["s5_lam_re"][l].reshape(1, S5_N)
    f["li"] = sp["s5_lam_im"][l].reshape(1, S5_N)
    f["ldt"] = jnp.repeat(sp["s5_log_dt"][l], S5_P).reshape(1, S5_N)
    bt = lambda b: jnp.transpose(b, (2, 0, 1)).reshape(S5_H, S5_N)
    f["bt"] = jnp.concatenate([bt(sp["s5_b_re"][l]), bt(sp["s5_b_im"][l])], axis=1)
    f["cbd"] = _c_expand(sp["s5_c_re"][l], sp["s5_c_im"][l])
    f["drow"] = sp["s5_d"][l].reshape(1, S5_W)
    f["wg"] = _glu_expand(sp["s5_w_glu"][l])
    f["bg"] = sp["s5_b_glu"][l].reshape(1, S5_W)
    for n in ("ln1_g", "ln1_b", "ln2_g", "ln2_b"):
        f[n] = sp[n][l][None, :]
    f["cb"] = sp["conv_b"][l][None, :]
    return f


def layer_fwd_mix(x, mod, w_in, f):
    p = mm_mod(x, w_in, NP, "in_proj", mod, (0, 1))
    ygm = gm_fwd(p, f["gm_lg"], f["gm_lb"], f["gm_ws"], f["gm_bst"])
    arow, bbt = s5_prep_fwd(f["lr"], f["li"], f["ldt"], f["bt"])
    bbd = _blockdiag_expand(bbt)
    ys5, st = s5_fwd(p, arow, bbd, f["cbd"], f["drow"], f["wg"], f["bg"])
    qa, ka, qat, kat, vt = fox_prep(p, f["bf"])
    yfx, lse = attn(qat, ka, vt)
    mixcat = jnp.concatenate([ygm, ys5, yfx], axis=1)
    return mixcat, dict(f=f, x=x, p=p, arow=arow, bbd=bbd, st=st, qa=qa, ka=ka, qat=qat, kat=kat, yfx=yfx, lse=lse,
                        mixcat=mixcat)


def layer_fwd_rest(x, mixcat, mod, w, f, saved):
    mix, x1 = proj_post(mixcat, w["w_out"], x, mod, 2, f["ln1_g"], f["ln1_b"], "out_proj")
    up = mm_mod(x1, w["w_up"], DFF, "up_proj", mod, (3, 4))
    act, conv = conv_fwd(up, w["conv_w"], f["cb"])
    ffn, x2 = proj_post(act, w["w_down"], x1, mod, 5, f["ln2_g"], f["ln2_b"], "down_proj")
    return x2, dict(saved, mix=mix, x1=x1, up=up, conv=conv, act=act, ffn=ffn)


def layer_fwd(x, mod, w, f):
    mixcat, saved = layer_fwd_mix(x, mod, w["w_in"], f)
    return layer_fwd_rest(x, mixcat, mod, w, f, saved)


def layer_bwd_ffn(dx, sv, mod, w):
    f = sv["f"]
    dx1, dffn, dg2, dlg2, dlb2 = post_bwd(sv["x1"], sv["ffn"], mod, 5, f["ln2_g"], f["ln2_b"], dx, "post2_bwd")
    g_down = mm_tn(sv["act"], dffn, D, "down_dw")
    dact = mm_nt(dffn, w["w_down"], "down_dx")
    dup, dcw, dcb = conv_bwd(sv["up"], sv["conv"], w["conv_w"], dact)
    g_up = mm_tn(dup, sv["x1"], D // 2, "up_dw", mod=mod, rows=(3, 4))
    dx1, dsh2, dsc2 = mm_nt_mod(dup, w["w_up"], sv["x1"], dx1, mod, (3, 4), "up_dx")
    return dx1, dict(w_up=g_up, w_down=g_down, conv_w=dcw), dict(dsh2=dsh2, dsc2=dsc2, dg2=dg2, conv_b=dcb[0],
                                                                 ln2_g=dlg2[0], ln2_b=dlb2[0])


def layer_bwd_mix(dx1, sv, mod, w, part):
    f = sv["f"]
    dx0, dmix, dg1, dlg1, dlb1 = post_bwd(sv["x"], sv["mix"], mod, 2, f["ln1_g"], f["ln1_b"], dx1, "post1_bwd")
    g_out = mm_tn(sv["mixcat"], dmix, D, "out_dw")
    dmc = mm_nt(dmix, w["w_out"], "out_dx")
    duv, dgm_lg, dgm_lb, dgm_ws, dgm_bst = gm_bwd(sv["p"], f["gm_lg"], f["gm_lb"], f["gm_ws"], f["gm_bst"], dmc)
    du5, da, dbbd, dcbd, dd5, dwg, dbg = s5_bwd(sv["p"], sv["st"], sv["arow"], sv["bbd"], f["cbd"], f["drow"],
                                                f["wg"], f["bg"], dmc)
    dlr, dli, dldt, dbt = s5_prep_bwd(f["lr"], f["li"], f["ldt"], f["bt"], da, _blockdiag_extract(dbbd))
    dq, dk, dv, dfq, dfk = attn_grad(sv["qa"], sv["qat"], sv["ka"], sv["kat"], sv["p"], sv["yfx"], sv["lse"], dmc)
    dff, dbf = fox_prep_grad(sv["p"], f["bf"], dfq, dfk)
    dp = jnp.concatenate([duv, du5, dq, dk, dv, dff], axis=1)
    g_in = mm_tn(dp, sv["x"], D, "in_dw", mod=mod, rows=(0, 1))
    dx, dsh1, dsc1 = mm_nt_mod(dp, w["w_in"], sv["x"], dx0, mod, (0, 1), "in_dx")

    dmod = jnp.concatenate([dsh1, dsc1, dg1, part["dsh2"], part["dsc2"], part["dg2"]], axis=0)
    dc_re, dc_im = _c_extract(dcbd)
    dbt4 = dbt.reshape(S5_H, 2, S5_G, S5_P)
    vals = dict(b_f=dbf[0, :FX_H], gm_ln_g=dgm_lg.reshape(GM_H, HD), gm_ln_b=dgm_lb.reshape(GM_H, HD),
                gm_w_s=dgm_ws, gm_b_s=dgm_bst.T, s5_lam_re=dlr.reshape(S5_G, S5_P),
                s5_lam_im=dli.reshape(S5_G, S5_P), s5_log_dt=dldt[0, :S5_G],
                s5_b_re=jnp.transpose(dbt4[:, 0], (1, 2, 0)), s5_b_im=jnp.transpose(dbt4[:, 1], (1, 2, 0)),
                s5_c_re=dc_re, s5_c_im=dc_im, s5_d=dd5.reshape(S5_G, S5_H), s5_w_glu=_glu_extract(dwg),
                s5_b_glu=dbg.reshape(S5_G, S5_H), ln1_g=dlg1[0], ln1_b=dlb1[0], conv_b=part["conv_b"],
                ln2_g=part["ln2_g"], ln2_b=part["ln2_b"])
    return dx, dict(w_in=g_in, w_out=g_out), vals, dmod


def layer_bwd(dx, sv, mod, w):
    dx1, g_ffn, part = layer_bwd_ffn(dx, sv, mod, w)
    dx, g_mix, vals, dmod = layer_bwd_mix(dx1, sv, mod, w, part)
    return dx, dict(g_ffn, **g_mix), vals, dmod


def local_step(x, target, mods, big, sp):
    saved = []
    for l in range(DEPTH):
        x, sv = layer_fwd(x, mods[l], big[l], _layer_operands(sp, l))
        saved.append(sv)
    loss_tile, dx = loss_kernel(x, target)
    gbig, vals, dmods = [None] * DEPTH, [None] * DEPTH, [None] * DEPTH
    for l in reversed(range(DEPTH)):
        dx, gbig[l], vals[l], dmods[l] = layer_bwd(dx, saved[l], mods[l], big[l])
    gsm = {n: jnp.stack([v[n] for v in vals]) for n in SMALL}
    return loss_tile, dx, gbig, gsm, jnp.stack(dmods)


def _my_index():
    return 4 * lax.axis_index("x") + 2 * lax.axis_index("y") + lax.axis_index("c")


def exchange(tensors, scatter, name):
    n = len(tensors)

    def body(*refs):
        ins, outs = refs[:n], refs[n:2 * n]
        send_sems, recv_sems, local_sems = refs[2 * n:]
        x, y, c = lax.axis_index("x"), lax.axis_index("y"), lax.axis_index("c")
        me = 4 * x + 2 * y + c
        local = []
        for t in range(n):
            cp = pltpu.make_async_copy(ins[t].at[me] if scatter else ins[t], outs[t].at[me], local_sems.at[t])
            cp.start()
            local.append(cp)
        remote = []
        for m in range(1, NDEV):
            px = 1 - x if m & 4 else x
            py = 1 - y if m & 2 else y
            pc = 1 - c if m & 1 else c
            peer = 4 * px + 2 * py + pc
            for t in range(n):
                k = t * (NDEV - 1) + m - 1
                cp = pltpu.make_async_remote_copy(
                    src_ref=ins[t].at[peer] if scatter else ins[t], dst_ref=outs[t].at[me],
                    send_sem=send_sems.at[k], recv_sem=recv_sems.at[k],
                    device_id=(px, py, pc), device_id_type=MESH_IDS)
                cp.start()
                remote.append(cp)
        for cp in remote:
            cp.wait()
        for cp in local:
            cp.wait()

    hbm = pl.BlockSpec(memory_space=pltpu.HBM)
    out_shape = [jax.ShapeDtypeStruct(t.shape if scatter else (NDEV,) + t.shape, t.dtype) for t in tensors]
    return _call(body, name=name, in_specs=[hbm] * n, out_specs=[hbm] * n, out_shape=out_shape,
                 scratch_shapes=[pltpu.SemaphoreType.DMA((n * (NDEV - 1),)), pltpu.SemaphoreType.DMA((n * (NDEV - 1),)),
                                 pltpu.SemaphoreType.DMA((n,))])(*tensors)


def _peers():
    x, y, c = lax.axis_index("x"), lax.axis_index("y"), lax.axis_index("c")
    out = []
    for m in range(1, NDEV):
        px = 1 - x if m & 4 else x
        py = 1 - y if m & 2 else y
        pc = 1 - c if m & 1 else c
        out.append(((px, py, pc), 4 * px + 2 * py + pc))
    return 4 * x + 2 * y + c, out


def _split_copies(v_refs, land_refs, send_sems, recv_sems, scatter):
    me, peers = _peers()
    return [pltpu.make_async_remote_copy(
        src_ref=v_ref.at[idx] if scatter else v_ref, dst_ref=land_ref.at[me],
        send_sem=send_sems.at[t * (NDEV - 1) + k], recv_sem=recv_sems.at[t * (NDEV - 1) + k],
        device_id=pos, device_id_type=MESH_IDS)
        for t, (v_ref, land_ref) in enumerate(zip(v_refs, land_refs)) for k, (pos, idx) in enumerate(peers)]


_HBM_SPEC = pl.BlockSpec(memory_space=pltpu.HBM)
_SEM_SPEC = pl.BlockSpec(memory_space=pltpu.SEMAPHORE)
_SPLIT_EFFECT = pltpu.SideEffectType.DATAFLOW_SIDE_EFFECTING


def exchange_start(tensors, scatter, name):
    n = len(tensors)
    land_shapes = [t.shape if scatter else (NDEV,) + t.shape for t in tensors]

    def body(*refs):
        v_refs, land_refs = refs[:n], refs[n:2 * n]
        send_sems, recv_sems = refs[2 * n], refs[2 * n + 1]
        token = refs[-1]
        for cp in _split_copies(v_refs, land_refs, send_sems, recv_sems, scatter):
            cp.start()
        token[...] = jnp.zeros_like(token)

    sems = pltpu.SemaphoreType.DMA((n * (NDEV - 1),))
    out = _call(
        body, name=name,
        out_shape=(sems, sems, *[pltpu.HBM(t.shape, t.dtype) for t in tensors],
                   *[pltpu.HBM(s, t.dtype) for s, t in zip(land_shapes, tensors)], jax.ShapeDtypeStruct((8, 128), F32)),
        in_specs=(_HBM_SPEC,) * (2 * n),
        out_specs=(_SEM_SPEC, _SEM_SPEC) + (_HBM_SPEC,) * (2 * n) + (pl.BlockSpec(memory_space=pltpu.VMEM),),
        input_output_aliases={i: i + 2 for i in range(2 * n)},
        compiler_params=pltpu.CompilerParams(has_side_effects=_SPLIT_EFFECT),
    )(*[pltpu.with_memory_space_constraint(t, pltpu.HBM) for t in tensors],
      *[pltpu.with_memory_space_constraint(lax.empty(s, t.dtype), pltpu.HBM) for s, t in zip(land_shapes, tensors)])
    return out[0], out[1], list(out[2:2 + n]), list(out[2 + n:2 + 2 * n]), out[-1]


def exchange_wait(started, after, scatter, name):
    send_sems, recv_sems, v_thru, land_thru, _ = started
    n = len(v_thru)

    def body(*refs):
        v_refs, land_refs = refs[:n], refs[n:2 * n]
        for cp in _split_copies(v_refs, land_refs, refs[2 * n], refs[2 * n + 1], scatter):
            cp.wait_send()
            cp.wait_recv()

    out = _call(
        body, name=name,
        out_shape=tuple(pltpu.HBM(t.shape, t.dtype) for t in v_thru + land_thru),
        in_specs=(_HBM_SPEC,) * (2 * n) + (_SEM_SPEC, _SEM_SPEC, pl.BlockSpec(memory_space=pl.ANY)),
        out_specs=(_HBM_SPEC,) * (2 * n), input_output_aliases={i: i for i in range(2 * n)},
        compiler_params=pltpu.CompilerParams(has_side_effects=_SPLIT_EFFECT),
    )(*v_thru, *land_thru, send_sems, recv_sems, after)
    return list(out[:n]), list(out[n:])


def mod_slices(c_all, w_ada, b_loc):
    nl, _, nc = w_ada.shape

    def body(c_ref, w_ref, b_ref, o_ref):
        cv = c_ref[...]
        o_ref[0] = _nn(cv * jax.nn.sigmoid(cv), w_ref[0]) + b_ref[0]

    return _call(body, name="mod_slices", grid=(nl,),
                 in_specs=[pl.BlockSpec((NDEV, D), lambda l: (0, 0)), pl.BlockSpec((1, D, nc), lambda l: (l, 0, 0)),
                           pl.BlockSpec((1, 1, nc), lambda l: (l, 0, 0))],
                 out_specs=pl.BlockSpec((1, NDEV, nc), lambda l: (l, 0, 0)),
                 out_shape=jax.ShapeDtypeStruct((nl, NDEV, nc), F32),
                 compiler_params=_params("arbitrary"))(c_all, w_ada, b_loc.reshape(nl, 1, nc))


def ada_grad(c_all, dm_loc):
    nl, _, nc = dm_loc.shape

    def body(c_ref, d_ref, o_ref):
        cv = c_ref[...]
        o_ref[0] = _tn(cv * jax.nn.sigmoid(cv), d_ref[0])

    return _call(body, name="ada_grad", grid=(nl,),
                 in_specs=[pl.BlockSpec((NDEV, D), lambda l: (0, 0)), pl.BlockSpec((1, NDEV, nc), lambda l: (l, 0, 0))],
                 out_specs=pl.BlockSpec((1, D, nc), lambda l: (l, 0, 0)),
                 out_shape=jax.ShapeDtypeStruct((nl, D, nc), F32),
                 compiler_params=_params("arbitrary"))(c_all, dm_loc)


def sum_chunks(chunks):
    r = chunks.shape[1]

    def body(c_ref, o_ref):
        acc = c_ref[0]
        for i in range(1, NDEV):
            acc = acc + c_ref[i]
        o_ref[...] = acc

    return _call(body, name="sum_chunks", out_shape=jax.ShapeDtypeStruct((r, 128), F32))(chunks)


def _row_tile(r):
    if r <= 256:
        return r
    for t in range(256, 7, -8):
        if r % t == 0:
            return t
    return r


def adamw(w, m, v, g=None, chunks=None, name="adamw"):
    r, cdim = w.shape
    tr = _row_tile(r)
    bc1 = 1.0 - ADAM_B1 ** ADAM_STEP
    bc2 = 1.0 - ADAM_B2 ** ADAM_STEP

    def body(g_ref, w_ref, m_ref, v_ref, go_ref, d_ref, mo_ref, vo_ref):
        if chunks is None:
            grad = g_ref[...]
        else:
            grad = g_ref[0].astype(F32)
            for i in range(1, NDEV):
                grad = grad + g_ref[i].astype(F32)
        mn = ADAM_B1 * m_ref[...] + (1.0 - ADAM_B1) * grad
        vn = ADAM_B2 * v_ref[...] + (1.0 - ADAM_B2) * (grad * grad)
        m_hat = mn / bc1
        v_hat = vn / bc2
        go_ref[...] = grad
        d_ref[...] = -ADAM_LR * (m_hat / (jnp.sqrt(v_hat) + ADAM_EPS) + ADAM_WD * w_ref[...])
        mo_ref[...] = mn
        vo_ref[...] = vn

    tile = pl.BlockSpec((tr, cdim), lambda i: (i, 0))
    gspec = tile if chunks is None else pl.BlockSpec((NDEV, tr, cdim), lambda i: (0, i, 0))
    shp = jax.ShapeDtypeStruct((r, cdim), F32)
    return _call(body, name=name, grid=(r // tr,), in_specs=[gspec, tile, tile, tile],
                 out_specs=[tile] * 4, out_shape=[shp] * 4,
                 compiler_params=_params("arbitrary"))(g if chunks is None else chunks, w, m, v)


def adamw_layers(w, m, v, chunks, name):
    nl, r, cdim = w.shape
    tr = _row_tile(r)
    bc1 = 1.0 - ADAM_B1 ** ADAM_STEP
    bc2 = 1.0 - ADAM_B2 ** ADAM_STEP
    outs = [lax.empty(w.shape, F32) for _ in range(4)]
    for l in range(nl):
        def body(g_ref, w_ref, m_ref, v_ref, p0, p1, p2, p3, go_ref, d_ref, mo_ref, vo_ref):
            grad = g_ref[0].astype(F32)
            for i in range(1, NDEV):
                grad = grad + g_ref[i].astype(F32)
            mn = ADAM_B1 * m_ref[...] + (1.0 - ADAM_B1) * grad
            vn = ADAM_B2 * v_ref[...] + (1.0 - ADAM_B2) * (grad * grad)
            go_ref[...] = grad
            d_ref[...] = -ADAM_LR * ((mn / bc1) / (jnp.sqrt(vn / bc2) + ADAM_EPS) + ADAM_WD * w_ref[...])
            mo_ref[...] = mn
            vo_ref[...] = vn

        tile = pl.BlockSpec((None, tr, cdim), lambda i, l=l: (l, i, 0))
        whole = pl.BlockSpec(memory_space=pl.ANY)
        outs = _call(body, name=f"{name}_{l}", grid=(r // tr,),
                     in_specs=[pl.BlockSpec((NDEV, tr, cdim), lambda i: (0, i, 0)), tile, tile, tile] + [whole] * 4,
                     out_specs=[tile] * 4, out_shape=[jax.ShapeDtypeStruct(w.shape, F32)] * 4,
                     input_output_aliases={4: 0, 5: 1, 6: 2, 7: 3},
                     compiler_params=_params("arbitrary"))(chunks[l], w, m, v, *outs)
    return outs


WEIGHTS = ("w_ada", "b_ada", "w_in", "b_f", "gm_ln_g", "gm_ln_b", "gm_w_s", "gm_b_s", "s5_lam_re", "s5_lam_im",
           "s5_log_dt", "s5_b_re", "s5_b_im", "s5_c_re", "s5_c_im", "s5_d", "s5_w_glu", "s5_b_glu", "w_out", "ln1_g",
           "ln1_b", "w_up", "conv_w", "conv_b", "w_down", "ln2_g", "ln2_b")
LARGE = ("w_in", "w_out", "w_up", "w_down")
TRANSPOSED = ("w_in", "w_up")
PACKED = ("b_ada",) + SMALL
PACK_SEG = 8 * 128


def _gather_cols(g):
    nd, nl, r, c = g.shape
    return jnp.transpose(g, (1, 2, 0, 3)).reshape(nl, r, nd * c)


def _chunk_cols(g):
    nl, r, c8 = g.shape
    return jnp.transpose(g.reshape(nl, r, NDEV, c8 // NDEV), (2, 0, 1, 3))


def _join_rows(g):
    nd, r, c = g.shape
    return g.reshape(nd * r, c)


def _split_rows(g):
    r8, c = g.shape
    return g.reshape(NDEV, r8 // NDEV, c)


def _pack(parts):
    segs = []
    for n in PACKED:
        flat = parts[n].reshape(-1)
        segs.append(jnp.pad(flat, (0, -flat.shape[0] % PACK_SEG)).reshape(-1, 128))
    rows = jnp.concatenate(segs, axis=0)
    return jnp.pad(rows, ((0, -rows.shape[0] % (NDEV * 8)), (0, 0)))


def _unpack(rows, shapes):
    out, off = {}, 0
    for n in PACKED:
        size = math.prod(shapes[n])
        nrows = -(-size // PACK_SEG) * 8
        out[n] = rows[off:off + nrows].reshape(-1)[:size].reshape(shapes[n])
        off += nrows
    return out


def kernel(x, c, w_ada, b_ada, w_in, b_f, gm_ln_g, gm_ln_b, gm_w_s, gm_b_s, s5_lam_re, s5_lam_im, s5_log_dt, s5_b_re, s5_b_im, s5_c_re, s5_c_im, s5_d, s5_w_glu, s5_b_glu, w_out, ln1_g, ln1_b, w_up, conv_w, conv_b, w_down, ln2_g, ln2_b, loss_target, m_w_ada, m_b_ada, m_w_in, m_b_f, m_gm_ln_g, m_gm_ln_b, m_gm_w_s, m_gm_b_s, m_s5_lam_re, m_s5_lam_im, m_s5_log_dt, m_s5_b_re, m_s5_b_im, m_s5_c_re, m_s5_c_im, m_s5_d, m_s5_w_glu, m_s5_b_glu, m_w_out, m_ln1_g, m_ln1_b, m_w_up, m_conv_w, m_conv_b, m_w_down, m_ln2_g, m_ln2_b, v_w_ada, v_b_ada, v_w_in, v_b_f, v_gm_ln_g, v_gm_ln_b, v_gm_w_s, v_gm_b_s, v_s5_lam_re, v_s5_lam_im, v_s5_log_dt, v_s5_b_re, v_s5_b_im, v_s5_c_re, v_s5_c_im, v_s5_d, v_s5_w_glu, v_s5_b_glu, v_w_out, v_ln1_g, v_ln1_b, v_w_up, v_conv_w, v_conv_b, v_w_down, v_ln2_g, v_ln2_b):
    given = dict(locals())
    wts = {n: given[n] for n in WEIGHTS}
    mom = {n: given["m_" + n] for n in WEIGHTS}
    var = {n: given["v_" + n] for n in WEIGHTS}
    nl = w_ada.shape[0]
    me = _my_index()
    ada_cols = w_ada.shape[2]

    (c_all,) = exchange([c], False, "gather_c")
    c_all = c_all.reshape(NDEV, D)
    b_loc = lax.dynamic_slice_in_dim(b_ada, me * ada_cols, ada_cols, axis=1)
    mod_part = mod_slices(c_all, w_ada, b_loc)

    mod_all, conv_all = exchange([mod_part, conv_w], False, "gather_mod")
    mod_mine = lax.dynamic_index_in_dim(mod_all, me, axis=2, keepdims=False)
    mods = jnp.transpose(mod_mine, (1, 0, 2)).reshape(nl, 6, D)
    mods = jnp.pad(mods, ((0, 0), (0, 2), (0, 0)))
    conv_full = _gather_cols(conv_all)
    sp = {n: wts[n] for n in SMALL}
    rowwise = {n: [jnp.swapaxes(a[n], 1, 2) if n in TRANSPOSED else a[n] for a in (wts, mom, var)] for n in LARGE}

    def joined(own, land):
        return _join_rows(lax.dynamic_update_index_in_dim(land, own, me, 0))

    def block(l, names):
        return [rowwise[n][0][l].astype(BF16) for n in names]

    def chunked(grads, names):
        return [_split_rows(grads[n]) for n in names]

    head, tail = LARGE[:1], LARGE[1:]
    got_head = exchange_start(block(0, head), False, "gather_start_0_in")
    rest = block(0, tail)
    rest[0] = rest[0] + got_head[4][0, 0].astype(BF16)
    got_tail = exchange_start(rest, False, "gather_start_0_rest")
    xl, saved, weights = x[0], [], []
    for l in range(nl):
        if l == 0:
            own, land = exchange_wait(got_head, got_tail[4], False, "gather_wait_0_in")
            w = {n: joined(o, g) for n, o, g in zip(head, own, land)}
        else:
            own, land = exchange_wait(started, xl, False, f"gather_wait_{l}")
            w = {n: joined(o, g) for n, o, g in zip(LARGE, own, land)}
        mod_l = mods[l]
        if l + 1 < nl:
            nxt, w["w_in"] = lax.optimization_barrier((block(l + 1, LARGE), w["w_in"]))
            started = exchange_start(nxt, False, f"gather_start_{l + 1}")
            mod_l = mod_l + started[4][0, 0]
        w["w_in"] = jnp.pad(w["w_in"], ((0, NP - D_IN), (0, 0)))
        f = _layer_operands(sp, l)
        mixcat, sv = layer_fwd_mix(xl, mod_l, w["w_in"], f)
        if l == 0:
            own, land = exchange_wait(got_tail, mixcat, False, "gather_wait_0_rest")
            w.update({n: joined(o, g) for n, o, g in zip(tail, own, land)})
        w["conv_w"] = conv_full[l]
        xl, sv = layer_fwd_rest(xl, mixcat, mod_l, w, f, sv)
        weights.append(w)
        saved.append(sv)

    loss_tile, dx = loss_kernel(xl, loss_target[0])

    ffn_names, mix_names = ("w_up", "w_down"), ("w_in", "w_out")
    scattering, vals, dmods, gconv = [None] * nl, [None] * nl, [None] * nl, [None] * nl
    token = jnp.zeros((), F32)
    for l in reversed(range(nl)):
        mod_l = mods[l] + token
        dx1, g_ffn, part = layer_bwd_ffn(dx, saved[l], mod_l, weights[l])
        gconv[l] = g_ffn["conv_w"]
        if l == 0:
            sent_ffn = exchange_start(chunked(g_ffn, ffn_names), True, "scatter_start_0_ffn")
            mod_l = mod_l + sent_ffn[4][0, 0]
        dx, g_mix, vals[l], dmods[l] = layer_bwd_mix(dx1, saved[l], mod_l, weights[l], part)
        g_mix["w_in"] = g_mix["w_in"][:D_IN]
        if l == 0:
            scattering[l] = [(ffn_names, sent_ffn),
                             (mix_names, exchange_start(chunked(g_mix, mix_names), True, "scatter_start_0_mix"))]
        else:
            sent = exchange_start(chunked(dict(g_ffn, **g_mix), LARGE), True, f"scatter_start_{l}")
            scattering[l] = [(LARGE, sent)]
            token = sent[4][0, 0]
    gx = dx
    dmods = jnp.stack(dmods)
    gsm = {n: jnp.stack([v[n] for v in vals]) for n in SMALL}

    gsm["b_ada"] = dmods.reshape(nl, 6 * D)
    packed = _pack(gsm).reshape(NDEV, -1, 128)
    conv_recv, small_recv = exchange([_chunk_cols(jnp.stack(gconv)), packed], True, "scatter_small")
    small_sum = sum_chunks(small_recv)
    small_all, dmod_all = exchange([small_sum, dmods.reshape(nl, 6 * D)], False, "gather_small")

    received = [dict() for _ in range(nl)]

    def arrive(l, k, after):
        names, sent = scattering[l][k]
        own, land = exchange_wait(sent, after, True, f"scatter_wait_{l}_{k}")
        for n, o, g in zip(names, own, land):
            mine = lax.dynamic_index_in_dim(o, me, 0, keepdims=False)
            received[l][n] = lax.dynamic_update_index_in_dim(g, mine, me, 0)
        return land[0]

    after = small_all
    for l in reversed(range(nl)):
        after = arrive(l, 0, after)
    out = {}
    def update(n):
        res = adamw_layers(*rowwise[n], [received[l][n] for l in range(nl)], "adamw_" + n)
        return [jnp.swapaxes(r, 1, 2) for r in res] if n in TRANSPOSED else res

    for n in ffn_names:
        out[n] = update(n)
    shp = conv_w.shape
    two_d = lambda a: a.reshape(shp[0] * shp[1], shp[2])
    res = adamw(two_d(conv_w), two_d(m_conv_w), two_d(v_conv_w),
                chunks=conv_recv.reshape(NDEV, shp[0] * shp[1], shp[2]), name="adamw_conv_w")
    out["conv_w"] = [r.reshape(shp) for r in res]

    dm_loc = lax.dynamic_slice_in_dim(dmod_all, me * ada_cols, ada_cols, axis=2)
    g_ada = ada_grad(c_all, jnp.transpose(dm_loc, (1, 0, 2)))
    two_d = lambda a: a.reshape(nl * D, ada_cols)
    res = adamw(two_d(w_ada), two_d(m_w_ada), two_d(v_w_ada), g=two_d(g_ada), name="adamw_w_ada")
    out["w_ada"] = [r.reshape(w_ada.shape) for r in res]

    shapes = {n: wts[n].shape for n in PACKED}
    res = adamw(_pack(wts), _pack(mom), _pack(var), g=small_all.reshape(-1, 128), name="adamw_small")
    unpacked = [_unpack(r, shapes) for r in res]
    for n in PACKED:
        out[n] = [u[n] for u in unpacked]

    behind = out[ffn_names[-1]][0][0, 0, :128] + out["w_ada"][0][0, 0, :128] + res[0][0, :]
    arrive(0, 1, behind)
    for n in mix_names:
        out[n] = update(n)

    loss = lax.psum(loss_tile[0, 0], ("x", "y", "c"))
    return (loss, gx[None], *[out[n][0] for n in WEIGHTS], *[out[n][1] for n in WEIGHTS],
            *[out[n][2] for n in WEIGHTS], *[out[n][3] for n in WEIGHTS])
```

```python
import math

import jax
import jax.numpy as jnp
from jax import lax
from jax.experimental import pallas as pl
from jax.experimental.pallas import tpu as pltpu

F32 = jnp.float32
BF16 = jnp.bfloat16
MESH_IDS = pl.DeviceIdType.MESH

D = 1024
SEQ = 4096
DEPTH = 4
NDEV = 8
HD = 64
GM_W = 256
GM_H = 4
GM_C = 128
S5_W = 256
S5_G = 16
S5_H = 16
S5_P = 64
S5_N = S5_G * S5_P
FX_W = 512
FX_H = 8
D_IN = 2 * GM_W + S5_W + 3 * FX_W + FX_H
NP = 2432
FF_COL = 2304
DFF = 2816
LN_EPS = 1e-5
DN_ALPHA = (2.0 * DEPTH) ** 0.25
NEG_INF = -1e30
ADAM_LR = 0.001
ADAM_B1 = 0.9
ADAM_B2 = 0.999
ADAM_EPS = 1e-08
ADAM_WD = 0.01
ADAM_STEP = 10

V7X_VMEM_LIMIT = 56 * 1024 * 1024
TS = 512
TS_C = 256
CONV_STRIP = 128
T_S5 = 512
TQ = 512
TQ_FWD = 1024
ATTN_HEADS = 2


def _call(body, **kw):
    return pl.pallas_call(body, **kw)


def _params(*sem):
    return pltpu.CompilerParams(dimension_semantics=sem if sem else None,
                                vmem_limit_bytes=V7X_VMEM_LIMIT)


def _nn(a, b):
    return jnp.dot(a.astype(BF16), b.astype(BF16), preferred_element_type=F32)


def _nt(a, b):
    return lax.dot_general(a.astype(BF16), b.astype(BF16), (((1,), (1,)), ((), ())),
                           preferred_element_type=F32)


def _tn(a, b):
    return lax.dot_general(a.astype(BF16), b.astype(BF16), (((0,), (0,)), ((), ())),
                           preferred_element_type=F32)


@jax.custom_vjp
def _bdot(a, b):
    return _nn(a, b)


def _bdot_fwd(a, b):
    return _nn(a, b), (a, b)


def _bdot_bwd(res, g):
    a, b = res
    return _nt(g, b), _tn(a, g)


_bdot.defvjp(_bdot_fwd, _bdot_bwd)


@jax.custom_vjp
def _bdot_nt(a, b):
    return _nt(a, b)


def _bdot_nt_fwd(a, b):
    return _nt(a, b), (a, b)


def _bdot_nt_bwd(res, g):
    a, b = res
    return _nn(g, b), _tn(g, a)


_bdot_nt.defvjp(_bdot_nt_fwd, _bdot_nt_bwd)


def _ln(r, g, b):
    mu = jnp.mean(r, axis=-1, keepdims=True)
    xc = r - mu
    var = jnp.mean(xc * xc, axis=-1, keepdims=True)
    return xc * lax.rsqrt(var + LN_EPS) * g + b


def _rows(shape):
    return lax.broadcasted_iota(jnp.int32, shape, 0)


def _lanes(shape):
    return lax.broadcasted_iota(jnp.int32, shape, 1)


def mm_mod(a, wt, tn, name, mod, rows):
    s, k = a.shape
    n = wt.shape[0]
    ts = min(TS, s)

    def body(a_ref, m_ref, w_ref, o_ref):
        h = a_ref[...] * (1.0 + m_ref[rows[1]:rows[1] + 1, :]) + m_ref[rows[0]:rows[0] + 1, :]
        o_ref[...] = _nt(h, w_ref[...])

    return _call(body, name=name, grid=(n // tn, s // ts),
                 in_specs=[pl.BlockSpec((ts, k), lambda j, i: (i, 0)), pl.BlockSpec((8, k), lambda j, i: (0, 0)),
                           pl.BlockSpec((tn, k), lambda j, i: (j, 0))],
                 out_specs=pl.BlockSpec((ts, tn), lambda j, i: (i, j)),
                 out_shape=jax.ShapeDtypeStruct((s, n), F32),
                 compiler_params=_params("arbitrary", "arbitrary"))(a, mod, wt)


def mm_nt(dy, w, name):
    s, n = dy.shape
    k = w.shape[0]
    ts = min(TS, s)

    def body(dy_ref, w_ref, o_ref):
        o_ref[...] = _nt(dy_ref[...], w_ref[...])

    return _call(body, name=name, grid=(s // ts,),
                 in_specs=[pl.BlockSpec((ts, n), lambda i: (i, 0)),
                           pl.BlockSpec((k, n), lambda i: (0, 0))],
                 out_specs=pl.BlockSpec((ts, k), lambda i: (i, 0)),
                 out_shape=jax.ShapeDtypeStruct((s, k), F32),
                 compiler_params=_params("arbitrary"))(dy, w)


def mm_nt_mod(dy, w, x, dres, mod, rows, name):
    s, n = dy.shape
    k = w.shape[1]
    ts = min(TS, s)

    def body(dy_ref, w_ref, x_ref, r_ref, m_ref, dx_ref, dsh_ref, dsc_ref):
        @pl.when(pl.program_id(0) == 0)
        def _():
            dsh_ref[...] = jnp.zeros_like(dsh_ref)
            dsc_ref[...] = jnp.zeros_like(dsc_ref)

        dh = _nn(dy_ref[...], w_ref[...])
        dx_ref[...] = r_ref[...] + dh * (1.0 + m_ref[rows[1]:rows[1] + 1, :])
        dsh_ref[...] += jnp.sum(dh, axis=0, keepdims=True)
        dsc_ref[...] += jnp.sum(dh * x_ref[...], axis=0, keepdims=True)

    row = pl.BlockSpec((1, k), lambda i: (0, 0))
    tile = pl.BlockSpec((ts, k), lambda i: (i, 0))
    return _call(body, name=name, grid=(s // ts,),
                 in_specs=[pl.BlockSpec((ts, n), lambda i: (i, 0)),
                           pl.BlockSpec((n, k), lambda i: (0, 0)), tile, tile,
                           pl.BlockSpec((8, k), lambda i: (0, 0))],
                 out_specs=[tile, row, row],
                 out_shape=[jax.ShapeDtypeStruct((s, k), F32),
                            jax.ShapeDtypeStruct((1, k), F32),
                            jax.ShapeDtypeStruct((1, k), F32)],
                 compiler_params=_params("arbitrary"))(dy, w, x, dres, mod)


def mm_tn(a, dy, tn, name, mod=None, rows=None):
    s, k = a.shape
    n = dy.shape[1]
    ts = min(TS, s)
    ns = s // ts

    def body(*refs):
        if mod is None:
            a_ref, dy_ref, o_ref, acc = refs
            h = dy_ref[...]
        else:
            a_ref, dy_ref, m_ref, o_ref, acc = refs
            h = dy_ref[...] * (1.0 + m_ref[rows[1]:rows[1] + 1, :]) + m_ref[rows[0]:rows[0] + 1, :]
        i = pl.program_id(1)

        @pl.when(i == 0)
        def _():
            acc[...] = jnp.zeros_like(acc)

        acc[...] += _tn(a_ref[...], h)

        @pl.when(i == ns - 1)
        def _():
            o_ref[...] = acc[...].astype(BF16)

    in_specs = [pl.BlockSpec((ts, k), lambda j, i: (i, 0)), pl.BlockSpec((ts, tn), lambda j, i: (i, j))]
    args = [a, dy]
    if mod is not None:
        in_specs.append(pl.BlockSpec((8, tn), lambda j, i: (0, j)))
        args.append(mod)
    return _call(body, name=name, grid=(n // tn, s // ts), in_specs=in_specs,
                 out_specs=pl.BlockSpec((k, tn), lambda j, i: (0, j)),
                 out_shape=jax.ShapeDtypeStruct((k, n), BF16),
                 scratch_shapes=[pltpu.VMEM((k, tn), F32)],
                 compiler_params=_params("arbitrary", "arbitrary"))(*args)


def _post_fn(x, br, gate, lg, lb):
    return _ln(DN_ALPHA * x + (1.0 + gate) * br, lg, lb)


def proj_post(a, w, x, mod, grow, lg, lb, name):
    s, k = a.shape
    ts = min(TS, s)

    def body(a_ref, w_ref, x_ref, m_ref, lg_ref, lb_ref, b_ref, o_ref):
        br = jnp.dot(a_ref[...], w_ref[...], preferred_element_type=F32)
        b_ref[...] = br
        o_ref[...] = _post_fn(x_ref[...], br, m_ref[grow:grow + 1, :], lg_ref[...], lb_ref[...])

    tile = pl.BlockSpec((ts, D), lambda i: (i, 0))
    row = pl.BlockSpec((1, D), lambda i: (0, 0))
    out = jax.ShapeDtypeStruct((s, D), F32)
    return _call(body, name=name, grid=(s // ts,),
                 in_specs=[pl.BlockSpec((ts, k), lambda i: (i, 0)), pl.BlockSpec((k, D), lambda i: (0, 0)), tile,
                           pl.BlockSpec((8, D), lambda i: (0, 0)), row, row],
                 out_specs=[tile, tile], out_shape=[out, out],
                 compiler_params=_params("arbitrary"))(a, w, x, mod, lg, lb)


def post_bwd(x, br, mod, grow, lg, lb, dy, name):
    s = x.shape[0]
    ts = min(TS, s)

    def body(x_ref, b_ref, m_ref, lg_ref, lb_ref, dy_ref, dx_ref, db_ref, dg_ref, dlg_ref, dlb_ref):
        @pl.when(pl.program_id(0) == 0)
        def _():
            dg_ref[...] = jnp.zeros_like(dg_ref)
            dlg_ref[...] = jnp.zeros_like(dlg_ref)
            dlb_ref[...] = jnp.zeros_like(dlb_ref)

        _, vjp = jax.vjp(_post_fn, x_ref[...], b_ref[...], m_ref[grow:grow + 1, :], lg_ref[...], lb_ref[...])
        dx, db, dg, dlg, dlb = vjp(dy_ref[...])
        dx_ref[...] = dx
        db_ref[...] = db.astype(BF16)
        dg_ref[...] += dg
        dlg_ref[...] += dlg
        dlb_ref[...] += dlb

    tile = pl.BlockSpec((ts, D), lambda i: (i, 0))
    row = pl.BlockSpec((1, D), lambda i: (0, 0))
    rs = jax.ShapeDtypeStruct((1, D), F32)
    return _call(body, name=name, grid=(s // ts,),
                 in_specs=[tile, tile, pl.BlockSpec((8, D), lambda i: (0, 0)), row, row, tile],
                 out_specs=[tile, tile, row, row, row],
                 out_shape=[jax.ShapeDtypeStruct((s, D), F32), jax.ShapeDtypeStruct((s, D), BF16), rs, rs, rs],
                 compiler_params=_params("arbitrary"))(x, br, mod, lg, lb, dy)


def loss_kernel(y, target):
    s = y.shape[0]
    ts = min(TS, s)

    def body(y_ref, t_ref, l_ref, dy_ref):
        @pl.when(pl.program_id(0) == 0)
        def _():
            l_ref[...] = jnp.zeros_like(l_ref)

        err = y_ref[...] - t_ref[...]
        dy_ref[...] = err * (1.0 / D)
        per_tok = jnp.mean(err * err, axis=-1, keepdims=True)
        l_ref[...] += 0.5 * jnp.sum(per_tok)

    tile = pl.BlockSpec((ts, D), lambda i: (i, 0))
    return _call(body, name="loss", grid=(s // ts,), in_specs=[tile, tile],
                 out_specs=[pl.BlockSpec((8, 128), lambda i: (0, 0)), tile],
                 out_shape=[jax.ShapeDtypeStruct((8, 128), F32), jax.ShapeDtypeStruct((s, D), F32)],
                 compiler_params=_params("arbitrary"))(y, target)


def _gm_pair(u, v, lg, lb, w0, w1, bs0, bs1):
    t = u.shape[0]
    low = _lanes((t, 2 * HD)) < HD

    def head_mean(x):
        lo = jnp.sum(jnp.where(low, x, 0.0), axis=-1, keepdims=True)
        hi = jnp.sum(jnp.where(low, 0.0, x), axis=-1, keepdims=True)
        return jnp.where(low, lo, hi) * (1.0 / HD)

    xc = v - head_mean(v)
    vn = xc * lax.rsqrt(head_mean(xc * xc) + LN_EPS) * lg + lb
    v0 = jnp.where(low, vn, 0.0)
    v1 = jnp.where(low, 0.0, vn)
    causal = _rows((GM_C, GM_C)) >= _lanes((GM_C, GM_C))
    wm0 = jnp.where(causal, w0, 0.0)
    wm1 = jnp.where(causal, w1, 0.0)
    bias = jnp.where(_lanes((GM_C, 2 * HD)) < HD, bs0, bs1)
    chunks = []
    for n in range(t // GM_C):
        rs = slice(n * GM_C, (n + 1) * GM_C)
        chunks.append(u[rs] * (_bdot(wm0, v0[rs]) + _bdot(wm1, v1[rs]) + bias))
    return jnp.concatenate(chunks, axis=0)


def gm_fwd(p, lg, lb, ws, bst):
    s = p.shape[0]
    ts = min(TS_C, s)

    def body(u_ref, v_ref, lg_ref, lb_ref, ws_ref, bs_ref, o_ref):
        for j in range(GM_H // 2):
            sl = slice(j * 2 * HD, (j + 1) * 2 * HD)
            o_ref[:, sl] = _gm_pair(u_ref[:, sl], v_ref[:, sl], lg_ref[:, sl], lb_ref[:, sl], ws_ref[2 * j],
                                    ws_ref[2 * j + 1], bs_ref[:, 2 * j:2 * j + 1],
                                    bs_ref[:, 2 * j + 1:2 * j + 2]).astype(BF16)

    full = lambda shape: pl.BlockSpec(shape, lambda i: (0,) * len(shape))
    return _call(body, name="gm_fwd", grid=(s // ts,),
                 in_specs=[pl.BlockSpec((ts, GM_W), lambda i: (i, 0)), pl.BlockSpec((ts, GM_W), lambda i: (i, 1)),
                           full((1, GM_W)), full((1, GM_W)), full((GM_H, GM_C, GM_C)), full((GM_C, GM_H))],
                 out_specs=pl.BlockSpec((ts, GM_W), lambda i: (i, 0)),
                 out_shape=jax.ShapeDtypeStruct((s, GM_W), BF16),
                 compiler_params=_params("arbitrary"))(p, p, lg, lb, ws, bst)


def gm_bwd(p, lg, lb, ws, bst, dmix):
    s = p.shape[0]
    ts = min(TS_C, s)

    def body(u_ref, v_ref, lg_ref, lb_ref, ws_ref, bs_ref, dy_ref, duv_ref, dlg_ref, dlb_ref, dws_ref, dbs_ref):
        @pl.when(pl.program_id(0) == 0)
        def _():
            dlg_ref[...] = jnp.zeros_like(dlg_ref)
            dlb_ref[...] = jnp.zeros_like(dlb_ref)
            dws_ref[...] = jnp.zeros_like(dws_ref)
            dbs_ref[...] = jnp.zeros_like(dbs_ref)

        for j in range(GM_H // 2):
            sl = slice(j * 2 * HD, (j + 1) * 2 * HD)
            _, vjp = jax.vjp(_gm_pair, u_ref[:, sl], v_ref[:, sl], lg_ref[:, sl], lb_ref[:, sl], ws_ref[2 * j],
                             ws_ref[2 * j + 1], bs_ref[:, 2 * j:2 * j + 1], bs_ref[:, 2 * j + 1:2 * j + 2])
            du, dv, dlg, dlb, dw0, dw1, dbs0, dbs1 = vjp(dy_ref[:, sl])
            duv_ref[:, sl] = du.astype(BF16)
            duv_ref[:, GM_W + j * 2 * HD:GM_W + (j + 1) * 2 * HD] = dv.astype(BF16)
            dlg_ref[:, sl] += dlg
            dlb_ref[:, sl] += dlb
            dws_ref[2 * j] += dw0
            dws_ref[2 * j + 1] += dw1
            dbs_ref[:, 2 * j:2 * j + 1] += dbs0
            dbs_ref[:, 2 * j + 1:2 * j + 2] += dbs1

    full = lambda shape: pl.BlockSpec(shape, lambda i: (0,) * len(shape))
    return _call(body, name="gm_bwd", grid=(s // ts,),
                 in_specs=[pl.BlockSpec((ts, GM_W), lambda i: (i, 0)), pl.BlockSpec((ts, GM_W), lambda i: (i, 1)),
                           full((1, GM_W)), full((1, GM_W)), full((GM_H, GM_C, GM_C)), full((GM_C, GM_H)),
                           pl.BlockSpec((ts, GM_W), lambda i: (i, 0))],
                 out_specs=[pl.BlockSpec((ts, 2 * GM_W), lambda i: (i, 0)), full((1, GM_W)), full((1, GM_W)),
                            full((GM_H, GM_C, GM_C)), full((GM_C, GM_H))],
                 out_shape=[jax.ShapeDtypeStruct((s, 2 * GM_W), BF16), jax.ShapeDtypeStruct((1, GM_W), F32),
                            jax.ShapeDtypeStruct((1, GM_W), F32), jax.ShapeDtypeStruct((GM_H, GM_C, GM_C), F32),
                            jax.ShapeDtypeStruct((GM_C, GM_H), F32)],
                 compiler_params=_params("arbitrary"))(p, p, lg, lb, ws, bst, dmix)


def _s5_prep_fn(lr, li, ldt, bt):
    dt = jnp.exp(ldt)
    er = jnp.exp(lr * dt)
    ar = er * jnp.cos(li * dt)
    ai = er * jnp.sin(li * dt)
    den = lr * lr + li * li
    nr = ar - 1.0
    cr = (nr * lr + ai * li) / den
    ci = (ai * lr - nr * li) / den
    br, bi = bt[:, :S5_N], bt[:, S5_N:]
    return ar, ai, jnp.concatenate([cr * br - ci * bi, cr * bi + ci * br], axis=1)


def s5_prep_fwd(lr, li, ldt, bt):
    def body(lr_ref, li_ref, ldt_ref, bt_ref, a_ref, bb_ref):
        ar, ai, bb = _s5_prep_fn(lr_ref[...], li_ref[...], ldt_ref[...], bt_ref[...])
        a_ref[...] = jnp.concatenate([ar, ai, jnp.zeros((6, S5_N), F32)], axis=0)
        bb_ref[...] = bb

    return _call(body, name="s5_prep_fwd",
                 out_shape=[jax.ShapeDtypeStruct((8, S5_N), F32), jax.ShapeDtypeStruct((S5_H, 2 * S5_N), F32)])(lr, li, ldt, bt)


def s5_prep_bwd(lr, li, ldt, bt, da, dbb):
    def body(lr_ref, li_ref, ldt_ref, bt_ref, da_ref, dbb_ref, dlr_ref, dli_ref, dldt_ref, dbt_ref):
        _, vjp = jax.vjp(_s5_prep_fn, lr_ref[...], li_ref[...], ldt_ref[...], bt_ref[...])
        dlr, dli, dldt, dbt = vjp((da_ref[0:1, :], da_ref[1:2, :], dbb_ref[...]))
        dlr_ref[...] = dlr
        dli_ref[...] = dli
        dbt_ref[...] = dbt
        group = (_rows((S5_N, 128)) // S5_P == _lanes((S5_N, 128))).astype(F32)
        dldt_ref[...] = jnp.dot(jnp.broadcast_to(dldt, (8, S5_N)), group, precision=lax.Precision.HIGHEST,
                                preferred_element_type=F32)[0:1, :]

    r = jax.ShapeDtypeStruct((1, S5_N), F32)
    return _call(body, name="s5_prep_bwd",
                 out_shape=[r, r, jax.ShapeDtypeStruct((1, 128), F32),
                            jax.ShapeDtypeStruct((S5_H, 2 * S5_N), F32)])(lr, li, ldt, bt, da, dbb)


def _s5_out_fn(x, u, cbd, drow, wg, bg):
    y = _bdot_nt(x[:, :S5_N], cbd[:, :S5_N]) - _bdot_nt(x[:, S5_N:], cbd[:, S5_N:]) + drow * u
    y = jax.nn.gelu(y)
    gate = _bdot_nt(y, wg) + bg
    return y * jax.nn.sigmoid(gate)


def _scan_chunk(buf, ar, ai, cr, ci, reverse):
    t = buf.shape[0]

    def local(xr, xi, rows):
        within = _rows(xr.shape) % 8
        pr, pi = ar, ai
        for d in (1, 2, 4):
            keep = within < 8 - d if reverse else within >= d
            shift = rows - d if reverse else d
            sr = jnp.where(keep, pltpu.roll(xr, shift, 0), 0.0)
            si = jnp.where(keep, pltpu.roll(xi, shift, 0), 0.0)
            xr, xi = xr + pr * sr - pi * si, xi + pr * si + pi * sr
            pr, pi = pr * pr - pi * pi, 2.0 * pr * pi
        return xr, xi

    xr, xi = local(buf[:, :S5_N], buf[:, S5_N:], t)
    buf[:, :S5_N] = xr
    buf[:, S5_N:] = xi
    edge = _rows((8, S5_N)) == (7 if reverse else 0)
    pr8, pi8 = local(jnp.where(edge, ar, 0.0), jnp.where(edge, ai, 0.0), 8)

    def group(j, c):
        g = t // 8 - 1 - j if reverse else j
        rows = pl.ds(pl.multiple_of(g * 8, 8), 8)
        gr = buf[rows, :S5_N] + pr8 * c[0] - pi8 * c[1]
        gi = buf[rows, S5_N:] + pr8 * c[1] + pi8 * c[0]
        buf[rows, :S5_N] = gr
        buf[rows, S5_N:] = gi
        return (gr[0:1, :], gi[0:1, :]) if reverse else (gr[7:8, :], gi[7:8, :])

    return lax.fori_loop(0, t // 8, group, (cr, ci), unroll=4)


def s5_fwd(p, arow, bbd, cbd, drow, wg, bg):
    s = p.shape[0]
    t = min(T_S5, s)

    def body(u_ref, a_ref, bbd_ref, cbd_ref, d_ref, wg_ref, bg_ref, y_ref, st_ref, carry):
        @pl.when(pl.program_id(0) == 0)
        def _():
            carry[...] = jnp.zeros_like(carry)

        u = u_ref[...]
        st_ref[...] = _nn(u, bbd_ref[...])
        cr, ci = _scan_chunk(st_ref, a_ref[0:1, :], a_ref[1:2, :], carry[0:1, :S5_N], carry[0:1, S5_N:], False)
        carry[0:1, :S5_N] = cr
        carry[0:1, S5_N:] = ci
        y_ref[...] = _s5_out_fn(st_ref[...], u, cbd_ref[...], d_ref[...], wg_ref[...], bg_ref[...]).astype(BF16)

    full = lambda shape: pl.BlockSpec(shape, lambda i: (0,) * len(shape))
    return _call(body, name="s5_fwd", grid=(s // t,),
                 in_specs=[pl.BlockSpec((t, S5_W), lambda i: (i, 2)), full((8, S5_N)), full((S5_W, 2 * S5_N)),
                           full((S5_W, 2 * S5_N)), full((1, S5_W)), full((S5_W, S5_W)), full((1, S5_W))],
                 out_specs=[pl.BlockSpec((t, S5_W), lambda i: (i, 0)), pl.BlockSpec((t, 2 * S5_N), lambda i: (i, 0))],
                 out_shape=[jax.ShapeDtypeStruct((s, S5_W), BF16), jax.ShapeDtypeStruct((s, 2 * S5_N), F32)],
                 scratch_shapes=[pltpu.VMEM((8, 2 * S5_N), F32)],
                 compiler_params=_params("arbitrary"))(p, arow, bbd, cbd, drow, wg, bg)


def s5_bwd(p, st, arow, bbd, cbd, drow, wg, bg, dmix):
    s = p.shape[0]
    t = min(T_S5, s)
    nc = s // t

    def body(u_ref, st_ref, prev_ref, a_ref, bbd_ref, cbd_ref, d_ref, wg_ref, bg_ref, dy_ref,
             du_ref, da_ref, dbbd_ref, dcbd_ref, dd_ref, dwg_ref, dbg_ref, carry, gbuf):
        i = pl.program_id(0)

        @pl.when(i == 0)
        def _():
            carry[...] = jnp.zeros_like(carry)
            for r in (da_ref, dbbd_ref, dcbd_ref, dd_ref, dwg_ref, dbg_ref):
                r[...] = jnp.zeros_like(r)

        u = u_ref[...]
        x = st_ref[...]
        _, vjp = jax.vjp(_s5_out_fn, x, u, cbd_ref[...], d_ref[...], wg_ref[...], bg_ref[...])
        dx, du1, dcbd, dd, dwg, dbg = vjp(dy_ref[...])
        gbuf[...] = dx
        cr, ci = _scan_chunk(gbuf, a_ref[0:1, :], -a_ref[1:2, :], carry[0:1, :S5_N], carry[0:1, S5_N:], True)
        carry[0:1, :S5_N] = cr
        carry[0:1, S5_N:] = ci
        gr, gi = gbuf[:, :S5_N], gbuf[:, S5_N:]
        rid = _rows((t, S5_N))
        has_prev = (i < nc - 1).astype(F32)
        top_r = prev_ref[7:8, :S5_N] * has_prev
        top_i = prev_ref[7:8, S5_N:] * has_prev
        xpr = jnp.where(rid == 0, top_r, pltpu.roll(x[:, :S5_N], 1, 0))
        xpi = jnp.where(rid == 0, top_i, pltpu.roll(x[:, S5_N:], 1, 0))
        da_ref[0:1, :] += jnp.sum(xpr * gr + xpi * gi, axis=0, keepdims=True)
        da_ref[1:2, :] += jnp.sum(xpr * gi - xpi * gr, axis=0, keepdims=True)
        g = jnp.concatenate([gr, gi], axis=1)
        dbbd_ref[...] += _tn(u, g)
        du_ref[...] = (_nt(g, bbd_ref[...]) + du1).astype(BF16)
        dcbd_ref[...] += dcbd
        dd_ref[...] += dd
        dwg_ref[...] += dwg
        dbg_ref[...] += dbg

    full = lambda shape: pl.BlockSpec(shape, lambda i: (0,) * len(shape))
    rev = lambda col: (lambda i: (nc - 1 - i, col))
    prev_map = lambda i: (jnp.maximum((nc - 1 - i) * (t // 8) - 1, 0), 0)
    return _call(body, name="s5_bwd", grid=(nc,),
                 in_specs=[pl.BlockSpec((t, S5_W), rev(2)), pl.BlockSpec((t, 2 * S5_N), rev(0)),
                           pl.BlockSpec((8, 2 * S5_N), prev_map), full((8, S5_N)), full((S5_W, 2 * S5_N)),
                           full((S5_W, 2 * S5_N)), full((1, S5_W)), full((S5_W, S5_W)), full((1, S5_W)),
                           pl.BlockSpec((t, S5_W), rev(1))],
                 out_specs=[pl.BlockSpec((t, S5_W), rev(0)), full((8, S5_N)), full((S5_W, 2 * S5_N)),
                            full((S5_W, 2 * S5_N)), full((1, S5_W)), full((S5_W, S5_W)), full((1, S5_W))],
                 out_shape=[jax.ShapeDtypeStruct((s, S5_W), BF16), jax.ShapeDtypeStruct((8, S5_N), F32),
                            jax.ShapeDtypeStruct((S5_W, 2 * S5_N), F32), jax.ShapeDtypeStruct((S5_W, 2 * S5_N), F32),
                            jax.ShapeDtypeStruct((1, S5_W), F32), jax.ShapeDtypeStruct((S5_W, S5_W), F32),
                            jax.ShapeDtypeStruct((1, S5_W), F32)],
                 scratch_shapes=[pltpu.VMEM((8, 2 * S5_N), F32), pltpu.VMEM((t, 2 * S5_N), F32)],
                 compiler_params=_params("arbitrary"))(p, st, st, arow, bbd, cbd, drow, wg, bg, dmix)


def _cum_steps(s):
    return int(math.ceil(math.log2(s)))


V_BLK = (2 * GM_W + S5_W + 2 * FX_W) // 128


AUG = 2 * HD
BIAS_COL = HD
FQ_COL = HD + 3
PAIR_W = 256


def _split3(f):
    hi = f.astype(BF16).astype(F32)
    r = f - hi
    mid = r.astype(BF16).astype(F32)
    lo = (r - mid).astype(BF16).astype(F32)
    return hi, mid, lo


def fox_prep(p, bf):
    s = p.shape[0]
    ts = min(TS, s)
    scale = HD ** -0.5

    def body(q0_ref, q1_ref, k0_ref, k1_ref, v0_ref, v1_ref, f_ref, bf_ref,
             qa_ref, ka_ref, qat_ref, kat_ref, vt_ref, carry):
        @pl.when(pl.program_id(0) == 0)
        def _():
            carry[...] = jnp.zeros_like(carry)

        lane = _lanes((ts, 128))
        lf = jax.nn.log_sigmoid(f_ref[...] + bf_ref[...])
        acc = jnp.where(lane < FX_H, lf, 0.0)
        rid = _rows((ts, 128))
        for k in range(_cum_steps(ts)):
            d = 1 << k
            acc = acc + jnp.where(rid >= d, pltpu.roll(acc, d, 0), 0.0)
        acc = acc + carry[0:1, :]
        carry[0:1, :] = acc[ts - 1:ts, :]

        low = lane < HD
        for h in range(FX_H):
            blk, pos = divmod(h, 4)
            pair = slice((pos // 2) * 128, (pos // 2) * 128 + 128)
            hi, mid, lo = _split3(acc[:, h:h + 1])
            one = jnp.ones((ts, 1), F32)

            def augment(ref, cols):
                x = ref[:, pair]
                if pos % 2:
                    x = pltpu.roll(x, HD, 1)
                out = jnp.where(low, x, 0.0)
                for j, cval in enumerate(cols):
                    out = jnp.where(lane == HD + j, cval, out)
                return out

            qa = augment((q0_ref, q1_ref)[blk], (one, one, one, hi, mid, lo))
            qa = jnp.where(low, qa * scale, qa)
            ka = augment((k0_ref, k1_ref)[blk], (-hi, -mid, -lo, one, one, one))
            cs = slice(h * AUG, (h + 1) * AUG)
            qa_ref[:, cs] = qa.astype(BF16)
            ka_ref[:, cs] = ka.astype(BF16)
            qat_ref[cs, :] = jnp.transpose(qa).astype(BF16)
            kat_ref[cs, :] = jnp.transpose(ka).astype(BF16)
        for j in range(FX_H // 2):
            vref = (v0_ref, v1_ref)[j // 2]
            vt_ref[j * 128:(j + 1) * 128, :] = jnp.transpose(vref[:, (j % 2) * 128:(j % 2) * 128 + 128]).astype(BF16)

    q_blk = (2 * GM_W + S5_W) // PAIR_W
    col = lambda b: pl.BlockSpec((ts, PAIR_W), lambda i: (i, b))
    wide = FX_H * AUG
    return _call(body, name="fox_prep", grid=(s // ts,),
                 in_specs=[col(q_blk), col(q_blk + 1), col(q_blk + 2), col(q_blk + 3), col(q_blk + 4), col(q_blk + 5),
                           pl.BlockSpec((ts, 128), lambda i: (i, FF_COL // 128)), pl.BlockSpec((1, 128), lambda i: (0, 0))],
                 out_specs=[pl.BlockSpec((ts, wide), lambda i: (i, 0)), pl.BlockSpec((ts, wide), lambda i: (i, 0)),
                            pl.BlockSpec((wide, ts), lambda i: (0, i)), pl.BlockSpec((wide, ts), lambda i: (0, i)),
                            pl.BlockSpec((FX_W, ts), lambda i: (0, i))],
                 out_shape=[jax.ShapeDtypeStruct((s, wide), BF16), jax.ShapeDtypeStruct((s, wide), BF16),
                            jax.ShapeDtypeStruct((wide, s), BF16), jax.ShapeDtypeStruct((wide, s), BF16),
                            jax.ShapeDtypeStruct((FX_W, s), BF16)],
                 scratch_shapes=[pltpu.VMEM((8, 128), F32)],
                 compiler_params=_params("arbitrary"))(p, p, p, p, p, p, p, bf)


def fox_prep_grad(p, bf, dfq, dfk):
    s = p.shape[0]
    ts = min(TS, s)
    ns = s // ts

    def body(f_ref, bf_ref, dfq_ref, dfk_ref, df_ref, dbf_ref, carry):
        @pl.when(pl.program_id(0) == 0)
        def _():
            carry[...] = jnp.zeros_like(carry)
            dbf_ref[...] = jnp.zeros_like(dbf_ref)

        lane = _lanes((ts, 128))
        acc = jnp.zeros((ts, 128), F32)
        for h in range(FX_H):
            c = (h // 2) * 128 + h % 2
            acc = jnp.where(lane == h, dfq_ref[:, c:c + 1] + dfk_ref[:, c:c + 1], acc)
        rid = _rows((ts, 128))
        for k in range(_cum_steps(ts)):
            d = 1 << k
            acc = acc + jnp.where(rid < ts - d, pltpu.roll(acc, ts - d, 0), 0.0)
        acc = acc + carry[0:1, :]
        carry[0:1, :] = acc[0:1, :]
        z = f_ref[...] + bf_ref[...]
        df = jnp.where(lane < FX_H, acc * jax.nn.sigmoid(-z), 0.0)
        df_ref[...] = df.astype(BF16)
        dbf_ref[...] += jnp.sum(df, axis=0, keepdims=True)

    rev = lambda i: (ns - 1 - i, 0)
    return _call(body, name="fox_prep_grad", grid=(ns,),
                 in_specs=[pl.BlockSpec((ts, 128), lambda i: (ns - 1 - i, FF_COL // 128)),
                           pl.BlockSpec((1, 128), lambda i: (0, 0)),
                           pl.BlockSpec((ts, FX_W), rev), pl.BlockSpec((ts, FX_W), rev)],
                 out_specs=[pl.BlockSpec((ts, 128), rev), pl.BlockSpec((1, 128), lambda i: (0, 0))],
                 out_shape=[jax.ShapeDtypeStruct((s, 128), BF16), jax.ShapeDtypeStruct((1, 128), F32)],
                 scratch_shapes=[pltpu.VMEM((8, 128), F32)],
                 compiler_params=_params("arbitrary"))(p, bf, dfq, dfk)


def attn(qat, ka, vt):
    s = ka.shape[0]
    tq = min(TQ_FWD, s)
    nq = s // tq

    nh = ATTN_HEADS

    def body(qat_ref, ka_ref, vt_ref, o_ref, lse_ref):
        qi = pl.program_id(1)
        half = tq // 2
        lse_ref[...] = jnp.zeros_like(lse_ref)

        def update(m, l, acc, st, v):
            m_new = jnp.maximum(m, jnp.max(st, axis=0, keepdims=True))
            alpha = jnp.exp(m - m_new)
            pt = jnp.exp(st - m_new)
            return (m_new, alpha * l + jnp.sum(pt, axis=0, keepdims=True),
                    alpha * acc + jnp.dot(v, pt.astype(BF16), preferred_element_type=F32))

        def step(kj, carry):
            off = pl.multiple_of(kj * tq, tq)
            out = []
            for hh in range(nh):
                cs = slice(hh * AUG, (hh + 1) * AUG)
                st = jnp.dot(ka_ref[pl.ds(off, tq), cs], qat_ref[cs, :], preferred_element_type=F32)
                out.append(update(*carry[hh], st, vt_ref[hh * HD:(hh + 1) * HD, pl.ds(off, tq)]))
            return tuple(out)

        def diagonal(carry):
            off = pl.multiple_of(qi * tq, tq)
            off2 = pl.multiple_of(off + half, half)
            lower = _rows((half, tq)) <= _lanes((half, tq))
            out = []
            for hh in range(nh):
                cs = slice(hh * AUG, (hh + 1) * AUG)
                hs = slice(hh * HD, (hh + 1) * HD)
                st = jnp.dot(ka_ref[pl.ds(off, half), cs], qat_ref[cs, :], preferred_element_type=F32)
                m, l, acc = update(*carry[hh], jnp.where(lower, st, NEG_INF), vt_ref[hs, pl.ds(off, half)])
                st = jnp.dot(ka_ref[pl.ds(off2, half), cs], qat_ref[cs, half:], preferred_element_type=F32)
                mr, lr, ar = update(m[:, half:], l[:, half:], acc[:, half:], jnp.where(lower[:, :half], st, NEG_INF),
                                    vt_ref[hs, pl.ds(off2, half)])
                out.append((jnp.concatenate([m[:, :half], mr], axis=1), jnp.concatenate([l[:, :half], lr], axis=1),
                            jnp.concatenate([acc[:, :half], ar], axis=1)))
            return tuple(out)

        init = tuple((jnp.full((1, tq), NEG_INF, F32), jnp.zeros((1, tq), F32), jnp.zeros((HD, tq), F32))
                     for _ in range(nh))
        carry = diagonal(lax.fori_loop(0, qi, step, init))
        for hh in range(nh):
            m, l, _ = carry[hh]
            lse_ref[hh // 2, hh % 2:hh % 2 + 1, :] = m + jnp.log(l)
        for j in range(nh // 2):
            pair = jnp.concatenate([carry[2 * j][2] / carry[2 * j][1], carry[2 * j + 1][2] / carry[2 * j + 1][1]], axis=0)
            o_ref[:, j * 128:(j + 1) * 128] = jnp.transpose(pair).astype(BF16)

    return _call(body, name="attn", grid=(FX_H // nh, nq),
                 in_specs=[pl.BlockSpec((nh * AUG, tq), lambda h, i: (h, i)),
                           pl.BlockSpec((s, nh * AUG), lambda h, i: (0, h)),
                           pl.BlockSpec((nh * HD, s), lambda h, i: (h, 0))],
                 out_specs=[pl.BlockSpec((tq, nh * HD), lambda h, i: (i, h)),
                            pl.BlockSpec((nh // 2, 8, tq), lambda h, i: (h, 0, i))],
                 out_shape=[jax.ShapeDtypeStruct((s, FX_W), BF16), jax.ShapeDtypeStruct((FX_H // 2, 8, s), F32)],
                 compiler_params=_params("arbitrary", "arbitrary"))(qat, ka, vt)


def attn_grad(qa, qat, ka, kat, p, o, lse, dmix):
    s = qa.shape[0]
    tq = min(TQ, s)
    nq = s // tq
    scale = HD ** -0.5

    def body(qa_ref, qat_ref, ka_ref, kat_ref, v_ref, o_ref, lse_ref, do_ref,
             dq_ref, dk_ref, dv_ref, dfq_ref, dfk_ref, dot_scr, delta, dqt):
        kj = pl.program_id(1)
        lane = _lanes((tq, 128))
        low = lane < HD
        causal = _rows((tq, tq)) <= _lanes((tq, tq))

        @pl.when(kj == 0)
        def _():
            dqt[...] = jnp.zeros_like(dqt)
            delta[...] = jnp.zeros_like(delta)

            def prep(c, _):
                rows = pl.ds(pl.multiple_of(c * tq, tq), tq)
                do = do_ref[rows, :]
                pt = jnp.transpose(do * o_ref[rows, :].astype(F32))
                delta[0:1, rows] = jnp.sum(pt[:HD], axis=0, keepdims=True)
                delta[1:2, rows] = jnp.sum(pt[HD:], axis=0, keepdims=True)
                dot_scr[:, rows] = jnp.transpose(do).astype(BF16)
                return 0

            lax.fori_loop(0, nq, prep, 0)

        v = v_ref[...]
        vms = [jnp.where(low, v, 0.0).astype(BF16), jnp.where(low, 0.0, v).astype(BF16)]

        def tile(qi, carry, masked):
            cols = pl.ds(pl.multiple_of(qi * tq, tq), tq)
            do = do_ref[cols, :].astype(BF16)
            out = []
            for hh in range(2):
                cs = slice(hh * AUG, (hh + 1) * AUG)
                dka, dv = carry[hh]
                st = jnp.dot(ka_ref[:, cs], qat_ref[cs, cols], preferred_element_type=F32)
                if masked:
                    st = jnp.where(causal, st, NEG_INF)
                pt = jnp.exp(st - lse_ref[0, hh:hh + 1, cols])
                dv = dv + jnp.dot(pt.astype(BF16), do, preferred_element_type=F32)
                dpt = jnp.dot(vms[hh], dot_scr[:, cols], preferred_element_type=F32)
                dsb = (pt * (dpt - delta[hh:hh + 1, cols])).astype(BF16)
                dka = dka + jnp.dot(dsb, qa_ref[cols, cs], preferred_element_type=F32)
                dqt[hh, :, cols] += jnp.dot(kat_ref[cs, :], dsb, preferred_element_type=F32)
                out.append((dka, dv))
            return tuple(out)

        init = tuple((jnp.zeros((tq, AUG), F32), jnp.zeros((tq, 128), F32)) for _ in range(2))
        carry = tile(kj, init, True)
        carry = lax.fori_loop(kj + 1, nq, lambda qi, c: tile(qi, c, False), carry)
        dks = [carry[0][0], carry[1][0]]
        dvs = [carry[0][1], carry[1][1]]
        dv_ref[...] = jnp.where(low, dvs[0], dvs[1]).astype(BF16)
        dk_ref[...] = jnp.where(low, dks[0], pltpu.roll(dks[1], HD, 1)).astype(BF16)
        dfk_ref[...] = jnp.where(lane == 0, -dks[0][:, BIAS_COL:BIAS_COL + 1],
                                 jnp.where(lane == 1, -dks[1][:, BIAS_COL:BIAS_COL + 1], 0.0))

        @pl.when(kj == nq - 1)
        def _():
            def finish(c, _):
                rows = pl.ds(pl.multiple_of(c * tq, tq), tq)
                t0 = jnp.transpose(dqt[0, :, rows])
                t1 = jnp.transpose(dqt[1, :, rows])
                dq_ref[rows, :] = (jnp.where(low, t0, pltpu.roll(t1, HD, 1)) * scale).astype(BF16)
                dfq_ref[rows, :] = jnp.where(lane == 0, t0[:, FQ_COL:FQ_COL + 1],
                                             jnp.where(lane == 1, t1[:, FQ_COL:FQ_COL + 1], 0.0))
                return 0

            lax.fori_loop(0, nq, finish, 0)

    seq128 = lambda blk: pl.BlockSpec((s, 128), lambda h, j: (0, blk + h))
    tile128 = pl.BlockSpec((tq, 128), lambda h, j: (j, h))
    out_b = jax.ShapeDtypeStruct((s, FX_W), BF16)
    out_f = jax.ShapeDtypeStruct((s, FX_W), F32)
    return _call(body, name="attn_grad", grid=(FX_H // 2, nq),
                 in_specs=[pl.BlockSpec((s, 2 * AUG), lambda h, j: (0, h)), pl.BlockSpec((2 * AUG, s), lambda h, j: (h, 0)),
                           pl.BlockSpec((tq, 2 * AUG), lambda h, j: (j, h)), pl.BlockSpec((2 * AUG, tq), lambda h, j: (h, j)),
                           pl.BlockSpec((tq, 128), lambda h, j: (j, V_BLK + h)), seq128(0),
                           pl.BlockSpec((1, 8, s), lambda h, j: (h, 0, 0)), seq128(4)],
                 out_specs=[seq128(0), tile128, tile128, seq128(0), tile128],
                 out_shape=[out_b, out_b, out_b, out_f, out_f],
                 scratch_shapes=[pltpu.VMEM((128, s), BF16), pltpu.VMEM((8, s), F32), pltpu.VMEM((2, AUG, s), F32)],
                 compiler_params=_params("arbitrary", "arbitrary"))(qa, qat, ka, kat, p, o, lse, dmix)


def _shift_down(a, prev8, k):
    r = pltpu.roll(a, k, 0)
    top = jnp.where(_rows(prev8.shape) < k, pltpu.roll(prev8, k, 0), r[0:8])
    return jnp.concatenate([top, r[8:]], axis=0)


def _shift_up(a, next8, k):
    t = a.shape[0]
    r = pltpu.roll(a, t - k, 0)
    bot = jnp.where(_rows(next8.shape) >= 8 - k, pltpu.roll(next8, 8 - k, 0), r[t - 8:t])
    return jnp.concatenate([r[:t - 8], bot], axis=0)


def _conv(a, prev8, cw, cb):
    return cb + cw[0:1, :] * _shift_down(a, prev8, 2) + cw[1:2, :] * _shift_down(a, prev8, 1) + cw[2:3, :] * a


GELU_K0 = math.sqrt(2.0 / math.pi)
GELU_K1 = GELU_K0 * 0.044715


def _gelu_parts(c):
    c2 = c * c
    return c2, 0.5 + 0.5 * jnp.tanh(c * (GELU_K0 + GELU_K1 * c2))


def conv_fwd(up, cw, cb):
    s = up.shape[0]
    ts = min(TS_C, s)

    def body(a_ref, g_ref, cw_ref, cb_ref, o_ref, c_ref, halo):
        @pl.when(pl.program_id(0) == 0)
        def _():
            halo[...] = jnp.zeros_like(halo)

        a = a_ref[...]
        c = _conv(a, halo[...], cw_ref[...], cb_ref[...])
        _, h = _gelu_parts(c)
        c_ref[...] = c
        o_ref[...] = (c * h * g_ref[...]).astype(BF16)
        halo[...] = a[ts - 8:ts, :]

    tile = pl.BlockSpec((ts, DFF), lambda i: (i, 0))
    return _call(body, name="conv_fwd", grid=(s // ts,),
                 in_specs=[tile, pl.BlockSpec((ts, DFF), lambda i: (i, 1)),
                           pl.BlockSpec((3, DFF), lambda i: (0, 0)), pl.BlockSpec((1, DFF), lambda i: (0, 0))],
                 out_specs=[tile, tile],
                 out_shape=[jax.ShapeDtypeStruct((s, DFF), BF16), jax.ShapeDtypeStruct((s, DFF), F32)],
                 scratch_shapes=[pltpu.VMEM((8, DFF), F32)],
                 compiler_params=_params("arbitrary"))(up, up, cw, cb)


def conv_bwd(up, c, cw, dact):
    s = up.shape[0]
    ts = min(TS_C, s)
    ns = s // ts

    def body(a_ref, g_ref, c_ref, cw_ref, dact_ref, dup_ref, dcw_ref, dcb_ref, halo):
        @pl.when(pl.program_id(0) == 0)
        def _():
            halo[...] = jnp.zeros_like(halo)
            dcw_ref[...] = jnp.zeros_like(dcw_ref)
            dcb_ref[...] = jnp.zeros_like(dcb_ref)

        def strip(j, _):
            cols = pl.ds(pl.multiple_of(j * CONV_STRIP, CONV_STRIP), CONV_STRIP)
            gate_cols = pl.ds(pl.multiple_of(DFF + j * CONV_STRIP, CONV_STRIP), CONV_STRIP)
            a = a_ref[:, cols]
            cw = cw_ref[:, cols]
            cv = c_ref[:, cols]
            dact = dact_ref[:, cols]
            c2, h = _gelu_parts(cv)
            dup_ref[:, gate_cols] = (dact * (cv * h)).astype(BF16)
            dgel = h + cv * (2.0 * h * (1.0 - h)) * (GELU_K0 + 3.0 * GELU_K1 * c2)
            dc = dact * g_ref[:, cols] * dgel
            nxt = halo[:, cols]
            up1 = _shift_up(dc, nxt, 1)
            up2 = _shift_up(dc, nxt, 2)
            dup_ref[:, cols] = (cw[2:3, :] * dc + cw[1:2, :] * up1 + cw[0:1, :] * up2).astype(BF16)
            dcw_ref[0:1, cols] += jnp.sum(a * up2, axis=0, keepdims=True)
            dcw_ref[1:2, cols] += jnp.sum(a * up1, axis=0, keepdims=True)
            dcw_ref[2:3, cols] += jnp.sum(a * dc, axis=0, keepdims=True)
            dcb_ref[:, cols] += jnp.sum(dc, axis=0, keepdims=True)
            halo[:, cols] = dc[0:8, :]
            return 0

        lax.fori_loop(0, DFF // CONV_STRIP, strip, 0)

    rev = lambda col: (lambda i: (ns - 1 - i, col))
    return _call(body, name="conv_bwd", grid=(ns,),
                 in_specs=[pl.BlockSpec((ts, DFF), rev(0)), pl.BlockSpec((ts, DFF), rev(1)),
                           pl.BlockSpec((ts, DFF), rev(0)), pl.BlockSpec((3, DFF), lambda i: (0, 0)),
                           pl.BlockSpec((ts, DFF), rev(0))],
                 out_specs=[pl.BlockSpec((ts, 2 * DFF), rev(0)), pl.BlockSpec((3, DFF), lambda i: (0, 0)),
                            pl.BlockSpec((1, DFF), lambda i: (0, 0))],
                 out_shape=[jax.ShapeDtypeStruct((s, 2 * DFF), BF16), jax.ShapeDtypeStruct((3, DFF), F32),
                            jax.ShapeDtypeStruct((1, DFF), F32)],
                 scratch_shapes=[pltpu.VMEM((8, DFF), F32)],
                 compiler_params=_params("arbitrary"))(up, up, c, cw, dact)


def _blockdiag_expand(m):
    m4 = m.reshape(S5_H, 2, S5_G, S5_P)
    eye = jnp.eye(S5_G, dtype=bool)[:, None, None, :, None]
    return jnp.where(eye, m4[None], 0.0).reshape(S5_W, 2 * S5_N)


def _blockdiag_extract(mbd):
    m5 = mbd.reshape(S5_G, S5_H, 2, S5_G, S5_P)
    diag = jnp.stack([m5[g, :, :, g, :] for g in range(S5_G)], axis=2)
    return diag.reshape(S5_H, 2 * S5_N)


def _c_expand(c_re, c_im):
    c4 = jnp.stack([c_re, c_im], axis=2)
    eye = jnp.eye(S5_G, dtype=bool)[:, None, None, :, None]
    return jnp.where(eye, c4[:, :, :, None, :], 0.0).reshape(S5_W, 2 * S5_N)


def _c_extract(cbd):
    m5 = cbd.reshape(S5_G, S5_H, 2, S5_G, S5_P)
    d = jnp.stack([m5[g, :, :, g, :] for g in range(S5_G)], axis=0)
    return d[:, :, 0, :], d[:, :, 1, :]


def _glu_expand(w):
    eye = jnp.eye(S5_G, dtype=bool)[:, None, :, None]
    return jnp.where(eye, w[:, :, None, :], 0.0).reshape(S5_W, S5_W)


def _glu_extract(wbd):
    m4 = wbd.reshape(S5_G, S5_H, S5_G, S5_H)
    return jnp.stack([m4[g, :, g, :] for g in range(S5_G)], axis=0)


SMALL = ("b_f", "gm_ln_g", "gm_ln_b", "gm_w_s", "gm_b_s", "s5_lam_re", "s5_lam_im", "s5_log_dt", "s5_b_re", "s5_b_im",
         "s5_c_re", "s5_c_im", "s5_d", "s5_w_glu", "s5_b_glu", "ln1_g", "ln1_b", "conv_b", "ln2_g", "ln2_b")


def _layer_operands(sp, l):
    f = {}
    f["bf"] = jnp.pad(sp["b_f"][l][None, :], ((0, 0), (0, 128 - FX_H)))
    f["gm_lg"] = sp["gm_ln_g"][l].reshape(1, GM_W)
    f["gm_lb"] = sp["gm_ln_b"][l].reshape(1, GM_W)
    f["gm_ws"] = sp["gm_w_s"][l]
    f["gm_bst"] = sp["gm_b_s"][l].T
    f["lr"] = sp["s5_lam_re"][l].reshape(1, S5_N)
    f["li"] = sp["s5_lam_im"][l].reshape(1, S5_N)
    f["ldt"] = jnp.repeat(sp["s5_log_dt"][l], S5_P).reshape(1, S5_N)
    bt = lambda b: jnp.transpose(b, (2, 0, 1)).reshape(S5_H, S5_N)
    f["bt"] = jnp.concatenate([bt(sp["s5_b_re"][l]), bt(sp["s5_b_im"][l])], axis=1)
    f["cbd"] = _c_expand(sp["s5_c_re"][l], sp["s5_c_im"][l])
    f["drow"] = sp["s5_d"][l].reshape(1, S5_W)
    f["wg"] = _glu_expand(sp["s5_w_glu"][l])
    f["bg"] = sp["s5_b_glu"][l].reshape(1, S5_W)
    for n in ("ln1_g", "ln1_b", "ln2_g", "ln2_b"):
        f[n] = sp[n][l][None, :]
    f["cb"] = sp["conv_b"][l][None, :]
    return f


def layer_fwd_mix(x, mod, w_in, f):
    p = mm_mod(x, w_in, NP, "in_proj", mod, (0, 1))
    ygm = gm_fwd(p, f["gm_lg"], f["gm_lb"], f["gm_ws"], f["gm_bst"])
    arow, bbt = s5_prep_fwd(f["lr"], f["li"], f["ldt"], f["bt"])
    bbd = _blockdiag_expand(bbt)
    ys5, st = s5_fwd(p, arow, bbd, f["cbd"], f["drow"], f["wg"], f["bg"])
    qa, ka, qat, kat, vt = fox_prep(p, f["bf"])
    yfx, lse = attn(qat, ka, vt)
    mixcat = jnp.concatenate([ygm, ys5, yfx], axis=1)
    return mixcat, dict(f=f, x=x, p=p, arow=arow, bbd=bbd, st=st, qa=qa, ka=ka, qat=qat, kat=kat, yfx=yfx, lse=lse,
                        mixcat=mixcat)


def layer_fwd_rest(x, mixcat, mod, w, f, saved):
    mix, x1 = proj_post(mixcat, w["w_out"], x, mod, 2, f["ln1_g"], f["ln1_b"], "out_proj")
    up = mm_mod(x1, w["w_up"], DFF, "up_proj", mod, (3, 4))
    act, conv = conv_fwd(up, w["conv_w"], f["cb"])
    ffn, x2 = proj_post(act, w["w_down"], x1, mod, 5, f["ln2_g"], f["ln2_b"], "down_proj")
    return x2, dict(saved, mix=mix, x1=x1, up=up, conv=conv, act=act, ffn=ffn)


def layer_fwd(x, mod, w, f):
    mixcat, saved = layer_fwd_mix(x, mod, w["w_in"], f)
    return layer_fwd_rest(x, mixcat, mod, w, f, saved)


def layer_bwd_ffn(dx, sv, mod, w):
    f = sv["f"]
    dx1, dffn, dg2, dlg2, dlb2 = post_bwd(sv["x1"], sv["ffn"], mod, 5, f["ln2_g"], f["ln2_b"], dx, "post2_bwd")
    g_down = mm_tn(sv["act"], dffn, D, "down_dw")
    dact = mm_nt(dffn, w["w_down"], "down_dx")
    dup, dcw, dcb = conv_bwd(sv["up"], sv["conv"], w["conv_w"], dact)
    g_up = mm_tn(dup, sv["x1"], D // 2, "up_dw", mod=mod, rows=(3, 4))
    dx1, dsh2, dsc2 = mm_nt_mod(dup, w["w_up"], sv["x1"], dx1, mod, (3, 4), "up_dx")
    return dx1, dict(w_up=g_up, w_down=g_down, conv_w=dcw), dict(dsh2=dsh2, dsc2=dsc2, dg2=dg2, conv_b=dcb[0],
                                                                 ln2_g=dlg2[0], ln2_b=dlb2[0])


def layer_bwd_mix(dx1, sv, mod, w, part):
    f = sv["f"]
    dx0, dmix, dg1, dlg1, dlb1 = post_bwd(sv["x"], sv["mix"], mod, 2, f["ln1_g"], f["ln1_b"], dx1, "post1_bwd")
    g_out = mm_tn(sv["mixcat"], dmix, D, "out_dw")
    dmc = mm_nt(dmix, w["w_out"], "out_dx")
    duv, dgm_lg, dgm_lb, dgm_ws, dgm_bst = gm_bwd(sv["p"], f["gm_lg"], f["gm_lb"], f["gm_ws"], f["gm_bst"], dmc)
    du5, da, dbbd, dcbd, dd5, dwg, dbg = s5_bwd(sv["p"], sv["st"], sv["arow"], sv["bbd"], f["cbd"], f["drow"],
                                                f["wg"], f["bg"], dmc)
    dlr, dli, dldt, dbt = s5_prep_bwd(f["lr"], f["li"], f["ldt"], f["bt"], da, _blockdiag_extract(dbbd))
    dq, dk, dv, dfq, dfk = attn_grad(sv["qa"], sv["qat"], sv["ka"], sv["kat"], sv["p"], sv["yfx"], sv["lse"], dmc)
    dff, dbf = fox_prep_grad(sv["p"], f["bf"], dfq, dfk)
    dp = jnp.concatenate([duv, du5, dq, dk, dv, dff], axis=1)
    g_in = mm_tn(dp, sv["x"], D, "in_dw", mod=mod, rows=(0, 1))
    dx, dsh1, dsc1 = mm_nt_mod(dp, w["w_in"], sv["x"], dx0, mod, (0, 1), "in_dx")

    dmod = jnp.concatenate([dsh1, dsc1, dg1, part["dsh2"], part["dsc2"], part["dg2"]], axis=0)
    dc_re, dc_im = _c_extract(dcbd)
    dbt4 = dbt.reshape(S5_H, 2, S5_G, S5_P)
    vals = dict(b_f=dbf[0, :FX_H], gm_ln_g=dgm_lg.reshape(GM_H, HD), gm_ln_b=dgm_lb.reshape(GM_H, HD),
                gm_w_s=dgm_ws, gm_b_s=dgm_bst.T, s5_lam_re=dlr.reshape(S5_G, S5_P),
                s5_lam_im=dli.reshape(S5_G, S5_P), s5_log_dt=dldt[0, :S5_G],
                s5_b_re=jnp.transpose(dbt4[:, 0], (1, 2, 0)), s5_b_im=jnp.transpose(dbt4[:, 1], (1, 2, 0)),
                s5_c_re=dc_re, s5_c_im=dc_im, s5_d=dd5.reshape(S5_G, S5_H), s5_w_glu=_glu_extract(dwg),
                s5_b_glu=dbg.reshape(S5_G, S5_H), ln1_g=dlg1[0], ln1_b=dlb1[0], conv_b=part["conv_b"],
                ln2_g=part["ln2_g"], ln2_b=part["ln2_b"])
    return dx, dict(w_in=g_in, w_out=g_out), vals, dmod


def layer_bwd(dx, sv, mod, w):
    dx1, g_ffn, part = layer_bwd_ffn(dx, sv, mod, w)
    dx, g_mix, vals, dmod = layer_bwd_mix(dx1, sv, mod, w, part)
    return dx, dict(g_ffn, **g_mix), vals, dmod


def local_step(x, target, mods, big, sp):
    saved = []
    for l in range(DEPTH):
        x, sv = layer_fwd(x, mods[l], big[l], _layer_operands(sp, l))
        saved.append(sv)
    loss_tile, dx = loss_kernel(x, target)
    gbig, vals, dmods = [None] * DEPTH, [None] * DEPTH, [None] * DEPTH
    for l in reversed(range(DEPTH)):
        dx, gbig[l], vals[l], dmods[l] = layer_bwd(dx, saved[l], mods[l], big[l])
    gsm = {n: jnp.stack([v[n] for v in vals]) for n in SMALL}
    return loss_tile, dx, gbig, gsm, jnp.stack(dmods)


def _my_index():
    return 4 * lax.axis_index("x") + 2 * lax.axis_index("y") + lax.axis_index("c")


def exchange(tensors, scatter, name):
    n = len(tensors)

    def body(*refs):
        ins, outs = refs[:n], refs[n:2 * n]
        send_sems, recv_sems, local_sems = refs[2 * n:]
        x, y, c = lax.axis_index("x"), lax.axis_index("y"), lax.axis_index("c")
        me = 4 * x + 2 * y + c
        local = []
        for t in range(n):
            cp = pltpu.make_async_copy(ins[t].at[me] if scatter else ins[t], outs[t].at[me], local_sems.at[t])
            cp.start()
            local.append(cp)
        remote = []
        for m in range(1, NDEV):
            px = 1 - x if m & 4 else x
            py = 1 - y if m & 2 else y
            pc = 1 - c if m & 1 else c
            peer = 4 * px + 2 * py + pc
            for t in range(n):
                k = t * (NDEV - 1) + m - 1
                cp = pltpu.make_async_remote_copy(
                    src_ref=ins[t].at[peer] if scatter else ins[t], dst_ref=outs[t].at[me],
                    send_sem=send_sems.at[k], recv_sem=recv_sems.at[k],
                    device_id=(px, py, pc), device_id_type=MESH_IDS)
                cp.start()
                remote.append(cp)
        for cp in remote:
            cp.wait()
        for cp in local:
            cp.wait()

    hbm = pl.BlockSpec(memory_space=pltpu.HBM)
    out_shape = [jax.ShapeDtypeStruct(t.shape if scatter else (NDEV,) + t.shape, t.dtype) for t in tensors]
    return _call(body, name=name, in_specs=[hbm] * n, out_specs=[hbm] * n, out_shape=out_shape,
                 scratch_shapes=[pltpu.SemaphoreType.DMA((n * (NDEV - 1),)), pltpu.SemaphoreType.DMA((n * (NDEV - 1),)),
                                 pltpu.SemaphoreType.DMA((n,))])(*tensors)


def _peers():
    x, y, c = lax.axis_index("x"), lax.axis_index("y"), lax.axis_index("c")
    out = []
    for m in range(1, NDEV):
        px = 1 - x if m & 4 else x
        py = 1 - y if m & 2 else y
        pc = 1 - c if m & 1 else c
        out.append(((px, py, pc), 4 * px + 2 * py + pc))
    return 4 * x + 2 * y + c, out


def _split_copies(v_refs, land_refs, send_sems, recv_sems, scatter):
    me, peers = _peers()
    return [pltpu.make_async_remote_copy(
        src_ref=v_ref.at[idx] if scatter else v_ref, dst_ref=land_ref.at[me],
        send_sem=send_sems.at[t * (NDEV - 1) + k], recv_sem=recv_sems.at[t * (NDEV - 1) + k],
        device_id=pos, device_id_type=MESH_IDS)
        for t, (v_ref, land_ref) in enumerate(zip(v_refs, land_refs)) for k, (pos, idx) in enumerate(peers)]


_HBM_SPEC = pl.BlockSpec(memory_space=pltpu.HBM)
_SEM_SPEC = pl.BlockSpec(memory_space=pltpu.SEMAPHORE)
_SPLIT_EFFECT = pltpu.SideEffectType.DATAFLOW_SIDE_EFFECTING


def exchange_start(tensors, scatter, name):
    n = len(tensors)
    land_shapes = [t.shape if scatter else (NDEV,) + t.shape for t in tensors]

    def body(*refs):
        v_refs, land_refs = refs[:n], refs[n:2 * n]
        send_sems, recv_sems = refs[2 * n], refs[2 * n + 1]
        token = refs[-1]
        for cp in _split_copies(v_refs, land_refs, send_sems, recv_sems, scatter):
            cp.start()
        token[...] = jnp.zeros_like(token)

    sems = pltpu.SemaphoreType.DMA((n * (NDEV - 1),))
    out = _call(
        body, name=name,
        out_shape=(sems, sems, *[pltpu.HBM(t.shape, t.dtype) for t in tensors],
                   *[pltpu.HBM(s, t.dtype) for s, t in zip(land_shapes, tensors)], jax.ShapeDtypeStruct((8, 128), F32)),
        in_specs=(_HBM_SPEC,) * (2 * n),
        out_specs=(_SEM_SPEC, _SEM_SPEC) + (_HBM_SPEC,) * (2 * n) + (pl.BlockSpec(memory_space=pltpu.VMEM),),
        input_output_aliases={i: i + 2 for i in range(2 * n)},
        compiler_params=pltpu.CompilerParams(has_side_effects=_SPLIT_EFFECT),
    )(*[pltpu.with_memory_space_constraint(t, pltpu.HBM) for t in tensors],
      *[pltpu.with_memory_space_constraint(lax.empty(s, t.dtype), pltpu.HBM) for s, t in zip(land_shapes, tensors)])
    return out[0], out[1], list(out[2:2 + n]), list(out[2 + n:2 + 2 * n]), out[-1]


def exchange_wait(started, after, scatter, name):
    send_sems, recv_sems, v_thru, land_thru, _ = started
    n = len(v_thru)

    def body(*refs):
        v_refs, land_refs = refs[:n], refs[n:2 * n]
        for cp in _split_copies(v_refs, land_refs, refs[2 * n], refs[2 * n + 1], scatter):
            cp.wait_send()
            cp.wait_recv()

    out = _call(
        body, name=name,
        out_shape=tuple(pltpu.HBM(t.shape, t.dtype) for t in v_thru + land_thru),
        in_specs=(_HBM_SPEC,) * (2 * n) + (_SEM_SPEC, _SEM_SPEC, pl.BlockSpec(memory_space=pl.ANY)),
        out_specs=(_HBM_SPEC,) * (2 * n), input_output_aliases={i: i for i in range(2 * n)},
        compiler_params=pltpu.CompilerParams(has_side_effects=_SPLIT_EFFECT),
    )(*v_thru, *land_thru, send_sems, recv_sems, after)
    return list(out[:n]), list(out[n:])


def mod_slices(c_all, w_ada, b_loc):
    nl, _, nc = w_ada.shape

    def body(c_ref, w_ref, b_ref, o_ref):
        cv = c_ref[...]
        o_ref[0] = _nn(cv * jax.nn.sigmoid(cv), w_ref[0]) + b_ref[0]

    return _call(body, name="mod_slices", grid=(nl,),
                 in_specs=[pl.BlockSpec((NDEV, D), lambda l: (0, 0)), pl.BlockSpec((1, D, nc), lambda l: (l, 0, 0)),
                           pl.BlockSpec((1, 1, nc), lambda l: (l, 0, 0))],
                 out_specs=pl.BlockSpec((1, NDEV, nc), lambda l: (l, 0, 0)),
                 out_shape=jax.ShapeDtypeStruct((nl, NDEV, nc), F32),
                 compiler_params=_params("arbitrary"))(c_all, w_ada, b_loc.reshape(nl, 1, nc))


def ada_grad(c_all, dm_loc):
    nl, _, nc = dm_loc.shape

    def body(c_ref, d_ref, o_ref):
        cv = c_ref[...]
        o_ref[0] = _tn(cv * jax.nn.sigmoid(cv), d_ref[0])

    return _call(body, name="ada_grad", grid=(nl,),
                 in_specs=[pl.BlockSpec((NDEV, D), lambda l: (0, 0)), pl.BlockSpec((1, NDEV, nc), lambda l: (l, 0, 0))],
                 out_specs=pl.BlockSpec((1, D, nc), lambda l: (l, 0, 0)),
                 out_shape=jax.ShapeDtypeStruct((nl, D, nc), F32),
                 compiler_params=_params("arbitrary"))(c_all, dm_loc)


def sum_chunks(chunks):
    r = chunks.shape[1]

    def body(c_ref, o_ref):
        acc = c_ref[0]
        for i in range(1, NDEV):
            acc = acc + c_ref[i]
        o_ref[...] = acc

    return _call(body, name="sum_chunks", out_shape=jax.ShapeDtypeStruct((r, 128), F32))(chunks)


def _row_tile(r):
    if r <= 256:
        return r
    for t in range(256, 7, -8):
        if r % t == 0:
            return t
    return r


def adamw(w, m, v, g=None, chunks=None, name="adamw"):
    r, cdim = w.shape
    tr = _row_tile(r)
    bc1 = 1.0 - ADAM_B1 ** ADAM_STEP
    bc2 = 1.0 - ADAM_B2 ** ADAM_STEP

    def body(g_ref, w_ref, m_ref, v_ref, go_ref, d_ref, mo_ref, vo_ref):
        if chunks is None:
            grad = g_ref[...]
        else:
            grad = g_ref[0].astype(F32)
            for i in range(1, NDEV):
                grad = grad + g_ref[i].astype(F32)
        mn = ADAM_B1 * m_ref[...] + (1.0 - ADAM_B1) * grad
        vn = ADAM_B2 * v_ref[...] + (1.0 - ADAM_B2) * (grad * grad)
        m_hat = mn / bc1
        v_hat = vn / bc2
        go_ref[...] = grad
        d_ref[...] = -ADAM_LR * (m_hat / (jnp.sqrt(v_hat) + ADAM_EPS) + ADAM_WD * w_ref[...])
        mo_ref[...] = mn
        vo_ref[...] = vn

    tile = pl.BlockSpec((tr, cdim), lambda i: (i, 0))
    gspec = tile if chunks is None else pl.BlockSpec((NDEV, tr, cdim), lambda i: (0, i, 0))
    shp = jax.ShapeDtypeStruct((r, cdim), F32)
    return _call(body, name=name, grid=(r // tr,), in_specs=[gspec, tile, tile, tile],
                 out_specs=[tile] * 4, out_shape=[shp] * 4,
                 compiler_params=_params("arbitrary"))(g if chunks is None else chunks, w, m, v)


def adamw_layers(w, m, v, chunks, name):
    nl, r, cdim = w.shape
    tr = _row_tile(r)
    bc1 = 1.0 - ADAM_B1 ** ADAM_STEP
    bc2 = 1.0 - ADAM_B2 ** ADAM_STEP
    outs = [lax.empty(w.shape, F32) for _ in range(4)]
    for l in range(nl):
        def body(g_ref, w_ref, m_ref, v_ref, p0, p1, p2, p3, go_ref, d_ref, mo_ref, vo_ref):
            grad = g_ref[0].astype(F32)
            for i in range(1, NDEV):
                grad = grad + g_ref[i].astype(F32)
            mn = ADAM_B1 * m_ref[...] + (1.0 - ADAM_B1) * grad
            vn = ADAM_B2 * v_ref[...] + (1.0 - ADAM_B2) * (grad * grad)
            go_ref[...] = grad
            d_ref[...] = -ADAM_LR * ((mn / bc1) / (jnp.sqrt(vn / bc2) + ADAM_EPS) + ADAM_WD * w_ref[...])
            mo_ref[...] = mn
            vo_ref[...] = vn

        tile = pl.BlockSpec((None, tr, cdim), lambda i, l=l: (l, i, 0))
        whole = pl.BlockSpec(memory_space=pl.ANY)
        outs = _call(body, name=f"{name}_{l}", grid=(r // tr,),
                     in_specs=[pl.BlockSpec((NDEV, tr, cdim), lambda i: (0, i, 0)), tile, tile, tile] + [whole] * 4,
                     out_specs=[tile] * 4, out_shape=[jax.ShapeDtypeStruct(w.shape, F32)] * 4,
                     input_output_aliases={4: 0, 5: 1, 6: 2, 7: 3},
                     compiler_params=_params("arbitrary"))(chunks[l], w, m, v, *outs)
    return outs


WEIGHTS = ("w_ada", "b_ada", "w_in", "b_f", "gm_ln_g", "gm_ln_b", "gm_w_s", "gm_b_s", "s5_lam_re", "s5_lam_im",
           "s5_log_dt", "s5_b_re", "s5_b_im", "s5_c_re", "s5_c_im", "s5_d", "s5_w_glu", "s5_b_glu", "w_out", "ln1_g",
           "ln1_b", "w_up", "conv_w", "conv_b", "w_down", "ln2_g", "ln2_b")
LARGE = ("w_in", "w_out", "w_up", "w_down")
TRANSPOSED = ("w_in", "w_up")
PACKED = ("b_ada",) + SMALL
PACK_SEG = 8 * 128


def _gather_cols(g):
    nd, nl, r, c = g.shape
    return jnp.transpose(g, (1, 2, 0, 3)).reshape(nl, r, nd * c)


def _chunk_cols(g):
    nl, r, c8 = g.shape
    return jnp.transpose(g.reshape(nl, r, NDEV, c8 // NDEV), (2, 0, 1, 3))


def _join_rows(g):
    nd, r, c = g.shape
    return g.reshape(nd * r, c)


def _split_rows(g):
    r8, c = g.shape
    return g.reshape(NDEV, r8 // NDEV, c)


def _pack(parts):
    segs = []
    for n in PACKED:
        flat = parts[n].reshape(-1)
        segs.append(jnp.pad(flat, (0, -flat.shape[0] % PACK_SEG)).reshape(-1, 128))
    rows = jnp.concatenate(segs, axis=0)
    return jnp.pad(rows, ((0, -rows.shape[0] % (NDEV * 8)), (0, 0)))


def _unpack(rows, shapes):
    out, off = {}, 0
    for n in PACKED:
        size = math.prod(shapes[n])
        nrows = -(-size // PACK_SEG) * 8
        out[n] = rows[off:off + nrows].reshape(-1)[:size].reshape(shapes[n])
        off += nrows
    return out


def kernel(x, c, w_ada, b_ada, w_in, b_f, gm_ln_g, gm_ln_b, gm_w_s, gm_b_s, s5_lam_re, s5_lam_im, s5_log_dt, s5_b_re, s5_b_im, s5_c_re, s5_c_im, s5_d, s5_w_glu, s5_b_glu, w_out, ln1_g, ln1_b, w_up, conv_w, conv_b, w_down, ln2_g, ln2_b, loss_target, m_w_ada, m_b_ada, m_w_in, m_b_f, m_gm_ln_g, m_gm_ln_b, m_gm_w_s, m_gm_b_s, m_s5_lam_re, m_s5_lam_im, m_s5_log_dt, m_s5_b_re, m_s5_b_im, m_s5_c_re, m_s5_c_im, m_s5_d, m_s5_w_glu, m_s5_b_glu, m_w_out, m_ln1_g, m_ln1_b, m_w_up, m_conv_w, m_conv_b, m_w_down, m_ln2_g, m_ln2_b, v_w_ada, v_b_ada, v_w_in, v_b_f, v_gm_ln_g, v_gm_ln_b, v_gm_w_s, v_gm_b_s, v_s5_lam_re, v_s5_lam_im, v_s5_log_dt, v_s5_b_re, v_s5_b_im, v_s5_c_re, v_s5_c_im, v_s5_d, v_s5_w_glu, v_s5_b_glu, v_w_out, v_ln1_g, v_ln1_b, v_w_up, v_conv_w, v_conv_b, v_w_down, v_ln2_g, v_ln2_b):
    given = dict(locals())
    wts = {n: given[n] for n in WEIGHTS}
    mom = {n: given["m_" + n] for n in WEIGHTS}
    var = {n: given["v_" + n] for n in WEIGHTS}
    nl = w_ada.shape[0]
    me = _my_index()
    ada_cols = w_ada.shape[2]

    (c_all,) = exchange([c], False, "gather_c")
    c_all = c_all.reshape(NDEV, D)
    b_loc = lax.dynamic_slice_in_dim(b_ada, me * ada_cols, ada_cols, axis=1)
    mod_part = mod_slices(c_all, w_ada, b_loc)

    mod_all, conv_all = exchange([mod_part, conv_w], False, "gather_mod")
    mod_mine = lax.dynamic_index_in_dim(mod_all, me, axis=2, keepdims=False)
    mods = jnp.transpose(mod_mine, (1, 0, 2)).reshape(nl, 6, D)
    mods = jnp.pad(mods, ((0, 0), (0, 2), (0, 0)))
    conv_full = _gather_cols(conv_all)
    sp = {n: wts[n] for n in SMALL}
    rowwise = {n: [jnp.swapaxes(a[n], 1, 2) if n in TRANSPOSED else a[n] for a in (wts, mom, var)] for n in LARGE}

    def joined(own, land):
        return _join_rows(lax.dynamic_update_index_in_dim(land, own, me, 0))

    def block(l, names):
        return [rowwise[n][0][l].astype(BF16) for n in names]

    def chunked(grads, names):
        return [_split_rows(grads[n]) for n in names]

    head, tail = LARGE[:1], LARGE[1:]
    got_head = exchange_start(block(0, head), False, "gather_start_0_in")
    rest = block(0, tail)
    rest[0] = rest[0] + got_head[4][0, 0].astype(BF16)
    got_tail = exchange_start(rest, False, "gather_start_0_rest")
    xl, saved, weights = x[0], [], []
    for l in range(nl):
        if l == 0:
            own, land = exchange_wait(got_head, got_tail[4], False, "gather_wait_0_in")
            w = {n: joined(o, g) for n, o, g in zip(head, own, land)}
        else:
            own, land = exchange_wait(started, xl, False, f"gather_wait_{l}")
            w = {n: joined(o, g) for n, o, g in zip(LARGE, own, land)}
        mod_l = mods[l]
        if l + 1 < nl:
            nxt, w["w_in"] = lax.optimization_barrier((block(l + 1, LARGE), w["w_in"]))
            started = exchange_start(nxt, False, f"gather_start_{l + 1}")
            mod_l = mod_l + started[4][0, 0]
        w["w_in"] = jnp.pad(w["w_in"], ((0, NP - D_IN), (0, 0)))
        f = _layer_operands(sp, l)
        mixcat, sv = layer_fwd_mix(xl, mod_l, w["w_in"], f)
        if l == 0:
            own, land = exchange_wait(got_tail, mixcat, False, "gather_wait_0_rest")
            w.update({n: joined(o, g) for n, o, g in zip(tail, own, land)})
        w["conv_w"] = conv_full[l]
        xl, sv = layer_fwd_rest(xl, mixcat, mod_l, w, f, sv)
        weights.append(w)
        saved.append(sv)

    loss_tile, dx = loss_kernel(xl, loss_target[0])

    ffn_names, mix_names = ("w_up", "w_down"), ("w_in", "w_out")
    scattering, vals, dmods, gconv = [None] * nl, [None] * nl, [None] * nl, [None] * nl
    token = jnp.zeros((), F32)
    for l in reversed(range(nl)):
        mod_l = mods[l] + token
        dx1, g_ffn, part = layer_bwd_ffn(dx, saved[l], mod_l, weights[l])
        gconv[l] = g_ffn["conv_w"]
        if l == 0:
            sent_ffn = exchange_start(chunked(g_ffn, ffn_names), True, "scatter_start_0_ffn")
            mod_l = mod_l + sent_ffn[4][0, 0]
        dx, g_mix, vals[l], dmods[l] = layer_bwd_mix(dx1, saved[l], mod_l, weights[l], part)
        g_mix["w_in"] = g_mix["w_in"][:D_IN]
        if l == 0:
            scattering[l] = [(ffn_names, sent_ffn),
                             (mix_names, exchange_start(chunked(g_mix, mix_names), True, "scatter_start_0_mix"))]
        else:
            sent = exchange_start(chunked(dict(g_ffn, **g_mix), LARGE), True, f"scatter_start_{l}")
            scattering[l] = [(LARGE, sent)]
            token = sent[4][0, 0]
    gx = dx
    dmods = jnp.stack(dmods)
    gsm = {n: jnp.stack([v[n] for v in vals]) for n in SMALL}

    gsm["b_ada"] = dmods.reshape(nl, 6 * D)
    packed = _pack(gsm).reshape(NDEV, -1, 128)
    conv_recv, small_recv = exchange([_chunk_cols(jnp.stack(gconv)), packed], True, "scatter_small")
    small_sum = sum_chunks(small_recv)
    small_all, dmod_all = exchange([small_sum, dmods.reshape(nl, 6 * D)], False, "gather_small")

    received = [dict() for _ in range(nl)]

    def arrive(l, k, after):
        names, sent = scattering[l][k]
        own, land = exchange_wait(sent, after, True, f"scatter_wait_{l}_{k}")
        for n, o, g in zip(names, own, land):
            mine = lax.dynamic_index_in_dim(o, me, 0, keepdims=False)
            received[l][n] = lax.dynamic_update_index_in_dim(g, mine, me, 0)
        return land[0]

    after = small_all
    for l in reversed(range(nl)):
        after = arrive(l, 0, after)
    out = {}
    def update(n):
        res = adamw_layers(*rowwise[n], [received[l][n] for l in range(nl)], "adamw_" + n)
        return [jnp.swapaxes(r, 1, 2) for r in res] if n in TRANSPOSED else res

    for n in ffn_names:
        out[n] = update(n)
    arrive(0, 1, out[ffn_names[-1]][0])
    for n in mix_names:
        out[n] = update(n)
    shp = conv_w.shape
    two_d = lambda a: a.reshape(shp[0] * shp[1], shp[2])
    res = adamw(two_d(conv_w), two_d(m_conv_w), two_d(v_conv_w),
                chunks=conv_recv.reshape(NDEV, shp[0] * shp[1], shp[2]), name="adamw_conv_w")
    out["conv_w"] = [r.reshape(shp) for r in res]

    dm_loc = lax.dynamic_slice_in_dim(dmod_all, me * ada_cols, ada_cols, axis=2)
    g_ada = ada_grad(c_all, jnp.transpose(dm_loc, (1, 0, 2)))
    two_d = lambda a: a.reshape(nl * D, ada_cols)
    res = adamw(two_d(w_ada), two_d(m_w_ada), two_d(v_w_ada), g=two_d(g_ada), name="adamw_w_ada")
    out["w_ada"] = [r.reshape(w_ada.shape) for r in res]

    shapes = {n: wts[n].shape for n in PACKED}
    res = adamw(_pack(wts), _pack(mom), _pack(var), g=small_all.reshape(-1, 128), name="adamw_small")
    unpacked = [_unpack(r, shapes) for r in res]
    for n in PACKED:
        out[n] = [u[n] for u in unpacked]

    loss = lax.psum(loss_tile[0, 0], ("x", "y", "c"))
    return (loss, gx[None], *[out[n][0] for n in WEIGHTS], *[out[n][1] for n in WEIGHTS],
            *[out[n][2] for n in WEIGHTS], *[out[n][3] for n in WEIGHTS])
```
